```python
import math
import jax, jax.numpy as jnp
from jax import lax
import numpy as np

D_MODEL = 1024
BATCH = 16
SEQ = 4096
DEPTH = 1

SSM_EXPAND = 2
SSM_D_INNER = SSM_EXPAND * D_MODEL
SSM_HEAD_DIM = 64
SSM_N_HEADS = SSM_D_INNER // SSM_HEAD_DIM
SSM_N_GROUPS = 4
SSM_D_STATE = 128
SSM_CONV = 4
SSM_CHUNK = 128
SSM_CONV_DIM = SSM_D_INNER + 2 * SSM_N_GROUPS * SSM_D_STATE

ATT_HEAD_DIM = 128
ATT_HEADS_PER_GROUP = 4
ATT_PATTERNS = ((128, 1), (512, 4), (2048, 16))
ATT_N_HEADS = ATT_HEADS_PER_GROUP * len(ATT_PATTERNS)
ATT_QKV_DIM = 3 * ATT_N_HEADS * ATT_HEAD_DIM
ATT_OUT_DIM = ATT_HEADS_PER_GROUP * ATT_HEAD_DIM
ATT_BLOCK = 128
ROPE_THETA = 10000.0

N_BRANCH = 2
D_FF = -(-8 * D_MODEL // (3 * 256)) * 256
IN_PROJ_SIZES = (SSM_D_INNER, SSM_CONV_DIM, SSM_N_HEADS, ATT_QKV_DIM, N_BRANCH * D_MODEL)
IN_PROJ_DIM = sum(IN_PROJ_SIZES)
EPS = 1e-6

kernel_name = "hybrid_ssd_dilated_swa_block"


def rmsnorm(x, g):
    xf = x.astype(jnp.float32)
    y = xf * lax.rsqrt(jnp.mean(xf * xf, axis=-1, keepdims=True) + EPS)
    return (y * g.astype(jnp.float32)).astype(x.dtype)


def rope(t, pos):
    half = t.shape[-1] // 2
    inv = ROPE_THETA ** (-jnp.arange(half, dtype=jnp.float32) / half)
    ang = pos.astype(jnp.float32)[:, None] * inv[None, :]
    cos = jnp.cos(ang)[None, :, None, :]
    sin = jnp.sin(ang)[None, :, None, :]
    t1, t2 = t[..., :half], t[..., half:]
    return jnp.concatenate([t1 * cos - t2 * sin, t2 * cos + t1 * sin], axis=-1)


def segsum(a):
    T = a.shape[-1]
    xx = jnp.broadcast_to(a[..., :, None], a.shape + (T,))
    cs = jnp.cumsum(jnp.where(jnp.tril(jnp.ones((T, T), bool), -1), xx, 0.0), axis=-2)
    return jnp.where(jnp.tril(jnp.ones((T, T), bool)), cs, -jnp.inf)


def causal_depthwise_conv(u, w, b):
    K, C = w.shape
    out = lax.conv_general_dilated(u, w[:, None, :], window_strides=(1,), padding=[(K - 1, 0)],
                                   dimension_numbers=('NWC', 'WIO', 'NWC'), feature_group_count=C)
    return out + b


def ssd_chunked(xs, dt, A, Bm, Cm):
    b, S, H, P = xs.shape
    G, N = Bm.shape[-2:]
    J = H // G
    Q = SSM_CHUNK
    nc = S // Q
    xdt = (xs * dt[..., None]).reshape(b, nc, Q, G, J, P)
    a = (dt * A).reshape(b, nc, Q, G, J).transpose(0, 1, 3, 4, 2)
    a_cs = jnp.cumsum(a, axis=-1)
    Br = Bm.reshape(b, nc, Q, G, N)
    Cr = Cm.reshape(b, nc, Q, G, N)
    tri = jnp.tril(jnp.ones((Q, Q), bool))
    Lmat = jnp.exp(jnp.where(tri, a_cs[..., :, None] - a_cs[..., None, :], -jnp.inf))
    CB = jnp.einsum('bclgn,bcsgn->bcgls', Cr, Br)
    y_diag = jnp.einsum('bcgjls,bcsgjp->bclgjp', CB[:, :, :, None] * Lmat, xdt)
    decay_states = jnp.exp(a_cs[..., -1:] - a_cs)
    states = jnp.einsum('bclgn,bcgjl,bclgjp->bcgjpn', Br, decay_states, xdt)
    chunk_tot = jnp.pad(a_cs[..., -1].transpose(0, 2, 3, 1), ((0, 0), (0, 0), (0, 0), (1, 0)))
    decay_chunk = jnp.exp(segsum(chunk_tot))
    states = jnp.concatenate([jnp.zeros_like(states[:, :1]), states], axis=1)
    new_states = jnp.einsum('bgjzc,bcgjpn->bzgjpn', decay_chunk, states)
    prev_states = new_states[:, :-1]
    y_off = jnp.einsum('bclgn,bcgjpn,bcgjl->bclgjp', Cr, prev_states, jnp.exp(a_cs))
    return (y_diag + y_off).reshape(b, S, H, P)


def mamba2_branch(z, xBC, dt_raw, conv_w, conv_b, dt_bias, a_log, d_skip, ssm_norm):
    b, S, _ = z.shape
    xBC = jax.nn.silu(causal_depthwise_conv(xBC, conv_w, conv_b)).astype(jnp.float32)
    gn = SSM_N_GROUPS * SSM_D_STATE
    xs = xBC[..., :SSM_D_INNER].reshape(b, S, SSM_N_HEADS, SSM_HEAD_DIM)
    Bm = xBC[..., SSM_D_INNER:SSM_D_INNER + gn].reshape(b, S, SSM_N_GROUPS, SSM_D_STATE)
    Cm = xBC[..., SSM_D_INNER + gn:].reshape(b, S, SSM_N_GROUPS, SSM_D_STATE)
    dt = jax.nn.softplus(dt_raw.astype(jnp.float32) + dt_bias.astype(jnp.float32))
    A = -jnp.exp(a_log.astype(jnp.float32))
    y = ssd_chunked(xs, dt, A, Bm, Cm) + d_skip.astype(jnp.float32)[:, None] * xs
    y = y.reshape(b, S, SSM_D_INNER) * jax.nn.silu(z.astype(jnp.float32))
    yg = y.reshape(b, S, SSM_N_GROUPS, SSM_D_INNER // SSM_N_GROUPS)
    yg = yg * lax.rsqrt(jnp.mean(yg * yg, axis=-1, keepdims=True) + EPS)
    y = yg.reshape(b, S, SSM_D_INNER) * ssm_norm.astype(jnp.float32)
    return y.astype(z.dtype)


def dilated_window_group(q, k, v, window, dilation):
    b, S, h, d = q.shape
    r = dilation
    w_sub = window // r
    L = S // r
    nb = -(-L // ATT_BLOCK)
    Lp = nb * ATT_BLOCK

    def to_blocks(t):
        t = t.reshape(b, L, r, h, d).transpose(0, 2, 1, 3, 4)
        t = jnp.pad(t, ((0, 0), (0, 0), (0, Lp - L), (0, 0), (0, 0)))
        return t.reshape(b, r, nb, ATT_BLOCK, h, d)

    def with_prev(t):
        prev = jnp.pad(t[:, :, :-1], ((0, 0), (0, 0), (1, 0), (0, 0), (0, 0), (0, 0)))
        return jnp.concatenate([prev, t], axis=3)

    qb = to_blocks(q)
    kk = with_prev(to_blocks(k))
    vv = with_prev(to_blocks(v))
    s = jnp.einsum('brnqhd,brnkhd->brnhqk', qb, kk) * (d ** -0.5)
    qi = jnp.arange(ATT_BLOCK)[:, None]
    kj = jnp.arange(2 * ATT_BLOCK)[None, :]
    dist = qi + ATT_BLOCK - kj
    band = (dist >= 0) & (dist <= w_sub)
    has_prev = (jnp.arange(nb) > 0)[:, None, None] | (kj >= ATT_BLOCK)[None]
    mask = band[None] & has_prev
    s = jnp.where(mask[None, None, :, None], s, -jnp.inf)
    m = jnp.max(s, axis=-1, keepdims=True)
    p = jnp.exp(s - m)
    den = jnp.sum(p, axis=-1, keepdims=True)
    o = jnp.einsum('brnhqk,brnkhd->brnqhd', p / den, vv)
    lse = (m + jnp.log(den))[..., 0].transpose(0, 1, 2, 4, 3)
    o = o.reshape(b, r, Lp, h, d)[:, :, :L].transpose(0, 2, 1, 3, 4).reshape(b, S, h, d)
    lse = lse.reshape(b, r, Lp, h)[:, :, :L].transpose(0, 2, 1, 3).reshape(b, S, h)
    return o, lse


def dilated_attention_branch(q, k, v):
    outs, lses = [], []
    for gi, (window, dilation) in enumerate(ATT_PATTERNS):
        sl = slice(gi * ATT_HEADS_PER_GROUP, (gi + 1) * ATT_HEADS_PER_GROUP)
        o, lse = dilated_window_group(q[:, :, sl], k[:, :, sl], v[:, :, sl], window, dilation)
        outs.append(o)
        lses.append(lse)
    o = jnp.stack(outs, axis=0)
    wts = jax.nn.softmax(jnp.stack(lses, axis=0), axis=0)
    return jnp.sum(wts[..., None] * o, axis=0)


def _fwd_setup_inputs(seed: int = 0) -> dict:
    key = jax.random.key(seed)
    ks = jax.random.split(key, 20)
    f32 = jnp.float32

    def nrm(k, shape, scale):
        return jax.random.normal(k, shape, f32) * scale

    dt0 = jnp.exp(jax.random.uniform(ks[6], (DEPTH, SSM_N_HEADS), f32,
                                     minval=math.log(1e-3), maxval=math.log(1e-1)))
    return {
        "x": nrm(ks[0], (BATCH, SEQ, D_MODEL), 1.0),
        "norm_mix": 1.0 + nrm(ks[1], (DEPTH, D_MODEL), 0.05),
        "w_in": nrm(ks[2], (DEPTH, D_MODEL, IN_PROJ_DIM), D_MODEL ** -0.5),
        "b_gate": nrm(ks[3], (DEPTH, N_BRANCH * D_MODEL), 0.01),
        "conv_w": nrm(ks[4], (DEPTH, SSM_CONV, SSM_CONV_DIM), SSM_CONV ** -0.5),
        "conv_b": nrm(ks[5], (DEPTH, SSM_CONV_DIM), 0.01),
        "dt_bias": dt0 + jnp.log(-jnp.expm1(-dt0)),
        "a_log": jnp.log(jax.random.uniform(ks[7], (DEPTH, SSM_N_HEADS), f32, minval=1.0, maxval=16.0)),
        "d_skip": 1.0 + nrm(ks[8], (DEPTH, SSM_N_HEADS), 0.1),
        "ssm_norm": 1.0 + nrm(ks[9], (DEPTH, SSM_D_INNER), 0.05),
        "w_ssm_out": nrm(ks[10], (DEPTH, SSM_D_INNER, D_MODEL), SSM_D_INNER ** -0.5),
        "w_att_out": nrm(ks[11], (DEPTH, ATT_OUT_DIM, D_MODEL), ATT_OUT_DIM ** -0.5),
        "w_mix_out": nrm(ks[12], (DEPTH, D_MODEL, D_MODEL), D_MODEL ** -0.5),
        "norm_ffn": 1.0 + nrm(ks[13], (DEPTH, D_MODEL), 0.05),
        "w_ffn_gate": nrm(ks[14], (DEPTH, D_MODEL, D_FF), D_MODEL ** -0.5),
        "w_ffn_up": nrm(ks[15], (DEPTH, D_MODEL, D_FF), D_MODEL ** -0.5),
        "w_ffn_down": nrm(ks[16], (DEPTH, D_FF, D_MODEL), D_FF ** -0.5),
        "norm_final": 1.0 + nrm(ks[17], (D_MODEL,), 0.05),
    }


def _fwd_reference(x, norm_mix, w_in, b_gate, conv_w, conv_b, dt_bias, a_log, d_skip, ssm_norm,
              w_ssm_out, w_att_out, w_mix_out, norm_ffn, w_ffn_gate, w_ffn_up, w_ffn_down, norm_final):
    b, S, _ = x.shape
    pos = jnp.arange(S)
    offs = [0]
    for sz in IN_PROJ_SIZES[:-1]:
        offs.append(offs[-1] + sz)
    for l in range(DEPTH):
        h = rmsnorm(x, norm_mix[l])
        proj = h @ w_in[l]
        z, xBC, dt_raw, qkv, gate_logits = jnp.split(proj, offs[1:], axis=-1)

        y_ssm = mamba2_branch(z, xBC, dt_raw, conv_w[l], conv_b[l], dt_bias[l], a_log[l],
                              d_skip[l], ssm_norm[l]) @ w_ssm_out[l]

        qkv = qkv.astype(jnp.float32).reshape(b, S, 3, ATT_N_HEADS, ATT_HEAD_DIM)
        q = rope(qkv[:, :, 0], pos)
        k = rope(qkv[:, :, 1], pos)
        v = qkv[:, :, 2]
        y_att = dilated_attention_branch(q, k, v).reshape(b, S, ATT_OUT_DIM).astype(x.dtype) @ w_att_out[l]

        gates = jax.nn.sigmoid((gate_logits + b_gate[l]).astype(jnp.float32))
        gates = gates.reshape(b, S, N_BRANCH, D_MODEL).astype(x.dtype)
        mixed = gates[:, :, 0] * y_ssm + gates[:, :, 1] * y_att
        x = x + mixed @ w_mix_out[l]

        h = rmsnorm(x, norm_ffn[l])
        x = x + (jax.nn.silu(h @ w_ffn_gate[l]) * (h @ w_ffn_up[l])) @ w_ffn_down[l]
    return rmsnorm(x, norm_final)


import jax as _jax
import jax.numpy as _jnp

TWIN_FORMAT = 'train_step'
FWD_PARAMS = ['x', 'norm_mix', 'w_in', 'b_gate', 'conv_w', 'conv_b', 'dt_bias', 'a_log', 'd_skip', 'ssm_norm', 'w_ssm_out', 'w_att_out', 'w_mix_out', 'norm_ffn', 'w_ffn_gate', 'w_ffn_up', 'w_ffn_down', 'norm_final']
TWIN_WEIGHTS = ['norm_mix', 'w_in', 'b_gate', 'conv_w', 'conv_b', 'dt_bias', 'a_log', 'd_skip', 'ssm_norm', 'w_ssm_out', 'w_att_out', 'w_mix_out', 'norm_ffn', 'w_ffn_gate', 'w_ffn_up', 'w_ffn_down', 'norm_final']
TWIN_DIFF_INPUT = 'x'
TWIN_INPUTS = ['x', 'norm_mix', 'w_in', 'b_gate', 'conv_w', 'conv_b', 'dt_bias', 'a_log', 'd_skip', 'ssm_norm', 'w_ssm_out', 'w_att_out', 'w_mix_out', 'norm_ffn', 'w_ffn_gate', 'w_ffn_up', 'w_ffn_down', 'norm_final', 'loss_target', 'm_norm_mix', 'm_w_in', 'm_b_gate', 'm_conv_w', 'm_conv_b', 'm_dt_bias', 'm_a_log', 'm_d_skip', 'm_ssm_norm', 'm_w_ssm_out', 'm_w_att_out', 'm_w_mix_out', 'm_norm_ffn', 'm_w_ffn_gate', 'm_w_ffn_up', 'm_w_ffn_down', 'm_norm_final', 'v_norm_mix', 'v_w_in', 'v_b_gate', 'v_conv_w', 'v_conv_b', 'v_dt_bias', 'v_a_log', 'v_d_skip', 'v_ssm_norm', 'v_w_ssm_out', 'v_w_att_out', 'v_w_mix_out', 'v_norm_ffn', 'v_w_ffn_gate', 'v_w_ffn_up', 'v_w_ffn_down', 'v_norm_final']
TWIN_OUTPUTS = ['loss', 'grad_x', 'grad_norm_mix', 'grad_w_in', 'grad_b_gate', 'grad_conv_w', 'grad_conv_b', 'grad_dt_bias', 'grad_a_log', 'grad_d_skip', 'grad_ssm_norm', 'grad_w_ssm_out', 'grad_w_att_out', 'grad_w_mix_out', 'grad_norm_ffn', 'grad_w_ffn_gate', 'grad_w_ffn_up', 'grad_w_ffn_down', 'grad_norm_final', 'delta_norm_mix', 'delta_w_in', 'delta_b_gate', 'delta_conv_w', 'delta_conv_b', 'delta_dt_bias', 'delta_a_log', 'delta_d_skip', 'delta_ssm_norm', 'delta_w_ssm_out', 'delta_w_att_out', 'delta_w_mix_out', 'delta_norm_ffn', 'delta_w_ffn_gate', 'delta_w_ffn_up', 'delta_w_ffn_down', 'delta_norm_final', 'new_m_norm_mix', 'new_m_w_in', 'new_m_b_gate', 'new_m_conv_w', 'new_m_conv_b', 'new_m_dt_bias', 'new_m_a_log', 'new_m_d_skip', 'new_m_ssm_norm', 'new_m_w_ssm_out', 'new_m_w_att_out', 'new_m_w_mix_out', 'new_m_norm_ffn', 'new_m_w_ffn_gate', 'new_m_w_ffn_up', 'new_m_w_ffn_down', 'new_m_norm_final', 'new_v_norm_mix', 'new_v_w_in', 'new_v_b_gate', 'new_v_conv_w', 'new_v_conv_b', 'new_v_dt_bias', 'new_v_a_log', 'new_v_d_skip', 'new_v_ssm_norm', 'new_v_w_ssm_out', 'new_v_w_att_out', 'new_v_w_mix_out', 'new_v_norm_ffn', 'new_v_w_ffn_gate', 'new_v_w_ffn_up', 'new_v_w_ffn_down', 'new_v_norm_final']
TWIN_LEAF_KINDS = {'loss': 'loss', 'grad_x': 'grad_x', 'grad_norm_mix': 'grad_w', 'grad_w_in': 'grad_w', 'grad_b_gate': 'grad_w', 'grad_conv_w': 'grad_w', 'grad_conv_b': 'grad_w', 'grad_dt_bias': 'grad_w', 'grad_a_log': 'grad_w', 'grad_d_skip': 'grad_w', 'grad_ssm_norm': 'grad_w', 'grad_w_ssm_out': 'grad_w', 'grad_w_att_out': 'grad_w', 'grad_w_mix_out': 'grad_w', 'grad_norm_ffn': 'grad_w', 'grad_w_ffn_gate': 'grad_w', 'grad_w_ffn_up': 'grad_w', 'grad_w_ffn_down': 'grad_w', 'grad_norm_final': 'grad_w', 'delta_norm_mix': 'delta_w', 'delta_w_in': 'delta_w', 'delta_b_gate': 'delta_w', 'delta_conv_w': 'delta_w', 'delta_conv_b': 'delta_w', 'delta_dt_bias': 'delta_w', 'delta_a_log': 'delta_w', 'delta_d_skip': 'delta_w', 'delta_ssm_norm': 'delta_w', 'delta_w_ssm_out': 'delta_w', 'delta_w_att_out': 'delta_w', 'delta_w_mix_out': 'delta_w', 'delta_norm_ffn': 'delta_w', 'delta_w_ffn_gate': 'delta_w', 'delta_w_ffn_up': 'delta_w', 'delta_w_ffn_down': 'delta_w', 'delta_norm_final': 'delta_w', 'new_m_norm_mix': 'new_m', 'new_m_w_in': 'new_m', 'new_m_b_gate': 'new_m', 'new_m_conv_w': 'new_m', 'new_m_conv_b': 'new_m', 'new_m_dt_bias': 'new_m', 'new_m_a_log': 'new_m', 'new_m_d_skip': 'new_m', 'new_m_ssm_norm': 'new_m', 'new_m_w_ssm_out': 'new_m', 'new_m_w_att_out': 'new_m', 'new_m_w_mix_out': 'new_m', 'new_m_norm_ffn': 'new_m', 'new_m_w_ffn_gate': 'new_m', 'new_m_w_ffn_up': 'new_m', 'new_m_w_ffn_down': 'new_m', 'new_m_norm_final': 'new_m', 'new_v_norm_mix': 'new_v', 'new_v_w_in': 'new_v', 'new_v_b_gate': 'new_v', 'new_v_conv_w': 'new_v', 'new_v_conv_b': 'new_v', 'new_v_dt_bias': 'new_v', 'new_v_a_log': 'new_v', 'new_v_d_skip': 'new_v', 'new_v_ssm_norm': 'new_v', 'new_v_w_ssm_out': 'new_v', 'new_v_w_att_out': 'new_v', 'new_v_w_mix_out': 'new_v', 'new_v_norm_ffn': 'new_v', 'new_v_w_ffn_gate': 'new_v', 'new_v_w_ffn_up': 'new_v', 'new_v_w_ffn_down': 'new_v', 'new_v_norm_final': 'new_v'}


def _forward(args):
    return _fwd_reference(*[args[k] for k in FWD_PARAMS])


def _output_shape():
    out = _jax.eval_shape(lambda: _forward(_fwd_setup_inputs(0)))
    return out.shape, out.dtype

N_MICROBATCH = 1
ADAM_LR = 0.001
ADAM_B1 = 0.9
ADAM_B2 = 0.999
ADAM_EPS = 1e-08
ADAM_WD = 0.01
ADAM_STEP = 10
PER_EXAMPLE_BATCH_AXIS = {'x': 0, 'loss_target': 0}
SHARED_INPUTS = []
_WEIGHT_DTYPES = {'norm_mix': _jnp.float32, 'w_in': _jnp.float32, 'b_gate': _jnp.float32, 'conv_w': _jnp.float32, 'conv_b': _jnp.float32, 'dt_bias': _jnp.float32, 'a_log': _jnp.float32, 'd_skip': _jnp.float32, 'ssm_norm': _jnp.float32, 'w_ssm_out': _jnp.float32, 'w_att_out': _jnp.float32, 'w_mix_out': _jnp.float32, 'norm_ffn': _jnp.float32, 'w_ffn_gate': _jnp.float32, 'w_ffn_up': _jnp.float32, 'w_ffn_down': _jnp.float32, 'norm_final': _jnp.float32}
MOMENT_SCALE = {'norm_mix': 2.074209e-01, 'w_in': 6.211306e-02, 'b_gate': 3.785716e-02, 'conv_w': 8.037365e-02, 'conv_b': 1.143570e-01, 'dt_bias': 4.147982e-01, 'a_log': 4.286097e-01, 'd_skip': 4.019025e-01, 'ssm_norm': 9.588224e-02, 'w_ssm_out': 1.370216e-01, 'w_att_out': 2.488059e-02, 'w_mix_out': 1.356207e-01, 'norm_ffn': 1.677647e-01, 'w_ffn_gate': 7.346289e-02, 'w_ffn_up': 7.155388e-02, 'w_ffn_down': 1.193232e-01, 'norm_final': 6.416957e+01}


def _to_microbatches(a, axis):
    t = _jnp.moveaxis(a, axis, 0)
    t = t.reshape((N_MICROBATCH, t.shape[0] // N_MICROBATCH) + t.shape[1:])
    return _jnp.moveaxis(t, 1, axis + 1)


def setup_inputs(seed: int = 0) -> dict:
    inp = _fwd_setup_inputs(seed)
    key = _jax.random.fold_in(_jax.random.key(seed), 7919)
    shape, _ = _output_shape()
    out = dict(inp)
    out["loss_target"] = _jax.random.normal(_jax.random.fold_in(key, 0), shape, _jnp.float32)
    for i, name in enumerate(TWIN_WEIGHTS):
        w = inp[name].astype(_jnp.float32)
        if MOMENT_SCALE is None:
            s = _jnp.sqrt(_jnp.mean(_jnp.square(w)) + 1e-30)
        else:
            s = MOMENT_SCALE[name]
        km, kv = _jax.random.split(_jax.random.fold_in(key, i + 1))
        out[name] = w
        out["m_" + name] = s * _jax.random.normal(km, w.shape, _jnp.float32)
        out["v_" + name] = (s * s) * _jax.random.uniform(kv, w.shape, _jnp.float32, 0.5, 1.5)
    if N_MICROBATCH > 1:
        for name, axis in PER_EXAMPLE_BATCH_AXIS.items():
            out[name] = _to_microbatches(out[name], axis)
    return {'x': out['x'], 'norm_mix': out['norm_mix'], 'w_in': out['w_in'], 'b_gate': out['b_gate'], 'conv_w': out['conv_w'], 'conv_b': out['conv_b'], 'dt_bias': out['dt_bias'], 'a_log': out['a_log'], 'd_skip': out['d_skip'], 'ssm_norm': out['ssm_norm'], 'w_ssm_out': out['w_ssm_out'], 'w_att_out': out['w_att_out'], 'w_mix_out': out['w_mix_out'], 'norm_ffn': out['norm_ffn'], 'w_ffn_gate': out['w_ffn_gate'], 'w_ffn_up': out['w_ffn_up'], 'w_ffn_down': out['w_ffn_down'], 'norm_final': out['norm_final'], 'loss_target': out['loss_target'], 'm_norm_mix': out['m_norm_mix'], 'm_w_in': out['m_w_in'], 'm_b_gate': out['m_b_gate'], 'm_conv_w': out['m_conv_w'], 'm_conv_b': out['m_conv_b'], 'm_dt_bias': out['m_dt_bias'], 'm_a_log': out['m_a_log'], 'm_d_skip': out['m_d_skip'], 'm_ssm_norm': out['m_ssm_norm'], 'm_w_ssm_out': out['m_w_ssm_out'], 'm_w_att_out': out['m_w_att_out'], 'm_w_mix_out': out['m_w_mix_out'], 'm_norm_ffn': out['m_norm_ffn'], 'm_w_ffn_gate': out['m_w_ffn_gate'], 'm_w_ffn_up': out['m_w_ffn_up'], 'm_w_ffn_down': out['m_w_ffn_down'], 'm_norm_final': out['m_norm_final'], 'v_norm_mix': out['v_norm_mix'], 'v_w_in': out['v_w_in'], 'v_b_gate': out['v_b_gate'], 'v_conv_w': out['v_conv_w'], 'v_conv_b': out['v_conv_b'], 'v_dt_bias': out['v_dt_bias'], 'v_a_log': out['v_a_log'], 'v_d_skip': out['v_d_skip'], 'v_ssm_norm': out['v_ssm_norm'], 'v_w_ssm_out': out['v_w_ssm_out'], 'v_w_att_out': out['v_w_att_out'], 'v_w_mix_out': out['v_w_mix_out'], 'v_norm_ffn': out['v_norm_ffn'], 'v_w_ffn_gate': out['v_w_ffn_gate'], 'v_w_ffn_up': out['v_w_ffn_up'], 'v_w_ffn_down': out['v_w_ffn_down'], 'v_norm_final': out['v_norm_final']}


def _loss(weights, diff, rest, loss_target):
    with _jax.named_scope("forward"):
        args = {**rest, TWIN_DIFF_INPUT: diff, **{k: w.astype(_WEIGHT_DTYPES[k]) for k, w in weights.items()}}
        y = _forward(args)
    with _jax.named_scope("loss_head"):
        err = _jnp.square(y.astype(_jnp.float32) - loss_target)
        return 0.5 * _jnp.sum(_jnp.mean(err, axis=-1)) if err.ndim else 0.5 * err


def _adamw(w, g, m, v):
    m = ADAM_B1 * m + (1.0 - ADAM_B1) * g
    v = ADAM_B2 * v + (1.0 - ADAM_B2) * _jnp.square(g)
    m_hat = m / (1.0 - ADAM_B1 ** ADAM_STEP)
    v_hat = v / (1.0 - ADAM_B2 ** ADAM_STEP)
    delta = -ADAM_LR * (m_hat / (_jnp.sqrt(v_hat) + ADAM_EPS) + ADAM_WD * w)
    return delta, m, v


def reference(x, norm_mix, w_in, b_gate, conv_w, conv_b, dt_bias, a_log, d_skip, ssm_norm, w_ssm_out, w_att_out, w_mix_out, norm_ffn, w_ffn_gate, w_ffn_up, w_ffn_down, norm_final, loss_target, m_norm_mix, m_w_in, m_b_gate, m_conv_w, m_conv_b, m_dt_bias, m_a_log, m_d_skip, m_ssm_norm, m_w_ssm_out, m_w_att_out, m_w_mix_out, m_norm_ffn, m_w_ffn_gate, m_w_ffn_up, m_w_ffn_down, m_norm_final, v_norm_mix, v_w_in, v_b_gate, v_conv_w, v_conv_b, v_dt_bias, v_a_log, v_d_skip, v_ssm_norm, v_w_ssm_out, v_w_att_out, v_w_mix_out, v_norm_ffn, v_w_ffn_gate, v_w_ffn_up, v_w_ffn_down, v_norm_final):
    given = dict(x=x, norm_mix=norm_mix, w_in=w_in, b_gate=b_gate, conv_w=conv_w, conv_b=conv_b, dt_bias=dt_bias, a_log=a_log, d_skip=d_skip, ssm_norm=ssm_norm, w_ssm_out=w_ssm_out, w_att_out=w_att_out, w_mix_out=w_mix_out, norm_ffn=norm_ffn, w_ffn_gate=w_ffn_gate, w_ffn_up=w_ffn_up, w_ffn_down=w_ffn_down, norm_final=norm_final, loss_target=loss_target, m_norm_mix=m_norm_mix, m_w_in=m_w_in, m_b_gate=m_b_gate, m_conv_w=m_conv_w, m_conv_b=m_conv_b, m_dt_bias=m_dt_bias, m_a_log=m_a_log, m_d_skip=m_d_skip, m_ssm_norm=m_ssm_norm, m_w_ssm_out=m_w_ssm_out, m_w_att_out=m_w_att_out, m_w_mix_out=m_w_mix_out, m_norm_ffn=m_norm_ffn, m_w_ffn_gate=m_w_ffn_gate, m_w_ffn_up=m_w_ffn_up, m_w_ffn_down=m_w_ffn_down, m_norm_final=m_norm_final, v_norm_mix=v_norm_mix, v_w_in=v_w_in, v_b_gate=v_b_gate, v_conv_w=v_conv_w, v_conv_b=v_conv_b, v_dt_bias=v_dt_bias, v_a_log=v_a_log, v_d_skip=v_d_skip, v_ssm_norm=v_ssm_norm, v_w_ssm_out=v_w_ssm_out, v_w_att_out=v_w_att_out, v_w_mix_out=v_w_mix_out, v_norm_ffn=v_norm_ffn, v_w_ffn_gate=v_w_ffn_gate, v_w_ffn_up=v_w_ffn_up, v_w_ffn_down=v_w_ffn_down, v_norm_final=v_norm_final)
    weights = {n: given[n] for n in TWIN_WEIGHTS}
    shared = {n: given[n] for n in SHARED_INPUTS}
    per_example = {n: given[n] for n in ['x']}
    grad_fn = _jax.value_and_grad(_loss, argnums=(0, 1))

    def one_microbatch(ex, loss_target):
        ex = dict(ex)
        diff = ex.pop(TWIN_DIFF_INPUT)
        return grad_fn(weights, diff, {**shared, **ex}, loss_target)

    if N_MICROBATCH == 1:
        loss, (grad_w, grad_x) = one_microbatch(per_example, given["loss_target"])
    else:
        def body(carry, xs):
            loss_sum, grad_sum = carry
            l_k, (gw_k, gx_k) = one_microbatch(xs[0], xs[1])
            with _jax.named_scope("update"):
                return (loss_sum + l_k, _jax.tree.map(_jnp.add, grad_sum, gw_k)), gx_k

        init = (_jnp.zeros((), _jnp.float32), _jax.tree.map(_jnp.zeros_like, weights))
        (loss, grad_w), grad_x = _jax.lax.scan(body, init, (per_example, given["loss_target"]))
    with _jax.named_scope("update"):
        delta_w, new_m, new_v = {}, {}, {}
        for n in TWIN_WEIGHTS:
            delta_w[n], new_m[n], new_v[n] = _adamw(weights[n], grad_w[n], given["m_" + n], given["v_" + n])
    return (loss, grad_x, *[grad_w[n] for n in TWIN_WEIGHTS], *[delta_w[n] for n in TWIN_WEIGHTS],
            *[new_m[n] for n in TWIN_WEIGHTS], *[new_v[n] for n in TWIN_WEIGHTS])
```

```python
import functools
import math

import jax
import jax.numpy as jnp
from jax import lax
from jax.experimental import pallas as pl
from jax.experimental.pallas import tpu as pltpu

F32 = jnp.float32
BF16 = jnp.bfloat16
SDS = jax.ShapeDtypeStruct
MESH = pl.DeviceIdType.MESH

D_MODEL = 1024
D_INNER = 2048
N_HEADS = 32
HEAD_P = 64
N_GROUPS = 4
HEADS_PER_GROUP = N_HEADS // N_GROUPS
D_STATE = 128
CONV_K = 4
CHUNK = 128
CONV_DIM = D_INNER + 2 * N_GROUPS * D_STATE
GROUP_W = D_INNER // N_GROUPS + 2 * D_STATE
ATT_HEADS = 12
ATT_D = 128
ATT_SLOTS = 4
ATT_W = ATT_SLOTS * ATT_D
ATT_DILATIONS = (1, 4, 16)
ATT_BLOCK = 128
QKV_DIM = 3 * ATT_HEADS * ATT_D
D_FF = 2816
DT_PAD = 128
ROPE_THETA = 10000.0
EPS = 1e-6
N_CHIPS = 4
LANES = 128

ADAM_LR = 0.001
ADAM_B1 = 0.9
ADAM_B2 = 0.999
ADAM_EPS = 1e-08
ADAM_WD = 0.01
ADAM_STEP = 10

VMEM_LIMIT = 48 * 1024 * 1024


def _cparams(semantics):
    return pltpu.CompilerParams(dimension_semantics=semantics, vmem_limit_bytes=VMEM_LIMIT)


def _pick(n, cap):
    best = None
    for t in range(LANES, min(n, cap) + 1, LANES):
        if n % t == 0:
            best = t
    return best or n


def _row_tile(rows, cap):
    best = None
    for t in range(8, min(rows, cap) + 1, 8):
        if rows % t == 0:
            best = t
    return best or rows


def _sigmoid(x):
    return 1.0 / (1.0 + jnp.exp(-x))


def _softplus(x):
    return jnp.maximum(x, 0.0) + jnp.log(1.0 + jnp.exp(-jnp.abs(x)))


def _dot(a, b):
    return jnp.dot(a, b, preferred_element_type=F32)


def _dot_nt(a, b):
    return lax.dot_general(a, b, (((1,), (1,)), ((), ())), preferred_element_type=F32)


def _dot_tn(a, b):
    return lax.dot_general(a, b, (((0,), (0,)), ((), ())), preferred_element_type=F32)


def _mm(a, b, mode, out_dtype, name, add=None):
    if mode == "nn":
        (m, k), (_, n) = a.shape, b.shape
    elif mode == "nt":
        (m, k), (n, _) = a.shape, b.shape
    else:
        (k, m), (_, n) = a.shape, b.shape
    tm, tn, tk = _pick(m, 512), _pick(n, 1536), _pick(k, 512)
    nk = k // tk
    dims = {"nn": ((1,), (0,)), "nt": ((1,), (1,)), "tn": ((0,), (0,))}[mode]

    def body(*refs):
        if add is None:
            a_ref, b_ref, o_ref, acc = refs
        else:
            a_ref, b_ref, c_ref, o_ref, acc = refs
        kk = pl.program_id(2)

        @pl.when(kk == 0)
        def _():
            acc[...] = jnp.zeros_like(acc)

        acc[...] += lax.dot_general(a_ref[...].astype(BF16), b_ref[...].astype(BF16), (dims, ((), ())),
                                    preferred_element_type=F32)

        @pl.when(kk == nk - 1)
        def _():
            r = acc[...]
            if add is not None:
                r = r + c_ref[...].astype(F32)
            o_ref[...] = r.astype(out_dtype)

    a_spec = {"nn": pl.BlockSpec((tm, tk), lambda i, j, q: (i, q)),
              "nt": pl.BlockSpec((tm, tk), lambda i, j, q: (i, q)),
              "tn": pl.BlockSpec((tk, tm), lambda i, j, q: (q, i))}[mode]
    b_spec = {"nn": pl.BlockSpec((tk, tn), lambda i, j, q: (q, j)),
              "nt": pl.BlockSpec((tn, tk), lambda i, j, q: (j, q)),
              "tn": pl.BlockSpec((tk, tn), lambda i, j, q: (q, j))}[mode]
    o_spec = pl.BlockSpec((tm, tn), lambda i, j, q: (i, j))
    ins, specs = [a, b], [a_spec, b_spec]
    if add is not None:
        ins.append(add)
        specs.append(o_spec)
    return pl.pallas_call(
        body, name=name, grid=(m // tm, n // tn, nk), in_specs=specs, out_specs=o_spec,
        out_shape=SDS((m, n), out_dtype), scratch_shapes=[pltpu.VMEM((tm, tn), F32)],
        compiler_params=_cparams(("parallel", "parallel", "arbitrary")))(*ins)


def _rw(name, fn, nsteps, ins, outs, n_acc=0):
    n_in, n_out = len(ins), len(outs)

    def body(*refs):
        i = pl.program_id(0)
        vals = fn(i, *refs[:n_in])
        for q, (r, v) in enumerate(zip(refs[n_in:], vals)):
            if q < n_out - n_acc:
                r[...] = v.astype(r.dtype)
            else:
                @pl.when(i == 0)
                def _(r=r):
                    r[...] = jnp.zeros_like(r)

                r[...] += v

    return pl.pallas_call(
        body, name=name, grid=(nsteps,), in_specs=[s for _, s in ins], out_specs=[s for _, s in outs],
        out_shape=[o for o, _ in outs], compiler_params=_cparams(("arbitrary",)))(*[a for a, _ in ins])


def _rs(tm, w, cb=0):
    return pl.BlockSpec((tm, w), lambda i: (i, cb))


def _fs(shape):
    nd = len(shape)
    return pl.BlockSpec(shape, lambda i: (0,) * nd)


def _gs(g, tm, w):
    return pl.BlockSpec((None, tm, w), lambda i: (g, i, 0))


def _colsum(v):
    return jnp.sum(v, axis=0, keepdims=True)


def _rms_fwd(x, g, name):
    t, d = x.shape
    tm = 512

    def fn(i, x_ref, g_ref):
        xv = x_ref[...]
        r = lax.rsqrt(jnp.mean(xv * xv, axis=-1, keepdims=True) + EPS)
        return [xv * r * g_ref[...]]

    return _rw(name, fn, t // tm, [(x, _rs(tm, d)), (g, _fs((1, d)))], [(SDS((t, d), BF16), _rs(tm, d))])[0]


def _rms_bwd(x, dh, g, dres, name):
    t, d = x.shape
    tm = 512

    def fn(i, x_ref, dh_ref, g_ref, dres_ref):
        xv = x_ref[...]
        r = lax.rsqrt(jnp.mean(xv * xv, axis=-1, keepdims=True) + EPS)
        xhat = xv * r
        dhv = dh_ref[...]
        dxhat = dhv * g_ref[...]
        dx = r * (dxhat - xhat * jnp.mean(dxhat * xhat, axis=-1, keepdims=True))
        return [dres_ref[...] + dx, _colsum(dhv * xhat)]

    return _rw(name, fn, t // tm,
               [(x, _rs(tm, d)), (dh, _rs(tm, d)), (g, _fs((1, d))), (dres, _rs(tm, d))],
               [(SDS((t, d), F32), _rs(tm, d)), (SDS((1, d), F32), _fs((1, d)))], n_acc=1)


def _final_fwd_bwd(x2, target, g):
    t, d = x2.shape
    tm = 512

    def fn(i, x_ref, t_ref, g_ref):
        xv = x_ref[...]
        gv = g_ref[...]
        r = lax.rsqrt(jnp.mean(xv * xv, axis=-1, keepdims=True) + EPS)
        xhat = xv * r
        diff = xhat * gv - t_ref[...]
        lsum = 0.5 * jnp.sum(jnp.sum(diff * diff, axis=-1, keepdims=True) * (1.0 / d), axis=0, keepdims=True)
        dy = diff * (1.0 / d)
        dxhat = dy * gv
        dx = r * (dxhat - xhat * jnp.mean(dxhat * xhat, axis=-1, keepdims=True))
        return [dx, _colsum(dy * xhat), lsum]

    return _rw("final_norm_loss", fn, t // tm,
               [(x2, _rs(tm, d)), (target, _rs(tm, d)), (g, _fs((1, d)))],
               [(SDS((t, d), F32), _rs(tm, d)), (SDS((1, d), F32), _fs((1, d))), (SDS((1, 1), F32), _fs((1, 1)))],
               n_acc=2)


CONV_TS = 512
CONV_HALO = 8


def _conv_specs(seq, c):
    ts, tc = CONV_TS, GROUP_W
    hb = ts // CONV_HALO
    u_spec = pl.BlockSpec((ts, tc), lambda j, i: (i, j))
    prev_spec = pl.BlockSpec((CONV_HALO, tc), lambda j, i: (jnp.maximum(i * hb - 1, 0), j))
    w_spec = pl.BlockSpec((CONV_K, tc), lambda j, i: (0, j))
    b_spec = pl.BlockSpec((1, tc), lambda j, i: (0, j))
    return u_spec, prev_spec, w_spec, b_spec


def _conv_pre(i, seq, u_ref, prev_ref, w_ref, b_ref, ext):
    ts = CONV_TS
    first = (i % (seq // ts)) == 0
    ext[0:CONV_HALO, :] = jnp.where(first, 0.0, prev_ref[...])
    ext[CONV_HALO:, :] = u_ref[...]
    acc = jnp.broadcast_to(b_ref[...], u_ref.shape)
    for q in range(CONV_K):
        acc = acc + w_ref[q:q + 1, :] * ext[pl.ds(CONV_HALO - CONV_K + 1 + q, ts), :]
    return acc


def _conv_fwd(u, w, b, seq):
    t, c = u.shape
    ts, tc = CONV_TS, GROUP_W
    u_spec, prev_spec, w_spec, b_spec = _conv_specs(seq, c)

    def body(u_ref, prev_ref, w_ref, b_ref, o_ref, ext):
        pre = _conv_pre(pl.program_id(1), seq, u_ref, prev_ref, w_ref, b_ref, ext)
        o_ref[...] = pre * _sigmoid(pre)

    return pl.pallas_call(
        body, name="conv_fwd", grid=(c // tc, t // ts), in_specs=[u_spec, prev_spec, w_spec, b_spec],
        out_specs=u_spec, out_shape=SDS((t, c), F32), scratch_shapes=[pltpu.VMEM((ts + CONV_HALO, tc), F32)],
        compiler_params=_cparams(("parallel", "arbitrary")))(u, u, w, b)


def _conv_bwd_pre(u, w, b, dxc, seq):
    t, c = u.shape
    ts, tc = CONV_TS, GROUP_W
    u_spec, prev_spec, w_spec, b_spec = _conv_specs(seq, c)

    def body(u_ref, prev_ref, w_ref, b_ref, d_ref, dpre_ref, dw_ref, db_ref, ext):
        i = pl.program_id(1)
        pre = _conv_pre(i, seq, u_ref, prev_ref, w_ref, b_ref, ext)
        sg = _sigmoid(pre)
        dpre = d_ref[...] * sg * (1.0 + pre * (1.0 - sg))
        dpre_ref[...] = dpre

        @pl.when(i == 0)
        def _():
            dw_ref[...] = jnp.zeros_like(dw_ref)
            db_ref[...] = jnp.zeros_like(db_ref)

        db_ref[...] += _colsum(dpre)
        for q in range(CONV_K):
            dw_ref[q:q + 1, :] += _colsum(dpre * ext[pl.ds(CONV_HALO - CONV_K + 1 + q, ts), :])

    return pl.pallas_call(
        body, name="conv_bwd_pre", grid=(c // tc, t // ts),
        in_specs=[u_spec, prev_spec, w_spec, b_spec, u_spec], out_specs=[u_spec, w_spec, b_spec],
        out_shape=[SDS((t, c), F32), SDS((CONV_K, c), F32), SDS((1, c), F32)],
        scratch_shapes=[pltpu.VMEM((ts + CONV_HALO, tc), F32)],
        compiler_params=_cparams(("parallel", "arbitrary")))(u, u, w, b, dxc)


def _conv_bwd_in(dpre, w, seq):
    t, c = dpre.shape
    ts, tc = CONV_TS, GROUP_W
    hb = ts // CONV_HALO
    last = t // CONV_HALO - 1
    d_spec = pl.BlockSpec((ts, tc), lambda j, i: (i, j))
    next_spec = pl.BlockSpec((CONV_HALO, tc), lambda j, i: (jnp.minimum((i + 1) * hb, last), j))
    w_spec = pl.BlockSpec((CONV_K, tc), lambda j, i: (0, j))

    def body(d_ref, next_ref, w_ref, o_ref, ext):
        i = pl.program_id(1)
        nts = seq // ts
        is_last = (i % nts) == nts - 1
        ext[0:ts, :] = d_ref[...]
        ext[ts:, :] = jnp.where(is_last, 0.0, next_ref[...])
        acc = jnp.zeros(d_ref.shape, F32)
        for q in range(CONV_K):
            acc = acc + w_ref[q:q + 1, :] * ext[pl.ds(CONV_K - 1 - q, ts), :]
        o_ref[...] = acc.astype(o_ref.dtype)

    return pl.pallas_call(
        body, name="conv_bwd_in", grid=(c // tc, t // ts), in_specs=[d_spec, next_spec, w_spec],
        out_specs=d_spec, out_shape=SDS((t, c), BF16), scratch_shapes=[pltpu.VMEM((ts + CONV_HALO, tc), F32)],
        compiler_params=_cparams(("parallel", "arbitrary")))(dpre, dpre, w)


def _split3(v):
    hi = v.astype(BF16)
    r1 = v - hi.astype(F32)
    mid = r1.astype(BF16)
    lo = (r1 - mid.astype(F32)).astype(BF16)
    return hi, mid, lo


def _ssd_prelude(dtr_ref, dtrt_ref, bias_ref, biast_ref, a_ref, at_ref):
    dt = _softplus(dtr_ref[...] + bias_ref[...])
    dtt = _softplus(dtrt_ref[...] + biast_ref[...])
    ri = lax.broadcasted_iota(jnp.int32, (CHUNK, CHUNK), 0)
    ci = lax.broadcasted_iota(jnp.int32, (CHUNK, CHUNK), 1)
    lower = ri >= ci
    upper = ri <= ci
    lower_b = jnp.where(lower, 1.0, 0.0).astype(BF16)
    upper_b = jnp.where(upper, 1.0, 0.0).astype(BF16)
    acs = sum(_dot(lower_b, p) for p in _split3(dt * a_ref[...]))
    acst = sum(_dot(p, upper_b) for p in _split3(dtt * at_ref[...]))
    return dt, acs, acst, lower, upper, lower_b, upper_b


def _ssd_specs(seq):
    nc = seq // CHUNK
    hg = HEADS_PER_GROUP
    row = lambda cc: (lambda g, b, c: (b * nc + cc(c), g))
    fwd = lambda c: c
    rev = lambda c: nc - 1 - c

    def specs(cc):
        return dict(
            xc=pl.BlockSpec((CHUNK, GROUP_W), lambda g, b, c: (b * nc + cc(c), g)),
            y=pl.BlockSpec((CHUNK, D_INNER // N_GROUPS), lambda g, b, c: (b * nc + cc(c), g)),
            dtr=pl.BlockSpec((None, CHUNK, hg), lambda g, b, c: (g, b * nc + cc(c), 0)),
            dtrt=pl.BlockSpec((None, None, hg, CHUNK), lambda g, b, c: (g, b, 0, cc(c))),
            prow=pl.BlockSpec((None, 1, hg), lambda g, b, c: (g, 0, 0)),
            pcol=pl.BlockSpec((None, hg, 1), lambda g, b, c: (g, 0, 0)),
            st=pl.BlockSpec((None, None, None, hg, D_STATE, HEAD_P), lambda g, b, c: (g, b, cc(c), 0, 0, 0)),
        )

    return specs(fwd), specs(rev)


def _ssd_fwd(xc, dtr, dtrt, bias, biast, a, at, dskip, nb, seq):
    t = xc.shape[0]
    nc = seq // CHUNK
    hg = HEADS_PER_GROUP
    sp, _ = _ssd_specs(seq)

    def body(xc_ref, dtr_ref, dtrt_ref, bias_ref, biast_ref, a_ref, at_ref, d_ref, y_ref, sin_ref, st):
        @pl.when(pl.program_id(2) == 0)
        def _():
            st[...] = jnp.zeros_like(st)

        sin_ref[...] = st[...]
        dt, acs, acst, lower, _, _, _ = _ssd_prelude(dtr_ref, dtrt_ref, bias_ref, biast_ref, a_ref, at_ref)
        x = xc_ref[...]
        bm = x[:, hg * HEAD_P:hg * HEAD_P + D_STATE]
        cb16 = x[:, hg * HEAD_P + D_STATE:].astype(BF16)
        cb = _dot_nt(cb16, bm.astype(BF16))
        for j in range(hg):
            xh = x[:, HEAD_P * j:HEAD_P * (j + 1)]
            xdt = (xh * dt[:, j:j + 1]).astype(BF16)
            col = acs[:, j:j + 1]
            row = acst[j:j + 1, :]
            last = acs[CHUNK - 1:CHUNK, j:j + 1]
            decay = jnp.exp(jnp.where(lower, col - row, -jnp.inf))
            ydiag = _dot((cb * decay).astype(BF16), xdt)
            sj = st[j]
            yoff = jnp.exp(col) * _dot(cb16, sj.astype(BF16))
            bdec = (bm * jnp.exp(last - col)).astype(BF16)
            st[j] = jnp.exp(last) * sj + _dot_tn(bdec, xdt)
            y_ref[:, HEAD_P * j:HEAD_P * (j + 1)] = ydiag + yoff + d_ref[:, j:j + 1] * xh

    return pl.pallas_call(
        body, name="ssd_fwd", grid=(N_GROUPS, nb, nc),
        in_specs=[sp["xc"], sp["dtr"], sp["dtrt"], sp["prow"], sp["pcol"], sp["prow"], sp["pcol"], sp["prow"]],
        out_specs=[sp["y"], sp["st"]],
        out_shape=[SDS((t, D_INNER), F32), SDS((N_GROUPS, nb, nc, hg, D_STATE, HEAD_P), F32)],
        scratch_shapes=[pltpu.VMEM((hg, D_STATE, HEAD_P), F32)],
        compiler_params=_cparams(("parallel", "parallel", "arbitrary")))(xc, dtr, dtrt, bias, biast, a, at, dskip)


def _ssd_bwd(xc, dtr, dtrt, bias, biast, a, at, dskip, states, dy, nb, seq):
    t = xc.shape[0]
    nc = seq // CHUNK
    hg = HEADS_PER_GROUP
    _, sp = _ssd_specs(seq)

    def body(xc_ref, dtr_ref, dtrt_ref, bias_ref, biast_ref, a_ref, at_ref, d_ref, sin_ref, dy_ref,
             dxc_ref, ddtr_ref, gbias_ref, ga_ref, gd_ref, ds):
        first = (pl.program_id(1) == 0) & (pl.program_id(2) == 0)

        @pl.when(pl.program_id(2) == 0)
        def _():
            ds[...] = jnp.zeros_like(ds)

        @pl.when(first)
        def _():
            gbias_ref[...] = jnp.zeros_like(gbias_ref)
            ga_ref[...] = jnp.zeros_like(ga_ref)
            gd_ref[...] = jnp.zeros_like(gd_ref)

        dt, acs, acst, lower, upper, _, upper_b = _ssd_prelude(dtr_ref, dtrt_ref, bias_ref, biast_ref, a_ref, at_ref)
        x = xc_ref[...]
        dy = dy_ref[...]
        bm = x[:, hg * HEAD_P:hg * HEAD_P + D_STATE]
        cm = x[:, hg * HEAD_P + D_STATE:]
        b16 = bm.astype(BF16)
        c16 = cm.astype(BF16)
        cb = _dot_nt(c16, b16)
        cbt = _dot_nt(b16, c16)
        lane8 = lax.broadcasted_iota(jnp.int32, (CHUNK, hg), 1)
        rowid = lax.broadcasted_iota(jnp.int32, (CHUNK, 1), 0)
        lane1 = lax.broadcasted_iota(jnp.int32, (1, hg), 1)
        dacs8 = jnp.zeros((CHUNK, hg), F32)
        ddtx8 = jnp.zeros((CHUNK, hg), F32)
        gd8 = jnp.zeros((1, hg), F32)
        dbm = jnp.zeros((CHUNK, D_STATE), F32)
        dcm = jnp.zeros((CHUNK, D_STATE), F32)
        for j in range(hg):
            xh = x[:, HEAD_P * j:HEAD_P * (j + 1)]
            dyh = dy[:, HEAD_P * j:HEAD_P * (j + 1)]
            dy16 = dyh.astype(BF16)
            dtc = dt[:, j:j + 1]
            xdtf = xh * dtc
            xdt = xdtf.astype(BF16)
            col = acs[:, j:j + 1]
            row = acst[j:j + 1, :]
            last = acs[CHUNK - 1:CHUNK, j:j + 1]
            decay = jnp.exp(jnp.where(lower, col - row, -jnp.inf))
            decayt = jnp.exp(jnp.where(upper, row - col, -jnp.inf))
            e = jnp.exp(col)
            dec = jnp.exp(last - col)
            tot = jnp.exp(last)
            sj = sin_ref[j]
            s16 = sj.astype(BF16)
            dsj = ds[j]
            ds16 = dsj.astype(BF16)
            dm = _dot_nt(dy16, xdt)
            dmt = _dot_nt(xdt, dy16)
            bds = _dot(b16, ds16)
            cs = _dot(c16, s16)
            dx = _dot((cbt * decayt).astype(BF16), dy16) + dec * bds
            dcm = dcm + _dot((dm * decay).astype(BF16), b16) + e * _dot_nt(dy16, s16)
            dbm = dbm + _dot((dmt * decayt).astype(BF16), c16) + dec * _dot_nt(xdt, ds16)
            ds[j] = tot * dsj + _dot_tn((cm * e).astype(BF16), dy16)
            de = jnp.sum(dyh * cs, axis=-1, keepdims=True)
            ddec = jnp.sum(xdtf * bds, axis=-1, keepdims=True)
            dtot = jnp.sum(jnp.sum(dsj * sj, axis=-1, keepdims=True), axis=0, keepdims=True)
            dacs = (jnp.sum(dm * cb * decay, axis=-1, keepdims=True)
                    - jnp.sum(dmt * cbt * decayt, axis=-1, keepdims=True) + de * e - ddec * dec)
            extra = jnp.sum(ddec * dec, axis=0, keepdims=True) + dtot * tot
            dacs = dacs + jnp.where(rowid == CHUNK - 1, extra, 0.0)
            dacs8 = dacs8 + jnp.where(lane8 == j, dacs, 0.0)
            ddtx8 = ddtx8 + jnp.where(lane8 == j, jnp.sum(dx * xh, axis=-1, keepdims=True), 0.0)
            gd8 = gd8 + jnp.where(lane1 == j, jnp.sum(jnp.sum(dyh * xh, axis=-1, keepdims=True), axis=0,
                                                       keepdims=True), 0.0)
            dxc_ref[:, HEAD_P * j:HEAD_P * (j + 1)] = dx * dtc + d_ref[:, j:j + 1] * dyh
        dxc_ref[:, hg * HEAD_P:hg * HEAD_P + D_STATE] = dbm
        dxc_ref[:, hg * HEAD_P + D_STATE:] = dcm
        da = sum(_dot(upper_b, p) for p in _split3(dacs8))
        av = a_ref[...]
        ddt = da * av + ddtx8
        ddtr = ddt * _sigmoid(dtr_ref[...] + bias_ref[...])
        ddtr_ref[...] = ddtr
        gbias_ref[...] += _colsum(ddtr)
        ga_ref[...] += _colsum(da * dt) * av
        gd_ref[...] += gd8

    return pl.pallas_call(
        body, name="ssd_bwd", grid=(N_GROUPS, nb, nc),
        in_specs=[sp["xc"], sp["dtr"], sp["dtrt"], sp["prow"], sp["pcol"], sp["prow"], sp["pcol"], sp["prow"],
                  sp["st"], sp["y"]],
        out_specs=[sp["xc"], sp["dtr"], sp["prow"], sp["prow"], sp["prow"]],
        out_shape=[SDS((t, N_GROUPS * GROUP_W), F32), SDS((N_GROUPS, t, hg), F32)]
        + [SDS((N_GROUPS, 1, hg), F32)] * 3,
        scratch_shapes=[pltpu.VMEM((hg, D_STATE, HEAD_P), F32)],
        compiler_params=_cparams(("arbitrary", "arbitrary", "arbitrary")))(
            xc, dtr, dtrt, bias, biast, a, at, dskip, states, dy)


def _group_bcast(v, width, fn):
    parts = []
    for q in range(v.shape[-1] // width):
        s = fn(v[:, q * width:(q + 1) * width])
        parts.append(jnp.broadcast_to(s, (v.shape[0], width)))
    return jnp.concatenate(parts, axis=-1)


def _gate_norm_fwd(y, z, g):
    t, d = y.shape
    tm = 256
    gw = d // N_GROUPS

    def fn(i, y_ref, z_ref, g_ref):
        zv = z_ref[...]
        u = y_ref[...] * (zv * _sigmoid(zv))
        r = lax.rsqrt(_group_bcast(u * u, gw, lambda p: jnp.mean(p, axis=-1, keepdims=True)) + EPS)
        return [u * r * g_ref[...]]

    return _rw("gate_norm_fwd", fn, t // tm, [(y, _rs(tm, d)), (z, _rs(tm, d)), (g, _fs((1, d)))],
               [(SDS((t, d), BF16), _rs(tm, d))])[0]


def _gate_norm_bwd(y, z, g, dyn):
    t, d = y.shape
    tm = 256
    gw = d // N_GROUPS

    def fn(i, y_ref, z_ref, g_ref, dyn_ref):
        zv = z_ref[...]
        yv = y_ref[...]
        sg = _sigmoid(zv)
        sz = zv * sg
        u = yv * sz
        r = lax.rsqrt(_group_bcast(u * u, gw, lambda p: jnp.mean(p, axis=-1, keepdims=True)) + EPS)
        uhat = u * r
        dv = dyn_ref[...]
        duhat = dv * g_ref[...]
        du = r * (duhat - uhat * _group_bcast(duhat * uhat, gw, lambda p: jnp.mean(p, axis=-1, keepdims=True)))
        dz = du * yv * sg * (1.0 + zv * (1.0 - sg))
        return [du * sz, dz, _colsum(dv * uhat)]

    return _rw("gate_norm_bwd", fn, t // tm,
               [(y, _rs(tm, d)), (z, _rs(tm, d)), (g, _fs((1, d))), (dyn, _rs(tm, d))],
               [(SDS((t, d), F32), _rs(tm, d)), (SDS((t, d), BF16), _rs(tm, d)), (SDS((1, d), F32), _fs((1, d)))],
               n_acc=1)


def _rope_tables(seq):
    half = ATT_D // 2
    inv = ROPE_THETA ** (-jnp.arange(half, dtype=F32) / half)
    ang = jnp.arange(seq, dtype=F32)[:, None] * inv[None, :]
    cos, sin = jnp.cos(ang), jnp.sin(ang)
    return jnp.concatenate([cos, cos], axis=-1), jnp.concatenate([-sin, sin], axis=-1)


def _rope_fwd(qkv, cos, sin, seq):
    t = qkv.shape[0]
    tm = 256
    w = ATT_HEADS * ATT_D
    tab = pl.BlockSpec((tm, ATT_D), lambda i: (i % (seq // tm), 0))

    def fn(i, q_ref, k_ref, v_ref, cos_ref, sin_ref):
        c, s = cos_ref[...], sin_ref[...]

        def rot(ref):
            parts = []
            for h in range(ATT_HEADS):
                p = ref[:, h * ATT_D:(h + 1) * ATT_D]
                parts.append(p * c + pltpu.roll(p, ATT_D // 2, 1) * s)
            return jnp.concatenate(parts, axis=-1)

        return [rot(q_ref), rot(k_ref), v_ref[...]]

    return _rw("rope_fwd", fn, t // tm,
               [(qkv, _rs(tm, w, 0)), (qkv, _rs(tm, w, 1)), (qkv, _rs(tm, w, 2)), (cos, tab), (sin, tab)],
               [(SDS((t, w), BF16), _rs(tm, w))] * 3)


def _rope_bwd(dq, dk, dv, cos, sin, seq):
    t = dq.shape[0]
    tm = 256
    w = ATT_HEADS * ATT_D
    tab = pl.BlockSpec((tm, ATT_D), lambda i: (i % (seq // tm), 0))

    def fn(i, dq_ref, dk_ref, dv_ref, cos_ref, sin_ref):
        c, s = cos_ref[...], sin_ref[...]

        def rot(ref):
            parts = []
            for h in range(ATT_HEADS):
                p = ref[:, h * ATT_D:(h + 1) * ATT_D]
                parts.append(p * c - pltpu.roll(p, ATT_D // 2, 1) * s)
            return jnp.concatenate(parts, axis=-1)

        return [jnp.concatenate([rot(dq_ref), rot(dk_ref), dv_ref[...]], axis=-1)]

    return _rw("rope_bwd", fn, t // tm,
               [(dq, _rs(tm, w)), (dk, _rs(tm, w)), (dv, _rs(tm, w)), (cos, tab), (sin, tab)],
               [(SDS((t, 3 * w), BF16), _rs(tm, 3 * w))])[0]


def _att_masks():
    ri = lax.broadcasted_iota(jnp.int32, (ATT_BLOCK, ATT_BLOCK), 0)
    ci = lax.broadcasted_iota(jnp.int32, (ATT_BLOCK, ATT_BLOCK), 1)
    return ci <= ri, ci >= ri


def _blocks_per_seq(g, seq):
    return (seq // ATT_BLOCK) >> (2 * g)


def _att_fwd(q, k, v, seq):
    ng, t, w = q.shape
    nblk = t // ATT_BLOCK
    scale = ATT_D ** -0.5
    cur = pl.BlockSpec((None, ATT_BLOCK, w), lambda g, n: (g, n, 0))
    prev = pl.BlockSpec((None, ATT_BLOCK, w), lambda g, n: (g, jnp.maximum(n - 1, 0), 0))

    def body(q_ref, kc_ref, kp_ref, vc_ref, vp_ref, o_ref, lse_ref):
        g, n = pl.program_id(0), pl.program_id(1)
        has_prev = (n % _blocks_per_seq(g, seq)) != 0
        mcur, mprev = _att_masks()
        mprev = mprev & has_prev
        for h in range(ATT_SLOTS):
            sl = slice(h * ATT_D, (h + 1) * ATT_D)
            qh = q_ref[:, sl]
            sc = jnp.where(mcur, _dot_nt(qh, kc_ref[:, sl]) * scale, -jnp.inf)
            sp = jnp.where(mprev, _dot_nt(qh, kp_ref[:, sl]) * scale, -jnp.inf)
            m = jnp.maximum(jnp.max(sc, axis=-1, keepdims=True), jnp.max(sp, axis=-1, keepdims=True))
            pc = jnp.exp(sc - m)
            pp = jnp.exp(sp - m)
            den = jnp.sum(pc, axis=-1, keepdims=True) + jnp.sum(pp, axis=-1, keepdims=True)
            o = _dot(pc.astype(BF16), vc_ref[:, sl]) + _dot(pp.astype(BF16), vp_ref[:, sl])
            o_ref[:, sl] = o / den
            lse_ref[:, sl] = jnp.broadcast_to(m + jnp.log(den), (ATT_BLOCK, ATT_D))

    return pl.pallas_call(
        body, name="att_fwd", grid=(ng, nblk), in_specs=[cur, cur, prev, cur, prev], out_specs=[cur, cur],
        out_shape=[SDS((ng, t, w), F32), SDS((ng, t, w), F32)],
        compiler_params=_cparams(("parallel", "arbitrary")))(q, k, k, v, v)


def _att_bwd(q, k, v, do, lse, dlt, seq):
    ng, t, w = q.shape
    nblk = t // ATT_BLOCK
    scale = ATT_D ** -0.5
    cur = pl.BlockSpec((None, ATT_BLOCK, w), lambda g, n: (g, n, 0))
    nxt = pl.BlockSpec((None, ATT_BLOCK, w), lambda g, n: (g, jnp.minimum(n + 1, nblk - 1), 0))

    def body(qc_ref, qn_ref, k_ref, v_ref, doc_ref, don_ref, lsec_ref, lsen_ref, dltc_ref, dltn_ref,
             dq_ref, dk_ref, dv_ref, carry):
        g, n = pl.program_id(0), pl.program_id(1)
        nbs = _blocks_per_seq(g, seq)

        @pl.when((n % nbs) == 0)
        def _():
            carry[...] = jnp.zeros_like(carry)

        has_next = (((n + 1) % nbs) != 0) & (n + 1 < nblk)
        mcur, mprev = _att_masks()
        mnext = mprev & has_next
        for h in range(ATT_SLOTS):
            sl = slice(h * ATT_D, (h + 1) * ATT_D)
            kh, vh = k_ref[:, sl], v_ref[:, sl]
            qc, doc = qc_ref[:, sl], doc_ref[:, sl]
            p = jnp.where(mcur, jnp.exp(_dot_nt(qc, kh) * scale - lsec_ref[:, sl]), 0.0)
            dsc = (p * (_dot_nt(doc, vh) - dltc_ref[:, sl]) * scale).astype(BF16)
            dq_ref[:, sl] = carry[:, sl] + _dot(dsc, kh)
            qn, don = qn_ref[:, sl], don_ref[:, sl]
            pn = jnp.where(mnext, jnp.exp(_dot_nt(qn, kh) * scale - lsen_ref[:, sl]), 0.0)
            dsn = (pn * (_dot_nt(don, vh) - dltn_ref[:, sl]) * scale).astype(BF16)
            carry[:, sl] = _dot(dsn, kh)
            dk_ref[:, sl] = _dot_tn(dsc, qc) + _dot_tn(dsn, qn)
            dv_ref[:, sl] = _dot_tn(p.astype(BF16), doc) + _dot_tn(pn.astype(BF16), don)

    return pl.pallas_call(
        body, name="att_bwd", grid=(ng, nblk), in_specs=[cur, nxt, cur, cur, cur, nxt, cur, nxt, cur, nxt],
        out_specs=[cur, cur, cur], out_shape=[SDS((ng, t, w), F32)] * 3,
        scratch_shapes=[pltpu.VMEM((ATT_BLOCK, w), F32)],
        compiler_params=_cparams(("parallel", "arbitrary")))(q, q, k, v, do, do, lse, lse, dlt, dlt)


def _merge_weights(lse_refs):
    ls = [r[...] for r in lse_refs]
    m = jnp.maximum(jnp.maximum(ls[0], ls[1]), ls[2])
    es = [jnp.exp(v - m) for v in ls]
    den = es[0] + es[1] + es[2]
    return [e / den for e in es]


def _merge_fwd(o, lse):
    ng, t, w = o.shape
    tm = 512

    def fn(i, o0, o1, o2, l0, l1, l2):
        ws = _merge_weights((l0, l1, l2))
        return [ws[0] * o0[...] + ws[1] * o1[...] + ws[2] * o2[...]]

    ins = [(o, _gs(g, tm, w)) for g in range(ng)] + [(lse, _gs(g, tm, w)) for g in range(ng)]
    return _rw("att_merge_fwd", fn, t // tm, ins, [(SDS((t, w), BF16), _rs(tm, w))])[0]


def _merge_bwd(o, lse, datt):
    ng, t, w = o.shape
    tm = 512

    def fn(i, o0, o1, o2, l0, l1, l2, d_ref):
        ws = _merge_weights((l0, l1, l2))
        ov = [o0[...], o1[...], o2[...]]
        dv = d_ref[...]
        att = ws[0] * ov[0] + ws[1] * ov[1] + ws[2] * ov[2]
        dot = _group_bcast(dv * att, ATT_D, lambda p: jnp.sum(p, axis=-1, keepdims=True))
        return [jnp.stack([wg * dv for wg in ws]), jnp.stack([wg * dot for wg in ws])]

    ins = ([(o, _gs(g, tm, w)) for g in range(ng)] + [(lse, _gs(g, tm, w)) for g in range(ng)]
           + [(datt, _rs(tm, w))])
    full = pl.BlockSpec((ng, tm, w), lambda i: (0, i, 0))
    return _rw("att_merge_bwd", fn, t // tm, ins, [(SDS((ng, t, w), BF16), full), (SDS((ng, t, w), F32), full)])


def _to_groups(a, nb, seq):
    outs = []
    for g, r in enumerate(ATT_DILATIONS):
        p = a[:, g * ATT_W:(g + 1) * ATT_W].reshape(nb, seq // r, r, ATT_W)
        outs.append(p.transpose(0, 2, 1, 3).reshape(nb * seq, ATT_W))
    return jnp.stack(outs)


def _from_groups(a, nb, seq):
    outs = []
    for g, r in enumerate(ATT_DILATIONS):
        p = a[g].reshape(nb, r, seq // r, ATT_W)
        outs.append(p.transpose(0, 2, 1, 3).reshape(nb * seq, ATT_W))
    return outs


def _mix_fwd(gate_logits, b_gate, y_ssm, y_att):
    t, d = y_ssm.shape
    tm = 512

    def fn(i, g0_ref, g1_ref, b0_ref, b1_ref, ys_ref, ya_ref):
        g0 = _sigmoid(g0_ref[...] + b0_ref[...])
        g1 = _sigmoid(g1_ref[...] + b1_ref[...])
        return [g0 * ys_ref[...] + g1 * ya_ref[...]]

    b_spec = lambda cb: pl.BlockSpec((1, d), lambda i: (0, cb))
    return _rw("mix_fwd", fn, t // tm,
               [(gate_logits, _rs(tm, d, 0)), (gate_logits, _rs(tm, d, 1)), (b_gate, b_spec(0)), (b_gate, b_spec(1)),
                (y_ssm, _rs(tm, d)), (y_att, _rs(tm, d))],
               [(SDS((t, d), BF16), _rs(tm, d))])[0]


def _mix_bwd(gate_logits, b_gate, y_ssm, y_att, dmixed):
    t, d = y_ssm.shape
    tm = 256

    def fn(i, g0_ref, g1_ref, b0_ref, b1_ref, ys_ref, ya_ref, dm_ref):
        g0 = _sigmoid(g0_ref[...] + b0_ref[...])
        g1 = _sigmoid(g1_ref[...] + b1_ref[...])
        dm = dm_ref[...]
        dg = jnp.concatenate([dm * ys_ref[...] * g0 * (1.0 - g0), dm * ya_ref[...] * g1 * (1.0 - g1)], axis=-1)
        return [dm * g0, dm * g1, dg, _colsum(dg)]

    b_spec = lambda cb: pl.BlockSpec((1, d), lambda i: (0, cb))
    return _rw("mix_bwd", fn, t // tm,
               [(gate_logits, _rs(tm, d, 0)), (gate_logits, _rs(tm, d, 1)), (b_gate, b_spec(0)), (b_gate, b_spec(1)),
                (y_ssm, _rs(tm, d)), (y_att, _rs(tm, d)), (dmixed, _rs(tm, d))],
               [(SDS((t, d), BF16), _rs(tm, d)), (SDS((t, d), BF16), _rs(tm, d)),
                (SDS((t, 2 * d), BF16), _rs(tm, 2 * d)), (SDS((1, 2 * d), F32), _fs((1, 2 * d)))], n_acc=1)


def _swiglu_fwd(gt, up):
    t, f = gt.shape
    tm = 256

    def fn(i, g_ref, u_ref):
        gv = g_ref[...]
        return [gv * _sigmoid(gv) * u_ref[...]]

    return _rw("swiglu_fwd", fn, t // tm, [(gt, _rs(tm, f)), (up, _rs(tm, f))], [(SDS((t, f), BF16), _rs(tm, f))])[0]


def _swiglu_bwd(gt, up, dact):
    t, f = gt.shape
    tm = 256

    def fn(i, g_ref, u_ref, d_ref):
        gv, dv = g_ref[...], d_ref[...]
        sg = _sigmoid(gv)
        return [dv * u_ref[...] * sg * (1.0 + gv * (1.0 - sg)), dv * gv * sg]

    return _rw("swiglu_bwd", fn, t // tm, [(gt, _rs(tm, f)), (up, _rs(tm, f)), (dact, _rs(tm, f))],
               [(SDS((t, f), BF16), _rs(tm, f))] * 2)


def _adamw(w, g, m, v, name):
    r, c = w.shape
    tr = _row_tile(r, max(8, 400_000 // c))
    c1 = 1.0 / (1.0 - ADAM_B1 ** ADAM_STEP)
    c2 = 1.0 / (1.0 - ADAM_B2 ** ADAM_STEP)

    def fn(i, w_ref, g_ref, m_ref, v_ref):
        gv = g_ref[...]
        mn = ADAM_B1 * m_ref[...] + (1.0 - ADAM_B1) * gv
        vn = ADAM_B2 * v_ref[...] + (1.0 - ADAM_B2) * (gv * gv)
        delta = -ADAM_LR * ((mn * c1) / (jnp.sqrt(vn * c2) + ADAM_EPS) + ADAM_WD * w_ref[...])
        return [delta, mn, vn]

    spec = pl.BlockSpec((tr, c), lambda i: (i, 0))
    return _rw(name, fn, r // tr, [(w, spec), (g, spec), (m, spec), (v, spec)], [(SDS((r, c), F32), spec)] * 3)


ANY = pl.BlockSpec(memory_space=pl.ANY)


def _place():
    x, y, c = lax.axis_index("x"), lax.axis_index("y"), lax.axis_index("c")
    chips = [(1 - x, y), (x, 1 - y), (1 - x, 1 - y)]
    return x, y, c, chips


def _remote(src, dst, ssem, rsem, to):
    return pltpu.make_async_remote_copy(src_ref=src, dst_ref=dst, send_sem=ssem, recv_sem=rsem, device_id=to,
                                        device_id_type=MESH)


def _gather_weights(wp):
    def body(w_ref, out_ref, ssem, rsem, lsem):
        x, y, c, chips = _place()
        me = 2 * x + y
        sib = (x, y, 1 - c)
        loc = pltpu.make_async_copy(w_ref, out_ref.at[me], lsem)
        loc.start()
        first = [_remote(w_ref.at[c], out_ref.at[me, c], ssem.at[j], rsem.at[j], (*chip, c))
                 for j, chip in enumerate(chips)]
        for cp in first:
            cp.start()
        passed = []
        for j, chip in enumerate(chips):
            ci = 2 * chip[0] + chip[1]
            _remote(w_ref.at[c], out_ref.at[ci, c], ssem.at[j], rsem.at[j], (*chip, c)).wait_recv()
            cp = _remote(out_ref.at[ci, c], out_ref.at[ci, c], ssem.at[3 + j], rsem.at[3 + j], sib)
            cp.start()
            passed.append(cp)
        for j, chip in enumerate(chips):
            ci = 2 * chip[0] + chip[1]
            _remote(out_ref.at[ci, 1 - c], out_ref.at[ci, 1 - c], ssem.at[3 + j], rsem.at[3 + j], sib).wait_recv()
        for cp in first + passed:
            cp.wait_send()
        loc.wait()

    return pl.pallas_call(
        body, name="gather_weights", in_specs=[ANY], out_specs=ANY,
        out_shape=SDS((N_CHIPS,) + wp.shape, wp.dtype),
        scratch_shapes=[pltpu.SemaphoreType.DMA((6,)), pltpu.SemaphoreType.DMA((6,)), pltpu.SemaphoreType.DMA(())],
        compiler_params=pltpu.CompilerParams(has_side_effects=True))(wp)


def _swap_halves(g2):
    def body(g_ref, out_ref, ssem, rsem):
        x, y, c, _ = _place()
        cp = _remote(g_ref.at[1 - c], out_ref, ssem, rsem, (x, y, 1 - c))
        cp.start()
        cp.wait()

    return pl.pallas_call(
        body, name="swap_halves", in_specs=[ANY], out_specs=ANY, out_shape=SDS(g2.shape[1:], g2.dtype),
        scratch_shapes=[pltpu.SemaphoreType.DMA(()), pltpu.SemaphoreType.DMA(())],
        compiler_params=pltpu.CompilerParams(has_side_effects=True))(g2)


def _add_own_half(g2, other, c):
    _, nch, rows, w = g2.shape
    tr = _row_tile(rows, 512)
    nr = rows // tr

    def body(c_ref, a_ref, b_ref, o_ref):
        o_ref[...] = a_ref[...] + b_ref[...]

    grid_spec = pltpu.PrefetchScalarGridSpec(
        num_scalar_prefetch=1, grid=(nch, nr),
        in_specs=[pl.BlockSpec((None, None, tr, w), lambda k, i, c_ref: (c_ref[0], k, i, 0)),
                  pl.BlockSpec((None, tr, w), lambda k, i, c_ref: (k, i, 0))],
        out_specs=pl.BlockSpec((None, tr, w), lambda k, i, c_ref: (k, i, 0)))
    return pl.pallas_call(
        body, name="add_own_half", grid_spec=grid_spec, out_shape=SDS(other.shape, other.dtype),
        compiler_params=_cparams(("arbitrary", "arbitrary")))(jnp.reshape(c, (1,)).astype(jnp.int32), g2, other)


def _scatter_to_chips(p):
    def body(p_ref, q_ref, ssem, rsem, lsem):
        x, y, c, chips = _place()
        me = 2 * x + y
        loc = pltpu.make_async_copy(p_ref.at[me], q_ref.at[me], lsem)
        loc.start()
        sent = []
        for j, chip in enumerate(chips):
            ci = 2 * chip[0] + chip[1]
            cp = _remote(p_ref.at[ci], q_ref.at[me], ssem.at[j], rsem.at[j], (*chip, c))
            cp.start()
            sent.append(cp)
        for j, chip in enumerate(chips):
            ci = 2 * chip[0] + chip[1]
            _remote(p_ref.at[ci], q_ref.at[ci], ssem.at[j], rsem.at[j], (*chip, c)).wait_recv()
        for cp in sent:
            cp.wait_send()
        loc.wait()

    return pl.pallas_call(
        body, name="scatter_to_chips", in_specs=[ANY], out_specs=ANY, out_shape=SDS(p.shape, p.dtype),
        scratch_shapes=[pltpu.SemaphoreType.DMA((3,)), pltpu.SemaphoreType.DMA((3,)), pltpu.SemaphoreType.DMA(())],
        compiler_params=pltpu.CompilerParams(has_side_effects=True))(p)


def _sum_chips(q):
    nch, rows, w = q.shape
    tr = _row_tile(rows, 512)

    def fn(i, q_ref):
        return [((q_ref[0] + q_ref[1]) + q_ref[2]) + q_ref[3]]

    return _rw("sum_chips", fn, rows // tr, [(q, pl.BlockSpec((nch, tr, w), lambda i: (0, i, 0)))],
               [(SDS((rows, w), q.dtype), _rs(tr, w))])[0]


def _join_halves(h):
    def body(h_ref, out_ref, ssem, rsem, lsem):
        x, y, c, _ = _place()
        loc = pltpu.make_async_copy(h_ref, out_ref.at[c], lsem)
        loc.start()
        cp = _remote(h_ref, out_ref.at[c], ssem, rsem, (x, y, 1 - c))
        cp.start()
        _remote(h_ref, out_ref.at[1 - c], ssem, rsem, (x, y, 1 - c)).wait_recv()
        cp.wait_send()
        loc.wait()

    return pl.pallas_call(
        body, name="join_halves", in_specs=[ANY], out_specs=ANY, out_shape=SDS((2,) + h.shape, h.dtype),
        scratch_shapes=[pltpu.SemaphoreType.DMA(()), pltpu.SemaphoreType.DMA(()), pltpu.SemaphoreType.DMA(())],
        compiler_params=pltpu.CompilerParams(has_side_effects=True))(h)


PACK_W = 1024
SHARDED = ("w_in", "conv_w", "w_ssm_out", "w_att_out", "w_mix_out", "w_ffn_gate", "w_ffn_up", "w_ffn_down")
COL_SHARDED = ("w_in", "conv_w", "w_att_out", "w_ffn_gate", "w_ffn_up")
SMALL = ("norm_mix", "b_gate", "conv_b", "dt_bias", "a_log", "d_skip", "ssm_norm", "norm_ffn", "norm_final")


def _rows(n):
    return -(-n // PACK_W)


def _pack_rows(parts, total_rows):
    rows = []
    for p in parts:
        flat = p.reshape(-1)
        pad = _rows(flat.shape[0]) * PACK_W - flat.shape[0]
        if pad:
            flat = jnp.concatenate([flat, jnp.zeros((pad,), flat.dtype)])
        rows.append(flat.reshape(-1, PACK_W))
    used = sum(r.shape[0] for r in rows)
    if total_rows > used:
        rows.append(jnp.zeros((total_rows - used, PACK_W), rows[0].dtype))
    return jnp.concatenate(rows, axis=0)


def _padded_rows(n):
    return -(-n // 32) * 32


def _shard_of(full, name, k):
    if name in COL_SHARDED:
        n = full.shape[1] // N_CHIPS
        return full[:, k * n:(k + 1) * n]
    n = full.shape[0] // N_CHIPS
    return full[k * n:(k + 1) * n]


def _join_shards(shards, name):
    return jnp.concatenate(shards, axis=1 if name in COL_SHARDED else 0)


def _group_major(a, axis):
    gw = D_INNER // N_GROUPS
    take = lambda lo, n: lax.slice_in_dim(a, lo, lo + n, axis=axis)
    parts = []
    for g in range(N_GROUPS):
        parts += [take(g * gw, gw), take(D_INNER + g * D_STATE, D_STATE),
                  take(D_INNER + N_GROUPS * D_STATE + g * D_STATE, D_STATE)]
    return jnp.concatenate(parts, axis=axis)


def _group_major_inv(a, axis):
    gw = D_INNER // N_GROUPS
    take = lambda lo, n: lax.slice_in_dim(a, lo, lo + n, axis=axis)
    xs = [take(g * GROUP_W, gw) for g in range(N_GROUPS)]
    bs = [take(g * GROUP_W + gw, D_STATE) for g in range(N_GROUPS)]
    cs = [take(g * GROUP_W + gw + D_STATE, D_STATE) for g in range(N_GROUPS)]
    return jnp.concatenate(xs + bs + cs, axis=axis)


def _local_step(x, target, wts):
    nb, seq, d = x.shape
    t = nb * seq
    x = x.reshape(t, d)
    target = target.reshape(t, d)
    hg = HEADS_PER_GROUP

    w_in = wts["w_in"]
    o1, o2, o3, o4 = D_INNER, D_INNER + CONV_DIM, D_INNER + CONV_DIM + N_HEADS, D_INNER + CONV_DIM + N_HEADS + QKV_DIM
    w_z = w_in[:, :o1]
    w_xbc = _group_major(w_in[:, o1:o2], 1)
    w_dt = jnp.pad(w_in[:, o2:o3], ((0, 0), (0, DT_PAD - N_HEADS)))
    w_qkv = w_in[:, o3:o4]
    w_gate = w_in[:, o4:]
    conv_w = _group_major(wts["conv_w"], 1)
    conv_b = _group_major(wts["conv_b"], 1)

    def per_group_row(p):
        return p.reshape(N_GROUPS, 1, hg)

    def per_group_col(p):
        return p.reshape(N_GROUPS, hg, 1)

    a_neg = -jnp.exp(wts["a_log"])
    bias_r, bias_c = per_group_row(wts["dt_bias"]), per_group_col(wts["dt_bias"])
    a_r, a_c = per_group_row(a_neg), per_group_col(a_neg)
    dskip_r = per_group_row(wts["d_skip"])
    cos, sin = _rope_tables(seq)

    h = _rms_fwd(x, wts["norm_mix"], "rms_mix_fwd")
    z = _mm(h, w_z, "nn", F32, "proj_z")
    xbc = _mm(h, w_xbc, "nn", F32, "proj_xbc")
    dt_raw = _mm(h, w_dt, "nn", F32, "proj_dt")
    qkv = _mm(h, w_qkv, "nn", F32, "proj_qkv")
    gate_logits = _mm(h, w_gate, "nn", F32, "proj_gate")

    xc = _conv_fwd(xbc, conv_w, conv_b, seq)
    dtr = dt_raw[:, :N_HEADS].reshape(t, N_GROUPS, hg).transpose(1, 0, 2)
    dtrt = dt_raw[:, :N_HEADS].reshape(nb, seq, N_GROUPS, hg).transpose(2, 0, 3, 1)
    y, states = _ssd_fwd(xc, dtr, dtrt, bias_r, bias_c, a_r, a_c, dskip_r, nb, seq)
    yn = _gate_norm_fwd(y, z, wts["ssm_norm"])
    y_ssm = _mm(yn, wts["w_ssm_out"], "nn", F32, "ssm_out")

    q_r, k_r, v_b = _rope_fwd(qkv, cos, sin, seq)
    qg, kg, vg = _to_groups(q_r, nb, seq), _to_groups(k_r, nb, seq), _to_groups(v_b, nb, seq)
    o_g, lse_g = _att_fwd(qg, kg, vg, seq)
    o_t = jnp.stack(_from_groups(o_g, nb, seq))
    lse_t = jnp.stack(_from_groups(lse_g, nb, seq))
    att = _merge_fwd(o_t, lse_t)
    y_att = _mm(att, wts["w_att_out"], "nn", F32, "att_out")

    mixed = _mix_fwd(gate_logits, wts["b_gate"], y_ssm, y_att)
    x1 = _mm(mixed, wts["w_mix_out"], "nn", F32, "mix_out", add=x)
    h2 = _rms_fwd(x1, wts["norm_ffn"], "rms_ffn_fwd")
    gt = _mm(h2, wts["w_ffn_gate"], "nn", F32, "ffn_gate")
    up = _mm(h2, wts["w_ffn_up"], "nn", F32, "ffn_up")
    act = _swiglu_fwd(gt, up)
    x2 = _mm(act, wts["w_ffn_down"], "nn", F32, "ffn_down", add=x1)

    g = {}
    dx2, g["norm_final"], loss = _final_fwd_bwd(x2, target, wts["norm_final"].reshape(1, d))
    dx2_b = dx2.astype(BF16)
    dact = _mm(dx2_b, wts["w_ffn_down"], "nt", F32, "d_act")
    g["w_ffn_down"] = _mm(act, dx2_b, "tn", F32, "g_ffn_down")
    dgt, dup = _swiglu_bwd(gt, up, dact)
    g["w_ffn_gate"] = _mm(h2, dgt, "tn", F32, "g_ffn_gate")
    g["w_ffn_up"] = _mm(h2, dup, "tn", F32, "g_ffn_up")
    dh2 = _mm(dgt, wts["w_ffn_gate"], "nt", F32, "d_h2_gate")
    dh2 = _mm(dup, wts["w_ffn_up"], "nt", F32, "d_h2_up", add=dh2)
    dx1, g["norm_ffn"] = _rms_bwd(x1, dh2, wts["norm_ffn"], dx2, "rms_ffn_bwd")

    dx1_b = dx1.astype(BF16)
    dmixed = _mm(dx1_b, wts["w_mix_out"], "nt", F32, "d_mixed")
    g["w_mix_out"] = _mm(mixed, dx1_b, "tn", F32, "g_mix_out")
    dy_ssm, dy_att, dgate, g["b_gate"] = _mix_bwd(gate_logits, wts["b_gate"], y_ssm, y_att, dmixed)

    datt = _mm(dy_att, wts["w_att_out"], "nt", F32, "d_att")
    g["w_att_out"] = _mm(att, dy_att, "tn", F32, "g_att_out")
    do_t, dlt_t = _merge_bwd(o_t, lse_t, datt)
    do_g = jnp.stack([_to_groups_one(do_t[i], i, nb, seq) for i in range(3)])
    dlt_g = jnp.stack([_to_groups_one(dlt_t[i], i, nb, seq) for i in range(3)])
    dq_g, dk_g, dv_g = _att_bwd(qg, kg, vg, do_g, lse_g, dlt_g, seq)
    dq = jnp.concatenate(_from_groups(dq_g, nb, seq), axis=-1)
    dk = jnp.concatenate(_from_groups(dk_g, nb, seq), axis=-1)
    dv = jnp.concatenate(_from_groups(dv_g, nb, seq), axis=-1)
    dqkv = _rope_bwd(dq, dk, dv, cos, sin, seq)

    dyn = _mm(dy_ssm, wts["w_ssm_out"], "nt", F32, "d_yn")
    g["w_ssm_out"] = _mm(yn, dy_ssm, "tn", F32, "g_ssm_out")
    dy, dz, g["ssm_norm"] = _gate_norm_bwd(y, z, wts["ssm_norm"], dyn)
    dxc, ddtr, g_bias, g_alog, g_dskip = _ssd_bwd(xc, dtr, dtrt, bias_r, bias_c, a_r, a_c, dskip_r, states, dy,
                                                   nb, seq)
    g["dt_bias"] = g_bias.reshape(1, N_HEADS)
    g["a_log"] = g_alog.reshape(1, N_HEADS)
    g["d_skip"] = g_dskip.reshape(1, N_HEADS)
    dpre, g_conv_w, g_conv_b = _conv_bwd_pre(xbc, conv_w, conv_b, dxc, seq)
    g["conv_w"] = _group_major_inv(g_conv_w, 1)
    g["conv_b"] = _group_major_inv(g_conv_b, 1)
    dxbc = _conv_bwd_in(dpre, conv_w, seq)
    ddt = jnp.pad(ddtr.transpose(1, 0, 2).reshape(t, N_HEADS), ((0, 0), (0, DT_PAD - N_HEADS))).astype(BF16)

    dh = _mm(dz, w_z, "nt", F32, "d_h_z")
    dh = _mm(dxbc, w_xbc, "nt", F32, "d_h_xbc", add=dh)
    dh = _mm(ddt, w_dt, "nt", F32, "d_h_dt", add=dh)
    dh = _mm(dqkv, w_qkv, "nt", F32, "d_h_qkv", add=dh)
    dh = _mm(dgate, w_gate, "nt", F32, "d_h_gate", add=dh)
    g["w_in"] = jnp.concatenate([
        _mm(h, dz, "tn", F32, "g_in_z"),
        _group_major_inv(_mm(h, dxbc, "tn", F32, "g_in_xbc"), 1),
        _mm(h, ddt, "tn", F32, "g_in_dt")[:, :N_HEADS],
        _mm(h, dqkv, "tn", F32, "g_in_qkv"),
        _mm(h, dgate, "tn", F32, "g_in_gate")], axis=1)
    dx, g["norm_mix"] = _rms_bwd(x, dh, wts["norm_mix"], dx1, "rms_mix_bwd")
    return loss[0, 0], dx.reshape(nb, seq, d), g


def _to_groups_one(a, g, nb, seq):
    r = ATT_DILATIONS[g]
    return a.reshape(nb, seq // r, r, ATT_W).transpose(0, 2, 1, 3).reshape(nb * seq, ATT_W)


def kernel(x, norm_mix, w_in, b_gate, conv_w, conv_b, dt_bias, a_log, d_skip, ssm_norm, w_ssm_out, w_att_out, w_mix_out, norm_ffn, w_ffn_gate, w_ffn_up, w_ffn_down, norm_final, loss_target, m_norm_mix, m_w_in, m_b_gate, m_conv_w, m_conv_b, m_dt_bias, m_a_log, m_d_skip, m_ssm_norm, m_w_ssm_out, m_w_att_out, m_w_mix_out, m_norm_ffn, m_w_ffn_gate, m_w_ffn_up, m_w_ffn_down, m_norm_final, v_norm_mix, v_w_in, v_b_gate, v_conv_w, v_conv_b, v_dt_bias, v_a_log, v_d_skip, v_ssm_norm, v_w_ssm_out, v_w_att_out, v_w_mix_out, v_norm_ffn, v_w_ffn_gate, v_w_ffn_up, v_w_ffn_down, v_norm_final):
    names = ("norm_mix", "w_in", "b_gate", "conv_w", "conv_b", "dt_bias", "a_log", "d_skip", "ssm_norm", "w_ssm_out",
             "w_att_out", "w_mix_out", "norm_ffn", "w_ffn_gate", "w_ffn_up", "w_ffn_down", "norm_final")
    w_loc = dict(zip(names, (norm_mix, w_in, b_gate, conv_w, conv_b, dt_bias, a_log, d_skip, ssm_norm, w_ssm_out,
                             w_att_out, w_mix_out, norm_ffn, w_ffn_gate, w_ffn_up, w_ffn_down, norm_final)))
    m_loc = dict(zip(names, (m_norm_mix, m_w_in, m_b_gate, m_conv_w, m_conv_b, m_dt_bias, m_a_log, m_d_skip,
                             m_ssm_norm, m_w_ssm_out, m_w_att_out, m_w_mix_out, m_norm_ffn, m_w_ffn_gate,
                             m_w_ffn_up, m_w_ffn_down, m_norm_final)))
    v_loc = dict(zip(names, (v_norm_mix, v_w_in, v_b_gate, v_conv_w, v_conv_b, v_dt_bias, v_a_log, v_d_skip,
                             v_ssm_norm, v_w_ssm_out, v_w_att_out, v_w_mix_out, v_norm_ffn, v_w_ffn_gate,
                             v_w_ffn_up, v_w_ffn_down, v_norm_final)))
    two_d = lambda a: a.reshape(a.shape[-2:]) if a.ndim >= 2 else a.reshape(1, -1)
    w2 = {n: two_d(a) for n, a in w_loc.items()}
    shard_shapes = {n: w2[n].shape for n in SHARDED}

    def wire(n):
        if n == "conv_w":
            return lax.bitcast_convert_type(w2[n], BF16)
        return w2[n].astype(BF16)

    seg_rows = {n: _rows(int(wire(n).size)) for n in SHARDED}
    w_rows = _padded_rows(sum(seg_rows.values()))
    wp = _pack_rows([wire(n) for n in SHARDED], w_rows).reshape(2, w_rows // 2, PACK_W)
    wg = _gather_weights(wp).reshape(N_CHIPS, w_rows, PACK_W)
    full = {}
    off = 0
    for n in SHARDED:
        r, (sr, sc) = seg_rows[n], shard_shapes[n]
        seg = wg[:, off:off + r].reshape(N_CHIPS, r * PACK_W)
        if n == "conv_w":
            shards = [lax.bitcast_convert_type(seg[k, :sr * sc * 2].reshape(sr, sc, 2), F32) for k in range(N_CHIPS)]
        else:
            shards = [seg[k, :sr * sc].reshape(sr, sc) for k in range(N_CHIPS)]
        full[n] = _join_shards(shards, n)
        off += r
    for n in SMALL:
        full[n] = w2[n]

    loss_sum, grad_x, g_full = _local_step(x, loss_target, full)
    loss = lax.psum(loss_sum, ("x", "y", "c"))

    small_rows = sum(_rows(int(w2[n].size)) for n in SMALL)
    g_rows = _padded_rows(sum(_rows(sr * sc) for sr, sc in shard_shapes.values()) + small_rows)
    sections = [_pack_rows([_shard_of(g_full[n], n, k) for n in SHARDED] + [g_full[n] for n in SMALL], g_rows)
                for k in range(N_CHIPS)]
    g2 = jnp.stack(sections).reshape(N_CHIPS, 2, g_rows // 2, PACK_W).transpose(1, 0, 2, 3)
    c = lax.axis_index("c")
    chip_sum = _add_own_half(g2, _swap_halves(g2), c)
    reduced = _join_halves(_sum_chips(_scatter_to_chips(chip_sum))).reshape(g_rows, PACK_W)
    g_shard = {}
    off = 0
    for n in SHARDED + SMALL:
        sr, sc = w2[n].shape
        r = _rows(sr * sc)
        g_shard[n] = reduced[off:off + r].reshape(-1)[:sr * sc].reshape(sr, sc)
        off += r

    grads, deltas, new_m, new_v = [], [], [], []
    for n in names:
        shape = w_loc[n].shape
        d_, m_, v_ = _adamw(w2[n], g_shard[n], two_d(m_loc[n]), two_d(v_loc[n]), "adamw_" + n)
        grads.append(g_shard[n].reshape(shape))
        deltas.append(d_.reshape(shape))
        new_m.append(m_.reshape(shape))
        new_v.append(v_.reshape(shape))
    return (loss, grad_x, *grads, *deltas, *new_m, *new_v)
```

```python
import functools
import math

import jax
import jax.numpy as jnp
from jax import lax
from jax.experimental import pallas as pl
from jax.experimental.pallas import tpu as pltpu

F32 = jnp.float32
BF16 = jnp.bfloat16
SDS = jax.ShapeDtypeStruct
MESH = pl.DeviceIdType.MESH

D_MODEL = 1024
D_INNER = 2048
N_HEADS = 32
HEAD_P = 64
N_GROUPS = 4
HEADS_PER_GROUP = N_HEADS // N_GROUPS
D_STATE = 128
CONV_K = 4
CHUNK = 128
CONV_DIM = D_INNER + 2 * N_GROUPS * D_STATE
GROUP_W = D_INNER // N_GROUPS + 2 * D_STATE
ATT_HEADS = 12
ATT_D = 128
ATT_SLOTS = 4
ATT_W = ATT_SLOTS * ATT_D
ATT_DILATIONS = (1, 4, 16)
ATT_BLOCK = 128
QKV_DIM = 3 * ATT_HEADS * ATT_D
D_FF = 2816
DT_PAD = 128
ROPE_THETA = 10000.0
EPS = 1e-6
N_CHIPS = 4
LANES = 128

ADAM_LR = 0.001
ADAM_B1 = 0.9
ADAM_B2 = 0.999
ADAM_EPS = 1e-08
ADAM_WD = 0.01
ADAM_STEP = 10

VMEM_LIMIT = 48 * 1024 * 1024


def _cparams(semantics):
    return pltpu.CompilerParams(dimension_semantics=semantics, vmem_limit_bytes=VMEM_LIMIT)


def _pick(n, cap):
    best = None
    for t in range(LANES, min(n, cap) + 1, LANES):
        if n % t == 0:
            best = t
    return best or n


def _row_tile(rows, cap):
    best = None
    for t in range(8, min(rows, cap) + 1, 8):
        if rows % t == 0:
            best = t
    return best or rows


def _sigmoid(x):
    return 1.0 / (1.0 + jnp.exp(-x))


def _softplus(x):
    return jnp.maximum(x, 0.0) + jnp.log(1.0 + jnp.exp(-jnp.abs(x)))


def _dot(a, b):
    return jnp.dot(a, b, preferred_element_type=F32)


def _dot_nt(a, b):
    return lax.dot_general(a, b, (((1,), (1,)), ((), ())), preferred_element_type=F32)


def _dot_tn(a, b):
    return lax.dot_general(a, b, (((0,), (0,)), ((), ())), preferred_element_type=F32)


def _mm(a, b, mode, out_dtype, name, add=None):
    if mode == "nn":
        (m, k), (_, n) = a.shape, b.shape
    elif mode == "nt":
        (m, k), (n, _) = a.shape, b.shape
    else:
        (k, m), (_, n) = a.shape, b.shape
    tm, tn = _pick(m, 512), _pick(n, 1536)
    tk = k if k <= 2048 else _pick(k, 2048)
    nk = k // tk
    dims = {"nn": ((1,), (0,)), "nt": ((1,), (1,)), "tn": ((0,), (0,))}[mode]

    def partial_product(a_ref, b_ref):
        return lax.dot_general(a_ref[...].astype(BF16), b_ref[...].astype(BF16), (dims, ((), ())),
                               preferred_element_type=F32)

    def body(*refs):
        a_ref, b_ref = refs[:2]
        c_ref = refs[2] if add is not None else None
        o_ref = refs[3] if add is not None else refs[2]

        def finish(r):
            if add is not None:
                r = r + c_ref[...].astype(F32)
            o_ref[...] = r.astype(out_dtype)

        if nk == 1:
            finish(partial_product(a_ref, b_ref))
            return
        acc = refs[-1]
        kk = pl.program_id(2)

        @pl.when(kk == 0)
        def _():
            acc[...] = partial_product(a_ref, b_ref)

        @pl.when((kk > 0) & (kk < nk - 1))
        def _():
            acc[...] += partial_product(a_ref, b_ref)

        @pl.when(kk == nk - 1)
        def _():
            finish(acc[...] + partial_product(a_ref, b_ref))

    a_spec = {"nn": pl.BlockSpec((tm, tk), lambda j, i, q: (i, q)),
              "nt": pl.BlockSpec((tm, tk), lambda j, i, q: (i, q)),
              "tn": pl.BlockSpec((tk, tm), lambda j, i, q: (q, i))}[mode]
    b_spec = {"nn": pl.BlockSpec((tk, tn), lambda j, i, q: (q, j)),
              "nt": pl.BlockSpec((tn, tk), lambda j, i, q: (j, q)),
              "tn": pl.BlockSpec((tk, tn), lambda j, i, q: (q, j))}[mode]
    o_spec = pl.BlockSpec((tm, tn), lambda j, i, q: (i, j))
    ins, specs = [a, b], [a_spec, b_spec]
    if add is not None:
        ins.append(add)
        specs.append(o_spec)
    return pl.pallas_call(
        body, name=name, grid=(n // tn, m // tm, nk), in_specs=specs, out_specs=o_spec,
        out_shape=SDS((m, n), out_dtype), scratch_shapes=[pltpu.VMEM((tm, tn), F32)] if nk > 1 else [],
        compiler_params=_cparams(("parallel", "parallel", "arbitrary")))(*ins)


def _rw(name, fn, nsteps, ins, outs, n_acc=0):
    n_in, n_out = len(ins), len(outs)

    def body(*refs):
        i = pl.program_id(0)
        vals = fn(i, *refs[:n_in])
        for q, (r, v) in enumerate(zip(refs[n_in:], vals)):
            if q < n_out - n_acc:
                r[...] = v.astype(r.dtype)
            else:
                @pl.when(i == 0)
                def _(r=r):
                    r[...] = jnp.zeros_like(r)

                r[...] += v

    return pl.pallas_call(
        body, name=name, grid=(nsteps,), in_specs=[s for _, s in ins], out_specs=[s for _, s in outs],
        out_shape=[o for o, _ in outs], compiler_params=_cparams(("arbitrary",)))(*[a for a, _ in ins])


def _rs(tm, w, cb=0):
    return pl.BlockSpec((tm, w), lambda i: (i, cb))


def _fs(shape):
    nd = len(shape)
    return pl.BlockSpec(shape, lambda i: (0,) * nd)


def _gs(g, tm, w):
    return pl.BlockSpec((None, tm, w), lambda i: (g, i, 0))


def _colsum(v):
    return jnp.sum(v, axis=0, keepdims=True)


def _rms_fwd(x, g, name):
    t, d = x.shape
    tm = 512

    def fn(i, x_ref, g_ref):
        xv = x_ref[...]
        r = lax.rsqrt(jnp.mean(xv * xv, axis=-1, keepdims=True) + EPS)
        return [xv * r * g_ref[...]]

    return _rw(name, fn, t // tm, [(x, _rs(tm, d)), (g, _fs((1, d)))], [(SDS((t, d), BF16), _rs(tm, d))])[0]


def _rms_bwd(x, dh, g, dres, name):
    t, d = x.shape
    tm = 512

    def fn(i, x_ref, dh_ref, g_ref, dres_ref):
        xv = x_ref[...]
        r = lax.rsqrt(jnp.mean(xv * xv, axis=-1, keepdims=True) + EPS)
        xhat = xv * r
        dhv = dh_ref[...]
        dxhat = dhv * g_ref[...]
        dx = r * (dxhat - xhat * jnp.mean(dxhat * xhat, axis=-1, keepdims=True))
        return [dres_ref[...] + dx, _colsum(dhv * xhat)]

    return _rw(name, fn, t // tm,
               [(x, _rs(tm, d)), (dh, _rs(tm, d)), (g, _fs((1, d))), (dres, _rs(tm, d))],
               [(SDS((t, d), F32), _rs(tm, d)), (SDS((1, d), F32), _fs((1, d)))], n_acc=1)


def _final_fwd_bwd(x2, target, g):
    t, d = x2.shape
    tm = 512

    def fn(i, x_ref, t_ref, g_ref):
        xv = x_ref[...]
        gv = g_ref[...]
        r = lax.rsqrt(jnp.mean(xv * xv, axis=-1, keepdims=True) + EPS)
        xhat = xv * r
        diff = xhat * gv - t_ref[...]
        lsum = 0.5 * jnp.sum(jnp.sum(diff * diff, axis=-1, keepdims=True) * (1.0 / d), axis=0, keepdims=True)
        dy = diff * (1.0 / d)
        dxhat = dy * gv
        dx = r * (dxhat - xhat * jnp.mean(dxhat * xhat, axis=-1, keepdims=True))
        return [dx, _colsum(dy * xhat), lsum]

    return _rw("final_norm_loss", fn, t // tm,
               [(x2, _rs(tm, d)), (target, _rs(tm, d)), (g, _fs((1, d)))],
               [(SDS((t, d), F32), _rs(tm, d)), (SDS((1, d), F32), _fs((1, d))), (SDS((1, 1), F32), _fs((1, 1)))],
               n_acc=2)


CONV_TS = 512
CONV_HALO = 8


def _conv_specs(seq, c):
    ts, tc = CONV_TS, GROUP_W
    hb = ts // CONV_HALO
    u_spec = pl.BlockSpec((ts, tc), lambda j, i: (i, j))
    prev_spec = pl.BlockSpec((CONV_HALO, tc), lambda j, i: (jnp.maximum(i * hb - 1, 0), j))
    w_spec = pl.BlockSpec((CONV_K, tc), lambda j, i: (0, j))
    b_spec = pl.BlockSpec((1, tc), lambda j, i: (0, j))
    return u_spec, prev_spec, w_spec, b_spec


def _conv_pre(i, seq, u_ref, prev_ref, w_ref, b_ref, ext):
    ts = CONV_TS
    first = (i % (seq // ts)) == 0
    ext[0:CONV_HALO, :] = jnp.where(first, 0.0, prev_ref[...])
    ext[CONV_HALO:, :] = u_ref[...]
    acc = jnp.broadcast_to(b_ref[...], u_ref.shape)
    for q in range(CONV_K):
        acc = acc + w_ref[q:q + 1, :] * ext[pl.ds(CONV_HALO - CONV_K + 1 + q, ts), :]
    return acc


def _conv_fwd(u, w, b, seq):
    t, c = u.shape
    ts, tc = CONV_TS, GROUP_W
    u_spec, prev_spec, w_spec, b_spec = _conv_specs(seq, c)

    def body(u_ref, prev_ref, w_ref, b_ref, o_ref, ext):
        pre = _conv_pre(pl.program_id(1), seq, u_ref, prev_ref, w_ref, b_ref, ext)
        o_ref[...] = pre * _sigmoid(pre)

    return pl.pallas_call(
        body, name="conv_fwd", grid=(c // tc, t // ts), in_specs=[u_spec, prev_spec, w_spec, b_spec],
        out_specs=u_spec, out_shape=SDS((t, c), F32), scratch_shapes=[pltpu.VMEM((ts + CONV_HALO, tc), F32)],
        compiler_params=_cparams(("parallel", "arbitrary")))(u, u, w, b)


def _conv_bwd_pre(u, w, b, dxc, seq):
    t, c = u.shape
    ts, tc = CONV_TS, GROUP_W
    u_spec, prev_spec, w_spec, b_spec = _conv_specs(seq, c)

    def body(u_ref, prev_ref, w_ref, b_ref, d_ref, dpre_ref, dw_ref, db_ref, ext):
        i = pl.program_id(1)
        pre = _conv_pre(i, seq, u_ref, prev_ref, w_ref, b_ref, ext)
        sg = _sigmoid(pre)
        dpre = d_ref[...] * sg * (1.0 + pre * (1.0 - sg))
        dpre_ref[...] = dpre

        @pl.when(i == 0)
        def _():
            dw_ref[...] = jnp.zeros_like(dw_ref)
            db_ref[...] = jnp.zeros_like(db_ref)

        db_ref[...] += _colsum(dpre)
        for q in range(CONV_K):
            dw_ref[q:q + 1, :] += _colsum(dpre * ext[pl.ds(CONV_HALO - CONV_K + 1 + q, ts), :])

    return pl.pallas_call(
        body, name="conv_bwd_pre", grid=(c // tc, t // ts),
        in_specs=[u_spec, prev_spec, w_spec, b_spec, u_spec], out_specs=[u_spec, w_spec, b_spec],
        out_shape=[SDS((t, c), F32), SDS((CONV_K, c), F32), SDS((1, c), F32)],
        scratch_shapes=[pltpu.VMEM((ts + CONV_HALO, tc), F32)],
        compiler_params=_cparams(("parallel", "arbitrary")))(u, u, w, b, dxc)


def _conv_bwd_in(dpre, w, seq):
    t, c = dpre.shape
    ts, tc = CONV_TS, GROUP_W
    hb = ts // CONV_HALO
    last = t // CONV_HALO - 1
    d_spec = pl.BlockSpec((ts, tc), lambda j, i: (i, j))
    next_spec = pl.BlockSpec((CONV_HALO, tc), lambda j, i: (jnp.minimum((i + 1) * hb, last), j))
    w_spec = pl.BlockSpec((CONV_K, tc), lambda j, i: (0, j))

    def body(d_ref, next_ref, w_ref, o_ref, ext):
        i = pl.program_id(1)
        nts = seq // ts
        is_last = (i % nts) == nts - 1
        ext[0:ts, :] = d_ref[...]
        ext[ts:, :] = jnp.where(is_last, 0.0, next_ref[...])
        acc = jnp.zeros(d_ref.shape, F32)
        for q in range(CONV_K):
            acc = acc + w_ref[q:q + 1, :] * ext[pl.ds(CONV_K - 1 - q, ts), :]
        o_ref[...] = acc.astype(o_ref.dtype)

    return pl.pallas_call(
        body, name="conv_bwd_in", grid=(c // tc, t // ts), in_specs=[d_spec, next_spec, w_spec],
        out_specs=d_spec, out_shape=SDS((t, c), BF16), scratch_shapes=[pltpu.VMEM((ts + CONV_HALO, tc), F32)],
        compiler_params=_cparams(("parallel", "arbitrary")))(dpre, dpre, w)


def _split3(v):
    hi = v.astype(BF16)
    r1 = v - hi.astype(F32)
    mid = r1.astype(BF16)
    lo = (r1 - mid.astype(F32)).astype(BF16)
    return hi, mid, lo


def _ssd_prelude(dtr_ref, dtrt_ref, bias_ref, biast_ref, a_ref, at_ref):
    dt = _softplus(dtr_ref[...] + bias_ref[...])
    dtt = _softplus(dtrt_ref[...] + biast_ref[...])
    ri = lax.broadcasted_iota(jnp.int32, (CHUNK, CHUNK), 0)
    ci = lax.broadcasted_iota(jnp.int32, (CHUNK, CHUNK), 1)
    lower = ri >= ci
    upper = ri <= ci
    lower_b = jnp.where(lower, 1.0, 0.0).astype(BF16)
    upper_b = jnp.where(upper, 1.0, 0.0).astype(BF16)
    acs = sum(_dot(lower_b, p) for p in _split3(dt * a_ref[...]))
    acst = sum(_dot(p, upper_b) for p in _split3(dtt * at_ref[...]))
    return dt, acs, acst, lower, upper, lower_b, upper_b


def _ssd_specs(seq):
    nc = seq // CHUNK
    hg = HEADS_PER_GROUP
    row = lambda cc: (lambda g, b, c: (b * nc + cc(c), g))
    fwd = lambda c: c
    rev = lambda c: nc - 1 - c

    def specs(cc):
        return dict(
            xc=pl.BlockSpec((CHUNK, GROUP_W), lambda g, b, c: (b * nc + cc(c), g)),
            y=pl.BlockSpec((CHUNK, D_INNER // N_GROUPS), lambda g, b, c: (b * nc + cc(c), g)),
            dtr=pl.BlockSpec((None, CHUNK, hg), lambda g, b, c: (g, b * nc + cc(c), 0)),
            dtrt=pl.BlockSpec((None, None, hg, CHUNK), lambda g, b, c: (g, b, 0, cc(c))),
            prow=pl.BlockSpec((None, 1, hg), lambda g, b, c: (g, 0, 0)),
            pcol=pl.BlockSpec((None, hg, 1), lambda g, b, c: (g, 0, 0)),
            st=pl.BlockSpec((None, None, None, D_STATE, hg * HEAD_P), lambda g, b, c: (g, b, cc(c), 0, 0)),
        )

    return specs(fwd), specs(rev)


def _head_maps():
    hw = HEADS_PER_GROUP * HEAD_P
    shift = HEAD_P.bit_length() - 1
    hj = lax.broadcasted_iota(jnp.int32, (HEADS_PER_GROUP, hw), 0)
    lq = jnp.right_shift(lax.broadcasted_iota(jnp.int32, (HEADS_PER_GROUP, hw), 1), shift)
    spread = jnp.where(hj == lq, 1.0, 0.0).astype(BF16)
    rq = jnp.right_shift(lax.broadcasted_iota(jnp.int32, (hw, LANES), 0), shift)
    cj = lax.broadcasted_iota(jnp.int32, (hw, LANES), 1)
    gather = jnp.where(rq == cj, 1.0, 0.0).astype(BF16)
    return spread, gather


def _exact_dot(v, m01):
    return sum(_dot(p, m01) for p in _split3(v))


def _ssd_fwd(xc, dtr, dtrt, bias, biast, a, at, dskip, nb, seq):
    t = xc.shape[0]
    nc = seq // CHUNK
    hg = HEADS_PER_GROUP
    hw = hg * HEAD_P
    sp, _ = _ssd_specs(seq)

    def body(xc_ref, dtr_ref, dtrt_ref, bias_ref, biast_ref, a_ref, at_ref, d_ref, y_ref, sin_ref, st):
        @pl.when(pl.program_id(2) == 0)
        def _():
            st[...] = jnp.zeros_like(st)

        s_in = st[...]
        sin_ref[...] = s_in
        dt, acs, acst, lower, _, _, _ = _ssd_prelude(dtr_ref, dtrt_ref, bias_ref, biast_ref, a_ref, at_ref)
        spread, _ = _head_maps()
        x = xc_ref[...]
        xs = x[:, :hw]
        b16 = x[:, hw:hw + D_STATE].astype(BF16)
        c16 = x[:, hw + D_STATE:].astype(BF16)
        cb = _dot_nt(c16, b16)
        last = acs[CHUNK - 1:CHUNK, :]
        e_x = _exact_dot(jnp.exp(acs), spread)
        dec_x = _exact_dot(jnp.exp(last - acs), spread)
        tot_x = e_x[CHUNK - 1:CHUNK, :]
        d_x = _exact_dot(jnp.broadcast_to(d_ref[...], (8, hg)), spread)[0:1, :]
        xdtf = xs * _exact_dot(dt, spread)
        xdt16 = xdtf.astype(BF16)
        yoff = e_x * _dot(c16, s_in.astype(BF16))
        st[...] = tot_x * s_in + _dot_tn(b16, (dec_x * xdtf).astype(BF16))
        parts = []
        for j in range(hg):
            decay = jnp.exp(jnp.where(lower, acs[:, j:j + 1] - acst[j:j + 1, :], -jnp.inf))
            parts.append(_dot((cb * decay).astype(BF16), xdt16[:, HEAD_P * j:HEAD_P * (j + 1)]))
        y_ref[...] = jnp.concatenate(parts, axis=-1) + yoff + d_x * xs

    return pl.pallas_call(
        body, name="ssd_fwd", grid=(N_GROUPS, nb, nc),
        in_specs=[sp["xc"], sp["dtr"], sp["dtrt"], sp["prow"], sp["pcol"], sp["prow"], sp["pcol"], sp["prow"]],
        out_specs=[sp["y"], sp["st"]],
        out_shape=[SDS((t, D_INNER), F32), SDS((N_GROUPS, nb, nc, D_STATE, hw), F32)],
        scratch_shapes=[pltpu.VMEM((D_STATE, hw), F32)],
        compiler_params=_cparams(("parallel", "parallel", "arbitrary")))(xc, dtr, dtrt, bias, biast, a, at, dskip)


def _ssd_bwd(xc, dtr, dtrt, bias, biast, a, at, dskip, states, dy, nb, seq):
    t = xc.shape[0]
    nc = seq // CHUNK
    hg = HEADS_PER_GROUP
    hw = hg * HEAD_P
    _, sp = _ssd_specs(seq)

    def body(xc_ref, dtr_ref, dtrt_ref, bias_ref, biast_ref, a_ref, at_ref, d_ref, sin_ref, dy_ref,
             dxc_ref, ddtr_ref, gbias_ref, ga_ref, gd_ref, ds):
        first = (pl.program_id(1) == 0) & (pl.program_id(2) == 0)

        @pl.when(pl.program_id(2) == 0)
        def _():
            ds[...] = jnp.zeros_like(ds)

        @pl.when(first)
        def _():
            gbias_ref[...] = jnp.zeros_like(gbias_ref)
            ga_ref[...] = jnp.zeros_like(ga_ref)
            gd_ref[...] = jnp.zeros_like(gd_ref)

        dt, acs, acst, lower, upper, _, upper_b = _ssd_prelude(dtr_ref, dtrt_ref, bias_ref, biast_ref, a_ref, at_ref)
        spread, gather = _head_maps()
        x = xc_ref[...]
        dy = dy_ref[...]
        xs = x[:, :hw]
        b16 = x[:, hw:hw + D_STATE].astype(BF16)
        c16 = x[:, hw + D_STATE:].astype(BF16)
        dy16 = dy.astype(BF16)
        cb = _dot_nt(c16, b16)
        cbt = _dot_nt(b16, c16)
        last = acs[CHUNK - 1:CHUNK, :]
        e8 = jnp.exp(acs)
        dec8 = jnp.exp(last - acs)
        e_x = _exact_dot(e8, spread)
        dec_x = _exact_dot(dec8, spread)
        tot_x = e_x[CHUNK - 1:CHUNK, :]
        dt_x = _exact_dot(dt, spread)
        d_x = _exact_dot(jnp.broadcast_to(d_ref[...], (8, hg)), spread)[0:1, :]
        xdtf = xs * dt_x
        xdt16 = xdtf.astype(BF16)
        s_in = sin_ref[...]
        s16 = s_in.astype(BF16)
        ds_out = ds[...]
        ds16 = ds_out.astype(BF16)
        bds = _dot(b16, ds16)
        cs = _dot(c16, s16)
        edy16 = (e_x * dy).astype(BF16)
        ds[...] = tot_x * ds_out + _dot_tn(c16, edy16)
        lane8 = lax.broadcasted_iota(jnp.int32, (CHUNK, hg), 1)
        row8 = lax.broadcasted_iota(jnp.int32, (CHUNK, hg), 0)
        dacs8 = jnp.zeros((CHUNK, hg), F32)
        acc_m = jnp.zeros((CHUNK, CHUNK), F32)
        acc_mt = jnp.zeros((CHUNK, CHUNK), F32)
        dx_parts = []
        for j in range(hg):
            sl = slice(HEAD_P * j, HEAD_P * (j + 1))
            col = acs[:, j:j + 1]
            row = acst[j:j + 1, :]
            decay = jnp.exp(jnp.where(lower, col - row, -jnp.inf))
            decayt = jnp.exp(jnp.where(upper, row - col, -jnp.inf))
            wm = _dot_nt(dy16[:, sl], xdt16[:, sl]) * decay
            wmt = _dot_nt(xdt16[:, sl], dy16[:, sl]) * decayt
            acc_m = acc_m + wm
            acc_mt = acc_mt + wmt
            dacs8 = dacs8 + jnp.where(lane8 == j, jnp.sum(wm * cb, axis=-1, keepdims=True)
                                      - jnp.sum(wmt * cbt, axis=-1, keepdims=True), 0.0)
            dx_parts.append(_dot((cbt * decayt).astype(BF16), dy16[:, sl]))
        dx = jnp.concatenate(dx_parts, axis=-1) + dec_x * bds
        dxc_ref[:, :hw] = dx * dt_x + d_x * dy
        dxc_ref[:, hw:hw + D_STATE] = _dot(acc_mt.astype(BF16), c16) + _dot_nt((dec_x * xdtf).astype(BF16), ds16)
        dxc_ref[:, hw + D_STATE:] = _dot(acc_m.astype(BF16), b16) + _dot_nt(edy16, s16)
        dtot_rows = jnp.broadcast_to(_colsum(ds_out * s_in), (8, hw))
        sums = _exact_dot(jnp.concatenate([dy * cs, xdtf * bds, dx * xs, dy * xs, dtot_rows], axis=0), gather)
        de8 = sums[0:CHUNK, :hg]
        ddec8 = sums[CHUNK:2 * CHUNK, :hg]
        ddtx8 = sums[2 * CHUNK:3 * CHUNK, :hg]
        gd8 = _colsum(sums[3 * CHUNK:4 * CHUNK, :hg])
        dtot8 = sums[4 * CHUNK:4 * CHUNK + 1, :hg]
        extra = _colsum(ddec8 * dec8) + dtot8 * e8[CHUNK - 1:CHUNK, :]
        dacs8 = dacs8 + de8 * e8 - ddec8 * dec8 + jnp.where(row8 == CHUNK - 1, extra, 0.0)
        da = sum(_dot(upper_b, p) for p in _split3(dacs8))
        av = a_ref[...]
        ddt = da * av + ddtx8
        ddtr = ddt * _sigmoid(dtr_ref[...] + bias_ref[...])
        ddtr_ref[...] = ddtr
        gbias_ref[...] += _colsum(ddtr)
        ga_ref[...] += _colsum(da * dt) * av
        gd_ref[...] += gd8

    return pl.pallas_call(
        body, name="ssd_bwd", grid=(N_GROUPS, nb, nc),
        in_specs=[sp["xc"], sp["dtr"], sp["dtrt"], sp["prow"], sp["pcol"], sp["prow"], sp["pcol"], sp["prow"],
                  sp["st"], sp["y"]],
        out_specs=[sp["xc"], sp["dtr"], sp["prow"], sp["prow"], sp["prow"]],
        out_shape=[SDS((t, N_GROUPS * GROUP_W), F32), SDS((N_GROUPS, t, hg), F32)]
        + [SDS((N_GROUPS, 1, hg), F32)] * 3,
        scratch_shapes=[pltpu.VMEM((D_STATE, hw), F32)],
        compiler_params=_cparams(("arbitrary", "arbitrary", "arbitrary")))(
            xc, dtr, dtrt, bias, biast, a, at, dskip, states, dy)


def _group_bcast(v, width, fn):
    parts = []
    for q in range(v.shape[-1] // width):
        s = fn(v[:, q * width:(q + 1) * width])
        parts.append(jnp.broadcast_to(s, (v.shape[0], width)))
    return jnp.concatenate(parts, axis=-1)


def _gate_norm_fwd(y, z, g):
    t, d = y.shape
    tm = 256
    gw = d // N_GROUPS

    def fn(i, y_ref, z_ref, g_ref):
        zv = z_ref[...]
        u = y_ref[...] * (zv * _sigmoid(zv))
        r = lax.rsqrt(_group_bcast(u * u, gw, lambda p: jnp.mean(p, axis=-1, keepdims=True)) + EPS)
        return [u * r * g_ref[...]]

    return _rw("gate_norm_fwd", fn, t // tm, [(y, _rs(tm, d)), (z, _rs(tm, d)), (g, _fs((1, d)))],
               [(SDS((t, d), BF16), _rs(tm, d))])[0]


def _gate_norm_bwd(y, z, g, dyn):
    t, d = y.shape
    tm = 256
    gw = d // N_GROUPS

    def fn(i, y_ref, z_ref, g_ref, dyn_ref):
        zv = z_ref[...]
        yv = y_ref[...]
        sg = _sigmoid(zv)
        sz = zv * sg
        u = yv * sz
        r = lax.rsqrt(_group_bcast(u * u, gw, lambda p: jnp.mean(p, axis=-1, keepdims=True)) + EPS)
        uhat = u * r
        dv = dyn_ref[...]
        duhat = dv * g_ref[...]
        du = r * (duhat - uhat * _group_bcast(duhat * uhat, gw, lambda p: jnp.mean(p, axis=-1, keepdims=True)))
        dz = du * yv * sg * (1.0 + zv * (1.0 - sg))
        return [du * sz, dz, _colsum(dv * uhat)]

    return _rw("gate_norm_bwd", fn, t // tm,
               [(y, _rs(tm, d)), (z, _rs(tm, d)), (g, _fs((1, d))), (dyn, _rs(tm, d))],
               [(SDS((t, d), F32), _rs(tm, d)), (SDS((t, d), BF16), _rs(tm, d)), (SDS((1, d), F32), _fs((1, d)))],
               n_acc=1)


def _rope_tables(seq):
    half = ATT_D // 2
    inv = ROPE_THETA ** (-jnp.arange(half, dtype=F32) / half)
    ang = jnp.arange(seq, dtype=F32)[:, None] * inv[None, :]
    cos, sin = jnp.cos(ang), jnp.sin(ang)
    return jnp.concatenate([cos, cos], axis=-1), jnp.concatenate([-sin, sin], axis=-1)


def _rope_fwd(qkv, cos, sin, seq):
    t = qkv.shape[0]
    tm = 256
    w = ATT_HEADS * ATT_D
    tab = pl.BlockSpec((tm, ATT_D), lambda i: (i % (seq // tm), 0))

    def fn(i, q_ref, k_ref, v_ref, cos_ref, sin_ref):
        c, s = cos_ref[...], sin_ref[...]

        def rot(ref):
            parts = []
            for h in range(ATT_HEADS):
                p = ref[:, h * ATT_D:(h + 1) * ATT_D]
                parts.append(p * c + pltpu.roll(p, ATT_D // 2, 1) * s)
            return jnp.concatenate(parts, axis=-1)

        return [rot(q_ref), rot(k_ref), v_ref[...]]

    return _rw("rope_fwd", fn, t // tm,
               [(qkv, _rs(tm, w, 0)), (qkv, _rs(tm, w, 1)), (qkv, _rs(tm, w, 2)), (cos, tab), (sin, tab)],
               [(SDS((t, w), BF16), _rs(tm, w))] * 3)


def _rope_bwd(dq, dk, dv, cos, sin, seq):
    t = dq.shape[0]
    tm = 256
    w = ATT_HEADS * ATT_D
    tab = pl.BlockSpec((tm, ATT_D), lambda i: (i % (seq // tm), 0))

    def fn(i, dq_ref, dk_ref, dv_ref, cos_ref, sin_ref):
        c, s = cos_ref[...], sin_ref[...]

        def rot(ref):
            parts = []
            for h in range(ATT_HEADS):
                p = ref[:, h * ATT_D:(h + 1) * ATT_D]
                parts.append(p * c - pltpu.roll(p, ATT_D // 2, 1) * s)
            return jnp.concatenate(parts, axis=-1)

        return [jnp.concatenate([rot(dq_ref), rot(dk_ref), dv_ref[...]], axis=-1)]

    return _rw("rope_bwd", fn, t // tm,
               [(dq, _rs(tm, w)), (dk, _rs(tm, w)), (dv, _rs(tm, w)), (cos, tab), (sin, tab)],
               [(SDS((t, 3 * w), BF16), _rs(tm, 3 * w))])[0]


def _att_masks():
    ri = lax.broadcasted_iota(jnp.int32, (ATT_BLOCK, ATT_BLOCK), 0)
    ci = lax.broadcasted_iota(jnp.int32, (ATT_BLOCK, ATT_BLOCK), 1)
    return ci <= ri, ci >= ri


def _blocks_per_seq(g, seq):
    return (seq // ATT_BLOCK) >> (2 * g)


def _att_fwd(q, k, v, seq):
    ng, t, w = q.shape
    nblk = t // ATT_BLOCK
    scale = ATT_D ** -0.5
    cur = pl.BlockSpec((None, ATT_BLOCK, w), lambda g, n: (g, n, 0))
    prev = pl.BlockSpec((None, ATT_BLOCK, w), lambda g, n: (g, jnp.maximum(n - 1, 0), 0))

    def body(q_ref, kc_ref, kp_ref, vc_ref, vp_ref, o_ref, lse_ref):
        g, n = pl.program_id(0), pl.program_id(1)
        has_prev = (n % _blocks_per_seq(g, seq)) != 0
        mcur, mprev = _att_masks()
        mprev = mprev & has_prev
        for h in range(ATT_SLOTS):
            sl = slice(h * ATT_D, (h + 1) * ATT_D)
            qh = q_ref[:, sl]
            sc = jnp.where(mcur, _dot_nt(qh, kc_ref[:, sl]) * scale, -jnp.inf)
            sp = jnp.where(mprev, _dot_nt(qh, kp_ref[:, sl]) * scale, -jnp.inf)
            m = jnp.maximum(jnp.max(sc, axis=-1, keepdims=True), jnp.max(sp, axis=-1, keepdims=True))
            pc = jnp.exp(sc - m)
            pp = jnp.exp(sp - m)
            den = jnp.sum(pc, axis=-1, keepdims=True) + jnp.sum(pp, axis=-1, keepdims=True)
            o = _dot(pc.astype(BF16), vc_ref[:, sl]) + _dot(pp.astype(BF16), vp_ref[:, sl])
            o_ref[:, sl] = o / den
            lse_ref[:, sl] = jnp.broadcast_to(m + jnp.log(den), (ATT_BLOCK, ATT_D))

    return pl.pallas_call(
        body, name="att_fwd", grid=(ng, nblk), in_specs=[cur, cur, prev, cur, prev], out_specs=[cur, cur],
        out_shape=[SDS((ng, t, w), F32), SDS((ng, t, w), F32)],
        compiler_params=_cparams(("parallel", "arbitrary")))(q, k, k, v, v)


def _att_bwd(q, k, v, do, lse, dlt, seq):
    ng, t, w = q.shape
    nblk = t // ATT_BLOCK
    scale = ATT_D ** -0.5
    cur = pl.BlockSpec((None, ATT_BLOCK, w), lambda g, n: (g, n, 0))
    nxt = pl.BlockSpec((None, ATT_BLOCK, w), lambda g, n: (g, jnp.minimum(n + 1, nblk - 1), 0))

    def body(qc_ref, qn_ref, k_ref, v_ref, doc_ref, don_ref, lsec_ref, lsen_ref, dltc_ref, dltn_ref,
             dq_ref, dk_ref, dv_ref, carry):
        g, n = pl.program_id(0), pl.program_id(1)
        nbs = _blocks_per_seq(g, seq)

        @pl.when((n % nbs) == 0)
        def _():
            carry[...] = jnp.zeros_like(carry)

        has_next = (((n + 1) % nbs) != 0) & (n + 1 < nblk)
        mcur, mprev = _att_masks()
        mnext = mprev & has_next
        for h in range(ATT_SLOTS):
            sl = slice(h * ATT_D, (h + 1) * ATT_D)
            kh, vh = k_ref[:, sl], v_ref[:, sl]
            qc, doc = qc_ref[:, sl], doc_ref[:, sl]
            p = jnp.where(mcur, jnp.exp(_dot_nt(qc, kh) * scale - lsec_ref[:, sl]), 0.0)
            dsc = (p * (_dot_nt(doc, vh) - dltc_ref[:, sl]) * scale).astype(BF16)
            dq_ref[:, sl] = carry[:, sl] + _dot(dsc, kh)
            qn, don = qn_ref[:, sl], don_ref[:, sl]
            pn = jnp.where(mnext, jnp.exp(_dot_nt(qn, kh) * scale - lsen_ref[:, sl]), 0.0)
            dsn = (pn * (_dot_nt(don, vh) - dltn_ref[:, sl]) * scale).astype(BF16)
            carry[:, sl] = _dot(dsn, kh)
            dk_ref[:, sl] = _dot_tn(dsc, qc) + _dot_tn(dsn, qn)
            dv_ref[:, sl] = _dot_tn(p.astype(BF16), doc) + _dot_tn(pn.astype(BF16), don)

    return pl.pallas_call(
        body, name="att_bwd", grid=(ng, nblk), in_specs=[cur, nxt, cur, cur, cur, nxt, cur, nxt, cur, nxt],
        out_specs=[cur, cur, cur], out_shape=[SDS((ng, t, w), F32)] * 3,
        scratch_shapes=[pltpu.VMEM((ATT_BLOCK, w), F32)],
        compiler_params=_cparams(("parallel", "arbitrary")))(q, q, k, v, do, do, lse, lse, dlt, dlt)


def _merge_weights(lse_refs):
    ls = [r[...] for r in lse_refs]
    m = jnp.maximum(jnp.maximum(ls[0], ls[1]), ls[2])
    es = [jnp.exp(v - m) for v in ls]
    den = es[0] + es[1] + es[2]
    return [e / den for e in es]


def _merge_fwd(o, lse):
    ng, t, w = o.shape
    tm = 512

    def fn(i, o0, o1, o2, l0, l1, l2):
        ws = _merge_weights((l0, l1, l2))
        return [ws[0] * o0[...] + ws[1] * o1[...] + ws[2] * o2[...]]

    ins = [(o, _gs(g, tm, w)) for g in range(ng)] + [(lse, _gs(g, tm, w)) for g in range(ng)]
    return _rw("att_merge_fwd", fn, t // tm, ins, [(SDS((t, w), BF16), _rs(tm, w))])[0]


def _merge_bwd(o, lse, datt):
    ng, t, w = o.shape
    tm = 512

    def fn(i, o0, o1, o2, l0, l1, l2, d_ref):
        ws = _merge_weights((l0, l1, l2))
        ov = [o0[...], o1[...], o2[...]]
        dv = d_ref[...]
        att = ws[0] * ov[0] + ws[1] * ov[1] + ws[2] * ov[2]
        dot = _group_bcast(dv * att, ATT_D, lambda p: jnp.sum(p, axis=-1, keepdims=True))
        return [jnp.stack([wg * dv for wg in ws]), jnp.stack([wg * dot for wg in ws])]

    ins = ([(o, _gs(g, tm, w)) for g in range(ng)] + [(lse, _gs(g, tm, w)) for g in range(ng)]
           + [(datt, _rs(tm, w))])
    full = pl.BlockSpec((ng, tm, w), lambda i: (0, i, 0))
    return _rw("att_merge_bwd", fn, t // tm, ins, [(SDS((ng, t, w), BF16), full), (SDS((ng, t, w), F32), full)])


def _to_groups(a, nb, seq):
    outs = []
    for g, r in enumerate(ATT_DILATIONS):
        p = a[:, g * ATT_W:(g + 1) * ATT_W].reshape(nb, seq // r, r, ATT_W)
        outs.append(p.transpose(0, 2, 1, 3).reshape(nb * seq, ATT_W))
    return jnp.stack(outs)


def _from_groups(a, nb, seq):
    outs = []
    for g, r in enumerate(ATT_DILATIONS):
        p = a[g].reshape(nb, r, seq // r, ATT_W)
        outs.append(p.transpose(0, 2, 1, 3).reshape(nb * seq, ATT_W))
    return outs


def _mix_fwd(gate_logits, b_gate, y_ssm, y_att):
    t, d = y_ssm.shape
    tm = 512

    def fn(i, g0_ref, g1_ref, b0_ref, b1_ref, ys_ref, ya_ref):
        g0 = _sigmoid(g0_ref[...] + b0_ref[...])
        g1 = _sigmoid(g1_ref[...] + b1_ref[...])
        return [g0 * ys_ref[...] + g1 * ya_ref[...]]

    b_spec = lambda cb: pl.BlockSpec((1, d), lambda i: (0, cb))
    return _rw("mix_fwd", fn, t // tm,
               [(gate_logits, _rs(tm, d, 0)), (gate_logits, _rs(tm, d, 1)), (b_gate, b_spec(0)), (b_gate, b_spec(1)),
                (y_ssm, _rs(tm, d)), (y_att, _rs(tm, d))],
               [(SDS((t, d), BF16), _rs(tm, d))])[0]


def _mix_bwd(gate_logits, b_gate, y_ssm, y_att, dmixed):
    t, d = y_ssm.shape
    tm = 256

    def fn(i, g0_ref, g1_ref, b0_ref, b1_ref, ys_ref, ya_ref, dm_ref):
        g0 = _sigmoid(g0_ref[...] + b0_ref[...])
        g1 = _sigmoid(g1_ref[...] + b1_ref[...])
        dm = dm_ref[...]
        dg = jnp.concatenate([dm * ys_ref[...] * g0 * (1.0 - g0), dm * ya_ref[...] * g1 * (1.0 - g1)], axis=-1)
        return [dm * g0, dm * g1, dg, _colsum(dg)]

    b_spec = lambda cb: pl.BlockSpec((1, d), lambda i: (0, cb))
    return _rw("mix_bwd", fn, t // tm,
               [(gate_logits, _rs(tm, d, 0)), (gate_logits, _rs(tm, d, 1)), (b_gate, b_spec(0)), (b_gate, b_spec(1)),
                (y_ssm, _rs(tm, d)), (y_att, _rs(tm, d)), (dmixed, _rs(tm, d))],
               [(SDS((t, d), BF16), _rs(tm, d)), (SDS((t, d), BF16), _rs(tm, d)),
                (SDS((t, 2 * d), BF16), _rs(tm, 2 * d)), (SDS((1, 2 * d), F32), _fs((1, 2 * d)))], n_acc=1)


def _swiglu_fwd(gt, up):
    t, f = gt.shape
    tm = 256

    def fn(i, g_ref, u_ref):
        gv = g_ref[...]
        return [gv * _sigmoid(gv) * u_ref[...]]

    return _rw("swiglu_fwd", fn, t // tm, [(gt, _rs(tm, f)), (up, _rs(tm, f))], [(SDS((t, f), BF16), _rs(tm, f))])[0]


def _swiglu_bwd(gt, up, dact):
    t, f = gt.shape
    tm = 256

    def fn(i, g_ref, u_ref, d_ref):
        gv, dv = g_ref[...], d_ref[...]
        sg = _sigmoid(gv)
        return [dv * u_ref[...] * sg * (1.0 + gv * (1.0 - sg)), dv * gv * sg]

    return _rw("swiglu_bwd", fn, t // tm, [(gt, _rs(tm, f)), (up, _rs(tm, f)), (dact, _rs(tm, f))],
               [(SDS((t, f), BF16), _rs(tm, f))] * 2)


def _adamw(w, g, m, v, name):
    r, c = w.shape
    tr = _row_tile(r, max(8, 400_000 // c))
    c1 = 1.0 / (1.0 - ADAM_B1 ** ADAM_STEP)
    c2 = 1.0 / (1.0 - ADAM_B2 ** ADAM_STEP)

    def fn(i, w_ref, g_ref, m_ref, v_ref):
        gv = g_ref[...]
        mn = ADAM_B1 * m_ref[...] + (1.0 - ADAM_B1) * gv
        vn = ADAM_B2 * v_ref[...] + (1.0 - ADAM_B2) * (gv * gv)
        delta = -ADAM_LR * ((mn * c1) / (jnp.sqrt(vn * c2) + ADAM_EPS) + ADAM_WD * w_ref[...])
        return [delta, mn, vn]

    spec = pl.BlockSpec((tr, c), lambda i: (i, 0))
    return _rw(name, fn, r // tr, [(w, spec), (g, spec), (m, spec), (v, spec)], [(SDS((r, c), F32), spec)] * 3)


ANY = pl.BlockSpec(memory_space=pl.ANY)


def _place():
    x, y, c = lax.axis_index("x"), lax.axis_index("y"), lax.axis_index("c")
    chips = [(1 - x, y), (x, 1 - y), (1 - x, 1 - y)]
    return x, y, c, chips


def _remote(src, dst, ssem, rsem, to):
    return pltpu.make_async_remote_copy(src_ref=src, dst_ref=dst, send_sem=ssem, recv_sem=rsem, device_id=to,
                                        device_id_type=MESH)


def _gather_weights(wp):
    def body(w_ref, out_ref, ssem, rsem):
        x, y, c, chips = _place()
        me = 2 * x + y
        sib = (x, y, 1 - c)
        first = [_remote(w_ref.at[c], out_ref.at[me, c], ssem.at[j], rsem.at[j], (*chip, c))
                 for j, chip in enumerate(chips)]
        for cp in first:
            cp.start()
        passed = []
        for j, chip in enumerate(chips):
            ci = 2 * chip[0] + chip[1]
            _remote(w_ref.at[c], out_ref.at[ci, c], ssem.at[j], rsem.at[j], (*chip, c)).wait_recv()
            cp = _remote(out_ref.at[ci, c], out_ref.at[ci, c], ssem.at[3 + j], rsem.at[3 + j], sib)
            cp.start()
            passed.append(cp)
        for j, chip in enumerate(chips):
            ci = 2 * chip[0] + chip[1]
            _remote(out_ref.at[ci, 1 - c], out_ref.at[ci, 1 - c], ssem.at[3 + j], rsem.at[3 + j], sib).wait_recv()
        for cp in first + passed:
            cp.wait_send()

    return pl.pallas_call(
        body, name="gather_weights", in_specs=[ANY], out_specs=ANY,
        out_shape=SDS((N_CHIPS,) + wp.shape, wp.dtype),
        scratch_shapes=[pltpu.SemaphoreType.DMA((6,)), pltpu.SemaphoreType.DMA((6,))],
        compiler_params=pltpu.CompilerParams(has_side_effects=True))(wp)


def _swap_halves(g2):
    def body(g_ref, out_ref, ssem, rsem):
        x, y, c, _ = _place()
        cp = _remote(g_ref.at[1 - c], out_ref, ssem, rsem, (x, y, 1 - c))
        cp.start()
        cp.wait()

    return pl.pallas_call(
        body, name="swap_halves", in_specs=[ANY], out_specs=ANY, out_shape=SDS(g2.shape[1:], g2.dtype),
        scratch_shapes=[pltpu.SemaphoreType.DMA(()), pltpu.SemaphoreType.DMA(())],
        compiler_params=pltpu.CompilerParams(has_side_effects=True))(g2)


def _add_own_half(g2, other, c):
    _, nch, rows, w = g2.shape
    tr = _row_tile(rows, 512)
    nr = rows // tr

    def body(c_ref, a_ref, b_ref, o_ref):
        o_ref[...] = (a_ref[...].astype(F32) + b_ref[...].astype(F32)).astype(o_ref.dtype)

    grid_spec = pltpu.PrefetchScalarGridSpec(
        num_scalar_prefetch=1, grid=(nch, nr),
        in_specs=[pl.BlockSpec((None, None, tr, w), lambda k, i, c_ref: (c_ref[0], k, i, 0)),
                  pl.BlockSpec((None, tr, w), lambda k, i, c_ref: (k, i, 0))],
        out_specs=pl.BlockSpec((None, tr, w), lambda k, i, c_ref: (k, i, 0)))
    return pl.pallas_call(
        body, name="add_own_half", grid_spec=grid_spec, out_shape=SDS(other.shape, other.dtype),
        compiler_params=_cparams(("arbitrary", "arbitrary")))(jnp.reshape(c, (1,)).astype(jnp.int32), g2, other)


def _scatter_to_chips(p):
    def body(p_ref, q_ref, ssem, rsem):
        x, y, c, chips = _place()
        me = 2 * x + y
        sent = []
        for j, chip in enumerate(chips):
            ci = 2 * chip[0] + chip[1]
            cp = _remote(p_ref.at[ci], q_ref.at[me], ssem.at[j], rsem.at[j], (*chip, c))
            cp.start()
            sent.append(cp)
        for j, chip in enumerate(chips):
            ci = 2 * chip[0] + chip[1]
            _remote(p_ref.at[ci], q_ref.at[ci], ssem.at[j], rsem.at[j], (*chip, c)).wait_recv()
        for cp in sent:
            cp.wait_send()

    return pl.pallas_call(
        body, name="scatter_to_chips", in_specs=[ANY], out_specs=ANY, out_shape=SDS(p.shape, p.dtype),
        scratch_shapes=[pltpu.SemaphoreType.DMA((3,)), pltpu.SemaphoreType.DMA((3,))],
        compiler_params=pltpu.CompilerParams(has_side_effects=True))(p)


def _sum_chips(q):
    nch, rows, w = q.shape
    tr = _row_tile(rows, 512)

    def fn(i, q_ref):
        return [((q_ref[0].astype(F32) + q_ref[1].astype(F32)) + q_ref[2].astype(F32)) + q_ref[3].astype(F32)]

    return _rw("sum_chips", fn, rows // tr, [(q, pl.BlockSpec((nch, tr, w), lambda i: (0, i, 0)))],
               [(SDS((rows, w), F32), _rs(tr, w))])[0]


def _allreduce_small(v):
    rows, w = v.shape
    offsets = [(dx, dy, dc) for dx in (0, 1) for dy in (0, 1) for dc in (0, 1)][1:]

    def body(v_ref, o_ref, buf, ssem, rsem):
        x, y, c, _ = _place()
        flip = lambda p, d: 1 - p if d else p
        peers = [(flip(x, dx), flip(y, dy), flip(c, dc)) for dx, dy, dc in offsets]
        index = lambda p: 4 * p[0] + 2 * p[1] + p[2]
        me = index((x, y, c))
        buf[me] = v_ref[...]
        sent = [_remote(v_ref, buf.at[me], ssem.at[q], rsem.at[q], p) for q, p in enumerate(peers)]
        for cp in sent:
            cp.start()
        for q, p in enumerate(peers):
            _remote(v_ref, buf.at[index(p)], ssem.at[q], rsem.at[q], p).wait_recv()
        for cp in sent:
            cp.wait_send()
        acc = buf[0]
        for q in range(1, 8):
            acc = acc + buf[q]
        o_ref[...] = acc

    vm = pl.BlockSpec(memory_space=pltpu.VMEM)
    return pl.pallas_call(
        body, name="allreduce_small", in_specs=[vm], out_specs=vm, out_shape=SDS((rows, w), F32),
        scratch_shapes=[pltpu.VMEM((8, rows, w), F32), pltpu.SemaphoreType.DMA((7,)), pltpu.SemaphoreType.DMA((7,))],
        compiler_params=pltpu.CompilerParams(has_side_effects=True))(v)


def _join_halves(h):
    def body(h_ref, out_ref, ssem, rsem):
        x, y, c, _ = _place()
        cp = _remote(h_ref, out_ref.at[c], ssem, rsem, (x, y, 1 - c))
        cp.start()
        _remote(h_ref, out_ref.at[1 - c], ssem, rsem, (x, y, 1 - c)).wait_recv()
        cp.wait_send()

    return pl.pallas_call(
        body, name="join_halves", in_specs=[ANY], out_specs=ANY, out_shape=SDS((2,) + h.shape, h.dtype),
        scratch_shapes=[pltpu.SemaphoreType.DMA(()), pltpu.SemaphoreType.DMA(())],
        compiler_params=pltpu.CompilerParams(has_side_effects=True))(h)


PACK_W = 1024
SHARDED = ("w_in", "conv_w", "w_ssm_out", "w_att_out", "w_mix_out", "w_ffn_gate", "w_ffn_up", "w_ffn_down")
COL_SHARDED = ("w_in", "conv_w", "w_att_out", "w_ffn_gate", "w_ffn_up")
SMALL = ("norm_mix", "b_gate", "conv_b", "dt_bias", "a_log", "d_skip", "ssm_norm", "norm_ffn", "norm_final")


PACK_ROW_ALIGN = 16


def _rows(n):
    return -(-n // (PACK_W * PACK_ROW_ALIGN)) * PACK_ROW_ALIGN


def _pack_rows(parts, total_rows):
    rows = []
    for p in parts:
        flat = p.reshape(-1)
        pad = _rows(flat.shape[0]) * PACK_W - flat.shape[0]
        if pad:
            flat = jnp.concatenate([flat, jnp.zeros((pad,), flat.dtype)])
        rows.append(flat.reshape(-1, PACK_W))
    used = sum(r.shape[0] for r in rows)
    if total_rows > used:
        rows.append(jnp.zeros((total_rows - used, PACK_W), rows[0].dtype))
    return jnp.concatenate(rows, axis=0)


def _padded_rows(n):
    return -(-n // 32) * 32


def _shard_of(full, name, k):
    if name in COL_SHARDED:
        n = full.shape[1] // N_CHIPS
        return full[:, k * n:(k + 1) * n]
    n = full.shape[0] // N_CHIPS
    return full[k * n:(k + 1) * n]


def _join_shards(shards, name):
    return jnp.concatenate(shards, axis=1 if name in COL_SHARDED else 0)


def _group_major(a, axis):
    gw = D_INNER // N_GROUPS
    take = lambda lo, n: lax.slice_in_dim(a, lo, lo + n, axis=axis)
    parts = []
    for g in range(N_GROUPS):
        parts += [take(g * gw, gw), take(D_INNER + g * D_STATE, D_STATE),
                  take(D_INNER + N_GROUPS * D_STATE + g * D_STATE, D_STATE)]
    return jnp.concatenate(parts, axis=axis)


def _group_major_inv(a, axis):
    gw = D_INNER // N_GROUPS
    take = lambda lo, n: lax.slice_in_dim(a, lo, lo + n, axis=axis)
    xs = [take(g * GROUP_W, gw) for g in range(N_GROUPS)]
    bs = [take(g * GROUP_W + gw, D_STATE) for g in range(N_GROUPS)]
    cs = [take(g * GROUP_W + gw + D_STATE, D_STATE) for g in range(N_GROUPS)]
    return jnp.concatenate(xs + bs + cs, axis=axis)


def _local_step(x, target, wts):
    nb, seq, d = x.shape
    t = nb * seq
    x = x.reshape(t, d)
    target = target.reshape(t, d)
    hg = HEADS_PER_GROUP

    w_in = wts["w_in"]
    o1, o2, o3, o4 = D_INNER, D_INNER + CONV_DIM, D_INNER + CONV_DIM + N_HEADS, D_INNER + CONV_DIM + N_HEADS + QKV_DIM
    w_z = w_in[:, :o1]
    w_xbc = _group_major(w_in[:, o1:o2], 1)
    w_dt = jnp.pad(w_in[:, o2:o3], ((0, 0), (0, DT_PAD - N_HEADS)))
    w_qkv = w_in[:, o3:o4]
    w_gate = w_in[:, o4:]
    conv_w = _group_major(wts["conv_w"], 1)
    conv_b = _group_major(wts["conv_b"], 1)

    def per_group_row(p):
        return p.reshape(N_GROUPS, 1, hg)

    def per_group_col(p):
        return p.reshape(N_GROUPS, hg, 1)

    a_neg = -jnp.exp(wts["a_log"])
    bias_r, bias_c = per_group_row(wts["dt_bias"]), per_group_col(wts["dt_bias"])
    a_r, a_c = per_group_row(a_neg), per_group_col(a_neg)
    dskip_r = per_group_row(wts["d_skip"])
    cos, sin = _rope_tables(seq)

    h = _rms_fwd(x, wts["norm_mix"], "rms_mix_fwd")
    z = _mm(h, w_z, "nn", F32, "proj_z")
    xbc = _mm(h, w_xbc, "nn", F32, "proj_xbc")
    dt_raw = _mm(h, w_dt, "nn", F32, "proj_dt")
    qkv = _mm(h, w_qkv, "nn", F32, "proj_qkv")
    gate_logits = _mm(h, w_gate, "nn", F32, "proj_gate")

    xc = _conv_fwd(xbc, conv_w, conv_b, seq)
    dtr = dt_raw[:, :N_HEADS].reshape(t, N_GROUPS, hg).transpose(1, 0, 2)
    dtrt = dt_raw[:, :N_HEADS].reshape(nb, seq, N_GROUPS, hg).transpose(2, 0, 3, 1)
    y, states = _ssd_fwd(xc, dtr, dtrt, bias_r, bias_c, a_r, a_c, dskip_r, nb, seq)
    yn = _gate_norm_fwd(y, z, wts["ssm_norm"])
    y_ssm = _mm(yn, wts["w_ssm_out"], "nn", F32, "ssm_out")

    q_r, k_r, v_b = _rope_fwd(qkv, cos, sin, seq)
    qg, kg, vg = _to_groups(q_r, nb, seq), _to_groups(k_r, nb, seq), _to_groups(v_b, nb, seq)
    o_g, lse_g = _att_fwd(qg, kg, vg, seq)
    o_t = jnp.stack(_from_groups(o_g, nb, seq))
    lse_t = jnp.stack(_from_groups(lse_g, nb, seq))
    att = _merge_fwd(o_t, lse_t)
    y_att = _mm(att, wts["w_att_out"], "nn", F32, "att_out")

    mixed = _mix_fwd(gate_logits, wts["b_gate"], y_ssm, y_att)
    x1 = _mm(mixed, wts["w_mix_out"], "nn", F32, "mix_out", add=x)
    h2 = _rms_fwd(x1, wts["norm_ffn"], "rms_ffn_fwd")
    gt = _mm(h2, wts["w_ffn_gate"], "nn", F32, "ffn_gate")
    up = _mm(h2, wts["w_ffn_up"], "nn", F32, "ffn_up")
    act = _swiglu_fwd(gt, up)
    x2 = _mm(act, wts["w_ffn_down"], "nn", F32, "ffn_down", add=x1)

    g = {}
    dx2, g["norm_final"], loss = _final_fwd_bwd(x2, target, wts["norm_final"].reshape(1, d))
    dx2_b = dx2.astype(BF16)
    dact = _mm(dx2_b, wts["w_ffn_down"], "nt", F32, "d_act")
    g["w_ffn_down"] = _mm(act, dx2_b, "tn", BF16, "g_ffn_down")
    dgt, dup = _swiglu_bwd(gt, up, dact)
    g["w_ffn_gate"] = _mm(h2, dgt, "tn", BF16, "g_ffn_gate")
    g["w_ffn_up"] = _mm(h2, dup, "tn", BF16, "g_ffn_up")
    dh2 = _mm(dgt, wts["w_ffn_gate"], "nt", F32, "d_h2_gate")
    dh2 = _mm(dup, wts["w_ffn_up"], "nt", F32, "d_h2_up", add=dh2)
    dx1, g["norm_ffn"] = _rms_bwd(x1, dh2, wts["norm_ffn"], dx2, "rms_ffn_bwd")

    dx1_b = dx1.astype(BF16)
    dmixed = _mm(dx1_b, wts["w_mix_out"], "nt", F32, "d_mixed")
    g["w_mix_out"] = _mm(mixed, dx1_b, "tn", BF16, "g_mix_out")
    dy_ssm, dy_att, dgate, g["b_gate"] = _mix_bwd(gate_logits, wts["b_gate"], y_ssm, y_att, dmixed)

    datt = _mm(dy_att, wts["w_att_out"], "nt", F32, "d_att")
    g["w_att_out"] = _mm(att, dy_att, "tn", BF16, "g_att_out")
    do_t, dlt_t = _merge_bwd(o_t, lse_t, datt)
    do_g = jnp.stack([_to_groups_one(do_t[i], i, nb, seq) for i in range(3)])
    dlt_g = jnp.stack([_to_groups_one(dlt_t[i], i, nb, seq) for i in range(3)])
    dq_g, dk_g, dv_g = _att_bwd(qg, kg, vg, do_g, lse_g, dlt_g, seq)
    dq = jnp.concatenate(_from_groups(dq_g, nb, seq), axis=-1)
    dk = jnp.concatenate(_from_groups(dk_g, nb, seq), axis=-1)
    dv = jnp.concatenate(_from_groups(dv_g, nb, seq), axis=-1)
    dqkv = _rope_bwd(dq, dk, dv, cos, sin, seq)

    dyn = _mm(dy_ssm, wts["w_ssm_out"], "nt", F32, "d_yn")
    g["w_ssm_out"] = _mm(yn, dy_ssm, "tn", BF16, "g_ssm_out")
    dy, dz, g["ssm_norm"] = _gate_norm_bwd(y, z, wts["ssm_norm"], dyn)
    dxc, ddtr, g_bias, g_alog, g_dskip = _ssd_bwd(xc, dtr, dtrt, bias_r, bias_c, a_r, a_c, dskip_r, states, dy,
                                                   nb, seq)
    g["dt_bias"] = g_bias.reshape(1, N_HEADS)
    g["a_log"] = g_alog.reshape(1, N_HEADS)
    g["d_skip"] = g_dskip.reshape(1, N_HEADS)
    dpre, g_conv_w, g_conv_b = _conv_bwd_pre(xbc, conv_w, conv_b, dxc, seq)
    g["conv_w"] = _group_major_inv(g_conv_w, 1)
    g["conv_b"] = _group_major_inv(g_conv_b, 1)
    dxbc = _conv_bwd_in(dpre, conv_w, seq)
    ddt = jnp.pad(ddtr.transpose(1, 0, 2).reshape(t, N_HEADS), ((0, 0), (0, DT_PAD - N_HEADS))).astype(BF16)

    dh = _mm(dz, w_z, "nt", F32, "d_h_z")
    dh = _mm(dxbc, w_xbc, "nt", F32, "d_h_xbc", add=dh)
    dh = _mm(ddt, w_dt, "nt", F32, "d_h_dt", add=dh)
    dh = _mm(dqkv, w_qkv, "nt", F32, "d_h_qkv", add=dh)
    dh = _mm(dgate, w_gate, "nt", F32, "d_h_gate", add=dh)
    g["w_in"] = jnp.concatenate([
        _mm(h, dz, "tn", BF16, "g_in_z"),
        _group_major_inv(_mm(h, dxbc, "tn", BF16, "g_in_xbc"), 1),
        _mm(h, ddt, "tn", BF16, "g_in_dt")[:, :N_HEADS],
        _mm(h, dqkv, "tn", BF16, "g_in_qkv"),
        _mm(h, dgate, "tn", BF16, "g_in_gate")], axis=1)
    dx, g["norm_mix"] = _rms_bwd(x, dh, wts["norm_mix"], dx1, "rms_mix_bwd")
    return loss[0, 0], dx.reshape(nb, seq, d), g


def _to_groups_one(a, g, nb, seq):
    r = ATT_DILATIONS[g]
    return a.reshape(nb, seq // r, r, ATT_W).transpose(0, 2, 1, 3).reshape(nb * seq, ATT_W)


def kernel(x, norm_mix, w_in, b_gate, conv_w, conv_b, dt_bias, a_log, d_skip, ssm_norm, w_ssm_out, w_att_out, w_mix_out, norm_ffn, w_ffn_gate, w_ffn_up, w_ffn_down, norm_final, loss_target, m_norm_mix, m_w_in, m_b_gate, m_conv_w, m_conv_b, m_dt_bias, m_a_log, m_d_skip, m_ssm_norm, m_w_ssm_out, m_w_att_out, m_w_mix_out, m_norm_ffn, m_w_ffn_gate, m_w_ffn_up, m_w_ffn_down, m_norm_final, v_norm_mix, v_w_in, v_b_gate, v_conv_w, v_conv_b, v_dt_bias, v_a_log, v_d_skip, v_ssm_norm, v_w_ssm_out, v_w_att_out, v_w_mix_out, v_norm_ffn, v_w_ffn_gate, v_w_ffn_up, v_w_ffn_down, v_norm_final):
    names = ("norm_mix", "w_in", "b_gate", "conv_w", "conv_b", "dt_bias", "a_log", "d_skip", "ssm_norm", "w_ssm_out",
             "w_att_out", "w_mix_out", "norm_ffn", "w_ffn_gate", "w_ffn_up", "w_ffn_down", "norm_final")
    w_loc = dict(zip(names, (norm_mix, w_in, b_gate, conv_w, conv_b, dt_bias, a_log, d_skip, ssm_norm, w_ssm_out,
                             w_att_out, w_mix_out, norm_ffn, w_ffn_gate, w_ffn_up, w_ffn_down, norm_final)))
    m_loc = dict(zip(names, (m_norm_mix, m_w_in, m_b_gate, m_conv_w, m_conv_b, m_dt_bias, m_a_log, m_d_skip,
                             m_ssm_norm, m_w_ssm_out, m_w_att_out, m_w_mix_out, m_norm_ffn, m_w_ffn_gate,
                             m_w_ffn_up, m_w_ffn_down, m_norm_final)))
    v_loc = dict(zip(names, (v_norm_mix, v_w_in, v_b_gate, v_conv_w, v_conv_b, v_dt_bias, v_a_log, v_d_skip,
                             v_ssm_norm, v_w_ssm_out, v_w_att_out, v_w_mix_out, v_norm_ffn, v_w_ffn_gate,
                             v_w_ffn_up, v_w_ffn_down, v_norm_final)))
    two_d = lambda a: a.reshape(a.shape[-2:]) if a.ndim >= 2 else a.reshape(1, -1)
    w2 = {n: two_d(a) for n, a in w_loc.items()}
    shard_shapes = {n: w2[n].shape for n in SHARDED}

    def wire(n):
        if n == "conv_w":
            return lax.bitcast_convert_type(w2[n], BF16)
        return w2[n].astype(BF16)

    seg_rows = {n: _rows(int(wire(n).size)) for n in SHARDED}
    w_rows = _padded_rows(sum(seg_rows.values()))
    wp = _pack_rows([wire(n) for n in SHARDED], w_rows).reshape(2, w_rows // 2, PACK_W)
    chip = 2 * lax.axis_index("x") + lax.axis_index("y")
    c = lax.axis_index("c")
    wg = lax.dynamic_update_index_in_dim(_gather_weights(wp), wp, chip, 0).reshape(N_CHIPS, w_rows, PACK_W)
    full = {}
    off = 0
    for n in SHARDED:
        r, (sr, sc) = seg_rows[n], shard_shapes[n]
        seg = wg[:, off:off + r].reshape(N_CHIPS, r * PACK_W)
        if n == "conv_w":
            shards = [lax.bitcast_convert_type(seg[k, :sr * sc * 2].reshape(sr, sc, 2), F32) for k in range(N_CHIPS)]
        else:
            shards = [seg[k, :sr * sc].reshape(sr, sc) for k in range(N_CHIPS)]
        full[n] = _join_shards(shards, n)
        off += r
    for n in SMALL:
        full[n] = w2[n]

    loss_sum, grad_x, g_full = _local_step(x, loss_target, full)
    loss = lax.psum(loss_sum, ("x", "y", "c"))

    g_shard = {}
    small_flat = jnp.concatenate([g_full[n].reshape(-1) for n in SMALL])
    small = _allreduce_small(_pack_rows([small_flat], _rows(int(small_flat.size)))).reshape(-1)
    off = 0
    for n in SMALL:
        g_shard[n] = small[off:off + w2[n].size].reshape(w2[n].shape)
        off += w2[n].size

    g_rows = _padded_rows(sum(_rows(sr * sc) for sr, sc in shard_shapes.values()))
    sections = [_pack_rows([_shard_of(g_full[n].astype(BF16), n, k) for n in SHARDED], g_rows)
                for k in range(N_CHIPS)]
    g2 = jnp.stack(sections).reshape(N_CHIPS, 2, g_rows // 2, PACK_W).transpose(1, 0, 2, 3)
    chip_sum = _add_own_half(g2, _swap_halves(g2), c)
    own = lax.dynamic_index_in_dim(chip_sum, chip, 0, keepdims=False)
    by_source = lax.dynamic_update_index_in_dim(_scatter_to_chips(chip_sum), own, chip, 0)
    half = _sum_chips(by_source)
    reduced = lax.dynamic_update_index_in_dim(_join_halves(half), half, c, 0).reshape(g_rows, PACK_W)
    off = 0
    for n in SHARDED:
        sr, sc = w2[n].shape
        r = _rows(sr * sc)
        g_shard[n] = reduced[off:off + r].reshape(-1)[:sr * sc].reshape(sr, sc)
        off += r

    grads, deltas, new_m, new_v = [], [], [], []
    for n in names:
        shape = w_loc[n].shape
        d_, m_, v_ = _adamw(w2[n], g_shard[n], two_d(m_loc[n]), two_d(v_loc[n]), "adamw_" + n)
        grads.append(g_shard[n].reshape(shape))
        deltas.append(d_.reshape(shape))
        new_m.append(m_.reshape(shape))
        new_v.append(v_.reshape(shape))
    return (loss, grad_x, *grads, *deltas, *new_m, *new_v)
```

```python
import functools
import math

import jax
import jax.numpy as jnp
from jax import lax
from jax.experimental import pallas as pl
from jax.experimental.pallas import tpu as pltpu

F32 = jnp.float32
BF16 = jnp.bfloat16
SDS = jax.ShapeDtypeStruct
MESH = pl.DeviceIdType.MESH

D_MODEL = 1024
D_INNER = 2048
N_HEADS = 32
HEAD_P = 64
N_GROUPS = 4
HEADS_PER_GROUP = N_HEADS // N_GROUPS
D_STATE = 128
CONV_K = 4
CHUNK = 128
CONV_DIM = D_INNER + 2 * N_GROUPS * D_STATE
GROUP_W = D_INNER // N_GROUPS + 2 * D_STATE
ATT_HEADS = 12
ATT_D = 128
ATT_SLOTS = 4
ATT_W = ATT_SLOTS * ATT_D
ATT_DILATIONS = (1, 4, 16)
ATT_BLOCK = 128
QKV_DIM = 3 * ATT_HEADS * ATT_D
D_FF = 2816
DT_PAD = 128
ROPE_THETA = 10000.0
EPS = 1e-6
N_CHIPS = 4
LANES = 128

ADAM_LR = 0.001
ADAM_B1 = 0.9
ADAM_B2 = 0.999
ADAM_EPS = 1e-08
ADAM_WD = 0.01
ADAM_STEP = 10

VMEM_LIMIT = 48 * 1024 * 1024


def _cparams(semantics):
    return pltpu.CompilerParams(dimension_semantics=semantics, vmem_limit_bytes=VMEM_LIMIT)


def _pick(n, cap):
    best = None
    for t in range(LANES, min(n, cap) + 1, LANES):
        if n % t == 0:
            best = t
    return best or n


def _row_tile(rows, cap):
    best = None
    for t in range(8, min(rows, cap) + 1, 8):
        if rows % t == 0:
            best = t
    return best or rows


def _sigmoid(x):
    return 1.0 / (1.0 + jnp.exp(-x))


def _softplus(x):
    return jnp.maximum(x, 0.0) + jnp.log(1.0 + jnp.exp(-jnp.abs(x)))


def _dot(a, b):
    return jnp.dot(a, b, preferred_element_type=F32)


def _dot_nt(a, b):
    return lax.dot_general(a, b, (((1,), (1,)), ((), ())), preferred_element_type=F32)


def _dot_tn(a, b):
    return lax.dot_general(a, b, (((0,), (0,)), ((), ())), preferred_element_type=F32)


def _mm(a, b, mode, out_dtype, name, add=None):
    if mode == "nn":
        (m, k), (_, n) = a.shape, b.shape
    elif mode == "nt":
        (m, k), (n, _) = a.shape, b.shape
    else:
        (k, m), (_, n) = a.shape, b.shape
    tm, tn = _pick(m, 512), _pick(n, 1536)
    tk = k if k <= 2048 else _pick(k, 2048)
    nk = k // tk
    dims = {"nn": ((1,), (0,)), "nt": ((1,), (1,)), "tn": ((0,), (0,))}[mode]

    def partial_product(a_ref, b_ref):
        return lax.dot_general(a_ref[...].astype(BF16), b_ref[...].astype(BF16), (dims, ((), ())),
                               preferred_element_type=F32)

    def body(*refs):
        a_ref, b_ref = refs[:2]
        c_ref = refs[2] if add is not None else None
        o_ref = refs[3] if add is not None else refs[2]

        def finish(r):
            if add is not None:
                r = r + c_ref[...].astype(F32)
            o_ref[...] = r.astype(out_dtype)

        if nk == 1:
            finish(partial_product(a_ref, b_ref))
            return
        acc = refs[-1]
        kk = pl.program_id(2)

        @pl.when(kk == 0)
        def _():
            acc[...] = partial_product(a_ref, b_ref)

        @pl.when((kk > 0) & (kk < nk - 1))
        def _():
            acc[...] += partial_product(a_ref, b_ref)

        @pl.when(kk == nk - 1)
        def _():
            finish(acc[...] + partial_product(a_ref, b_ref))

    a_spec = {"nn": pl.BlockSpec((tm, tk), lambda j, i, q: (i, q)),
              "nt": pl.BlockSpec((tm, tk), lambda j, i, q: (i, q)),
              "tn": pl.BlockSpec((tk, tm), lambda j, i, q: (q, i))}[mode]
    b_spec = {"nn": pl.BlockSpec((tk, tn), lambda j, i, q: (q, j)),
              "nt": pl.BlockSpec((tn, tk), lambda j, i, q: (j, q)),
              "tn": pl.BlockSpec((tk, tn), lambda j, i, q: (q, j))}[mode]
    o_spec = pl.BlockSpec((tm, tn), lambda j, i, q: (i, j))
    ins, specs = [a, b], [a_spec, b_spec]
    if add is not None:
        ins.append(add)
        specs.append(o_spec)
    return pl.pallas_call(
        body, name=name, grid=(n // tn, m // tm, nk), in_specs=specs, out_specs=o_spec,
        out_shape=SDS((m, n), out_dtype), scratch_shapes=[pltpu.VMEM((tm, tn), F32)] if nk > 1 else [],
        compiler_params=_cparams(("parallel", "parallel", "arbitrary")))(*ins)


def _rw(name, fn, nsteps, ins, outs, n_acc=0):
    n_in, n_out = len(ins), len(outs)

    def body(*refs):
        i = pl.program_id(0)
        vals = fn(i, *refs[:n_in])
        for q, (r, v) in enumerate(zip(refs[n_in:], vals)):
            if q < n_out - n_acc:
                r[...] = v.astype(r.dtype)
            else:
                @pl.when(i == 0)
                def _(r=r):
                    r[...] = jnp.zeros_like(r)

                r[...] += v

    return pl.pallas_call(
        body, name=name, grid=(nsteps,), in_specs=[s for _, s in ins], out_specs=[s for _, s in outs],
        out_shape=[o for o, _ in outs], compiler_params=_cparams(("arbitrary",)))(*[a for a, _ in ins])


def _rs(tm, w, cb=0):
    return pl.BlockSpec((tm, w), lambda i: (i, cb))


def _fs(shape):
    nd = len(shape)
    return pl.BlockSpec(shape, lambda i: (0,) * nd)


def _gs(g, tm, w):
    return pl.BlockSpec((None, tm, w), lambda i: (g, i, 0))


def _colsum(v):
    return jnp.sum(v, axis=0, keepdims=True)


def _rms_fwd(x, g, name):
    t, d = x.shape
    tm = 512

    def fn(i, x_ref, g_ref):
        xv = x_ref[...]
        r = lax.rsqrt(jnp.mean(xv * xv, axis=-1, keepdims=True) + EPS)
        return [xv * r * g_ref[...]]

    return _rw(name, fn, t // tm, [(x, _rs(tm, d)), (g, _fs((1, d)))], [(SDS((t, d), BF16), _rs(tm, d))])[0]


def _rms_bwd(x, dh, g, dres, name):
    t, d = x.shape
    tm = 512

    def fn(i, x_ref, dh_ref, g_ref, dres_ref):
        xv = x_ref[...]
        r = lax.rsqrt(jnp.mean(xv * xv, axis=-1, keepdims=True) + EPS)
        xhat = xv * r
        dhv = dh_ref[...]
        dxhat = dhv * g_ref[...]
        dx = dres_ref[...] + r * (dxhat - xhat * jnp.mean(dxhat * xhat, axis=-1, keepdims=True))
        return [dx, dx, _colsum(dhv * xhat)]

    return _rw(name, fn, t // tm,
               [(x, _rs(tm, d)), (dh, _rs(tm, d)), (g, _fs((1, d))), (dres, _rs(tm, d))],
               [(SDS((t, d), F32), _rs(tm, d)), (SDS((t, d), BF16), _rs(tm, d)), (SDS((1, d), F32), _fs((1, d)))],
               n_acc=1)


def _final_fwd_bwd(x2, target, g):
    t, d = x2.shape
    tm = 512

    def fn(i, x_ref, t_ref, g_ref):
        xv = x_ref[...]
        gv = g_ref[...]
        r = lax.rsqrt(jnp.mean(xv * xv, axis=-1, keepdims=True) + EPS)
        xhat = xv * r
        diff = xhat * gv - t_ref[...]
        lsum = 0.5 * jnp.sum(jnp.sum(diff * diff, axis=-1, keepdims=True) * (1.0 / d), axis=0, keepdims=True)
        dy = diff * (1.0 / d)
        dxhat = dy * gv
        dx = r * (dxhat - xhat * jnp.mean(dxhat * xhat, axis=-1, keepdims=True))
        return [dx, dx, _colsum(dy * xhat), lsum]

    return _rw("final_norm_loss", fn, t // tm,
               [(x2, _rs(tm, d)), (target, _rs(tm, d)), (g, _fs((1, d)))],
               [(SDS((t, d), F32), _rs(tm, d)), (SDS((t, d), BF16), _rs(tm, d)), (SDS((1, d), F32), _fs((1, d))),
                (SDS((1, 1), F32), _fs((1, 1)))], n_acc=2)


CONV_TS = 512
CONV_HALO = 8


def _conv_specs(seq, c):
    ts, tc = CONV_TS, GROUP_W
    hb = ts // CONV_HALO
    u_spec = pl.BlockSpec((ts, tc), lambda j, i: (i, j))
    prev_spec = pl.BlockSpec((CONV_HALO, tc), lambda j, i: (jnp.maximum(i * hb - 1, 0), j))
    w_spec = pl.BlockSpec((CONV_K, tc), lambda j, i: (0, j))
    b_spec = pl.BlockSpec((1, tc), lambda j, i: (0, j))
    return u_spec, prev_spec, w_spec, b_spec


def _conv_pre(i, seq, u_ref, prev_ref, w_ref, b_ref, ext):
    ts = CONV_TS
    first = (i % (seq // ts)) == 0
    ext[0:CONV_HALO, :] = jnp.where(first, 0.0, prev_ref[...])
    ext[CONV_HALO:, :] = u_ref[...]
    acc = jnp.broadcast_to(b_ref[...], u_ref.shape)
    for q in range(CONV_K):
        acc = acc + w_ref[q:q + 1, :] * ext[pl.ds(CONV_HALO - CONV_K + 1 + q, ts), :]
    return acc


def _conv_fwd(u, w, b, seq):
    t, c = u.shape
    ts, tc = CONV_TS, GROUP_W
    u_spec, prev_spec, w_spec, b_spec = _conv_specs(seq, c)

    def body(u_ref, prev_ref, w_ref, b_ref, o_ref, ext):
        pre = _conv_pre(pl.program_id(1), seq, u_ref, prev_ref, w_ref, b_ref, ext)
        o_ref[...] = pre * _sigmoid(pre)

    return pl.pallas_call(
        body, name="conv_fwd", grid=(c // tc, t // ts), in_specs=[u_spec, prev_spec, w_spec, b_spec],
        out_specs=u_spec, out_shape=SDS((t, c), F32), scratch_shapes=[pltpu.VMEM((ts + CONV_HALO, tc), F32)],
        compiler_params=_cparams(("parallel", "arbitrary")))(u, u, w, b)


def _conv_bwd_pre(u, w, b, dxc, seq):
    t, c = u.shape
    ts, tc = CONV_TS, GROUP_W
    u_spec, prev_spec, w_spec, b_spec = _conv_specs(seq, c)

    def body(u_ref, prev_ref, w_ref, b_ref, d_ref, dpre_ref, dw_ref, db_ref, ext):
        i = pl.program_id(1)
        pre = _conv_pre(i, seq, u_ref, prev_ref, w_ref, b_ref, ext)
        sg = _sigmoid(pre)
        dpre = d_ref[...] * sg * (1.0 + pre * (1.0 - sg))
        dpre_ref[...] = dpre

        @pl.when(i == 0)
        def _():
            dw_ref[...] = jnp.zeros_like(dw_ref)
            db_ref[...] = jnp.zeros_like(db_ref)

        db_ref[...] += _colsum(dpre)
        for q in range(CONV_K):
            dw_ref[q:q + 1, :] += _colsum(dpre * ext[pl.ds(CONV_HALO - CONV_K + 1 + q, ts), :])

    return pl.pallas_call(
        body, name="conv_bwd_pre", grid=(c // tc, t // ts),
        in_specs=[u_spec, prev_spec, w_spec, b_spec, u_spec], out_specs=[u_spec, w_spec, b_spec],
        out_shape=[SDS((t, c), F32), SDS((CONV_K, c), F32), SDS((1, c), F32)],
        scratch_shapes=[pltpu.VMEM((ts + CONV_HALO, tc), F32)],
        compiler_params=_cparams(("parallel", "arbitrary")))(u, u, w, b, dxc)


def _conv_bwd_in(dpre, w, seq):
    t, c = dpre.shape
    ts, tc = CONV_TS, GROUP_W
    hb = ts // CONV_HALO
    last = t // CONV_HALO - 1
    d_spec = pl.BlockSpec((ts, tc), lambda j, i: (i, j))
    next_spec = pl.BlockSpec((CONV_HALO, tc), lambda j, i: (jnp.minimum((i + 1) * hb, last), j))
    w_spec = pl.BlockSpec((CONV_K, tc), lambda j, i: (0, j))

    def body(d_ref, next_ref, w_ref, o_ref, ext):
        i = pl.program_id(1)
        nts = seq // ts
        is_last = (i % nts) == nts - 1
        ext[0:ts, :] = d_ref[...]
        ext[ts:, :] = jnp.where(is_last, 0.0, next_ref[...])
        acc = jnp.zeros(d_ref.shape, F32)
        for q in range(CONV_K):
            acc = acc + w_ref[q:q + 1, :] * ext[pl.ds(CONV_K - 1 - q, ts), :]
        o_ref[...] = acc.astype(o_ref.dtype)

    return pl.pallas_call(
        body, name="conv_bwd_in", grid=(c // tc, t // ts), in_specs=[d_spec, next_spec, w_spec],
        out_specs=d_spec, out_shape=SDS((t, c), BF16), scratch_shapes=[pltpu.VMEM((ts + CONV_HALO, tc), F32)],
        compiler_params=_cparams(("parallel", "arbitrary")))(dpre, dpre, w)


def _split3(v):
    hi = v.astype(BF16)
    r1 = v - hi.astype(F32)
    mid = r1.astype(BF16)
    lo = (r1 - mid.astype(F32)).astype(BF16)
    return hi, mid, lo


def _ssd_prelude(dtr_ref, dtrt_ref, bias_ref, biast_ref, a_ref, at_ref):
    dt = _softplus(dtr_ref[...] + bias_ref[...])
    dtt = _softplus(dtrt_ref[...] + biast_ref[...])
    ri = lax.broadcasted_iota(jnp.int32, (CHUNK, CHUNK), 0)
    ci = lax.broadcasted_iota(jnp.int32, (CHUNK, CHUNK), 1)
    lower = ri >= ci
    upper = ri <= ci
    lower_b = jnp.where(lower, 1.0, 0.0).astype(BF16)
    upper_b = jnp.where(upper, 1.0, 0.0).astype(BF16)
    acs = sum(_dot(lower_b, p) for p in _split3(dt * a_ref[...]))
    acst = sum(_dot(p, upper_b) for p in _split3(dtt * at_ref[...]))
    return dt, acs, acst, lower, upper, lower_b, upper_b


def _ssd_specs(seq):
    nc = seq // CHUNK
    hg = HEADS_PER_GROUP
    row = lambda cc: (lambda g, b, c: (b * nc + cc(c), g))
    fwd = lambda c: c
    rev = lambda c: nc - 1 - c

    def specs(cc):
        return dict(
            xc=pl.BlockSpec((CHUNK, GROUP_W), lambda g, b, c: (b * nc + cc(c), g)),
            y=pl.BlockSpec((CHUNK, D_INNER // N_GROUPS), lambda g, b, c: (b * nc + cc(c), g)),
            dtr=pl.BlockSpec((None, CHUNK, hg), lambda g, b, c: (g, b * nc + cc(c), 0)),
            dtrt=pl.BlockSpec((None, None, hg, CHUNK), lambda g, b, c: (g, b, 0, cc(c))),
            prow=pl.BlockSpec((None, 1, hg), lambda g, b, c: (g, 0, 0)),
            pcol=pl.BlockSpec((None, hg, 1), lambda g, b, c: (g, 0, 0)),
            st=pl.BlockSpec((None, None, None, D_STATE, hg * HEAD_P), lambda g, b, c: (g, b, cc(c), 0, 0)),
        )

    return specs(fwd), specs(rev)


def _head_maps():
    hw = HEADS_PER_GROUP * HEAD_P
    shift = HEAD_P.bit_length() - 1
    hj = lax.broadcasted_iota(jnp.int32, (HEADS_PER_GROUP, hw), 0)
    lq = jnp.right_shift(lax.broadcasted_iota(jnp.int32, (HEADS_PER_GROUP, hw), 1), shift)
    spread = jnp.where(hj == lq, 1.0, 0.0).astype(BF16)
    rq = jnp.right_shift(lax.broadcasted_iota(jnp.int32, (hw, LANES), 0), shift)
    cj = lax.broadcasted_iota(jnp.int32, (hw, LANES), 1)
    gather = jnp.where(rq == cj, 1.0, 0.0).astype(BF16)
    return spread, gather


def _exact_dot(v, m01):
    return sum(_dot(p, m01) for p in _split3(v))


def _ssd_fwd(xc, dtr, dtrt, bias, biast, a, at, dskip, nb, seq):
    t = xc.shape[0]
    nc = seq // CHUNK
    hg = HEADS_PER_GROUP
    hw = hg * HEAD_P
    sp, _ = _ssd_specs(seq)

    def body(xc_ref, dtr_ref, dtrt_ref, bias_ref, biast_ref, a_ref, at_ref, d_ref, y_ref, sin_ref, st):
        @pl.when(pl.program_id(2) == 0)
        def _():
            st[...] = jnp.zeros_like(st)

        s_in = st[...]
        sin_ref[...] = s_in
        dt, acs, acst, lower, _, _, _ = _ssd_prelude(dtr_ref, dtrt_ref, bias_ref, biast_ref, a_ref, at_ref)
        spread, _ = _head_maps()
        x = xc_ref[...]
        xs = x[:, :hw]
        b16 = x[:, hw:hw + D_STATE].astype(BF16)
        c16 = x[:, hw + D_STATE:].astype(BF16)
        cb = _dot_nt(c16, b16)
        last = acs[CHUNK - 1:CHUNK, :]
        e_x = _exact_dot(jnp.exp(acs), spread)
        dec_x = _exact_dot(jnp.exp(last - acs), spread)
        tot_x = e_x[CHUNK - 1:CHUNK, :]
        d_x = _exact_dot(jnp.broadcast_to(d_ref[...], (8, hg)), spread)[0:1, :]
        xdtf = xs * _exact_dot(dt, spread)
        xdt16 = xdtf.astype(BF16)
        yoff = e_x * _dot(c16, s_in.astype(BF16))
        st[...] = tot_x * s_in + _dot_tn(b16, (dec_x * xdtf).astype(BF16))
        parts = []
        for j in range(hg):
            decay = jnp.exp(jnp.where(lower, acs[:, j:j + 1] - acst[j:j + 1, :], -jnp.inf))
            parts.append(_dot((cb * decay).astype(BF16), xdt16[:, HEAD_P * j:HEAD_P * (j + 1)]))
        y_ref[...] = jnp.concatenate(parts, axis=-1) + yoff + d_x * xs

    return pl.pallas_call(
        body, name="ssd_fwd", grid=(N_GROUPS, nb, nc),
        in_specs=[sp["xc"], sp["dtr"], sp["dtrt"], sp["prow"], sp["pcol"], sp["prow"], sp["pcol"], sp["prow"]],
        out_specs=[sp["y"], sp["st"]],
        out_shape=[SDS((t, D_INNER), F32), SDS((N_GROUPS, nb, nc, D_STATE, hw), F32)],
        scratch_shapes=[pltpu.VMEM((D_STATE, hw), F32)],
        compiler_params=_cparams(("parallel", "parallel", "arbitrary")))(xc, dtr, dtrt, bias, biast, a, at, dskip)


def _ssd_bwd(xc, dtr, dtrt, bias, biast, a, at, dskip, states, dy, nb, seq):
    t = xc.shape[0]
    nc = seq // CHUNK
    hg = HEADS_PER_GROUP
    hw = hg * HEAD_P
    _, sp = _ssd_specs(seq)

    def body(xc_ref, dtr_ref, dtrt_ref, bias_ref, biast_ref, a_ref, at_ref, d_ref, sin_ref, dy_ref,
             dxc_ref, ddtr_ref, gbias_ref, ga_ref, gd_ref, ds):
        first = (pl.program_id(1) == 0) & (pl.program_id(2) == 0)

        @pl.when(pl.program_id(2) == 0)
        def _():
            ds[...] = jnp.zeros_like(ds)

        @pl.when(first)
        def _():
            gbias_ref[...] = jnp.zeros_like(gbias_ref)
            ga_ref[...] = jnp.zeros_like(ga_ref)
            gd_ref[...] = jnp.zeros_like(gd_ref)

        dt, acs, acst, lower, upper, _, upper_b = _ssd_prelude(dtr_ref, dtrt_ref, bias_ref, biast_ref, a_ref, at_ref)
        spread, gather = _head_maps()
        x = xc_ref[...]
        dy = dy_ref[...]
        xs = x[:, :hw]
        b16 = x[:, hw:hw + D_STATE].astype(BF16)
        c16 = x[:, hw + D_STATE:].astype(BF16)
        dy16 = dy.astype(BF16)
        cb = _dot_nt(c16, b16)
        cbt = _dot_nt(b16, c16)
        last = acs[CHUNK - 1:CHUNK, :]
        e8 = jnp.exp(acs)
        dec8 = jnp.exp(last - acs)
        e_x = _exact_dot(e8, spread)
        dec_x = _exact_dot(dec8, spread)
        tot_x = e_x[CHUNK - 1:CHUNK, :]
        dt_x = _exact_dot(dt, spread)
        d_x = _exact_dot(jnp.broadcast_to(d_ref[...], (8, hg)), spread)[0:1, :]
        xdtf = xs * dt_x
        xdt16 = xdtf.astype(BF16)
        s_in = sin_ref[...]
        s16 = s_in.astype(BF16)
        ds_out = ds[...]
        ds16 = ds_out.astype(BF16)
        bds = _dot(b16, ds16)
        cs = _dot(c16, s16)
        edy16 = (e_x * dy).astype(BF16)
        ds[...] = tot_x * ds_out + _dot_tn(c16, edy16)
        lane8 = lax.broadcasted_iota(jnp.int32, (CHUNK, hg), 1)
        row8 = lax.broadcasted_iota(jnp.int32, (CHUNK, hg), 0)
        dacs8 = jnp.zeros((CHUNK, hg), F32)
        acc_m = jnp.zeros((CHUNK, CHUNK), F32)
        acc_mt = jnp.zeros((CHUNK, CHUNK), F32)
        dx_parts = []
        for j in range(hg):
            sl = slice(HEAD_P * j, HEAD_P * (j + 1))
            col = acs[:, j:j + 1]
            row = acst[j:j + 1, :]
            decay = jnp.exp(jnp.where(lower, col - row, -jnp.inf))
            decayt = jnp.exp(jnp.where(upper, row - col, -jnp.inf))
            wm = _dot_nt(dy16[:, sl], xdt16[:, sl]) * decay
            wmt = _dot_nt(xdt16[:, sl], dy16[:, sl]) * decayt
            acc_m = acc_m + wm
            acc_mt = acc_mt + wmt
            dacs8 = dacs8 + jnp.where(lane8 == j, jnp.sum(wm * cb, axis=-1, keepdims=True)
                                      - jnp.sum(wmt * cbt, axis=-1, keepdims=True), 0.0)
            dx_parts.append(_dot((cbt * decayt).astype(BF16), dy16[:, sl]))
        dx = jnp.concatenate(dx_parts, axis=-1) + dec_x * bds
        dxc_ref[:, :hw] = dx * dt_x + d_x * dy
        dxc_ref[:, hw:hw + D_STATE] = _dot(acc_mt.astype(BF16), c16) + _dot_nt((dec_x * xdtf).astype(BF16), ds16)
        dxc_ref[:, hw + D_STATE:] = _dot(acc_m.astype(BF16), b16) + _dot_nt(edy16, s16)
        dtot_rows = jnp.broadcast_to(_colsum(ds_out * s_in), (8, hw))
        sums = _exact_dot(jnp.concatenate([dy * cs, xdtf * bds, dx * xs, dy * xs, dtot_rows], axis=0), gather)
        de8 = sums[0:CHUNK, :hg]
        ddec8 = sums[CHUNK:2 * CHUNK, :hg]
        ddtx8 = sums[2 * CHUNK:3 * CHUNK, :hg]
        gd8 = _colsum(sums[3 * CHUNK:4 * CHUNK, :hg])
        dtot8 = sums[4 * CHUNK:4 * CHUNK + 1, :hg]
        extra = _colsum(ddec8 * dec8) + dtot8 * e8[CHUNK - 1:CHUNK, :]
        dacs8 = dacs8 + de8 * e8 - ddec8 * dec8 + jnp.where(row8 == CHUNK - 1, extra, 0.0)
        da = sum(_dot(upper_b, p) for p in _split3(dacs8))
        av = a_ref[...]
        ddt = da * av + ddtx8
        ddtr = ddt * _sigmoid(dtr_ref[...] + bias_ref[...])
        ddtr_ref[...] = ddtr
        gbias_ref[...] += _colsum(ddtr)
        ga_ref[...] += _colsum(da * dt) * av
        gd_ref[...] += gd8

    return pl.pallas_call(
        body, name="ssd_bwd", grid=(N_GROUPS, nb, nc),
        in_specs=[sp["xc"], sp["dtr"], sp["dtrt"], sp["prow"], sp["pcol"], sp["prow"], sp["pcol"], sp["prow"],
                  sp["st"], sp["y"]],
        out_specs=[sp["xc"], sp["dtr"], sp["prow"], sp["prow"], sp["prow"]],
        out_shape=[SDS((t, N_GROUPS * GROUP_W), F32), SDS((N_GROUPS, t, hg), F32)]
        + [SDS((N_GROUPS, 1, hg), F32)] * 3,
        scratch_shapes=[pltpu.VMEM((D_STATE, hw), F32)],
        compiler_params=_cparams(("arbitrary", "arbitrary", "arbitrary")))(
            xc, dtr, dtrt, bias, biast, a, at, dskip, states, dy)


def _group_bcast(v, width, fn):
    parts = []
    for q in range(v.shape[-1] // width):
        s = fn(v[:, q * width:(q + 1) * width])
        parts.append(jnp.broadcast_to(s, (v.shape[0], width)))
    return jnp.concatenate(parts, axis=-1)


def _gate_norm_fwd(y, z, g):
    t, d = y.shape
    tm = 256
    gw = d // N_GROUPS

    def fn(i, y_ref, z_ref, g_ref):
        zv = z_ref[...]
        u = y_ref[...] * (zv * _sigmoid(zv))
        r = lax.rsqrt(_group_bcast(u * u, gw, lambda p: jnp.mean(p, axis=-1, keepdims=True)) + EPS)
        return [u * r * g_ref[...]]

    return _rw("gate_norm_fwd", fn, t // tm, [(y, _rs(tm, d)), (z, _rs(tm, d)), (g, _fs((1, d)))],
               [(SDS((t, d), BF16), _rs(tm, d))])[0]


def _gate_norm_bwd(y, z, g, dyn):
    t, d = y.shape
    tm = 256
    gw = d // N_GROUPS

    def fn(i, y_ref, z_ref, g_ref, dyn_ref):
        zv = z_ref[...]
        yv = y_ref[...]
        sg = _sigmoid(zv)
        sz = zv * sg
        u = yv * sz
        r = lax.rsqrt(_group_bcast(u * u, gw, lambda p: jnp.mean(p, axis=-1, keepdims=True)) + EPS)
        uhat = u * r
        dv = dyn_ref[...]
        duhat = dv * g_ref[...]
        du = r * (duhat - uhat * _group_bcast(duhat * uhat, gw, lambda p: jnp.mean(p, axis=-1, keepdims=True)))
        dz = du * yv * sg * (1.0 + zv * (1.0 - sg))
        return [du * sz, dz, _colsum(dv * uhat)]

    return _rw("gate_norm_bwd", fn, t // tm,
               [(y, _rs(tm, d)), (z, _rs(tm, d)), (g, _fs((1, d))), (dyn, _rs(tm, d))],
               [(SDS((t, d), F32), _rs(tm, d)), (SDS((t, d), BF16), _rs(tm, d)), (SDS((1, d), F32), _fs((1, d)))],
               n_acc=1)


def _rope_tables(seq):
    half = ATT_D // 2
    inv = ROPE_THETA ** (-jnp.arange(half, dtype=F32) / half)
    ang = jnp.arange(seq, dtype=F32)[:, None] * inv[None, :]
    cos, sin = jnp.cos(ang), jnp.sin(ang)
    return jnp.concatenate([cos, cos], axis=-1), jnp.concatenate([-sin, sin], axis=-1)


def _rope_fwd(qkv, cos, sin, seq):
    t = qkv.shape[0]
    tm = 256
    w = ATT_HEADS * ATT_D
    tab = pl.BlockSpec((tm, ATT_D), lambda i: (i % (seq // tm), 0))

    def fn(i, q_ref, k_ref, v_ref, cos_ref, sin_ref):
        c, s = cos_ref[...], sin_ref[...]

        def rot(ref):
            parts = []
            for h in range(ATT_HEADS):
                p = ref[:, h * ATT_D:(h + 1) * ATT_D]
                parts.append(p * c + pltpu.roll(p, ATT_D // 2, 1) * s)
            return jnp.concatenate(parts, axis=-1)

        return [rot(q_ref), rot(k_ref), v_ref[...]]

    return _rw("rope_fwd", fn, t // tm,
               [(qkv, _rs(tm, w, 0)), (qkv, _rs(tm, w, 1)), (qkv, _rs(tm, w, 2)), (cos, tab), (sin, tab)],
               [(SDS((t, w), BF16), _rs(tm, w))] * 3)


def _rope_bwd(dq, dk, dv, cos, sin, seq):
    t = dq.shape[0]
    tm = 256
    w = ATT_HEADS * ATT_D
    tab = pl.BlockSpec((tm, ATT_D), lambda i: (i % (seq // tm), 0))

    def fn(i, dq_ref, dk_ref, dv_ref, cos_ref, sin_ref):
        c, s = cos_ref[...], sin_ref[...]

        def rot(ref):
            parts = []
            for h in range(ATT_HEADS):
                p = ref[:, h * ATT_D:(h + 1) * ATT_D]
                parts.append(p * c - pltpu.roll(p, ATT_D // 2, 1) * s)
            return jnp.concatenate(parts, axis=-1)

        return [jnp.concatenate([rot(dq_ref), rot(dk_ref), dv_ref[...]], axis=-1)]

    return _rw("rope_bwd", fn, t // tm,
               [(dq, _rs(tm, w)), (dk, _rs(tm, w)), (dv, _rs(tm, w)), (cos, tab), (sin, tab)],
               [(SDS((t, 3 * w), BF16), _rs(tm, 3 * w))])[0]


def _att_masks():
    ri = lax.broadcasted_iota(jnp.int32, (ATT_BLOCK, ATT_BLOCK), 0)
    ci = lax.broadcasted_iota(jnp.int32, (ATT_BLOCK, ATT_BLOCK), 1)
    return ci <= ri, ci >= ri


def _blocks_per_seq(g, seq):
    return (seq // ATT_BLOCK) >> (2 * g)


def _att_fwd(q, k, v, seq):
    ng, t, w = q.shape
    nblk = t // ATT_BLOCK
    scale = ATT_D ** -0.5
    cur = pl.BlockSpec((None, ATT_BLOCK, w), lambda g, n: (g, n, 0))
    prev = pl.BlockSpec((None, ATT_BLOCK, w), lambda g, n: (g, jnp.maximum(n - 1, 0), 0))

    def body(q_ref, kc_ref, kp_ref, vc_ref, vp_ref, o_ref, lse_ref):
        g, n = pl.program_id(0), pl.program_id(1)
        has_prev = (n % _blocks_per_seq(g, seq)) != 0
        mcur, mprev = _att_masks()
        mprev = mprev & has_prev
        for h in range(ATT_SLOTS):
            sl = slice(h * ATT_D, (h + 1) * ATT_D)
            qh = q_ref[:, sl]
            sc = jnp.where(mcur, _dot_nt(qh, kc_ref[:, sl]) * scale, -jnp.inf)
            sp = jnp.where(mprev, _dot_nt(qh, kp_ref[:, sl]) * scale, -jnp.inf)
            m = jnp.maximum(jnp.max(sc, axis=-1, keepdims=True), jnp.max(sp, axis=-1, keepdims=True))
            pc = jnp.exp(sc - m)
            pp = jnp.exp(sp - m)
            den = jnp.sum(pc, axis=-1, keepdims=True) + jnp.sum(pp, axis=-1, keepdims=True)
            o = _dot(pc.astype(BF16), vc_ref[:, sl]) + _dot(pp.astype(BF16), vp_ref[:, sl])
            o_ref[:, sl] = o / den
            lse_ref[:, sl] = jnp.broadcast_to(m + jnp.log(den), (ATT_BLOCK, ATT_D))

    return pl.pallas_call(
        body, name="att_fwd", grid=(ng, nblk), in_specs=[cur, cur, prev, cur, prev], out_specs=[cur, cur],
        out_shape=[SDS((ng, t, w), F32), SDS((ng, t, w), F32)],
        compiler_params=_cparams(("parallel", "arbitrary")))(q, k, k, v, v)


def _att_bwd(q, k, v, do, lse, dlt, seq):
    ng, t, w = q.shape
    nblk = t // ATT_BLOCK
    scale = ATT_D ** -0.5
    cur = pl.BlockSpec((None, ATT_BLOCK, w), lambda g, n: (g, n, 0))
    nxt = pl.BlockSpec((None, ATT_BLOCK, w), lambda g, n: (g, jnp.minimum(n + 1, nblk - 1), 0))

    def body(qc_ref, qn_ref, k_ref, v_ref, doc_ref, don_ref, lsec_ref, lsen_ref, dltc_ref, dltn_ref,
             dq_ref, dk_ref, dv_ref, carry):
        g, n = pl.program_id(0), pl.program_id(1)
        nbs = _blocks_per_seq(g, seq)

        @pl.when((n % nbs) == 0)
        def _():
            carry[...] = jnp.zeros_like(carry)

        has_next = (((n + 1) % nbs) != 0) & (n + 1 < nblk)
        mcur, mprev = _att_masks()
        mnext = mprev & has_next
        for h in range(ATT_SLOTS):
            sl = slice(h * ATT_D, (h + 1) * ATT_D)
            kh, vh = k_ref[:, sl], v_ref[:, sl]
            qc, doc = qc_ref[:, sl], doc_ref[:, sl]
            p = jnp.where(mcur, jnp.exp(_dot_nt(qc, kh) * scale - lsec_ref[:, sl]), 0.0)
            dsc = (p * (_dot_nt(doc, vh) - dltc_ref[:, sl]) * scale).astype(BF16)
            dq_ref[:, sl] = carry[:, sl] + _dot(dsc, kh)
            qn, don = qn_ref[:, sl], don_ref[:, sl]
            pn = jnp.where(mnext, jnp.exp(_dot_nt(qn, kh) * scale - lsen_ref[:, sl]), 0.0)
            dsn = (pn * (_dot_nt(don, vh) - dltn_ref[:, sl]) * scale).astype(BF16)
            carry[:, sl] = _dot(dsn, kh)
            dk_ref[:, sl] = _dot_tn(dsc, qc) + _dot_tn(dsn, qn)
            dv_ref[:, sl] = _dot_tn(p.astype(BF16), doc) + _dot_tn(pn.astype(BF16), don)

    return pl.pallas_call(
        body, name="att_bwd", grid=(ng, nblk), in_specs=[cur, nxt, cur, cur, cur, nxt, cur, nxt, cur, nxt],
        out_specs=[cur, cur, cur], out_shape=[SDS((ng, t, w), F32)] * 3,
        scratch_shapes=[pltpu.VMEM((ATT_BLOCK, w), F32)],
        compiler_params=_cparams(("parallel", "arbitrary")))(q, q, k, v, do, do, lse, lse, dlt, dlt)


def _merge_weights(lse_refs):
    ls = [r[...] for r in lse_refs]
    m = jnp.maximum(jnp.maximum(ls[0], ls[1]), ls[2])
    es = [jnp.exp(v - m) for v in ls]
    den = es[0] + es[1] + es[2]
    return [e / den for e in es]


def _merge_fwd(o, lse):
    ng, t, w = o.shape
    tm = 512

    def fn(i, o0, o1, o2, l0, l1, l2):
        ws = _merge_weights((l0, l1, l2))
        return [ws[0] * o0[...] + ws[1] * o1[...] + ws[2] * o2[...]]

    ins = [(o, _gs(g, tm, w)) for g in range(ng)] + [(lse, _gs(g, tm, w)) for g in range(ng)]
    return _rw("att_merge_fwd", fn, t // tm, ins, [(SDS((t, w), BF16), _rs(tm, w))])[0]


def _merge_bwd(o, lse, datt):
    ng, t, w = o.shape
    tm = 512

    def fn(i, o0, o1, o2, l0, l1, l2, d_ref):
        ws = _merge_weights((l0, l1, l2))
        ov = [o0[...], o1[...], o2[...]]
        dv = d_ref[...]
        att = ws[0] * ov[0] + ws[1] * ov[1] + ws[2] * ov[2]
        dot = _group_bcast(dv * att, ATT_D, lambda p: jnp.sum(p, axis=-1, keepdims=True))
        return [jnp.stack([wg * dv for wg in ws]), jnp.stack([wg * dot for wg in ws])]

    ins = ([(o, _gs(g, tm, w)) for g in range(ng)] + [(lse, _gs(g, tm, w)) for g in range(ng)]
           + [(datt, _rs(tm, w))])
    full = pl.BlockSpec((ng, tm, w), lambda i: (0, i, 0))
    return _rw("att_merge_bwd", fn, t // tm, ins, [(SDS((ng, t, w), BF16), full), (SDS((ng, t, w), F32), full)])


def _to_groups(a, nb, seq):
    outs = []
    for g, r in enumerate(ATT_DILATIONS):
        p = a[:, g * ATT_W:(g + 1) * ATT_W].reshape(nb, seq // r, r, ATT_W)
        outs.append(p.transpose(0, 2, 1, 3).reshape(nb * seq, ATT_W))
    return jnp.stack(outs)


def _from_groups(a, nb, seq):
    outs = []
    for g, r in enumerate(ATT_DILATIONS):
        p = a[g].reshape(nb, r, seq // r, ATT_W)
        outs.append(p.transpose(0, 2, 1, 3).reshape(nb * seq, ATT_W))
    return outs


def _mix_fwd(gate_logits, b_gate, y_ssm, y_att):
    t, d = y_ssm.shape
    tm = 512

    def fn(i, g0_ref, g1_ref, b0_ref, b1_ref, ys_ref, ya_ref):
        g0 = _sigmoid(g0_ref[...] + b0_ref[...])
        g1 = _sigmoid(g1_ref[...] + b1_ref[...])
        return [g0 * ys_ref[...] + g1 * ya_ref[...]]

    b_spec = lambda cb: pl.BlockSpec((1, d), lambda i: (0, cb))
    return _rw("mix_fwd", fn, t // tm,
               [(gate_logits, _rs(tm, d, 0)), (gate_logits, _rs(tm, d, 1)), (b_gate, b_spec(0)), (b_gate, b_spec(1)),
                (y_ssm, _rs(tm, d)), (y_att, _rs(tm, d))],
               [(SDS((t, d), BF16), _rs(tm, d))])[0]


def _mix_bwd(gate_logits, b_gate, y_ssm, y_att, dmixed):
    t, d = y_ssm.shape
    tm = 256

    def fn(i, g0_ref, g1_ref, b0_ref, b1_ref, ys_ref, ya_ref, dm_ref):
        g0 = _sigmoid(g0_ref[...] + b0_ref[...])
        g1 = _sigmoid(g1_ref[...] + b1_ref[...])
        dm = dm_ref[...]
        dg = jnp.concatenate([dm * ys_ref[...] * g0 * (1.0 - g0), dm * ya_ref[...] * g1 * (1.0 - g1)], axis=-1)
        return [dm * g0, dm * g1, dg, _colsum(dg)]

    b_spec = lambda cb: pl.BlockSpec((1, d), lambda i: (0, cb))
    return _rw("mix_bwd", fn, t // tm,
               [(gate_logits, _rs(tm, d, 0)), (gate_logits, _rs(tm, d, 1)), (b_gate, b_spec(0)), (b_gate, b_spec(1)),
                (y_ssm, _rs(tm, d)), (y_att, _rs(tm, d)), (dmixed, _rs(tm, d))],
               [(SDS((t, d), BF16), _rs(tm, d)), (SDS((t, d), BF16), _rs(tm, d)),
                (SDS((t, 2 * d), BF16), _rs(tm, 2 * d)), (SDS((1, 2 * d), F32), _fs((1, 2 * d)))], n_acc=1)


def _swiglu_fwd(gt, up):
    t, f = gt.shape
    tm = 256

    def fn(i, g_ref, u_ref):
        gv = g_ref[...]
        return [gv * _sigmoid(gv) * u_ref[...]]

    return _rw("swiglu_fwd", fn, t // tm, [(gt, _rs(tm, f)), (up, _rs(tm, f))], [(SDS((t, f), BF16), _rs(tm, f))])[0]


def _swiglu_bwd(gt, up, dact):
    t, f = gt.shape
    tm = 256

    def fn(i, g_ref, u_ref, d_ref):
        gv, dv = g_ref[...], d_ref[...]
        sg = _sigmoid(gv)
        return [dv * u_ref[...] * sg * (1.0 + gv * (1.0 - sg)), dv * gv * sg]

    return _rw("swiglu_bwd", fn, t // tm, [(gt, _rs(tm, f)), (up, _rs(tm, f)), (dact, _rs(tm, f))],
               [(SDS((t, f), BF16), _rs(tm, f))] * 2)


def _adamw(w, g, m, v, name):
    r, c = w.shape
    tr = _row_tile(r, max(8, 400_000 // c))
    c1 = 1.0 / (1.0 - ADAM_B1 ** ADAM_STEP)
    c2 = 1.0 / (1.0 - ADAM_B2 ** ADAM_STEP)

    def fn(i, w_ref, g_ref, m_ref, v_ref):
        gv = g_ref[...]
        mn = ADAM_B1 * m_ref[...] + (1.0 - ADAM_B1) * gv
        vn = ADAM_B2 * v_ref[...] + (1.0 - ADAM_B2) * (gv * gv)
        delta = -ADAM_LR * ((mn * c1) / (jnp.sqrt(vn * c2) + ADAM_EPS) + ADAM_WD * w_ref[...])
        return [delta, mn, vn]

    spec = pl.BlockSpec((tr, c), lambda i: (i, 0))
    return _rw(name, fn, r // tr, [(w, spec), (g, spec), (m, spec), (v, spec)], [(SDS((r, c), F32), spec)] * 3)


ANY = pl.BlockSpec(memory_space=pl.ANY)


def _place():
    x, y, c = lax.axis_index("x"), lax.axis_index("y"), lax.axis_index("c")
    chips = [(1 - x, y), (x, 1 - y), (1 - x, 1 - y)]
    return x, y, c, chips


def _remote(src, dst, ssem, rsem, to):
    return pltpu.make_async_remote_copy(src_ref=src, dst_ref=dst, send_sem=ssem, recv_sem=rsem, device_id=to,
                                        device_id_type=MESH)


def _gather_weights(wp):
    def body(w_ref, out_ref, ssem, rsem):
        x, y, c, chips = _place()
        me = 2 * x + y
        sib = (x, y, 1 - c)
        first = [_remote(w_ref.at[c], out_ref.at[me, c], ssem.at[j], rsem.at[j], (*chip, c))
                 for j, chip in enumerate(chips)]
        for cp in first:
            cp.start()
        passed = []
        for j, chip in enumerate(chips):
            ci = 2 * chip[0] + chip[1]
            _remote(w_ref.at[c], out_ref.at[ci, c], ssem.at[j], rsem.at[j], (*chip, c)).wait_recv()
            cp = _remote(out_ref.at[ci, c], out_ref.at[ci, c], ssem.at[3 + j], rsem.at[3 + j], sib)
            cp.start()
            passed.append(cp)
        for j, chip in enumerate(chips):
            ci = 2 * chip[0] + chip[1]
            _remote(out_ref.at[ci, 1 - c], out_ref.at[ci, 1 - c], ssem.at[3 + j], rsem.at[3 + j], sib).wait_recv()
        for cp in first + passed:
            cp.wait_send()

    return pl.pallas_call(
        body, name="gather_weights", in_specs=[ANY], out_specs=ANY,
        out_shape=SDS((N_CHIPS,) + wp.shape, wp.dtype),
        scratch_shapes=[pltpu.SemaphoreType.DMA((6,)), pltpu.SemaphoreType.DMA((6,))],
        compiler_params=pltpu.CompilerParams(has_side_effects=True))(wp)


def _swap_halves(g2):
    def body(g_ref, out_ref, ssem, rsem):
        x, y, c, _ = _place()
        cp = _remote(g_ref.at[1 - c], out_ref, ssem, rsem, (x, y, 1 - c))
        cp.start()
        cp.wait()

    return pl.pallas_call(
        body, name="swap_halves", in_specs=[ANY], out_specs=ANY, out_shape=SDS(g2.shape[1:], g2.dtype),
        scratch_shapes=[pltpu.SemaphoreType.DMA(()), pltpu.SemaphoreType.DMA(())],
        compiler_params=pltpu.CompilerParams(has_side_effects=True))(g2)


def _add_own_half(g2, other, c):
    _, nch, rows, w = g2.shape
    tr = _row_tile(rows, 512)
    nr = rows // tr

    def body(c_ref, a_ref, b_ref, o_ref):
        o_ref[...] = (a_ref[...].astype(F32) + b_ref[...].astype(F32)).astype(o_ref.dtype)

    grid_spec = pltpu.PrefetchScalarGridSpec(
        num_scalar_prefetch=1, grid=(nch, nr),
        in_specs=[pl.BlockSpec((None, None, tr, w), lambda k, i, c_ref: (c_ref[0], k, i, 0)),
                  pl.BlockSpec((None, tr, w), lambda k, i, c_ref: (k, i, 0))],
        out_specs=pl.BlockSpec((None, tr, w), lambda k, i, c_ref: (k, i, 0)))
    return pl.pallas_call(
        body, name="add_own_half", grid_spec=grid_spec, out_shape=SDS(other.shape, other.dtype),
        compiler_params=_cparams(("arbitrary", "arbitrary")))(jnp.reshape(c, (1,)).astype(jnp.int32), g2, other)


def _scatter_to_chips(p):
    def body(p_ref, q_ref, ssem, rsem):
        x, y, c, chips = _place()
        me = 2 * x + y
        sent = []
        for j, chip in enumerate(chips):
            ci = 2 * chip[0] + chip[1]
            cp = _remote(p_ref.at[ci], q_ref.at[me], ssem.at[j], rsem.at[j], (*chip, c))
            cp.start()
            sent.append(cp)
        for j, chip in enumerate(chips):
            ci = 2 * chip[0] + chip[1]
            _remote(p_ref.at[ci], q_ref.at[ci], ssem.at[j], rsem.at[j], (*chip, c)).wait_recv()
        for cp in sent:
            cp.wait_send()

    return pl.pallas_call(
        body, name="scatter_to_chips", in_specs=[ANY], out_specs=ANY, out_shape=SDS(p.shape, p.dtype),
        scratch_shapes=[pltpu.SemaphoreType.DMA((3,)), pltpu.SemaphoreType.DMA((3,))],
        compiler_params=pltpu.CompilerParams(has_side_effects=True))(p)


def _sum_chips(q):
    nch, rows, w = q.shape
    tr = _row_tile(rows, 512)

    def fn(i, q_ref):
        return [((q_ref[0].astype(F32) + q_ref[1].astype(F32)) + q_ref[2].astype(F32)) + q_ref[3].astype(F32)]

    return _rw("sum_chips", fn, rows // tr, [(q, pl.BlockSpec((nch, tr, w), lambda i: (0, i, 0)))],
               [(SDS((rows, w), F32), _rs(tr, w))])[0]


def _allreduce_small(v, name):
    rows, w = v.shape
    offsets = [(dx, dy, dc) for dx in (0, 1) for dy in (0, 1) for dc in (0, 1)][1:]

    def body(v_ref, o_ref, buf, ssem, rsem):
        x, y, c, _ = _place()
        flip = lambda p, d: 1 - p if d else p
        peers = [(flip(x, dx), flip(y, dy), flip(c, dc)) for dx, dy, dc in offsets]
        index = lambda p: 4 * p[0] + 2 * p[1] + p[2]
        me = index((x, y, c))
        buf[me] = v_ref[...]
        sent = [_remote(v_ref, buf.at[me], ssem.at[q], rsem.at[q], p) for q, p in enumerate(peers)]
        for cp in sent:
            cp.start()
        for q, p in enumerate(peers):
            _remote(v_ref, buf.at[index(p)], ssem.at[q], rsem.at[q], p).wait_recv()
        for cp in sent:
            cp.wait_send()
        acc = buf[0]
        for q in range(1, 8):
            acc = acc + buf[q]
        o_ref[...] = acc

    vm = pl.BlockSpec(memory_space=pltpu.VMEM)
    return pl.pallas_call(
        body, name=name, in_specs=[vm], out_specs=vm, out_shape=SDS((rows, w), F32),
        scratch_shapes=[pltpu.VMEM((8, rows, w), F32), pltpu.SemaphoreType.DMA((7,)), pltpu.SemaphoreType.DMA((7,))],
        compiler_params=pltpu.CompilerParams(has_side_effects=True))(v)


def _join_halves(h):
    def body(h_ref, out_ref, ssem, rsem):
        x, y, c, _ = _place()
        cp = _remote(h_ref, out_ref.at[c], ssem, rsem, (x, y, 1 - c))
        cp.start()
        _remote(h_ref, out_ref.at[1 - c], ssem, rsem, (x, y, 1 - c)).wait_recv()
        cp.wait_send()

    return pl.pallas_call(
        body, name="join_halves", in_specs=[ANY], out_specs=ANY, out_shape=SDS((2,) + h.shape, h.dtype),
        scratch_shapes=[pltpu.SemaphoreType.DMA(()), pltpu.SemaphoreType.DMA(())],
        compiler_params=pltpu.CompilerParams(has_side_effects=True))(h)


PACK_W = 1024
SHARDED = ("w_in", "w_ffn_gate", "w_ffn_up", "w_ssm_out", "w_att_out", "w_mix_out", "w_ffn_down")
COL_SHARDED = ("w_in", "w_ffn_gate", "w_ffn_up", "w_att_out")
SMALL = ("norm_mix", "b_gate", "conv_b", "dt_bias", "a_log", "d_skip", "ssm_norm", "norm_ffn", "norm_final")


PACK_ROW_ALIGN = 16


def _rows(n):
    return -(-n // (PACK_W * PACK_ROW_ALIGN)) * PACK_ROW_ALIGN


def _pack_rows(parts, total_rows):
    rows = []
    for p in parts:
        flat = p.reshape(-1)
        pad = _rows(flat.shape[0]) * PACK_W - flat.shape[0]
        if pad:
            flat = jnp.concatenate([flat, jnp.zeros((pad,), flat.dtype)])
        rows.append(flat.reshape(-1, PACK_W))
    used = sum(r.shape[0] for r in rows)
    if total_rows > used:
        rows.append(jnp.zeros((total_rows - used, PACK_W), rows[0].dtype))
    return jnp.concatenate(rows, axis=0)


def _padded_rows(n):
    return -(-n // 32) * 32


def _wire_name(name):
    return name + "_t" if name in COL_SHARDED else name


def _wire_shard(w, name):
    return w.T if name in COL_SHARDED else w


def _group_major(a, axis):
    gw = D_INNER // N_GROUPS
    take = lambda lo, n: lax.slice_in_dim(a, lo, lo + n, axis=axis)
    parts = []
    for g in range(N_GROUPS):
        parts += [take(g * gw, gw), take(D_INNER + g * D_STATE, D_STATE),
                  take(D_INNER + N_GROUPS * D_STATE + g * D_STATE, D_STATE)]
    return jnp.concatenate(parts, axis=axis)


def _group_major_inv(a, axis):
    gw = D_INNER // N_GROUPS
    take = lambda lo, n: lax.slice_in_dim(a, lo, lo + n, axis=axis)
    xs = [take(g * GROUP_W, gw) for g in range(N_GROUPS)]
    bs = [take(g * GROUP_W + gw, D_STATE) for g in range(N_GROUPS)]
    cs = [take(g * GROUP_W + gw + D_STATE, D_STATE) for g in range(N_GROUPS)]
    return jnp.concatenate(xs + bs + cs, axis=axis)


def _local_step(x, target, wts):
    nb, seq, d = x.shape
    t = nb * seq
    x = x.reshape(t, d)
    target = target.reshape(t, d)
    hg = HEADS_PER_GROUP

    w_in_t = wts["w_in_t"]
    o1, o2, o3, o4 = D_INNER, D_INNER + CONV_DIM, D_INNER + CONV_DIM + N_HEADS, D_INNER + CONV_DIM + N_HEADS + QKV_DIM
    w_z = w_in_t[:o1]
    w_xbc = _group_major(w_in_t[o1:o2], 0)
    w_dt = jnp.pad(w_in_t[o2:o3], ((0, DT_PAD - N_HEADS), (0, 0)))
    w_qkv = w_in_t[o3:o4]
    w_gate = w_in_t[o4:]
    conv_w = _group_major(wts["conv_w"], 1)
    conv_b = _group_major(wts["conv_b"], 1)

    def per_group_row(p):
        return p.reshape(N_GROUPS, 1, hg)

    def per_group_col(p):
        return p.reshape(N_GROUPS, hg, 1)

    a_neg = -jnp.exp(wts["a_log"])
    bias_r, bias_c = per_group_row(wts["dt_bias"]), per_group_col(wts["dt_bias"])
    a_r, a_c = per_group_row(a_neg), per_group_col(a_neg)
    dskip_r = per_group_row(wts["d_skip"])
    cos, sin = _rope_tables(seq)

    h = _rms_fwd(x, wts["norm_mix"], "rms_mix_fwd")
    z = _mm(h, w_z, "nt", F32, "proj_z")
    xbc = _mm(h, w_xbc, "nt", F32, "proj_xbc")
    dt_raw = _mm(h, w_dt, "nt", F32, "proj_dt")
    qkv = _mm(h, w_qkv, "nt", F32, "proj_qkv")
    gate_logits = _mm(h, w_gate, "nt", F32, "proj_gate")

    xc = _conv_fwd(xbc, conv_w, conv_b, seq)
    dtr = dt_raw[:, :N_HEADS].reshape(t, N_GROUPS, hg).transpose(1, 0, 2)
    dtrt = dt_raw[:, :N_HEADS].reshape(nb, seq, N_GROUPS, hg).transpose(2, 0, 3, 1)
    y, states = _ssd_fwd(xc, dtr, dtrt, bias_r, bias_c, a_r, a_c, dskip_r, nb, seq)
    yn = _gate_norm_fwd(y, z, wts["ssm_norm"])
    y_ssm = _mm(yn, wts["w_ssm_out"], "nn", F32, "ssm_out")

    q_r, k_r, v_b = _rope_fwd(qkv, cos, sin, seq)
    qg, kg, vg = _to_groups(q_r, nb, seq), _to_groups(k_r, nb, seq), _to_groups(v_b, nb, seq)
    o_g, lse_g = _att_fwd(qg, kg, vg, seq)
    o_t = jnp.stack(_from_groups(o_g, nb, seq))
    lse_t = jnp.stack(_from_groups(lse_g, nb, seq))
    att = _merge_fwd(o_t, lse_t)
    y_att = _mm(att, wts["w_att_out_t"], "nt", F32, "att_out")

    mixed = _mix_fwd(gate_logits, wts["b_gate"], y_ssm, y_att)
    x1 = _mm(mixed, wts["w_mix_out"], "nn", F32, "mix_out", add=x)
    h2 = _rms_fwd(x1, wts["norm_ffn"], "rms_ffn_fwd")
    gt = _mm(h2, wts["w_ffn_gate_t"], "nt", F32, "ffn_gate")
    up = _mm(h2, wts["w_ffn_up_t"], "nt", F32, "ffn_up")
    act = _swiglu_fwd(gt, up)
    x2 = _mm(act, wts["w_ffn_down"], "nn", F32, "ffn_down", add=x1)

    g = {}
    dx2, dx2_b, g["norm_final"], loss = _final_fwd_bwd(x2, target, wts["norm_final"].reshape(1, d))
    dact = _mm(dx2_b, wts["w_ffn_down"], "nt", F32, "d_act")
    g["w_ffn_down"] = _mm(act, dx2_b, "tn", BF16, "g_ffn_down")
    dgt, dup = _swiglu_bwd(gt, up, dact)
    g["w_ffn_gate_t"] = _mm(dgt, h2, "tn", BF16, "g_ffn_gate")
    g["w_ffn_up_t"] = _mm(dup, h2, "tn", BF16, "g_ffn_up")
    dh2 = _mm(dgt, wts["w_ffn_gate_t"], "nn", F32, "d_h2_gate")
    dh2 = _mm(dup, wts["w_ffn_up_t"], "nn", F32, "d_h2_up", add=dh2)
    dx1, dx1_b, g["norm_ffn"] = _rms_bwd(x1, dh2, wts["norm_ffn"], dx2, "rms_ffn_bwd")

    dmixed = _mm(dx1_b, wts["w_mix_out"], "nt", F32, "d_mixed")
    g["w_mix_out"] = _mm(mixed, dx1_b, "tn", BF16, "g_mix_out")
    dy_ssm, dy_att, dgate, g["b_gate"] = _mix_bwd(gate_logits, wts["b_gate"], y_ssm, y_att, dmixed)

    datt = _mm(dy_att, wts["w_att_out_t"], "nn", F32, "d_att")
    g["w_att_out_t"] = _mm(dy_att, att, "tn", BF16, "g_att_out")
    do_t, dlt_t = _merge_bwd(o_t, lse_t, datt)
    do_g = jnp.stack([_to_groups_one(do_t[i], i, nb, seq) for i in range(3)])
    dlt_g = jnp.stack([_to_groups_one(dlt_t[i], i, nb, seq) for i in range(3)])
    dq_g, dk_g, dv_g = _att_bwd(qg, kg, vg, do_g, lse_g, dlt_g, seq)
    dq = jnp.concatenate(_from_groups(dq_g, nb, seq), axis=-1)
    dk = jnp.concatenate(_from_groups(dk_g, nb, seq), axis=-1)
    dv = jnp.concatenate(_from_groups(dv_g, nb, seq), axis=-1)
    dqkv = _rope_bwd(dq, dk, dv, cos, sin, seq)

    dyn = _mm(dy_ssm, wts["w_ssm_out"], "nt", F32, "d_yn")
    g["w_ssm_out"] = _mm(yn, dy_ssm, "tn", BF16, "g_ssm_out")
    dy, dz, g["ssm_norm"] = _gate_norm_bwd(y, z, wts["ssm_norm"], dyn)
    dxc, ddtr, g_bias, g_alog, g_dskip = _ssd_bwd(xc, dtr, dtrt, bias_r, bias_c, a_r, a_c, dskip_r, states, dy,
                                                   nb, seq)
    g["dt_bias"] = g_bias.reshape(1, N_HEADS)
    g["a_log"] = g_alog.reshape(1, N_HEADS)
    g["d_skip"] = g_dskip.reshape(1, N_HEADS)
    dpre, g_conv_w, g_conv_b = _conv_bwd_pre(xbc, conv_w, conv_b, dxc, seq)
    g["conv_w"] = _group_major_inv(g_conv_w, 1)
    g["conv_b"] = _group_major_inv(g_conv_b, 1)
    dxbc = _conv_bwd_in(dpre, conv_w, seq)
    ddt = jnp.pad(ddtr.transpose(1, 0, 2).reshape(t, N_HEADS), ((0, 0), (0, DT_PAD - N_HEADS))).astype(BF16)

    dh = _mm(dz, w_z, "nn", F32, "d_h_z")
    dh = _mm(dxbc, w_xbc, "nn", F32, "d_h_xbc", add=dh)
    dh = _mm(ddt, w_dt, "nn", F32, "d_h_dt", add=dh)
    dh = _mm(dqkv, w_qkv, "nn", F32, "d_h_qkv", add=dh)
    dh = _mm(dgate, w_gate, "nn", F32, "d_h_gate", add=dh)
    g["w_in_t"] = jnp.concatenate([
        _mm(dz, h, "tn", BF16, "g_in_z"),
        _group_major_inv(_mm(dxbc, h, "tn", BF16, "g_in_xbc"), 0),
        _mm(ddt, h, "tn", BF16, "g_in_dt")[:N_HEADS],
        _mm(dqkv, h, "tn", BF16, "g_in_qkv"),
        _mm(dgate, h, "tn", BF16, "g_in_gate")], axis=0)
    dx, _, g["norm_mix"] = _rms_bwd(x, dh, wts["norm_mix"], dx1, "rms_mix_bwd")
    return loss[0, 0], dx.reshape(nb, seq, d), g


def _to_groups_one(a, g, nb, seq):
    r = ATT_DILATIONS[g]
    return a.reshape(nb, seq // r, r, ATT_W).transpose(0, 2, 1, 3).reshape(nb * seq, ATT_W)


def kernel(x, norm_mix, w_in, b_gate, conv_w, conv_b, dt_bias, a_log, d_skip, ssm_norm, w_ssm_out, w_att_out, w_mix_out, norm_ffn, w_ffn_gate, w_ffn_up, w_ffn_down, norm_final, loss_target, m_norm_mix, m_w_in, m_b_gate, m_conv_w, m_conv_b, m_dt_bias, m_a_log, m_d_skip, m_ssm_norm, m_w_ssm_out, m_w_att_out, m_w_mix_out, m_norm_ffn, m_w_ffn_gate, m_w_ffn_up, m_w_ffn_down, m_norm_final, v_norm_mix, v_w_in, v_b_gate, v_conv_w, v_conv_b, v_dt_bias, v_a_log, v_d_skip, v_ssm_norm, v_w_ssm_out, v_w_att_out, v_w_mix_out, v_norm_ffn, v_w_ffn_gate, v_w_ffn_up, v_w_ffn_down, v_norm_final):
    names = ("norm_mix", "w_in", "b_gate", "conv_w", "conv_b", "dt_bias", "a_log", "d_skip", "ssm_norm", "w_ssm_out",
             "w_att_out", "w_mix_out", "norm_ffn", "w_ffn_gate", "w_ffn_up", "w_ffn_down", "norm_final")
    w_loc = dict(zip(names, (norm_mix, w_in, b_gate, conv_w, conv_b, dt_bias, a_log, d_skip, ssm_norm, w_ssm_out,
                             w_att_out, w_mix_out, norm_ffn, w_ffn_gate, w_ffn_up, w_ffn_down, norm_final)))
    m_loc = dict(zip(names, (m_norm_mix, m_w_in, m_b_gate, m_conv_w, m_conv_b, m_dt_bias, m_a_log, m_d_skip,
                             m_ssm_norm, m_w_ssm_out, m_w_att_out, m_w_mix_out, m_norm_ffn, m_w_ffn_gate,
                             m_w_ffn_up, m_w_ffn_down, m_norm_final)))
    v_loc = dict(zip(names, (v_norm_mix, v_w_in, v_b_gate, v_conv_w, v_conv_b, v_dt_bias, v_a_log, v_d_skip,
                             v_ssm_norm, v_w_ssm_out, v_w_att_out, v_w_mix_out, v_norm_ffn, v_w_ffn_gate,
                             v_w_ffn_up, v_w_ffn_down, v_norm_final)))
    two_d = lambda a: a.reshape(a.shape[-2:]) if a.ndim >= 2 else a.reshape(1, -1)
    w2 = {n: two_d(a) for n, a in w_loc.items()}
    chip = 2 * lax.axis_index("x") + lax.axis_index("y")
    c = lax.axis_index("c")

    wire_shapes = {n: _wire_shard(w2[n], n).shape for n in SHARDED}
    true_rows = {n: wire_shapes[n][0] * wire_shapes[n][1] // PACK_W for n in SHARDED}
    seg_rows = {n: _rows(wire_shapes[n][0] * wire_shapes[n][1]) for n in SHARDED}
    w_rows = _padded_rows(sum(seg_rows.values()))
    wp = _pack_rows([_wire_shard(w2[n], n).astype(BF16) for n in SHARDED], w_rows).reshape(2, w_rows // 2, PACK_W)
    wg = lax.dynamic_update_index_in_dim(_gather_weights(wp), wp, chip, 0).reshape(N_CHIPS, w_rows, PACK_W)
    full = {}
    off = 0
    for n in SHARDED:
        rows, cols = wire_shapes[n]
        full[_wire_name(n)] = wg[:, off:off + true_rows[n]].reshape(N_CHIPS * rows, cols)
        off += seg_rows[n]
    for n in SMALL:
        full[n] = w2[n]

    n_conv = w2["conv_w"].shape[1]
    placed = lax.dynamic_update_slice_in_dim(jnp.zeros((CONV_K, N_CHIPS * n_conv), F32), w2["conv_w"], chip * n_conv, 1)
    placed = jnp.where(c == 0, placed, 0.0)
    full["conv_w"] = _allreduce_small(_pack_rows([placed], _rows(int(placed.size))), "gather_conv_w").reshape(
        -1)[:placed.size].reshape(placed.shape)

    loss_sum, grad_x, g_full = _local_step(x, loss_target, full)
    loss = lax.psum(loss_sum, ("x", "y", "c"))

    g_shard = {}
    small_names = SMALL + ("conv_w",)
    small_flat = jnp.concatenate([g_full[n].reshape(-1) for n in small_names])
    small = _allreduce_small(_pack_rows([small_flat], _rows(int(small_flat.size))), "allreduce_small").reshape(-1)
    off = 0
    for n in small_names:
        size = int(g_full[n].size)
        g_shard[n] = small[off:off + size].reshape(g_full[n].shape)
        off += size
    g_shard["conv_w"] = lax.dynamic_slice_in_dim(g_shard["conv_w"], chip * n_conv, n_conv, 1)

    sections = [_pack_rows([g_full[_wire_name(n)].reshape(N_CHIPS, true_rows[n], PACK_W)[k] for n in SHARDED], w_rows)
                for k in range(N_CHIPS)]
    g2 = jnp.stack(sections).reshape(N_CHIPS, 2, w_rows // 2, PACK_W).transpose(1, 0, 2, 3)
    chip_sum = _add_own_half(g2, _swap_halves(g2), c)
    own = lax.dynamic_index_in_dim(chip_sum, chip, 0, keepdims=False)
    by_source = lax.dynamic_update_index_in_dim(_scatter_to_chips(chip_sum), own, chip, 0)
    half = _sum_chips(by_source)
    reduced = lax.dynamic_update_index_in_dim(_join_halves(half), half, c, 0).reshape(w_rows, PACK_W)
    off = 0
    for n in SHARDED:
        wire = reduced[off:off + true_rows[n]].reshape(wire_shapes[n])
        g_shard[n] = wire.T if n in COL_SHARDED else wire
        off += seg_rows[n]

    grads, deltas, new_m, new_v = [], [], [], []
    for n in names:
        shape = w_loc[n].shape
        d_, m_, v_ = _adamw(w2[n], g_shard[n], two_d(m_loc[n]), two_d(v_loc[n]), "adamw_" + n)
        grads.append(g_shard[n].reshape(shape))
        deltas.append(d_.reshape(shape))
        new_m.append(m_.reshape(shape))
        new_v.append(v_.reshape(shape))
    return (loss, grad_x, *grads, *deltas, *new_m, *new_v)
```

```python
import functools
import math

import jax
import jax.numpy as jnp
from jax import lax
from jax.experimental import pallas as pl
from jax.experimental.pallas import tpu as pltpu

F32 = jnp.float32
BF16 = jnp.bfloat16
SDS = jax.ShapeDtypeStruct
MESH = pl.DeviceIdType.MESH

D_MODEL = 1024
D_INNER = 2048
N_HEADS = 32
HEAD_P = 64
N_GROUPS = 4
HEADS_PER_GROUP = N_HEADS // N_GROUPS
D_STATE = 128
CONV_K = 4
CHUNK = 128
CONV_DIM = D_INNER + 2 * N_GROUPS * D_STATE
GROUP_W = D_INNER // N_GROUPS + 2 * D_STATE
ATT_HEADS = 12
ATT_D = 128
ATT_SLOTS = 4
ATT_W = ATT_SLOTS * ATT_D
ATT_DILATIONS = (1, 4, 16)
ATT_BLOCK = 128
QKV_DIM = 3 * ATT_HEADS * ATT_D
D_FF = 2816
DT_PAD = 128
ROPE_THETA = 10000.0
EPS = 1e-6
N_CHIPS = 4
LANES = 128

ADAM_LR = 0.001
ADAM_B1 = 0.9
ADAM_B2 = 0.999
ADAM_EPS = 1e-08
ADAM_WD = 0.01
ADAM_STEP = 10

VMEM_LIMIT = 48 * 1024 * 1024


def _cparams(semantics):
    return pltpu.CompilerParams(dimension_semantics=semantics, vmem_limit_bytes=VMEM_LIMIT)


def _pick(n, cap):
    best = None
    for t in range(LANES, min(n, cap) + 1, LANES):
        if n % t == 0:
            best = t
    return best or n


def _row_tile(rows, cap):
    best = None
    for t in range(8, min(rows, cap) + 1, 8):
        if rows % t == 0:
            best = t
    return best or rows


def _sigmoid(x):
    return 1.0 / (1.0 + jnp.exp(-x))


def _softplus(x):
    return jnp.maximum(x, 0.0) + jnp.log(1.0 + jnp.exp(-jnp.abs(x)))


def _dot(a, b):
    return jnp.dot(a, b, preferred_element_type=F32)


def _dot_nt(a, b):
    return lax.dot_general(a, b, (((1,), (1,)), ((), ())), preferred_element_type=F32)


def _dot_tn(a, b):
    return lax.dot_general(a, b, (((0,), (0,)), ((), ())), preferred_element_type=F32)


def _mm(a, b, mode, out_dtype, name, add=None):
    if mode == "nn":
        (m, k), (_, n) = a.shape, b.shape
    elif mode == "nt":
        (m, k), (n, _) = a.shape, b.shape
    else:
        (k, m), (_, n) = a.shape, b.shape
    tm, tn = _pick(m, 512), _pick(n, 1536)
    tk = k if k <= 2048 else _pick(k, 2048)
    nk = k // tk
    dims = {"nn": ((1,), (0,)), "nt": ((1,), (1,)), "tn": ((0,), (0,))}[mode]

    def partial_product(a_ref, b_ref):
        return lax.dot_general(a_ref[...].astype(BF16), b_ref[...].astype(BF16), (dims, ((), ())),
                               preferred_element_type=F32)

    def body(*refs):
        a_ref, b_ref = refs[:2]
        c_ref = refs[2] if add is not None else None
        o_ref = refs[3] if add is not None else refs[2]

        def finish(r):
            if add is not None:
                r = r + c_ref[...].astype(F32)
            o_ref[...] = r.astype(out_dtype)

        if nk == 1:
            finish(partial_product(a_ref, b_ref))
            return
        acc = refs[-1]
        kk = pl.program_id(2)

        @pl.when(kk == 0)
        def _():
            acc[...] = partial_product(a_ref, b_ref)

        @pl.when((kk > 0) & (kk < nk - 1))
        def _():
            acc[...] += partial_product(a_ref, b_ref)

        @pl.when(kk == nk - 1)
        def _():
            finish(acc[...] + partial_product(a_ref, b_ref))

    a_spec = {"nn": pl.BlockSpec((tm, tk), lambda j, i, q: (i, q)),
              "nt": pl.BlockSpec((tm, tk), lambda j, i, q: (i, q)),
              "tn": pl.BlockSpec((tk, tm), lambda j, i, q: (q, i))}[mode]
    b_spec = {"nn": pl.BlockSpec((tk, tn), lambda j, i, q: (q, j)),
              "nt": pl.BlockSpec((tn, tk), lambda j, i, q: (j, q)),
              "tn": pl.BlockSpec((tk, tn), lambda j, i, q: (q, j))}[mode]
    o_spec = pl.BlockSpec((tm, tn), lambda j, i, q: (i, j))
    ins, specs = [a, b], [a_spec, b_spec]
    if add is not None:
        ins.append(add)
        specs.append(o_spec)
    return pl.pallas_call(
        body, name=name, grid=(n // tn, m // tm, nk), in_specs=specs, out_specs=o_spec,
        out_shape=SDS((m, n), out_dtype), scratch_shapes=[pltpu.VMEM((tm, tn), F32)] if nk > 1 else [],
        compiler_params=_cparams(("parallel", "parallel", "arbitrary")))(*ins)


def _rw(name, fn, nsteps, ins, outs, n_acc=0):
    n_in, n_out = len(ins), len(outs)

    def body(*refs):
        i = pl.program_id(0)
        vals = fn(i, *refs[:n_in])
        for q, (r, v) in enumerate(zip(refs[n_in:], vals)):
            if q < n_out - n_acc:
                r[...] = v.astype(r.dtype)
            else:
                @pl.when(i == 0)
                def _(r=r):
                    r[...] = jnp.zeros_like(r)

                r[...] += v

    return pl.pallas_call(
        body, name=name, grid=(nsteps,), in_specs=[s for _, s in ins], out_specs=[s for _, s in outs],
        out_shape=[o for o, _ in outs], compiler_params=_cparams(("arbitrary",)))(*[a for a, _ in ins])


def _rs(tm, w, cb=0):
    return pl.BlockSpec((tm, w), lambda i: (i, cb))


def _fs(shape):
    nd = len(shape)
    return pl.BlockSpec(shape, lambda i: (0,) * nd)


def _colsum(v):
    return jnp.sum(v, axis=0, keepdims=True)


def _rms_fwd(x, g, name):
    t, d = x.shape
    tm = 512

    def fn(i, x_ref, g_ref):
        xv = x_ref[...]
        r = lax.rsqrt(jnp.mean(xv * xv, axis=-1, keepdims=True) + EPS)
        return [xv * r * g_ref[...]]

    return _rw(name, fn, t // tm, [(x, _rs(tm, d)), (g, _fs((1, d)))], [(SDS((t, d), BF16), _rs(tm, d))])[0]


def _rms_bwd(x, dh, g, dres, name):
    t, d = x.shape
    tm = 512

    def fn(i, x_ref, dh_ref, g_ref, dres_ref):
        xv = x_ref[...]
        r = lax.rsqrt(jnp.mean(xv * xv, axis=-1, keepdims=True) + EPS)
        xhat = xv * r
        dhv = dh_ref[...]
        dxhat = dhv * g_ref[...]
        dx = dres_ref[...] + r * (dxhat - xhat * jnp.mean(dxhat * xhat, axis=-1, keepdims=True))
        return [dx, dx, _colsum(dhv * xhat)]

    return _rw(name, fn, t // tm,
               [(x, _rs(tm, d)), (dh, _rs(tm, d)), (g, _fs((1, d))), (dres, _rs(tm, d))],
               [(SDS((t, d), F32), _rs(tm, d)), (SDS((t, d), BF16), _rs(tm, d)), (SDS((1, d), F32), _fs((1, d)))],
               n_acc=1)


def _final_fwd_bwd(x2, target, g):
    t, d = x2.shape
    tm = 512

    def fn(i, x_ref, t_ref, g_ref):
        xv = x_ref[...]
        gv = g_ref[...]
        r = lax.rsqrt(jnp.mean(xv * xv, axis=-1, keepdims=True) + EPS)
        xhat = xv * r
        diff = xhat * gv - t_ref[...]
        lsum = 0.5 * jnp.sum(jnp.sum(diff * diff, axis=-1, keepdims=True) * (1.0 / d), axis=0, keepdims=True)
        dy = diff * (1.0 / d)
        dxhat = dy * gv
        dx = r * (dxhat - xhat * jnp.mean(dxhat * xhat, axis=-1, keepdims=True))
        return [dx, dx, _colsum(dy * xhat), lsum]

    return _rw("final_norm_loss", fn, t // tm,
               [(x2, _rs(tm, d)), (target, _rs(tm, d)), (g, _fs((1, d)))],
               [(SDS((t, d), F32), _rs(tm, d)), (SDS((t, d), BF16), _rs(tm, d)), (SDS((1, d), F32), _fs((1, d))),
                (SDS((1, 1), F32), _fs((1, 1)))], n_acc=2)


CONV_TS = 512
CONV_HALO = 8


def _conv_specs(seq, c):
    ts, tc = CONV_TS, GROUP_W
    hb = ts // CONV_HALO
    u_spec = pl.BlockSpec((ts, tc), lambda j, i: (i, j))
    prev_spec = pl.BlockSpec((CONV_HALO, tc), lambda j, i: (jnp.maximum(i * hb - 1, 0), j))
    w_spec = pl.BlockSpec((CONV_K, tc), lambda j, i: (0, j))
    b_spec = pl.BlockSpec((1, tc), lambda j, i: (0, j))
    return u_spec, prev_spec, w_spec, b_spec


def _conv_pre(i, seq, u_ref, prev_ref, w_ref, b_ref, ext):
    ts = CONV_TS
    first = (i % (seq // ts)) == 0
    ext[0:CONV_HALO, :] = jnp.where(first, 0.0, prev_ref[...])
    ext[CONV_HALO:, :] = u_ref[...]
    acc = jnp.broadcast_to(b_ref[...], u_ref.shape)
    for q in range(CONV_K):
        acc = acc + w_ref[q:q + 1, :] * ext[pl.ds(CONV_HALO - CONV_K + 1 + q, ts), :]
    return acc


def _conv_fwd(u, w, b, seq):
    t, c = u.shape
    ts, tc = CONV_TS, GROUP_W
    u_spec, prev_spec, w_spec, b_spec = _conv_specs(seq, c)

    def body(u_ref, prev_ref, w_ref, b_ref, o_ref, ext):
        pre = _conv_pre(pl.program_id(1), seq, u_ref, prev_ref, w_ref, b_ref, ext)
        o_ref[...] = pre * _sigmoid(pre)

    return pl.pallas_call(
        body, name="conv_fwd", grid=(c // tc, t // ts), in_specs=[u_spec, prev_spec, w_spec, b_spec],
        out_specs=u_spec, out_shape=SDS((t, c), F32), scratch_shapes=[pltpu.VMEM((ts + CONV_HALO, tc), F32)],
        compiler_params=_cparams(("parallel", "arbitrary")))(u, u, w, b)


def _conv_bwd_pre(u, w, b, dxc, seq):
    t, c = u.shape
    ts, tc = CONV_TS, GROUP_W
    u_spec, prev_spec, w_spec, b_spec = _conv_specs(seq, c)

    def body(u_ref, prev_ref, w_ref, b_ref, d_ref, dpre_ref, dw_ref, db_ref, ext):
        i = pl.program_id(1)
        pre = _conv_pre(i, seq, u_ref, prev_ref, w_ref, b_ref, ext)
        sg = _sigmoid(pre)
        dpre = d_ref[...] * sg * (1.0 + pre * (1.0 - sg))
        dpre_ref[...] = dpre

        @pl.when(i == 0)
        def _():
            dw_ref[...] = jnp.zeros_like(dw_ref)
            db_ref[...] = jnp.zeros_like(db_ref)

        db_ref[...] += _colsum(dpre)
        for q in range(CONV_K):
            dw_ref[q:q + 1, :] += _colsum(dpre * ext[pl.ds(CONV_HALO - CONV_K + 1 + q, ts), :])

    return pl.pallas_call(
        body, name="conv_bwd_pre", grid=(c // tc, t // ts),
        in_specs=[u_spec, prev_spec, w_spec, b_spec, u_spec], out_specs=[u_spec, w_spec, b_spec],
        out_shape=[SDS((t, c), F32), SDS((CONV_K, c), F32), SDS((1, c), F32)],
        scratch_shapes=[pltpu.VMEM((ts + CONV_HALO, tc), F32)],
        compiler_params=_cparams(("parallel", "arbitrary")))(u, u, w, b, dxc)


def _conv_bwd_in(dpre, w, seq):
    t, c = dpre.shape
    ts, tc = CONV_TS, GROUP_W
    hb = ts // CONV_HALO
    last = t // CONV_HALO - 1
    d_spec = pl.BlockSpec((ts, tc), lambda j, i: (i, j))
    next_spec = pl.BlockSpec((CONV_HALO, tc), lambda j, i: (jnp.minimum((i + 1) * hb, last), j))
    w_spec = pl.BlockSpec((CONV_K, tc), lambda j, i: (0, j))

    def body(d_ref, next_ref, w_ref, o_ref, ext):
        i = pl.program_id(1)
        nts = seq // ts
        is_last = (i % nts) == nts - 1
        ext[0:ts, :] = d_ref[...]
        ext[ts:, :] = jnp.where(is_last, 0.0, next_ref[...])
        acc = jnp.zeros(d_ref.shape, F32)
        for q in range(CONV_K):
            acc = acc + w_ref[q:q + 1, :] * ext[pl.ds(CONV_K - 1 - q, ts), :]
        o_ref[...] = acc.astype(o_ref.dtype)

    return pl.pallas_call(
        body, name="conv_bwd_in", grid=(c // tc, t // ts), in_specs=[d_spec, next_spec, w_spec],
        out_specs=d_spec, out_shape=SDS((t, c), BF16), scratch_shapes=[pltpu.VMEM((ts + CONV_HALO, tc), F32)],
        compiler_params=_cparams(("parallel", "arbitrary")))(dpre, dpre, w)


def _split3(v):
    hi = v.astype(BF16)
    r1 = v - hi.astype(F32)
    mid = r1.astype(BF16)
    lo = (r1 - mid.astype(F32)).astype(BF16)
    return hi, mid, lo


def _ssd_prelude(dtr_ref, dtrt_ref, bias_ref, biast_ref, a_ref, at_ref):
    dt = _softplus(dtr_ref[...] + bias_ref[...])
    dtt = _softplus(dtrt_ref[...] + biast_ref[...])
    ri = lax.broadcasted_iota(jnp.int32, (CHUNK, CHUNK), 0)
    ci = lax.broadcasted_iota(jnp.int32, (CHUNK, CHUNK), 1)
    lower = ri >= ci
    upper = ri <= ci
    lower_b = jnp.where(lower, 1.0, 0.0).astype(BF16)
    upper_b = jnp.where(upper, 1.0, 0.0).astype(BF16)
    acs = sum(_dot(lower_b, p) for p in _split3(dt * a_ref[...]))
    acst = sum(_dot(p, upper_b) for p in _split3(dtt * at_ref[...]))
    return dt, acs, acst, lower, upper, lower_b, upper_b


def _ssd_specs(seq):
    nc = seq // CHUNK
    hg = HEADS_PER_GROUP
    row = lambda cc: (lambda g, b, c: (b * nc + cc(c), g))
    fwd = lambda c: c
    rev = lambda c: nc - 1 - c

    def specs(cc):
        return dict(
            xc=pl.BlockSpec((CHUNK, GROUP_W), lambda g, b, c: (b * nc + cc(c), g)),
            y=pl.BlockSpec((CHUNK, D_INNER // N_GROUPS), lambda g, b, c: (b * nc + cc(c), g)),
            dtr=pl.BlockSpec((None, CHUNK, hg), lambda g, b, c: (g, b * nc + cc(c), 0)),
            dtrt=pl.BlockSpec((None, None, hg, CHUNK), lambda g, b, c: (g, b, 0, cc(c))),
            prow=pl.BlockSpec((None, 1, hg), lambda g, b, c: (g, 0, 0)),
            pcol=pl.BlockSpec((None, hg, 1), lambda g, b, c: (g, 0, 0)),
            st=pl.BlockSpec((None, None, None, D_STATE, hg * HEAD_P), lambda g, b, c: (g, b, cc(c), 0, 0)),
        )

    return specs(fwd), specs(rev)


def _head_maps():
    hw = HEADS_PER_GROUP * HEAD_P
    shift = HEAD_P.bit_length() - 1
    hj = lax.broadcasted_iota(jnp.int32, (HEADS_PER_GROUP, hw), 0)
    lq = jnp.right_shift(lax.broadcasted_iota(jnp.int32, (HEADS_PER_GROUP, hw), 1), shift)
    spread = jnp.where(hj == lq, 1.0, 0.0).astype(BF16)
    rq = jnp.right_shift(lax.broadcasted_iota(jnp.int32, (hw, LANES), 0), shift)
    cj = lax.broadcasted_iota(jnp.int32, (hw, LANES), 1)
    gather = jnp.where(rq == cj, 1.0, 0.0).astype(BF16)
    return spread, gather


def _exact_dot(v, m01):
    return sum(_dot(p, m01) for p in _split3(v))


def _ssd_fwd(xc, dtr, dtrt, bias, biast, a, at, dskip, nb, seq):
    t = xc.shape[0]
    nc = seq // CHUNK
    hg = HEADS_PER_GROUP
    hw = hg * HEAD_P
    sp, _ = _ssd_specs(seq)

    def body(xc_ref, dtr_ref, dtrt_ref, bias_ref, biast_ref, a_ref, at_ref, d_ref, y_ref, sin_ref, st):
        @pl.when(pl.program_id(2) == 0)
        def _():
            st[...] = jnp.zeros_like(st)

        s_in = st[...]
        sin_ref[...] = s_in
        dt, acs, acst, lower, _, _, _ = _ssd_prelude(dtr_ref, dtrt_ref, bias_ref, biast_ref, a_ref, at_ref)
        spread, _ = _head_maps()
        x = xc_ref[...]
        xs = x[:, :hw]
        b16 = x[:, hw:hw + D_STATE].astype(BF16)
        c16 = x[:, hw + D_STATE:].astype(BF16)
        cb = _dot_nt(c16, b16)
        last = acs[CHUNK - 1:CHUNK, :]
        e_x = _exact_dot(jnp.exp(acs), spread)
        dec_x = _exact_dot(jnp.exp(last - acs), spread)
        tot_x = e_x[CHUNK - 1:CHUNK, :]
        d_x = _exact_dot(jnp.broadcast_to(d_ref[...], (8, hg)), spread)[0:1, :]
        xdtf = xs * _exact_dot(dt, spread)
        xdt16 = xdtf.astype(BF16)
        yoff = e_x * _dot(c16, s_in.astype(BF16))
        st[...] = tot_x * s_in + _dot_tn(b16, (dec_x * xdtf).astype(BF16))
        parts = []
        for j in range(hg):
            decay = jnp.exp(jnp.where(lower, acs[:, j:j + 1] - acst[j:j + 1, :], -jnp.inf))
            parts.append(_dot((cb * decay).astype(BF16), xdt16[:, HEAD_P * j:HEAD_P * (j + 1)]))
        y_ref[...] = jnp.concatenate(parts, axis=-1) + yoff + d_x * xs

    return pl.pallas_call(
        body, name="ssd_fwd", grid=(N_GROUPS, nb, nc),
        in_specs=[sp["xc"], sp["dtr"], sp["dtrt"], sp["prow"], sp["pcol"], sp["prow"], sp["pcol"], sp["prow"]],
        out_specs=[sp["y"], sp["st"]],
        out_shape=[SDS((t, D_INNER), F32), SDS((N_GROUPS, nb, nc, D_STATE, hw), F32)],
        scratch_shapes=[pltpu.VMEM((D_STATE, hw), F32)],
        compiler_params=_cparams(("parallel", "parallel", "arbitrary")))(xc, dtr, dtrt, bias, biast, a, at, dskip)


def _ssd_bwd(xc, dtr, dtrt, bias, biast, a, at, dskip, states, dy, nb, seq):
    t = xc.shape[0]
    nc = seq // CHUNK
    hg = HEADS_PER_GROUP
    hw = hg * HEAD_P
    _, sp = _ssd_specs(seq)

    def body(xc_ref, dtr_ref, dtrt_ref, bias_ref, biast_ref, a_ref, at_ref, d_ref, sin_ref, dy_ref,
             dxc_ref, ddtr_ref, gbias_ref, ga_ref, gd_ref, ds):
        first = (pl.program_id(1) == 0) & (pl.program_id(2) == 0)

        @pl.when(pl.program_id(2) == 0)
        def _():
            ds[...] = jnp.zeros_like(ds)

        @pl.when(first)
        def _():
            gbias_ref[...] = jnp.zeros_like(gbias_ref)
            ga_ref[...] = jnp.zeros_like(ga_ref)
            gd_ref[...] = jnp.zeros_like(gd_ref)

        dt, acs, acst, lower, upper, _, upper_b = _ssd_prelude(dtr_ref, dtrt_ref, bias_ref, biast_ref, a_ref, at_ref)
        spread, gather = _head_maps()
        x = xc_ref[...]
        dy = dy_ref[...]
        xs = x[:, :hw]
        b16 = x[:, hw:hw + D_STATE].astype(BF16)
        c16 = x[:, hw + D_STATE:].astype(BF16)
        dy16 = dy.astype(BF16)
        cb = _dot_nt(c16, b16)
        cbt = _dot_nt(b16, c16)
        last = acs[CHUNK - 1:CHUNK, :]
        e8 = jnp.exp(acs)
        dec8 = jnp.exp(last - acs)
        e_x = _exact_dot(e8, spread)
        dec_x = _exact_dot(dec8, spread)
        tot_x = e_x[CHUNK - 1:CHUNK, :]
        dt_x = _exact_dot(dt, spread)
        d_x = _exact_dot(jnp.broadcast_to(d_ref[...], (8, hg)), spread)[0:1, :]
        xdtf = xs * dt_x
        xdt16 = xdtf.astype(BF16)
        s_in = sin_ref[...]
        s16 = s_in.astype(BF16)
        ds_out = ds[...]
        ds16 = ds_out.astype(BF16)
        bds = _dot(b16, ds16)
        cs = _dot(c16, s16)
        edy16 = (e_x * dy).astype(BF16)
        ds[...] = tot_x * ds_out + _dot_tn(c16, edy16)
        lane8 = lax.broadcasted_iota(jnp.int32, (CHUNK, hg), 1)
        row8 = lax.broadcasted_iota(jnp.int32, (CHUNK, hg), 0)
        dacs8 = jnp.zeros((CHUNK, hg), F32)
        acc_m = jnp.zeros((CHUNK, CHUNK), F32)
        acc_mt = jnp.zeros((CHUNK, CHUNK), F32)
        dx_parts = []
        for j in range(hg):
            sl = slice(HEAD_P * j, HEAD_P * (j + 1))
            col = acs[:, j:j + 1]
            row = acst[j:j + 1, :]
            decay = jnp.exp(jnp.where(lower, col - row, -jnp.inf))
            decayt = jnp.exp(jnp.where(upper, row - col, -jnp.inf))
            wm = _dot_nt(dy16[:, sl], xdt16[:, sl]) * decay
            wmt = _dot_nt(xdt16[:, sl], dy16[:, sl]) * decayt
            acc_m = acc_m + wm
            acc_mt = acc_mt + wmt
            dacs8 = dacs8 + jnp.where(lane8 == j, jnp.sum(wm * cb, axis=-1, keepdims=True)
                                      - jnp.sum(wmt * cbt, axis=-1, keepdims=True), 0.0)
            dx_parts.append(_dot((cbt * decayt).astype(BF16), dy16[:, sl]))
        dx = jnp.concatenate(dx_parts, axis=-1) + dec_x * bds
        dxc_ref[:, :hw] = dx * dt_x + d_x * dy
        dxc_ref[:, hw:hw + D_STATE] = _dot(acc_mt.astype(BF16), c16) + _dot_nt((dec_x * xdtf).astype(BF16), ds16)
        dxc_ref[:, hw + D_STATE:] = _dot(acc_m.astype(BF16), b16) + _dot_nt(edy16, s16)
        dtot_rows = jnp.broadcast_to(_colsum(ds_out * s_in), (8, hw))
        sums = _exact_dot(jnp.concatenate([dy * cs, xdtf * bds, dx * xs, dy * xs, dtot_rows], axis=0), gather)
        de8 = sums[0:CHUNK, :hg]
        ddec8 = sums[CHUNK:2 * CHUNK, :hg]
        ddtx8 = sums[2 * CHUNK:3 * CHUNK, :hg]
        gd8 = _colsum(sums[3 * CHUNK:4 * CHUNK, :hg])
        dtot8 = sums[4 * CHUNK:4 * CHUNK + 1, :hg]
        extra = _colsum(ddec8 * dec8) + dtot8 * e8[CHUNK - 1:CHUNK, :]
        dacs8 = dacs8 + de8 * e8 - ddec8 * dec8 + jnp.where(row8 == CHUNK - 1, extra, 0.0)
        da = sum(_dot(upper_b, p) for p in _split3(dacs8))
        av = a_ref[...]
        ddt = da * av + ddtx8
        ddtr = ddt * _sigmoid(dtr_ref[...] + bias_ref[...])
        ddtr_ref[...] = ddtr
        gbias_ref[...] += _colsum(ddtr)
        ga_ref[...] += _colsum(da * dt) * av
        gd_ref[...] += gd8

    return pl.pallas_call(
        body, name="ssd_bwd", grid=(N_GROUPS, nb, nc),
        in_specs=[sp["xc"], sp["dtr"], sp["dtrt"], sp["prow"], sp["pcol"], sp["prow"], sp["pcol"], sp["prow"],
                  sp["st"], sp["y"]],
        out_specs=[sp["xc"], sp["dtr"], sp["prow"], sp["prow"], sp["prow"]],
        out_shape=[SDS((t, N_GROUPS * GROUP_W), F32), SDS((N_GROUPS, t, hg), F32)]
        + [SDS((N_GROUPS, 1, hg), F32)] * 3,
        scratch_shapes=[pltpu.VMEM((D_STATE, hw), F32)],
        compiler_params=_cparams(("arbitrary", "arbitrary", "arbitrary")))(
            xc, dtr, dtrt, bias, biast, a, at, dskip, states, dy)


def _group_bcast(v, width, fn):
    parts = []
    for q in range(v.shape[-1] // width):
        s = fn(v[:, q * width:(q + 1) * width])
        parts.append(jnp.broadcast_to(s, (v.shape[0], width)))
    return jnp.concatenate(parts, axis=-1)


def _gate_norm_fwd(y, z, g):
    t, d = y.shape
    tm = 256
    gw = d // N_GROUPS

    def fn(i, y_ref, z_ref, g_ref):
        zv = z_ref[...]
        u = y_ref[...] * (zv * _sigmoid(zv))
        r = lax.rsqrt(_group_bcast(u * u, gw, lambda p: jnp.mean(p, axis=-1, keepdims=True)) + EPS)
        return [u * r * g_ref[...]]

    return _rw("gate_norm_fwd", fn, t // tm, [(y, _rs(tm, d)), (z, _rs(tm, d)), (g, _fs((1, d)))],
               [(SDS((t, d), BF16), _rs(tm, d))])[0]


def _gate_norm_bwd(y, z, g, dyn):
    t, d = y.shape
    tm = 256
    gw = d // N_GROUPS

    def fn(i, y_ref, z_ref, g_ref, dyn_ref):
        zv = z_ref[...]
        yv = y_ref[...]
        sg = _sigmoid(zv)
        sz = zv * sg
        u = yv * sz
        r = lax.rsqrt(_group_bcast(u * u, gw, lambda p: jnp.mean(p, axis=-1, keepdims=True)) + EPS)
        uhat = u * r
        dv = dyn_ref[...]
        duhat = dv * g_ref[...]
        du = r * (duhat - uhat * _group_bcast(duhat * uhat, gw, lambda p: jnp.mean(p, axis=-1, keepdims=True)))
        dz = du * yv * sg * (1.0 + zv * (1.0 - sg))
        return [du * sz, dz, _colsum(dv * uhat)]

    return _rw("gate_norm_bwd", fn, t // tm,
               [(y, _rs(tm, d)), (z, _rs(tm, d)), (g, _fs((1, d))), (dyn, _rs(tm, d))],
               [(SDS((t, d), F32), _rs(tm, d)), (SDS((t, d), BF16), _rs(tm, d)), (SDS((1, d), F32), _fs((1, d)))],
               n_acc=1)


def _rope_tables(seq):
    half = ATT_D // 2
    inv = ROPE_THETA ** (-jnp.arange(half, dtype=F32) / half)
    ang = jnp.arange(seq, dtype=F32)[:, None] * inv[None, :]
    cos, sin = jnp.cos(ang), jnp.sin(ang)
    return jnp.concatenate([cos, cos], axis=-1), jnp.concatenate([-sin, sin], axis=-1)


ATT_TILE = 512


def _strided_spec(r, mtiles):
    return pl.BlockSpec((None, r, None, ATT_TILE // r, ATT_W), lambda i: (i // mtiles, 0, i % mtiles, 0, 0))


def _strided_shape(nb, r, mtiles, dtype):
    return SDS((nb, r, mtiles, ATT_TILE // r, ATT_W), dtype)


def _to_strided(val, out_ref, lanes, r, sc):
    if r == 1:
        out_ref[0, :, lanes] = val.astype(out_ref.dtype)
        return
    sc[...] = val
    for rr in range(r):
        out_ref[rr, :, lanes] = sc[pl.ds(rr, ATT_TILE // r, stride=r), :].astype(out_ref.dtype)


def _from_strided(in_ref, lanes, r, sc):
    if r == 1:
        return in_ref[0, :, lanes]
    for rr in range(r):
        sc[pl.ds(rr, ATT_TILE // r, stride=r), :] = in_ref[rr, :, lanes]
    return sc[...]


def _rope_fwd(qkv, cos, sin, nb, seq):
    t = qkv.shape[0]
    tm = ATT_TILE
    mtiles = seq // tm
    w = ATT_HEADS * ATT_D
    tab = pl.BlockSpec((tm, ATT_D), lambda i: (i % mtiles, 0))
    ng = len(ATT_DILATIONS)

    def body(q_ref, k_ref, v_ref, cos_ref, sin_ref, *rest):
        outs, sc = rest[:3 * ng], rest[3 * ng]
        c, s = cos_ref[...], sin_ref[...]
        for which, ref in enumerate((q_ref, k_ref, v_ref)):
            for h in range(ATT_HEADS):
                g, slot = divmod(h, ATT_SLOTS)
                p = ref[:, h * ATT_D:(h + 1) * ATT_D]
                if which < 2:
                    p = p * c + pltpu.roll(p, ATT_D // 2, 1) * s
                _to_strided(p, outs[which * ng + g], slice(slot * ATT_D, (slot + 1) * ATT_D), ATT_DILATIONS[g], sc)

    out_specs = [_strided_spec(r, mtiles) for _ in range(3) for r in ATT_DILATIONS]
    out_shape = [_strided_shape(nb, r, mtiles, BF16) for _ in range(3) for r in ATT_DILATIONS]
    outs = pl.pallas_call(
        body, name="rope_fwd", grid=(t // tm,),
        in_specs=[_rs(tm, w, 0), _rs(tm, w, 1), _rs(tm, w, 2), tab, tab], out_specs=out_specs, out_shape=out_shape,
        scratch_shapes=[pltpu.VMEM((tm, ATT_D), F32)], compiler_params=_cparams(("arbitrary",)))(
            qkv, qkv, qkv, cos, sin)
    flat = [o.reshape(t, ATT_W) for o in outs]
    return flat[0:ng], flat[ng:2 * ng], flat[2 * ng:]


def _rope_bwd(dq, dk, dv, cos, sin, nb, seq):
    t = dq[0].shape[0]
    tm = ATT_TILE
    mtiles = seq // tm
    w = ATT_HEADS * ATT_D
    tab = pl.BlockSpec((tm, ATT_D), lambda i: (i % mtiles, 0))
    ng = len(ATT_DILATIONS)

    def body(*refs):
        ins, (cos_ref, sin_ref, o_ref, sc) = refs[:3 * ng], refs[3 * ng:]
        c, s = cos_ref[...], sin_ref[...]
        for which in range(3):
            for h in range(ATT_HEADS):
                g, slot = divmod(h, ATT_SLOTS)
                p = _from_strided(ins[which * ng + g], slice(slot * ATT_D, (slot + 1) * ATT_D), ATT_DILATIONS[g], sc)
                if which < 2:
                    p = p * c - pltpu.roll(p, ATT_D // 2, 1) * s
                o_ref[:, which * w + h * ATT_D:which * w + (h + 1) * ATT_D] = p.astype(o_ref.dtype)

    views = [a.reshape(nb, r, mtiles, tm // r, ATT_W) for grp in (dq, dk, dv) for a, r in zip(grp, ATT_DILATIONS)]
    return pl.pallas_call(
        body, name="rope_bwd", grid=(t // tm,),
        in_specs=[_strided_spec(r, mtiles) for _ in range(3) for r in ATT_DILATIONS] + [tab, tab],
        out_specs=_rs(tm, 3 * w), out_shape=SDS((t, 3 * w), BF16),
        scratch_shapes=[pltpu.VMEM((tm, ATT_D), F32)], compiler_params=_cparams(("arbitrary",)))(*views, cos, sin)


def _att_masks():
    ri = lax.broadcasted_iota(jnp.int32, (ATT_BLOCK, ATT_BLOCK), 0)
    ci = lax.broadcasted_iota(jnp.int32, (ATT_BLOCK, ATT_BLOCK), 1)
    return ci <= ri, ci >= ri


def _att_fwd(q, k, v, g, seq):
    t, w = q.shape
    nblk = t // ATT_BLOCK
    nbs = seq // ATT_DILATIONS[g] // ATT_BLOCK
    scale = ATT_D ** -0.5
    cur = pl.BlockSpec((ATT_BLOCK, w), lambda n: (n, 0))
    prev = pl.BlockSpec((ATT_BLOCK, w), lambda n: (jnp.maximum(n - 1, 0), 0))

    def body(q_ref, kc_ref, kp_ref, vc_ref, vp_ref, o_ref, lse_ref):
        has_prev = (pl.program_id(0) % nbs) != 0
        mcur, mprev = _att_masks()
        mprev = mprev & has_prev
        for h in range(ATT_SLOTS):
            sl = slice(h * ATT_D, (h + 1) * ATT_D)
            qh = q_ref[:, sl]
            sc = jnp.where(mcur, _dot_nt(qh, kc_ref[:, sl]) * scale, -jnp.inf)
            sp = jnp.where(mprev, _dot_nt(qh, kp_ref[:, sl]) * scale, -jnp.inf)
            m = jnp.maximum(jnp.max(sc, axis=-1, keepdims=True), jnp.max(sp, axis=-1, keepdims=True))
            pc = jnp.exp(sc - m)
            pp = jnp.exp(sp - m)
            den = jnp.sum(pc, axis=-1, keepdims=True) + jnp.sum(pp, axis=-1, keepdims=True)
            o = _dot(pc.astype(BF16), vc_ref[:, sl]) + _dot(pp.astype(BF16), vp_ref[:, sl])
            o_ref[:, sl] = o / den
            lse_ref[:, sl] = jnp.broadcast_to(m + jnp.log(den), (ATT_BLOCK, ATT_D))

    return pl.pallas_call(
        body, name=f"att_fwd_{g}", grid=(nblk,), in_specs=[cur, cur, prev, cur, prev], out_specs=[cur, cur],
        out_shape=[SDS((t, w), F32), SDS((t, w), F32)],
        compiler_params=_cparams(("arbitrary",)))(q, k, k, v, v)


def _att_bwd(q, k, v, do, lse, dlt, g, seq):
    t, w = q.shape
    nblk = t // ATT_BLOCK
    nbs = seq // ATT_DILATIONS[g] // ATT_BLOCK
    scale = ATT_D ** -0.5
    cur = pl.BlockSpec((ATT_BLOCK, w), lambda n: (n, 0))
    nxt = pl.BlockSpec((ATT_BLOCK, w), lambda n: (jnp.minimum(n + 1, nblk - 1), 0))

    def body(qc_ref, qn_ref, k_ref, v_ref, doc_ref, don_ref, lsec_ref, lsen_ref, dltc_ref, dltn_ref,
             dq_ref, dk_ref, dv_ref, carry):
        n = pl.program_id(0)

        @pl.when((n % nbs) == 0)
        def _():
            carry[...] = jnp.zeros_like(carry)

        has_next = (((n + 1) % nbs) != 0) & (n + 1 < nblk)
        mcur, mprev = _att_masks()
        mnext = mprev & has_next
        for h in range(ATT_SLOTS):
            sl = slice(h * ATT_D, (h + 1) * ATT_D)
            kh, vh = k_ref[:, sl], v_ref[:, sl]
            qc, doc = qc_ref[:, sl], doc_ref[:, sl]
            p = jnp.where(mcur, jnp.exp(_dot_nt(qc, kh) * scale - lsec_ref[:, sl]), 0.0)
            dsc = (p * (_dot_nt(doc, vh) - dltc_ref[:, sl]) * scale).astype(BF16)
            dq_ref[:, sl] = carry[:, sl] + _dot(dsc, kh)
            qn, don = qn_ref[:, sl], don_ref[:, sl]
            pn = jnp.where(mnext, jnp.exp(_dot_nt(qn, kh) * scale - lsen_ref[:, sl]), 0.0)
            dsn = (pn * (_dot_nt(don, vh) - dltn_ref[:, sl]) * scale).astype(BF16)
            carry[:, sl] = _dot(dsn, kh)
            dk_ref[:, sl] = _dot_tn(dsc, qc) + _dot_tn(dsn, qn)
            dv_ref[:, sl] = _dot_tn(p.astype(BF16), doc) + _dot_tn(pn.astype(BF16), don)

    return pl.pallas_call(
        body, name=f"att_bwd_{g}", grid=(nblk,), in_specs=[cur, nxt, cur, cur, cur, nxt, cur, nxt, cur, nxt],
        out_specs=[cur, cur, cur], out_shape=[SDS((t, w), F32)] * 3,
        scratch_shapes=[pltpu.VMEM((ATT_BLOCK, w), F32)],
        compiler_params=_cparams(("arbitrary",)))(q, q, k, v, do, do, lse, lse, dlt, dlt)


def _merge_weights(ls):
    m = jnp.maximum(jnp.maximum(ls[0], ls[1]), ls[2])
    es = [jnp.exp(v - m) for v in ls]
    den = es[0] + es[1] + es[2]
    return [e / den for e in es]


def _merge_fwd(o, lse, nb, seq):
    t = o[0].shape[0]
    tm = ATT_TILE
    mtiles = seq // tm
    ng = len(ATT_DILATIONS)

    def body(*refs):
        o_refs, l_refs, out_ref, scs = refs[:ng], refs[ng:2 * ng], refs[2 * ng], refs[2 * ng + 1:]
        for slot in range(ATT_SLOTS):
            lanes = slice(slot * ATT_D, (slot + 1) * ATT_D)
            ov = [_from_strided(o_refs[g], lanes, r, scs[2 * g]) for g, r in enumerate(ATT_DILATIONS)]
            ws = _merge_weights([_from_strided(l_refs[g], lanes, r, scs[2 * g + 1])
                                 for g, r in enumerate(ATT_DILATIONS)])
            out_ref[:, lanes] = (ws[0] * ov[0] + ws[1] * ov[1] + ws[2] * ov[2]).astype(out_ref.dtype)

    views = [a.reshape(nb, r, mtiles, tm // r, ATT_W) for grp in (o, lse) for a, r in zip(grp, ATT_DILATIONS)]
    return pl.pallas_call(
        body, name="att_merge_fwd", grid=(t // tm,),
        in_specs=[_strided_spec(r, mtiles) for _ in range(2) for r in ATT_DILATIONS],
        out_specs=_rs(tm, ATT_W), out_shape=SDS((t, ATT_W), BF16),
        scratch_shapes=[pltpu.VMEM((tm, ATT_D), F32)] * (2 * ng), compiler_params=_cparams(("arbitrary",)))(*views)


def _merge_bwd(o, lse, datt, nb, seq):
    t = o[0].shape[0]
    tm = ATT_TILE
    mtiles = seq // tm
    ng = len(ATT_DILATIONS)

    def body(*refs):
        o_refs, l_refs, d_ref = refs[:ng], refs[ng:2 * ng], refs[2 * ng]
        do_refs, dlt_refs = refs[2 * ng + 1:3 * ng + 1], refs[3 * ng + 1:4 * ng + 1]
        scs = refs[4 * ng + 1:]
        for slot in range(ATT_SLOTS):
            lanes = slice(slot * ATT_D, (slot + 1) * ATT_D)
            ov = [_from_strided(o_refs[g], lanes, r, scs[2 * g]) for g, r in enumerate(ATT_DILATIONS)]
            ws = _merge_weights([_from_strided(l_refs[g], lanes, r, scs[2 * g + 1])
                                 for g, r in enumerate(ATT_DILATIONS)])
            dv = d_ref[:, lanes]
            att = ws[0] * ov[0] + ws[1] * ov[1] + ws[2] * ov[2]
            dot = jnp.broadcast_to(jnp.sum(dv * att, axis=-1, keepdims=True), (tm, ATT_D))
            for g, r in enumerate(ATT_DILATIONS):
                _to_strided(ws[g] * dv, do_refs[g], lanes, r, scs[2 * ng])
                _to_strided(ws[g] * dot, dlt_refs[g], lanes, r, scs[2 * ng + 1])

    views = [a.reshape(nb, r, mtiles, tm // r, ATT_W) for grp in (o, lse) for a, r in zip(grp, ATT_DILATIONS)]
    outs = pl.pallas_call(
        body, name="att_merge_bwd", grid=(t // tm,),
        in_specs=[_strided_spec(r, mtiles) for _ in range(2) for r in ATT_DILATIONS] + [_rs(tm, ATT_W)],
        out_specs=[_strided_spec(r, mtiles) for _ in range(2) for r in ATT_DILATIONS],
        out_shape=[_strided_shape(nb, r, mtiles, dt) for dt in (BF16, F32) for r in ATT_DILATIONS],
        scratch_shapes=[pltpu.VMEM((tm, ATT_D), F32)] * (2 * ng + 2), compiler_params=_cparams(("arbitrary",)))(
            *views, datt)
    flat = [a.reshape(t, ATT_W) for a in outs]
    return flat[:ng], flat[ng:]


def _mix_fwd(gate_logits, b_gate, y_ssm, y_att):
    t, d = y_ssm.shape
    tm = 512

    def fn(i, g0_ref, g1_ref, b0_ref, b1_ref, ys_ref, ya_ref):
        g0 = _sigmoid(g0_ref[...] + b0_ref[...])
        g1 = _sigmoid(g1_ref[...] + b1_ref[...])
        return [g0 * ys_ref[...] + g1 * ya_ref[...]]

    b_spec = lambda cb: pl.BlockSpec((1, d), lambda i: (0, cb))
    return _rw("mix_fwd", fn, t // tm,
               [(gate_logits, _rs(tm, d, 0)), (gate_logits, _rs(tm, d, 1)), (b_gate, b_spec(0)), (b_gate, b_spec(1)),
                (y_ssm, _rs(tm, d)), (y_att, _rs(tm, d))],
               [(SDS((t, d), BF16), _rs(tm, d))])[0]


def _mix_bwd(gate_logits, b_gate, y_ssm, y_att, dmixed):
    t, d = y_ssm.shape
    tm = 256

    def fn(i, g0_ref, g1_ref, b0_ref, b1_ref, ys_ref, ya_ref, dm_ref):
        g0 = _sigmoid(g0_ref[...] + b0_ref[...])
        g1 = _sigmoid(g1_ref[...] + b1_ref[...])
        dm = dm_ref[...]
        dg = jnp.concatenate([dm * ys_ref[...] * g0 * (1.0 - g0), dm * ya_ref[...] * g1 * (1.0 - g1)], axis=-1)
        return [dm * g0, dm * g1, dg, _colsum(dg)]

    b_spec = lambda cb: pl.BlockSpec((1, d), lambda i: (0, cb))
    return _rw("mix_bwd", fn, t // tm,
               [(gate_logits, _rs(tm, d, 0)), (gate_logits, _rs(tm, d, 1)), (b_gate, b_spec(0)), (b_gate, b_spec(1)),
                (y_ssm, _rs(tm, d)), (y_att, _rs(tm, d)), (dmixed, _rs(tm, d))],
               [(SDS((t, d), BF16), _rs(tm, d)), (SDS((t, d), BF16), _rs(tm, d)),
                (SDS((t, 2 * d), BF16), _rs(tm, 2 * d)), (SDS((1, 2 * d), F32), _fs((1, 2 * d)))], n_acc=1)


def _swiglu_fwd(gt, up):
    t, f = gt.shape
    tm = 256

    def fn(i, g_ref, u_ref):
        gv = g_ref[...]
        return [gv * _sigmoid(gv) * u_ref[...]]

    return _rw("swiglu_fwd", fn, t // tm, [(gt, _rs(tm, f)), (up, _rs(tm, f))], [(SDS((t, f), BF16), _rs(tm, f))])[0]


def _swiglu_bwd(gt, up, dact):
    t, f = gt.shape
    tm = 256

    def fn(i, g_ref, u_ref, d_ref):
        gv, dv = g_ref[...], d_ref[...]
        sg = _sigmoid(gv)
        return [dv * u_ref[...] * sg * (1.0 + gv * (1.0 - sg)), dv * gv * sg]

    return _rw("swiglu_bwd", fn, t // tm, [(gt, _rs(tm, f)), (up, _rs(tm, f)), (dact, _rs(tm, f))],
               [(SDS((t, f), BF16), _rs(tm, f))] * 2)


def _adamw(w, g, m, v, name):
    r, c = w.shape
    tr = _row_tile(r, max(8, 400_000 // c))
    c1 = 1.0 / (1.0 - ADAM_B1 ** ADAM_STEP)
    c2 = 1.0 / (1.0 - ADAM_B2 ** ADAM_STEP)

    def fn(i, w_ref, g_ref, m_ref, v_ref):
        gv = g_ref[...]
        mn = ADAM_B1 * m_ref[...] + (1.0 - ADAM_B1) * gv
        vn = ADAM_B2 * v_ref[...] + (1.0 - ADAM_B2) * (gv * gv)
        delta = -ADAM_LR * ((mn * c1) / (jnp.sqrt(vn * c2) + ADAM_EPS) + ADAM_WD * w_ref[...])
        return [delta, mn, vn]

    spec = pl.BlockSpec((tr, c), lambda i: (i, 0))
    return _rw(name, fn, r // tr, [(w, spec), (g, spec), (m, spec), (v, spec)], [(SDS((r, c), F32), spec)] * 3)


ANY = pl.BlockSpec(memory_space=pl.ANY)


def _place():
    x, y, c = lax.axis_index("x"), lax.axis_index("y"), lax.axis_index("c")
    chips = [(1 - x, y), (x, 1 - y), (1 - x, 1 - y)]
    return x, y, c, chips


def _remote(src, dst, ssem, rsem, to):
    return pltpu.make_async_remote_copy(src_ref=src, dst_ref=dst, send_sem=ssem, recv_sem=rsem, device_id=to,
                                        device_id_type=MESH)


def _gather_weights(wp):
    def body(w_ref, out_ref, ssem, rsem):
        x, y, c, chips = _place()
        me = 2 * x + y
        sib = (x, y, 1 - c)
        first = [_remote(w_ref.at[c], out_ref.at[me, c], ssem.at[j], rsem.at[j], (*chip, c))
                 for j, chip in enumerate(chips)]
        for cp in first:
            cp.start()
        passed = []
        for j, chip in enumerate(chips):
            ci = 2 * chip[0] + chip[1]
            _remote(w_ref.at[c], out_ref.at[ci, c], ssem.at[j], rsem.at[j], (*chip, c)).wait_recv()
            cp = _remote(out_ref.at[ci, c], out_ref.at[ci, c], ssem.at[3 + j], rsem.at[3 + j], sib)
            cp.start()
            passed.append(cp)
        for j, chip in enumerate(chips):
            ci = 2 * chip[0] + chip[1]
            _remote(out_ref.at[ci, 1 - c], out_ref.at[ci, 1 - c], ssem.at[3 + j], rsem.at[3 + j], sib).wait_recv()
        for cp in first + passed:
            cp.wait_send()

    return pl.pallas_call(
        body, name="gather_weights", in_specs=[ANY], out_specs=ANY,
        out_shape=SDS((N_CHIPS,) + wp.shape, wp.dtype),
        scratch_shapes=[pltpu.SemaphoreType.DMA((6,)), pltpu.SemaphoreType.DMA((6,))],
        compiler_params=pltpu.CompilerParams(has_side_effects=True))(wp)


def _swap_halves(g2):
    def body(g_ref, out_ref, ssem, rsem):
        x, y, c, _ = _place()
        cp = _remote(g_ref.at[1 - c], out_ref, ssem, rsem, (x, y, 1 - c))
        cp.start()
        cp.wait()

    return pl.pallas_call(
        body, name="swap_halves", in_specs=[ANY], out_specs=ANY, out_shape=SDS(g2.shape[1:], g2.dtype),
        scratch_shapes=[pltpu.SemaphoreType.DMA(()), pltpu.SemaphoreType.DMA(())],
        compiler_params=pltpu.CompilerParams(has_side_effects=True))(g2)


def _add_own_half(g2, other, c):
    _, nch, rows, w = g2.shape
    tr = _row_tile(rows, 512)
    nr = rows // tr

    def body(c_ref, a_ref, b_ref, o_ref):
        o_ref[...] = (a_ref[...].astype(F32) + b_ref[...].astype(F32)).astype(o_ref.dtype)

    grid_spec = pltpu.PrefetchScalarGridSpec(
        num_scalar_prefetch=1, grid=(nch, nr),
        in_specs=[pl.BlockSpec((None, None, tr, w), lambda k, i, c_ref: (c_ref[0], k, i, 0)),
                  pl.BlockSpec((None, tr, w), lambda k, i, c_ref: (k, i, 0))],
        out_specs=pl.BlockSpec((None, tr, w), lambda k, i, c_ref: (k, i, 0)))
    return pl.pallas_call(
        body, name="add_own_half", grid_spec=grid_spec, out_shape=SDS(other.shape, other.dtype),
        compiler_params=_cparams(("arbitrary", "arbitrary")))(jnp.reshape(c, (1,)).astype(jnp.int32), g2, other)


def _scatter_to_chips(p):
    def body(p_ref, q_ref, ssem, rsem):
        x, y, c, chips = _place()
        me = 2 * x + y
        sent = []
        for j, chip in enumerate(chips):
            ci = 2 * chip[0] + chip[1]
            cp = _remote(p_ref.at[ci], q_ref.at[me], ssem.at[j], rsem.at[j], (*chip, c))
            cp.start()
            sent.append(cp)
        for j, chip in enumerate(chips):
            ci = 2 * chip[0] + chip[1]
            _remote(p_ref.at[ci], q_ref.at[ci], ssem.at[j], rsem.at[j], (*chip, c)).wait_recv()
        for cp in sent:
            cp.wait_send()

    return pl.pallas_call(
        body, name="scatter_to_chips", in_specs=[ANY], out_specs=ANY, out_shape=SDS(p.shape, p.dtype),
        scratch_shapes=[pltpu.SemaphoreType.DMA((3,)), pltpu.SemaphoreType.DMA((3,))],
        compiler_params=pltpu.CompilerParams(has_side_effects=True))(p)


def _sum_chips(q):
    nch, rows, w = q.shape
    tr = _row_tile(rows, 512)

    def fn(i, q_ref):
        return [((q_ref[0].astype(F32) + q_ref[1].astype(F32)) + q_ref[2].astype(F32)) + q_ref[3].astype(F32)]

    return _rw("sum_chips", fn, rows // tr, [(q, pl.BlockSpec((nch, tr, w), lambda i: (0, i, 0)))],
               [(SDS((rows, w), F32), _rs(tr, w))])[0]


def _allreduce_small(v, name):
    rows, w = v.shape
    offsets = [(dx, dy, dc) for dx in (0, 1) for dy in (0, 1) for dc in (0, 1)][1:]

    def body(v_ref, o_ref, buf, ssem, rsem):
        x, y, c, _ = _place()
        flip = lambda p, d: 1 - p if d else p
        peers = [(flip(x, dx), flip(y, dy), flip(c, dc)) for dx, dy, dc in offsets]
        index = lambda p: 4 * p[0] + 2 * p[1] + p[2]
        me = index((x, y, c))
        buf[me] = v_ref[...]
        sent = [_remote(v_ref, buf.at[me], ssem.at[q], rsem.at[q], p) for q, p in enumerate(peers)]
        for cp in sent:
            cp.start()
        for q, p in enumerate(peers):
            _remote(v_ref, buf.at[index(p)], ssem.at[q], rsem.at[q], p).wait_recv()
        for cp in sent:
            cp.wait_send()
        acc = buf[0]
        for q in range(1, 8):
            acc = acc + buf[q]
        o_ref[...] = acc

    vm = pl.BlockSpec(memory_space=pltpu.VMEM)
    return pl.pallas_call(
        body, name=name, in_specs=[vm], out_specs=vm, out_shape=SDS((rows, w), F32),
        scratch_shapes=[pltpu.VMEM((8, rows, w), F32), pltpu.SemaphoreType.DMA((7,)), pltpu.SemaphoreType.DMA((7,))],
        compiler_params=pltpu.CompilerParams(has_side_effects=True))(v)


def _join_halves(h):
    def body(h_ref, out_ref, ssem, rsem):
        x, y, c, _ = _place()
        cp = _remote(h_ref, out_ref.at[c], ssem, rsem, (x, y, 1 - c))
        cp.start()
        _remote(h_ref, out_ref.at[1 - c], ssem, rsem, (x, y, 1 - c)).wait_recv()
        cp.wait_send()

    return pl.pallas_call(
        body, name="join_halves", in_specs=[ANY], out_specs=ANY, out_shape=SDS((2,) + h.shape, h.dtype),
        scratch_shapes=[pltpu.SemaphoreType.DMA(()), pltpu.SemaphoreType.DMA(())],
        compiler_params=pltpu.CompilerParams(has_side_effects=True))(h)


PACK_W = 1024
SHARDED = ("w_in", "w_ffn_gate", "w_ffn_up", "w_ssm_out", "w_att_out", "w_mix_out", "w_ffn_down")
COL_SHARDED = ("w_in", "w_ffn_gate", "w_ffn_up", "w_att_out")
SMALL = ("norm_mix", "b_gate", "conv_b", "dt_bias", "a_log", "d_skip", "ssm_norm", "norm_ffn", "norm_final")


PACK_ROW_ALIGN = 16


def _rows(n):
    return -(-n // (PACK_W * PACK_ROW_ALIGN)) * PACK_ROW_ALIGN


def _pack_rows(parts, total_rows):
    rows = []
    for p in parts:
        flat = p.reshape(-1)
        pad = _rows(flat.shape[0]) * PACK_W - flat.shape[0]
        if pad:
            flat = jnp.concatenate([flat, jnp.zeros((pad,), flat.dtype)])
        rows.append(flat.reshape(-1, PACK_W))
    used = sum(r.shape[0] for r in rows)
    if total_rows > used:
        rows.append(jnp.zeros((total_rows - used, PACK_W), rows[0].dtype))
    return jnp.concatenate(rows, axis=0)


def _padded_rows(n):
    return -(-n // 32) * 32


def _wire_name(name):
    return name + "_t" if name in COL_SHARDED else name


def _wire_shard(w, name):
    return w.T if name in COL_SHARDED else w


def _group_major(a, axis):
    gw = D_INNER // N_GROUPS
    take = lambda lo, n: lax.slice_in_dim(a, lo, lo + n, axis=axis)
    parts = []
    for g in range(N_GROUPS):
        parts += [take(g * gw, gw), take(D_INNER + g * D_STATE, D_STATE),
                  take(D_INNER + N_GROUPS * D_STATE + g * D_STATE, D_STATE)]
    return jnp.concatenate(parts, axis=axis)


def _group_major_inv(a, axis):
    gw = D_INNER // N_GROUPS
    take = lambda lo, n: lax.slice_in_dim(a, lo, lo + n, axis=axis)
    xs = [take(g * GROUP_W, gw) for g in range(N_GROUPS)]
    bs = [take(g * GROUP_W + gw, D_STATE) for g in range(N_GROUPS)]
    cs = [take(g * GROUP_W + gw + D_STATE, D_STATE) for g in range(N_GROUPS)]
    return jnp.concatenate(xs + bs + cs, axis=axis)


def _local_step(x, target, wts):
    nb, seq, d = x.shape
    t = nb * seq
    x = x.reshape(t, d)
    target = target.reshape(t, d)
    hg = HEADS_PER_GROUP

    w_in_t = wts["w_in_t"]
    o1, o2, o3, o4 = D_INNER, D_INNER + CONV_DIM, D_INNER + CONV_DIM + N_HEADS, D_INNER + CONV_DIM + N_HEADS + QKV_DIM
    w_z = w_in_t[:o1]
    w_xbc = _group_major(w_in_t[o1:o2], 0)
    w_dt = jnp.pad(w_in_t[o2:o3], ((0, DT_PAD - N_HEADS), (0, 0)))
    w_qkv = w_in_t[o3:o4]
    w_gate = w_in_t[o4:]
    conv_w = _group_major(wts["conv_w"], 1)
    conv_b = _group_major(wts["conv_b"], 1)

    def per_group_row(p):
        return p.reshape(N_GROUPS, 1, hg)

    def per_group_col(p):
        return p.reshape(N_GROUPS, hg, 1)

    a_neg = -jnp.exp(wts["a_log"])
    bias_r, bias_c = per_group_row(wts["dt_bias"]), per_group_col(wts["dt_bias"])
    a_r, a_c = per_group_row(a_neg), per_group_col(a_neg)
    dskip_r = per_group_row(wts["d_skip"])
    cos, sin = _rope_tables(seq)

    h = _rms_fwd(x, wts["norm_mix"], "rms_mix_fwd")
    z = _mm(h, w_z, "nt", F32, "proj_z")
    xbc = _mm(h, w_xbc, "nt", F32, "proj_xbc")
    dt_raw = _mm(h, w_dt, "nt", F32, "proj_dt")
    qkv = _mm(h, w_qkv, "nt", F32, "proj_qkv")
    gate_logits = _mm(h, w_gate, "nt", F32, "proj_gate")

    xc = _conv_fwd(xbc, conv_w, conv_b, seq)
    dtr = dt_raw[:, :N_HEADS].reshape(t, N_GROUPS, hg).transpose(1, 0, 2)
    dtrt = dt_raw[:, :N_HEADS].reshape(nb, seq, N_GROUPS, hg).transpose(2, 0, 3, 1)
    y, states = _ssd_fwd(xc, dtr, dtrt, bias_r, bias_c, a_r, a_c, dskip_r, nb, seq)
    yn = _gate_norm_fwd(y, z, wts["ssm_norm"])
    y_ssm = _mm(yn, wts["w_ssm_out"], "nn", F32, "ssm_out")

    groups = range(len(ATT_DILATIONS))
    qg, kg, vg = _rope_fwd(qkv, cos, sin, nb, seq)
    o_g, lse_g = zip(*[_att_fwd(qg[i], kg[i], vg[i], i, seq) for i in groups])
    att = _merge_fwd(o_g, lse_g, nb, seq)
    y_att = _mm(att, wts["w_att_out_t"], "nt", F32, "att_out")

    mixed = _mix_fwd(gate_logits, wts["b_gate"], y_ssm, y_att)
    x1 = _mm(mixed, wts["w_mix_out"], "nn", F32, "mix_out", add=x)
    h2 = _rms_fwd(x1, wts["norm_ffn"], "rms_ffn_fwd")
    gt = _mm(h2, wts["w_ffn_gate_t"], "nt", F32, "ffn_gate")
    up = _mm(h2, wts["w_ffn_up_t"], "nt", F32, "ffn_up")
    act = _swiglu_fwd(gt, up)
    x2 = _mm(act, wts["w_ffn_down"], "nn", F32, "ffn_down", add=x1)

    g = {}
    dx2, dx2_b, g["norm_final"], loss = _final_fwd_bwd(x2, target, wts["norm_final"].reshape(1, d))
    dact = _mm(dx2_b, wts["w_ffn_down"], "nt", F32, "d_act")
    g["w_ffn_down"] = _mm(act, dx2_b, "tn", BF16, "g_ffn_down")
    dgt, dup = _swiglu_bwd(gt, up, dact)
    g["w_ffn_gate_t"] = _mm(dgt, h2, "tn", BF16, "g_ffn_gate")
    g["w_ffn_up_t"] = _mm(dup, h2, "tn", BF16, "g_ffn_up")
    dh2 = _mm(dgt, wts["w_ffn_gate_t"], "nn", F32, "d_h2_gate")
    dh2 = _mm(dup, wts["w_ffn_up_t"], "nn", F32, "d_h2_up", add=dh2)
    dx1, dx1_b, g["norm_ffn"] = _rms_bwd(x1, dh2, wts["norm_ffn"], dx2, "rms_ffn_bwd")

    dmixed = _mm(dx1_b, wts["w_mix_out"], "nt", F32, "d_mixed")
    g["w_mix_out"] = _mm(mixed, dx1_b, "tn", BF16, "g_mix_out")
    dy_ssm, dy_att, dgate, g["b_gate"] = _mix_bwd(gate_logits, wts["b_gate"], y_ssm, y_att, dmixed)

    datt = _mm(dy_att, wts["w_att_out_t"], "nn", F32, "d_att")
    g["w_att_out_t"] = _mm(dy_att, att, "tn", BF16, "g_att_out")
    do_g, dlt_g = _merge_bwd(o_g, lse_g, datt, nb, seq)
    dq_g, dk_g, dv_g = zip(*[_att_bwd(qg[i], kg[i], vg[i], do_g[i], lse_g[i], dlt_g[i], i, seq) for i in groups])
    dqkv = _rope_bwd(dq_g, dk_g, dv_g, cos, sin, nb, seq)

    dyn = _mm(dy_ssm, wts["w_ssm_out"], "nt", F32, "d_yn")
    g["w_ssm_out"] = _mm(yn, dy_ssm, "tn", BF16, "g_ssm_out")
    dy, dz, g["ssm_norm"] = _gate_norm_bwd(y, z, wts["ssm_norm"], dyn)
    dxc, ddtr, g_bias, g_alog, g_dskip = _ssd_bwd(xc, dtr, dtrt, bias_r, bias_c, a_r, a_c, dskip_r, states, dy,
                                                   nb, seq)
    g["dt_bias"] = g_bias.reshape(1, N_HEADS)
    g["a_log"] = g_alog.reshape(1, N_HEADS)
    g["d_skip"] = g_dskip.reshape(1, N_HEADS)
    dpre, g_conv_w, g_conv_b = _conv_bwd_pre(xbc, conv_w, conv_b, dxc, seq)
    g["conv_w"] = _group_major_inv(g_conv_w, 1)
    g["conv_b"] = _group_major_inv(g_conv_b, 1)
    dxbc = _conv_bwd_in(dpre, conv_w, seq)
    ddt = jnp.pad(ddtr.transpose(1, 0, 2).reshape(t, N_HEADS), ((0, 0), (0, DT_PAD - N_HEADS))).astype(BF16)

    dh = _mm(dz, w_z, "nn", F32, "d_h_z")
    dh = _mm(dxbc, w_xbc, "nn", F32, "d_h_xbc", add=dh)
    dh = _mm(ddt, w_dt, "nn", F32, "d_h_dt", add=dh)
    dh = _mm(dqkv, w_qkv, "nn", F32, "d_h_qkv", add=dh)
    dh = _mm(dgate, w_gate, "nn", F32, "d_h_gate", add=dh)
    g["w_in_t"] = jnp.concatenate([
        _mm(dz, h, "tn", BF16, "g_in_z"),
        _group_major_inv(_mm(dxbc, h, "tn", BF16, "g_in_xbc"), 0),
        _mm(ddt, h, "tn", BF16, "g_in_dt")[:N_HEADS],
        _mm(dqkv, h, "tn", BF16, "g_in_qkv"),
        _mm(dgate, h, "tn", BF16, "g_in_gate")], axis=0)
    dx, _, g["norm_mix"] = _rms_bwd(x, dh, wts["norm_mix"], dx1, "rms_mix_bwd")
    return loss[0, 0], dx.reshape(nb, seq, d), g


def kernel(x, norm_mix, w_in, b_gate, conv_w, conv_b, dt_bias, a_log, d_skip, ssm_norm, w_ssm_out, w_att_out, w_mix_out, norm_ffn, w_ffn_gate, w_ffn_up, w_ffn_down, norm_final, loss_target, m_norm_mix, m_w_in, m_b_gate, m_conv_w, m_conv_b, m_dt_bias, m_a_log, m_d_skip, m_ssm_norm, m_w_ssm_out, m_w_att_out, m_w_mix_out, m_norm_ffn, m_w_ffn_gate, m_w_ffn_up, m_w_ffn_down, m_norm_final, v_norm_mix, v_w_in, v_b_gate, v_conv_w, v_conv_b, v_dt_bias, v_a_log, v_d_skip, v_ssm_norm, v_w_ssm_out, v_w_att_out, v_w_mix_out, v_norm_ffn, v_w_ffn_gate, v_w_ffn_up, v_w_ffn_down, v_norm_final):
    names = ("norm_mix", "w_in", "b_gate", "conv_w", "conv_b", "dt_bias", "a_log", "d_skip", "ssm_norm", "w_ssm_out",
             "w_att_out", "w_mix_out", "norm_ffn", "w_ffn_gate", "w_ffn_up", "w_ffn_down", "norm_final")
    w_loc = dict(zip(names, (norm_mix, w_in, b_gate, conv_w, conv_b, dt_bias, a_log, d_skip, ssm_norm, w_ssm_out,
                             w_att_out, w_mix_out, norm_ffn, w_ffn_gate, w_ffn_up, w_ffn_down, norm_final)))
    m_loc = dict(zip(names, (m_norm_mix, m_w_in, m_b_gate, m_conv_w, m_conv_b, m_dt_bias, m_a_log, m_d_skip,
                             m_ssm_norm, m_w_ssm_out, m_w_att_out, m_w_mix_out, m_norm_ffn, m_w_ffn_gate,
                             m_w_ffn_up, m_w_ffn_down, m_norm_final)))
    v_loc = dict(zip(names, (v_norm_mix, v_w_in, v_b_gate, v_conv_w, v_conv_b, v_dt_bias, v_a_log, v_d_skip,
                             v_ssm_norm, v_w_ssm_out, v_w_att_out, v_w_mix_out, v_norm_ffn, v_w_ffn_gate,
                             v_w_ffn_up, v_w_ffn_down, v_norm_final)))
    two_d = lambda a: a.reshape(a.shape[-2:]) if a.ndim >= 2 else a.reshape(1, -1)
    w2 = {n: two_d(a) for n, a in w_loc.items()}
    chip = 2 * lax.axis_index("x") + lax.axis_index("y")
    c = lax.axis_index("c")

    wire_shapes = {n: _wire_shard(w2[n], n).shape for n in SHARDED}
    true_rows = {n: wire_shapes[n][0] * wire_shapes[n][1] // PACK_W for n in SHARDED}
    seg_rows = {n: _rows(wire_shapes[n][0] * wire_shapes[n][1]) for n in SHARDED}
    w_rows = _padded_rows(sum(seg_rows.values()))
    wp = _pack_rows([_wire_shard(w2[n], n).astype(BF16) for n in SHARDED], w_rows).reshape(2, w_rows // 2, PACK_W)
    wg = lax.dynamic_update_index_in_dim(_gather_weights(wp), wp, chip, 0).reshape(N_CHIPS, w_rows, PACK_W)
    full = {}
    off = 0
    for n in SHARDED:
        rows, cols = wire_shapes[n]
        full[_wire_name(n)] = wg[:, off:off + true_rows[n]].reshape(N_CHIPS * rows, cols)
        off += seg_rows[n]
    for n in SMALL:
        full[n] = w2[n]

    n_conv = w2["conv_w"].shape[1]
    placed = lax.dynamic_update_slice_in_dim(jnp.zeros((CONV_K, N_CHIPS * n_conv), F32), w2["conv_w"], chip * n_conv, 1)
    placed = jnp.where(c == 0, placed, 0.0)
    full["conv_w"] = _allreduce_small(_pack_rows([placed], _rows(int(placed.size))), "gather_conv_w").reshape(
        -1)[:placed.size].reshape(placed.shape)

    loss_sum, grad_x, g_full = _local_step(x, loss_target, full)
    loss = lax.psum(loss_sum, ("x", "y", "c"))

    g_shard = {}
    small_names = SMALL + ("conv_w",)
    small_flat = jnp.concatenate([g_full[n].reshape(-1) for n in small_names])
    small = _allreduce_small(_pack_rows([small_flat], _rows(int(small_flat.size))), "allreduce_small").reshape(-1)
    off = 0
    for n in small_names:
        size = int(g_full[n].size)
        g_shard[n] = small[off:off + size].reshape(g_full[n].shape)
        off += size
    g_shard["conv_w"] = lax.dynamic_slice_in_dim(g_shard["conv_w"], chip * n_conv, n_conv, 1)

    sections = [_pack_rows([g_full[_wire_name(n)].reshape(N_CHIPS, true_rows[n], PACK_W)[k] for n in SHARDED], w_rows)
                for k in range(N_CHIPS)]
    g2 = jnp.stack(sections).reshape(N_CHIPS, 2, w_rows // 2, PACK_W).transpose(1, 0, 2, 3)
    chip_sum = _add_own_half(g2, _swap_halves(g2), c)
    own = lax.dynamic_index_in_dim(chip_sum, chip, 0, keepdims=False)
    by_source = lax.dynamic_update_index_in_dim(_scatter_to_chips(chip_sum), own, chip, 0)
    half = _sum_chips(by_source)
    reduced = lax.dynamic_update_index_in_dim(_join_halves(half), half, c, 0).reshape(w_rows, PACK_W)
    off = 0
    for n in SHARDED:
        wire = reduced[off:off + true_rows[n]].reshape(wire_shapes[n])
        g_shard[n] = wire.T if n in COL_SHARDED else wire
        off += seg_rows[n]

    grads, deltas, new_m, new_v = [], [], [], []
    for n in names:
        shape = w_loc[n].shape
        d_, m_, v_ = _adamw(w2[n], g_shard[n], two_d(m_loc[n]), two_d(v_loc[n]), "adamw_" + n)
        grads.append(g_shard[n].reshape(shape))
        deltas.append(d_.reshape(shape))
        new_m.append(m_.reshape(shape))
        new_v.append(v_.reshape(shape))
    return (loss, grad_x, *grads, *deltas, *new_m, *new_v)
```

```python
import functools
import math

import jax
import jax.numpy as jnp
from jax import lax
from jax.experimental import pallas as pl
from jax.experimental.pallas import tpu as pltpu

F32 = jnp.float32
BF16 = jnp.bfloat16
SDS = jax.ShapeDtypeStruct
MESH = pl.DeviceIdType.MESH

D_MODEL = 1024
D_INNER = 2048
N_HEADS = 32
HEAD_P = 64
N_GROUPS = 4
HEADS_PER_GROUP = N_HEADS // N_GROUPS
D_STATE = 128
CONV_K = 4
CHUNK = 128
CONV_DIM = D_INNER + 2 * N_GROUPS * D_STATE
GROUP_W = D_INNER // N_GROUPS + 2 * D_STATE
ATT_HEADS = 12
ATT_D = 128
ATT_SLOTS = 4
ATT_W = ATT_SLOTS * ATT_D
ATT_DILATIONS = (1, 4, 16)
ATT_BLOCK = 128
QKV_DIM = 3 * ATT_HEADS * ATT_D
D_FF = 2816
DT_PAD = 128
ROPE_THETA = 10000.0
EPS = 1e-6
N_CHIPS = 4
LANES = 128

ADAM_LR = 0.001
ADAM_B1 = 0.9
ADAM_B2 = 0.999
ADAM_EPS = 1e-08
ADAM_WD = 0.01
ADAM_STEP = 10

VMEM_LIMIT = 48 * 1024 * 1024


def _cparams(semantics):
    return pltpu.CompilerParams(dimension_semantics=semantics, vmem_limit_bytes=VMEM_LIMIT)


def _pick(n, cap):
    best = None
    for t in range(LANES, min(n, cap) + 1, LANES):
        if n % t == 0:
            best = t
    return best or n


def _row_tile(rows, cap):
    best = None
    for t in range(8, min(rows, cap) + 1, 8):
        if rows % t == 0:
            best = t
    return best or rows


def _sigmoid(x):
    return 1.0 / (1.0 + jnp.exp(-x))


def _softplus(x):
    return jnp.maximum(x, 0.0) + jnp.log(1.0 + jnp.exp(-jnp.abs(x)))


def _dot(a, b):
    return jnp.dot(a, b, preferred_element_type=F32)


def _dot_nt(a, b):
    return lax.dot_general(a, b, (((1,), (1,)), ((), ())), preferred_element_type=F32)


def _dot_tn(a, b):
    return lax.dot_general(a, b, (((0,), (0,)), ((), ())), preferred_element_type=F32)


def _mm(a, b, mode, out_dtype, name, add=None):
    if mode == "nn":
        (m, k), (_, n) = a.shape, b.shape
    elif mode == "nt":
        (m, k), (n, _) = a.shape, b.shape
    else:
        (k, m), (_, n) = a.shape, b.shape
    tm, tn = _pick(m, 512), _pick(n, 1536)
    tk = k if k <= 2048 else _pick(k, 2048)
    nk = k // tk
    dims = {"nn": ((1,), (0,)), "nt": ((1,), (1,)), "tn": ((0,), (0,))}[mode]

    def partial_product(a_ref, b_ref):
        return lax.dot_general(a_ref[...].astype(BF16), b_ref[...].astype(BF16), (dims, ((), ())),
                               preferred_element_type=F32)

    def body(*refs):
        a_ref, b_ref = refs[:2]
        c_ref = refs[2] if add is not None else None
        o_ref = refs[3] if add is not None else refs[2]

        def finish(r):
            if add is not None:
                r = r + c_ref[...].astype(F32)
            o_ref[...] = r.astype(out_dtype)

        if nk == 1:
            finish(partial_product(a_ref, b_ref))
            return
        acc = refs[-1]
        kk = pl.program_id(2)

        @pl.when(kk == 0)
        def _():
            acc[...] = partial_product(a_ref, b_ref)

        @pl.when((kk > 0) & (kk < nk - 1))
        def _():
            acc[...] += partial_product(a_ref, b_ref)

        @pl.when(kk == nk - 1)
        def _():
            finish(acc[...] + partial_product(a_ref, b_ref))

    a_spec = {"nn": pl.BlockSpec((tm, tk), lambda j, i, q: (i, q)),
              "nt": pl.BlockSpec((tm, tk), lambda j, i, q: (i, q)),
              "tn": pl.BlockSpec((tk, tm), lambda j, i, q: (q, i))}[mode]
    b_spec = {"nn": pl.BlockSpec((tk, tn), lambda j, i, q: (q, j)),
              "nt": pl.BlockSpec((tn, tk), lambda j, i, q: (j, q)),
              "tn": pl.BlockSpec((tk, tn), lambda j, i, q: (q, j))}[mode]
    o_spec = pl.BlockSpec((tm, tn), lambda j, i, q: (i, j))
    ins, specs = [a, b], [a_spec, b_spec]
    if add is not None:
        ins.append(add)
        specs.append(o_spec)
    return pl.pallas_call(
        body, name=name, grid=(n // tn, m // tm, nk), in_specs=specs, out_specs=o_spec,
        out_shape=SDS((m, n), out_dtype), scratch_shapes=[pltpu.VMEM((tm, tn), F32)] if nk > 1 else [],
        compiler_params=_cparams(("parallel", "parallel", "arbitrary")))(*ins)


def _rw(name, fn, nsteps, ins, outs, n_acc=0):
    n_in, n_out = len(ins), len(outs)

    def body(*refs):
        i = pl.program_id(0)
        vals = fn(i, *refs[:n_in])
        for q, (r, v) in enumerate(zip(refs[n_in:], vals)):
            if q < n_out - n_acc:
                r[...] = v.astype(r.dtype)
            else:
                @pl.when(i == 0)
                def _(r=r):
                    r[...] = jnp.zeros_like(r)

                r[...] += v

    return pl.pallas_call(
        body, name=name, grid=(nsteps,), in_specs=[s for _, s in ins], out_specs=[s for _, s in outs],
        out_shape=[o for o, _ in outs], compiler_params=_cparams(("arbitrary",)))(*[a for a, _ in ins])


def _rs(tm, w, cb=0):
    return pl.BlockSpec((tm, w), lambda i: (i, cb))


def _fs(shape):
    nd = len(shape)
    return pl.BlockSpec(shape, lambda i: (0,) * nd)


def _colsum(v):
    return jnp.sum(v, axis=0, keepdims=True)


def _rms_fwd(x, g, name):
    t, d = x.shape
    tm = 512

    def fn(i, x_ref, g_ref):
        xv = x_ref[...]
        r = lax.rsqrt(jnp.mean(xv * xv, axis=-1, keepdims=True) + EPS)
        return [xv * r * g_ref[...]]

    return _rw(name, fn, t // tm, [(x, _rs(tm, d)), (g, _fs((1, d)))], [(SDS((t, d), BF16), _rs(tm, d))])[0]


def _rms_bwd(x, dh, g, dres, name):
    t, d = x.shape
    tm = 512

    def fn(i, x_ref, dh_ref, g_ref, dres_ref):
        xv = x_ref[...]
        r = lax.rsqrt(jnp.mean(xv * xv, axis=-1, keepdims=True) + EPS)
        xhat = xv * r
        dhv = dh_ref[...]
        dxhat = dhv * g_ref[...]
        dx = dres_ref[...] + r * (dxhat - xhat * jnp.mean(dxhat * xhat, axis=-1, keepdims=True))
        return [dx, dx, _colsum(dhv * xhat)]

    return _rw(name, fn, t // tm,
               [(x, _rs(tm, d)), (dh, _rs(tm, d)), (g, _fs((1, d))), (dres, _rs(tm, d))],
               [(SDS((t, d), F32), _rs(tm, d)), (SDS((t, d), BF16), _rs(tm, d)), (SDS((1, d), F32), _fs((1, d)))],
               n_acc=1)


def _final_fwd_bwd(x2, target, g):
    t, d = x2.shape
    tm = 512

    def fn(i, x_ref, t_ref, g_ref):
        xv = x_ref[...]
        gv = g_ref[...]
        r = lax.rsqrt(jnp.mean(xv * xv, axis=-1, keepdims=True) + EPS)
        xhat = xv * r
        diff = xhat * gv - t_ref[...]
        lsum = 0.5 * jnp.sum(jnp.sum(diff * diff, axis=-1, keepdims=True) * (1.0 / d), axis=0, keepdims=True)
        dy = diff * (1.0 / d)
        dxhat = dy * gv
        dx = r * (dxhat - xhat * jnp.mean(dxhat * xhat, axis=-1, keepdims=True))
        return [dx, dx, _colsum(dy * xhat), lsum]

    return _rw("final_norm_loss", fn, t // tm,
               [(x2, _rs(tm, d)), (target, _rs(tm, d)), (g, _fs((1, d)))],
               [(SDS((t, d), F32), _rs(tm, d)), (SDS((t, d), BF16), _rs(tm, d)), (SDS((1, d), F32), _fs((1, d))),
                (SDS((1, 1), F32), _fs((1, 1)))], n_acc=2)


CONV_TS = 512
CONV_HALO = 8


def _conv_specs(seq, c):
    ts, tc = CONV_TS, GROUP_W
    hb = ts // CONV_HALO
    u_spec = pl.BlockSpec((ts, tc), lambda j, i: (i, j))
    prev_spec = pl.BlockSpec((CONV_HALO, tc), lambda j, i: (jnp.maximum(i * hb - 1, 0), j))
    w_spec = pl.BlockSpec((CONV_K, tc), lambda j, i: (0, j))
    b_spec = pl.BlockSpec((1, tc), lambda j, i: (0, j))
    return u_spec, prev_spec, w_spec, b_spec


CONV_ROWS = 16


def _conv_fill(i, seq, u_ref, prev_ref, ext):
    first = (i % (seq // CONV_TS)) == 0
    ext[0:CONV_HALO, :] = jnp.where(first, 0.0, prev_ref[...])
    ext[CONV_HALO:, :] = u_ref[...]


def _conv_taps(ext, r0):
    blk = ext[pl.ds(r0, CONV_ROWS + CONV_HALO), :]
    lo = CONV_HALO - CONV_K + 1
    return [blk[lo + q:lo + q + CONV_ROWS] for q in range(CONV_K)]


def _conv_fwd(u, w, b, seq):
    t, c = u.shape
    ts, tc = CONV_TS, GROUP_W
    u_spec, prev_spec, w_spec, b_spec = _conv_specs(seq, c)

    def body(u_ref, prev_ref, w_ref, b_ref, o_ref, ext):
        _conv_fill(pl.program_id(1), seq, u_ref, prev_ref, ext)
        wv, bv = w_ref[...], b_ref[...]

        def rows(j, carry):
            r0 = pl.multiple_of(j * CONV_ROWS, CONV_ROWS)
            pre = bv
            for q, tap in enumerate(_conv_taps(ext, r0)):
                pre = pre + wv[q:q + 1] * tap
            o_ref[pl.ds(r0, CONV_ROWS), :] = pre * _sigmoid(pre)
            return carry

        lax.fori_loop(0, ts // CONV_ROWS, rows, 0)

    return pl.pallas_call(
        body, name="conv_fwd", grid=(c // tc, t // ts), in_specs=[u_spec, prev_spec, w_spec, b_spec],
        out_specs=u_spec, out_shape=SDS((t, c), F32), scratch_shapes=[pltpu.VMEM((ts + CONV_HALO, tc), F32)],
        compiler_params=_cparams(("parallel", "arbitrary")))(u, u, w, b)


def _conv_bwd_pre(u, w, b, dxc, seq):
    t, c = u.shape
    ts, tc = CONV_TS, GROUP_W
    u_spec, prev_spec, w_spec, b_spec = _conv_specs(seq, c)

    def body(u_ref, prev_ref, w_ref, b_ref, d_ref, dpre_ref, dw_ref, db_ref, ext):
        i = pl.program_id(1)
        _conv_fill(i, seq, u_ref, prev_ref, ext)
        wv, bv = w_ref[...], b_ref[...]
        fold = lambda v: sum(v[8 * s:8 * (s + 1)] for s in range(CONV_ROWS // 8))

        def rows(j, sums):
            r0 = pl.multiple_of(j * CONV_ROWS, CONV_ROWS)
            taps = _conv_taps(ext, r0)
            pre = bv
            for q, tap in enumerate(taps):
                pre = pre + wv[q:q + 1] * tap
            sg = _sigmoid(pre)
            dpre = d_ref[pl.ds(r0, CONV_ROWS), :] * sg * (1.0 + pre * (1.0 - sg))
            dpre_ref[pl.ds(r0, CONV_ROWS), :] = dpre
            return tuple(s + fold(dpre * f) for s, f in zip(sums, taps + [1.0]))

        zero = jnp.zeros((8, tc), F32)
        sums = lax.fori_loop(0, ts // CONV_ROWS, rows, (zero,) * (CONV_K + 1))

        @pl.when(i == 0)
        def _():
            dw_ref[...] = jnp.zeros_like(dw_ref)
            db_ref[...] = jnp.zeros_like(db_ref)

        db_ref[...] += _colsum(sums[CONV_K])
        for q in range(CONV_K):
            dw_ref[q:q + 1, :] += _colsum(sums[q])

    return pl.pallas_call(
        body, name="conv_bwd_pre", grid=(c // tc, t // ts),
        in_specs=[u_spec, prev_spec, w_spec, b_spec, u_spec], out_specs=[u_spec, w_spec, b_spec],
        out_shape=[SDS((t, c), F32), SDS((CONV_K, c), F32), SDS((1, c), F32)],
        scratch_shapes=[pltpu.VMEM((ts + CONV_HALO, tc), F32)],
        compiler_params=_cparams(("parallel", "arbitrary")))(u, u, w, b, dxc)


def _conv_bwd_in(dpre, w, seq):
    t, c = dpre.shape
    ts, tc = CONV_TS, GROUP_W
    hb = ts // CONV_HALO
    last = t // CONV_HALO - 1
    d_spec = pl.BlockSpec((ts, tc), lambda j, i: (i, j))
    next_spec = pl.BlockSpec((CONV_HALO, tc), lambda j, i: (jnp.minimum((i + 1) * hb, last), j))
    w_spec = pl.BlockSpec((CONV_K, tc), lambda j, i: (0, j))

    def body(d_ref, next_ref, w_ref, o_ref, ext):
        i = pl.program_id(1)
        nts = seq // ts
        is_last = (i % nts) == nts - 1
        ext[0:ts, :] = d_ref[...]
        ext[ts:, :] = jnp.where(is_last, 0.0, next_ref[...])
        wv = w_ref[...]

        def rows(j, carry):
            r0 = pl.multiple_of(j * CONV_ROWS, CONV_ROWS)
            blk = ext[pl.ds(r0, CONV_ROWS + CONV_HALO), :]
            acc = wv[CONV_K - 1:CONV_K] * blk[0:CONV_ROWS]
            for q in range(CONV_K - 1):
                acc = acc + wv[q:q + 1] * blk[CONV_K - 1 - q:CONV_K - 1 - q + CONV_ROWS]
            o_ref[pl.ds(r0, CONV_ROWS), :] = acc.astype(o_ref.dtype)
            return carry

        lax.fori_loop(0, ts // CONV_ROWS, rows, 0)

    return pl.pallas_call(
        body, name="conv_bwd_in", grid=(c // tc, t // ts), in_specs=[d_spec, next_spec, w_spec],
        out_specs=d_spec, out_shape=SDS((t, c), BF16), scratch_shapes=[pltpu.VMEM((ts + CONV_HALO, tc), F32)],
        compiler_params=_cparams(("parallel", "arbitrary")))(dpre, dpre, w)


def _split3(v):
    hi = v.astype(BF16)
    r1 = v - hi.astype(F32)
    mid = r1.astype(BF16)
    lo = (r1 - mid.astype(F32)).astype(BF16)
    return hi, mid, lo


def _ssd_prelude(dtr_ref, dtrt_ref, bias_ref, biast_ref, a_ref, at_ref):
    dt = _softplus(dtr_ref[...] + bias_ref[...])
    dtt = _softplus(dtrt_ref[...] + biast_ref[...])
    ri = lax.broadcasted_iota(jnp.int32, (CHUNK, CHUNK), 0)
    ci = lax.broadcasted_iota(jnp.int32, (CHUNK, CHUNK), 1)
    lower = ri >= ci
    upper = ri <= ci
    lower_b = jnp.where(lower, 1.0, 0.0).astype(BF16)
    upper_b = jnp.where(upper, 1.0, 0.0).astype(BF16)
    acs = sum(_dot(lower_b, p) for p in _split3(dt * a_ref[...]))
    acst = sum(_dot(p, upper_b) for p in _split3(dtt * at_ref[...]))
    return dt, acs, acst, lower, upper, lower_b, upper_b


def _ssd_specs(seq):
    nc = seq // CHUNK
    hg = HEADS_PER_GROUP
    row = lambda cc: (lambda g, b, c: (b * nc + cc(c), g))
    fwd = lambda c: c
    rev = lambda c: nc - 1 - c

    def specs(cc):
        return dict(
            xc=pl.BlockSpec((CHUNK, GROUP_W), lambda g, b, c: (b * nc + cc(c), g)),
            y=pl.BlockSpec((CHUNK, D_INNER // N_GROUPS), lambda g, b, c: (b * nc + cc(c), g)),
            dtr=pl.BlockSpec((None, CHUNK, hg), lambda g, b, c: (g, b * nc + cc(c), 0)),
            dtrt=pl.BlockSpec((None, None, hg, CHUNK), lambda g, b, c: (g, b, 0, cc(c))),
            prow=pl.BlockSpec((None, 1, hg), lambda g, b, c: (g, 0, 0)),
            pcol=pl.BlockSpec((None, hg, 1), lambda g, b, c: (g, 0, 0)),
            st=pl.BlockSpec((None, None, None, D_STATE, hg * HEAD_P), lambda g, b, c: (g, b, cc(c), 0, 0)),
        )

    return specs(fwd), specs(rev)


def _head_maps():
    hw = HEADS_PER_GROUP * HEAD_P
    shift = HEAD_P.bit_length() - 1
    hj = lax.broadcasted_iota(jnp.int32, (HEADS_PER_GROUP, hw), 0)
    lq = jnp.right_shift(lax.broadcasted_iota(jnp.int32, (HEADS_PER_GROUP, hw), 1), shift)
    spread = jnp.where(hj == lq, 1.0, 0.0).astype(BF16)
    rq = jnp.right_shift(lax.broadcasted_iota(jnp.int32, (hw, LANES), 0), shift)
    cj = lax.broadcasted_iota(jnp.int32, (hw, LANES), 1)
    gather = jnp.where(rq == cj, 1.0, 0.0).astype(BF16)
    return spread, gather


def _exact_dot(v, m01):
    return sum(_dot(p, m01) for p in _split3(v))


def _ssd_fwd(xc, dtr, dtrt, bias, biast, a, at, dskip, nb, seq):
    t = xc.shape[0]
    nc = seq // CHUNK
    hg = HEADS_PER_GROUP
    hw = hg * HEAD_P
    sp, _ = _ssd_specs(seq)

    def body(xc_ref, dtr_ref, dtrt_ref, bias_ref, biast_ref, a_ref, at_ref, d_ref, y_ref, sin_ref, st):
        @pl.when(pl.program_id(2) == 0)
        def _():
            st[...] = jnp.zeros_like(st)

        s_in = st[...]
        sin_ref[...] = s_in
        dt, acs, acst, lower, _, _, _ = _ssd_prelude(dtr_ref, dtrt_ref, bias_ref, biast_ref, a_ref, at_ref)
        spread, _ = _head_maps()
        x = xc_ref[...]
        xs = x[:, :hw]
        b16 = x[:, hw:hw + D_STATE].astype(BF16)
        c16 = x[:, hw + D_STATE:].astype(BF16)
        cb = _dot_nt(c16, b16)
        last = acs[CHUNK - 1:CHUNK, :]
        e_x = _exact_dot(jnp.exp(acs), spread)
        dec_x = _exact_dot(jnp.exp(last - acs), spread)
        tot_x = e_x[CHUNK - 1:CHUNK, :]
        d_x = _exact_dot(jnp.broadcast_to(d_ref[...], (8, hg)), spread)[0:1, :]
        xdtf = xs * _exact_dot(dt, spread)
        xdt16 = xdtf.astype(BF16)
        yoff = e_x * _dot(c16, s_in.astype(BF16))
        st[...] = tot_x * s_in + _dot_tn(b16, (dec_x * xdtf).astype(BF16))
        parts = []
        for j in range(hg):
            decay = jnp.exp(jnp.where(lower, acs[:, j:j + 1] - acst[j:j + 1, :], -jnp.inf))
            parts.append(_dot((cb * decay).astype(BF16), xdt16[:, HEAD_P * j:HEAD_P * (j + 1)]))
        y_ref[...] = jnp.concatenate(parts, axis=-1) + yoff + d_x * xs

    return pl.pallas_call(
        body, name="ssd_fwd", grid=(N_GROUPS, nb, nc),
        in_specs=[sp["xc"], sp["dtr"], sp["dtrt"], sp["prow"], sp["pcol"], sp["prow"], sp["pcol"], sp["prow"]],
        out_specs=[sp["y"], sp["st"]],
        out_shape=[SDS((t, D_INNER), F32), SDS((N_GROUPS, nb, nc, D_STATE, hw), F32)],
        scratch_shapes=[pltpu.VMEM((D_STATE, hw), F32)],
        compiler_params=_cparams(("parallel", "parallel", "arbitrary")))(xc, dtr, dtrt, bias, biast, a, at, dskip)


def _ssd_bwd(xc, dtr, dtrt, bias, biast, a, at, dskip, states, dy, nb, seq):
    t = xc.shape[0]
    nc = seq // CHUNK
    hg = HEADS_PER_GROUP
    hw = hg * HEAD_P
    _, sp = _ssd_specs(seq)

    def body(xc_ref, dtr_ref, dtrt_ref, bias_ref, biast_ref, a_ref, at_ref, d_ref, sin_ref, dy_ref,
             dxc_ref, ddtr_ref, gbias_ref, ga_ref, gd_ref, ds):
        first = (pl.program_id(1) == 0) & (pl.program_id(2) == 0)

        @pl.when(pl.program_id(2) == 0)
        def _():
            ds[...] = jnp.zeros_like(ds)

        @pl.when(first)
        def _():
            gbias_ref[...] = jnp.zeros_like(gbias_ref)
            ga_ref[...] = jnp.zeros_like(ga_ref)
            gd_ref[...] = jnp.zeros_like(gd_ref)

        dt, acs, acst, lower, upper, _, upper_b = _ssd_prelude(dtr_ref, dtrt_ref, bias_ref, biast_ref, a_ref, at_ref)
        spread, gather = _head_maps()
        x = xc_ref[...]
        dy = dy_ref[...]
        xs = x[:, :hw]
        b16 = x[:, hw:hw + D_STATE].astype(BF16)
        c16 = x[:, hw + D_STATE:].astype(BF16)
        dy16 = dy.astype(BF16)
        cb = _dot_nt(c16, b16)
        cbt = _dot_nt(b16, c16)
        last = acs[CHUNK - 1:CHUNK, :]
        e8 = jnp.exp(acs)
        dec8 = jnp.exp(last - acs)
        e_x = _exact_dot(e8, spread)
        dec_x = _exact_dot(dec8, spread)
        tot_x = e_x[CHUNK - 1:CHUNK, :]
        dt_x = _exact_dot(dt, spread)
        d_x = _exact_dot(jnp.broadcast_to(d_ref[...], (8, hg)), spread)[0:1, :]
        xdtf = xs * dt_x
        xdt16 = xdtf.astype(BF16)
        s_in = sin_ref[...]
        s16 = s_in.astype(BF16)
        ds_out = ds[...]
        ds16 = ds_out.astype(BF16)
        bds = _dot(b16, ds16)
        cs = _dot(c16, s16)
        edy16 = (e_x * dy).astype(BF16)
        ds[...] = tot_x * ds_out + _dot_tn(c16, edy16)
        lane8 = lax.broadcasted_iota(jnp.int32, (CHUNK, hg), 1)
        row8 = lax.broadcasted_iota(jnp.int32, (CHUNK, hg), 0)
        dacs8 = jnp.zeros((CHUNK, hg), F32)
        acc_m = jnp.zeros((CHUNK, CHUNK), F32)
        acc_mt = jnp.zeros((CHUNK, CHUNK), F32)
        dx_parts = []
        for j in range(hg):
            sl = slice(HEAD_P * j, HEAD_P * (j + 1))
            col = acs[:, j:j + 1]
            row = acst[j:j + 1, :]
            decay = jnp.exp(jnp.where(lower, col - row, -jnp.inf))
            decayt = jnp.exp(jnp.where(upper, row - col, -jnp.inf))
            wm = _dot_nt(dy16[:, sl], xdt16[:, sl]) * decay
            wmt = _dot_nt(xdt16[:, sl], dy16[:, sl]) * decayt
            acc_m = acc_m + wm
            acc_mt = acc_mt + wmt
            dacs8 = dacs8 + jnp.where(lane8 == j, jnp.sum(wm * cb, axis=-1, keepdims=True)
                                      - jnp.sum(wmt * cbt, axis=-1, keepdims=True), 0.0)
            dx_parts.append(_dot((cbt * decayt).astype(BF16), dy16[:, sl]))
        dx = jnp.concatenate(dx_parts, axis=-1) + dec_x * bds
        dxc_ref[:, :hw] = dx * dt_x + d_x * dy
        dxc_ref[:, hw:hw + D_STATE] = _dot(acc_mt.astype(BF16), c16) + _dot_nt((dec_x * xdtf).astype(BF16), ds16)
        dxc_ref[:, hw + D_STATE:] = _dot(acc_m.astype(BF16), b16) + _dot_nt(edy16, s16)
        dtot_rows = jnp.broadcast_to(_colsum(ds_out * s_in), (8, hw))
        sums = _exact_dot(jnp.concatenate([dy * cs, xdtf * bds, dx * xs, dy * xs, dtot_rows], axis=0), gather)
        de8 = sums[0:CHUNK, :hg]
        ddec8 = sums[CHUNK:2 * CHUNK, :hg]
        ddtx8 = sums[2 * CHUNK:3 * CHUNK, :hg]
        gd8 = _colsum(sums[3 * CHUNK:4 * CHUNK, :hg])
        dtot8 = sums[4 * CHUNK:4 * CHUNK + 1, :hg]
        extra = _colsum(ddec8 * dec8) + dtot8 * e8[CHUNK - 1:CHUNK, :]
        dacs8 = dacs8 + de8 * e8 - ddec8 * dec8 + jnp.where(row8 == CHUNK - 1, extra, 0.0)
        da = sum(_dot(upper_b, p) for p in _split3(dacs8))
        av = a_ref[...]
        ddt = da * av + ddtx8
        ddtr = ddt * _sigmoid(dtr_ref[...] + bias_ref[...])
        ddtr_ref[...] = ddtr
        gbias_ref[...] += _colsum(ddtr)
        ga_ref[...] += _colsum(da * dt) * av
        gd_ref[...] += gd8

    return pl.pallas_call(
        body, name="ssd_bwd", grid=(N_GROUPS, nb, nc),
        in_specs=[sp["xc"], sp["dtr"], sp["dtrt"], sp["prow"], sp["pcol"], sp["prow"], sp["pcol"], sp["prow"],
                  sp["st"], sp["y"]],
        out_specs=[sp["xc"], sp["dtr"], sp["prow"], sp["prow"], sp["prow"]],
        out_shape=[SDS((t, N_GROUPS * GROUP_W), F32), SDS((N_GROUPS, t, hg), F32)]
        + [SDS((N_GROUPS, 1, hg), F32)] * 3,
        scratch_shapes=[pltpu.VMEM((D_STATE, hw), F32)],
        compiler_params=_cparams(("arbitrary", "arbitrary", "arbitrary")))(
            xc, dtr, dtrt, bias, biast, a, at, dskip, states, dy)


def _group_bcast(v, width, fn):
    parts = []
    for q in range(v.shape[-1] // width):
        s = fn(v[:, q * width:(q + 1) * width])
        parts.append(jnp.broadcast_to(s, (v.shape[0], width)))
    return jnp.concatenate(parts, axis=-1)


def _gate_norm_fwd(y, z, g):
    t, d = y.shape
    tm = 256
    gw = d // N_GROUPS

    def fn(i, y_ref, z_ref, g_ref):
        zv = z_ref[...]
        u = y_ref[...] * (zv * _sigmoid(zv))
        r = lax.rsqrt(_group_bcast(u * u, gw, lambda p: jnp.mean(p, axis=-1, keepdims=True)) + EPS)
        return [u * r * g_ref[...]]

    return _rw("gate_norm_fwd", fn, t // tm, [(y, _rs(tm, d)), (z, _rs(tm, d)), (g, _fs((1, d)))],
               [(SDS((t, d), BF16), _rs(tm, d))])[0]


def _gate_norm_bwd(y, z, g, dyn):
    t, d = y.shape
    tm = 256
    gw = d // N_GROUPS

    def fn(i, y_ref, z_ref, g_ref, dyn_ref):
        zv = z_ref[...]
        yv = y_ref[...]
        sg = _sigmoid(zv)
        sz = zv * sg
        u = yv * sz
        r = lax.rsqrt(_group_bcast(u * u, gw, lambda p: jnp.mean(p, axis=-1, keepdims=True)) + EPS)
        uhat = u * r
        dv = dyn_ref[...]
        duhat = dv * g_ref[...]
        du = r * (duhat - uhat * _group_bcast(duhat * uhat, gw, lambda p: jnp.mean(p, axis=-1, keepdims=True)))
        dz = du * yv * sg * (1.0 + zv * (1.0 - sg))
        return [du * sz, dz, _colsum(dv * uhat)]

    return _rw("gate_norm_bwd", fn, t // tm,
               [(y, _rs(tm, d)), (z, _rs(tm, d)), (g, _fs((1, d))), (dyn, _rs(tm, d))],
               [(SDS((t, d), F32), _rs(tm, d)), (SDS((t, d), BF16), _rs(tm, d)), (SDS((1, d), F32), _fs((1, d)))],
               n_acc=1)


def _rope_tables(seq):
    half = ATT_D // 2
    inv = ROPE_THETA ** (-jnp.arange(half, dtype=F32) / half)
    ang = jnp.arange(seq, dtype=F32)[:, None] * inv[None, :]
    cos, sin = jnp.cos(ang), jnp.sin(ang)
    return jnp.concatenate([cos, cos], axis=-1), jnp.concatenate([-sin, sin], axis=-1)


ATT_TILE = 512


def _strided_spec(r, mtiles):
    return pl.BlockSpec((None, r, None, ATT_TILE // r, ATT_W), lambda i: (i // mtiles, 0, i % mtiles, 0, 0))


def _strided_shape(nb, r, mtiles, dtype):
    return SDS((nb, r, mtiles, ATT_TILE // r, ATT_W), dtype)


def _to_strided(val, out_ref, lanes, r, sc):
    if r == 1:
        out_ref[0, :, lanes] = val.astype(out_ref.dtype)
        return
    sc[...] = val
    for rr in range(r):
        out_ref[rr, :, lanes] = sc[pl.ds(rr, ATT_TILE // r, stride=r), :].astype(out_ref.dtype)


def _from_strided(in_ref, lanes, r, sc):
    if r == 1:
        return in_ref[0, :, lanes]
    for rr in range(r):
        sc[pl.ds(rr, ATT_TILE // r, stride=r), :] = in_ref[rr, :, lanes]
    return sc[...]


def _rope_fwd(qkv, cos, sin, nb, seq):
    t = qkv.shape[0]
    tm = ATT_TILE
    mtiles = seq // tm
    w = ATT_HEADS * ATT_D
    tab = pl.BlockSpec((tm, ATT_D), lambda i: (i % mtiles, 0))
    ng = len(ATT_DILATIONS)

    def body(q_ref, k_ref, v_ref, cos_ref, sin_ref, *rest):
        outs, sc = rest[:3 * ng], rest[3 * ng]
        c, s = cos_ref[...], sin_ref[...]
        for which, ref in enumerate((q_ref, k_ref, v_ref)):
            for h in range(ATT_HEADS):
                g, slot = divmod(h, ATT_SLOTS)
                p = ref[:, h * ATT_D:(h + 1) * ATT_D]
                if which < 2:
                    p = p * c + pltpu.roll(p, ATT_D // 2, 1) * s
                _to_strided(p, outs[which * ng + g], slice(slot * ATT_D, (slot + 1) * ATT_D), ATT_DILATIONS[g], sc)

    out_specs = [_strided_spec(r, mtiles) for _ in range(3) for r in ATT_DILATIONS]
    out_shape = [_strided_shape(nb, r, mtiles, BF16) for _ in range(3) for r in ATT_DILATIONS]
    outs = pl.pallas_call(
        body, name="rope_fwd", grid=(t // tm,),
        in_specs=[_rs(tm, w, 0), _rs(tm, w, 1), _rs(tm, w, 2), tab, tab], out_specs=out_specs, out_shape=out_shape,
        scratch_shapes=[pltpu.VMEM((tm, ATT_D), F32)], compiler_params=_cparams(("arbitrary",)))(
            qkv, qkv, qkv, cos, sin)
    flat = [o.reshape(t, ATT_W) for o in outs]
    return flat[0:ng], flat[ng:2 * ng], flat[2 * ng:]


def _rope_bwd(dq, dk, dv, cos, sin, nb, seq):
    t = dq[0].shape[0]
    tm = ATT_TILE
    mtiles = seq // tm
    w = ATT_HEADS * ATT_D
    tab = pl.BlockSpec((tm, ATT_D), lambda i: (i % mtiles, 0))
    ng = len(ATT_DILATIONS)

    def body(*refs):
        ins, (cos_ref, sin_ref, o_ref, sc) = refs[:3 * ng], refs[3 * ng:]
        c, s = cos_ref[...], sin_ref[...]
        for which in range(3):
            for h in range(ATT_HEADS):
                g, slot = divmod(h, ATT_SLOTS)
                p = _from_strided(ins[which * ng + g], slice(slot * ATT_D, (slot + 1) * ATT_D), ATT_DILATIONS[g], sc)
                if which < 2:
                    p = p * c - pltpu.roll(p, ATT_D // 2, 1) * s
                o_ref[:, which * w + h * ATT_D:which * w + (h + 1) * ATT_D] = p.astype(o_ref.dtype)

    views = [a.reshape(nb, r, mtiles, tm // r, ATT_W) for grp in (dq, dk, dv) for a, r in zip(grp, ATT_DILATIONS)]
    return pl.pallas_call(
        body, name="rope_bwd", grid=(t // tm,),
        in_specs=[_strided_spec(r, mtiles) for _ in range(3) for r in ATT_DILATIONS] + [tab, tab],
        out_specs=_rs(tm, 3 * w), out_shape=SDS((t, 3 * w), BF16),
        scratch_shapes=[pltpu.VMEM((tm, ATT_D), F32)], compiler_params=_cparams(("arbitrary",)))(*views, cos, sin)


def _att_masks():
    ri = lax.broadcasted_iota(jnp.int32, (ATT_BLOCK, ATT_BLOCK), 0)
    ci = lax.broadcasted_iota(jnp.int32, (ATT_BLOCK, ATT_BLOCK), 1)
    return ci <= ri, ci >= ri


def _att_fwd(q, k, v, g, seq):
    t, w = q.shape
    nblk = t // ATT_BLOCK
    nbs = seq // ATT_DILATIONS[g] // ATT_BLOCK
    scale = ATT_D ** -0.5
    cur = pl.BlockSpec((ATT_BLOCK, w), lambda n: (n, 0))
    prev = pl.BlockSpec((ATT_BLOCK, w), lambda n: (jnp.maximum(n - 1, 0), 0))

    def body(q_ref, kc_ref, kp_ref, vc_ref, vp_ref, o_ref, lse_ref):
        has_prev = (pl.program_id(0) % nbs) != 0
        mcur, mprev = _att_masks()
        mask = jnp.concatenate([mprev & has_prev, mcur], axis=-1)
        for h in range(ATT_SLOTS):
            sl = slice(h * ATT_D, (h + 1) * ATT_D)
            keys = jnp.concatenate([kp_ref[:, sl], kc_ref[:, sl]], axis=0)
            vals = jnp.concatenate([vp_ref[:, sl], vc_ref[:, sl]], axis=0)
            s = jnp.where(mask, _dot_nt(q_ref[:, sl], keys) * scale, -jnp.inf)
            m = jnp.max(s, axis=-1, keepdims=True)
            p = jnp.exp(s - m)
            den = jnp.sum(p, axis=-1, keepdims=True)
            o_ref[:, sl] = _dot(p.astype(BF16), vals) / den
            lse_ref[:, sl] = jnp.broadcast_to(m + jnp.log(den), (ATT_BLOCK, ATT_D))

    return pl.pallas_call(
        body, name=f"att_fwd_{g}", grid=(nblk,), in_specs=[cur, cur, prev, cur, prev], out_specs=[cur, cur],
        out_shape=[SDS((t, w), F32), SDS((t, w), F32)],
        compiler_params=_cparams(("arbitrary",)))(q, k, k, v, v)


def _att_bwd(q, k, v, do, lse, dlt, g, seq):
    t, w = q.shape
    nblk = t // ATT_BLOCK
    nbs = seq // ATT_DILATIONS[g] // ATT_BLOCK
    scale = ATT_D ** -0.5
    cur = pl.BlockSpec((ATT_BLOCK, w), lambda n: (n, 0))
    nxt = pl.BlockSpec((ATT_BLOCK, w), lambda n: (jnp.minimum(n + 1, nblk - 1), 0))

    def body(qc_ref, qn_ref, k_ref, v_ref, doc_ref, don_ref, lsec_ref, lsen_ref, dltc_ref, dltn_ref,
             dq_ref, dk_ref, dv_ref, carry):
        n = pl.program_id(0)

        @pl.when((n % nbs) == 0)
        def _():
            carry[...] = jnp.zeros_like(carry)

        has_next = (((n + 1) % nbs) != 0) & (n + 1 < nblk)
        mcur, mprev = _att_masks()
        mask = jnp.concatenate([mcur, mprev & has_next], axis=0)
        for h in range(ATT_SLOTS):
            sl = slice(h * ATT_D, (h + 1) * ATT_D)
            kh, vh = k_ref[:, sl], v_ref[:, sl]
            qs = jnp.concatenate([qc_ref[:, sl], qn_ref[:, sl]], axis=0)
            dos = jnp.concatenate([doc_ref[:, sl], don_ref[:, sl]], axis=0)
            lse = jnp.concatenate([lsec_ref[:, sl], lsen_ref[:, sl]], axis=0)
            dlt = jnp.concatenate([dltc_ref[:, sl], dltn_ref[:, sl]], axis=0)
            p = jnp.where(mask, jnp.exp(_dot_nt(qs, kh) * scale - lse), 0.0)
            ds = (p * (_dot_nt(dos, vh) - dlt) * scale).astype(BF16)
            dqs = _dot(ds, kh)
            dq_ref[:, sl] = carry[:, sl] + dqs[:ATT_BLOCK]
            carry[:, sl] = dqs[ATT_BLOCK:]
            dk_ref[:, sl] = _dot_tn(ds, qs)
            dv_ref[:, sl] = _dot_tn(p.astype(BF16), dos)

    return pl.pallas_call(
        body, name=f"att_bwd_{g}", grid=(nblk,), in_specs=[cur, nxt, cur, cur, cur, nxt, cur, nxt, cur, nxt],
        out_specs=[cur, cur, cur], out_shape=[SDS((t, w), F32)] * 3,
        scratch_shapes=[pltpu.VMEM((ATT_BLOCK, w), F32)],
        compiler_params=_cparams(("arbitrary",)))(q, q, k, v, do, do, lse, lse, dlt, dlt)


def _merge_weights(ls):
    m = jnp.maximum(jnp.maximum(ls[0], ls[1]), ls[2])
    es = [jnp.exp(v - m) for v in ls]
    den = es[0] + es[1] + es[2]
    return [e / den for e in es]


def _merge_fwd(o, lse, nb, seq):
    t = o[0].shape[0]
    tm = ATT_TILE
    mtiles = seq // tm
    ng = len(ATT_DILATIONS)

    def body(*refs):
        o_refs, l_refs, out_ref, scs = refs[:ng], refs[ng:2 * ng], refs[2 * ng], refs[2 * ng + 1:]
        for slot in range(ATT_SLOTS):
            lanes = slice(slot * ATT_D, (slot + 1) * ATT_D)
            ov = [_from_strided(o_refs[g], lanes, r, scs[2 * g]) for g, r in enumerate(ATT_DILATIONS)]
            ws = _merge_weights([_from_strided(l_refs[g], lanes, r, scs[2 * g + 1])
                                 for g, r in enumerate(ATT_DILATIONS)])
            out_ref[:, lanes] = (ws[0] * ov[0] + ws[1] * ov[1] + ws[2] * ov[2]).astype(out_ref.dtype)

    views = [a.reshape(nb, r, mtiles, tm // r, ATT_W) for grp in (o, lse) for a, r in zip(grp, ATT_DILATIONS)]
    return pl.pallas_call(
        body, name="att_merge_fwd", grid=(t // tm,),
        in_specs=[_strided_spec(r, mtiles) for _ in range(2) for r in ATT_DILATIONS],
        out_specs=_rs(tm, ATT_W), out_shape=SDS((t, ATT_W), BF16),
        scratch_shapes=[pltpu.VMEM((tm, ATT_D), F32)] * (2 * ng), compiler_params=_cparams(("arbitrary",)))(*views)


def _merge_bwd(o, lse, datt, nb, seq):
    t = o[0].shape[0]
    tm = ATT_TILE
    mtiles = seq // tm
    ng = len(ATT_DILATIONS)

    def body(*refs):
        o_refs, l_refs, d_ref = refs[:ng], refs[ng:2 * ng], refs[2 * ng]
        do_refs, dlt_refs = refs[2 * ng + 1:3 * ng + 1], refs[3 * ng + 1:4 * ng + 1]
        scs = refs[4 * ng + 1:]
        for slot in range(ATT_SLOTS):
            lanes = slice(slot * ATT_D, (slot + 1) * ATT_D)
            ov = [_from_strided(o_refs[g], lanes, r, scs[2 * g]) for g, r in enumerate(ATT_DILATIONS)]
            ws = _merge_weights([_from_strided(l_refs[g], lanes, r, scs[2 * g + 1])
                                 for g, r in enumerate(ATT_DILATIONS)])
            dv = d_ref[:, lanes]
            att = ws[0] * ov[0] + ws[1] * ov[1] + ws[2] * ov[2]
            dot = jnp.broadcast_to(jnp.sum(dv * att, axis=-1, keepdims=True), (tm, ATT_D))
            for g, r in enumerate(ATT_DILATIONS):
                _to_strided(ws[g] * dv, do_refs[g], lanes, r, scs[2 * ng])
                _to_strided(ws[g] * dot, dlt_refs[g], lanes, r, scs[2 * ng + 1])

    views = [a.reshape(nb, r, mtiles, tm // r, ATT_W) for grp in (o, lse) for a, r in zip(grp, ATT_DILATIONS)]
    outs = pl.pallas_call(
        body, name="att_merge_bwd", grid=(t // tm,),
        in_specs=[_strided_spec(r, mtiles) for _ in range(2) for r in ATT_DILATIONS] + [_rs(tm, ATT_W)],
        out_specs=[_strided_spec(r, mtiles) for _ in range(2) for r in ATT_DILATIONS],
        out_shape=[_strided_shape(nb, r, mtiles, dt) for dt in (BF16, F32) for r in ATT_DILATIONS],
        scratch_shapes=[pltpu.VMEM((tm, ATT_D), F32)] * (2 * ng + 2), compiler_params=_cparams(("arbitrary",)))(
            *views, datt)
    flat = [a.reshape(t, ATT_W) for a in outs]
    return flat[:ng], flat[ng:]


def _mix_fwd(gate_logits, b_gate, y_ssm, y_att):
    t, d = y_ssm.shape
    tm = 512

    def fn(i, g0_ref, g1_ref, b0_ref, b1_ref, ys_ref, ya_ref):
        g0 = _sigmoid(g0_ref[...] + b0_ref[...])
        g1 = _sigmoid(g1_ref[...] + b1_ref[...])
        return [g0 * ys_ref[...] + g1 * ya_ref[...]]

    b_spec = lambda cb: pl.BlockSpec((1, d), lambda i: (0, cb))
    return _rw("mix_fwd", fn, t // tm,
               [(gate_logits, _rs(tm, d, 0)), (gate_logits, _rs(tm, d, 1)), (b_gate, b_spec(0)), (b_gate, b_spec(1)),
                (y_ssm, _rs(tm, d)), (y_att, _rs(tm, d))],
               [(SDS((t, d), BF16), _rs(tm, d))])[0]


def _mix_bwd(gate_logits, b_gate, y_ssm, y_att, dmixed):
    t, d = y_ssm.shape
    tm = 256

    def fn(i, g0_ref, g1_ref, b0_ref, b1_ref, ys_ref, ya_ref, dm_ref):
        g0 = _sigmoid(g0_ref[...] + b0_ref[...])
        g1 = _sigmoid(g1_ref[...] + b1_ref[...])
        dm = dm_ref[...]
        dg = jnp.concatenate([dm * ys_ref[...] * g0 * (1.0 - g0), dm * ya_ref[...] * g1 * (1.0 - g1)], axis=-1)
        return [dm * g0, dm * g1, dg, _colsum(dg)]

    b_spec = lambda cb: pl.BlockSpec((1, d), lambda i: (0, cb))
    return _rw("mix_bwd", fn, t // tm,
               [(gate_logits, _rs(tm, d, 0)), (gate_logits, _rs(tm, d, 1)), (b_gate, b_spec(0)), (b_gate, b_spec(1)),
                (y_ssm, _rs(tm, d)), (y_att, _rs(tm, d)), (dmixed, _rs(tm, d))],
               [(SDS((t, d), BF16), _rs(tm, d)), (SDS((t, d), BF16), _rs(tm, d)),
                (SDS((t, 2 * d), BF16), _rs(tm, 2 * d)), (SDS((1, 2 * d), F32), _fs((1, 2 * d)))], n_acc=1)


def _swiglu_fwd(gt, up):
    t, f = gt.shape
    tm = 256

    def fn(i, g_ref, u_ref):
        gv = g_ref[...]
        return [gv * _sigmoid(gv) * u_ref[...]]

    return _rw("swiglu_fwd", fn, t // tm, [(gt, _rs(tm, f)), (up, _rs(tm, f))], [(SDS((t, f), BF16), _rs(tm, f))])[0]


def _swiglu_bwd(gt, up, dact):
    t, f = gt.shape
    tm = 256

    def fn(i, g_ref, u_ref, d_ref):
        gv, dv = g_ref[...], d_ref[...]
        sg = _sigmoid(gv)
        return [dv * u_ref[...] * sg * (1.0 + gv * (1.0 - sg)), dv * gv * sg]

    return _rw("swiglu_bwd", fn, t // tm, [(gt, _rs(tm, f)), (up, _rs(tm, f)), (dact, _rs(tm, f))],
               [(SDS((t, f), BF16), _rs(tm, f))] * 2)


def _adamw(w, g, m, v, name):
    r, c = w.shape
    tr = _row_tile(r, max(8, 400_000 // c))
    c1 = 1.0 / (1.0 - ADAM_B1 ** ADAM_STEP)
    c2 = 1.0 / (1.0 - ADAM_B2 ** ADAM_STEP)

    def fn(i, w_ref, g_ref, m_ref, v_ref):
        gv = g_ref[...]
        mn = ADAM_B1 * m_ref[...] + (1.0 - ADAM_B1) * gv
        vn = ADAM_B2 * v_ref[...] + (1.0 - ADAM_B2) * (gv * gv)
        delta = -ADAM_LR * ((mn * c1) / (jnp.sqrt(vn * c2) + ADAM_EPS) + ADAM_WD * w_ref[...])
        return [delta, mn, vn]

    spec = pl.BlockSpec((tr, c), lambda i: (i, 0))
    return _rw(name, fn, r // tr, [(w, spec), (g, spec), (m, spec), (v, spec)], [(SDS((r, c), F32), spec)] * 3)


ANY = pl.BlockSpec(memory_space=pl.ANY)


def _place():
    x, y, c = lax.axis_index("x"), lax.axis_index("y"), lax.axis_index("c")
    chips = [(1 - x, y), (x, 1 - y), (1 - x, 1 - y)]
    return x, y, c, chips


def _remote(src, dst, ssem, rsem, to):
    return pltpu.make_async_remote_copy(src_ref=src, dst_ref=dst, send_sem=ssem, recv_sem=rsem, device_id=to,
                                        device_id_type=MESH)


def _copy_through_vmem(src, dst, buf, isem, osem):
    chunk = buf.shape[1]
    n = src.shape[0] // chunk
    load = lambda k: pltpu.make_async_copy(src.at[pl.ds(k * chunk, chunk)], buf.at[k % 2], isem.at[k % 2])
    store = lambda k: pltpu.make_async_copy(buf.at[k % 2], dst.at[pl.ds(k * chunk, chunk)], osem.at[k % 2])
    load(0).start()
    for k in range(n):
        load(k).wait()
        if k + 1 < n:
            if k >= 1:
                store(k - 1).wait()
            load(k + 1).start()
        store(k).start()
    if n >= 2:
        store(n - 2).wait()
    store(n - 1).wait()


def _copy_scratch(rows, width, dtype):
    chunk = _row_tile(rows, 512)
    return [pltpu.VMEM((2, chunk, width), dtype), pltpu.SemaphoreType.DMA((2,)), pltpu.SemaphoreType.DMA((2,))]


def _gather_weights(wp):
    def body(w_ref, out_ref, ssem, rsem, buf, isem, osem):
        x, y, c, chips = _place()
        me = 2 * x + y
        sib = (x, y, 1 - c)
        first = [_remote(w_ref.at[c], out_ref.at[me, c], ssem.at[j], rsem.at[j], (*chip, c))
                 for j, chip in enumerate(chips)]
        for cp in first:
            cp.start()
        for half in range(2):
            _copy_through_vmem(w_ref.at[half], out_ref.at[me, half], buf, isem, osem)
        passed = []
        for j, chip in enumerate(chips):
            ci = 2 * chip[0] + chip[1]
            _remote(w_ref.at[c], out_ref.at[ci, c], ssem.at[j], rsem.at[j], (*chip, c)).wait_recv()
            cp = _remote(out_ref.at[ci, c], out_ref.at[ci, c], ssem.at[3 + j], rsem.at[3 + j], sib)
            cp.start()
            passed.append(cp)
        for j, chip in enumerate(chips):
            ci = 2 * chip[0] + chip[1]
            _remote(out_ref.at[ci, 1 - c], out_ref.at[ci, 1 - c], ssem.at[3 + j], rsem.at[3 + j], sib).wait_recv()
        for cp in first + passed:
            cp.wait_send()

    return pl.pallas_call(
        body, name="gather_weights", in_specs=[ANY], out_specs=ANY,
        out_shape=SDS((N_CHIPS,) + wp.shape, wp.dtype),
        scratch_shapes=[pltpu.SemaphoreType.DMA((6,)), pltpu.SemaphoreType.DMA((6,))]
        + _copy_scratch(wp.shape[1], wp.shape[2], wp.dtype),
        compiler_params=pltpu.CompilerParams(has_side_effects=True))(wp)


def _swap_halves(g2):
    def body(g_ref, out_ref, ssem, rsem):
        x, y, c, _ = _place()
        cp = _remote(g_ref.at[1 - c], out_ref, ssem, rsem, (x, y, 1 - c))
        cp.start()
        cp.wait()

    return pl.pallas_call(
        body, name="swap_halves", in_specs=[ANY], out_specs=ANY, out_shape=SDS(g2.shape[1:], g2.dtype),
        scratch_shapes=[pltpu.SemaphoreType.DMA(()), pltpu.SemaphoreType.DMA(())],
        compiler_params=pltpu.CompilerParams(has_side_effects=True))(g2)


def _add_own_half(g2, other, c):
    _, nch, rows, w = g2.shape
    tr = _row_tile(rows, 512)
    nr = rows // tr

    def body(c_ref, a_ref, b_ref, o_ref):
        o_ref[...] = (a_ref[...].astype(F32) + b_ref[...].astype(F32)).astype(o_ref.dtype)

    grid_spec = pltpu.PrefetchScalarGridSpec(
        num_scalar_prefetch=1, grid=(nch, nr),
        in_specs=[pl.BlockSpec((None, None, tr, w), lambda k, i, c_ref: (c_ref[0], k, i, 0)),
                  pl.BlockSpec((None, tr, w), lambda k, i, c_ref: (k, i, 0))],
        out_specs=pl.BlockSpec((None, tr, w), lambda k, i, c_ref: (k, i, 0)))
    return pl.pallas_call(
        body, name="add_own_half", grid_spec=grid_spec, out_shape=SDS(other.shape, other.dtype),
        compiler_params=_cparams(("arbitrary", "arbitrary")))(jnp.reshape(c, (1,)).astype(jnp.int32), g2, other)


def _scatter_to_chips(p):
    def body(p_ref, q_ref, ssem, rsem, buf, isem, osem):
        x, y, c, chips = _place()
        me = 2 * x + y
        sent = []
        for j, chip in enumerate(chips):
            ci = 2 * chip[0] + chip[1]
            cp = _remote(p_ref.at[ci], q_ref.at[me], ssem.at[j], rsem.at[j], (*chip, c))
            cp.start()
            sent.append(cp)
        _copy_through_vmem(p_ref.at[me], q_ref.at[me], buf, isem, osem)
        for j, chip in enumerate(chips):
            ci = 2 * chip[0] + chip[1]
            _remote(p_ref.at[ci], q_ref.at[ci], ssem.at[j], rsem.at[j], (*chip, c)).wait_recv()
        for cp in sent:
            cp.wait_send()

    return pl.pallas_call(
        body, name="scatter_to_chips", in_specs=[ANY], out_specs=ANY, out_shape=SDS(p.shape, p.dtype),
        scratch_shapes=[pltpu.SemaphoreType.DMA((3,)), pltpu.SemaphoreType.DMA((3,))]
        + _copy_scratch(p.shape[1], p.shape[2], p.dtype),
        compiler_params=pltpu.CompilerParams(has_side_effects=True))(p)


def _sum_chips(q):
    nch, rows, w = q.shape
    tr = _row_tile(rows, 512)

    def fn(i, q_ref):
        return [((q_ref[0].astype(F32) + q_ref[1].astype(F32)) + q_ref[2].astype(F32)) + q_ref[3].astype(F32)]

    return _rw("sum_chips", fn, rows // tr, [(q, pl.BlockSpec((nch, tr, w), lambda i: (0, i, 0)))],
               [(SDS((rows, w), F32), _rs(tr, w))])[0]


def _allreduce_small(v, name):
    rows, w = v.shape
    offsets = [(dx, dy, dc) for dx in (0, 1) for dy in (0, 1) for dc in (0, 1)][1:]

    def body(v_ref, o_ref, buf, ssem, rsem):
        x, y, c, _ = _place()
        flip = lambda p, d: 1 - p if d else p
        peers = [(flip(x, dx), flip(y, dy), flip(c, dc)) for dx, dy, dc in offsets]
        index = lambda p: 4 * p[0] + 2 * p[1] + p[2]
        me = index((x, y, c))
        buf[me] = v_ref[...]
        sent = [_remote(v_ref, buf.at[me], ssem.at[q], rsem.at[q], p) for q, p in enumerate(peers)]
        for cp in sent:
            cp.start()
        for q, p in enumerate(peers):
            _remote(v_ref, buf.at[index(p)], ssem.at[q], rsem.at[q], p).wait_recv()
        for cp in sent:
            cp.wait_send()
        acc = buf[0]
        for q in range(1, 8):
            acc = acc + buf[q]
        o_ref[...] = acc

    vm = pl.BlockSpec(memory_space=pltpu.VMEM)
    return pl.pallas_call(
        body, name=name, in_specs=[vm], out_specs=vm, out_shape=SDS((rows, w), F32),
        scratch_shapes=[pltpu.VMEM((8, rows, w), F32), pltpu.SemaphoreType.DMA((7,)), pltpu.SemaphoreType.DMA((7,))],
        compiler_params=pltpu.CompilerParams(has_side_effects=True))(v)


def _join_halves(h):
    def body(h_ref, out_ref, ssem, rsem, buf, isem, osem):
        x, y, c, _ = _place()
        cp = _remote(h_ref, out_ref.at[c], ssem, rsem, (x, y, 1 - c))
        cp.start()
        _copy_through_vmem(h_ref, out_ref.at[c], buf, isem, osem)
        _remote(h_ref, out_ref.at[1 - c], ssem, rsem, (x, y, 1 - c)).wait_recv()
        cp.wait_send()

    return pl.pallas_call(
        body, name="join_halves", in_specs=[ANY], out_specs=ANY, out_shape=SDS((2,) + h.shape, h.dtype),
        scratch_shapes=[pltpu.SemaphoreType.DMA(()), pltpu.SemaphoreType.DMA(())]
        + _copy_scratch(h.shape[0], h.shape[1], h.dtype),
        compiler_params=pltpu.CompilerParams(has_side_effects=True))(h)


PACK_W = 1024
SHARDED = ("w_in", "w_ffn_gate", "w_ffn_up", "w_ssm_out", "w_att_out", "w_mix_out", "w_ffn_down")
COL_SHARDED = ("w_in", "w_ffn_gate", "w_ffn_up", "w_att_out")
SMALL = ("norm_mix", "b_gate", "conv_b", "dt_bias", "a_log", "d_skip", "ssm_norm", "norm_ffn", "norm_final")


PACK_ROW_ALIGN = 16


def _rows(n):
    return -(-n // (PACK_W * PACK_ROW_ALIGN)) * PACK_ROW_ALIGN


def _pack_rows(parts, total_rows):
    rows = []
    for p in parts:
        flat = p.reshape(-1)
        pad = _rows(flat.shape[0]) * PACK_W - flat.shape[0]
        if pad:
            flat = jnp.concatenate([flat, jnp.zeros((pad,), flat.dtype)])
        rows.append(flat.reshape(-1, PACK_W))
    used = sum(r.shape[0] for r in rows)
    if total_rows > used:
        rows.append(jnp.zeros((total_rows - used, PACK_W), rows[0].dtype))
    return jnp.concatenate(rows, axis=0)


def _padded_rows(n):
    return -(-n // 32) * 32


def _wire_name(name):
    return name + "_t" if name in COL_SHARDED else name


def _wire_shard(w, name):
    return w.T if name in COL_SHARDED else w


def _group_major(a, axis):
    gw = D_INNER // N_GROUPS
    take = lambda lo, n: lax.slice_in_dim(a, lo, lo + n, axis=axis)
    parts = []
    for g in range(N_GROUPS):
        parts += [take(g * gw, gw), take(D_INNER + g * D_STATE, D_STATE),
                  take(D_INNER + N_GROUPS * D_STATE + g * D_STATE, D_STATE)]
    return jnp.concatenate(parts, axis=axis)


def _group_major_inv(a, axis):
    gw = D_INNER // N_GROUPS
    take = lambda lo, n: lax.slice_in_dim(a, lo, lo + n, axis=axis)
    xs = [take(g * GROUP_W, gw) for g in range(N_GROUPS)]
    bs = [take(g * GROUP_W + gw, D_STATE) for g in range(N_GROUPS)]
    cs = [take(g * GROUP_W + gw + D_STATE, D_STATE) for g in range(N_GROUPS)]
    return jnp.concatenate(xs + bs + cs, axis=axis)


def _local_step(x, target, wts):
    nb, seq, d = x.shape
    t = nb * seq
    x = x.reshape(t, d)
    target = target.reshape(t, d)
    hg = HEADS_PER_GROUP

    w_in_t = wts["w_in_t"]
    o1, o2, o3, o4 = D_INNER, D_INNER + CONV_DIM, D_INNER + CONV_DIM + N_HEADS, D_INNER + CONV_DIM + N_HEADS + QKV_DIM
    w_z = w_in_t[:o1]
    w_xbc = _group_major(w_in_t[o1:o2], 0)
    w_dt = jnp.pad(w_in_t[o2:o3], ((0, DT_PAD - N_HEADS), (0, 0)))
    w_qkv = w_in_t[o3:o4]
    w_gate = w_in_t[o4:]
    conv_w = _group_major(wts["conv_w"], 1)
    conv_b = _group_major(wts["conv_b"], 1)

    def per_group_row(p):
        return p.reshape(N_GROUPS, 1, hg)

    def per_group_col(p):
        return p.reshape(N_GROUPS, hg, 1)

    a_neg = -jnp.exp(wts["a_log"])
    bias_r, bias_c = per_group_row(wts["dt_bias"]), per_group_col(wts["dt_bias"])
    a_r, a_c = per_group_row(a_neg), per_group_col(a_neg)
    dskip_r = per_group_row(wts["d_skip"])
    cos, sin = _rope_tables(seq)

    h = _rms_fwd(x, wts["norm_mix"], "rms_mix_fwd")
    z = _mm(h, w_z, "nt", F32, "proj_z")
    xbc = _mm(h, w_xbc, "nt", F32, "proj_xbc")
    dt_raw = _mm(h, w_dt, "nt", F32, "proj_dt")
    qkv = _mm(h, w_qkv, "nt", F32, "proj_qkv")
    gate_logits = _mm(h, w_gate, "nt", F32, "proj_gate")

    xc = _conv_fwd(xbc, conv_w, conv_b, seq)
    dtr = dt_raw[:, :N_HEADS].reshape(t, N_GROUPS, hg).transpose(1, 0, 2)
    dtrt = dt_raw[:, :N_HEADS].reshape(nb, seq, N_GROUPS, hg).transpose(2, 0, 3, 1)
    y, states = _ssd_fwd(xc, dtr, dtrt, bias_r, bias_c, a_r, a_c, dskip_r, nb, seq)
    yn = _gate_norm_fwd(y, z, wts["ssm_norm"])
    y_ssm = _mm(yn, wts["w_ssm_out"], "nn", F32, "ssm_out")

    groups = range(len(ATT_DILATIONS))
    qg, kg, vg = _rope_fwd(qkv, cos, sin, nb, seq)
    o_g, lse_g = zip(*[_att_fwd(qg[i], kg[i], vg[i], i, seq) for i in groups])
    att = _merge_fwd(o_g, lse_g, nb, seq)
    y_att = _mm(att, wts["w_att_out_t"], "nt", F32, "att_out")

    mixed = _mix_fwd(gate_logits, wts["b_gate"], y_ssm, y_att)
    x1 = _mm(mixed, wts["w_mix_out"], "nn", F32, "mix_out", add=x)
    h2 = _rms_fwd(x1, wts["norm_ffn"], "rms_ffn_fwd")
    gt = _mm(h2, wts["w_ffn_gate_t"], "nt", F32, "ffn_gate")
    up = _mm(h2, wts["w_ffn_up_t"], "nt", F32, "ffn_up")
    act = _swiglu_fwd(gt, up)
    x2 = _mm(act, wts["w_ffn_down"], "nn", F32, "ffn_down", add=x1)

    g = {}
    dx2, dx2_b, g["norm_final"], loss = _final_fwd_bwd(x2, target, wts["norm_final"].reshape(1, d))
    dact = _mm(dx2_b, wts["w_ffn_down"], "nt", F32, "d_act")
    g["w_ffn_down"] = _mm(act, dx2_b, "tn", BF16, "g_ffn_down")
    dgt, dup = _swiglu_bwd(gt, up, dact)
    g["w_ffn_gate_t"] = _mm(dgt, h2, "tn", BF16, "g_ffn_gate")
    g["w_ffn_up_t"] = _mm(dup, h2, "tn", BF16, "g_ffn_up")
    dh2 = _mm(dgt, wts["w_ffn_gate_t"], "nn", F32, "d_h2_gate")
    dh2 = _mm(dup, wts["w_ffn_up_t"], "nn", F32, "d_h2_up", add=dh2)
    dx1, dx1_b, g["norm_ffn"] = _rms_bwd(x1, dh2, wts["norm_ffn"], dx2, "rms_ffn_bwd")

    dmixed = _mm(dx1_b, wts["w_mix_out"], "nt", F32, "d_mixed")
    g["w_mix_out"] = _mm(mixed, dx1_b, "tn", BF16, "g_mix_out")
    dy_ssm, dy_att, dgate, g["b_gate"] = _mix_bwd(gate_logits, wts["b_gate"], y_ssm, y_att, dmixed)

    datt = _mm(dy_att, wts["w_att_out_t"], "nn", F32, "d_att")
    g["w_att_out_t"] = _mm(dy_att, att, "tn", BF16, "g_att_out")
    do_g, dlt_g = _merge_bwd(o_g, lse_g, datt, nb, seq)
    dq_g, dk_g, dv_g = zip(*[_att_bwd(qg[i], kg[i], vg[i], do_g[i], lse_g[i], dlt_g[i], i, seq) for i in groups])
    dqkv = _rope_bwd(dq_g, dk_g, dv_g, cos, sin, nb, seq)

    dyn = _mm(dy_ssm, wts["w_ssm_out"], "nt", F32, "d_yn")
    g["w_ssm_out"] = _mm(yn, dy_ssm, "tn", BF16, "g_ssm_out")
    dy, dz, g["ssm_norm"] = _gate_norm_bwd(y, z, wts["ssm_norm"], dyn)
    dxc, ddtr, g_bias, g_alog, g_dskip = _ssd_bwd(xc, dtr, dtrt, bias_r, bias_c, a_r, a_c, dskip_r, states, dy,
                                                   nb, seq)
    g["dt_bias"] = g_bias.reshape(1, N_HEADS)
    g["a_log"] = g_alog.reshape(1, N_HEADS)
    g["d_skip"] = g_dskip.reshape(1, N_HEADS)
    dpre, g_conv_w, g_conv_b = _conv_bwd_pre(xbc, conv_w, conv_b, dxc, seq)
    g["conv_w"] = _group_major_inv(g_conv_w, 1)
    g["conv_b"] = _group_major_inv(g_conv_b, 1)
    dxbc = _conv_bwd_in(dpre, conv_w, seq)
    ddt = jnp.pad(ddtr.transpose(1, 0, 2).reshape(t, N_HEADS), ((0, 0), (0, DT_PAD - N_HEADS))).astype(BF16)

    dh = _mm(dz, w_z, "nn", F32, "d_h_z")
    dh = _mm(dxbc, w_xbc, "nn", F32, "d_h_xbc", add=dh)
    dh = _mm(ddt, w_dt, "nn", F32, "d_h_dt", add=dh)
    dh = _mm(dqkv, w_qkv, "nn", F32, "d_h_qkv", add=dh)
    dh = _mm(dgate, w_gate, "nn", F32, "d_h_gate", add=dh)
    g["w_in_t"] = jnp.concatenate([
        _mm(dz, h, "tn", BF16, "g_in_z"),
        _group_major_inv(_mm(dxbc, h, "tn", BF16, "g_in_xbc"), 0),
        _mm(ddt, h, "tn", BF16, "g_in_dt")[:N_HEADS],
        _mm(dqkv, h, "tn", BF16, "g_in_qkv"),
        _mm(dgate, h, "tn", BF16, "g_in_gate")], axis=0)
    dx, _, g["norm_mix"] = _rms_bwd(x, dh, wts["norm_mix"], dx1, "rms_mix_bwd")
    return loss[0, 0], dx.reshape(nb, seq, d), g


def kernel(x, norm_mix, w_in, b_gate, conv_w, conv_b, dt_bias, a_log, d_skip, ssm_norm, w_ssm_out, w_att_out, w_mix_out, norm_ffn, w_ffn_gate, w_ffn_up, w_ffn_down, norm_final, loss_target, m_norm_mix, m_w_in, m_b_gate, m_conv_w, m_conv_b, m_dt_bias, m_a_log, m_d_skip, m_ssm_norm, m_w_ssm_out, m_w_att_out, m_w_mix_out, m_norm_ffn, m_w_ffn_gate, m_w_ffn_up, m_w_ffn_down, m_norm_final, v_norm_mix, v_w_in, v_b_gate, v_conv_w, v_conv_b, v_dt_bias, v_a_log, v_d_skip, v_ssm_norm, v_w_ssm_out, v_w_att_out, v_w_mix_out, v_norm_ffn, v_w_ffn_gate, v_w_ffn_up, v_w_ffn_down, v_norm_final):
    names = ("norm_mix", "w_in", "b_gate", "conv_w", "conv_b", "dt_bias", "a_log", "d_skip", "ssm_norm", "w_ssm_out",
             "w_att_out", "w_mix_out", "norm_ffn", "w_ffn_gate", "w_ffn_up", "w_ffn_down", "norm_final")
    w_loc = dict(zip(names, (norm_mix, w_in, b_gate, conv_w, conv_b, dt_bias, a_log, d_skip, ssm_norm, w_ssm_out,
                             w_att_out, w_mix_out, norm_ffn, w_ffn_gate, w_ffn_up, w_ffn_down, norm_final)))
    m_loc = dict(zip(names, (m_norm_mix, m_w_in, m_b_gate, m_conv_w, m_conv_b, m_dt_bias, m_a_log, m_d_skip,
                             m_ssm_norm, m_w_ssm_out, m_w_att_out, m_w_mix_out, m_norm_ffn, m_w_ffn_gate,
                             m_w_ffn_up, m_w_ffn_down, m_norm_final)))
    v_loc = dict(zip(names, (v_norm_mix, v_w_in, v_b_gate, v_conv_w, v_conv_b, v_dt_bias, v_a_log, v_d_skip,
                             v_ssm_norm, v_w_ssm_out, v_w_att_out, v_w_mix_out, v_norm_ffn, v_w_ffn_gate,
                             v_w_ffn_up, v_w_ffn_down, v_norm_final)))
    two_d = lambda a: a.reshape(a.shape[-2:]) if a.ndim >= 2 else a.reshape(1, -1)
    w2 = {n: two_d(a) for n, a in w_loc.items()}
    chip = 2 * lax.axis_index("x") + lax.axis_index("y")
    c = lax.axis_index("c")

    wire_shapes = {n: _wire_shard(w2[n], n).shape for n in SHARDED}
    true_rows = {n: wire_shapes[n][0] * wire_shapes[n][1] // PACK_W for n in SHARDED}
    seg_rows = {n: _rows(wire_shapes[n][0] * wire_shapes[n][1]) for n in SHARDED}
    w_rows = _padded_rows(sum(seg_rows.values()))
    wp = _pack_rows([_wire_shard(w2[n], n).astype(BF16) for n in SHARDED], w_rows).reshape(2, w_rows // 2, PACK_W)
    wg = _gather_weights(wp).reshape(N_CHIPS, w_rows, PACK_W)
    full = {}
    off = 0
    for n in SHARDED:
        rows, cols = wire_shapes[n]
        full[_wire_name(n)] = wg[:, off:off + true_rows[n]].reshape(N_CHIPS * rows, cols)
        off += seg_rows[n]
    for n in SMALL:
        full[n] = w2[n]

    n_conv = w2["conv_w"].shape[1]
    placed = lax.dynamic_update_slice_in_dim(jnp.zeros((CONV_K, N_CHIPS * n_conv), F32), w2["conv_w"], chip * n_conv, 1)
    placed = jnp.where(c == 0, placed, 0.0)
    full["conv_w"] = _allreduce_small(_pack_rows([placed], _rows(int(placed.size))), "gather_conv_w").reshape(
        -1)[:placed.size].reshape(placed.shape)

    loss_sum, grad_x, g_full = _local_step(x, loss_target, full)
    loss = lax.psum(loss_sum, ("x", "y", "c"))

    g_shard = {}
    small_names = SMALL + ("conv_w",)
    small_flat = jnp.concatenate([g_full[n].reshape(-1) for n in small_names])
    small = _allreduce_small(_pack_rows([small_flat], _rows(int(small_flat.size))), "allreduce_small").reshape(-1)
    off = 0
    for n in small_names:
        size = int(g_full[n].size)
        g_shard[n] = small[off:off + size].reshape(g_full[n].shape)
        off += size
    g_shard["conv_w"] = lax.dynamic_slice_in_dim(g_shard["conv_w"], chip * n_conv, n_conv, 1)

    sections = [_pack_rows([g_full[_wire_name(n)].reshape(N_CHIPS, true_rows[n], PACK_W)[k] for n in SHARDED], w_rows)
                for k in range(N_CHIPS)]
    g2 = jnp.stack(sections).reshape(N_CHIPS, 2, w_rows // 2, PACK_W).transpose(1, 0, 2, 3)
    chip_sum = _add_own_half(g2, _swap_halves(g2), c)
    reduced = _join_halves(_sum_chips(_scatter_to_chips(chip_sum))).reshape(w_rows, PACK_W)
    off = 0
    for n in SHARDED:
        wire = reduced[off:off + true_rows[n]].reshape(wire_shapes[n])
        g_shard[n] = wire.T if n in COL_SHARDED else wire
        off += seg_rows[n]

    grads, deltas, new_m, new_v = [], [], [], []
    for n in names:
        shape = w_loc[n].shape
        d_, m_, v_ = _adamw(w2[n], g_shard[n], two_d(m_loc[n]), two_d(v_loc[n]), "adamw_" + n)
        grads.append(g_shard[n].reshape(shape))
        deltas.append(d_.reshape(shape))
        new_m.append(m_.reshape(shape))
        new_v.append(v_.reshape(shape))
    return (loss, grad_x, *grads, *deltas, *new_m, *new_v)
```

```python
import functools
import math

import jax
import jax.numpy as jnp
from jax import lax
from jax.experimental import pallas as pl
from jax.experimental.pallas import tpu as pltpu

F32 = jnp.float32
BF16 = jnp.bfloat16
SDS = jax.ShapeDtypeStruct
MESH = pl.DeviceIdType.MESH

D_MODEL = 1024
D_INNER = 2048
N_HEADS = 32
HEAD_P = 64
N_GROUPS = 4
HEADS_PER_GROUP = N_HEADS // N_GROUPS
D_STATE = 128
CONV_K = 4
CHUNK = 128
CONV_DIM = D_INNER + 2 * N_GROUPS * D_STATE
GROUP_W = D_INNER // N_GROUPS + 2 * D_STATE
ATT_HEADS = 12
ATT_D = 128
ATT_SLOTS = 4
ATT_W = ATT_SLOTS * ATT_D
ATT_DILATIONS = (1, 4, 16)
ATT_BLOCK = 128
QKV_DIM = 3 * ATT_HEADS * ATT_D
D_FF = 2816
DT_PAD = 128
ROPE_THETA = 10000.0
EPS = 1e-6
N_CHIPS = 4
LANES = 128

ADAM_LR = 0.001
ADAM_B1 = 0.9
ADAM_B2 = 0.999
ADAM_EPS = 1e-08
ADAM_WD = 0.01
ADAM_STEP = 10

VMEM_LIMIT = 48 * 1024 * 1024


def _cparams(semantics):
    return pltpu.CompilerParams(dimension_semantics=semantics, vmem_limit_bytes=VMEM_LIMIT)


def _pick(n, cap):
    best = None
    for t in range(LANES, min(n, cap) + 1, LANES):
        if n % t == 0:
            best = t
    return best or n


def _row_tile(rows, cap):
    best = None
    for t in range(8, min(rows, cap) + 1, 8):
        if rows % t == 0:
            best = t
    return best or rows


def _sigmoid(x):
    return 1.0 / (1.0 + jnp.exp(-x))


def _softplus(x):
    return jnp.maximum(x, 0.0) + jnp.log(1.0 + jnp.exp(-jnp.abs(x)))


def _dot(a, b):
    return jnp.dot(a, b, preferred_element_type=F32)


def _dot_nt(a, b):
    return lax.dot_general(a, b, (((1,), (1,)), ((), ())), preferred_element_type=F32)


def _dot_tn(a, b):
    return lax.dot_general(a, b, (((0,), (0,)), ((), ())), preferred_element_type=F32)


def _mm(a, b, mode, out_dtype, name, add=None):
    if mode == "nn":
        (m, k), (_, n) = a.shape, b.shape
    elif mode == "nt":
        (m, k), (n, _) = a.shape, b.shape
    else:
        (k, m), (_, n) = a.shape, b.shape
    tm, tn = _pick(m, 512), _pick(n, 1536)
    tk = k if k <= 2048 else _pick(k, 2048)
    nk = k // tk
    dims = {"nn": ((1,), (0,)), "nt": ((1,), (1,)), "tn": ((0,), (0,))}[mode]

    def partial_product(a_ref, b_ref):
        return lax.dot_general(a_ref[...].astype(BF16), b_ref[...].astype(BF16), (dims, ((), ())),
                               preferred_element_type=F32)

    def body(*refs):
        a_ref, b_ref = refs[:2]
        c_ref = refs[2] if add is not None else None
        o_ref = refs[3] if add is not None else refs[2]

        def finish(r):
            if add is not None:
                r = r + c_ref[...].astype(F32)
            o_ref[...] = r.astype(out_dtype)

        if nk == 1:
            finish(partial_product(a_ref, b_ref))
            return
        acc = refs[-1]
        kk = pl.program_id(2)

        @pl.when(kk == 0)
        def _():
            acc[...] = partial_product(a_ref, b_ref)

        @pl.when((kk > 0) & (kk < nk - 1))
        def _():
            acc[...] += partial_product(a_ref, b_ref)

        @pl.when(kk == nk - 1)
        def _():
            finish(acc[...] + partial_product(a_ref, b_ref))

    a_spec = {"nn": pl.BlockSpec((tm, tk), lambda j, i, q: (i, q)),
              "nt": pl.BlockSpec((tm, tk), lambda j, i, q: (i, q)),
              "tn": pl.BlockSpec((tk, tm), lambda j, i, q: (q, i))}[mode]
    b_spec = {"nn": pl.BlockSpec((tk, tn), lambda j, i, q: (q, j)),
              "nt": pl.BlockSpec((tn, tk), lambda j, i, q: (j, q)),
              "tn": pl.BlockSpec((tk, tn), lambda j, i, q: (q, j))}[mode]
    o_spec = pl.BlockSpec((tm, tn), lambda j, i, q: (i, j))
    ins, specs = [a, b], [a_spec, b_spec]
    if add is not None:
        ins.append(add)
        specs.append(o_spec)
    return pl.pallas_call(
        body, name=name, grid=(n // tn, m // tm, nk), in_specs=specs, out_specs=o_spec,
        out_shape=SDS((m, n), out_dtype), scratch_shapes=[pltpu.VMEM((tm, tn), F32)] if nk > 1 else [],
        compiler_params=_cparams(("parallel", "parallel", "arbitrary")))(*ins)


def _rw(name, fn, nsteps, ins, outs, n_acc=0):
    n_in, n_out = len(ins), len(outs)

    def body(*refs):
        i = pl.program_id(0)
        vals = fn(i, *refs[:n_in])
        for q, (r, v) in enumerate(zip(refs[n_in:], vals)):
            if q < n_out - n_acc:
                r[...] = v.astype(r.dtype)
            else:
                @pl.when(i == 0)
                def _(r=r):
                    r[...] = jnp.zeros_like(r)

                r[...] += v

    return pl.pallas_call(
        body, name=name, grid=(nsteps,), in_specs=[s for _, s in ins], out_specs=[s for _, s in outs],
        out_shape=[o for o, _ in outs], compiler_params=_cparams(("arbitrary",)))(*[a for a, _ in ins])


def _rs(tm, w, cb=0):
    return pl.BlockSpec((tm, w), lambda i: (i, cb))


def _fs(shape):
    nd = len(shape)
    return pl.BlockSpec(shape, lambda i: (0,) * nd)


def _colsum(v):
    return jnp.sum(v, axis=0, keepdims=True)


def _rms_fwd(x, g, name):
    t, d = x.shape
    tm = 512

    def fn(i, x_ref, g_ref):
        xv = x_ref[...]
        r = lax.rsqrt(jnp.mean(xv * xv, axis=-1, keepdims=True) + EPS)
        return [xv * r * g_ref[...]]

    return _rw(name, fn, t // tm, [(x, _rs(tm, d)), (g, _fs((1, d)))], [(SDS((t, d), BF16), _rs(tm, d))])[0]


def _rms_bwd(x, dh, g, dres, name):
    t, d = x.shape
    tm = 512

    def fn(i, x_ref, dh_ref, g_ref, dres_ref):
        xv = x_ref[...]
        r = lax.rsqrt(jnp.mean(xv * xv, axis=-1, keepdims=True) + EPS)
        xhat = xv * r
        dhv = dh_ref[...]
        dxhat = dhv * g_ref[...]
        dx = dres_ref[...] + r * (dxhat - xhat * jnp.mean(dxhat * xhat, axis=-1, keepdims=True))
        return [dx, dx, _colsum(dhv * xhat)]

    return _rw(name, fn, t // tm,
               [(x, _rs(tm, d)), (dh, _rs(tm, d)), (g, _fs((1, d))), (dres, _rs(tm, d))],
               [(SDS((t, d), F32), _rs(tm, d)), (SDS((t, d), BF16), _rs(tm, d)), (SDS((1, d), F32), _fs((1, d)))],
               n_acc=1)


def _final_fwd_bwd(x2, target, g):
    t, d = x2.shape
    tm = 512

    def fn(i, x_ref, t_ref, g_ref):
        xv = x_ref[...]
        gv = g_ref[...]
        r = lax.rsqrt(jnp.mean(xv * xv, axis=-1, keepdims=True) + EPS)
        xhat = xv * r
        diff = xhat * gv - t_ref[...]
        lsum = 0.5 * jnp.sum(jnp.sum(diff * diff, axis=-1, keepdims=True) * (1.0 / d), axis=0, keepdims=True)
        dy = diff * (1.0 / d)
        dxhat = dy * gv
        dx = r * (dxhat - xhat * jnp.mean(dxhat * xhat, axis=-1, keepdims=True))
        return [dx, dx, _colsum(dy * xhat), lsum]

    return _rw("final_norm_loss", fn, t // tm,
               [(x2, _rs(tm, d)), (target, _rs(tm, d)), (g, _fs((1, d)))],
               [(SDS((t, d), F32), _rs(tm, d)), (SDS((t, d), BF16), _rs(tm, d)), (SDS((1, d), F32), _fs((1, d))),
                (SDS((1, 1), F32), _fs((1, 1)))], n_acc=2)


CONV_TS = 512
CONV_HALO = 8


def _conv_specs(seq, c):
    ts, tc = CONV_TS, GROUP_W
    hb = ts // CONV_HALO
    u_spec = pl.BlockSpec((ts, tc), lambda j, i: (i, j))
    prev_spec = pl.BlockSpec((CONV_HALO, tc), lambda j, i: (jnp.maximum(i * hb - 1, 0), j))
    w_spec = pl.BlockSpec((CONV_K, tc), lambda j, i: (0, j))
    b_spec = pl.BlockSpec((1, tc), lambda j, i: (0, j))
    return u_spec, prev_spec, w_spec, b_spec


CONV_ROWS = 16


def _conv_pre(i, seq, u_ref, prev_ref, w_ref, b_ref, ext):
    ts = CONV_TS
    first = (i % (seq // ts)) == 0
    ext[0:CONV_HALO, :] = jnp.where(first, 0.0, prev_ref[...])
    ext[CONV_HALO:, :] = u_ref[...]
    acc = jnp.broadcast_to(b_ref[...], u_ref.shape)
    for q in range(CONV_K):
        acc = acc + w_ref[q:q + 1, :] * ext[pl.ds(CONV_HALO - CONV_K + 1 + q, ts), :]
    return acc


def _conv_fwd(u, w, b, seq):
    t, c = u.shape
    ts, tc = CONV_TS, GROUP_W
    u_spec, prev_spec, w_spec, b_spec = _conv_specs(seq, c)

    def body(u_ref, prev_ref, w_ref, b_ref, o_ref, ext):
        pre = _conv_pre(pl.program_id(1), seq, u_ref, prev_ref, w_ref, b_ref, ext)
        o_ref[...] = pre * _sigmoid(pre)

    return pl.pallas_call(
        body, name="conv_fwd", grid=(c // tc, t // ts), in_specs=[u_spec, prev_spec, w_spec, b_spec],
        out_specs=u_spec, out_shape=SDS((t, c), F32), scratch_shapes=[pltpu.VMEM((ts + CONV_HALO, tc), F32)],
        compiler_params=_cparams(("parallel", "arbitrary")))(u, u, w, b)


def _conv_bwd_pre(u, w, b, dxc, seq):
    t, c = u.shape
    ts, tc = CONV_TS, GROUP_W
    u_spec, prev_spec, w_spec, b_spec = _conv_specs(seq, c)

    def body(u_ref, prev_ref, w_ref, b_ref, d_ref, dpre_ref, dw_ref, db_ref, ext):
        i = pl.program_id(1)
        pre = _conv_pre(i, seq, u_ref, prev_ref, w_ref, b_ref, ext)
        sg = _sigmoid(pre)
        dpre = d_ref[...] * sg * (1.0 + pre * (1.0 - sg))
        dpre_ref[...] = dpre

        @pl.when(i == 0)
        def _():
            dw_ref[...] = jnp.zeros_like(dw_ref)
            db_ref[...] = jnp.zeros_like(db_ref)

        db_ref[...] += _colsum(dpre)
        for q in range(CONV_K):
            dw_ref[q:q + 1, :] += _colsum(dpre * ext[pl.ds(CONV_HALO - CONV_K + 1 + q, ts), :])

    return pl.pallas_call(
        body, name="conv_bwd_pre", grid=(c // tc, t // ts),
        in_specs=[u_spec, prev_spec, w_spec, b_spec, u_spec], out_specs=[u_spec, w_spec, b_spec],
        out_shape=[SDS((t, c), F32), SDS((CONV_K, c), F32), SDS((1, c), F32)],
        scratch_shapes=[pltpu.VMEM((ts + CONV_HALO, tc), F32)],
        compiler_params=_cparams(("parallel", "arbitrary")))(u, u, w, b, dxc)


def _conv_bwd_in(dpre, w, seq):
    t, c = dpre.shape
    ts, tc = CONV_TS, GROUP_W
    hb = ts // CONV_HALO
    last = t // CONV_HALO - 1
    d_spec = pl.BlockSpec((ts, tc), lambda j, i: (i, j))
    next_spec = pl.BlockSpec((CONV_HALO, tc), lambda j, i: (jnp.minimum((i + 1) * hb, last), j))
    w_spec = pl.BlockSpec((CONV_K, tc), lambda j, i: (0, j))

    def body(d_ref, next_ref, w_ref, o_ref, ext):
        i = pl.program_id(1)
        nts = seq // ts
        is_last = (i % nts) == nts - 1
        ext[0:ts, :] = d_ref[...]
        ext[ts:, :] = jnp.where(is_last, 0.0, next_ref[...])
        wv = w_ref[...]

        def rows(j, carry):
            r0 = pl.multiple_of(j * CONV_ROWS, CONV_ROWS)
            blk = ext[pl.ds(r0, CONV_ROWS + CONV_HALO), :]
            acc = wv[CONV_K - 1:CONV_K] * blk[0:CONV_ROWS]
            for q in range(CONV_K - 1):
                acc = acc + wv[q:q + 1] * blk[CONV_K - 1 - q:CONV_K - 1 - q + CONV_ROWS]
            o_ref[pl.ds(r0, CONV_ROWS), :] = acc.astype(o_ref.dtype)
            return carry

        lax.fori_loop(0, ts // CONV_ROWS, rows, 0)

    return pl.pallas_call(
        body, name="conv_bwd_in", grid=(c // tc, t // ts), in_specs=[d_spec, next_spec, w_spec],
        out_specs=d_spec, out_shape=SDS((t, c), BF16), scratch_shapes=[pltpu.VMEM((ts + CONV_HALO, tc), F32)],
        compiler_params=_cparams(("parallel", "arbitrary")))(dpre, dpre, w)


def _split3(v):
    hi = v.astype(BF16)
    r1 = v - hi.astype(F32)
    mid = r1.astype(BF16)
    lo = (r1 - mid.astype(F32)).astype(BF16)
    return hi, mid, lo


def _ssd_prelude(dtr_ref, dtrt_ref, bias_ref, biast_ref, a_ref, at_ref):
    dt = _softplus(dtr_ref[...] + bias_ref[...])
    dtt = _softplus(dtrt_ref[...] + biast_ref[...])
    ri = lax.broadcasted_iota(jnp.int32, (CHUNK, CHUNK), 0)
    ci = lax.broadcasted_iota(jnp.int32, (CHUNK, CHUNK), 1)
    lower = ri >= ci
    upper = ri <= ci
    lower_b = jnp.where(lower, 1.0, 0.0).astype(BF16)
    upper_b = jnp.where(upper, 1.0, 0.0).astype(BF16)
    acs = sum(_dot(lower_b, p) for p in _split3(dt * a_ref[...]))
    acst = sum(_dot(p, upper_b) for p in _split3(dtt * at_ref[...]))
    return dt, acs, acst, lower, upper, lower_b, upper_b


def _ssd_specs(seq):
    nc = seq // CHUNK
    hg = HEADS_PER_GROUP
    row = lambda cc: (lambda g, b, c: (b * nc + cc(c), g))
    fwd = lambda c: c
    rev = lambda c: nc - 1 - c

    def specs(cc):
        return dict(
            xc=pl.BlockSpec((CHUNK, GROUP_W), lambda g, b, c: (b * nc + cc(c), g)),
            y=pl.BlockSpec((CHUNK, D_INNER // N_GROUPS), lambda g, b, c: (b * nc + cc(c), g)),
            dtr=pl.BlockSpec((None, CHUNK, hg), lambda g, b, c: (g, b * nc + cc(c), 0)),
            dtrt=pl.BlockSpec((None, None, hg, CHUNK), lambda g, b, c: (g, b, 0, cc(c))),
            prow=pl.BlockSpec((None, 1, hg), lambda g, b, c: (g, 0, 0)),
            pcol=pl.BlockSpec((None, hg, 1), lambda g, b, c: (g, 0, 0)),
            st=pl.BlockSpec((None, None, None, D_STATE, hg * HEAD_P), lambda g, b, c: (g, b, cc(c), 0, 0)),
        )

    return specs(fwd), specs(rev)


def _head_maps():
    hw = HEADS_PER_GROUP * HEAD_P
    shift = HEAD_P.bit_length() - 1
    hj = lax.broadcasted_iota(jnp.int32, (HEADS_PER_GROUP, hw), 0)
    lq = jnp.right_shift(lax.broadcasted_iota(jnp.int32, (HEADS_PER_GROUP, hw), 1), shift)
    spread = jnp.where(hj == lq, 1.0, 0.0).astype(BF16)
    rq = jnp.right_shift(lax.broadcasted_iota(jnp.int32, (hw, LANES), 0), shift)
    cj = lax.broadcasted_iota(jnp.int32, (hw, LANES), 1)
    gather = jnp.where(rq == cj, 1.0, 0.0).astype(BF16)
    return spread, gather


def _exact_dot(v, m01):
    return sum(_dot(p, m01) for p in _split3(v))


def _ssd_fwd(xc, dtr, dtrt, bias, biast, a, at, dskip, nb, seq):
    t = xc.shape[0]
    nc = seq // CHUNK
    hg = HEADS_PER_GROUP
    hw = hg * HEAD_P
    sp, _ = _ssd_specs(seq)

    def body(xc_ref, dtr_ref, dtrt_ref, bias_ref, biast_ref, a_ref, at_ref, d_ref, y_ref, sin_ref, st):
        @pl.when(pl.program_id(2) == 0)
        def _():
            st[...] = jnp.zeros_like(st)

        s_in = st[...]
        sin_ref[...] = s_in
        dt, acs, acst, lower, _, _, _ = _ssd_prelude(dtr_ref, dtrt_ref, bias_ref, biast_ref, a_ref, at_ref)
        spread, _ = _head_maps()
        x = xc_ref[...]
        xs = x[:, :hw]
        b16 = x[:, hw:hw + D_STATE].astype(BF16)
        c16 = x[:, hw + D_STATE:].astype(BF16)
        cb = _dot_nt(c16, b16)
        last = acs[CHUNK - 1:CHUNK, :]
        e_x = _exact_dot(jnp.exp(acs), spread)
        dec_x = _exact_dot(jnp.exp(last - acs), spread)
        tot_x = e_x[CHUNK - 1:CHUNK, :]
        d_x = _exact_dot(jnp.broadcast_to(d_ref[...], (8, hg)), spread)[0:1, :]
        xdtf = xs * _exact_dot(dt, spread)
        xdt16 = xdtf.astype(BF16)
        yoff = e_x * _dot(c16, s_in.astype(BF16))
        st[...] = tot_x * s_in + _dot_tn(b16, (dec_x * xdtf).astype(BF16))
        parts = []
        for j in range(hg):
            decay = jnp.exp(jnp.where(lower, acs[:, j:j + 1] - acst[j:j + 1, :], -jnp.inf))
            parts.append(_dot((cb * decay).astype(BF16), xdt16[:, HEAD_P * j:HEAD_P * (j + 1)]))
        y_ref[...] = jnp.concatenate(parts, axis=-1) + yoff + d_x * xs

    return pl.pallas_call(
        body, name="ssd_fwd", grid=(N_GROUPS, nb, nc),
        in_specs=[sp["xc"], sp["dtr"], sp["dtrt"], sp["prow"], sp["pcol"], sp["prow"], sp["pcol"], sp["prow"]],
        out_specs=[sp["y"], sp["st"]],
        out_shape=[SDS((t, D_INNER), F32), SDS((N_GROUPS, nb, nc, D_STATE, hw), F32)],
        scratch_shapes=[pltpu.VMEM((D_STATE, hw), F32)],
        compiler_params=_cparams(("parallel", "parallel", "arbitrary")))(xc, dtr, dtrt, bias, biast, a, at, dskip)


def _ssd_bwd(xc, dtr, dtrt, bias, biast, a, at, dskip, states, dy, nb, seq):
    t = xc.shape[0]
    nc = seq // CHUNK
    hg = HEADS_PER_GROUP
    hw = hg * HEAD_P
    _, sp = _ssd_specs(seq)

    def body(xc_ref, dtr_ref, dtrt_ref, bias_ref, biast_ref, a_ref, at_ref, d_ref, sin_ref, dy_ref,
             dxc_ref, ddtr_ref, gbias_ref, ga_ref, gd_ref, ds):
        first = (pl.program_id(1) == 0) & (pl.program_id(2) == 0)

        @pl.when(pl.program_id(2) == 0)
        def _():
            ds[...] = jnp.zeros_like(ds)

        @pl.when(first)
        def _():
            gbias_ref[...] = jnp.zeros_like(gbias_ref)
            ga_ref[...] = jnp.zeros_like(ga_ref)
            gd_ref[...] = jnp.zeros_like(gd_ref)

        dt, acs, acst, lower, upper, _, upper_b = _ssd_prelude(dtr_ref, dtrt_ref, bias_ref, biast_ref, a_ref, at_ref)
        spread, gather = _head_maps()
        x = xc_ref[...]
        dy = dy_ref[...]
        xs = x[:, :hw]
        b16 = x[:, hw:hw + D_STATE].astype(BF16)
        c16 = x[:, hw + D_STATE:].astype(BF16)
        dy16 = dy.astype(BF16)
        cb = _dot_nt(c16, b16)
        cbt = _dot_nt(b16, c16)
        last = acs[CHUNK - 1:CHUNK, :]
        e8 = jnp.exp(acs)
        dec8 = jnp.exp(last - acs)
        e_x = _exact_dot(e8, spread)
        dec_x = _exact_dot(dec8, spread)
        tot_x = e_x[CHUNK - 1:CHUNK, :]
        dt_x = _exact_dot(dt, spread)
        d_x = _exact_dot(jnp.broadcast_to(d_ref[...], (8, hg)), spread)[0:1, :]
        xdtf = xs * dt_x
        xdt16 = xdtf.astype(BF16)
        s_in = sin_ref[...]
        s16 = s_in.astype(BF16)
        ds_out = ds[...]
        ds16 = ds_out.astype(BF16)
        bds = _dot(b16, ds16)
        cs = _dot(c16, s16)
        edy16 = (e_x * dy).astype(BF16)
        ds[...] = tot_x * ds_out + _dot_tn(c16, edy16)
        lane8 = lax.broadcasted_iota(jnp.int32, (CHUNK, hg), 1)
        row8 = lax.broadcasted_iota(jnp.int32, (CHUNK, hg), 0)
        dacs8 = jnp.zeros((CHUNK, hg), F32)
        acc_m = jnp.zeros((CHUNK, CHUNK), F32)
        acc_mt = jnp.zeros((CHUNK, CHUNK), F32)
        dx_parts = []
        for j in range(hg):
            sl = slice(HEAD_P * j, HEAD_P * (j + 1))
            col = acs[:, j:j + 1]
            row = acst[j:j + 1, :]
            decay = jnp.exp(jnp.where(lower, col - row, -jnp.inf))
            decayt = jnp.exp(jnp.where(upper, row - col, -jnp.inf))
            wm = _dot_nt(dy16[:, sl], xdt16[:, sl]) * decay
            wmt = _dot_nt(xdt16[:, sl], dy16[:, sl]) * decayt
            acc_m = acc_m + wm
            acc_mt = acc_mt + wmt
            dacs8 = dacs8 + jnp.where(lane8 == j, jnp.sum(wm * cb, axis=-1, keepdims=True)
                                      - jnp.sum(wmt * cbt, axis=-1, keepdims=True), 0.0)
            dx_parts.append(_dot((cbt * decayt).astype(BF16), dy16[:, sl]))
        dx = jnp.concatenate(dx_parts, axis=-1) + dec_x * bds
        dxc_ref[:, :hw] = dx * dt_x + d_x * dy
        dxc_ref[:, hw:hw + D_STATE] = _dot(acc_mt.astype(BF16), c16) + _dot_nt((dec_x * xdtf).astype(BF16), ds16)
        dxc_ref[:, hw + D_STATE:] = _dot(acc_m.astype(BF16), b16) + _dot_nt(edy16, s16)
        dtot_rows = jnp.broadcast_to(_colsum(ds_out * s_in), (8, hw))
        sums = _exact_dot(jnp.concatenate([dy * cs, xdtf * bds, dx * xs, dy * xs, dtot_rows], axis=0), gather)
        de8 = sums[0:CHUNK, :hg]
        ddec8 = sums[CHUNK:2 * CHUNK, :hg]
        ddtx8 = sums[2 * CHUNK:3 * CHUNK, :hg]
        gd8 = _colsum(sums[3 * CHUNK:4 * CHUNK, :hg])
        dtot8 = sums[4 * CHUNK:4 * CHUNK + 1, :hg]
        extra = _colsum(ddec8 * dec8) + dtot8 * e8[CHUNK - 1:CHUNK, :]
        dacs8 = dacs8 + de8 * e8 - ddec8 * dec8 + jnp.where(row8 == CHUNK - 1, extra, 0.0)
        da = sum(_dot(upper_b, p) for p in _split3(dacs8))
        av = a_ref[...]
        ddt = da * av + ddtx8
        ddtr = ddt * _sigmoid(dtr_ref[...] + bias_ref[...])
        ddtr_ref[...] = ddtr
        gbias_ref[...] += _colsum(ddtr)
        ga_ref[...] += _colsum(da * dt) * av
        gd_ref[...] += gd8

    return pl.pallas_call(
        body, name="ssd_bwd", grid=(N_GROUPS, nb, nc),
        in_specs=[sp["xc"], sp["dtr"], sp["dtrt"], sp["prow"], sp["pcol"], sp["prow"], sp["pcol"], sp["prow"],
                  sp["st"], sp["y"]],
        out_specs=[sp["xc"], sp["dtr"], sp["prow"], sp["prow"], sp["prow"]],
        out_shape=[SDS((t, N_GROUPS * GROUP_W), F32), SDS((N_GROUPS, t, hg), F32)]
        + [SDS((N_GROUPS, 1, hg), F32)] * 3,
        scratch_shapes=[pltpu.VMEM((D_STATE, hw), F32)],
        compiler_params=_cparams(("arbitrary", "arbitrary", "arbitrary")))(
            xc, dtr, dtrt, bias, biast, a, at, dskip, states, dy)


def _group_bcast(v, width, fn):
    parts = []
    for q in range(v.shape[-1] // width):
        s = fn(v[:, q * width:(q + 1) * width])
        parts.append(jnp.broadcast_to(s, (v.shape[0], width)))
    return jnp.concatenate(parts, axis=-1)


def _gate_norm_fwd(y, z, g):
    t, d = y.shape
    tm = 256
    gw = d // N_GROUPS

    def fn(i, y_ref, z_ref, g_ref):
        zv = z_ref[...].astype(F32)
        u = y_ref[...] * (zv * _sigmoid(zv))
        r = lax.rsqrt(_group_bcast(u * u, gw, lambda p: jnp.mean(p, axis=-1, keepdims=True)) + EPS)
        return [u * r * g_ref[...]]

    return _rw("gate_norm_fwd", fn, t // tm, [(y, _rs(tm, d)), (z, _rs(tm, d)), (g, _fs((1, d)))],
               [(SDS((t, d), BF16), _rs(tm, d))])[0]


def _gate_norm_bwd(y, z, g, dyn):
    t, d = y.shape
    tm = 256
    gw = d // N_GROUPS

    def fn(i, y_ref, z_ref, g_ref, dyn_ref):
        zv = z_ref[...].astype(F32)
        yv = y_ref[...]
        sg = _sigmoid(zv)
        sz = zv * sg
        u = yv * sz
        r = lax.rsqrt(_group_bcast(u * u, gw, lambda p: jnp.mean(p, axis=-1, keepdims=True)) + EPS)
        uhat = u * r
        dv = dyn_ref[...].astype(F32)
        duhat = dv * g_ref[...]
        du = r * (duhat - uhat * _group_bcast(duhat * uhat, gw, lambda p: jnp.mean(p, axis=-1, keepdims=True)))
        dz = du * yv * sg * (1.0 + zv * (1.0 - sg))
        return [du * sz, dz, _colsum(dv * uhat)]

    return _rw("gate_norm_bwd", fn, t // tm,
               [(y, _rs(tm, d)), (z, _rs(tm, d)), (g, _fs((1, d))), (dyn, _rs(tm, d))],
               [(SDS((t, d), F32), _rs(tm, d)), (SDS((t, d), BF16), _rs(tm, d)), (SDS((1, d), F32), _fs((1, d)))],
               n_acc=1)


def _rope_tables(seq):
    half = ATT_D // 2
    inv = ROPE_THETA ** (-jnp.arange(half, dtype=F32) / half)
    ang = jnp.arange(seq, dtype=F32)[:, None] * inv[None, :]
    cos, sin = jnp.cos(ang), jnp.sin(ang)
    return jnp.concatenate([cos, cos], axis=-1), jnp.concatenate([-sin, sin], axis=-1)


ATT_TILE = 512


def _strided_spec(r, mtiles):
    return pl.BlockSpec((None, r, None, ATT_TILE // r, ATT_W), lambda i: (i // mtiles, 0, i % mtiles, 0, 0))


def _strided_shape(nb, r, mtiles, dtype):
    return SDS((nb, r, mtiles, ATT_TILE // r, ATT_W), dtype)


def _to_strided(val, out_ref, lanes, r, sc):
    if r == 1:
        out_ref[0, :, lanes] = val.astype(out_ref.dtype)
        return
    sc[...] = val
    for rr in range(r):
        out_ref[rr, :, lanes] = sc[pl.ds(rr, ATT_TILE // r, stride=r), :].astype(out_ref.dtype)


def _from_strided(in_ref, lanes, r, sc):
    if r == 1:
        return in_ref[0, :, lanes].astype(F32)
    for rr in range(r):
        sc[pl.ds(rr, ATT_TILE // r, stride=r), :] = in_ref[rr, :, lanes].astype(F32)
    return sc[...]


def _rope_fwd(qkv, cos, sin, nb, seq):
    t = qkv.shape[0]
    tm = ATT_TILE
    mtiles = seq // tm
    w = ATT_HEADS * ATT_D
    tab = pl.BlockSpec((tm, ATT_D), lambda i: (i % mtiles, 0))
    ng = len(ATT_DILATIONS)

    def body(q_ref, k_ref, v_ref, cos_ref, sin_ref, *rest):
        outs, sc = rest[:3 * ng], rest[3 * ng]
        c, s = cos_ref[...], sin_ref[...]
        for which, ref in enumerate((q_ref, k_ref, v_ref)):
            for h in range(ATT_HEADS):
                g, slot = divmod(h, ATT_SLOTS)
                p = ref[:, h * ATT_D:(h + 1) * ATT_D].astype(F32)
                if which < 2:
                    p = p * c + pltpu.roll(p, ATT_D // 2, 1) * s
                _to_strided(p, outs[which * ng + g], slice(slot * ATT_D, (slot + 1) * ATT_D), ATT_DILATIONS[g], sc)

    out_specs = [_strided_spec(r, mtiles) for _ in range(3) for r in ATT_DILATIONS]
    out_shape = [_strided_shape(nb, r, mtiles, BF16) for _ in range(3) for r in ATT_DILATIONS]
    outs = pl.pallas_call(
        body, name="rope_fwd", grid=(t // tm,),
        in_specs=[_rs(tm, w, 0), _rs(tm, w, 1), _rs(tm, w, 2), tab, tab], out_specs=out_specs, out_shape=out_shape,
        scratch_shapes=[pltpu.VMEM((tm, ATT_D), F32)], compiler_params=_cparams(("arbitrary",)))(
            qkv, qkv, qkv, cos, sin)
    flat = [o.reshape(t, ATT_W) for o in outs]
    return flat[0:ng], flat[ng:2 * ng], flat[2 * ng:]


def _rope_bwd(dq, dk, dv, cos, sin, nb, seq):
    t = dq[0].shape[0]
    tm = ATT_TILE
    mtiles = seq // tm
    w = ATT_HEADS * ATT_D
    tab = pl.BlockSpec((tm, ATT_D), lambda i: (i % mtiles, 0))
    ng = len(ATT_DILATIONS)

    def body(*refs):
        ins, (cos_ref, sin_ref, o_ref, sc) = refs[:3 * ng], refs[3 * ng:]
        c, s = cos_ref[...], sin_ref[...]
        for which in range(3):
            for h in range(ATT_HEADS):
                g, slot = divmod(h, ATT_SLOTS)
                p = _from_strided(ins[which * ng + g], slice(slot * ATT_D, (slot + 1) * ATT_D), ATT_DILATIONS[g], sc)
                if which < 2:
                    p = p * c - pltpu.roll(p, ATT_D // 2, 1) * s
                o_ref[:, which * w + h * ATT_D:which * w + (h + 1) * ATT_D] = p.astype(o_ref.dtype)

    views = [a.reshape(nb, r, mtiles, tm // r, ATT_W) for grp in (dq, dk, dv) for a, r in zip(grp, ATT_DILATIONS)]
    return pl.pallas_call(
        body, name="rope_bwd", grid=(t // tm,),
        in_specs=[_strided_spec(r, mtiles) for _ in range(3) for r in ATT_DILATIONS] + [tab, tab],
        out_specs=_rs(tm, 3 * w), out_shape=SDS((t, 3 * w), BF16),
        scratch_shapes=[pltpu.VMEM((tm, ATT_D), F32)], compiler_params=_cparams(("arbitrary",)))(*views, cos, sin)


def _att_masks():
    ri = lax.broadcasted_iota(jnp.int32, (ATT_BLOCK, ATT_BLOCK), 0)
    ci = lax.broadcasted_iota(jnp.int32, (ATT_BLOCK, ATT_BLOCK), 1)
    return ci <= ri, ci >= ri


def _att_fwd(q, k, v, g, seq):
    t, w = q.shape
    nblk = t // ATT_BLOCK
    nbs = seq // ATT_DILATIONS[g] // ATT_BLOCK
    scale = ATT_D ** -0.5
    cur = pl.BlockSpec((ATT_BLOCK, w), lambda n: (n, 0))
    prev = pl.BlockSpec((ATT_BLOCK, w), lambda n: (jnp.maximum(n - 1, 0), 0))

    def body(q_ref, kc_ref, kp_ref, vc_ref, vp_ref, o_ref, lse_ref):
        has_prev = (pl.program_id(0) % nbs) != 0
        mcur, mprev = _att_masks()
        mask = jnp.concatenate([mprev & has_prev, mcur], axis=-1)
        for h in range(ATT_SLOTS):
            sl = slice(h * ATT_D, (h + 1) * ATT_D)
            keys = jnp.concatenate([kp_ref[:, sl], kc_ref[:, sl]], axis=0)
            vals = jnp.concatenate([vp_ref[:, sl], vc_ref[:, sl]], axis=0)
            s = jnp.where(mask, _dot_nt(q_ref[:, sl], keys) * scale, -jnp.inf)
            m = jnp.max(s, axis=-1, keepdims=True)
            p = jnp.exp(s - m)
            den = jnp.sum(p, axis=-1, keepdims=True)
            o_ref[:, sl] = _dot(p.astype(BF16), vals) / den
            lse_ref[:, sl] = jnp.broadcast_to(m + jnp.log(den), (ATT_BLOCK, ATT_D))

    return pl.pallas_call(
        body, name=f"att_fwd_{g}", grid=(nblk,), in_specs=[cur, cur, prev, cur, prev], out_specs=[cur, cur],
        out_shape=[SDS((t, w), F32), SDS((t, w), F32)],
        compiler_params=_cparams(("arbitrary",)))(q, k, k, v, v)


def _att_bwd(q, k, v, do, lse, dlt, g, seq):
    t, w = q.shape
    nblk = t // ATT_BLOCK
    nbs = seq // ATT_DILATIONS[g] // ATT_BLOCK
    scale = ATT_D ** -0.5
    cur = pl.BlockSpec((ATT_BLOCK, w), lambda n: (n, 0))
    nxt = pl.BlockSpec((ATT_BLOCK, w), lambda n: (jnp.minimum(n + 1, nblk - 1), 0))

    def body(qc_ref, qn_ref, k_ref, v_ref, doc_ref, don_ref, lsec_ref, lsen_ref, dltc_ref, dltn_ref,
             dq_ref, dk_ref, dv_ref, carry):
        n = pl.program_id(0)

        @pl.when((n % nbs) == 0)
        def _():
            carry[...] = jnp.zeros_like(carry)

        has_next = (((n + 1) % nbs) != 0) & (n + 1 < nblk)
        mcur, mprev = _att_masks()
        mask = jnp.concatenate([mcur, mprev & has_next], axis=0)
        for h in range(ATT_SLOTS):
            sl = slice(h * ATT_D, (h + 1) * ATT_D)
            kh, vh = k_ref[:, sl], v_ref[:, sl]
            qs = jnp.concatenate([qc_ref[:, sl], qn_ref[:, sl]], axis=0)
            dos = jnp.concatenate([doc_ref[:, sl], don_ref[:, sl]], axis=0)
            lse = jnp.concatenate([lsec_ref[:, sl], lsen_ref[:, sl]], axis=0)
            dlt = jnp.concatenate([dltc_ref[:, sl], dltn_ref[:, sl]], axis=0)
            p = jnp.where(mask, jnp.exp(_dot_nt(qs, kh) * scale - lse), 0.0)
            ds = (p * (_dot_nt(dos, vh) - dlt) * scale).astype(BF16)
            dqs = _dot(ds, kh)
            dq_ref[:, sl] = (carry[:, sl] + dqs[:ATT_BLOCK]).astype(dq_ref.dtype)
            carry[:, sl] = dqs[ATT_BLOCK:]
            dk_ref[:, sl] = _dot_tn(ds, qs).astype(dk_ref.dtype)
            dv_ref[:, sl] = _dot_tn(p.astype(BF16), dos).astype(dv_ref.dtype)

    return pl.pallas_call(
        body, name=f"att_bwd_{g}", grid=(nblk,), in_specs=[cur, nxt, cur, cur, cur, nxt, cur, nxt, cur, nxt],
        out_specs=[cur, cur, cur], out_shape=[SDS((t, w), BF16)] * 3,
        scratch_shapes=[pltpu.VMEM((ATT_BLOCK, w), F32)],
        compiler_params=_cparams(("arbitrary",)))(q, q, k, v, do, do, lse, lse, dlt, dlt)


def _merge_weights(ls):
    m = jnp.maximum(jnp.maximum(ls[0], ls[1]), ls[2])
    es = [jnp.exp(v - m) for v in ls]
    den = es[0] + es[1] + es[2]
    return [e / den for e in es]


def _merge_fwd(o, lse, nb, seq):
    t = o[0].shape[0]
    tm = ATT_TILE
    mtiles = seq // tm
    ng = len(ATT_DILATIONS)

    def body(*refs):
        o_refs, l_refs, out_ref, scs = refs[:ng], refs[ng:2 * ng], refs[2 * ng], refs[2 * ng + 1:]
        for slot in range(ATT_SLOTS):
            lanes = slice(slot * ATT_D, (slot + 1) * ATT_D)
            ov = [_from_strided(o_refs[g], lanes, r, scs[2 * g]) for g, r in enumerate(ATT_DILATIONS)]
            ws = _merge_weights([_from_strided(l_refs[g], lanes, r, scs[2 * g + 1])
                                 for g, r in enumerate(ATT_DILATIONS)])
            out_ref[:, lanes] = (ws[0] * ov[0] + ws[1] * ov[1] + ws[2] * ov[2]).astype(out_ref.dtype)

    views = [a.reshape(nb, r, mtiles, tm // r, ATT_W) for grp in (o, lse) for a, r in zip(grp, ATT_DILATIONS)]
    return pl.pallas_call(
        body, name="att_merge_fwd", grid=(t // tm,),
        in_specs=[_strided_spec(r, mtiles) for _ in range(2) for r in ATT_DILATIONS],
        out_specs=_rs(tm, ATT_W), out_shape=SDS((t, ATT_W), BF16),
        scratch_shapes=[pltpu.VMEM((tm, ATT_D), F32)] * (2 * ng), compiler_params=_cparams(("arbitrary",)))(*views)


def _merge_bwd(o, lse, datt, nb, seq):
    t = o[0].shape[0]
    tm = ATT_TILE
    mtiles = seq // tm
    ng = len(ATT_DILATIONS)

    def body(*refs):
        o_refs, l_refs, d_ref = refs[:ng], refs[ng:2 * ng], refs[2 * ng]
        do_refs, dlt_refs = refs[2 * ng + 1:3 * ng + 1], refs[3 * ng + 1:4 * ng + 1]
        scs = refs[4 * ng + 1:]
        for slot in range(ATT_SLOTS):
            lanes = slice(slot * ATT_D, (slot + 1) * ATT_D)
            ov = [_from_strided(o_refs[g], lanes, r, scs[2 * g]) for g, r in enumerate(ATT_DILATIONS)]
            ws = _merge_weights([_from_strided(l_refs[g], lanes, r, scs[2 * g + 1])
                                 for g, r in enumerate(ATT_DILATIONS)])
            dv = d_ref[:, lanes]
            att = ws[0] * ov[0] + ws[1] * ov[1] + ws[2] * ov[2]
            dot = jnp.broadcast_to(jnp.sum(dv * att, axis=-1, keepdims=True), (tm, ATT_D))
            for g, r in enumerate(ATT_DILATIONS):
                _to_strided(ws[g] * dv, do_refs[g], lanes, r, scs[2 * ng])
                _to_strided(ws[g] * dot, dlt_refs[g], lanes, r, scs[2 * ng + 1])

    views = [a.reshape(nb, r, mtiles, tm // r, ATT_W) for grp in (o, lse) for a, r in zip(grp, ATT_DILATIONS)]
    outs = pl.pallas_call(
        body, name="att_merge_bwd", grid=(t // tm,),
        in_specs=[_strided_spec(r, mtiles) for _ in range(2) for r in ATT_DILATIONS] + [_rs(tm, ATT_W)],
        out_specs=[_strided_spec(r, mtiles) for _ in range(2) for r in ATT_DILATIONS],
        out_shape=[_strided_shape(nb, r, mtiles, dt) for dt in (BF16, F32) for r in ATT_DILATIONS],
        scratch_shapes=[pltpu.VMEM((tm, ATT_D), F32)] * (2 * ng + 2), compiler_params=_cparams(("arbitrary",)))(
            *views, datt)
    flat = [a.reshape(t, ATT_W) for a in outs]
    return flat[:ng], flat[ng:]


def _mix_fwd(gate_logits, b_gate, y_ssm, y_att):
    t, d = y_ssm.shape
    tm = 512

    def fn(i, g0_ref, g1_ref, b0_ref, b1_ref, ys_ref, ya_ref):
        g0 = _sigmoid(g0_ref[...].astype(F32) + b0_ref[...])
        g1 = _sigmoid(g1_ref[...].astype(F32) + b1_ref[...])
        return [g0 * ys_ref[...].astype(F32) + g1 * ya_ref[...].astype(F32)]

    b_spec = lambda cb: pl.BlockSpec((1, d), lambda i: (0, cb))
    return _rw("mix_fwd", fn, t // tm,
               [(gate_logits, _rs(tm, d, 0)), (gate_logits, _rs(tm, d, 1)), (b_gate, b_spec(0)), (b_gate, b_spec(1)),
                (y_ssm, _rs(tm, d)), (y_att, _rs(tm, d))],
               [(SDS((t, d), BF16), _rs(tm, d))])[0]


def _mix_bwd(gate_logits, b_gate, y_ssm, y_att, dmixed):
    t, d = y_ssm.shape
    tm = 256

    def fn(i, g0_ref, g1_ref, b0_ref, b1_ref, ys_ref, ya_ref, dm_ref):
        g0 = _sigmoid(g0_ref[...].astype(F32) + b0_ref[...])
        g1 = _sigmoid(g1_ref[...].astype(F32) + b1_ref[...])
        dm = dm_ref[...]
        dg = jnp.concatenate([dm * ys_ref[...].astype(F32) * g0 * (1.0 - g0),
                              dm * ya_ref[...].astype(F32) * g1 * (1.0 - g1)], axis=-1)
        return [dm * g0, dm * g1, dg, _colsum(dg)]

    b_spec = lambda cb: pl.BlockSpec((1, d), lambda i: (0, cb))
    return _rw("mix_bwd", fn, t // tm,
               [(gate_logits, _rs(tm, d, 0)), (gate_logits, _rs(tm, d, 1)), (b_gate, b_spec(0)), (b_gate, b_spec(1)),
                (y_ssm, _rs(tm, d)), (y_att, _rs(tm, d)), (dmixed, _rs(tm, d))],
               [(SDS((t, d), BF16), _rs(tm, d)), (SDS((t, d), BF16), _rs(tm, d)),
                (SDS((t, 2 * d), BF16), _rs(tm, 2 * d)), (SDS((1, 2 * d), F32), _fs((1, 2 * d)))], n_acc=1)


def _swiglu_fwd(gt, up):
    t, f = gt.shape
    tm = 256

    def fn(i, g_ref, u_ref):
        gv = g_ref[...].astype(F32)
        return [gv * _sigmoid(gv) * u_ref[...].astype(F32)]

    return _rw("swiglu_fwd", fn, t // tm, [(gt, _rs(tm, f)), (up, _rs(tm, f))], [(SDS((t, f), BF16), _rs(tm, f))])[0]


def _swiglu_bwd(gt, up, dact):
    t, f = gt.shape
    tm = 256

    def fn(i, g_ref, u_ref, d_ref):
        gv, dv = g_ref[...].astype(F32), d_ref[...].astype(F32)
        sg = _sigmoid(gv)
        return [dv * u_ref[...].astype(F32) * sg * (1.0 + gv * (1.0 - sg)), dv * gv * sg]

    return _rw("swiglu_bwd", fn, t // tm, [(gt, _rs(tm, f)), (up, _rs(tm, f)), (dact, _rs(tm, f))],
               [(SDS((t, f), BF16), _rs(tm, f))] * 2)


def _adamw(w, g, m, v, name):
    r, c = w.shape
    tr = _row_tile(r, max(8, 400_000 // c))
    c1 = 1.0 / (1.0 - ADAM_B1 ** ADAM_STEP)
    c2 = 1.0 / (1.0 - ADAM_B2 ** ADAM_STEP)

    def fn(i, w_ref, g_ref, m_ref, v_ref):
        gv = g_ref[...]
        mn = ADAM_B1 * m_ref[...] + (1.0 - ADAM_B1) * gv
        vn = ADAM_B2 * v_ref[...] + (1.0 - ADAM_B2) * (gv * gv)
        delta = -ADAM_LR * ((mn * c1) / (jnp.sqrt(vn * c2) + ADAM_EPS) + ADAM_WD * w_ref[...])
        return [delta, mn, vn]

    spec = pl.BlockSpec((tr, c), lambda i: (i, 0))
    return _rw(name, fn, r // tr, [(w, spec), (g, spec), (m, spec), (v, spec)], [(SDS((r, c), F32), spec)] * 3)


ANY = pl.BlockSpec(memory_space=pl.ANY)


def _place():
    x, y, c = lax.axis_index("x"), lax.axis_index("y"), lax.axis_index("c")
    chips = [(1 - x, y), (x, 1 - y), (1 - x, 1 - y)]
    return x, y, c, chips


def _remote(src, dst, ssem, rsem, to):
    return pltpu.make_async_remote_copy(src_ref=src, dst_ref=dst, send_sem=ssem, recv_sem=rsem, device_id=to,
                                        device_id_type=MESH)


def _copy_through_vmem(src, dst, buf, isem, osem):
    chunk = buf.shape[1]
    n = src.shape[0] // chunk
    load = lambda k: pltpu.make_async_copy(src.at[pl.ds(k * chunk, chunk)], buf.at[k % 2], isem.at[k % 2])
    store = lambda k: pltpu.make_async_copy(buf.at[k % 2], dst.at[pl.ds(k * chunk, chunk)], osem.at[k % 2])
    load(0).start()
    for k in range(n):
        load(k).wait()
        if k + 1 < n:
            if k >= 1:
                store(k - 1).wait()
            load(k + 1).start()
        store(k).start()
    if n >= 2:
        store(n - 2).wait()
    store(n - 1).wait()


def _copy_scratch(rows, width, dtype):
    chunk = _row_tile(rows, 512)
    return [pltpu.VMEM((2, chunk, width), dtype), pltpu.SemaphoreType.DMA((2,)), pltpu.SemaphoreType.DMA((2,))]


def _gather_weights(wp):
    def body(w_ref, out_ref, ssem, rsem, buf, isem, osem):
        x, y, c, chips = _place()
        me = 2 * x + y
        sib = (x, y, 1 - c)
        first = [_remote(w_ref.at[c], out_ref.at[me, c], ssem.at[j], rsem.at[j], (*chip, c))
                 for j, chip in enumerate(chips)]
        for cp in first:
            cp.start()
        for half in range(2):
            _copy_through_vmem(w_ref.at[half], out_ref.at[me, half], buf, isem, osem)
        passed = []
        for j, chip in enumerate(chips):
            ci = 2 * chip[0] + chip[1]
            _remote(w_ref.at[c], out_ref.at[ci, c], ssem.at[j], rsem.at[j], (*chip, c)).wait_recv()
            cp = _remote(out_ref.at[ci, c], out_ref.at[ci, c], ssem.at[3 + j], rsem.at[3 + j], sib)
            cp.start()
            passed.append(cp)
        for j, chip in enumerate(chips):
            ci = 2 * chip[0] + chip[1]
            _remote(out_ref.at[ci, 1 - c], out_ref.at[ci, 1 - c], ssem.at[3 + j], rsem.at[3 + j], sib).wait_recv()
        for cp in first + passed:
            cp.wait_send()

    return pl.pallas_call(
        body, name="gather_weights", in_specs=[ANY], out_specs=ANY,
        out_shape=SDS((N_CHIPS,) + wp.shape, wp.dtype),
        scratch_shapes=[pltpu.SemaphoreType.DMA((6,)), pltpu.SemaphoreType.DMA((6,))]
        + _copy_scratch(wp.shape[1], wp.shape[2], wp.dtype),
        compiler_params=pltpu.CompilerParams(has_side_effects=True))(wp)


def _swap_halves(g2):
    def body(g_ref, out_ref, ssem, rsem):
        x, y, c, _ = _place()
        cp = _remote(g_ref.at[1 - c], out_ref, ssem, rsem, (x, y, 1 - c))
        cp.start()
        cp.wait()

    return pl.pallas_call(
        body, name="swap_halves", in_specs=[ANY], out_specs=ANY, out_shape=SDS(g2.shape[1:], g2.dtype),
        scratch_shapes=[pltpu.SemaphoreType.DMA(()), pltpu.SemaphoreType.DMA(())],
        compiler_params=pltpu.CompilerParams(has_side_effects=True))(g2)


def _add_own_half(g2, other, c):
    _, nch, rows, w = g2.shape
    tr = _row_tile(rows, 512)
    nr = rows // tr

    def body(c_ref, a_ref, b_ref, o_ref):
        o_ref[...] = (a_ref[...].astype(F32) + b_ref[...].astype(F32)).astype(o_ref.dtype)

    grid_spec = pltpu.PrefetchScalarGridSpec(
        num_scalar_prefetch=1, grid=(nch, nr),
        in_specs=[pl.BlockSpec((None, None, tr, w), lambda k, i, c_ref: (c_ref[0], k, i, 0)),
                  pl.BlockSpec((None, tr, w), lambda k, i, c_ref: (k, i, 0))],
        out_specs=pl.BlockSpec((None, tr, w), lambda k, i, c_ref: (k, i, 0)))
    return pl.pallas_call(
        body, name="add_own_half", grid_spec=grid_spec, out_shape=SDS(other.shape, other.dtype),
        compiler_params=_cparams(("arbitrary", "arbitrary")))(jnp.reshape(c, (1,)).astype(jnp.int32), g2, other)


def _scatter_to_chips(p):
    def body(p_ref, q_ref, ssem, rsem, buf, isem, osem):
        x, y, c, chips = _place()
        me = 2 * x + y
        sent = []
        for j, chip in enumerate(chips):
            ci = 2 * chip[0] + chip[1]
            cp = _remote(p_ref.at[ci], q_ref.at[me], ssem.at[j], rsem.at[j], (*chip, c))
            cp.start()
            sent.append(cp)
        _copy_through_vmem(p_ref.at[me], q_ref.at[me], buf, isem, osem)
        for j, chip in enumerate(chips):
            ci = 2 * chip[0] + chip[1]
            _remote(p_ref.at[ci], q_ref.at[ci], ssem.at[j], rsem.at[j], (*chip, c)).wait_recv()
        for cp in sent:
            cp.wait_send()

    return pl.pallas_call(
        body, name="scatter_to_chips", in_specs=[ANY], out_specs=ANY, out_shape=SDS(p.shape, p.dtype),
        scratch_shapes=[pltpu.SemaphoreType.DMA((3,)), pltpu.SemaphoreType.DMA((3,))]
        + _copy_scratch(p.shape[1], p.shape[2], p.dtype),
        compiler_params=pltpu.CompilerParams(has_side_effects=True))(p)


def _sum_chips(q):
    nch, rows, w = q.shape
    tr = _row_tile(rows, 512)

    def fn(i, q_ref):
        return [((q_ref[0].astype(F32) + q_ref[1].astype(F32)) + q_ref[2].astype(F32)) + q_ref[3].astype(F32)]

    return _rw("sum_chips", fn, rows // tr, [(q, pl.BlockSpec((nch, tr, w), lambda i: (0, i, 0)))],
               [(SDS((rows, w), F32), _rs(tr, w))])[0]


def _allreduce_small(v, name):
    rows, w = v.shape
    offsets = [(dx, dy, dc) for dx in (0, 1) for dy in (0, 1) for dc in (0, 1)][1:]

    def body(v_ref, o_ref, buf, ssem, rsem):
        x, y, c, _ = _place()
        flip = lambda p, d: 1 - p if d else p
        peers = [(flip(x, dx), flip(y, dy), flip(c, dc)) for dx, dy, dc in offsets]
        index = lambda p: 4 * p[0] + 2 * p[1] + p[2]
        me = index((x, y, c))
        buf[me] = v_ref[...]
        sent = [_remote(v_ref, buf.at[me], ssem.at[q], rsem.at[q], p) for q, p in enumerate(peers)]
        for cp in sent:
            cp.start()
        for q, p in enumerate(peers):
            _remote(v_ref, buf.at[index(p)], ssem.at[q], rsem.at[q], p).wait_recv()
        for cp in sent:
            cp.wait_send()
        acc = buf[0]
        for q in range(1, 8):
            acc = acc + buf[q]
        o_ref[...] = acc

    vm = pl.BlockSpec(memory_space=pltpu.VMEM)
    return pl.pallas_call(
        body, name=name, in_specs=[vm], out_specs=vm, out_shape=SDS((rows, w), F32),
        scratch_shapes=[pltpu.VMEM((8, rows, w), F32), pltpu.SemaphoreType.DMA((7,)), pltpu.SemaphoreType.DMA((7,))],
        compiler_params=pltpu.CompilerParams(has_side_effects=True))(v)


def _join_halves(h):
    def body(h_ref, out_ref, ssem, rsem, buf, isem, osem):
        x, y, c, _ = _place()
        cp = _remote(h_ref, out_ref.at[c], ssem, rsem, (x, y, 1 - c))
        cp.start()
        _copy_through_vmem(h_ref, out_ref.at[c], buf, isem, osem)
        _remote(h_ref, out_ref.at[1 - c], ssem, rsem, (x, y, 1 - c)).wait_recv()
        cp.wait_send()

    return pl.pallas_call(
        body, name="join_halves", in_specs=[ANY], out_specs=ANY, out_shape=SDS((2,) + h.shape, h.dtype),
        scratch_shapes=[pltpu.SemaphoreType.DMA(()), pltpu.SemaphoreType.DMA(())]
        + _copy_scratch(h.shape[0], h.shape[1], h.dtype),
        compiler_params=pltpu.CompilerParams(has_side_effects=True))(h)


PACK_W = 1024
SHARDED = ("w_in", "w_ffn_gate", "w_ffn_up", "w_ssm_out", "w_att_out", "w_mix_out", "w_ffn_down")
COL_SHARDED = ("w_in", "w_ffn_gate", "w_ffn_up", "w_att_out")
SMALL = ("norm_mix", "b_gate", "conv_b", "dt_bias", "a_log", "d_skip", "ssm_norm", "norm_ffn", "norm_final")


PACK_ROW_ALIGN = 16


def _rows(n):
    return -(-n // (PACK_W * PACK_ROW_ALIGN)) * PACK_ROW_ALIGN


def _pack_rows(parts, total_rows):
    rows = []
    for p in parts:
        flat = p.reshape(-1)
        pad = _rows(flat.shape[0]) * PACK_W - flat.shape[0]
        if pad:
            flat = jnp.concatenate([flat, jnp.zeros((pad,), flat.dtype)])
        rows.append(flat.reshape(-1, PACK_W))
    used = sum(r.shape[0] for r in rows)
    if total_rows > used:
        rows.append(jnp.zeros((total_rows - used, PACK_W), rows[0].dtype))
    return jnp.concatenate(rows, axis=0)


def _padded_rows(n):
    return -(-n // 32) * 32


def _wire_name(name):
    return name + "_t" if name in COL_SHARDED else name


def _wire_shard(w, name):
    return w.T if name in COL_SHARDED else w


def _group_major(a, axis):
    gw = D_INNER // N_GROUPS
    take = lambda lo, n: lax.slice_in_dim(a, lo, lo + n, axis=axis)
    parts = []
    for g in range(N_GROUPS):
        parts += [take(g * gw, gw), take(D_INNER + g * D_STATE, D_STATE),
                  take(D_INNER + N_GROUPS * D_STATE + g * D_STATE, D_STATE)]
    return jnp.concatenate(parts, axis=axis)


def _group_major_inv(a, axis):
    gw = D_INNER // N_GROUPS
    take = lambda lo, n: lax.slice_in_dim(a, lo, lo + n, axis=axis)
    xs = [take(g * GROUP_W, gw) for g in range(N_GROUPS)]
    bs = [take(g * GROUP_W + gw, D_STATE) for g in range(N_GROUPS)]
    cs = [take(g * GROUP_W + gw + D_STATE, D_STATE) for g in range(N_GROUPS)]
    return jnp.concatenate(xs + bs + cs, axis=axis)


def _local_step(x, target, wts):
    nb, seq, d = x.shape
    t = nb * seq
    x = x.reshape(t, d)
    target = target.reshape(t, d)
    hg = HEADS_PER_GROUP

    w_in_t = wts["w_in_t"]
    o1, o2, o3, o4 = D_INNER, D_INNER + CONV_DIM, D_INNER + CONV_DIM + N_HEADS, D_INNER + CONV_DIM + N_HEADS + QKV_DIM
    w_z = w_in_t[:o1]
    w_xbc = _group_major(w_in_t[o1:o2], 0)
    w_dt = jnp.pad(w_in_t[o2:o3], ((0, DT_PAD - N_HEADS), (0, 0)))
    w_qkv = w_in_t[o3:o4]
    w_gate = w_in_t[o4:]
    conv_w = _group_major(wts["conv_w"], 1)
    conv_b = _group_major(wts["conv_b"], 1)

    def per_group_row(p):
        return p.reshape(N_GROUPS, 1, hg)

    def per_group_col(p):
        return p.reshape(N_GROUPS, hg, 1)

    a_neg = -jnp.exp(wts["a_log"])
    bias_r, bias_c = per_group_row(wts["dt_bias"]), per_group_col(wts["dt_bias"])
    a_r, a_c = per_group_row(a_neg), per_group_col(a_neg)
    dskip_r = per_group_row(wts["d_skip"])
    cos, sin = _rope_tables(seq)

    h = _rms_fwd(x, wts["norm_mix"], "rms_mix_fwd")
    z = _mm(h, w_z, "nt", BF16, "proj_z")
    xbc = _mm(h, w_xbc, "nt", F32, "proj_xbc")
    dt_raw = _mm(h, w_dt, "nt", F32, "proj_dt")
    qkv = _mm(h, w_qkv, "nt", BF16, "proj_qkv")
    gate_logits = _mm(h, w_gate, "nt", BF16, "proj_gate")

    xc = _conv_fwd(xbc, conv_w, conv_b, seq)
    dtr = dt_raw[:, :N_HEADS].reshape(t, N_GROUPS, hg).transpose(1, 0, 2)
    dtrt = dt_raw[:, :N_HEADS].reshape(nb, seq, N_GROUPS, hg).transpose(2, 0, 3, 1)
    y, states = _ssd_fwd(xc, dtr, dtrt, bias_r, bias_c, a_r, a_c, dskip_r, nb, seq)
    yn = _gate_norm_fwd(y, z, wts["ssm_norm"])
    y_ssm = _mm(yn, wts["w_ssm_out"], "nn", BF16, "ssm_out")

    groups = range(len(ATT_DILATIONS))
    qg, kg, vg = _rope_fwd(qkv, cos, sin, nb, seq)
    o_g, lse_g = zip(*[_att_fwd(qg[i], kg[i], vg[i], i, seq) for i in groups])
    att = _merge_fwd(o_g, lse_g, nb, seq)
    y_att = _mm(att, wts["w_att_out_t"], "nt", BF16, "att_out")

    mixed = _mix_fwd(gate_logits, wts["b_gate"], y_ssm, y_att)
    x1 = _mm(mixed, wts["w_mix_out"], "nn", F32, "mix_out", add=x)
    h2 = _rms_fwd(x1, wts["norm_ffn"], "rms_ffn_fwd")
    gt = _mm(h2, wts["w_ffn_gate_t"], "nt", BF16, "ffn_gate")
    up = _mm(h2, wts["w_ffn_up_t"], "nt", BF16, "ffn_up")
    act = _swiglu_fwd(gt, up)
    x2 = _mm(act, wts["w_ffn_down"], "nn", F32, "ffn_down", add=x1)

    g = {}
    dx2, dx2_b, g["norm_final"], loss = _final_fwd_bwd(x2, target, wts["norm_final"].reshape(1, d))
    dact = _mm(dx2_b, wts["w_ffn_down"], "nt", BF16, "d_act")
    g["w_ffn_down"] = _mm(act, dx2_b, "tn", BF16, "g_ffn_down")
    dgt, dup = _swiglu_bwd(gt, up, dact)
    g["w_ffn_gate_t"] = _mm(dgt, h2, "tn", BF16, "g_ffn_gate")
    g["w_ffn_up_t"] = _mm(dup, h2, "tn", BF16, "g_ffn_up")
    dh2 = _mm(dgt, wts["w_ffn_gate_t"], "nn", F32, "d_h2_gate")
    dh2 = _mm(dup, wts["w_ffn_up_t"], "nn", F32, "d_h2_up", add=dh2)
    dx1, dx1_b, g["norm_ffn"] = _rms_bwd(x1, dh2, wts["norm_ffn"], dx2, "rms_ffn_bwd")

    dmixed = _mm(dx1_b, wts["w_mix_out"], "nt", F32, "d_mixed")
    g["w_mix_out"] = _mm(mixed, dx1_b, "tn", BF16, "g_mix_out")
    dy_ssm, dy_att, dgate, g["b_gate"] = _mix_bwd(gate_logits, wts["b_gate"], y_ssm, y_att, dmixed)

    datt = _mm(dy_att, wts["w_att_out_t"], "nn", F32, "d_att")
    g["w_att_out_t"] = _mm(dy_att, att, "tn", BF16, "g_att_out")
    do_g, dlt_g = _merge_bwd(o_g, lse_g, datt, nb, seq)
    dq_g, dk_g, dv_g = zip(*[_att_bwd(qg[i], kg[i], vg[i], do_g[i], lse_g[i], dlt_g[i], i, seq) for i in groups])
    dqkv = _rope_bwd(dq_g, dk_g, dv_g, cos, sin, nb, seq)

    dyn = _mm(dy_ssm, wts["w_ssm_out"], "nt", BF16, "d_yn")
    g["w_ssm_out"] = _mm(yn, dy_ssm, "tn", BF16, "g_ssm_out")
    dy, dz, g["ssm_norm"] = _gate_norm_bwd(y, z, wts["ssm_norm"], dyn)
    dxc, ddtr, g_bias, g_alog, g_dskip = _ssd_bwd(xc, dtr, dtrt, bias_r, bias_c, a_r, a_c, dskip_r, states, dy,
                                                   nb, seq)
    g["dt_bias"] = g_bias.reshape(1, N_HEADS)
    g["a_log"] = g_alog.reshape(1, N_HEADS)
    g["d_skip"] = g_dskip.reshape(1, N_HEADS)
    dpre, g_conv_w, g_conv_b = _conv_bwd_pre(xbc, conv_w, conv_b, dxc, seq)
    g["conv_w"] = _group_major_inv(g_conv_w, 1)
    g["conv_b"] = _group_major_inv(g_conv_b, 1)
    dxbc = _conv_bwd_in(dpre, conv_w, seq)
    ddt = jnp.pad(ddtr.transpose(1, 0, 2).reshape(t, N_HEADS), ((0, 0), (0, DT_PAD - N_HEADS))).astype(BF16)

    dh = _mm(dz, w_z, "nn", F32, "d_h_z")
    dh = _mm(dxbc, w_xbc, "nn", F32, "d_h_xbc", add=dh)
    dh = _mm(ddt, w_dt, "nn", F32, "d_h_dt", add=dh)
    dh = _mm(dqkv, w_qkv, "nn", F32, "d_h_qkv", add=dh)
    dh = _mm(dgate, w_gate, "nn", F32, "d_h_gate", add=dh)
    g["w_in_t"] = jnp.concatenate([
        _mm(dz, h, "tn", BF16, "g_in_z"),
        _group_major_inv(_mm(dxbc, h, "tn", BF16, "g_in_xbc"), 0),
        _mm(ddt, h, "tn", BF16, "g_in_dt")[:N_HEADS],
        _mm(dqkv, h, "tn", BF16, "g_in_qkv"),
        _mm(dgate, h, "tn", BF16, "g_in_gate")], axis=0)
    dx, _, g["norm_mix"] = _rms_bwd(x, dh, wts["norm_mix"], dx1, "rms_mix_bwd")
    return loss[0, 0], dx.reshape(nb, seq, d), g


def kernel(x, norm_mix, w_in, b_gate, conv_w, conv_b, dt_bias, a_log, d_skip, ssm_norm, w_ssm_out, w_att_out, w_mix_out, norm_ffn, w_ffn_gate, w_ffn_up, w_ffn_down, norm_final, loss_target, m_norm_mix, m_w_in, m_b_gate, m_conv_w, m_conv_b, m_dt_bias, m_a_log, m_d_skip, m_ssm_norm, m_w_ssm_out, m_w_att_out, m_w_mix_out, m_norm_ffn, m_w_ffn_gate, m_w_ffn_up, m_w_ffn_down, m_norm_final, v_norm_mix, v_w_in, v_b_gate, v_conv_w, v_conv_b, v_dt_bias, v_a_log, v_d_skip, v_ssm_norm, v_w_ssm_out, v_w_att_out, v_w_mix_out, v_norm_ffn, v_w_ffn_gate, v_w_ffn_up, v_w_ffn_down, v_norm_final):
    names = ("norm_mix", "w_in", "b_gate", "conv_w", "conv_b", "dt_bias", "a_log", "d_skip", "ssm_norm", "w_ssm_out",
             "w_att_out", "w_mix_out", "norm_ffn", "w_ffn_gate", "w_ffn_up", "w_ffn_down", "norm_final")
    w_loc = dict(zip(names, (norm_mix, w_in, b_gate, conv_w, conv_b, dt_bias, a_log, d_skip, ssm_norm, w_ssm_out,
                             w_att_out, w_mix_out, norm_ffn, w_ffn_gate, w_ffn_up, w_ffn_down, norm_final)))
    m_loc = dict(zip(names, (m_norm_mix, m_w_in, m_b_gate, m_conv_w, m_conv_b, m_dt_bias, m_a_log, m_d_skip,
                             m_ssm_norm, m_w_ssm_out, m_w_att_out, m_w_mix_out, m_norm_ffn, m_w_ffn_gate,
                             m_w_ffn_up, m_w_ffn_down, m_norm_final)))
    v_loc = dict(zip(names, (v_norm_mix, v_w_in, v_b_gate, v_conv_w, v_conv_b, v_dt_bias, v_a_log, v_d_skip,
                             v_ssm_norm, v_w_ssm_out, v_w_att_out, v_w_mix_out, v_norm_ffn, v_w_ffn_gate,
                             v_w_ffn_up, v_w_ffn_down, v_norm_final)))
    two_d = lambda a: a.reshape(a.shape[-2:]) if a.ndim >= 2 else a.reshape(1, -1)
    w2 = {n: two_d(a) for n, a in w_loc.items()}
    chip = 2 * lax.axis_index("x") + lax.axis_index("y")
    c = lax.axis_index("c")

    wire_shapes = {n: _wire_shard(w2[n], n).shape for n in SHARDED}
    true_rows = {n: wire_shapes[n][0] * wire_shapes[n][1] // PACK_W for n in SHARDED}
    seg_rows = {n: _rows(wire_shapes[n][0] * wire_shapes[n][1]) for n in SHARDED}
    w_rows = _padded_rows(sum(seg_rows.values()))
    wp = _pack_rows([_wire_shard(w2[n], n).astype(BF16) for n in SHARDED], w_rows).reshape(2, w_rows // 2, PACK_W)
    wg = _gather_weights(wp).reshape(N_CHIPS, w_rows, PACK_W)
    full = {}
    off = 0
    for n in SHARDED:
        rows, cols = wire_shapes[n]
        full[_wire_name(n)] = wg[:, off:off + true_rows[n]].reshape(N_CHIPS * rows, cols)
        off += seg_rows[n]
    for n in SMALL:
        full[n] = w2[n]

    n_conv = w2["conv_w"].shape[1]
    placed = lax.dynamic_update_slice_in_dim(jnp.zeros((CONV_K, N_CHIPS * n_conv), F32), w2["conv_w"], chip * n_conv, 1)
    placed = jnp.where(c == 0, placed, 0.0)
    full["conv_w"] = _allreduce_small(_pack_rows([placed], _rows(int(placed.size))), "gather_conv_w").reshape(
        -1)[:placed.size].reshape(placed.shape)

    loss_sum, grad_x, g_full = _local_step(x, loss_target, full)
    loss = lax.psum(loss_sum, ("x", "y", "c"))

    g_shard = {}
    small_names = SMALL + ("conv_w",)
    small_flat = jnp.concatenate([g_full[n].reshape(-1) for n in small_names])
    small = _allreduce_small(_pack_rows([small_flat], _rows(int(small_flat.size))), "allreduce_small").reshape(-1)
    off = 0
    for n in small_names:
        size = int(g_full[n].size)
        g_shard[n] = small[off:off + size].reshape(g_full[n].shape)
        off += size
    g_shard["conv_w"] = lax.dynamic_slice_in_dim(g_shard["conv_w"], chip * n_conv, n_conv, 1)

    sections = [_pack_rows([g_full[_wire_name(n)].reshape(N_CHIPS, true_rows[n], PACK_W)[k] for n in SHARDED], w_rows)
                for k in range(N_CHIPS)]
    g2 = jnp.stack(sections).reshape(N_CHIPS, 2, w_rows // 2, PACK_W).transpose(1, 0, 2, 3)
    chip_sum = _add_own_half(g2, _swap_halves(g2), c)
    reduced = _join_halves(_sum_chips(_scatter_to_chips(chip_sum))).reshape(w_rows, PACK_W)
    off = 0
    for n in SHARDED:
        wire = reduced[off:off + true_rows[n]].reshape(wire_shapes[n])
        g_shard[n] = wire.T if n in COL_SHARDED else wire
        off += seg_rows[n]

    grads, deltas, new_m, new_v = [], [], [], []
    for n in names:
        shape = w_loc[n].shape
        d_, m_, v_ = _adamw(w2[n], g_shard[n], two_d(m_loc[n]), two_d(v_loc[n]), "adamw_" + n)
        grads.append(g_shard[n].reshape(shape))
        deltas.append(d_.reshape(shape))
        new_m.append(m_.reshape(shape))
        new_v.append(v_.reshape(shape))
    return (loss, grad_x, *grads, *deltas, *new_m, *new_v)
```

```python
from typing import Callable, NamedTuple, Optional

import jax
import jax.numpy as jnp
from jax import lax
from jax.experimental import pallas as pl
from jax.experimental.pallas import tpu as pltpu

F32 = jnp.float32
BF16 = jnp.bfloat16
SDS = jax.ShapeDtypeStruct
MESH = pl.DeviceIdType.MESH

D_MODEL = 1024
D_INNER = 2048
N_HEADS = 32
HEAD_P = 64
N_GROUPS = 4
HEADS_PER_GROUP = N_HEADS // N_GROUPS
D_STATE = 128
CONV_K = 4
CHUNK = 128
CONV_DIM = D_INNER + 2 * N_GROUPS * D_STATE
GROUP_W = D_INNER // N_GROUPS + 2 * D_STATE
ATT_HEADS = 12
ATT_D = 128
ATT_SLOTS = 4
ATT_W = ATT_SLOTS * ATT_D
ATT_DILATIONS = (1, 4, 16)
ATT_BLOCK = 128
QKV_DIM = 3 * ATT_HEADS * ATT_D
D_FF = 2816
DT_PAD = 128
ROPE_THETA = 10000.0
EPS = 1e-6
N_CHIPS = 4
LANES = 128

ADAM_LR = 0.001
ADAM_B1 = 0.9
ADAM_B2 = 0.999
ADAM_EPS = 1e-08
ADAM_WD = 0.01
ADAM_STEP = 10

VMEM_LIMIT = 48 * 1024 * 1024


def _cparams(semantics):
    return pltpu.CompilerParams(dimension_semantics=semantics, vmem_limit_bytes=VMEM_LIMIT)


def _pick(n, cap):
    best = None
    for t in range(LANES, min(n, cap) + 1, LANES):
        if n % t == 0:
            best = t
    return best or n


def _row_tile(rows, cap):
    best = None
    for t in range(8, min(rows, cap) + 1, 8):
        if rows % t == 0:
            best = t
    return best or rows


def _sigmoid(x):
    return 1.0 / (1.0 + jnp.exp(-x))


def _softplus(x):
    return jnp.maximum(x, 0.0) + jnp.log(1.0 + jnp.exp(-jnp.abs(x)))


def _dot(a, b):
    return jnp.dot(a, b, preferred_element_type=F32)


def _dot_nt(a, b):
    return lax.dot_general(a, b, (((1,), (1,)), ((), ())), preferred_element_type=F32)


def _dot_tn(a, b):
    return lax.dot_general(a, b, (((0,), (0,)), ((), ())), preferred_element_type=F32)


def _mm(a, b, mode, out_dtype, name, add=None):
    if mode == "nn":
        (m, k), (_, n) = a.shape, b.shape
    elif mode == "nt":
        (m, k), (n, _) = a.shape, b.shape
    else:
        (k, m), (_, n) = a.shape, b.shape
    tm, tn = _pick(m, 512), _pick(n, 1536)
    tk = k if k <= 2048 else _pick(k, 2048)
    nk = k // tk
    dims = {"nn": ((1,), (0,)), "nt": ((1,), (1,)), "tn": ((0,), (0,))}[mode]

    def partial_product(a_ref, b_ref):
        return lax.dot_general(a_ref[...].astype(BF16), b_ref[...].astype(BF16), (dims, ((), ())),
                               preferred_element_type=F32)

    def body(*refs):
        a_ref, b_ref = refs[:2]
        c_ref = refs[2] if add is not None else None
        o_ref = refs[3] if add is not None else refs[2]

        def finish(r):
            if add is not None:
                r = r + c_ref[...].astype(F32)
            o_ref[...] = r.astype(out_dtype)

        if nk == 1:
            finish(partial_product(a_ref, b_ref))
            return
        acc = refs[-1]
        kk = pl.program_id(2)

        @pl.when(kk == 0)
        def _():
            acc[...] = partial_product(a_ref, b_ref)

        @pl.when((kk > 0) & (kk < nk - 1))
        def _():
            acc[...] += partial_product(a_ref, b_ref)

        @pl.when(kk == nk - 1)
        def _():
            finish(acc[...] + partial_product(a_ref, b_ref))

    a_spec = {"nn": pl.BlockSpec((tm, tk), lambda j, i, q: (i, q)),
              "nt": pl.BlockSpec((tm, tk), lambda j, i, q: (i, q)),
              "tn": pl.BlockSpec((tk, tm), lambda j, i, q: (q, i))}[mode]
    b_spec = {"nn": pl.BlockSpec((tk, tn), lambda j, i, q: (q, j)),
              "nt": pl.BlockSpec((tn, tk), lambda j, i, q: (j, q)),
              "tn": pl.BlockSpec((tk, tn), lambda j, i, q: (q, j))}[mode]
    o_spec = pl.BlockSpec((tm, tn), lambda j, i, q: (i, j))
    ins, specs = [a, b], [a_spec, b_spec]
    if add is not None:
        ins.append(add)
        specs.append(o_spec)
    return pl.pallas_call(
        body, name=name, grid=(n // tn, m // tm, nk), in_specs=specs, out_specs=o_spec,
        out_shape=SDS((m, n), out_dtype), scratch_shapes=[pltpu.VMEM((tm, tn), F32)] if nk > 1 else [],
        compiler_params=_cparams(("parallel", "parallel", "arbitrary")))(*ins)


def _rw(name, fn, nsteps, ins, outs, n_acc=0):
    n_in, n_out = len(ins), len(outs)

    def body(*refs):
        i = pl.program_id(0)
        vals = fn(i, *refs[:n_in])
        for q, (r, v) in enumerate(zip(refs[n_in:], vals)):
            if q < n_out - n_acc:
                r[...] = v.astype(r.dtype)
            else:
                @pl.when(i == 0)
                def _(r=r):
                    r[...] = jnp.zeros_like(r)

                r[...] += v

    return pl.pallas_call(
        body, name=name, grid=(nsteps,), in_specs=[s for _, s in ins], out_specs=[s for _, s in outs],
        out_shape=[o for o, _ in outs], compiler_params=_cparams(("arbitrary",)))(*[a for a, _ in ins])


def _rs(tm, w, cb=0):
    return pl.BlockSpec((tm, w), lambda i: (i, cb))


def _fs(shape):
    nd = len(shape)
    return pl.BlockSpec(shape, lambda i: (0,) * nd)


def _colsum(v):
    return jnp.sum(v, axis=0, keepdims=True)


def _rms_fwd(x, g, name):
    t, d = x.shape
    tm = 512

    def fn(i, x_ref, g_ref):
        xv = x_ref[...]
        r = lax.rsqrt(jnp.mean(xv * xv, axis=-1, keepdims=True) + EPS)
        return [xv * r * g_ref[...]]

    return _rw(name, fn, t // tm, [(x, _rs(tm, d)), (g, _fs((1, d)))], [(SDS((t, d), BF16), _rs(tm, d))])[0]


def _rms_bwd(x, dh, g, dres, name):
    t, d = x.shape
    tm = 512

    def fn(i, x_ref, dh_ref, g_ref, dres_ref):
        xv = x_ref[...]
        r = lax.rsqrt(jnp.mean(xv * xv, axis=-1, keepdims=True) + EPS)
        xhat = xv * r
        dhv = dh_ref[...]
        dxhat = dhv * g_ref[...]
        dx = dres_ref[...] + r * (dxhat - xhat * jnp.mean(dxhat * xhat, axis=-1, keepdims=True))
        return [dx, dx, _colsum(dhv * xhat)]

    return _rw(name, fn, t // tm,
               [(x, _rs(tm, d)), (dh, _rs(tm, d)), (g, _fs((1, d))), (dres, _rs(tm, d))],
               [(SDS((t, d), F32), _rs(tm, d)), (SDS((t, d), BF16), _rs(tm, d)), (SDS((1, d), F32), _fs((1, d)))],
               n_acc=1)


def _final_fwd_bwd(x2, target, g):
    t, d = x2.shape
    tm = 512

    def fn(i, x_ref, t_ref, g_ref):
        xv = x_ref[...]
        gv = g_ref[...]
        r = lax.rsqrt(jnp.mean(xv * xv, axis=-1, keepdims=True) + EPS)
        xhat = xv * r
        diff = xhat * gv - t_ref[...]
        lsum = 0.5 * jnp.sum(jnp.sum(diff * diff, axis=-1, keepdims=True) * (1.0 / d), axis=0, keepdims=True)
        dy = diff * (1.0 / d)
        dxhat = dy * gv
        dx = r * (dxhat - xhat * jnp.mean(dxhat * xhat, axis=-1, keepdims=True))
        return [dx, dx, _colsum(dy * xhat), lsum]

    return _rw("final_norm_loss", fn, t // tm,
               [(x2, _rs(tm, d)), (target, _rs(tm, d)), (g, _fs((1, d)))],
               [(SDS((t, d), F32), _rs(tm, d)), (SDS((t, d), BF16), _rs(tm, d)), (SDS((1, d), F32), _fs((1, d))),
                (SDS((1, 1), F32), _fs((1, 1)))], n_acc=2)


CONV_TS = 512
CONV_HALO = 8


def _conv_specs(seq, c):
    ts, tc = CONV_TS, GROUP_W
    hb = ts // CONV_HALO
    u_spec = pl.BlockSpec((ts, tc), lambda j, i: (i, j))
    prev_spec = pl.BlockSpec((CONV_HALO, tc), lambda j, i: (jnp.maximum(i * hb - 1, 0), j))
    w_spec = pl.BlockSpec((CONV_K, tc), lambda j, i: (0, j))
    b_spec = pl.BlockSpec((1, tc), lambda j, i: (0, j))
    return u_spec, prev_spec, w_spec, b_spec


CONV_ROWS = 16


def _conv_pre(i, seq, u_ref, prev_ref, w_ref, b_ref, ext):
    ts = CONV_TS
    first = (i % (seq // ts)) == 0
    ext[0:CONV_HALO, :] = jnp.where(first, 0.0, prev_ref[...])
    ext[CONV_HALO:, :] = u_ref[...]
    acc = jnp.broadcast_to(b_ref[...], u_ref.shape)
    for q in range(CONV_K):
        acc = acc + w_ref[q:q + 1, :] * ext[pl.ds(CONV_HALO - CONV_K + 1 + q, ts), :]
    return acc


def _conv_fwd(u, w, b, seq):
    t, c = u.shape
    ts, tc = CONV_TS, GROUP_W
    u_spec, prev_spec, w_spec, b_spec = _conv_specs(seq, c)

    def body(u_ref, prev_ref, w_ref, b_ref, o_ref, ext):
        pre = _conv_pre(pl.program_id(1), seq, u_ref, prev_ref, w_ref, b_ref, ext)
        o_ref[...] = pre * _sigmoid(pre)

    return pl.pallas_call(
        body, name="conv_fwd", grid=(c // tc, t // ts), in_specs=[u_spec, prev_spec, w_spec, b_spec],
        out_specs=u_spec, out_shape=SDS((t, c), F32), scratch_shapes=[pltpu.VMEM((ts + CONV_HALO, tc), F32)],
        compiler_params=_cparams(("parallel", "arbitrary")))(u, u, w, b)


def _conv_bwd_pre(u, w, b, dxc, seq):
    t, c = u.shape
    ts, tc = CONV_TS, GROUP_W
    u_spec, prev_spec, w_spec, b_spec = _conv_specs(seq, c)

    def body(u_ref, prev_ref, w_ref, b_ref, d_ref, dpre_ref, dw_ref, db_ref, ext):
        i = pl.program_id(1)
        pre = _conv_pre(i, seq, u_ref, prev_ref, w_ref, b_ref, ext)
        sg = _sigmoid(pre)
        dpre = d_ref[...] * sg * (1.0 + pre * (1.0 - sg))
        dpre_ref[...] = dpre

        @pl.when(i == 0)
        def _():
            dw_ref[...] = jnp.zeros_like(dw_ref)
            db_ref[...] = jnp.zeros_like(db_ref)

        db_ref[...] += _colsum(dpre)
        for q in range(CONV_K):
            dw_ref[q:q + 1, :] += _colsum(dpre * ext[pl.ds(CONV_HALO - CONV_K + 1 + q, ts), :])

    return pl.pallas_call(
        body, name="conv_bwd_pre", grid=(c // tc, t // ts),
        in_specs=[u_spec, prev_spec, w_spec, b_spec, u_spec], out_specs=[u_spec, w_spec, b_spec],
        out_shape=[SDS((t, c), F32), SDS((CONV_K, c), F32), SDS((1, c), F32)],
        scratch_shapes=[pltpu.VMEM((ts + CONV_HALO, tc), F32)],
        compiler_params=_cparams(("parallel", "arbitrary")))(u, u, w, b, dxc)


def _conv_bwd_in(dpre, w, seq):
    t, c = dpre.shape
    ts, tc = CONV_TS, GROUP_W
    hb = ts // CONV_HALO
    last = t // CONV_HALO - 1
    d_spec = pl.BlockSpec((ts, tc), lambda j, i: (i, j))
    next_spec = pl.BlockSpec((CONV_HALO, tc), lambda j, i: (jnp.minimum((i + 1) * hb, last), j))
    w_spec = pl.BlockSpec((CONV_K, tc), lambda j, i: (0, j))

    def body(d_ref, next_ref, w_ref, o_ref, ext):
        i = pl.program_id(1)
        nts = seq // ts
        is_last = (i % nts) == nts - 1
        ext[0:ts, :] = d_ref[...]
        ext[ts:, :] = jnp.where(is_last, 0.0, next_ref[...])
        wv = w_ref[...]

        def rows(j, carry):
            r0 = pl.multiple_of(j * CONV_ROWS, CONV_ROWS)
            blk = ext[pl.ds(r0, CONV_ROWS + CONV_HALO), :]
            acc = wv[CONV_K - 1:CONV_K] * blk[0:CONV_ROWS]
            for q in range(CONV_K - 1):
                acc = acc + wv[q:q + 1] * blk[CONV_K - 1 - q:CONV_K - 1 - q + CONV_ROWS]
            o_ref[pl.ds(r0, CONV_ROWS), :] = acc.astype(o_ref.dtype)
            return carry

        lax.fori_loop(0, ts // CONV_ROWS, rows, 0)

    return pl.pallas_call(
        body, name="conv_bwd_in", grid=(c // tc, t // ts), in_specs=[d_spec, next_spec, w_spec],
        out_specs=d_spec, out_shape=SDS((t, c), BF16), scratch_shapes=[pltpu.VMEM((ts + CONV_HALO, tc), F32)],
        compiler_params=_cparams(("parallel", "arbitrary")))(dpre, dpre, w)


def _split3(v):
    hi = v.astype(BF16)
    r1 = v - hi.astype(F32)
    mid = r1.astype(BF16)
    lo = (r1 - mid.astype(F32)).astype(BF16)
    return hi, mid, lo


def _ssd_prelude(dtr_ref, dtrt_ref, bias_ref, biast_ref, a_ref, at_ref):
    dt = _softplus(dtr_ref[...] + bias_ref[...])
    dtt = _softplus(dtrt_ref[...] + biast_ref[...])
    ri = lax.broadcasted_iota(jnp.int32, (CHUNK, CHUNK), 0)
    ci = lax.broadcasted_iota(jnp.int32, (CHUNK, CHUNK), 1)
    lower = ri >= ci
    upper = ri <= ci
    lower_b = jnp.where(lower, 1.0, 0.0).astype(BF16)
    upper_b = jnp.where(upper, 1.0, 0.0).astype(BF16)
    acs = sum(_dot(lower_b, p) for p in _split3(dt * a_ref[...]))
    acst = sum(_dot(p, upper_b) for p in _split3(dtt * at_ref[...]))
    return dt, acs, acst, lower, upper, lower_b, upper_b


def _ssd_specs(seq):
    nc = seq // CHUNK
    hg = HEADS_PER_GROUP
    row = lambda cc: (lambda g, b, c: (b * nc + cc(c), g))
    fwd = lambda c: c
    rev = lambda c: nc - 1 - c

    def specs(cc):
        return dict(
            xc=pl.BlockSpec((CHUNK, GROUP_W), lambda g, b, c: (b * nc + cc(c), g)),
            y=pl.BlockSpec((CHUNK, D_INNER // N_GROUPS), lambda g, b, c: (b * nc + cc(c), g)),
            dtr=pl.BlockSpec((None, CHUNK, hg), lambda g, b, c: (g, b * nc + cc(c), 0)),
            dtrt=pl.BlockSpec((None, None, hg, CHUNK), lambda g, b, c: (g, b, 0, cc(c))),
            prow=pl.BlockSpec((None, 1, hg), lambda g, b, c: (g, 0, 0)),
            pcol=pl.BlockSpec((None, hg, 1), lambda g, b, c: (g, 0, 0)),
            st=pl.BlockSpec((None, None, None, D_STATE, hg * HEAD_P), lambda g, b, c: (g, b, cc(c), 0, 0)),
        )

    return specs(fwd), specs(rev)


def _head_maps():
    hw = HEADS_PER_GROUP * HEAD_P
    shift = HEAD_P.bit_length() - 1
    hj = lax.broadcasted_iota(jnp.int32, (HEADS_PER_GROUP, hw), 0)
    lq = jnp.right_shift(lax.broadcasted_iota(jnp.int32, (HEADS_PER_GROUP, hw), 1), shift)
    spread = jnp.where(hj == lq, 1.0, 0.0).astype(BF16)
    rq = jnp.right_shift(lax.broadcasted_iota(jnp.int32, (hw, LANES), 0), shift)
    cj = lax.broadcasted_iota(jnp.int32, (hw, LANES), 1)
    gather = jnp.where(rq == cj, 1.0, 0.0).astype(BF16)
    return spread, gather


def _exact_dot(v, m01):
    return sum(_dot(p, m01) for p in _split3(v))


class _Side(NamedTuple):
    ins: tuple
    out_shapes: tuple
    scratch: tuple
    first: Callable
    mid: Optional[Callable]
    last: Callable


def _attach_side(body, n_in, n_out, side, nb, nc):
    si, so, ss = len(side.ins), len(side.out_shapes), len(side.scratch)

    def wrapped(*refs):
        ins, s_in = refs[:n_in], refs[n_in:n_in + si]
        outs = refs[n_in + si:n_in + si + n_out]
        s_out = refs[n_in + si + n_out:n_in + si + n_out + so]
        rest = refs[n_in + si + n_out + so:]
        scr, s_scr = rest[:len(rest) - ss], rest[len(rest) - ss:]
        g, b, c = pl.program_id(0), pl.program_id(1), pl.program_id(2)
        head = (b == 0) & (c == 0)

        @pl.when((g == 0) & head)
        def _():
            side.first(s_in, s_out, s_scr)

        if side.mid is not None:
            @pl.when((g == N_GROUPS - 1) & head)
            def _():
                side.mid(s_in, s_out, s_scr)

        body(*ins, *outs, *scr)

        @pl.when((g == N_GROUPS - 1) & (b == nb - 1) & (c == nc - 1))
        def _():
            side.last(s_in, s_out, s_scr)

    return wrapped


def _ssd_fwd(xc, dtr, dtrt, bias, biast, a, at, dskip, nb, seq, side):
    t = xc.shape[0]
    nc = seq // CHUNK
    hg = HEADS_PER_GROUP
    hw = hg * HEAD_P
    sp, _ = _ssd_specs(seq)

    def body(xc_ref, dtr_ref, dtrt_ref, bias_ref, biast_ref, a_ref, at_ref, d_ref, y_ref, sin_ref, st):
        @pl.when(pl.program_id(2) == 0)
        def _():
            st[...] = jnp.zeros_like(st)

        s_in = st[...]
        sin_ref[...] = s_in
        dt, acs, acst, lower, _, _, _ = _ssd_prelude(dtr_ref, dtrt_ref, bias_ref, biast_ref, a_ref, at_ref)
        spread, _ = _head_maps()
        x = xc_ref[...]
        xs = x[:, :hw]
        b16 = x[:, hw:hw + D_STATE].astype(BF16)
        c16 = x[:, hw + D_STATE:].astype(BF16)
        cb = _dot_nt(c16, b16)
        last = acs[CHUNK - 1:CHUNK, :]
        e_x = _exact_dot(jnp.exp(acs), spread)
        dec_x = _exact_dot(jnp.exp(last - acs), spread)
        tot_x = e_x[CHUNK - 1:CHUNK, :]
        d_x = _exact_dot(jnp.broadcast_to(d_ref[...], (8, hg)), spread)[0:1, :]
        xdtf = xs * _exact_dot(dt, spread)
        xdt16 = xdtf.astype(BF16)
        yoff = e_x * _dot(c16, s_in.astype(BF16))
        st[...] = tot_x * s_in + _dot_tn(b16, (dec_x * xdtf).astype(BF16))
        parts = []
        for j in range(hg):
            decay = jnp.exp(jnp.where(lower, acs[:, j:j + 1] - acst[j:j + 1, :], -jnp.inf))
            parts.append(_dot((cb * decay).astype(BF16), xdt16[:, HEAD_P * j:HEAD_P * (j + 1)]))
        y_ref[...] = jnp.concatenate(parts, axis=-1) + yoff + d_x * xs

    return pl.pallas_call(
        _attach_side(body, 8, 2, side, nb, nc), name="ssd_fwd", grid=(N_GROUPS, nb, nc),
        in_specs=[sp["xc"], sp["dtr"], sp["dtrt"], sp["prow"], sp["pcol"], sp["prow"], sp["pcol"], sp["prow"]]
        + [ANY] * len(side.ins),
        out_specs=[sp["y"], sp["st"]] + [ANY] * len(side.out_shapes),
        out_shape=[SDS((t, D_INNER), F32), SDS((N_GROUPS, nb, nc, D_STATE, hw), F32)] + list(side.out_shapes),
        scratch_shapes=[pltpu.VMEM((D_STATE, hw), F32)] + list(side.scratch),
        compiler_params=_cparams(("arbitrary", "arbitrary", "arbitrary")))(
            xc, dtr, dtrt, bias, biast, a, at, dskip, *side.ins)


def _ssd_bwd(xc, dtr, dtrt, bias, biast, a, at, dskip, states, dy, nb, seq, side):
    t = xc.shape[0]
    nc = seq // CHUNK
    hg = HEADS_PER_GROUP
    hw = hg * HEAD_P
    _, sp = _ssd_specs(seq)

    def body(xc_ref, dtr_ref, dtrt_ref, bias_ref, biast_ref, a_ref, at_ref, d_ref, sin_ref, dy_ref,
             dxc_ref, ddtr_ref, gbias_ref, ga_ref, gd_ref, ds):
        first = (pl.program_id(1) == 0) & (pl.program_id(2) == 0)

        @pl.when(pl.program_id(2) == 0)
        def _():
            ds[...] = jnp.zeros_like(ds)

        @pl.when(first)
        def _():
            gbias_ref[...] = jnp.zeros_like(gbias_ref)
            ga_ref[...] = jnp.zeros_like(ga_ref)
            gd_ref[...] = jnp.zeros_like(gd_ref)

        dt, acs, acst, lower, upper, _, upper_b = _ssd_prelude(dtr_ref, dtrt_ref, bias_ref, biast_ref, a_ref, at_ref)
        spread, gather = _head_maps()
        x = xc_ref[...]
        dy = dy_ref[...]
        xs = x[:, :hw]
        b16 = x[:, hw:hw + D_STATE].astype(BF16)
        c16 = x[:, hw + D_STATE:].astype(BF16)
        dy16 = dy.astype(BF16)
        cb = _dot_nt(c16, b16)
        cbt = _dot_nt(b16, c16)
        last = acs[CHUNK - 1:CHUNK, :]
        e8 = jnp.exp(acs)
        dec8 = jnp.exp(last - acs)
        e_x = _exact_dot(e8, spread)
        dec_x = _exact_dot(dec8, spread)
        tot_x = e_x[CHUNK - 1:CHUNK, :]
        dt_x = _exact_dot(dt, spread)
        d_x = _exact_dot(jnp.broadcast_to(d_ref[...], (8, hg)), spread)[0:1, :]
        xdtf = xs * dt_x
        xdt16 = xdtf.astype(BF16)
        s_in = sin_ref[...]
        s16 = s_in.astype(BF16)
        ds_out = ds[...]
        ds16 = ds_out.astype(BF16)
        bds = _dot(b16, ds16)
        cs = _dot(c16, s16)
        edy16 = (e_x * dy).astype(BF16)
        ds[...] = tot_x * ds_out + _dot_tn(c16, edy16)
        lane8 = lax.broadcasted_iota(jnp.int32, (CHUNK, hg), 1)
        row8 = lax.broadcasted_iota(jnp.int32, (CHUNK, hg), 0)
        dacs8 = jnp.zeros((CHUNK, hg), F32)
        acc_m = jnp.zeros((CHUNK, CHUNK), F32)
        acc_mt = jnp.zeros((CHUNK, CHUNK), F32)
        dx_parts = []
        for j in range(hg):
            sl = slice(HEAD_P * j, HEAD_P * (j + 1))
            col = acs[:, j:j + 1]
            row = acst[j:j + 1, :]
            decay = jnp.exp(jnp.where(lower, col - row, -jnp.inf))
            decayt = jnp.exp(jnp.where(upper, row - col, -jnp.inf))
            wm = _dot_nt(dy16[:, sl], xdt16[:, sl]) * decay
            wmt = _dot_nt(xdt16[:, sl], dy16[:, sl]) * decayt
            acc_m = acc_m + wm
            acc_mt = acc_mt + wmt
            dacs8 = dacs8 + jnp.where(lane8 == j, jnp.sum(wm * cb, axis=-1, keepdims=True)
                                      - jnp.sum(wmt * cbt, axis=-1, keepdims=True), 0.0)
            dx_parts.append(_dot((cbt * decayt).astype(BF16), dy16[:, sl]))
        dx = jnp.concatenate(dx_parts, axis=-1) + dec_x * bds
        dxc_ref[:, :hw] = dx * dt_x + d_x * dy
        dxc_ref[:, hw:hw + D_STATE] = _dot(acc_mt.astype(BF16), c16) + _dot_nt((dec_x * xdtf).astype(BF16), ds16)
        dxc_ref[:, hw + D_STATE:] = _dot(acc_m.astype(BF16), b16) + _dot_nt(edy16, s16)
        dtot_rows = jnp.broadcast_to(_colsum(ds_out * s_in), (8, hw))
        sums = _exact_dot(jnp.concatenate([dy * cs, xdtf * bds, dx * xs, dy * xs, dtot_rows], axis=0), gather)
        de8 = sums[0:CHUNK, :hg]
        ddec8 = sums[CHUNK:2 * CHUNK, :hg]
        ddtx8 = sums[2 * CHUNK:3 * CHUNK, :hg]
        gd8 = _colsum(sums[3 * CHUNK:4 * CHUNK, :hg])
        dtot8 = sums[4 * CHUNK:4 * CHUNK + 1, :hg]
        extra = _colsum(ddec8 * dec8) + dtot8 * e8[CHUNK - 1:CHUNK, :]
        dacs8 = dacs8 + de8 * e8 - ddec8 * dec8 + jnp.where(row8 == CHUNK - 1, extra, 0.0)
        da = sum(_dot(upper_b, p) for p in _split3(dacs8))
        av = a_ref[...]
        ddt = da * av + ddtx8
        ddtr = ddt * _sigmoid(dtr_ref[...] + bias_ref[...])
        ddtr_ref[...] = ddtr
        gbias_ref[...] += _colsum(ddtr)
        ga_ref[...] += _colsum(da * dt) * av
        gd_ref[...] += gd8

    return pl.pallas_call(
        _attach_side(body, 10, 5, side, nb, nc), name="ssd_bwd", grid=(N_GROUPS, nb, nc),
        in_specs=[sp["xc"], sp["dtr"], sp["dtrt"], sp["prow"], sp["pcol"], sp["prow"], sp["pcol"], sp["prow"],
                  sp["st"], sp["y"]] + [ANY] * len(side.ins),
        out_specs=[sp["xc"], sp["dtr"], sp["prow"], sp["prow"], sp["prow"]] + [ANY] * len(side.out_shapes),
        out_shape=[SDS((t, N_GROUPS * GROUP_W), F32), SDS((N_GROUPS, t, hg), F32)]
        + [SDS((N_GROUPS, 1, hg), F32)] * 3 + list(side.out_shapes),
        scratch_shapes=[pltpu.VMEM((D_STATE, hw), F32)] + list(side.scratch),
        compiler_params=_cparams(("arbitrary", "arbitrary", "arbitrary")))(
            xc, dtr, dtrt, bias, biast, a, at, dskip, states, dy, *side.ins)


def _group_bcast(v, width, fn):
    parts = []
    for q in range(v.shape[-1] // width):
        s = fn(v[:, q * width:(q + 1) * width])
        parts.append(jnp.broadcast_to(s, (v.shape[0], width)))
    return jnp.concatenate(parts, axis=-1)


def _gate_norm_fwd(y, z, g):
    t, d = y.shape
    tm = 256
    gw = d // N_GROUPS

    def fn(i, y_ref, z_ref, g_ref):
        zv = z_ref[...].astype(F32)
        u = y_ref[...] * (zv * _sigmoid(zv))
        r = lax.rsqrt(_group_bcast(u * u, gw, lambda p: jnp.mean(p, axis=-1, keepdims=True)) + EPS)
        return [u * r * g_ref[...]]

    return _rw("gate_norm_fwd", fn, t // tm, [(y, _rs(tm, d)), (z, _rs(tm, d)), (g, _fs((1, d)))],
               [(SDS((t, d), BF16), _rs(tm, d))])[0]


def _gate_norm_bwd(y, z, g, dyn):
    t, d = y.shape
    tm = 256
    gw = d // N_GROUPS

    def fn(i, y_ref, z_ref, g_ref, dyn_ref):
        zv = z_ref[...].astype(F32)
        yv = y_ref[...]
        sg = _sigmoid(zv)
        sz = zv * sg
        u = yv * sz
        r = lax.rsqrt(_group_bcast(u * u, gw, lambda p: jnp.mean(p, axis=-1, keepdims=True)) + EPS)
        uhat = u * r
        dv = dyn_ref[...].astype(F32)
        duhat = dv * g_ref[...]
        du = r * (duhat - uhat * _group_bcast(duhat * uhat, gw, lambda p: jnp.mean(p, axis=-1, keepdims=True)))
        dz = du * yv * sg * (1.0 + zv * (1.0 - sg))
        return [du * sz, dz, _colsum(dv * uhat)]

    return _rw("gate_norm_bwd", fn, t // tm,
               [(y, _rs(tm, d)), (z, _rs(tm, d)), (g, _fs((1, d))), (dyn, _rs(tm, d))],
               [(SDS((t, d), F32), _rs(tm, d)), (SDS((t, d), BF16), _rs(tm, d)), (SDS((1, d), F32), _fs((1, d)))],
               n_acc=1)


def _rope_tables(seq):
    half = ATT_D // 2
    inv = ROPE_THETA ** (-jnp.arange(half, dtype=F32) / half)
    ang = jnp.arange(seq, dtype=F32)[:, None] * inv[None, :]
    cos, sin = jnp.cos(ang), jnp.sin(ang)
    return jnp.concatenate([cos, cos], axis=-1), jnp.concatenate([-sin, sin], axis=-1)


ATT_TILE = 512


def _strided_spec(r, mtiles):
    return pl.BlockSpec((None, r, None, ATT_TILE // r, ATT_W), lambda i: (i // mtiles, 0, i % mtiles, 0, 0))


def _strided_shape(nb, r, mtiles, dtype):
    return SDS((nb, r, mtiles, ATT_TILE // r, ATT_W), dtype)


def _to_strided(val, out_ref, lanes, r, sc):
    if r == 1:
        out_ref[0, :, lanes] = val.astype(out_ref.dtype)
        return
    sc[...] = val
    for rr in range(r):
        out_ref[rr, :, lanes] = sc[pl.ds(rr, ATT_TILE // r, stride=r), :].astype(out_ref.dtype)


def _from_strided(in_ref, lanes, r, sc):
    if r == 1:
        return in_ref[0, :, lanes].astype(F32)
    for rr in range(r):
        sc[pl.ds(rr, ATT_TILE // r, stride=r), :] = in_ref[rr, :, lanes].astype(F32)
    return sc[...]


def _rope_fwd(qkv, cos, sin, nb, seq):
    t = qkv.shape[0]
    tm = ATT_TILE
    mtiles = seq // tm
    w = ATT_HEADS * ATT_D
    tab = pl.BlockSpec((tm, ATT_D), lambda i: (i % mtiles, 0))
    ng = len(ATT_DILATIONS)

    def body(q_ref, k_ref, v_ref, cos_ref, sin_ref, *rest):
        outs, sc = rest[:3 * ng], rest[3 * ng]
        c, s = cos_ref[...], sin_ref[...]
        for which, ref in enumerate((q_ref, k_ref, v_ref)):
            for h in range(ATT_HEADS):
                g, slot = divmod(h, ATT_SLOTS)
                p = ref[:, h * ATT_D:(h + 1) * ATT_D].astype(F32)
                if which < 2:
                    p = p * c + pltpu.roll(p, ATT_D // 2, 1) * s
                _to_strided(p, outs[which * ng + g], slice(slot * ATT_D, (slot + 1) * ATT_D), ATT_DILATIONS[g], sc)

    out_specs = [_strided_spec(r, mtiles) for _ in range(3) for r in ATT_DILATIONS]
    out_shape = [_strided_shape(nb, r, mtiles, BF16) for _ in range(3) for r in ATT_DILATIONS]
    outs = pl.pallas_call(
        body, name="rope_fwd", grid=(t // tm,),
        in_specs=[_rs(tm, w, 0), _rs(tm, w, 1), _rs(tm, w, 2), tab, tab], out_specs=out_specs, out_shape=out_shape,
        scratch_shapes=[pltpu.VMEM((tm, ATT_D), F32)], compiler_params=_cparams(("arbitrary",)))(
            qkv, qkv, qkv, cos, sin)
    flat = [o.reshape(t, ATT_W) for o in outs]
    return flat[0:ng], flat[ng:2 * ng], flat[2 * ng:]


def _rope_bwd(dq, dk, dv, cos, sin, nb, seq):
    t = dq[0].shape[0]
    tm = ATT_TILE
    mtiles = seq // tm
    w = ATT_HEADS * ATT_D
    tab = pl.BlockSpec((tm, ATT_D), lambda i: (i % mtiles, 0))
    ng = len(ATT_DILATIONS)

    def body(*refs):
        ins, (cos_ref, sin_ref, o_ref, sc) = refs[:3 * ng], refs[3 * ng:]
        c, s = cos_ref[...], sin_ref[...]
        for which in range(3):
            for h in range(ATT_HEADS):
                g, slot = divmod(h, ATT_SLOTS)
                p = _from_strided(ins[which * ng + g], slice(slot * ATT_D, (slot + 1) * ATT_D), ATT_DILATIONS[g], sc)
                if which < 2:
                    p = p * c - pltpu.roll(p, ATT_D // 2, 1) * s
                o_ref[:, which * w + h * ATT_D:which * w + (h + 1) * ATT_D] = p.astype(o_ref.dtype)

    views = [a.reshape(nb, r, mtiles, tm // r, ATT_W) for grp in (dq, dk, dv) for a, r in zip(grp, ATT_DILATIONS)]
    return pl.pallas_call(
        body, name="rope_bwd", grid=(t // tm,),
        in_specs=[_strided_spec(r, mtiles) for _ in range(3) for r in ATT_DILATIONS] + [tab, tab],
        out_specs=_rs(tm, 3 * w), out_shape=SDS((t, 3 * w), BF16),
        scratch_shapes=[pltpu.VMEM((tm, ATT_D), F32)], compiler_params=_cparams(("arbitrary",)))(*views, cos, sin)


def _att_masks():
    ri = lax.broadcasted_iota(jnp.int32, (ATT_BLOCK, ATT_BLOCK), 0)
    ci = lax.broadcasted_iota(jnp.int32, (ATT_BLOCK, ATT_BLOCK), 1)
    return ci <= ri, ci >= ri


def _att_fwd(q, k, v, g, seq):
    t, w = q.shape
    nblk = t // ATT_BLOCK
    nbs = seq // ATT_DILATIONS[g] // ATT_BLOCK
    scale = ATT_D ** -0.5
    cur = pl.BlockSpec((ATT_BLOCK, w), lambda n: (n, 0))
    prev = pl.BlockSpec((ATT_BLOCK, w), lambda n: (jnp.maximum(n - 1, 0), 0))

    def body(q_ref, kc_ref, kp_ref, vc_ref, vp_ref, o_ref, lse_ref):
        has_prev = (pl.program_id(0) % nbs) != 0
        mcur, mprev = _att_masks()
        mask = jnp.concatenate([mprev & has_prev, mcur], axis=-1)
        for h in range(ATT_SLOTS):
            sl = slice(h * ATT_D, (h + 1) * ATT_D)
            keys = jnp.concatenate([kp_ref[:, sl], kc_ref[:, sl]], axis=0)
            vals = jnp.concatenate([vp_ref[:, sl], vc_ref[:, sl]], axis=0)
            s = jnp.where(mask, _dot_nt(q_ref[:, sl], keys) * scale, -jnp.inf)
            m = jnp.max(s, axis=-1, keepdims=True)
            p = jnp.exp(s - m)
            den = jnp.sum(p, axis=-1, keepdims=True)
            o_ref[:, sl] = _dot(p.astype(BF16), vals) / den
            lse_ref[:, sl] = jnp.broadcast_to(m + jnp.log(den), (ATT_BLOCK, ATT_D))

    return pl.pallas_call(
        body, name=f"att_fwd_{g}", grid=(nblk,), in_specs=[cur, cur, prev, cur, prev], out_specs=[cur, cur],
        out_shape=[SDS((t, w), F32), SDS((t, w), F32)],
        compiler_params=_cparams(("arbitrary",)))(q, k, k, v, v)


def _att_bwd(q, k, v, do, lse, dlt, g, seq):
    t, w = q.shape
    nblk = t // ATT_BLOCK
    nbs = seq // ATT_DILATIONS[g] // ATT_BLOCK
    scale = ATT_D ** -0.5
    cur = pl.BlockSpec((ATT_BLOCK, w), lambda n: (n, 0))
    nxt = pl.BlockSpec((ATT_BLOCK, w), lambda n: (jnp.minimum(n + 1, nblk - 1), 0))

    def body(qc_ref, qn_ref, k_ref, v_ref, doc_ref, don_ref, lsec_ref, lsen_ref, dltc_ref, dltn_ref,
             dq_ref, dk_ref, dv_ref, carry):
        n = pl.program_id(0)

        @pl.when((n % nbs) == 0)
        def _():
            carry[...] = jnp.zeros_like(carry)

        has_next = (((n + 1) % nbs) != 0) & (n + 1 < nblk)
        mcur, mprev = _att_masks()
        mask = jnp.concatenate([mcur, mprev & has_next], axis=0)
        for h in range(ATT_SLOTS):
            sl = slice(h * ATT_D, (h + 1) * ATT_D)
            kh, vh = k_ref[:, sl], v_ref[:, sl]
            qs = jnp.concatenate([qc_ref[:, sl], qn_ref[:, sl]], axis=0)
            dos = jnp.concatenate([doc_ref[:, sl], don_ref[:, sl]], axis=0)
            lse = jnp.concatenate([lsec_ref[:, sl], lsen_ref[:, sl]], axis=0)
            dlt = jnp.concatenate([dltc_ref[:, sl], dltn_ref[:, sl]], axis=0)
            p = jnp.where(mask, jnp.exp(_dot_nt(qs, kh) * scale - lse), 0.0)
            ds = (p * (_dot_nt(dos, vh) - dlt) * scale).astype(BF16)
            dqs = _dot(ds, kh)
            dq_ref[:, sl] = (carry[:, sl] + dqs[:ATT_BLOCK]).astype(dq_ref.dtype)
            carry[:, sl] = dqs[ATT_BLOCK:]
            dk_ref[:, sl] = _dot_tn(ds, qs).astype(dk_ref.dtype)
            dv_ref[:, sl] = _dot_tn(p.astype(BF16), dos).astype(dv_ref.dtype)

    return pl.pallas_call(
        body, name=f"att_bwd_{g}", grid=(nblk,), in_specs=[cur, nxt, cur, cur, cur, nxt, cur, nxt, cur, nxt],
        out_specs=[cur, cur, cur], out_shape=[SDS((t, w), BF16)] * 3,
        scratch_shapes=[pltpu.VMEM((ATT_BLOCK, w), F32)],
        compiler_params=_cparams(("arbitrary",)))(q, q, k, v, do, do, lse, lse, dlt, dlt)


def _merge_weights(ls):
    m = jnp.maximum(jnp.maximum(ls[0], ls[1]), ls[2])
    es = [jnp.exp(v - m) for v in ls]
    den = es[0] + es[1] + es[2]
    return [e / den for e in es]


def _merge_fwd(o, lse, nb, seq):
    t = o[0].shape[0]
    tm = ATT_TILE
    mtiles = seq // tm
    ng = len(ATT_DILATIONS)

    def body(*refs):
        o_refs, l_refs, out_ref, scs = refs[:ng], refs[ng:2 * ng], refs[2 * ng], refs[2 * ng + 1:]
        for slot in range(ATT_SLOTS):
            lanes = slice(slot * ATT_D, (slot + 1) * ATT_D)
            ov = [_from_strided(o_refs[g], lanes, r, scs[2 * g]) for g, r in enumerate(ATT_DILATIONS)]
            ws = _merge_weights([_from_strided(l_refs[g], lanes, r, scs[2 * g + 1])
                                 for g, r in enumerate(ATT_DILATIONS)])
            out_ref[:, lanes] = (ws[0] * ov[0] + ws[1] * ov[1] + ws[2] * ov[2]).astype(out_ref.dtype)

    views = [a.reshape(nb, r, mtiles, tm // r, ATT_W) for grp in (o, lse) for a, r in zip(grp, ATT_DILATIONS)]
    return pl.pallas_call(
        body, name="att_merge_fwd", grid=(t // tm,),
        in_specs=[_strided_spec(r, mtiles) for _ in range(2) for r in ATT_DILATIONS],
        out_specs=_rs(tm, ATT_W), out_shape=SDS((t, ATT_W), BF16),
        scratch_shapes=[pltpu.VMEM((tm, ATT_D), F32)] * (2 * ng), compiler_params=_cparams(("arbitrary",)))(*views)


def _merge_bwd(o, lse, datt, nb, seq):
    t = o[0].shape[0]
    tm = ATT_TILE
    mtiles = seq // tm
    ng = len(ATT_DILATIONS)

    def body(*refs):
        o_refs, l_refs, d_ref = refs[:ng], refs[ng:2 * ng], refs[2 * ng]
        do_refs, dlt_refs = refs[2 * ng + 1:3 * ng + 1], refs[3 * ng + 1:4 * ng + 1]
        scs = refs[4 * ng + 1:]
        for slot in range(ATT_SLOTS):
            lanes = slice(slot * ATT_D, (slot + 1) * ATT_D)
            ov = [_from_strided(o_refs[g], lanes, r, scs[2 * g]) for g, r in enumerate(ATT_DILATIONS)]
            ws = _merge_weights([_from_strided(l_refs[g], lanes, r, scs[2 * g + 1])
                                 for g, r in enumerate(ATT_DILATIONS)])
            dv = d_ref[:, lanes]
            att = ws[0] * ov[0] + ws[1] * ov[1] + ws[2] * ov[2]
            dot = jnp.broadcast_to(jnp.sum(dv * att, axis=-1, keepdims=True), (tm, ATT_D))
            for g, r in enumerate(ATT_DILATIONS):
                _to_strided(ws[g] * dv, do_refs[g], lanes, r, scs[2 * ng])
                _to_strided(ws[g] * dot, dlt_refs[g], lanes, r, scs[2 * ng + 1])

    views = [a.reshape(nb, r, mtiles, tm // r, ATT_W) for grp in (o, lse) for a, r in zip(grp, ATT_DILATIONS)]
    outs = pl.pallas_call(
        body, name="att_merge_bwd", grid=(t // tm,),
        in_specs=[_strided_spec(r, mtiles) for _ in range(2) for r in ATT_DILATIONS] + [_rs(tm, ATT_W)],
        out_specs=[_strided_spec(r, mtiles) for _ in range(2) for r in ATT_DILATIONS],
        out_shape=[_strided_shape(nb, r, mtiles, dt) for dt in (BF16, F32) for r in ATT_DILATIONS],
        scratch_shapes=[pltpu.VMEM((tm, ATT_D), F32)] * (2 * ng + 2), compiler_params=_cparams(("arbitrary",)))(
            *views, datt)
    flat = [a.reshape(t, ATT_W) for a in outs]
    return flat[:ng], flat[ng:]


def _mix_fwd(gate_logits, b_gate, y_ssm, y_att):
    t, d = y_ssm.shape
    tm = 512

    def fn(i, g0_ref, g1_ref, b0_ref, b1_ref, ys_ref, ya_ref):
        g0 = _sigmoid(g0_ref[...].astype(F32) + b0_ref[...])
        g1 = _sigmoid(g1_ref[...].astype(F32) + b1_ref[...])
        return [g0 * ys_ref[...].astype(F32) + g1 * ya_ref[...].astype(F32)]

    b_spec = lambda cb: pl.BlockSpec((1, d), lambda i: (0, cb))
    return _rw("mix_fwd", fn, t // tm,
               [(gate_logits, _rs(tm, d, 0)), (gate_logits, _rs(tm, d, 1)), (b_gate, b_spec(0)), (b_gate, b_spec(1)),
                (y_ssm, _rs(tm, d)), (y_att, _rs(tm, d))],
               [(SDS((t, d), BF16), _rs(tm, d))])[0]


def _mix_bwd(gate_logits, b_gate, y_ssm, y_att, dmixed):
    t, d = y_ssm.shape
    tm = 256

    def fn(i, g0_ref, g1_ref, b0_ref, b1_ref, ys_ref, ya_ref, dm_ref):
        g0 = _sigmoid(g0_ref[...].astype(F32) + b0_ref[...])
        g1 = _sigmoid(g1_ref[...].astype(F32) + b1_ref[...])
        dm = dm_ref[...]
        dg = jnp.concatenate([dm * ys_ref[...].astype(F32) * g0 * (1.0 - g0),
                              dm * ya_ref[...].astype(F32) * g1 * (1.0 - g1)], axis=-1)
        return [dm * g0, dm * g1, dg, _colsum(dg)]

    b_spec = lambda cb: pl.BlockSpec((1, d), lambda i: (0, cb))
    return _rw("mix_bwd", fn, t // tm,
               [(gate_logits, _rs(tm, d, 0)), (gate_logits, _rs(tm, d, 1)), (b_gate, b_spec(0)), (b_gate, b_spec(1)),
                (y_ssm, _rs(tm, d)), (y_att, _rs(tm, d)), (dmixed, _rs(tm, d))],
               [(SDS((t, d), BF16), _rs(tm, d)), (SDS((t, d), BF16), _rs(tm, d)),
                (SDS((t, 2 * d), BF16), _rs(tm, 2 * d)), (SDS((1, 2 * d), F32), _fs((1, 2 * d)))], n_acc=1)


def _swiglu_fwd(gt, up):
    t, f = gt.shape
    tm = 256

    def fn(i, g_ref, u_ref):
        gv = g_ref[...].astype(F32)
        return [gv * _sigmoid(gv) * u_ref[...].astype(F32)]

    return _rw("swiglu_fwd", fn, t // tm, [(gt, _rs(tm, f)), (up, _rs(tm, f))], [(SDS((t, f), BF16), _rs(tm, f))])[0]


def _swiglu_bwd(gt, up, dact):
    t, f = gt.shape
    tm = 256

    def fn(i, g_ref, u_ref, d_ref):
        gv, dv = g_ref[...].astype(F32), d_ref[...].astype(F32)
        sg = _sigmoid(gv)
        return [dv * u_ref[...].astype(F32) * sg * (1.0 + gv * (1.0 - sg)), dv * gv * sg]

    return _rw("swiglu_bwd", fn, t // tm, [(gt, _rs(tm, f)), (up, _rs(tm, f)), (dact, _rs(tm, f))],
               [(SDS((t, f), BF16), _rs(tm, f))] * 2)


def _adamw(w, g, m, v, name):
    r, c = w.shape
    tr = _row_tile(r, max(8, 400_000 // c))
    c1 = 1.0 / (1.0 - ADAM_B1 ** ADAM_STEP)
    c2 = 1.0 / (1.0 - ADAM_B2 ** ADAM_STEP)

    def fn(i, w_ref, g_ref, m_ref, v_ref):
        gv = g_ref[...]
        mn = ADAM_B1 * m_ref[...] + (1.0 - ADAM_B1) * gv
        vn = ADAM_B2 * v_ref[...] + (1.0 - ADAM_B2) * (gv * gv)
        delta = -ADAM_LR * ((mn * c1) / (jnp.sqrt(vn * c2) + ADAM_EPS) + ADAM_WD * w_ref[...])
        return [delta, mn, vn]

    spec = pl.BlockSpec((tr, c), lambda i: (i, 0))
    return _rw(name, fn, r // tr, [(w, spec), (g, spec), (m, spec), (v, spec)], [(SDS((r, c), F32), spec)] * 3)


ANY = pl.BlockSpec(memory_space=pl.ANY)


def _place():
    x, y, c = lax.axis_index("x"), lax.axis_index("y"), lax.axis_index("c")
    chips = [(1 - x, y), (x, 1 - y), (1 - x, 1 - y)]
    return x, y, c, chips


def _remote(src, dst, ssem, rsem, to):
    return pltpu.make_async_remote_copy(src_ref=src, dst_ref=dst, send_sem=ssem, recv_sem=rsem, device_id=to,
                                        device_id_type=MESH)


def _copy_through_vmem(src, dst, buf, isem, osem):
    chunk = buf.shape[1]
    n = src.shape[0] // chunk
    load = lambda k: pltpu.make_async_copy(src.at[pl.ds(k * chunk, chunk)], buf.at[k % 2], isem.at[k % 2])
    store = lambda k: pltpu.make_async_copy(buf.at[k % 2], dst.at[pl.ds(k * chunk, chunk)], osem.at[k % 2])
    load(0).start()
    for k in range(n):
        load(k).wait()
        if k + 1 < n:
            if k >= 1:
                store(k - 1).wait()
            load(k + 1).start()
        store(k).start()
    if n >= 2:
        store(n - 2).wait()
    store(n - 1).wait()


def _copy_scratch(rows, width, dtype):
    chunk = _row_tile(rows, 512)
    return [pltpu.VMEM((2, chunk, width), dtype), pltpu.SemaphoreType.DMA((2,)), pltpu.SemaphoreType.DMA((2,))]


def _gather_weights(wp):
    def body(w_ref, out_ref, ssem, rsem, buf, isem, osem):
        x, y, c, chips = _place()
        me = 2 * x + y
        sib = (x, y, 1 - c)
        first = [_remote(w_ref.at[c], out_ref.at[me, c], ssem.at[j], rsem.at[j], (*chip, c))
                 for j, chip in enumerate(chips)]
        for cp in first:
            cp.start()
        for half in range(2):
            _copy_through_vmem(w_ref.at[half], out_ref.at[me, half], buf, isem, osem)
        passed = []
        for j, chip in enumerate(chips):
            ci = 2 * chip[0] + chip[1]
            _remote(w_ref.at[c], out_ref.at[ci, c], ssem.at[j], rsem.at[j], (*chip, c)).wait_recv()
            cp = _remote(out_ref.at[ci, c], out_ref.at[ci, c], ssem.at[3 + j], rsem.at[3 + j], sib)
            cp.start()
            passed.append(cp)
        for j, chip in enumerate(chips):
            ci = 2 * chip[0] + chip[1]
            _remote(out_ref.at[ci, 1 - c], out_ref.at[ci, 1 - c], ssem.at[3 + j], rsem.at[3 + j], sib).wait_recv()
        for cp in first + passed:
            cp.wait_send()

    return pl.pallas_call(
        body, name="gather_weights", in_specs=[ANY], out_specs=ANY,
        out_shape=SDS((N_CHIPS,) + wp.shape, wp.dtype),
        scratch_shapes=[pltpu.SemaphoreType.DMA((6,)), pltpu.SemaphoreType.DMA((6,))]
        + _copy_scratch(wp.shape[1], wp.shape[2], wp.dtype),
        compiler_params=pltpu.CompilerParams(has_side_effects=True))(wp)


def _swap_halves(g2, tag):
    def body(g_ref, out_ref, ssem, rsem):
        x, y, c, _ = _place()
        cp = _remote(g_ref.at[1 - c], out_ref, ssem, rsem, (x, y, 1 - c))
        cp.start()
        cp.wait()

    return pl.pallas_call(
        body, name="swap_halves_" + tag, in_specs=[ANY], out_specs=ANY, out_shape=SDS(g2.shape[1:], g2.dtype),
        scratch_shapes=[pltpu.SemaphoreType.DMA(()), pltpu.SemaphoreType.DMA(())],
        compiler_params=pltpu.CompilerParams(has_side_effects=True))(g2)


def _add_own_half(g2, other, c, tag):
    _, nch, rows, w = g2.shape
    tr = _row_tile(rows, 512)
    nr = rows // tr

    def body(c_ref, a_ref, b_ref, o_ref):
        o_ref[...] = (a_ref[...].astype(F32) + b_ref[...].astype(F32)).astype(o_ref.dtype)

    grid_spec = pltpu.PrefetchScalarGridSpec(
        num_scalar_prefetch=1, grid=(nch, nr),
        in_specs=[pl.BlockSpec((None, None, tr, w), lambda k, i, c_ref: (c_ref[0], k, i, 0)),
                  pl.BlockSpec((None, tr, w), lambda k, i, c_ref: (k, i, 0))],
        out_specs=pl.BlockSpec((None, tr, w), lambda k, i, c_ref: (k, i, 0)))
    return pl.pallas_call(
        body, name="add_own_half_" + tag, grid_spec=grid_spec, out_shape=SDS(other.shape, other.dtype),
        compiler_params=_cparams(("arbitrary", "arbitrary")))(jnp.reshape(c, (1,)).astype(jnp.int32), g2, other)


def _scatter_to_chips(p):
    def body(p_ref, q_ref, ssem, rsem, buf, isem, osem):
        x, y, c, chips = _place()
        me = 2 * x + y
        sent = []
        for j, chip in enumerate(chips):
            ci = 2 * chip[0] + chip[1]
            cp = _remote(p_ref.at[ci], q_ref.at[me], ssem.at[j], rsem.at[j], (*chip, c))
            cp.start()
            sent.append(cp)
        _copy_through_vmem(p_ref.at[me], q_ref.at[me], buf, isem, osem)
        for j, chip in enumerate(chips):
            ci = 2 * chip[0] + chip[1]
            _remote(p_ref.at[ci], q_ref.at[ci], ssem.at[j], rsem.at[j], (*chip, c)).wait_recv()
        for cp in sent:
            cp.wait_send()

    return pl.pallas_call(
        body, name="scatter_to_chips", in_specs=[ANY], out_specs=ANY, out_shape=SDS(p.shape, p.dtype),
        scratch_shapes=[pltpu.SemaphoreType.DMA((3,)), pltpu.SemaphoreType.DMA((3,))]
        + _copy_scratch(p.shape[1], p.shape[2], p.dtype),
        compiler_params=pltpu.CompilerParams(has_side_effects=True))(p)


def _sum_chips(q, tag):
    nch, rows, w = q.shape
    tr = _row_tile(rows, 512)

    def fn(i, q_ref):
        return [((q_ref[0].astype(F32) + q_ref[1].astype(F32)) + q_ref[2].astype(F32)) + q_ref[3].astype(F32)]

    return _rw("sum_chips_" + tag, fn, rows // tr, [(q, pl.BlockSpec((nch, tr, w), lambda i: (0, i, 0)))],
               [(SDS((rows, w), F32), _rs(tr, w))])[0]


def _chip_copies(src_ref, dst_ref, ssem, rsem, outgoing):
    x, y, c, chips = _place()
    me = 2 * x + y
    cps = []
    for j, chip in enumerate(chips):
        ci = 2 * chip[0] + chip[1]
        cps.append(_remote(src_ref.at[ci], dst_ref.at[me if outgoing else ci], ssem.at[j], rsem.at[j], (*chip, c)))
    return cps, me


def _scatter_side(p):
    def first(ins, outs, scr):
        cps, me = _chip_copies(ins[0], outs[0], scr[0], scr[1], True)
        for cp in cps:
            cp.start()
        pltpu.make_async_copy(ins[0].at[me], outs[0].at[me], scr[2]).start()

    def last(ins, outs, scr):
        for cp in _chip_copies(ins[0], outs[0], scr[0], scr[1], False)[0]:
            cp.wait_recv()
        cps, me = _chip_copies(ins[0], outs[0], scr[0], scr[1], True)
        for cp in cps:
            cp.wait_send()
        pltpu.make_async_copy(ins[0].at[me], outs[0].at[me], scr[2]).wait()

    return _Side((p,), (SDS(p.shape, p.dtype),),
                 (pltpu.SemaphoreType.DMA((3,)), pltpu.SemaphoreType.DMA((3,)), pltpu.SemaphoreType.DMA(())),
                 first, None, last)


def _gather_copies(w_ref, out_ref, ssem, rsem):
    x, y, c, chips = _place()
    me = 2 * x + y
    sib = (x, y, 1 - c)
    sends, arrivals, forwards, from_sib = [], [], [], []
    for j, chip in enumerate(chips):
        ci = 2 * chip[0] + chip[1]
        sends.append(_remote(w_ref.at[c], out_ref.at[me, c], ssem.at[j], rsem.at[j], (*chip, c)))
        arrivals.append(_remote(w_ref.at[c], out_ref.at[ci, c], ssem.at[j], rsem.at[j], (*chip, c)))
        forwards.append(_remote(out_ref.at[ci, c], out_ref.at[ci, c], ssem.at[3 + j], rsem.at[3 + j], sib))
        from_sib.append(_remote(out_ref.at[ci, 1 - c], out_ref.at[ci, 1 - c], ssem.at[3 + j], rsem.at[3 + j], sib))
    return sends, arrivals, forwards, from_sib, me


def _gather_side(wp):
    def first(ins, outs, scr):
        sends, _, _, _, me = _gather_copies(ins[0], outs[0], scr[0], scr[1])
        for cp in sends:
            cp.start()
        pltpu.make_async_copy(ins[0], outs[0].at[me], scr[2]).start()

    def mid(ins, outs, scr):
        _, arrivals, forwards, _, _ = _gather_copies(ins[0], outs[0], scr[0], scr[1])
        for arrived, forward in zip(arrivals, forwards):
            arrived.wait_recv()
            forward.start()

    def last(ins, outs, scr):
        sends, _, forwards, from_sib, me = _gather_copies(ins[0], outs[0], scr[0], scr[1])
        for cp in from_sib:
            cp.wait_recv()
        for cp in sends + forwards:
            cp.wait_send()
        pltpu.make_async_copy(ins[0], outs[0].at[me], scr[2]).wait()

    return _Side((wp,), (SDS((N_CHIPS,) + wp.shape, wp.dtype),),
                 (pltpu.SemaphoreType.DMA((6,)), pltpu.SemaphoreType.DMA((6,)), pltpu.SemaphoreType.DMA(())),
                 first, mid, last)


def _allreduce_small(v, name):
    rows, w = v.shape
    offsets = [(dx, dy, dc) for dx in (0, 1) for dy in (0, 1) for dc in (0, 1)][1:]

    def body(v_ref, o_ref, buf, ssem, rsem):
        x, y, c, _ = _place()
        flip = lambda p, d: 1 - p if d else p
        peers = [(flip(x, dx), flip(y, dy), flip(c, dc)) for dx, dy, dc in offsets]
        index = lambda p: 4 * p[0] + 2 * p[1] + p[2]
        me = index((x, y, c))
        buf[me] = v_ref[...]
        sent = [_remote(v_ref, buf.at[me], ssem.at[q], rsem.at[q], p) for q, p in enumerate(peers)]
        for cp in sent:
            cp.start()
        for q, p in enumerate(peers):
            _remote(v_ref, buf.at[index(p)], ssem.at[q], rsem.at[q], p).wait_recv()
        for cp in sent:
            cp.wait_send()
        acc = buf[0]
        for q in range(1, 8):
            acc = acc + buf[q]
        o_ref[...] = acc

    vm = pl.BlockSpec(memory_space=pltpu.VMEM)
    return pl.pallas_call(
        body, name=name, in_specs=[vm], out_specs=vm, out_shape=SDS((rows, w), F32),
        scratch_shapes=[pltpu.VMEM((8, rows, w), F32), pltpu.SemaphoreType.DMA((7,)), pltpu.SemaphoreType.DMA((7,))],
        compiler_params=pltpu.CompilerParams(has_side_effects=True))(v)


def _join_halves(h, tag):
    def body(h_ref, out_ref, ssem, rsem, buf, isem, osem):
        x, y, c, _ = _place()
        cp = _remote(h_ref, out_ref.at[c], ssem, rsem, (x, y, 1 - c))
        cp.start()
        _copy_through_vmem(h_ref, out_ref.at[c], buf, isem, osem)
        _remote(h_ref, out_ref.at[1 - c], ssem, rsem, (x, y, 1 - c)).wait_recv()
        cp.wait_send()

    return pl.pallas_call(
        body, name="join_halves_" + tag, in_specs=[ANY], out_specs=ANY, out_shape=SDS((2,) + h.shape, h.dtype),
        scratch_shapes=[pltpu.SemaphoreType.DMA(()), pltpu.SemaphoreType.DMA(())]
        + _copy_scratch(h.shape[0], h.shape[1], h.dtype),
        compiler_params=pltpu.CompilerParams(has_side_effects=True))(h)


PACK_W = 1024
SHARDED = ("w_in", "w_ffn_gate", "w_ffn_up", "w_ssm_out", "w_att_out", "w_mix_out", "w_ffn_down")
COL_SHARDED = ("w_in", "w_ffn_gate", "w_ffn_up", "w_att_out")
SMALL = ("norm_mix", "b_gate", "conv_b", "dt_bias", "a_log", "d_skip", "ssm_norm", "norm_ffn", "norm_final")


PACK_ROW_ALIGN = 16


def _rows(n):
    return -(-n // (PACK_W * PACK_ROW_ALIGN)) * PACK_ROW_ALIGN


def _pack_rows(parts, total_rows):
    rows = []
    for p in parts:
        flat = p.reshape(-1)
        pad = _rows(flat.shape[0]) * PACK_W - flat.shape[0]
        if pad:
            flat = jnp.concatenate([flat, jnp.zeros((pad,), flat.dtype)])
        rows.append(flat.reshape(-1, PACK_W))
    used = sum(r.shape[0] for r in rows)
    if total_rows > used:
        rows.append(jnp.zeros((total_rows - used, PACK_W), rows[0].dtype))
    return jnp.concatenate(rows, axis=0)


def _padded_rows(n):
    return -(-n // 32) * 32


def _wire_name(name):
    return name + "_t" if name in COL_SHARDED else name


def _wire_shard(w, name):
    return w.T if name in COL_SHARDED else w


def _group_major(a, axis):
    gw = D_INNER // N_GROUPS
    take = lambda lo, n: lax.slice_in_dim(a, lo, lo + n, axis=axis)
    parts = []
    for g in range(N_GROUPS):
        parts += [take(g * gw, gw), take(D_INNER + g * D_STATE, D_STATE),
                  take(D_INNER + N_GROUPS * D_STATE + g * D_STATE, D_STATE)]
    return jnp.concatenate(parts, axis=axis)


def _group_major_inv(a, axis):
    gw = D_INNER // N_GROUPS
    take = lambda lo, n: lax.slice_in_dim(a, lo, lo + n, axis=axis)
    xs = [take(g * GROUP_W, gw) for g in range(N_GROUPS)]
    bs = [take(g * GROUP_W + gw, D_STATE) for g in range(N_GROUPS)]
    cs = [take(g * GROUP_W + gw + D_STATE, D_STATE) for g in range(N_GROUPS)]
    return jnp.concatenate(xs + bs + cs, axis=axis)


LATE = ("w_ffn_gate_t", "w_ffn_up_t", "w_ssm_out", "w_att_out_t", "w_mix_out", "w_ffn_down")


class _Overlap(NamedTuple):
    gather_side: _Side
    late_weights: Callable
    scatter_side: Callable


def _local_step(x, target, wts, overlap):
    nb, seq, d = x.shape
    t = nb * seq
    x = x.reshape(t, d)
    target = target.reshape(t, d)
    hg = HEADS_PER_GROUP

    w_in_t = wts["w_in_t"]
    o1, o2, o3, o4 = D_INNER, D_INNER + CONV_DIM, D_INNER + CONV_DIM + N_HEADS, D_INNER + CONV_DIM + N_HEADS + QKV_DIM
    w_z = w_in_t[:o1]
    w_xbc = _group_major(w_in_t[o1:o2], 0)
    w_dt = jnp.pad(w_in_t[o2:o3], ((0, DT_PAD - N_HEADS), (0, 0)))
    w_qkv = w_in_t[o3:o4]
    w_gate = w_in_t[o4:]
    conv_w = _group_major(wts["conv_w"], 1)
    conv_b = _group_major(wts["conv_b"], 1)

    def per_group_row(p):
        return p.reshape(N_GROUPS, 1, hg)

    def per_group_col(p):
        return p.reshape(N_GROUPS, hg, 1)

    a_neg = -jnp.exp(wts["a_log"])
    bias_r, bias_c = per_group_row(wts["dt_bias"]), per_group_col(wts["dt_bias"])
    a_r, a_c = per_group_row(a_neg), per_group_col(a_neg)
    dskip_r = per_group_row(wts["d_skip"])
    cos, sin = _rope_tables(seq)

    h = _rms_fwd(x, wts["norm_mix"], "rms_mix_fwd")
    z = _mm(h, w_z, "nt", BF16, "proj_z")
    xbc = _mm(h, w_xbc, "nt", F32, "proj_xbc")
    dt_raw = _mm(h, w_dt, "nt", F32, "proj_dt")
    qkv = _mm(h, w_qkv, "nt", BF16, "proj_qkv")
    gate_logits = _mm(h, w_gate, "nt", BF16, "proj_gate")

    xc = _conv_fwd(xbc, conv_w, conv_b, seq)
    dtr = dt_raw[:, :N_HEADS].reshape(t, N_GROUPS, hg).transpose(1, 0, 2)
    dtrt = dt_raw[:, :N_HEADS].reshape(nb, seq, N_GROUPS, hg).transpose(2, 0, 3, 1)
    y, states, *gathered = _ssd_fwd(xc, dtr, dtrt, bias_r, bias_c, a_r, a_c, dskip_r, nb, seq, overlap.gather_side)
    wts = {**wts, **overlap.late_weights(gathered)}
    yn = _gate_norm_fwd(y, z, wts["ssm_norm"])
    y_ssm = _mm(yn, wts["w_ssm_out"], "nn", BF16, "ssm_out")

    groups = range(len(ATT_DILATIONS))
    qg, kg, vg = _rope_fwd(qkv, cos, sin, nb, seq)
    o_g, lse_g = zip(*[_att_fwd(qg[i], kg[i], vg[i], i, seq) for i in groups])
    att = _merge_fwd(o_g, lse_g, nb, seq)
    y_att = _mm(att, wts["w_att_out_t"], "nt", BF16, "att_out")

    mixed = _mix_fwd(gate_logits, wts["b_gate"], y_ssm, y_att)
    x1 = _mm(mixed, wts["w_mix_out"], "nn", F32, "mix_out", add=x)
    h2 = _rms_fwd(x1, wts["norm_ffn"], "rms_ffn_fwd")
    gt = _mm(h2, wts["w_ffn_gate_t"], "nt", BF16, "ffn_gate")
    up = _mm(h2, wts["w_ffn_up_t"], "nt", BF16, "ffn_up")
    act = _swiglu_fwd(gt, up)
    x2 = _mm(act, wts["w_ffn_down"], "nn", F32, "ffn_down", add=x1)

    g = {}
    dx2, dx2_b, g["norm_final"], loss = _final_fwd_bwd(x2, target, wts["norm_final"].reshape(1, d))
    dact = _mm(dx2_b, wts["w_ffn_down"], "nt", BF16, "d_act")
    g["w_ffn_down"] = _mm(act, dx2_b, "tn", BF16, "g_ffn_down")
    dgt, dup = _swiglu_bwd(gt, up, dact)
    g["w_ffn_gate_t"] = _mm(dgt, h2, "tn", BF16, "g_ffn_gate")
    g["w_ffn_up_t"] = _mm(dup, h2, "tn", BF16, "g_ffn_up")
    dh2 = _mm(dgt, wts["w_ffn_gate_t"], "nn", F32, "d_h2_gate")
    dh2 = _mm(dup, wts["w_ffn_up_t"], "nn", F32, "d_h2_up", add=dh2)
    dx1, dx1_b, g["norm_ffn"] = _rms_bwd(x1, dh2, wts["norm_ffn"], dx2, "rms_ffn_bwd")

    dmixed = _mm(dx1_b, wts["w_mix_out"], "nt", F32, "d_mixed")
    g["w_mix_out"] = _mm(mixed, dx1_b, "tn", BF16, "g_mix_out")
    dy_ssm, dy_att, dgate, g["b_gate"] = _mix_bwd(gate_logits, wts["b_gate"], y_ssm, y_att, dmixed)

    datt = _mm(dy_att, wts["w_att_out_t"], "nn", F32, "d_att")
    g["w_att_out_t"] = _mm(dy_att, att, "tn", BF16, "g_att_out")
    do_g, dlt_g = _merge_bwd(o_g, lse_g, datt, nb, seq)
    dq_g, dk_g, dv_g = zip(*[_att_bwd(qg[i], kg[i], vg[i], do_g[i], lse_g[i], dlt_g[i], i, seq) for i in groups])
    dqkv = _rope_bwd(dq_g, dk_g, dv_g, cos, sin, nb, seq)

    dyn = _mm(dy_ssm, wts["w_ssm_out"], "nt", BF16, "d_yn")
    g["w_ssm_out"] = _mm(yn, dy_ssm, "tn", BF16, "g_ssm_out")
    dy, dz, g["ssm_norm"] = _gate_norm_bwd(y, z, wts["ssm_norm"], dyn)
    side = overlap.scatter_side({n: g.pop(n) for n in LATE})
    dxc, ddtr, g_bias, g_alog, g_dskip, *scattered = _ssd_bwd(xc, dtr, dtrt, bias_r, bias_c, a_r, a_c, dskip_r,
                                                               states, dy, nb, seq, side)
    g["dt_bias"] = g_bias.reshape(1, N_HEADS)
    g["a_log"] = g_alog.reshape(1, N_HEADS)
    g["d_skip"] = g_dskip.reshape(1, N_HEADS)
    dpre, g_conv_w, g_conv_b = _conv_bwd_pre(xbc, conv_w, conv_b, dxc, seq)
    g["conv_w"] = _group_major_inv(g_conv_w, 1)
    g["conv_b"] = _group_major_inv(g_conv_b, 1)
    dxbc = _conv_bwd_in(dpre, conv_w, seq)
    ddt = jnp.pad(ddtr.transpose(1, 0, 2).reshape(t, N_HEADS), ((0, 0), (0, DT_PAD - N_HEADS))).astype(BF16)

    dh = _mm(dz, w_z, "nn", F32, "d_h_z")
    dh = _mm(dxbc, w_xbc, "nn", F32, "d_h_xbc", add=dh)
    dh = _mm(ddt, w_dt, "nn", F32, "d_h_dt", add=dh)
    dh = _mm(dqkv, w_qkv, "nn", F32, "d_h_qkv", add=dh)
    dh = _mm(dgate, w_gate, "nn", F32, "d_h_gate", add=dh)
    g["w_in_t"] = jnp.concatenate([
        _mm(dz, h, "tn", BF16, "g_in_z"),
        _group_major_inv(_mm(dxbc, h, "tn", BF16, "g_in_xbc"), 0),
        _mm(ddt, h, "tn", BF16, "g_in_dt")[:N_HEADS],
        _mm(dqkv, h, "tn", BF16, "g_in_qkv"),
        _mm(dgate, h, "tn", BF16, "g_in_gate")], axis=0)
    dx, _, g["norm_mix"] = _rms_bwd(x, dh, wts["norm_mix"], dx1, "rms_mix_bwd")
    return loss[0, 0], dx.reshape(nb, seq, d), g, scattered


def kernel(x, norm_mix, w_in, b_gate, conv_w, conv_b, dt_bias, a_log, d_skip, ssm_norm, w_ssm_out, w_att_out, w_mix_out, norm_ffn, w_ffn_gate, w_ffn_up, w_ffn_down, norm_final, loss_target, m_norm_mix, m_w_in, m_b_gate, m_conv_w, m_conv_b, m_dt_bias, m_a_log, m_d_skip, m_ssm_norm, m_w_ssm_out, m_w_att_out, m_w_mix_out, m_norm_ffn, m_w_ffn_gate, m_w_ffn_up, m_w_ffn_down, m_norm_final, v_norm_mix, v_w_in, v_b_gate, v_conv_w, v_conv_b, v_dt_bias, v_a_log, v_d_skip, v_ssm_norm, v_w_ssm_out, v_w_att_out, v_w_mix_out, v_norm_ffn, v_w_ffn_gate, v_w_ffn_up, v_w_ffn_down, v_norm_final):
    names = ("norm_mix", "w_in", "b_gate", "conv_w", "conv_b", "dt_bias", "a_log", "d_skip", "ssm_norm", "w_ssm_out",
             "w_att_out", "w_mix_out", "norm_ffn", "w_ffn_gate", "w_ffn_up", "w_ffn_down", "norm_final")
    w_loc = dict(zip(names, (norm_mix, w_in, b_gate, conv_w, conv_b, dt_bias, a_log, d_skip, ssm_norm, w_ssm_out,
                             w_att_out, w_mix_out, norm_ffn, w_ffn_gate, w_ffn_up, w_ffn_down, norm_final)))
    m_loc = dict(zip(names, (m_norm_mix, m_w_in, m_b_gate, m_conv_w, m_conv_b, m_dt_bias, m_a_log, m_d_skip,
                             m_ssm_norm, m_w_ssm_out, m_w_att_out, m_w_mix_out, m_norm_ffn, m_w_ffn_gate,
                             m_w_ffn_up, m_w_ffn_down, m_norm_final)))
    v_loc = dict(zip(names, (v_norm_mix, v_w_in, v_b_gate, v_conv_w, v_conv_b, v_dt_bias, v_a_log, v_d_skip,
                             v_ssm_norm, v_w_ssm_out, v_w_att_out, v_w_mix_out, v_norm_ffn, v_w_ffn_gate,
                             v_w_ffn_up, v_w_ffn_down, v_norm_final)))
    two_d = lambda a: a.reshape(a.shape[-2:]) if a.ndim >= 2 else a.reshape(1, -1)
    w2 = {n: two_d(a) for n, a in w_loc.items()}
    chip = 2 * lax.axis_index("x") + lax.axis_index("y")
    c = lax.axis_index("c")

    wire_shapes = {n: _wire_shard(w2[n], n).shape for n in SHARDED}
    true_rows = {n: wire_shapes[n][0] * wire_shapes[n][1] // PACK_W for n in SHARDED}
    seg_rows = {n: _rows(wire_shapes[n][0] * wire_shapes[n][1]) for n in SHARDED}
    buckets = {"first": ("w_in",), "late": tuple(n for n in SHARDED if n != "w_in")}
    rows_of = {b: _padded_rows(sum(seg_rows[n] for n in ns)) for b, ns in buckets.items()}

    def pack_shards(b):
        packed = _pack_rows([_wire_shard(w2[n], n).astype(BF16) for n in buckets[b]], rows_of[b])
        return packed.reshape(2, rows_of[b] // 2, PACK_W)

    def unpack_full(gathered, b):
        wg, out, off = gathered.reshape(N_CHIPS, rows_of[b], PACK_W), {}, 0
        for n in buckets[b]:
            rows, cols = wire_shapes[n]
            out[_wire_name(n)] = wg[:, off:off + true_rows[n]].reshape(N_CHIPS * rows, cols)
            off += seg_rows[n]
        return out

    def pack_grads(g, b):
        sections = [_pack_rows([g[_wire_name(n)].reshape(N_CHIPS, true_rows[n], PACK_W)[k] for n in buckets[b]],
                               rows_of[b]) for k in range(N_CHIPS)]
        return jnp.stack(sections).reshape(N_CHIPS, 2, rows_of[b] // 2, PACK_W).transpose(1, 0, 2, 3)

    def chip_sums(g, b):
        g2 = pack_grads(g, b)
        return _add_own_half(g2, _swap_halves(g2, b), c, b)

    def finish(by_source, b):
        reduced = _join_halves(_sum_chips(by_source, b), b).reshape(rows_of[b], PACK_W)
        out, off = {}, 0
        for n in buckets[b]:
            wire = reduced[off:off + true_rows[n]].reshape(wire_shapes[n])
            out[n] = wire.T if n in COL_SHARDED else wire
            off += seg_rows[n]
        return out

    full = unpack_full(_gather_weights(pack_shards("first")), "first")
    for n in SMALL:
        full[n] = w2[n]
    overlap = _Overlap(_gather_side(pack_shards("late")), lambda outs: unpack_full(outs[0], "late"),
                       lambda g: _scatter_side(chip_sums(g, "late")))

    n_conv = w2["conv_w"].shape[1]
    placed = lax.dynamic_update_slice_in_dim(jnp.zeros((CONV_K, N_CHIPS * n_conv), F32), w2["conv_w"], chip * n_conv, 1)
    placed = jnp.where(c == 0, placed, 0.0)
    full["conv_w"] = _allreduce_small(_pack_rows([placed], _rows(int(placed.size))), "gather_conv_w").reshape(
        -1)[:placed.size].reshape(placed.shape)

    loss_sum, grad_x, g_full, scattered = _local_step(x, loss_target, full, overlap)
    loss = lax.psum(loss_sum, ("x", "y", "c"))

    g_shard = {}
    small_names = SMALL + ("conv_w",)
    small_flat = jnp.concatenate([g_full[n].reshape(-1) for n in small_names])
    small = _allreduce_small(_pack_rows([small_flat], _rows(int(small_flat.size))), "allreduce_small").reshape(-1)
    off = 0
    for n in small_names:
        size = int(g_full[n].size)
        g_shard[n] = small[off:off + size].reshape(g_full[n].shape)
        off += size
    g_shard["conv_w"] = lax.dynamic_slice_in_dim(g_shard["conv_w"], chip * n_conv, n_conv, 1)

    g_shard.update(finish(scattered[0], "late"))
    g_shard.update(finish(_scatter_to_chips(chip_sums(g_full, "first")), "first"))

    grads, deltas, new_m, new_v = [], [], [], []
    for n in names:
        shape = w_loc[n].shape
        d_, m_, v_ = _adamw(w2[n], g_shard[n], two_d(m_loc[n]), two_d(v_loc[n]), "adamw_" + n)
        grads.append(g_shard[n].reshape(shape))
        deltas.append(d_.reshape(shape))
        new_m.append(m_.reshape(shape))
        new_v.append(v_.reshape(shape))
    return (loss, grad_x, *grads, *deltas, *new_m, *new_v)
```

```python
import functools
from typing import Callable, NamedTuple, Optional

import jax
import jax.numpy as jnp
from jax import lax
from jax.experimental import pallas as pl
from jax.experimental.pallas import tpu as pltpu

F32 = jnp.float32
BF16 = jnp.bfloat16
SDS = jax.ShapeDtypeStruct
MESH = pl.DeviceIdType.MESH

D_MODEL = 1024
D_INNER = 2048
N_HEADS = 32
HEAD_P = 64
N_GROUPS = 4
HEADS_PER_GROUP = N_HEADS // N_GROUPS
D_STATE = 128
CONV_K = 4
CHUNK = 128
CONV_DIM = D_INNER + 2 * N_GROUPS * D_STATE
GROUP_W = D_INNER // N_GROUPS + 2 * D_STATE
ATT_HEADS = 12
ATT_D = 128
ATT_SLOTS = 4
ATT_W = ATT_SLOTS * ATT_D
ATT_DILATIONS = (1, 4, 16)
ATT_BLOCK = 128
QKV_DIM = 3 * ATT_HEADS * ATT_D
D_FF = 2816
DT_PAD = 128
ROPE_THETA = 10000.0
EPS = 1e-6
N_CHIPS = 4
LANES = 128

ADAM_LR = 0.001
ADAM_B1 = 0.9
ADAM_B2 = 0.999
ADAM_EPS = 1e-08
ADAM_WD = 0.01
ADAM_STEP = 10

VMEM_LIMIT = 48 * 1024 * 1024


def _cparams(semantics):
    return pltpu.CompilerParams(dimension_semantics=semantics, vmem_limit_bytes=VMEM_LIMIT)


def _pick(n, cap):
    best = None
    for t in range(LANES, min(n, cap) + 1, LANES):
        if n % t == 0:
            best = t
    return best or n


def _row_tile(rows, cap):
    best = None
    for t in range(8, min(rows, cap) + 1, 8):
        if rows % t == 0:
            best = t
    return best or rows


def _sigmoid(x):
    return 1.0 / (1.0 + jnp.exp(-x))


def _softplus(x):
    return jnp.maximum(x, 0.0) + jnp.log(1.0 + jnp.exp(-jnp.abs(x)))


def _dot(a, b):
    return jnp.dot(a, b, preferred_element_type=F32)


def _dot_nt(a, b):
    return lax.dot_general(a, b, (((1,), (1,)), ((), ())), preferred_element_type=F32)


def _dot_tn(a, b):
    return lax.dot_general(a, b, (((0,), (0,)), ((), ())), preferred_element_type=F32)


def _mm(a, b, mode, out_dtype, name, add=None, side=None):
    if mode == "nn":
        (m, k), (_, n) = a.shape, b.shape
    elif mode == "nt":
        (m, k), (n, _) = a.shape, b.shape
    else:
        (k, m), (_, n) = a.shape, b.shape
    tm, tn = _pick(m, 1536), _pick(n, 1536)
    tk = k if k <= 2048 else _pick(k, 2048)
    nk = k // tk
    dims = {"nn": ((1,), (0,)), "nt": ((1,), (1,)), "tn": ((0,), (0,))}[mode]

    def partial_product(a_ref, b_ref):
        return lax.dot_general(a_ref[...].astype(BF16), b_ref[...].astype(BF16), (dims, ((), ())),
                               preferred_element_type=F32)

    def body(*refs):
        a_ref, b_ref = refs[:2]
        c_ref = refs[2] if add is not None else None
        o_ref = refs[3] if add is not None else refs[2]

        def finish(r):
            if add is not None:
                r = r + c_ref[...].astype(F32)
            o_ref[...] = r.astype(out_dtype)

        if nk == 1:
            finish(partial_product(a_ref, b_ref))
            return
        acc = refs[-1]
        kk = pl.program_id(2)

        @pl.when(kk == 0)
        def _():
            acc[...] = partial_product(a_ref, b_ref)

        @pl.when((kk > 0) & (kk < nk - 1))
        def _():
            acc[...] += partial_product(a_ref, b_ref)

        @pl.when(kk == nk - 1)
        def _():
            finish(acc[...] + partial_product(a_ref, b_ref))

    a_spec = {"nn": pl.BlockSpec((tm, tk), lambda j, i, q: (i, q)),
              "nt": pl.BlockSpec((tm, tk), lambda j, i, q: (i, q)),
              "tn": pl.BlockSpec((tk, tm), lambda j, i, q: (q, i))}[mode]
    b_spec = {"nn": pl.BlockSpec((tk, tn), lambda j, i, q: (q, j)),
              "nt": pl.BlockSpec((tn, tk), lambda j, i, q: (j, q)),
              "tn": pl.BlockSpec((tk, tn), lambda j, i, q: (q, j))}[mode]
    o_spec = pl.BlockSpec((tm, tn), lambda j, i, q: (i, j))
    ins, specs = [a, b], [a_spec, b_spec]
    if add is not None:
        ins.append(add)
        specs.append(o_spec)
    acc = [pltpu.VMEM((tm, tn), F32)] if nk > 1 else []
    grid = (n // tn, m // tm, nk)
    if side is None:
        return pl.pallas_call(
            body, name=name, grid=grid, in_specs=specs, out_specs=o_spec, out_shape=SDS((m, n), out_dtype),
            scratch_shapes=acc, compiler_params=_cparams(("parallel", "parallel", "arbitrary")))(*ins)
    return pl.pallas_call(
        _attach_side(body, len(ins), 1, side, grid), name=name, grid=grid,
        in_specs=specs + [ANY] * len(side.ins), out_specs=[o_spec] + [ANY] * len(side.out_shapes),
        out_shape=[SDS((m, n), out_dtype)] + list(side.out_shapes), scratch_shapes=acc + list(side.scratch),
        compiler_params=_cparams(("arbitrary", "arbitrary", "arbitrary")))(*ins, *side.ins)


def _rw(name, fn, nsteps, ins, outs, n_acc=0):
    n_in, n_out = len(ins), len(outs)

    def body(*refs):
        i = pl.program_id(0)
        vals = fn(i, *refs[:n_in])
        for q, (r, v) in enumerate(zip(refs[n_in:], vals)):
            if q < n_out - n_acc:
                r[...] = v.astype(r.dtype)
            else:
                @pl.when(i == 0)
                def _(r=r):
                    r[...] = jnp.zeros_like(r)

                r[...] += v

    return pl.pallas_call(
        body, name=name, grid=(nsteps,), in_specs=[s for _, s in ins], out_specs=[s for _, s in outs],
        out_shape=[o for o, _ in outs], compiler_params=_cparams(("arbitrary",)))(*[a for a, _ in ins])


def _rs(tm, w, cb=0):
    return pl.BlockSpec((tm, w), lambda i: (i, cb))


def _fs(shape):
    nd = len(shape)
    return pl.BlockSpec(shape, lambda i: (0,) * nd)


def _colsum(v):
    return jnp.sum(v, axis=0, keepdims=True)


def _rms_fwd(x, g, name):
    t, d = x.shape
    tm = 512

    def fn(i, x_ref, g_ref):
        xv = x_ref[...]
        r = lax.rsqrt(jnp.mean(xv * xv, axis=-1, keepdims=True) + EPS)
        return [xv * r * g_ref[...]]

    return _rw(name, fn, t // tm, [(x, _rs(tm, d)), (g, _fs((1, d)))], [(SDS((t, d), BF16), _rs(tm, d))])[0]


def _rms_bwd(x, dh, g, dres, name):
    t, d = x.shape
    tm = 512

    def fn(i, x_ref, dh_ref, g_ref, dres_ref):
        xv = x_ref[...]
        r = lax.rsqrt(jnp.mean(xv * xv, axis=-1, keepdims=True) + EPS)
        xhat = xv * r
        dhv = dh_ref[...]
        dxhat = dhv * g_ref[...]
        dx = dres_ref[...] + r * (dxhat - xhat * jnp.mean(dxhat * xhat, axis=-1, keepdims=True))
        return [dx, dx, _colsum(dhv * xhat)]

    return _rw(name, fn, t // tm,
               [(x, _rs(tm, d)), (dh, _rs(tm, d)), (g, _fs((1, d))), (dres, _rs(tm, d))],
               [(SDS((t, d), F32), _rs(tm, d)), (SDS((t, d), BF16), _rs(tm, d)), (SDS((1, d), F32), _fs((1, d)))],
               n_acc=1)


def _final_fwd_bwd(x2, target, g):
    t, d = x2.shape
    tm = 512

    def fn(i, x_ref, t_ref, g_ref):
        xv = x_ref[...]
        gv = g_ref[...]
        r = lax.rsqrt(jnp.mean(xv * xv, axis=-1, keepdims=True) + EPS)
        xhat = xv * r
        diff = xhat * gv - t_ref[...]
        lsum = 0.5 * jnp.sum(jnp.sum(diff * diff, axis=-1, keepdims=True) * (1.0 / d), axis=0, keepdims=True)
        dy = diff * (1.0 / d)
        dxhat = dy * gv
        dx = r * (dxhat - xhat * jnp.mean(dxhat * xhat, axis=-1, keepdims=True))
        return [dx, dx, _colsum(dy * xhat), lsum]

    return _rw("final_norm_loss", fn, t // tm,
               [(x2, _rs(tm, d)), (target, _rs(tm, d)), (g, _fs((1, d)))],
               [(SDS((t, d), F32), _rs(tm, d)), (SDS((t, d), BF16), _rs(tm, d)), (SDS((1, d), F32), _fs((1, d))),
                (SDS((1, 1), F32), _fs((1, 1)))], n_acc=2)


CONV_TS = 512
CONV_HALO = 8


def _conv_specs(seq, c):
    ts, tc = CONV_TS, GROUP_W
    hb = ts // CONV_HALO
    u_spec = pl.BlockSpec((ts, tc), lambda j, i: (i, j))
    prev_spec = pl.BlockSpec((CONV_HALO, tc), lambda j, i: (jnp.maximum(i * hb - 1, 0), j))
    w_spec = pl.BlockSpec((CONV_K, tc), lambda j, i: (0, j))
    b_spec = pl.BlockSpec((1, tc), lambda j, i: (0, j))
    return u_spec, prev_spec, w_spec, b_spec


CONV_ROWS = 16


def _conv_pre(i, seq, u_ref, prev_ref, w_ref, b_ref, ext):
    ts = CONV_TS
    first = (i % (seq // ts)) == 0
    ext[0:CONV_HALO, :] = jnp.where(first, 0.0, prev_ref[...])
    ext[CONV_HALO:, :] = u_ref[...]
    acc = jnp.broadcast_to(b_ref[...], u_ref.shape)
    for q in range(CONV_K):
        acc = acc + w_ref[q:q + 1, :] * ext[pl.ds(CONV_HALO - CONV_K + 1 + q, ts), :]
    return acc


def _conv_fwd(u, w, b, seq):
    t, c = u.shape
    ts, tc = CONV_TS, GROUP_W
    u_spec, prev_spec, w_spec, b_spec = _conv_specs(seq, c)

    def body(u_ref, prev_ref, w_ref, b_ref, o_ref, ext):
        pre = _conv_pre(pl.program_id(1), seq, u_ref, prev_ref, w_ref, b_ref, ext)
        o_ref[...] = pre * _sigmoid(pre)

    return pl.pallas_call(
        body, name="conv_fwd", grid=(c // tc, t // ts), in_specs=[u_spec, prev_spec, w_spec, b_spec],
        out_specs=u_spec, out_shape=SDS((t, c), F32), scratch_shapes=[pltpu.VMEM((ts + CONV_HALO, tc), F32)],
        compiler_params=_cparams(("parallel", "arbitrary")))(u, u, w, b)


def _conv_bwd_pre(u, w, b, dxc, seq):
    t, c = u.shape
    ts, tc = CONV_TS, GROUP_W
    u_spec, prev_spec, w_spec, b_spec = _conv_specs(seq, c)

    def body(u_ref, prev_ref, w_ref, b_ref, d_ref, dpre_ref, dw_ref, db_ref, ext):
        i = pl.program_id(1)
        pre = _conv_pre(i, seq, u_ref, prev_ref, w_ref, b_ref, ext)
        sg = _sigmoid(pre)
        dpre = d_ref[...] * sg * (1.0 + pre * (1.0 - sg))
        dpre_ref[...] = dpre

        @pl.when(i == 0)
        def _():
            dw_ref[...] = jnp.zeros_like(dw_ref)
            db_ref[...] = jnp.zeros_like(db_ref)

        db_ref[...] += _colsum(dpre)
        for q in range(CONV_K):
            dw_ref[q:q + 1, :] += _colsum(dpre * ext[pl.ds(CONV_HALO - CONV_K + 1 + q, ts), :])

    return pl.pallas_call(
        body, name="conv_bwd_pre", grid=(c // tc, t // ts),
        in_specs=[u_spec, prev_spec, w_spec, b_spec, u_spec], out_specs=[u_spec, w_spec, b_spec],
        out_shape=[SDS((t, c), F32), SDS((CONV_K, c), F32), SDS((1, c), F32)],
        scratch_shapes=[pltpu.VMEM((ts + CONV_HALO, tc), F32)],
        compiler_params=_cparams(("parallel", "arbitrary")))(u, u, w, b, dxc)


def _conv_bwd_in(dpre, w, seq):
    t, c = dpre.shape
    ts, tc = CONV_TS, GROUP_W
    hb = ts // CONV_HALO
    last = t // CONV_HALO - 1
    d_spec = pl.BlockSpec((ts, tc), lambda j, i: (i, j))
    next_spec = pl.BlockSpec((CONV_HALO, tc), lambda j, i: (jnp.minimum((i + 1) * hb, last), j))
    w_spec = pl.BlockSpec((CONV_K, tc), lambda j, i: (0, j))

    def body(d_ref, next_ref, w_ref, o_ref, ext):
        i = pl.program_id(1)
        nts = seq // ts
        is_last = (i % nts) == nts - 1
        ext[0:ts, :] = d_ref[...]
        ext[ts:, :] = jnp.where(is_last, 0.0, next_ref[...])
        wv = w_ref[...]

        def rows(j, carry):
            r0 = pl.multiple_of(j * CONV_ROWS, CONV_ROWS)
            blk = ext[pl.ds(r0, CONV_ROWS + CONV_HALO), :]
            acc = wv[CONV_K - 1:CONV_K] * blk[0:CONV_ROWS]
            for q in range(CONV_K - 1):
                acc = acc + wv[q:q + 1] * blk[CONV_K - 1 - q:CONV_K - 1 - q + CONV_ROWS]
            o_ref[pl.ds(r0, CONV_ROWS), :] = acc.astype(o_ref.dtype)
            return carry

        lax.fori_loop(0, ts // CONV_ROWS, rows, 0)

    return pl.pallas_call(
        body, name="conv_bwd_in", grid=(c // tc, t // ts), in_specs=[d_spec, next_spec, w_spec],
        out_specs=d_spec, out_shape=SDS((t, c), BF16), scratch_shapes=[pltpu.VMEM((ts + CONV_HALO, tc), F32)],
        compiler_params=_cparams(("parallel", "arbitrary")))(dpre, dpre, w)


def _split3(v):
    hi = v.astype(BF16)
    r1 = v - hi.astype(F32)
    mid = r1.astype(BF16)
    lo = (r1 - mid.astype(F32)).astype(BF16)
    return hi, mid, lo


def _ssd_prelude(dtr_ref, dtrt_ref, bias_ref, biast_ref, a_ref, at_ref):
    dt = _softplus(dtr_ref[...] + bias_ref[...])
    dtt = _softplus(dtrt_ref[...] + biast_ref[...])
    ri = lax.broadcasted_iota(jnp.int32, (CHUNK, CHUNK), 0)
    ci = lax.broadcasted_iota(jnp.int32, (CHUNK, CHUNK), 1)
    lower = ri >= ci
    upper = ri <= ci
    lower_b = jnp.where(lower, 1.0, 0.0).astype(BF16)
    upper_b = jnp.where(upper, 1.0, 0.0).astype(BF16)
    acs = sum(_dot(lower_b, p) for p in _split3(dt * a_ref[...]))
    acst = sum(_dot(p, upper_b) for p in _split3(dtt * at_ref[...]))
    return dt, acs, acst, lower, upper, lower_b, upper_b


def _ssd_specs(seq):
    nc = seq // CHUNK
    hg = HEADS_PER_GROUP
    row = lambda cc: (lambda g, b, c: (b * nc + cc(c), g))
    fwd = lambda c: c
    rev = lambda c: nc - 1 - c

    def specs(cc):
        return dict(
            xc=pl.BlockSpec((CHUNK, GROUP_W), lambda g, b, c: (b * nc + cc(c), g)),
            y=pl.BlockSpec((CHUNK, D_INNER // N_GROUPS), lambda g, b, c: (b * nc + cc(c), g)),
            dtr=pl.BlockSpec((None, CHUNK, hg), lambda g, b, c: (g, b * nc + cc(c), 0)),
            dtrt=pl.BlockSpec((None, None, hg, CHUNK), lambda g, b, c: (g, b, 0, cc(c))),
            prow=pl.BlockSpec((None, 1, hg), lambda g, b, c: (g, 0, 0)),
            pcol=pl.BlockSpec((None, hg, 1), lambda g, b, c: (g, 0, 0)),
            st=pl.BlockSpec((None, None, None, D_STATE, hg * HEAD_P), lambda g, b, c: (g, b, cc(c), 0, 0)),
        )

    return specs(fwd), specs(rev)


def _head_maps():
    hw = HEADS_PER_GROUP * HEAD_P
    shift = HEAD_P.bit_length() - 1
    hj = lax.broadcasted_iota(jnp.int32, (HEADS_PER_GROUP, hw), 0)
    lq = jnp.right_shift(lax.broadcasted_iota(jnp.int32, (HEADS_PER_GROUP, hw), 1), shift)
    spread = jnp.where(hj == lq, 1.0, 0.0).astype(BF16)
    rq = jnp.right_shift(lax.broadcasted_iota(jnp.int32, (hw, LANES), 0), shift)
    cj = lax.broadcasted_iota(jnp.int32, (hw, LANES), 1)
    gather = jnp.where(rq == cj, 1.0, 0.0).astype(BF16)
    return spread, gather


def _dot01(v, m01):
    hi, mid, _ = _split3(v)
    return _dot(hi, m01) + _dot(mid, m01)


class _Side(NamedTuple):
    ins: tuple
    out_shapes: tuple
    scratch: tuple
    first: Callable
    mid: Optional[Callable]
    last: Callable


NO_SIDE = _Side((), (), (), lambda *refs: None, None, lambda *refs: None)


def _attach_side(body, n_in, n_out, side, grid):
    si, so, ss = len(side.ins), len(side.out_shapes), len(side.scratch)

    def wrapped(*refs):
        ins, s_in = refs[:n_in], refs[n_in:n_in + si]
        outs = refs[n_in + si:n_in + si + n_out]
        s_out = refs[n_in + si + n_out:n_in + si + n_out + so]
        rest = refs[n_in + si + n_out + so:]
        scr, s_scr = rest[:len(rest) - ss], rest[len(rest) - ss:]
        ids = [pl.program_id(a) for a in range(len(grid))]
        inner_first = functools.reduce(lambda p, q: p & q, [i == 0 for i in ids[1:]], ids[0] >= 0)
        at_last = functools.reduce(lambda p, q: p & q, [i == n - 1 for i, n in zip(ids, grid)])

        @pl.when((ids[0] == 0) & inner_first)
        def _():
            side.first(s_in, s_out, s_scr)

        if side.mid is not None:
            @pl.when((ids[0] == grid[0] - 1) & inner_first)
            def _():
                side.mid(s_in, s_out, s_scr)

        body(*ins, *outs, *scr)

        @pl.when(at_last)
        def _():
            side.last(s_in, s_out, s_scr)

    return wrapped


def _ssd_fwd(xc, dtr, dtrt, bias, biast, a, at, dskip, nb, seq, side):
    t = xc.shape[0]
    nc = seq // CHUNK
    hg = HEADS_PER_GROUP
    hw = hg * HEAD_P
    sp, _ = _ssd_specs(seq)

    def body(xc_ref, dtr_ref, dtrt_ref, bias_ref, biast_ref, a_ref, at_ref, d_ref, y_ref, sin_ref, st):
        @pl.when(pl.program_id(2) == 0)
        def _():
            st[...] = jnp.zeros_like(st)

        s_in = st[...]
        sin_ref[...] = s_in
        dt, acs, acst, lower, _, _, _ = _ssd_prelude(dtr_ref, dtrt_ref, bias_ref, biast_ref, a_ref, at_ref)
        spread, _ = _head_maps()
        x = xc_ref[...]
        xs = x[:, :hw]
        b16 = x[:, hw:hw + D_STATE].astype(BF16)
        c16 = x[:, hw + D_STATE:].astype(BF16)
        cb = _dot_nt(c16, b16)
        last = acs[CHUNK - 1:CHUNK, :]
        e_x = _dot01(jnp.exp(acs), spread)
        dec_x = _dot01(jnp.exp(last - acs), spread)
        tot_x = e_x[CHUNK - 1:CHUNK, :]
        d_x = _dot01(jnp.broadcast_to(d_ref[...], (8, hg)), spread)[0:1, :]
        xdtf = xs * _dot01(dt, spread)
        xdt16 = xdtf.astype(BF16)
        yoff = e_x * _dot(c16, s_in.astype(BF16))
        st[...] = tot_x * s_in + _dot_tn(b16, (dec_x * xdtf).astype(BF16))
        parts = []
        for j in range(hg):
            decay = jnp.exp(jnp.where(lower, acs[:, j:j + 1] - acst[j:j + 1, :], -jnp.inf))
            parts.append(_dot((cb * decay).astype(BF16), xdt16[:, HEAD_P * j:HEAD_P * (j + 1)]))
        y_ref[...] = jnp.concatenate(parts, axis=-1) + yoff + d_x * xs

    return pl.pallas_call(
        _attach_side(body, 8, 2, side, (N_GROUPS, nb, nc)), name="ssd_fwd", grid=(N_GROUPS, nb, nc),
        in_specs=[sp["xc"], sp["dtr"], sp["dtrt"], sp["prow"], sp["pcol"], sp["prow"], sp["pcol"], sp["prow"]]
        + [ANY] * len(side.ins),
        out_specs=[sp["y"], sp["st"]] + [ANY] * len(side.out_shapes),
        out_shape=[SDS((t, D_INNER), F32), SDS((N_GROUPS, nb, nc, D_STATE, hw), F32)] + list(side.out_shapes),
        scratch_shapes=[pltpu.VMEM((D_STATE, hw), F32)] + list(side.scratch),
        compiler_params=_cparams(("arbitrary", "arbitrary", "arbitrary")))(
            xc, dtr, dtrt, bias, biast, a, at, dskip, *side.ins)


def _ssd_bwd(xc, dtr, dtrt, bias, biast, a, at, dskip, states, dy, nb, seq, side):
    t = xc.shape[0]
    nc = seq // CHUNK
    hg = HEADS_PER_GROUP
    hw = hg * HEAD_P
    _, sp = _ssd_specs(seq)

    def body(xc_ref, dtr_ref, dtrt_ref, bias_ref, biast_ref, a_ref, at_ref, d_ref, sin_ref, dy_ref,
             dxc_ref, ddtr_ref, gbias_ref, ga_ref, gd_ref, ds):
        first = (pl.program_id(1) == 0) & (pl.program_id(2) == 0)

        @pl.when(pl.program_id(2) == 0)
        def _():
            ds[...] = jnp.zeros_like(ds)

        @pl.when(first)
        def _():
            gbias_ref[...] = jnp.zeros_like(gbias_ref)
            ga_ref[...] = jnp.zeros_like(ga_ref)
            gd_ref[...] = jnp.zeros_like(gd_ref)

        dt, acs, acst, lower, upper, _, upper_b = _ssd_prelude(dtr_ref, dtrt_ref, bias_ref, biast_ref, a_ref, at_ref)
        spread, gather = _head_maps()
        x = xc_ref[...]
        dy = dy_ref[...]
        xs = x[:, :hw]
        b16 = x[:, hw:hw + D_STATE].astype(BF16)
        c16 = x[:, hw + D_STATE:].astype(BF16)
        dy16 = dy.astype(BF16)
        cb = _dot_nt(c16, b16)
        cbt = _dot_nt(b16, c16)
        last = acs[CHUNK - 1:CHUNK, :]
        e8 = jnp.exp(acs)
        dec8 = jnp.exp(last - acs)
        e_x = _dot01(e8, spread)
        dec_x = _dot01(dec8, spread)
        tot_x = e_x[CHUNK - 1:CHUNK, :]
        dt_x = _dot01(dt, spread)
        d_x = _dot01(jnp.broadcast_to(d_ref[...], (8, hg)), spread)[0:1, :]
        xdtf = xs * dt_x
        xdt16 = xdtf.astype(BF16)
        s_in = sin_ref[...]
        s16 = s_in.astype(BF16)
        ds_out = ds[...]
        ds16 = ds_out.astype(BF16)
        bds = _dot(b16, ds16)
        cs = _dot(c16, s16)
        edy16 = (e_x * dy).astype(BF16)
        ds[...] = tot_x * ds_out + _dot_tn(c16, edy16)
        lane8 = lax.broadcasted_iota(jnp.int32, (CHUNK, hg), 1)
        row8 = lax.broadcasted_iota(jnp.int32, (CHUNK, hg), 0)
        dacs8 = jnp.zeros((CHUNK, hg), F32)
        acc_m = jnp.zeros((CHUNK, CHUNK), F32)
        acc_mt = jnp.zeros((CHUNK, CHUNK), F32)
        dx_parts = []
        for j in range(hg):
            sl = slice(HEAD_P * j, HEAD_P * (j + 1))
            col = acs[:, j:j + 1]
            row = acst[j:j + 1, :]
            decay = jnp.exp(jnp.where(lower, col - row, -jnp.inf))
            decayt = jnp.exp(jnp.where(upper, row - col, -jnp.inf))
            wm = _dot_nt(dy16[:, sl], xdt16[:, sl]) * decay
            wmt = _dot_nt(xdt16[:, sl], dy16[:, sl]) * decayt
            acc_m = acc_m + wm
            acc_mt = acc_mt + wmt
            dacs8 = dacs8 + jnp.where(lane8 == j, jnp.sum(wm * cb, axis=-1, keepdims=True)
                                      - jnp.sum(wmt * cbt, axis=-1, keepdims=True), 0.0)
            dx_parts.append(_dot((cbt * decayt).astype(BF16), dy16[:, sl]))
        dx = jnp.concatenate(dx_parts, axis=-1) + dec_x * bds
        dxc_ref[:, :hw] = dx * dt_x + d_x * dy
        dxc_ref[:, hw:hw + D_STATE] = _dot(acc_mt.astype(BF16), c16) + _dot_nt((dec_x * xdtf).astype(BF16), ds16)
        dxc_ref[:, hw + D_STATE:] = _dot(acc_m.astype(BF16), b16) + _dot_nt(edy16, s16)
        dtot_rows = jnp.broadcast_to(_colsum(ds_out * s_in), (8, hw))
        sums = _dot01(jnp.concatenate([dy * cs, xdtf * bds, dx * xs, dy * xs, dtot_rows], axis=0), gather)
        de8 = sums[0:CHUNK, :hg]
        ddec8 = sums[CHUNK:2 * CHUNK, :hg]
        ddtx8 = sums[2 * CHUNK:3 * CHUNK, :hg]
        gd8 = _colsum(sums[3 * CHUNK:4 * CHUNK, :hg])
        dtot8 = sums[4 * CHUNK:4 * CHUNK + 1, :hg]
        extra = _colsum(ddec8 * dec8) + dtot8 * e8[CHUNK - 1:CHUNK, :]
        dacs8 = dacs8 + de8 * e8 - ddec8 * dec8 + jnp.where(row8 == CHUNK - 1, extra, 0.0)
        da = sum(_dot(upper_b, p) for p in _split3(dacs8))
        av = a_ref[...]
        ddt = da * av + ddtx8
        ddtr = ddt * _sigmoid(dtr_ref[...] + bias_ref[...])
        ddtr_ref[...] = ddtr
        gbias_ref[...] += _colsum(ddtr)
        ga_ref[...] += _colsum(da * dt) * av
        gd_ref[...] += gd8

    return pl.pallas_call(
        _attach_side(body, 10, 5, side, (N_GROUPS, nb, nc)), name="ssd_bwd", grid=(N_GROUPS, nb, nc),
        in_specs=[sp["xc"], sp["dtr"], sp["dtrt"], sp["prow"], sp["pcol"], sp["prow"], sp["pcol"], sp["prow"],
                  sp["st"], sp["y"]] + [ANY] * len(side.ins),
        out_specs=[sp["xc"], sp["dtr"], sp["prow"], sp["prow"], sp["prow"]] + [ANY] * len(side.out_shapes),
        out_shape=[SDS((t, N_GROUPS * GROUP_W), F32), SDS((N_GROUPS, t, hg), F32)]
        + [SDS((N_GROUPS, 1, hg), F32)] * 3 + list(side.out_shapes),
        scratch_shapes=[pltpu.VMEM((D_STATE, hw), F32)] + list(side.scratch),
        compiler_params=_cparams(("arbitrary", "arbitrary", "arbitrary")))(
            xc, dtr, dtrt, bias, biast, a, at, dskip, states, dy, *side.ins)


def _group_bcast(v, width, fn):
    parts = []
    for q in range(v.shape[-1] // width):
        s = fn(v[:, q * width:(q + 1) * width])
        parts.append(jnp.broadcast_to(s, (v.shape[0], width)))
    return jnp.concatenate(parts, axis=-1)


def _gate_norm_fwd(y, z, g):
    t, d = y.shape
    tm = 256
    gw = d // N_GROUPS

    def fn(i, y_ref, z_ref, g_ref):
        zv = z_ref[...].astype(F32)
        u = y_ref[...] * (zv * _sigmoid(zv))
        r = lax.rsqrt(_group_bcast(u * u, gw, lambda p: jnp.mean(p, axis=-1, keepdims=True)) + EPS)
        return [u * r * g_ref[...]]

    return _rw("gate_norm_fwd", fn, t // tm, [(y, _rs(tm, d)), (z, _rs(tm, d)), (g, _fs((1, d)))],
               [(SDS((t, d), BF16), _rs(tm, d))])[0]


def _gate_norm_bwd(y, z, g, dyn):
    t, d = y.shape
    tm = 256
    gw = d // N_GROUPS

    def fn(i, y_ref, z_ref, g_ref, dyn_ref):
        zv = z_ref[...].astype(F32)
        yv = y_ref[...]
        sg = _sigmoid(zv)
        sz = zv * sg
        u = yv * sz
        r = lax.rsqrt(_group_bcast(u * u, gw, lambda p: jnp.mean(p, axis=-1, keepdims=True)) + EPS)
        uhat = u * r
        dv = dyn_ref[...].astype(F32)
        duhat = dv * g_ref[...]
        du = r * (duhat - uhat * _group_bcast(duhat * uhat, gw, lambda p: jnp.mean(p, axis=-1, keepdims=True)))
        dz = du * yv * sg * (1.0 + zv * (1.0 - sg))
        return [du * sz, dz, _colsum(dv * uhat)]

    return _rw("gate_norm_bwd", fn, t // tm,
               [(y, _rs(tm, d)), (z, _rs(tm, d)), (g, _fs((1, d))), (dyn, _rs(tm, d))],
               [(SDS((t, d), F32), _rs(tm, d)), (SDS((t, d), BF16), _rs(tm, d)), (SDS((1, d), F32), _fs((1, d)))],
               n_acc=1)


def _rope_tables(seq):
    half = ATT_D // 2
    inv = ROPE_THETA ** (-jnp.arange(half, dtype=F32) / half)
    ang = jnp.arange(seq, dtype=F32)[:, None] * inv[None, :]
    cos, sin = jnp.cos(ang), jnp.sin(ang)
    return jnp.concatenate([cos, cos], axis=-1), jnp.concatenate([-sin, sin], axis=-1)


ATT_TILE = 512


def _strided_spec(r, mtiles):
    return pl.BlockSpec((None, r, None, ATT_TILE // r, ATT_W), lambda i: (i // mtiles, 0, i % mtiles, 0, 0))


def _strided_shape(nb, r, mtiles, dtype):
    return SDS((nb, r, mtiles, ATT_TILE // r, ATT_W), dtype)


def _to_strided(val, out_ref, lanes, r, sc):
    if r == 1:
        out_ref[0, :, lanes] = val.astype(out_ref.dtype)
        return
    sc[...] = val
    for rr in range(r):
        out_ref[rr, :, lanes] = sc[pl.ds(rr, ATT_TILE // r, stride=r), :].astype(out_ref.dtype)


def _from_strided(in_ref, lanes, r, sc):
    if r == 1:
        return in_ref[0, :, lanes].astype(F32)
    for rr in range(r):
        sc[pl.ds(rr, ATT_TILE // r, stride=r), :] = in_ref[rr, :, lanes].astype(F32)
    return sc[...]


def _rope_fwd(qkv, cos, sin, nb, seq):
    t = qkv.shape[0]
    tm = ATT_TILE
    mtiles = seq // tm
    w = ATT_HEADS * ATT_D
    tab = pl.BlockSpec((tm, ATT_D), lambda i: (i % mtiles, 0))
    ng = len(ATT_DILATIONS)

    def body(q_ref, k_ref, v_ref, cos_ref, sin_ref, *rest):
        outs, sc = rest[:3 * ng], rest[3 * ng]
        c, s = cos_ref[...], sin_ref[...]
        for which, ref in enumerate((q_ref, k_ref, v_ref)):
            for h in range(ATT_HEADS):
                g, slot = divmod(h, ATT_SLOTS)
                p = ref[:, h * ATT_D:(h + 1) * ATT_D].astype(F32)
                if which < 2:
                    p = p * c + pltpu.roll(p, ATT_D // 2, 1) * s
                _to_strided(p, outs[which * ng + g], slice(slot * ATT_D, (slot + 1) * ATT_D), ATT_DILATIONS[g], sc)

    out_specs = [_strided_spec(r, mtiles) for _ in range(3) for r in ATT_DILATIONS]
    out_shape = [_strided_shape(nb, r, mtiles, BF16) for _ in range(3) for r in ATT_DILATIONS]
    outs = pl.pallas_call(
        body, name="rope_fwd", grid=(t // tm,),
        in_specs=[_rs(tm, w, 0), _rs(tm, w, 1), _rs(tm, w, 2), tab, tab], out_specs=out_specs, out_shape=out_shape,
        scratch_shapes=[pltpu.VMEM((tm, ATT_D), F32)], compiler_params=_cparams(("arbitrary",)))(
            qkv, qkv, qkv, cos, sin)
    flat = [o.reshape(t, ATT_W) for o in outs]
    return flat[0:ng], flat[ng:2 * ng], flat[2 * ng:]


def _rope_bwd(dq, dk, dv, cos, sin, nb, seq):
    t = dq[0].shape[0]
    tm = ATT_TILE
    mtiles = seq // tm
    w = ATT_HEADS * ATT_D
    tab = pl.BlockSpec((tm, ATT_D), lambda i: (i % mtiles, 0))
    ng = len(ATT_DILATIONS)

    def body(*refs):
        ins, (cos_ref, sin_ref, o_ref, sc) = refs[:3 * ng], refs[3 * ng:]
        c, s = cos_ref[...], sin_ref[...]
        for which in range(3):
            for h in range(ATT_HEADS):
                g, slot = divmod(h, ATT_SLOTS)
                p = _from_strided(ins[which * ng + g], slice(slot * ATT_D, (slot + 1) * ATT_D), ATT_DILATIONS[g], sc)
                if which < 2:
                    p = p * c - pltpu.roll(p, ATT_D // 2, 1) * s
                o_ref[:, which * w + h * ATT_D:which * w + (h + 1) * ATT_D] = p.astype(o_ref.dtype)

    views = [a.reshape(nb, r, mtiles, tm // r, ATT_W) for grp in (dq, dk, dv) for a, r in zip(grp, ATT_DILATIONS)]
    return pl.pallas_call(
        body, name="rope_bwd", grid=(t // tm,),
        in_specs=[_strided_spec(r, mtiles) for _ in range(3) for r in ATT_DILATIONS] + [tab, tab],
        out_specs=_rs(tm, 3 * w), out_shape=SDS((t, 3 * w), BF16),
        scratch_shapes=[pltpu.VMEM((tm, ATT_D), F32)], compiler_params=_cparams(("arbitrary",)))(*views, cos, sin)


def _att_masks():
    ri = lax.broadcasted_iota(jnp.int32, (ATT_BLOCK, ATT_BLOCK), 0)
    ci = lax.broadcasted_iota(jnp.int32, (ATT_BLOCK, ATT_BLOCK), 1)
    return ci <= ri, ci >= ri


def _att_fwd(q, k, v, g, seq):
    t, w = q.shape
    nblk = t // ATT_BLOCK
    nbs = seq // ATT_DILATIONS[g] // ATT_BLOCK
    scale = ATT_D ** -0.5
    cur = pl.BlockSpec((ATT_BLOCK, w), lambda n: (n, 0))
    prev = pl.BlockSpec((ATT_BLOCK, w), lambda n: (jnp.maximum(n - 1, 0), 0))

    def body(q_ref, kc_ref, kp_ref, vc_ref, vp_ref, o_ref, lse_ref):
        has_prev = (pl.program_id(0) % nbs) != 0
        mcur, mprev = _att_masks()
        mask = jnp.concatenate([mprev & has_prev, mcur], axis=-1)
        for h in range(ATT_SLOTS):
            sl = slice(h * ATT_D, (h + 1) * ATT_D)
            keys = jnp.concatenate([kp_ref[:, sl], kc_ref[:, sl]], axis=0)
            vals = jnp.concatenate([vp_ref[:, sl], vc_ref[:, sl]], axis=0)
            s = jnp.where(mask, _dot_nt(q_ref[:, sl], keys) * scale, -jnp.inf)
            m = jnp.max(s, axis=-1, keepdims=True)
            p = jnp.exp(s - m)
            den = jnp.sum(p, axis=-1, keepdims=True)
            o_ref[:, sl] = _dot(p.astype(BF16), vals) / den
            lse_ref[:, sl] = jnp.broadcast_to(m + jnp.log(den), (ATT_BLOCK, ATT_D))

    return pl.pallas_call(
        body, name=f"att_fwd_{g}", grid=(nblk,), in_specs=[cur, cur, prev, cur, prev], out_specs=[cur, cur],
        out_shape=[SDS((t, w), F32), SDS((t, w), F32)],
        compiler_params=_cparams(("arbitrary",)))(q, k, k, v, v)


def _att_bwd(q, k, v, do, lse, dlt, g, seq):
    t, w = q.shape
    nblk = t // ATT_BLOCK
    nbs = seq // ATT_DILATIONS[g] // ATT_BLOCK
    scale = ATT_D ** -0.5
    cur = pl.BlockSpec((ATT_BLOCK, w), lambda n: (n, 0))
    nxt = pl.BlockSpec((ATT_BLOCK, w), lambda n: (jnp.minimum(n + 1, nblk - 1), 0))

    def body(qc_ref, qn_ref, k_ref, v_ref, doc_ref, don_ref, lsec_ref, lsen_ref, dltc_ref, dltn_ref,
             dq_ref, dk_ref, dv_ref, carry):
        n = pl.program_id(0)

        @pl.when((n % nbs) == 0)
        def _():
            carry[...] = jnp.zeros_like(carry)

        has_next = (((n + 1) % nbs) != 0) & (n + 1 < nblk)
        mcur, mprev = _att_masks()
        mask = jnp.concatenate([mcur, mprev & has_next], axis=0)
        for h in range(ATT_SLOTS):
            sl = slice(h * ATT_D, (h + 1) * ATT_D)
            kh, vh = k_ref[:, sl], v_ref[:, sl]
            qs = jnp.concatenate([qc_ref[:, sl], qn_ref[:, sl]], axis=0)
            dos = jnp.concatenate([doc_ref[:, sl], don_ref[:, sl]], axis=0)
            lse = jnp.concatenate([lsec_ref[:, sl], lsen_ref[:, sl]], axis=0)
            dlt = jnp.concatenate([dltc_ref[:, sl], dltn_ref[:, sl]], axis=0)
            p = jnp.where(mask, jnp.exp(_dot_nt(qs, kh) * scale - lse), 0.0)
            ds = (p * (_dot_nt(dos, vh) - dlt) * scale).astype(BF16)
            dqs = _dot(ds, kh)
            dq_ref[:, sl] = (carry[:, sl] + dqs[:ATT_BLOCK]).astype(dq_ref.dtype)
            carry[:, sl] = dqs[ATT_BLOCK:]
            dk_ref[:, sl] = _dot_tn(ds, qs).astype(dk_ref.dtype)
            dv_ref[:, sl] = _dot_tn(p.astype(BF16), dos).astype(dv_ref.dtype)

    return pl.pallas_call(
        body, name=f"att_bwd_{g}", grid=(nblk,), in_specs=[cur, nxt, cur, cur, cur, nxt, cur, nxt, cur, nxt],
        out_specs=[cur, cur, cur], out_shape=[SDS((t, w), BF16)] * 3,
        scratch_shapes=[pltpu.VMEM((ATT_BLOCK, w), F32)],
        compiler_params=_cparams(("arbitrary",)))(q, q, k, v, do, do, lse, lse, dlt, dlt)


def _merge_weights(ls):
    m = jnp.maximum(jnp.maximum(ls[0], ls[1]), ls[2])
    es = [jnp.exp(v - m) for v in ls]
    den = es[0] + es[1] + es[2]
    return [e / den for e in es]


def _merge_fwd(o, lse, nb, seq):
    t = o[0].shape[0]
    tm = ATT_TILE
    mtiles = seq // tm
    ng = len(ATT_DILATIONS)

    def body(*refs):
        o_refs, l_refs, out_ref, scs = refs[:ng], refs[ng:2 * ng], refs[2 * ng], refs[2 * ng + 1:]
        for slot in range(ATT_SLOTS):
            lanes = slice(slot * ATT_D, (slot + 1) * ATT_D)
            ov = [_from_strided(o_refs[g], lanes, r, scs[2 * g]) for g, r in enumerate(ATT_DILATIONS)]
            ws = _merge_weights([_from_strided(l_refs[g], lanes, r, scs[2 * g + 1])
                                 for g, r in enumerate(ATT_DILATIONS)])
            out_ref[:, lanes] = (ws[0] * ov[0] + ws[1] * ov[1] + ws[2] * ov[2]).astype(out_ref.dtype)

    views = [a.reshape(nb, r, mtiles, tm // r, ATT_W) for grp in (o, lse) for a, r in zip(grp, ATT_DILATIONS)]
    return pl.pallas_call(
        body, name="att_merge_fwd", grid=(t // tm,),
        in_specs=[_strided_spec(r, mtiles) for _ in range(2) for r in ATT_DILATIONS],
        out_specs=_rs(tm, ATT_W), out_shape=SDS((t, ATT_W), BF16),
        scratch_shapes=[pltpu.VMEM((tm, ATT_D), F32)] * (2 * ng), compiler_params=_cparams(("arbitrary",)))(*views)


def _merge_bwd(o, lse, datt, nb, seq):
    t = o[0].shape[0]
    tm = ATT_TILE
    mtiles = seq // tm
    ng = len(ATT_DILATIONS)

    def body(*refs):
        o_refs, l_refs, d_ref = refs[:ng], refs[ng:2 * ng], refs[2 * ng]
        do_refs, dlt_refs = refs[2 * ng + 1:3 * ng + 1], refs[3 * ng + 1:4 * ng + 1]
        scs = refs[4 * ng + 1:]
        for slot in range(ATT_SLOTS):
            lanes = slice(slot * ATT_D, (slot + 1) * ATT_D)
            ov = [_from_strided(o_refs[g], lanes, r, scs[2 * g]) for g, r in enumerate(ATT_DILATIONS)]
            ws = _merge_weights([_from_strided(l_refs[g], lanes, r, scs[2 * g + 1])
                                 for g, r in enumerate(ATT_DILATIONS)])
            dv = d_ref[:, lanes]
            att = ws[0] * ov[0] + ws[1] * ov[1] + ws[2] * ov[2]
            dot = jnp.broadcast_to(jnp.sum(dv * att, axis=-1, keepdims=True), (tm, ATT_D))
            for g, r in enumerate(ATT_DILATIONS):
                _to_strided(ws[g] * dv, do_refs[g], lanes, r, scs[2 * ng])
                _to_strided(ws[g] * dot, dlt_refs[g], lanes, r, scs[2 * ng + 1])

    views = [a.reshape(nb, r, mtiles, tm // r, ATT_W) for grp in (o, lse) for a, r in zip(grp, ATT_DILATIONS)]
    outs = pl.pallas_call(
        body, name="att_merge_bwd", grid=(t // tm,),
        in_specs=[_strided_spec(r, mtiles) for _ in range(2) for r in ATT_DILATIONS] + [_rs(tm, ATT_W)],
        out_specs=[_strided_spec(r, mtiles) for _ in range(2) for r in ATT_DILATIONS],
        out_shape=[_strided_shape(nb, r, mtiles, dt) for dt in (BF16, F32) for r in ATT_DILATIONS],
        scratch_shapes=[pltpu.VMEM((tm, ATT_D), F32)] * (2 * ng + 2), compiler_params=_cparams(("arbitrary",)))(
            *views, datt)
    flat = [a.reshape(t, ATT_W) for a in outs]
    return flat[:ng], flat[ng:]


def _mix_fwd(gate_logits, b_gate, y_ssm, y_att):
    t, d = y_ssm.shape
    tm = 512

    def fn(i, g0_ref, g1_ref, b0_ref, b1_ref, ys_ref, ya_ref):
        g0 = _sigmoid(g0_ref[...].astype(F32) + b0_ref[...])
        g1 = _sigmoid(g1_ref[...].astype(F32) + b1_ref[...])
        return [g0 * ys_ref[...].astype(F32) + g1 * ya_ref[...].astype(F32)]

    b_spec = lambda cb: pl.BlockSpec((1, d), lambda i: (0, cb))
    return _rw("mix_fwd", fn, t // tm,
               [(gate_logits, _rs(tm, d, 0)), (gate_logits, _rs(tm, d, 1)), (b_gate, b_spec(0)), (b_gate, b_spec(1)),
                (y_ssm, _rs(tm, d)), (y_att, _rs(tm, d))],
               [(SDS((t, d), BF16), _rs(tm, d))])[0]


def _mix_bwd(gate_logits, b_gate, y_ssm, y_att, dmixed):
    t, d = y_ssm.shape
    tm = 256

    def fn(i, g0_ref, g1_ref, b0_ref, b1_ref, ys_ref, ya_ref, dm_ref):
        g0 = _sigmoid(g0_ref[...].astype(F32) + b0_ref[...])
        g1 = _sigmoid(g1_ref[...].astype(F32) + b1_ref[...])
        dm = dm_ref[...]
        dg = jnp.concatenate([dm * ys_ref[...].astype(F32) * g0 * (1.0 - g0),
                              dm * ya_ref[...].astype(F32) * g1 * (1.0 - g1)], axis=-1)
        return [dm * g0, dm * g1, dg, _colsum(dg)]

    b_spec = lambda cb: pl.BlockSpec((1, d), lambda i: (0, cb))
    return _rw("mix_bwd", fn, t // tm,
               [(gate_logits, _rs(tm, d, 0)), (gate_logits, _rs(tm, d, 1)), (b_gate, b_spec(0)), (b_gate, b_spec(1)),
                (y_ssm, _rs(tm, d)), (y_att, _rs(tm, d)), (dmixed, _rs(tm, d))],
               [(SDS((t, d), BF16), _rs(tm, d)), (SDS((t, d), BF16), _rs(tm, d)),
                (SDS((t, 2 * d), BF16), _rs(tm, 2 * d)), (SDS((1, 2 * d), F32), _fs((1, 2 * d)))], n_acc=1)


def _swiglu_fwd(gt, up):
    t, f = gt.shape
    tm = 256

    def fn(i, g_ref, u_ref):
        gv = g_ref[...].astype(F32)
        return [gv * _sigmoid(gv) * u_ref[...].astype(F32)]

    return _rw("swiglu_fwd", fn, t // tm, [(gt, _rs(tm, f)), (up, _rs(tm, f))], [(SDS((t, f), BF16), _rs(tm, f))])[0]


def _swiglu_bwd(gt, up, dact):
    t, f = gt.shape
    tm = 256

    def fn(i, g_ref, u_ref, d_ref):
        gv, dv = g_ref[...].astype(F32), d_ref[...].astype(F32)
        sg = _sigmoid(gv)
        return [dv * u_ref[...].astype(F32) * sg * (1.0 + gv * (1.0 - sg)), dv * gv * sg]

    return _rw("swiglu_bwd", fn, t // tm, [(gt, _rs(tm, f)), (up, _rs(tm, f)), (dact, _rs(tm, f))],
               [(SDS((t, f), BF16), _rs(tm, f))] * 2)


def _adamw(w, g, m, v, name):
    r, c = w.shape[-2:]
    lead = w.ndim - 2
    tr = _row_tile(r, max(8, 400_000 // c))
    c1 = 1.0 / (1.0 - ADAM_B1 ** ADAM_STEP)
    c2 = 1.0 / (1.0 - ADAM_B2 ** ADAM_STEP)

    def fn(i, w_ref, g_ref, m_ref, v_ref):
        gv = g_ref[...]
        mn = ADAM_B1 * m_ref[...] + (1.0 - ADAM_B1) * gv
        vn = ADAM_B2 * v_ref[...] + (1.0 - ADAM_B2) * (gv * gv)
        delta = -ADAM_LR * ((mn * c1) / (jnp.sqrt(vn * c2) + ADAM_EPS) + ADAM_WD * w_ref[...])
        return [delta, mn, vn]

    spec = pl.BlockSpec((None,) * lead + (tr, c), lambda i: (0,) * lead + (i, 0))
    return _rw(name, fn, r // tr, [(w, spec), (g, spec), (m, spec), (v, spec)], [(SDS(w.shape, F32), spec)] * 3)


ANY = pl.BlockSpec(memory_space=pl.ANY)


def _place():
    x, y, c = lax.axis_index("x"), lax.axis_index("y"), lax.axis_index("c")
    chips = [(1 - x, y), (x, 1 - y), (1 - x, 1 - y)]
    return x, y, c, chips


def _remote(src, dst, ssem, rsem, to):
    return pltpu.make_async_remote_copy(src_ref=src, dst_ref=dst, send_sem=ssem, recv_sem=rsem, device_id=to,
                                        device_id_type=MESH)


def _copy_through_vmem(src, dst, buf, isem, osem):
    chunk = buf.shape[1]
    n = src.shape[0] // chunk
    load = lambda k: pltpu.make_async_copy(src.at[pl.ds(k * chunk, chunk)], buf.at[k % 2], isem.at[k % 2])
    store = lambda k: pltpu.make_async_copy(buf.at[k % 2], dst.at[pl.ds(k * chunk, chunk)], osem.at[k % 2])
    load(0).start()
    for k in range(n):
        load(k).wait()
        if k + 1 < n:
            if k >= 1:
                store(k - 1).wait()
            load(k + 1).start()
        store(k).start()
    if n >= 2:
        store(n - 2).wait()
    store(n - 1).wait()


def _copy_scratch(rows, width, dtype):
    chunk = _row_tile(rows, 512)
    return [pltpu.VMEM((2, chunk, width), dtype), pltpu.SemaphoreType.DMA((2,)), pltpu.SemaphoreType.DMA((2,))]


def _gather_weights(wp):
    def body(w_ref, out_ref, ssem, rsem, buf, isem, osem):
        x, y, c, chips = _place()
        me = 2 * x + y
        sib = (x, y, 1 - c)
        first = [_remote(w_ref.at[c], out_ref.at[me, c], ssem.at[j], rsem.at[j], (*chip, c))
                 for j, chip in enumerate(chips)]
        for cp in first:
            cp.start()
        for half in range(2):
            _copy_through_vmem(w_ref.at[half], out_ref.at[me, half], buf, isem, osem)
        passed = []
        for j, chip in enumerate(chips):
            ci = 2 * chip[0] + chip[1]
            _remote(w_ref.at[c], out_ref.at[ci, c], ssem.at[j], rsem.at[j], (*chip, c)).wait_recv()
            cp = _remote(out_ref.at[ci, c], out_ref.at[ci, c], ssem.at[3 + j], rsem.at[3 + j], sib)
            cp.start()
            passed.append(cp)
        for j, chip in enumerate(chips):
            ci = 2 * chip[0] + chip[1]
            _remote(out_ref.at[ci, 1 - c], out_ref.at[ci, 1 - c], ssem.at[3 + j], rsem.at[3 + j], sib).wait_recv()
        for cp in first + passed:
            cp.wait_send()

    return pl.pallas_call(
        body, name="gather_weights", in_specs=[ANY], out_specs=ANY,
        out_shape=SDS((N_CHIPS,) + wp.shape, wp.dtype),
        scratch_shapes=[pltpu.SemaphoreType.DMA((6,)), pltpu.SemaphoreType.DMA((6,))]
        + _copy_scratch(wp.shape[1], wp.shape[2], wp.dtype),
        compiler_params=pltpu.CompilerParams(has_side_effects=True))(wp)


def _swap_halves(g2, tag):
    def body(g_ref, out_ref, ssem, rsem):
        x, y, c, _ = _place()
        cp = _remote(g_ref.at[1 - c], out_ref, ssem, rsem, (x, y, 1 - c))
        cp.start()
        cp.wait()

    return pl.pallas_call(
        body, name="swap_halves_" + tag, in_specs=[ANY], out_specs=ANY, out_shape=SDS(g2.shape[1:], g2.dtype),
        scratch_shapes=[pltpu.SemaphoreType.DMA(()), pltpu.SemaphoreType.DMA(())],
        compiler_params=pltpu.CompilerParams(has_side_effects=True))(g2)


def _add_own_half(g2, other, c, tag):
    _, nch, rows, w = g2.shape
    tr = _row_tile(rows, 512)
    nr = rows // tr

    def body(c_ref, a_ref, b_ref, o_ref):
        o_ref[...] = (a_ref[...].astype(F32) + b_ref[...].astype(F32)).astype(o_ref.dtype)

    grid_spec = pltpu.PrefetchScalarGridSpec(
        num_scalar_prefetch=1, grid=(nch, nr),
        in_specs=[pl.BlockSpec((None, None, tr, w), lambda k, i, c_ref: (c_ref[0], k, i, 0)),
                  pl.BlockSpec((None, tr, w), lambda k, i, c_ref: (k, i, 0))],
        out_specs=pl.BlockSpec((None, tr, w), lambda k, i, c_ref: (k, i, 0)))
    return pl.pallas_call(
        body, name="add_own_half_" + tag, grid_spec=grid_spec, out_shape=SDS(other.shape, other.dtype),
        compiler_params=_cparams(("arbitrary", "arbitrary")))(jnp.reshape(c, (1,)).astype(jnp.int32), g2, other)


def _sum_chips(q, tag):
    nch, rows, w = q.shape
    tr = _row_tile(rows, 512)

    def fn(i, q_ref):
        return [((q_ref[0].astype(F32) + q_ref[1].astype(F32)) + q_ref[2].astype(F32)) + q_ref[3].astype(F32)]

    return _rw("sum_chips_" + tag, fn, rows // tr, [(q, pl.BlockSpec((nch, tr, w), lambda i: (0, i, 0)))],
               [(SDS((rows, w), F32), _rs(tr, w))])[0]


def _chip_copies(src_ref, dst_ref, ssem, rsem, outgoing):
    x, y, c, chips = _place()
    me = 2 * x + y
    cps = []
    for j, chip in enumerate(chips):
        ci = 2 * chip[0] + chip[1]
        cps.append(_remote(src_ref.at[ci], dst_ref.at[me if outgoing else ci], ssem.at[j], rsem.at[j], (*chip, c)))
    return cps, me


def _scatter_side(p):
    def first(ins, outs, scr):
        cps, me = _chip_copies(ins[0], outs[0], scr[0], scr[1], True)
        for cp in cps:
            cp.start()
        pltpu.make_async_copy(ins[0].at[me], outs[0].at[me], scr[2]).start()

    def last(ins, outs, scr):
        for cp in _chip_copies(ins[0], outs[0], scr[0], scr[1], False)[0]:
            cp.wait_recv()
        cps, me = _chip_copies(ins[0], outs[0], scr[0], scr[1], True)
        for cp in cps:
            cp.wait_send()
        pltpu.make_async_copy(ins[0].at[me], outs[0].at[me], scr[2]).wait()

    return _Side((p,), (SDS(p.shape, p.dtype),),
                 (pltpu.SemaphoreType.DMA((3,)), pltpu.SemaphoreType.DMA((3,)), pltpu.SemaphoreType.DMA(())),
                 first, None, last)


def _gather_copies(w_ref, out_ref, ssem, rsem):
    x, y, c, chips = _place()
    me = 2 * x + y
    sib = (x, y, 1 - c)
    sends, arrivals, forwards, from_sib = [], [], [], []
    for j, chip in enumerate(chips):
        ci = 2 * chip[0] + chip[1]
        sends.append(_remote(w_ref.at[c], out_ref.at[me, c], ssem.at[j], rsem.at[j], (*chip, c)))
        arrivals.append(_remote(w_ref.at[c], out_ref.at[ci, c], ssem.at[j], rsem.at[j], (*chip, c)))
        forwards.append(_remote(out_ref.at[ci, c], out_ref.at[ci, c], ssem.at[3 + j], rsem.at[3 + j], sib))
        from_sib.append(_remote(out_ref.at[ci, 1 - c], out_ref.at[ci, 1 - c], ssem.at[3 + j], rsem.at[3 + j], sib))
    return sends, arrivals, forwards, from_sib, me


def _gather_side(wp):
    def first(ins, outs, scr):
        sends, _, _, _, me = _gather_copies(ins[0], outs[0], scr[0], scr[1])
        for cp in sends:
            cp.start()
        pltpu.make_async_copy(ins[0], outs[0].at[me], scr[2]).start()

    def mid(ins, outs, scr):
        _, arrivals, forwards, _, _ = _gather_copies(ins[0], outs[0], scr[0], scr[1])
        for arrived, forward in zip(arrivals, forwards):
            arrived.wait_recv()
            forward.start()

    def last(ins, outs, scr):
        sends, _, forwards, from_sib, me = _gather_copies(ins[0], outs[0], scr[0], scr[1])
        for cp in from_sib:
            cp.wait_recv()
        for cp in sends + forwards:
            cp.wait_send()
        pltpu.make_async_copy(ins[0], outs[0].at[me], scr[2]).wait()

    return _Side((wp,), (SDS((N_CHIPS,) + wp.shape, wp.dtype),),
                 (pltpu.SemaphoreType.DMA((6,)), pltpu.SemaphoreType.DMA((6,)), pltpu.SemaphoreType.DMA(())),
                 first, mid, last)


def _allreduce_small(v, name):
    rows, w = v.shape
    offsets = [(dx, dy, dc) for dx in (0, 1) for dy in (0, 1) for dc in (0, 1)][1:]

    def body(v_ref, o_ref, buf, ssem, rsem):
        x, y, c, _ = _place()
        flip = lambda p, d: 1 - p if d else p
        peers = [(flip(x, dx), flip(y, dy), flip(c, dc)) for dx, dy, dc in offsets]
        index = lambda p: 4 * p[0] + 2 * p[1] + p[2]
        me = index((x, y, c))
        buf[me] = v_ref[...]
        sent = [_remote(v_ref, buf.at[me], ssem.at[q], rsem.at[q], p) for q, p in enumerate(peers)]
        for cp in sent:
            cp.start()
        for q, p in enumerate(peers):
            _remote(v_ref, buf.at[index(p)], ssem.at[q], rsem.at[q], p).wait_recv()
        for cp in sent:
            cp.wait_send()
        acc = buf[0]
        for q in range(1, 8):
            acc = acc + buf[q]
        o_ref[...] = acc

    vm = pl.BlockSpec(memory_space=pltpu.VMEM)
    return pl.pallas_call(
        body, name=name, in_specs=[vm], out_specs=vm, out_shape=SDS((rows, w), F32),
        scratch_shapes=[pltpu.VMEM((8, rows, w), F32), pltpu.SemaphoreType.DMA((7,)), pltpu.SemaphoreType.DMA((7,))],
        compiler_params=pltpu.CompilerParams(has_side_effects=True))(v)


def _join_halves(h, tag):
    def body(h_ref, out_ref, ssem, rsem, buf, isem, osem):
        x, y, c, _ = _place()
        cp = _remote(h_ref, out_ref.at[c], ssem, rsem, (x, y, 1 - c))
        cp.start()
        _copy_through_vmem(h_ref, out_ref.at[c], buf, isem, osem)
        _remote(h_ref, out_ref.at[1 - c], ssem, rsem, (x, y, 1 - c)).wait_recv()
        cp.wait_send()

    return pl.pallas_call(
        body, name="join_halves_" + tag, in_specs=[ANY], out_specs=ANY, out_shape=SDS((2,) + h.shape, h.dtype),
        scratch_shapes=[pltpu.SemaphoreType.DMA(()), pltpu.SemaphoreType.DMA(())]
        + _copy_scratch(h.shape[0], h.shape[1], h.dtype),
        compiler_params=pltpu.CompilerParams(has_side_effects=True))(h)


PACK_W = 1024
SHARDED = ("w_in", "w_ffn_gate", "w_ffn_up", "w_ssm_out", "w_att_out", "w_mix_out", "w_ffn_down")
COL_SHARDED = ("w_in", "w_ffn_gate", "w_ffn_up", "w_att_out")
SMALL = ("norm_mix", "b_gate", "conv_b", "dt_bias", "a_log", "d_skip", "ssm_norm", "norm_ffn", "norm_final")


PACK_ROW_ALIGN = 16


def _rows(n):
    return -(-n // (PACK_W * PACK_ROW_ALIGN)) * PACK_ROW_ALIGN


def _pack_rows(parts, total_rows):
    rows = []
    for p in parts:
        flat = p.reshape(-1)
        pad = _rows(flat.shape[0]) * PACK_W - flat.shape[0]
        if pad:
            flat = jnp.concatenate([flat, jnp.zeros((pad,), flat.dtype)])
        rows.append(flat.reshape(-1, PACK_W))
    used = sum(r.shape[0] for r in rows)
    if total_rows > used:
        rows.append(jnp.zeros((total_rows - used, PACK_W), rows[0].dtype))
    return jnp.concatenate(rows, axis=0)


def _padded_rows(n):
    return -(-n // 32) * 32


def _wire_name(name):
    return name + "_t" if name in COL_SHARDED else name


def _wire_shard(w, name):
    return w.T if name in COL_SHARDED else w


def _group_major(a, axis):
    gw = D_INNER // N_GROUPS
    take = lambda lo, n: lax.slice_in_dim(a, lo, lo + n, axis=axis)
    parts = []
    for g in range(N_GROUPS):
        parts += [take(g * gw, gw), take(D_INNER + g * D_STATE, D_STATE),
                  take(D_INNER + N_GROUPS * D_STATE + g * D_STATE, D_STATE)]
    return jnp.concatenate(parts, axis=axis)


def _group_major_inv(a, axis):
    gw = D_INNER // N_GROUPS
    take = lambda lo, n: lax.slice_in_dim(a, lo, lo + n, axis=axis)
    xs = [take(g * GROUP_W, gw) for g in range(N_GROUPS)]
    bs = [take(g * GROUP_W + gw, D_STATE) for g in range(N_GROUPS)]
    cs = [take(g * GROUP_W + gw + D_STATE, D_STATE) for g in range(N_GROUPS)]
    return jnp.concatenate(xs + bs + cs, axis=axis)


LATE = ("w_ffn_gate_t", "w_ffn_up_t", "w_ssm_out", "w_att_out_t", "w_mix_out", "w_ffn_down")


class _Overlap(NamedTuple):
    gather_side: _Side
    late_weights: Callable
    scatter_side: Callable
    scatter_in: Callable


def _local_step(x, target, wts, overlap):
    nb, seq, d = x.shape
    t = nb * seq
    x = x.reshape(t, d)
    target = target.reshape(t, d)
    hg = HEADS_PER_GROUP

    w_in_t = wts["w_in_t"]
    o1, o2, o3, o4 = D_INNER, D_INNER + CONV_DIM, D_INNER + CONV_DIM + N_HEADS, D_INNER + CONV_DIM + N_HEADS + QKV_DIM
    w_z = w_in_t[:o1]
    w_xbc = _group_major(w_in_t[o1:o2], 0)
    w_dt = jnp.pad(w_in_t[o2:o3], ((0, DT_PAD - N_HEADS), (0, 0)))
    w_qkv = w_in_t[o3:o4]
    w_gate = w_in_t[o4:]
    conv_w = _group_major(wts["conv_w"], 1)
    conv_b = _group_major(wts["conv_b"], 1)

    def per_group_row(p):
        return p.reshape(N_GROUPS, 1, hg)

    def per_group_col(p):
        return p.reshape(N_GROUPS, hg, 1)

    a_neg = -jnp.exp(wts["a_log"])
    bias_r, bias_c = per_group_row(wts["dt_bias"]), per_group_col(wts["dt_bias"])
    a_r, a_c = per_group_row(a_neg), per_group_col(a_neg)
    dskip_r = per_group_row(wts["d_skip"])
    cos, sin = _rope_tables(seq)

    h = _rms_fwd(x, wts["norm_mix"], "rms_mix_fwd")
    z = _mm(h, w_z, "nt", BF16, "proj_z")
    xbc = _mm(h, w_xbc, "nt", F32, "proj_xbc")
    dt_raw = _mm(h, w_dt, "nt", F32, "proj_dt")
    qkv = _mm(h, w_qkv, "nt", BF16, "proj_qkv")
    gate_logits = _mm(h, w_gate, "nt", BF16, "proj_gate")

    xc = _conv_fwd(xbc, conv_w, conv_b, seq)
    dtr = dt_raw[:, :N_HEADS].reshape(t, N_GROUPS, hg).transpose(1, 0, 2)
    dtrt = dt_raw[:, :N_HEADS].reshape(nb, seq, N_GROUPS, hg).transpose(2, 0, 3, 1)
    y, states, *gathered = _ssd_fwd(xc, dtr, dtrt, bias_r, bias_c, a_r, a_c, dskip_r, nb, seq, overlap.gather_side)
    wts = {**wts, **overlap.late_weights(gathered)}
    yn = _gate_norm_fwd(y, z, wts["ssm_norm"])
    y_ssm = _mm(yn, wts["w_ssm_out"], "nn", BF16, "ssm_out")

    groups = range(len(ATT_DILATIONS))
    qg, kg, vg = _rope_fwd(qkv, cos, sin, nb, seq)
    o_g, lse_g = zip(*[_att_fwd(qg[i], kg[i], vg[i], i, seq) for i in groups])
    att = _merge_fwd(o_g, lse_g, nb, seq)
    y_att = _mm(att, wts["w_att_out_t"], "nt", BF16, "att_out")

    mixed = _mix_fwd(gate_logits, wts["b_gate"], y_ssm, y_att)
    x1 = _mm(mixed, wts["w_mix_out"], "nn", F32, "mix_out", add=x)
    h2 = _rms_fwd(x1, wts["norm_ffn"], "rms_ffn_fwd")
    gt = _mm(h2, wts["w_ffn_gate_t"], "nt", BF16, "ffn_gate")
    up = _mm(h2, wts["w_ffn_up_t"], "nt", BF16, "ffn_up")
    act = _swiglu_fwd(gt, up)
    x2 = _mm(act, wts["w_ffn_down"], "nn", F32, "ffn_down", add=x1)

    g = {}
    dx2, dx2_b, g["norm_final"], loss = _final_fwd_bwd(x2, target, wts["norm_final"].reshape(1, d))
    dact = _mm(dx2_b, wts["w_ffn_down"], "nt", BF16, "d_act")
    g["w_ffn_down"] = _mm(act, dx2_b, "tn", BF16, "g_ffn_down")
    dgt, dup = _swiglu_bwd(gt, up, dact)
    g["w_ffn_gate_t"] = _mm(dgt, h2, "tn", BF16, "g_ffn_gate")
    g["w_ffn_up_t"] = _mm(dup, h2, "tn", BF16, "g_ffn_up")
    dh2 = _mm(dgt, wts["w_ffn_gate_t"], "nn", F32, "d_h2_gate")
    dh2 = _mm(dup, wts["w_ffn_up_t"], "nn", F32, "d_h2_up", add=dh2)
    dx1, dx1_b, g["norm_ffn"] = _rms_bwd(x1, dh2, wts["norm_ffn"], dx2, "rms_ffn_bwd")

    dmixed = _mm(dx1_b, wts["w_mix_out"], "nt", F32, "d_mixed")
    g["w_mix_out"] = _mm(mixed, dx1_b, "tn", BF16, "g_mix_out")
    dy_ssm, dy_att, dgate, g["b_gate"] = _mix_bwd(gate_logits, wts["b_gate"], y_ssm, y_att, dmixed)

    datt = _mm(dy_att, wts["w_att_out_t"], "nn", F32, "d_att")
    g["w_att_out_t"] = _mm(dy_att, att, "tn", BF16, "g_att_out")
    do_g, dlt_g = _merge_bwd(o_g, lse_g, datt, nb, seq)
    dq_g, dk_g, dv_g = zip(*[_att_bwd(qg[i], kg[i], vg[i], do_g[i], lse_g[i], dlt_g[i], i, seq) for i in groups])
    dqkv = _rope_bwd(dq_g, dk_g, dv_g, cos, sin, nb, seq)

    dyn = _mm(dy_ssm, wts["w_ssm_out"], "nt", BF16, "d_yn")
    g["w_ssm_out"] = _mm(yn, dy_ssm, "tn", BF16, "g_ssm_out")
    dy, dz, g["ssm_norm"] = _gate_norm_bwd(y, z, wts["ssm_norm"], dyn)
    side = overlap.scatter_side({n: g.pop(n) for n in LATE})
    dxc, ddtr, g_bias, g_alog, g_dskip, *scattered = _ssd_bwd(xc, dtr, dtrt, bias_r, bias_c, a_r, a_c, dskip_r,
                                                               states, dy, nb, seq, side)
    g["dt_bias"] = g_bias.reshape(1, N_HEADS)
    g["a_log"] = g_alog.reshape(1, N_HEADS)
    g["d_skip"] = g_dskip.reshape(1, N_HEADS)
    dpre, g_conv_w, g_conv_b = _conv_bwd_pre(xbc, conv_w, conv_b, dxc, seq)
    g["conv_w"] = _group_major_inv(g_conv_w, 1)
    g["conv_b"] = _group_major_inv(g_conv_b, 1)
    dxbc = _conv_bwd_in(dpre, conv_w, seq)
    ddt = jnp.pad(ddtr.transpose(1, 0, 2).reshape(t, N_HEADS), ((0, 0), (0, DT_PAD - N_HEADS))).astype(BF16)

    g_in_t = jnp.concatenate([
        _mm(dz, h, "tn", BF16, "g_in_z"),
        _group_major_inv(_mm(dxbc, h, "tn", BF16, "g_in_xbc"), 0),
        _mm(ddt, h, "tn", BF16, "g_in_dt")[:N_HEADS],
        _mm(dqkv, h, "tn", BF16, "g_in_qkv"),
        _mm(dgate, h, "tn", BF16, "g_in_gate")], axis=0)
    dh = _mm(dz, w_z, "nn", F32, "d_h_z")
    dh = _mm(dxbc, w_xbc, "nn", F32, "d_h_xbc", add=dh)
    dh = _mm(ddt, w_dt, "nn", F32, "d_h_dt", add=dh)
    dh = _mm(dgate, w_gate, "nn", F32, "d_h_gate", add=dh)
    dh, *scattered_in = _mm(dqkv, w_qkv, "nn", F32, "d_h_qkv", add=dh, side=overlap.scatter_in({"w_in_t": g_in_t}))
    dx, _, g["norm_mix"] = _rms_bwd(x, dh, wts["norm_mix"], dx1, "rms_mix_bwd")
    return loss[0, 0], dx.reshape(nb, seq, d), g, scattered, scattered_in


def kernel(x, norm_mix, w_in, b_gate, conv_w, conv_b, dt_bias, a_log, d_skip, ssm_norm, w_ssm_out, w_att_out, w_mix_out, norm_ffn, w_ffn_gate, w_ffn_up, w_ffn_down, norm_final, loss_target, m_norm_mix, m_w_in, m_b_gate, m_conv_w, m_conv_b, m_dt_bias, m_a_log, m_d_skip, m_ssm_norm, m_w_ssm_out, m_w_att_out, m_w_mix_out, m_norm_ffn, m_w_ffn_gate, m_w_ffn_up, m_w_ffn_down, m_norm_final, v_norm_mix, v_w_in, v_b_gate, v_conv_w, v_conv_b, v_dt_bias, v_a_log, v_d_skip, v_ssm_norm, v_w_ssm_out, v_w_att_out, v_w_mix_out, v_norm_ffn, v_w_ffn_gate, v_w_ffn_up, v_w_ffn_down, v_norm_final):
    names = ("norm_mix", "w_in", "b_gate", "conv_w", "conv_b", "dt_bias", "a_log", "d_skip", "ssm_norm", "w_ssm_out",
             "w_att_out", "w_mix_out", "norm_ffn", "w_ffn_gate", "w_ffn_up", "w_ffn_down", "norm_final")
    w_loc = dict(zip(names, (norm_mix, w_in, b_gate, conv_w, conv_b, dt_bias, a_log, d_skip, ssm_norm, w_ssm_out,
                             w_att_out, w_mix_out, norm_ffn, w_ffn_gate, w_ffn_up, w_ffn_down, norm_final)))
    m_loc = dict(zip(names, (m_norm_mix, m_w_in, m_b_gate, m_conv_w, m_conv_b, m_dt_bias, m_a_log, m_d_skip,
                             m_ssm_norm, m_w_ssm_out, m_w_att_out, m_w_mix_out, m_norm_ffn, m_w_ffn_gate,
                             m_w_ffn_up, m_w_ffn_down, m_norm_final)))
    v_loc = dict(zip(names, (v_norm_mix, v_w_in, v_b_gate, v_conv_w, v_conv_b, v_dt_bias, v_a_log, v_d_skip,
                             v_ssm_norm, v_w_ssm_out, v_w_att_out, v_w_mix_out, v_norm_ffn, v_w_ffn_gate,
                             v_w_ffn_up, v_w_ffn_down, v_norm_final)))
    two_d = lambda a: a.reshape(a.shape[-2:]) if a.ndim >= 2 else a.reshape(1, -1)
    w2 = {n: two_d(a) for n, a in w_loc.items()}
    chip = 2 * lax.axis_index("x") + lax.axis_index("y")
    c = lax.axis_index("c")

    wire_shapes = {n: _wire_shard(w2[n], n).shape for n in SHARDED}
    true_rows = {n: wire_shapes[n][0] * wire_shapes[n][1] // PACK_W for n in SHARDED}
    seg_rows = {n: _rows(wire_shapes[n][0] * wire_shapes[n][1]) for n in SHARDED}
    buckets = {"first": ("w_in",), "late": tuple(n for n in SHARDED if n != "w_in")}
    rows_of = {b: _padded_rows(sum(seg_rows[n] for n in ns)) for b, ns in buckets.items()}

    def pack_shards(b):
        packed = _pack_rows([_wire_shard(w2[n], n).astype(BF16) for n in buckets[b]], rows_of[b])
        return packed.reshape(2, rows_of[b] // 2, PACK_W)

    def unpack_full(gathered, b):
        wg, out, off = gathered.reshape(N_CHIPS, rows_of[b], PACK_W), {}, 0
        for n in buckets[b]:
            rows, cols = wire_shapes[n]
            out[_wire_name(n)] = wg[:, off:off + true_rows[n]].reshape(N_CHIPS * rows, cols)
            off += seg_rows[n]
        return out

    def pack_grads(g, b):
        sections = [_pack_rows([g[_wire_name(n)].reshape(N_CHIPS, true_rows[n], PACK_W)[k] for n in buckets[b]],
                               rows_of[b]) for k in range(N_CHIPS)]
        return jnp.stack(sections).reshape(N_CHIPS, 2, rows_of[b] // 2, PACK_W).transpose(1, 0, 2, 3)

    def chip_sums(g, b):
        g2 = pack_grads(g, b)
        return _add_own_half(g2, _swap_halves(g2, b), c, b)

    def finish(by_source, b):
        reduced = _join_halves(_sum_chips(by_source, b), b).reshape(rows_of[b], PACK_W)
        out, off = {}, 0
        for n in buckets[b]:
            wire = reduced[off:off + true_rows[n]].reshape(wire_shapes[n])
            out[n] = wire.T if n in COL_SHARDED else wire
            off += seg_rows[n]
        return out

    full = unpack_full(_gather_weights(pack_shards("first")), "first")
    for n in SMALL:
        full[n] = w2[n]
    overlap = _Overlap(_gather_side(pack_shards("late")), lambda outs: unpack_full(outs[0], "late"),
                       lambda g: _scatter_side(chip_sums(g, "late")), lambda g: _scatter_side(chip_sums(g, "first")))

    n_conv = w2["conv_w"].shape[1]
    placed = lax.dynamic_update_slice_in_dim(jnp.zeros((CONV_K, N_CHIPS * n_conv), F32), w2["conv_w"], chip * n_conv, 1)
    placed = jnp.where(c == 0, placed, 0.0)
    full["conv_w"] = _allreduce_small(_pack_rows([placed], _rows(int(placed.size))), "gather_conv_w").reshape(
        -1)[:placed.size].reshape(placed.shape)

    loss_sum, grad_x, g_full, scattered, scattered_in = _local_step(x, loss_target, full, overlap)
    loss = lax.psum(loss_sum, ("x", "y", "c"))

    g_shard = {}
    small_names = SMALL + ("conv_w",)
    small_flat = jnp.concatenate([g_full[n].reshape(-1) for n in small_names])
    small = _allreduce_small(_pack_rows([small_flat], _rows(int(small_flat.size))), "allreduce_small").reshape(-1)
    off = 0
    for n in small_names:
        size = int(g_full[n].size)
        g_shard[n] = small[off:off + size].reshape(g_full[n].shape)
        off += size
    g_shard["conv_w"] = lax.dynamic_slice_in_dim(g_shard["conv_w"], chip * n_conv, n_conv, 1)

    g_shard.update(finish(scattered[0], "late"))
    g_shard.update(finish(scattered_in[0], "first"))

    grads, deltas, new_m, new_v = [], [], [], []
    for n in names:
        shape = w_loc[n].shape
        as_rows = (lambda a: a) if len(shape) >= 2 else two_d
        gn = g_shard[n].reshape(as_rows(w_loc[n]).shape)
        d_, m_, v_ = _adamw(as_rows(w_loc[n]), gn, as_rows(m_loc[n]), as_rows(v_loc[n]), "adamw_" + n)
        grads.append(gn.reshape(shape))
        deltas.append(d_.reshape(shape))
        new_m.append(m_.reshape(shape))
        new_v.append(v_.reshape(shape))
    return (loss, grad_x, *grads, *deltas, *new_m, *new_v)
```

```python
import functools
from typing import Callable, NamedTuple, Optional

import jax
import jax.numpy as jnp
from jax import lax
from jax.experimental import pallas as pl
from jax.experimental.pallas import tpu as pltpu

F32 = jnp.float32
BF16 = jnp.bfloat16
SDS = jax.ShapeDtypeStruct
MESH = pl.DeviceIdType.MESH

D_MODEL = 1024
D_INNER = 2048
N_HEADS = 32
HEAD_P = 64
N_GROUPS = 4
HEADS_PER_GROUP = N_HEADS // N_GROUPS
D_STATE = 128
CONV_K = 4
CHUNK = 128
CONV_DIM = D_INNER + 2 * N_GROUPS * D_STATE
GROUP_W = D_INNER // N_GROUPS + 2 * D_STATE
ATT_HEADS = 12
ATT_D = 128
ATT_SLOTS = 4
ATT_W = ATT_SLOTS * ATT_D
ATT_DILATIONS = (1, 4, 16)
ATT_BLOCK = 128
QKV_DIM = 3 * ATT_HEADS * ATT_D
D_FF = 2816
DT_PAD = 128
ROPE_THETA = 10000.0
EPS = 1e-6
N_CHIPS = 4
LANES = 128

ADAM_LR = 0.001
ADAM_B1 = 0.9
ADAM_B2 = 0.999
ADAM_EPS = 1e-08
ADAM_WD = 0.01
ADAM_STEP = 10

VMEM_LIMIT = 48 * 1024 * 1024


def _cparams(semantics):
    return pltpu.CompilerParams(dimension_semantics=semantics, vmem_limit_bytes=VMEM_LIMIT)


def _pick(n, cap):
    best = None
    for t in range(LANES, min(n, cap) + 1, LANES):
        if n % t == 0:
            best = t
    return best or n


def _row_tile(rows, cap):
    best = None
    for t in range(8, min(rows, cap) + 1, 8):
        if rows % t == 0:
            best = t
    return best or rows


def _sigmoid(x):
    return 1.0 / (1.0 + jnp.exp(-x))


def _softplus(x):
    return jnp.maximum(x, 0.0) + jnp.log(1.0 + jnp.exp(-jnp.abs(x)))


def _dot(a, b):
    return jnp.dot(a, b, preferred_element_type=F32)


def _dot_nt(a, b):
    return lax.dot_general(a, b, (((1,), (1,)), ((), ())), preferred_element_type=F32)


def _dot_tn(a, b):
    return lax.dot_general(a, b, (((0,), (0,)), ((), ())), preferred_element_type=F32)


def _mm(a, b, mode, out_dtype, name, add=None, side=None):
    if mode == "nn":
        (m, k), (_, n) = a.shape, b.shape
    elif mode == "nt":
        (m, k), (n, _) = a.shape, b.shape
    else:
        (k, m), (_, n) = a.shape, b.shape
    tm, tn = _pick(m, 1536), _pick(n, 2048)
    tk = k if k <= 2048 else _pick(k, 2048)
    nk = k // tk
    dims = {"nn": ((1,), (0,)), "nt": ((1,), (1,)), "tn": ((0,), (0,))}[mode]

    def partial_product(a_ref, b_ref):
        return lax.dot_general(a_ref[...].astype(BF16), b_ref[...].astype(BF16), (dims, ((), ())),
                               preferred_element_type=F32)

    def body(*refs):
        a_ref, b_ref = refs[:2]
        c_ref = refs[2] if add is not None else None
        o_ref = refs[3] if add is not None else refs[2]

        def finish(r):
            if add is not None:
                r = r + c_ref[...].astype(F32)
            o_ref[...] = r.astype(out_dtype)

        if nk == 1:
            finish(partial_product(a_ref, b_ref))
            return
        acc = refs[-1]
        kk = pl.program_id(2)

        @pl.when(kk == 0)
        def _():
            acc[...] = partial_product(a_ref, b_ref)

        @pl.when((kk > 0) & (kk < nk - 1))
        def _():
            acc[...] += partial_product(a_ref, b_ref)

        @pl.when(kk == nk - 1)
        def _():
            finish(acc[...] + partial_product(a_ref, b_ref))

    a_spec = {"nn": pl.BlockSpec((tm, tk), lambda j, i, q: (i, q)),
              "nt": pl.BlockSpec((tm, tk), lambda j, i, q: (i, q)),
              "tn": pl.BlockSpec((tk, tm), lambda j, i, q: (q, i))}[mode]
    b_spec = {"nn": pl.BlockSpec((tk, tn), lambda j, i, q: (q, j)),
              "nt": pl.BlockSpec((tn, tk), lambda j, i, q: (j, q)),
              "tn": pl.BlockSpec((tk, tn), lambda j, i, q: (q, j))}[mode]
    o_spec = pl.BlockSpec((tm, tn), lambda j, i, q: (i, j))
    ins, specs = [a, b], [a_spec, b_spec]
    if add is not None:
        ins.append(add)
        specs.append(o_spec)
    acc = [pltpu.VMEM((tm, tn), F32)] if nk > 1 else []
    grid = (n // tn, m // tm, nk)
    if side is None:
        return pl.pallas_call(
            body, name=name, grid=grid, in_specs=specs, out_specs=o_spec, out_shape=SDS((m, n), out_dtype),
            scratch_shapes=acc, compiler_params=_cparams(("parallel", "parallel", "arbitrary")))(*ins)
    return pl.pallas_call(
        _attach_side(body, len(ins), 1, side, grid), name=name, grid=grid,
        in_specs=specs + [ANY] * len(side.ins), out_specs=[o_spec] + [ANY] * len(side.out_shapes),
        out_shape=[SDS((m, n), out_dtype)] + list(side.out_shapes), scratch_shapes=acc + list(side.scratch),
        compiler_params=_cparams(("arbitrary", "arbitrary", "arbitrary")))(*ins, *side.ins)


def _rw(name, fn, nsteps, ins, outs, n_acc=0):
    n_in, n_out = len(ins), len(outs)

    def body(*refs):
        i = pl.program_id(0)
        vals = fn(i, *refs[:n_in])
        for q, (r, v) in enumerate(zip(refs[n_in:], vals)):
            if q < n_out - n_acc:
                r[...] = v.astype(r.dtype)
            else:
                @pl.when(i == 0)
                def _(r=r):
                    r[...] = jnp.zeros_like(r)

                r[...] += v

    return pl.pallas_call(
        body, name=name, grid=(nsteps,), in_specs=[s for _, s in ins], out_specs=[s for _, s in outs],
        out_shape=[o for o, _ in outs], compiler_params=_cparams(("arbitrary",)))(*[a for a, _ in ins])


def _rs(tm, w, cb=0):
    return pl.BlockSpec((tm, w), lambda i: (i, cb))


def _fs(shape):
    nd = len(shape)
    return pl.BlockSpec(shape, lambda i: (0,) * nd)


def _colsum(v):
    return jnp.sum(v, axis=0, keepdims=True)


def _rms_fwd(x, g, name):
    t, d = x.shape
    tm = 512

    def fn(i, x_ref, g_ref):
        xv = x_ref[...]
        r = lax.rsqrt(jnp.mean(xv * xv, axis=-1, keepdims=True) + EPS)
        return [xv * r * g_ref[...]]

    return _rw(name, fn, t // tm, [(x, _rs(tm, d)), (g, _fs((1, d)))], [(SDS((t, d), BF16), _rs(tm, d))])[0]


def _rms_bwd(x, dh, g, dres, name):
    t, d = x.shape
    tm = 512

    def fn(i, x_ref, dh_ref, g_ref, dres_ref):
        xv = x_ref[...]
        r = lax.rsqrt(jnp.mean(xv * xv, axis=-1, keepdims=True) + EPS)
        xhat = xv * r
        dhv = dh_ref[...]
        dxhat = dhv * g_ref[...]
        dx = dres_ref[...] + r * (dxhat - xhat * jnp.mean(dxhat * xhat, axis=-1, keepdims=True))
        return [dx, dx, _colsum(dhv * xhat)]

    return _rw(name, fn, t // tm,
               [(x, _rs(tm, d)), (dh, _rs(tm, d)), (g, _fs((1, d))), (dres, _rs(tm, d))],
               [(SDS((t, d), F32), _rs(tm, d)), (SDS((t, d), BF16), _rs(tm, d)), (SDS((1, d), F32), _fs((1, d)))],
               n_acc=1)


def _final_fwd_bwd(x2, target, g):
    t, d = x2.shape
    tm = 512

    def fn(i, x_ref, t_ref, g_ref):
        xv = x_ref[...]
        gv = g_ref[...]
        r = lax.rsqrt(jnp.mean(xv * xv, axis=-1, keepdims=True) + EPS)
        xhat = xv * r
        diff = xhat * gv - t_ref[...]
        lsum = 0.5 * jnp.sum(jnp.sum(diff * diff, axis=-1, keepdims=True) * (1.0 / d), axis=0, keepdims=True)
        dy = diff * (1.0 / d)
        dxhat = dy * gv
        dx = r * (dxhat - xhat * jnp.mean(dxhat * xhat, axis=-1, keepdims=True))
        return [dx, dx, _colsum(dy * xhat), lsum]

    return _rw("final_norm_loss", fn, t // tm,
               [(x2, _rs(tm, d)), (target, _rs(tm, d)), (g, _fs((1, d)))],
               [(SDS((t, d), F32), _rs(tm, d)), (SDS((t, d), BF16), _rs(tm, d)), (SDS((1, d), F32), _fs((1, d))),
                (SDS((1, 1), F32), _fs((1, 1)))], n_acc=2)


CONV_TS = 512
CONV_HALO = 8


def _conv_specs(seq, c):
    ts, tc = CONV_TS, GROUP_W
    hb = ts // CONV_HALO
    u_spec = pl.BlockSpec((ts, tc), lambda j, i: (i, j))
    prev_spec = pl.BlockSpec((CONV_HALO, tc), lambda j, i: (jnp.maximum(i * hb - 1, 0), j))
    w_spec = pl.BlockSpec((CONV_K, tc), lambda j, i: (0, j))
    b_spec = pl.BlockSpec((1, tc), lambda j, i: (0, j))
    return u_spec, prev_spec, w_spec, b_spec


CONV_ROWS = 16


def _conv_pre(i, seq, u_ref, prev_ref, w_ref, b_ref, ext):
    ts = CONV_TS
    first = (i % (seq // ts)) == 0
    ext[0:CONV_HALO, :] = jnp.where(first, 0.0, prev_ref[...])
    ext[CONV_HALO:, :] = u_ref[...]
    acc = jnp.broadcast_to(b_ref[...], u_ref.shape)
    for q in range(CONV_K):
        acc = acc + w_ref[q:q + 1, :] * ext[pl.ds(CONV_HALO - CONV_K + 1 + q, ts), :]
    return acc


def _conv_fwd(u, w, b, seq):
    t, c = u.shape
    ts, tc = CONV_TS, GROUP_W
    u_spec, prev_spec, w_spec, b_spec = _conv_specs(seq, c)

    def body(u_ref, prev_ref, w_ref, b_ref, o_ref, ext):
        pre = _conv_pre(pl.program_id(1), seq, u_ref, prev_ref, w_ref, b_ref, ext)
        o_ref[...] = pre * _sigmoid(pre)

    return pl.pallas_call(
        body, name="conv_fwd", grid=(c // tc, t // ts), in_specs=[u_spec, prev_spec, w_spec, b_spec],
        out_specs=u_spec, out_shape=SDS((t, c), F32), scratch_shapes=[pltpu.VMEM((ts + CONV_HALO, tc), F32)],
        compiler_params=_cparams(("parallel", "arbitrary")))(u, u, w, b)


def _conv_bwd_pre(u, w, b, dxc, seq):
    t, c = u.shape
    ts, tc = CONV_TS, GROUP_W
    u_spec, prev_spec, w_spec, b_spec = _conv_specs(seq, c)

    def body(u_ref, prev_ref, w_ref, b_ref, d_ref, dpre_ref, dw_ref, db_ref, ext):
        i = pl.program_id(1)
        pre = _conv_pre(i, seq, u_ref, prev_ref, w_ref, b_ref, ext)
        sg = _sigmoid(pre)
        dpre = d_ref[...] * sg * (1.0 + pre * (1.0 - sg))
        dpre_ref[...] = dpre

        @pl.when(i == 0)
        def _():
            dw_ref[...] = jnp.zeros_like(dw_ref)
            db_ref[...] = jnp.zeros_like(db_ref)

        db_ref[...] += _colsum(dpre)
        for q in range(CONV_K):
            dw_ref[q:q + 1, :] += _colsum(dpre * ext[pl.ds(CONV_HALO - CONV_K + 1 + q, ts), :])

    return pl.pallas_call(
        body, name="conv_bwd_pre", grid=(c // tc, t // ts),
        in_specs=[u_spec, prev_spec, w_spec, b_spec, u_spec], out_specs=[u_spec, w_spec, b_spec],
        out_shape=[SDS((t, c), F32), SDS((CONV_K, c), F32), SDS((1, c), F32)],
        scratch_shapes=[pltpu.VMEM((ts + CONV_HALO, tc), F32)],
        compiler_params=_cparams(("parallel", "arbitrary")))(u, u, w, b, dxc)


def _conv_bwd_in(dpre, w, seq):
    t, c = dpre.shape
    ts, tc = CONV_TS, GROUP_W
    hb = ts // CONV_HALO
    last = t // CONV_HALO - 1
    d_spec = pl.BlockSpec((ts, tc), lambda j, i: (i, j))
    next_spec = pl.BlockSpec((CONV_HALO, tc), lambda j, i: (jnp.minimum((i + 1) * hb, last), j))
    w_spec = pl.BlockSpec((CONV_K, tc), lambda j, i: (0, j))

    def body(d_ref, next_ref, w_ref, o_ref, ext):
        i = pl.program_id(1)
        nts = seq // ts
        is_last = (i % nts) == nts - 1
        ext[0:ts, :] = d_ref[...]
        ext[ts:, :] = jnp.where(is_last, 0.0, next_ref[...])
        wv = w_ref[...]

        def rows(j, carry):
            r0 = pl.multiple_of(j * CONV_ROWS, CONV_ROWS)
            blk = ext[pl.ds(r0, CONV_ROWS + CONV_HALO), :]
            acc = wv[CONV_K - 1:CONV_K] * blk[0:CONV_ROWS]
            for q in range(CONV_K - 1):
                acc = acc + wv[q:q + 1] * blk[CONV_K - 1 - q:CONV_K - 1 - q + CONV_ROWS]
            o_ref[pl.ds(r0, CONV_ROWS), :] = acc.astype(o_ref.dtype)
            return carry

        lax.fori_loop(0, ts // CONV_ROWS, rows, 0)

    return pl.pallas_call(
        body, name="conv_bwd_in", grid=(c // tc, t // ts), in_specs=[d_spec, next_spec, w_spec],
        out_specs=d_spec, out_shape=SDS((t, c), BF16), scratch_shapes=[pltpu.VMEM((ts + CONV_HALO, tc), F32)],
        compiler_params=_cparams(("parallel", "arbitrary")))(dpre, dpre, w)


def _split3(v):
    hi = v.astype(BF16)
    r1 = v - hi.astype(F32)
    mid = r1.astype(BF16)
    lo = (r1 - mid.astype(F32)).astype(BF16)
    return hi, mid, lo


def _ssd_prelude(dtr_ref, dtrt_ref, bias_ref, biast_ref, a_ref, at_ref):
    dt = _softplus(dtr_ref[...] + bias_ref[...])
    dtt = _softplus(dtrt_ref[...] + biast_ref[...])
    ri = lax.broadcasted_iota(jnp.int32, (CHUNK, CHUNK), 0)
    ci = lax.broadcasted_iota(jnp.int32, (CHUNK, CHUNK), 1)
    lower = ri >= ci
    upper = ri <= ci
    lower_b = jnp.where(lower, 1.0, 0.0).astype(BF16)
    upper_b = jnp.where(upper, 1.0, 0.0).astype(BF16)
    acs = sum(_dot(lower_b, p) for p in _split3(dt * a_ref[...]))
    acst = sum(_dot(p, upper_b) for p in _split3(dtt * at_ref[...]))
    return dt, acs, acst, lower, upper, lower_b, upper_b


def _ssd_specs(seq):
    nc = seq // CHUNK
    hg = HEADS_PER_GROUP
    row = lambda cc: (lambda g, b, c: (b * nc + cc(c), g))
    fwd = lambda c: c
    rev = lambda c: nc - 1 - c

    def specs(cc):
        return dict(
            xc=pl.BlockSpec((CHUNK, GROUP_W), lambda g, b, c: (b * nc + cc(c), g)),
            y=pl.BlockSpec((CHUNK, D_INNER // N_GROUPS), lambda g, b, c: (b * nc + cc(c), g)),
            dtr=pl.BlockSpec((None, CHUNK, hg), lambda g, b, c: (g, b * nc + cc(c), 0)),
            dtrt=pl.BlockSpec((None, None, hg, CHUNK), lambda g, b, c: (g, b, 0, cc(c))),
            prow=pl.BlockSpec((None, 1, hg), lambda g, b, c: (g, 0, 0)),
            pcol=pl.BlockSpec((None, hg, 1), lambda g, b, c: (g, 0, 0)),
            st=pl.BlockSpec((None, None, None, D_STATE, hg * HEAD_P), lambda g, b, c: (g, b, cc(c), 0, 0)),
        )

    return specs(fwd), specs(rev)


def _head_maps():
    hw = HEADS_PER_GROUP * HEAD_P
    shift = HEAD_P.bit_length() - 1
    hj = lax.broadcasted_iota(jnp.int32, (HEADS_PER_GROUP, hw), 0)
    lq = jnp.right_shift(lax.broadcasted_iota(jnp.int32, (HEADS_PER_GROUP, hw), 1), shift)
    spread = jnp.where(hj == lq, 1.0, 0.0).astype(BF16)
    rq = jnp.right_shift(lax.broadcasted_iota(jnp.int32, (hw, LANES), 0), shift)
    cj = lax.broadcasted_iota(jnp.int32, (hw, LANES), 1)
    gather = jnp.where(rq == cj, 1.0, 0.0).astype(BF16)
    return spread, gather


def _dot01(v, m01):
    hi, mid, _ = _split3(v)
    return _dot(hi, m01) + _dot(mid, m01)


class _Side(NamedTuple):
    ins: tuple
    out_shapes: tuple
    scratch: tuple
    first: Callable
    mid: Optional[Callable]
    last: Callable


NO_SIDE = _Side((), (), (), lambda *refs: None, None, lambda *refs: None)


def _attach_side(body, n_in, n_out, side, grid):
    si, so, ss = len(side.ins), len(side.out_shapes), len(side.scratch)

    def wrapped(*refs):
        ins, s_in = refs[:n_in], refs[n_in:n_in + si]
        outs = refs[n_in + si:n_in + si + n_out]
        s_out = refs[n_in + si + n_out:n_in + si + n_out + so]
        rest = refs[n_in + si + n_out + so:]
        scr, s_scr = rest[:len(rest) - ss], rest[len(rest) - ss:]
        ids = [pl.program_id(a) for a in range(len(grid))]
        inner_first = functools.reduce(lambda p, q: p & q, [i == 0 for i in ids[1:]], ids[0] >= 0)
        at_last = functools.reduce(lambda p, q: p & q, [i == n - 1 for i, n in zip(ids, grid)])

        @pl.when((ids[0] == 0) & inner_first)
        def _():
            side.first(s_in, s_out, s_scr)

        if side.mid is not None:
            @pl.when((ids[0] == grid[0] - 1) & inner_first)
            def _():
                side.mid(s_in, s_out, s_scr)

        body(*ins, *outs, *scr)

        @pl.when(at_last)
        def _():
            side.last(s_in, s_out, s_scr)

    return wrapped


def _ssd_fwd(xc, dtr, dtrt, bias, biast, a, at, dskip, nb, seq, side):
    t = xc.shape[0]
    nc = seq // CHUNK
    hg = HEADS_PER_GROUP
    hw = hg * HEAD_P
    sp, _ = _ssd_specs(seq)

    def body(xc_ref, dtr_ref, dtrt_ref, bias_ref, biast_ref, a_ref, at_ref, d_ref, y_ref, sin_ref, st):
        @pl.when(pl.program_id(2) == 0)
        def _():
            st[...] = jnp.zeros_like(st)

        s_in = st[...]
        sin_ref[...] = s_in
        dt, acs, acst, lower, _, _, _ = _ssd_prelude(dtr_ref, dtrt_ref, bias_ref, biast_ref, a_ref, at_ref)
        spread, _ = _head_maps()
        x = xc_ref[...]
        xs = x[:, :hw]
        b16 = x[:, hw:hw + D_STATE].astype(BF16)
        c16 = x[:, hw + D_STATE:].astype(BF16)
        cb = _dot_nt(c16, b16)
        last = acs[CHUNK - 1:CHUNK, :]
        e_x = _dot01(jnp.exp(acs), spread)
        dec_x = _dot01(jnp.exp(last - acs), spread)
        tot_x = e_x[CHUNK - 1:CHUNK, :]
        d_x = _dot01(jnp.broadcast_to(d_ref[...], (8, hg)), spread)[0:1, :]
        xdtf = xs * _dot01(dt, spread)
        xdt16 = xdtf.astype(BF16)
        yoff = e_x * _dot(c16, s_in.astype(BF16))
        st[...] = tot_x * s_in + _dot_tn(b16, (dec_x * xdtf).astype(BF16))
        parts = []
        for j in range(hg):
            decay = jnp.exp(jnp.where(lower, acs[:, j:j + 1] - acst[j:j + 1, :], -jnp.inf))
            parts.append(_dot((cb * decay).astype(BF16), xdt16[:, HEAD_P * j:HEAD_P * (j + 1)]))
        y_ref[...] = jnp.concatenate(parts, axis=-1) + yoff + d_x * xs

    return pl.pallas_call(
        _attach_side(body, 8, 2, side, (N_GROUPS, nb, nc)), name="ssd_fwd", grid=(N_GROUPS, nb, nc),
        in_specs=[sp["xc"], sp["dtr"], sp["dtrt"], sp["prow"], sp["pcol"], sp["prow"], sp["pcol"], sp["prow"]]
        + [ANY] * len(side.ins),
        out_specs=[sp["y"], sp["st"]] + [ANY] * len(side.out_shapes),
        out_shape=[SDS((t, D_INNER), F32), SDS((N_GROUPS, nb, nc, D_STATE, hw), F32)] + list(side.out_shapes),
        scratch_shapes=[pltpu.VMEM((D_STATE, hw), F32)] + list(side.scratch),
        compiler_params=_cparams(("arbitrary", "arbitrary", "arbitrary")))(
            xc, dtr, dtrt, bias, biast, a, at, dskip, *side.ins)


def _ssd_bwd(xc, dtr, dtrt, bias, biast, a, at, dskip, states, dy, nb, seq, side):
    t = xc.shape[0]
    nc = seq // CHUNK
    hg = HEADS_PER_GROUP
    hw = hg * HEAD_P
    _, sp = _ssd_specs(seq)

    def body(xc_ref, dtr_ref, dtrt_ref, bias_ref, biast_ref, a_ref, at_ref, d_ref, sin_ref, dy_ref,
             dxc_ref, ddtr_ref, gbias_ref, ga_ref, gd_ref, ds):
        first = (pl.program_id(1) == 0) & (pl.program_id(2) == 0)

        @pl.when(pl.program_id(2) == 0)
        def _():
            ds[...] = jnp.zeros_like(ds)

        @pl.when(first)
        def _():
            gbias_ref[...] = jnp.zeros_like(gbias_ref)
            ga_ref[...] = jnp.zeros_like(ga_ref)
            gd_ref[...] = jnp.zeros_like(gd_ref)

        dt, acs, acst, lower, upper, _, upper_b = _ssd_prelude(dtr_ref, dtrt_ref, bias_ref, biast_ref, a_ref, at_ref)
        spread, gather = _head_maps()
        x = xc_ref[...]
        dy = dy_ref[...]
        xs = x[:, :hw]
        b16 = x[:, hw:hw + D_STATE].astype(BF16)
        c16 = x[:, hw + D_STATE:].astype(BF16)
        dy16 = dy.astype(BF16)
        cb = _dot_nt(c16, b16)
        cbt = _dot_nt(b16, c16)
        last = acs[CHUNK - 1:CHUNK, :]
        e8 = jnp.exp(acs)
        dec8 = jnp.exp(last - acs)
        e_x = _dot01(e8, spread)
        dec_x = _dot01(dec8, spread)
        tot_x = e_x[CHUNK - 1:CHUNK, :]
        dt_x = _dot01(dt, spread)
        d_x = _dot01(jnp.broadcast_to(d_ref[...], (8, hg)), spread)[0:1, :]
        xdtf = xs * dt_x
        xdt16 = xdtf.astype(BF16)
        s_in = sin_ref[...]
        s16 = s_in.astype(BF16)
        ds_out = ds[...]
        ds16 = ds_out.astype(BF16)
        bds = _dot(b16, ds16)
        cs = _dot(c16, s16)
        edy16 = (e_x * dy).astype(BF16)
        ds[...] = tot_x * ds_out + _dot_tn(c16, edy16)
        lane8 = lax.broadcasted_iota(jnp.int32, (CHUNK, hg), 1)
        row8 = lax.broadcasted_iota(jnp.int32, (CHUNK, hg), 0)
        dacs8 = jnp.zeros((CHUNK, hg), F32)
        acc_m = jnp.zeros((CHUNK, CHUNK), F32)
        acc_mt = jnp.zeros((CHUNK, CHUNK), F32)
        dx_parts = []
        for j in range(hg):
            sl = slice(HEAD_P * j, HEAD_P * (j + 1))
            col = acs[:, j:j + 1]
            row = acst[j:j + 1, :]
            decay = jnp.exp(jnp.where(lower, col - row, -jnp.inf))
            decayt = jnp.exp(jnp.where(upper, row - col, -jnp.inf))
            wm = _dot_nt(dy16[:, sl], xdt16[:, sl]) * decay
            wmt = _dot_nt(xdt16[:, sl], dy16[:, sl]) * decayt
            acc_m = acc_m + wm
            acc_mt = acc_mt + wmt
            dacs8 = dacs8 + jnp.where(lane8 == j, jnp.sum(wm * cb, axis=-1, keepdims=True)
                                      - jnp.sum(wmt * cbt, axis=-1, keepdims=True), 0.0)
            dx_parts.append(_dot((cbt * decayt).astype(BF16), dy16[:, sl]))
        dx = jnp.concatenate(dx_parts, axis=-1) + dec_x * bds
        dxc_ref[:, :hw] = dx * dt_x + d_x * dy
        dxc_ref[:, hw:hw + D_STATE] = _dot(acc_mt.astype(BF16), c16) + _dot_nt((dec_x * xdtf).astype(BF16), ds16)
        dxc_ref[:, hw + D_STATE:] = _dot(acc_m.astype(BF16), b16) + _dot_nt(edy16, s16)
        dtot_rows = jnp.broadcast_to(_colsum(ds_out * s_in), (8, hw))
        sums = _dot01(jnp.concatenate([dy * cs, xdtf * bds, dx * xs, dy * xs, dtot_rows], axis=0), gather)
        de8 = sums[0:CHUNK, :hg]
        ddec8 = sums[CHUNK:2 * CHUNK, :hg]
        ddtx8 = sums[2 * CHUNK:3 * CHUNK, :hg]
        gd8 = _colsum(sums[3 * CHUNK:4 * CHUNK, :hg])
        dtot8 = sums[4 * CHUNK:4 * CHUNK + 1, :hg]
        extra = _colsum(ddec8 * dec8) + dtot8 * e8[CHUNK - 1:CHUNK, :]
        dacs8 = dacs8 + de8 * e8 - ddec8 * dec8 + jnp.where(row8 == CHUNK - 1, extra, 0.0)
        da = sum(_dot(upper_b, p) for p in _split3(dacs8))
        av = a_ref[...]
        ddt = da * av + ddtx8
        ddtr = ddt * _sigmoid(dtr_ref[...] + bias_ref[...])
        ddtr_ref[...] = ddtr
        gbias_ref[...] += _colsum(ddtr)
        ga_ref[...] += _colsum(da * dt) * av
        gd_ref[...] += gd8

    return pl.pallas_call(
        _attach_side(body, 10, 5, side, (N_GROUPS, nb, nc)), name="ssd_bwd", grid=(N_GROUPS, nb, nc),
        in_specs=[sp["xc"], sp["dtr"], sp["dtrt"], sp["prow"], sp["pcol"], sp["prow"], sp["pcol"], sp["prow"],
                  sp["st"], sp["y"]] + [ANY] * len(side.ins),
        out_specs=[sp["xc"], sp["dtr"], sp["prow"], sp["prow"], sp["prow"]] + [ANY] * len(side.out_shapes),
        out_shape=[SDS((t, N_GROUPS * GROUP_W), F32), SDS((N_GROUPS, t, hg), F32)]
        + [SDS((N_GROUPS, 1, hg), F32)] * 3 + list(side.out_shapes),
        scratch_shapes=[pltpu.VMEM((D_STATE, hw), F32)] + list(side.scratch),
        compiler_params=_cparams(("arbitrary", "arbitrary", "arbitrary")))(
            xc, dtr, dtrt, bias, biast, a, at, dskip, states, dy, *side.ins)


def _group_bcast(v, width, fn):
    parts = []
    for q in range(v.shape[-1] // width):
        s = fn(v[:, q * width:(q + 1) * width])
        parts.append(jnp.broadcast_to(s, (v.shape[0], width)))
    return jnp.concatenate(parts, axis=-1)


def _gate_norm_fwd(y, z, g):
    t, d = y.shape
    tm = 256
    gw = d // N_GROUPS

    def fn(i, y_ref, z_ref, g_ref):
        zv = z_ref[...].astype(F32)
        u = y_ref[...] * (zv * _sigmoid(zv))
        r = lax.rsqrt(_group_bcast(u * u, gw, lambda p: jnp.mean(p, axis=-1, keepdims=True)) + EPS)
        return [u * r * g_ref[...]]

    return _rw("gate_norm_fwd", fn, t // tm, [(y, _rs(tm, d)), (z, _rs(tm, d)), (g, _fs((1, d)))],
               [(SDS((t, d), BF16), _rs(tm, d))])[0]


def _gate_norm_bwd(y, z, g, dyn):
    t, d = y.shape
    tm = 256
    gw = d // N_GROUPS

    def fn(i, y_ref, z_ref, g_ref, dyn_ref):
        zv = z_ref[...].astype(F32)
        yv = y_ref[...]
        sg = _sigmoid(zv)
        sz = zv * sg
        u = yv * sz
        r = lax.rsqrt(_group_bcast(u * u, gw, lambda p: jnp.mean(p, axis=-1, keepdims=True)) + EPS)
        uhat = u * r
        dv = dyn_ref[...].astype(F32)
        duhat = dv * g_ref[...]
        du = r * (duhat - uhat * _group_bcast(duhat * uhat, gw, lambda p: jnp.mean(p, axis=-1, keepdims=True)))
        dz = du * yv * sg * (1.0 + zv * (1.0 - sg))
        return [du * sz, dz, _colsum(dv * uhat)]

    return _rw("gate_norm_bwd", fn, t // tm,
               [(y, _rs(tm, d)), (z, _rs(tm, d)), (g, _fs((1, d))), (dyn, _rs(tm, d))],
               [(SDS((t, d), F32), _rs(tm, d)), (SDS((t, d), BF16), _rs(tm, d)), (SDS((1, d), F32), _fs((1, d)))],
               n_acc=1)


def _rope_tables(seq):
    half = ATT_D // 2
    inv = ROPE_THETA ** (-jnp.arange(half, dtype=F32) / half)
    ang = jnp.arange(seq, dtype=F32)[:, None] * inv[None, :]
    cos, sin = jnp.cos(ang), jnp.sin(ang)
    return jnp.concatenate([cos, cos], axis=-1), jnp.concatenate([-sin, sin], axis=-1)


ATT_TILE = 512
ATT_QB = 4


def _strided_spec(r, mtiles):
    return pl.BlockSpec((None, r, None, ATT_TILE // r, ATT_W), lambda i: (i // mtiles, 0, i % mtiles, 0, 0))


def _strided_shape(nb, r, mtiles, dtype):
    return SDS((nb, r, mtiles, ATT_TILE // r, ATT_W), dtype)


def _to_strided(val, out_ref, lanes, r, sc):
    if r == 1:
        out_ref[0, :, lanes] = val.astype(out_ref.dtype)
        return
    sc[...] = val
    for rr in range(r):
        out_ref[rr, :, lanes] = sc[pl.ds(rr, ATT_TILE // r, stride=r), :].astype(out_ref.dtype)


def _from_strided(in_ref, lanes, r, sc):
    if r == 1:
        return in_ref[0, :, lanes].astype(F32)
    for rr in range(r):
        sc[pl.ds(rr, ATT_TILE // r, stride=r), :] = in_ref[rr, :, lanes].astype(F32)
    return sc[...]


def _rope_fwd(qkv, cos, sin, nb, seq):
    t = qkv.shape[0]
    tm = ATT_TILE
    mtiles = seq // tm
    w = ATT_HEADS * ATT_D
    tab = pl.BlockSpec((tm, ATT_D), lambda i: (i % mtiles, 0))
    ng = len(ATT_DILATIONS)

    def body(q_ref, k_ref, v_ref, cos_ref, sin_ref, *rest):
        outs, sc = rest[:3 * ng], rest[3 * ng]
        c, s = cos_ref[...], sin_ref[...]
        for which, ref in enumerate((q_ref, k_ref, v_ref)):
            for h in range(ATT_HEADS):
                g, slot = divmod(h, ATT_SLOTS)
                p = ref[:, h * ATT_D:(h + 1) * ATT_D].astype(F32)
                if which < 2:
                    p = p * c + pltpu.roll(p, ATT_D // 2, 1) * s
                _to_strided(p, outs[which * ng + g], slice(slot * ATT_D, (slot + 1) * ATT_D), ATT_DILATIONS[g], sc)

    out_specs = [_strided_spec(r, mtiles) for _ in range(3) for r in ATT_DILATIONS]
    out_shape = [_strided_shape(nb, r, mtiles, BF16) for _ in range(3) for r in ATT_DILATIONS]
    outs = pl.pallas_call(
        body, name="rope_fwd", grid=(t // tm,),
        in_specs=[_rs(tm, w, 0), _rs(tm, w, 1), _rs(tm, w, 2), tab, tab], out_specs=out_specs, out_shape=out_shape,
        scratch_shapes=[pltpu.VMEM((tm, ATT_D), F32)], compiler_params=_cparams(("arbitrary",)))(
            qkv, qkv, qkv, cos, sin)
    flat = [o.reshape(t, ATT_W) for o in outs]
    return flat[0:ng], flat[ng:2 * ng], flat[2 * ng:]


def _rope_bwd(dq, dk, dv, cos, sin, nb, seq):
    t = dq[0].shape[0]
    tm = ATT_TILE
    mtiles = seq // tm
    w = ATT_HEADS * ATT_D
    tab = pl.BlockSpec((tm, ATT_D), lambda i: (i % mtiles, 0))
    ng = len(ATT_DILATIONS)

    def body(*refs):
        ins, (cos_ref, sin_ref, o_ref, sc) = refs[:3 * ng], refs[3 * ng:]
        c, s = cos_ref[...], sin_ref[...]
        for which in range(3):
            for h in range(ATT_HEADS):
                g, slot = divmod(h, ATT_SLOTS)
                p = _from_strided(ins[which * ng + g], slice(slot * ATT_D, (slot + 1) * ATT_D), ATT_DILATIONS[g], sc)
                if which < 2:
                    p = p * c - pltpu.roll(p, ATT_D // 2, 1) * s
                o_ref[:, which * w + h * ATT_D:which * w + (h + 1) * ATT_D] = p.astype(o_ref.dtype)

    views = [a.reshape(nb, r, mtiles, tm // r, ATT_W) for grp in (dq, dk, dv) for a, r in zip(grp, ATT_DILATIONS)]
    return pl.pallas_call(
        body, name="rope_bwd", grid=(t // tm,),
        in_specs=[_strided_spec(r, mtiles) for _ in range(3) for r in ATT_DILATIONS] + [tab, tab],
        out_specs=_rs(tm, 3 * w), out_shape=SDS((t, 3 * w), BF16),
        scratch_shapes=[pltpu.VMEM((tm, ATT_D), F32)], compiler_params=_cparams(("arbitrary",)))(*views, cos, sin)


def _att_masks():
    ri = lax.broadcasted_iota(jnp.int32, (ATT_BLOCK, ATT_BLOCK), 0)
    ci = lax.broadcasted_iota(jnp.int32, (ATT_BLOCK, ATT_BLOCK), 1)
    return ci <= ri, ci >= ri


def _att_fwd(q, k, v, g, seq):
    t, w = q.shape
    rows = ATT_QB * ATT_BLOCK
    nbs = seq // ATT_DILATIONS[g] // ATT_BLOCK
    scale = ATT_D ** -0.5
    cur = pl.BlockSpec((rows, w), lambda n: (n, 0))
    prev = pl.BlockSpec((ATT_BLOCK, w), lambda n: (jnp.maximum(n * ATT_QB - 1, 0), 0))

    def body(q_ref, kc_ref, kp_ref, vc_ref, vp_ref, o_ref, lse_ref):
        mcur, mprev = _att_masks()
        for i in range(ATT_QB):
            blk = pl.program_id(0) * ATT_QB + i
            mask = jnp.concatenate([mprev & ((blk % nbs) != 0), mcur], axis=-1)
            own = slice(i * ATT_BLOCK, (i + 1) * ATT_BLOCK)
            for h in range(ATT_SLOTS):
                sl = slice(h * ATT_D, (h + 1) * ATT_D)
                if i == 0:
                    keys = jnp.concatenate([kp_ref[:, sl], kc_ref[own, sl]], axis=0)
                    vals = jnp.concatenate([vp_ref[:, sl], vc_ref[own, sl]], axis=0)
                else:
                    both = slice((i - 1) * ATT_BLOCK, (i + 1) * ATT_BLOCK)
                    keys, vals = kc_ref[both, sl], vc_ref[both, sl]
                s = jnp.where(mask, _dot_nt(q_ref[own, sl], keys) * scale, -jnp.inf)
                m = jnp.max(s, axis=-1, keepdims=True)
                p = jnp.exp(s - m)
                den = jnp.sum(p, axis=-1, keepdims=True)
                o_ref[own, sl] = _dot(p.astype(BF16), vals) / den
                lse_ref[own, sl] = jnp.broadcast_to(m + jnp.log(den), (ATT_BLOCK, ATT_D))

    return pl.pallas_call(
        body, name=f"att_fwd_{g}", grid=(t // rows,), in_specs=[cur, cur, prev, cur, prev], out_specs=[cur, cur],
        out_shape=[SDS((t, w), F32), SDS((t, w), F32)],
        compiler_params=_cparams(("arbitrary",)))(q, k, k, v, v)


def _att_bwd(q, k, v, do, lse, dlt, g, seq):
    t, w = q.shape
    nblk = t // ATT_BLOCK
    rows = ATT_QB * ATT_BLOCK
    nbs = seq // ATT_DILATIONS[g] // ATT_BLOCK
    scale = ATT_D ** -0.5
    cur = pl.BlockSpec((rows, w), lambda n: (n, 0))
    nxt = pl.BlockSpec((ATT_BLOCK, w), lambda n: (jnp.minimum((n + 1) * ATT_QB, nblk - 1), 0))

    def body(qc_ref, qn_ref, k_ref, v_ref, doc_ref, don_ref, lsec_ref, lsen_ref, dltc_ref, dltn_ref,
             dq_ref, dk_ref, dv_ref, carry):
        n = pl.program_id(0)

        @pl.when(n == 0)
        def _():
            carry[...] = jnp.zeros_like(carry)

        mcur, mprev = _att_masks()

        def pair(cur_ref, nxt_ref, i, sl):
            if i + 1 < ATT_QB:
                return cur_ref[i * ATT_BLOCK:(i + 2) * ATT_BLOCK, sl]
            return jnp.concatenate([cur_ref[i * ATT_BLOCK:, sl], nxt_ref[:, sl]], axis=0)

        for h in range(ATT_SLOTS):
            sl = slice(h * ATT_D, (h + 1) * ATT_D)
            from_prev = carry[:, sl]
            for i in range(ATT_QB):
                blk = n * ATT_QB + i
                has_next = (((blk + 1) % nbs) != 0) & (blk + 1 < nblk)
                mask = jnp.concatenate([mcur, mprev & has_next], axis=0)
                own = slice(i * ATT_BLOCK, (i + 1) * ATT_BLOCK)
                kh, vh = k_ref[own, sl], v_ref[own, sl]
                qs, dos = pair(qc_ref, qn_ref, i, sl), pair(doc_ref, don_ref, i, sl)
                lse, dlt = pair(lsec_ref, lsen_ref, i, sl), pair(dltc_ref, dltn_ref, i, sl)
                p = jnp.where(mask, jnp.exp(_dot_nt(qs, kh) * scale - lse), 0.0)
                ds = (p * (_dot_nt(dos, vh) - dlt) * scale).astype(BF16)
                dqs = _dot(ds, kh)
                dq_ref[own, sl] = (from_prev + dqs[:ATT_BLOCK]).astype(dq_ref.dtype)
                from_prev = dqs[ATT_BLOCK:]
                dk_ref[own, sl] = _dot_tn(ds, qs).astype(dk_ref.dtype)
                dv_ref[own, sl] = _dot_tn(p.astype(BF16), dos).astype(dv_ref.dtype)
            carry[:, sl] = from_prev

    return pl.pallas_call(
        body, name=f"att_bwd_{g}", grid=(t // rows,), in_specs=[cur, nxt, cur, cur, cur, nxt, cur, nxt, cur, nxt],
        out_specs=[cur, cur, cur], out_shape=[SDS((t, w), BF16)] * 3,
        scratch_shapes=[pltpu.VMEM((ATT_BLOCK, w), F32)],
        compiler_params=_cparams(("arbitrary",)))(q, q, k, v, do, do, lse, lse, dlt, dlt)


def _merge_weights(ls):
    m = jnp.maximum(jnp.maximum(ls[0], ls[1]), ls[2])
    es = [jnp.exp(v - m) for v in ls]
    den = es[0] + es[1] + es[2]
    return [e / den for e in es]


def _merge_fwd(o, lse, nb, seq):
    t = o[0].shape[0]
    tm = ATT_TILE
    mtiles = seq // tm
    ng = len(ATT_DILATIONS)

    def body(*refs):
        o_refs, l_refs, out_ref, scs = refs[:ng], refs[ng:2 * ng], refs[2 * ng], refs[2 * ng + 1:]
        for slot in range(ATT_SLOTS):
            lanes = slice(slot * ATT_D, (slot + 1) * ATT_D)
            ov = [_from_strided(o_refs[g], lanes, r, scs[2 * g]) for g, r in enumerate(ATT_DILATIONS)]
            ws = _merge_weights([_from_strided(l_refs[g], lanes, r, scs[2 * g + 1])
                                 for g, r in enumerate(ATT_DILATIONS)])
            out_ref[:, lanes] = (ws[0] * ov[0] + ws[1] * ov[1] + ws[2] * ov[2]).astype(out_ref.dtype)

    views = [a.reshape(nb, r, mtiles, tm // r, ATT_W) for grp in (o, lse) for a, r in zip(grp, ATT_DILATIONS)]
    return pl.pallas_call(
        body, name="att_merge_fwd", grid=(t // tm,),
        in_specs=[_strided_spec(r, mtiles) for _ in range(2) for r in ATT_DILATIONS],
        out_specs=_rs(tm, ATT_W), out_shape=SDS((t, ATT_W), BF16),
        scratch_shapes=[pltpu.VMEM((tm, ATT_D), F32)] * (2 * ng), compiler_params=_cparams(("arbitrary",)))(*views)


def _merge_bwd(o, lse, datt, nb, seq):
    t = o[0].shape[0]
    tm = ATT_TILE
    mtiles = seq // tm
    ng = len(ATT_DILATIONS)

    def body(*refs):
        o_refs, l_refs, d_ref = refs[:ng], refs[ng:2 * ng], refs[2 * ng]
        do_refs, dlt_refs = refs[2 * ng + 1:3 * ng + 1], refs[3 * ng + 1:4 * ng + 1]
        scs = refs[4 * ng + 1:]
        for slot in range(ATT_SLOTS):
            lanes = slice(slot * ATT_D, (slot + 1) * ATT_D)
            ov = [_from_strided(o_refs[g], lanes, r, scs[2 * g]) for g, r in enumerate(ATT_DILATIONS)]
            ws = _merge_weights([_from_strided(l_refs[g], lanes, r, scs[2 * g + 1])
                                 for g, r in enumerate(ATT_DILATIONS)])
            dv = d_ref[:, lanes]
            att = ws[0] * ov[0] + ws[1] * ov[1] + ws[2] * ov[2]
            dot = jnp.broadcast_to(jnp.sum(dv * att, axis=-1, keepdims=True), (tm, ATT_D))
            for g, r in enumerate(ATT_DILATIONS):
                _to_strided(ws[g] * dv, do_refs[g], lanes, r, scs[2 * ng])
                _to_strided(ws[g] * dot, dlt_refs[g], lanes, r, scs[2 * ng + 1])

    views = [a.reshape(nb, r, mtiles, tm // r, ATT_W) for grp in (o, lse) for a, r in zip(grp, ATT_DILATIONS)]
    outs = pl.pallas_call(
        body, name="att_merge_bwd", grid=(t // tm,),
        in_specs=[_strided_spec(r, mtiles) for _ in range(2) for r in ATT_DILATIONS] + [_rs(tm, ATT_W)],
        out_specs=[_strided_spec(r, mtiles) for _ in range(2) for r in ATT_DILATIONS],
        out_shape=[_strided_shape(nb, r, mtiles, dt) for dt in (BF16, F32) for r in ATT_DILATIONS],
        scratch_shapes=[pltpu.VMEM((tm, ATT_D), F32)] * (2 * ng + 2), compiler_params=_cparams(("arbitrary",)))(
            *views, datt)
    flat = [a.reshape(t, ATT_W) for a in outs]
    return flat[:ng], flat[ng:]


def _mix_fwd(gate_logits, b_gate, y_ssm, y_att):
    t, d = y_ssm.shape
    tm = 512

    def fn(i, g0_ref, g1_ref, b0_ref, b1_ref, ys_ref, ya_ref):
        g0 = _sigmoid(g0_ref[...].astype(F32) + b0_ref[...])
        g1 = _sigmoid(g1_ref[...].astype(F32) + b1_ref[...])
        return [g0 * ys_ref[...].astype(F32) + g1 * ya_ref[...].astype(F32)]

    b_spec = lambda cb: pl.BlockSpec((1, d), lambda i: (0, cb))
    return _rw("mix_fwd", fn, t // tm,
               [(gate_logits, _rs(tm, d, 0)), (gate_logits, _rs(tm, d, 1)), (b_gate, b_spec(0)), (b_gate, b_spec(1)),
                (y_ssm, _rs(tm, d)), (y_att, _rs(tm, d))],
               [(SDS((t, d), BF16), _rs(tm, d))])[0]


def _mix_bwd(gate_logits, b_gate, y_ssm, y_att, dmixed):
    t, d = y_ssm.shape
    tm = 256

    def fn(i, g0_ref, g1_ref, b0_ref, b1_ref, ys_ref, ya_ref, dm_ref):
        g0 = _sigmoid(g0_ref[...].astype(F32) + b0_ref[...])
        g1 = _sigmoid(g1_ref[...].astype(F32) + b1_ref[...])
        dm = dm_ref[...]
        dg = jnp.concatenate([dm * ys_ref[...].astype(F32) * g0 * (1.0 - g0),
                              dm * ya_ref[...].astype(F32) * g1 * (1.0 - g1)], axis=-1)
        return [dm * g0, dm * g1, dg, _colsum(dg)]

    b_spec = lambda cb: pl.BlockSpec((1, d), lambda i: (0, cb))
    return _rw("mix_bwd", fn, t // tm,
               [(gate_logits, _rs(tm, d, 0)), (gate_logits, _rs(tm, d, 1)), (b_gate, b_spec(0)), (b_gate, b_spec(1)),
                (y_ssm, _rs(tm, d)), (y_att, _rs(tm, d)), (dmixed, _rs(tm, d))],
               [(SDS((t, d), BF16), _rs(tm, d)), (SDS((t, d), BF16), _rs(tm, d)),
                (SDS((t, 2 * d), BF16), _rs(tm, 2 * d)), (SDS((1, 2 * d), F32), _fs((1, 2 * d)))], n_acc=1)


def _swiglu_fwd(gt, up):
    t, f = gt.shape
    tm = 256

    def fn(i, g_ref, u_ref):
        gv = g_ref[...].astype(F32)
        return [gv * _sigmoid(gv) * u_ref[...].astype(F32)]

    return _rw("swiglu_fwd", fn, t // tm, [(gt, _rs(tm, f)), (up, _rs(tm, f))], [(SDS((t, f), BF16), _rs(tm, f))])[0]


def _swiglu_bwd(gt, up, dact):
    t, f = gt.shape
    tm = 256

    def fn(i, g_ref, u_ref, d_ref):
        gv, dv = g_ref[...].astype(F32), d_ref[...].astype(F32)
        sg = _sigmoid(gv)
        return [dv * u_ref[...].astype(F32) * sg * (1.0 + gv * (1.0 - sg)), dv * gv * sg]

    return _rw("swiglu_bwd", fn, t // tm, [(gt, _rs(tm, f)), (up, _rs(tm, f)), (dact, _rs(tm, f))],
               [(SDS((t, f), BF16), _rs(tm, f))] * 2)


def _adamw(w, g, m, v, name):
    r, c = w.shape[-2:]
    lead = w.ndim - 2
    tr = _row_tile(r, max(8, 400_000 // c))
    c1 = 1.0 / (1.0 - ADAM_B1 ** ADAM_STEP)
    c2 = 1.0 / (1.0 - ADAM_B2 ** ADAM_STEP)

    def fn(i, w_ref, g_ref, m_ref, v_ref):
        gv = g_ref[...]
        mn = ADAM_B1 * m_ref[...] + (1.0 - ADAM_B1) * gv
        vn = ADAM_B2 * v_ref[...] + (1.0 - ADAM_B2) * (gv * gv)
        delta = -ADAM_LR * ((mn * c1) / (jnp.sqrt(vn * c2) + ADAM_EPS) + ADAM_WD * w_ref[...])
        return [delta, mn, vn]

    spec = pl.BlockSpec((None,) * lead + (tr, c), lambda i: (0,) * lead + (i, 0))
    return _rw(name, fn, r // tr, [(w, spec), (g, spec), (m, spec), (v, spec)], [(SDS(w.shape, F32), spec)] * 3)


ANY = pl.BlockSpec(memory_space=pl.ANY)


def _place():
    x, y, c = lax.axis_index("x"), lax.axis_index("y"), lax.axis_index("c")
    chips = [(1 - x, y), (x, 1 - y), (1 - x, 1 - y)]
    return x, y, c, chips


def _remote(src, dst, ssem, rsem, to):
    return pltpu.make_async_remote_copy(src_ref=src, dst_ref=dst, send_sem=ssem, recv_sem=rsem, device_id=to,
                                        device_id_type=MESH)


def _copy_through_vmem(src, dst, buf, isem, osem):
    chunk = buf.shape[1]
    n = src.shape[0] // chunk
    load = lambda k: pltpu.make_async_copy(src.at[pl.ds(k * chunk, chunk)], buf.at[k % 2], isem.at[k % 2])
    store = lambda k: pltpu.make_async_copy(buf.at[k % 2], dst.at[pl.ds(k * chunk, chunk)], osem.at[k % 2])
    load(0).start()
    for k in range(n):
        load(k).wait()
        if k + 1 < n:
            if k >= 1:
                store(k - 1).wait()
            load(k + 1).start()
        store(k).start()
    if n >= 2:
        store(n - 2).wait()
    store(n - 1).wait()


def _copy_scratch(rows, width, dtype):
    chunk = _row_tile(rows, 512)
    return [pltpu.VMEM((2, chunk, width), dtype), pltpu.SemaphoreType.DMA((2,)), pltpu.SemaphoreType.DMA((2,))]


def _gather_weights(wp):
    def body(w_ref, out_ref, ssem, rsem, buf, isem, osem):
        x, y, c, chips = _place()
        me = 2 * x + y
        sib = (x, y, 1 - c)
        first = [_remote(w_ref.at[c], out_ref.at[me, c], ssem.at[j], rsem.at[j], (*chip, c))
                 for j, chip in enumerate(chips)]
        for cp in first:
            cp.start()
        for half in range(2):
            _copy_through_vmem(w_ref.at[half], out_ref.at[me, half], buf, isem, osem)
        passed = []
        for j, chip in enumerate(chips):
            ci = 2 * chip[0] + chip[1]
            _remote(w_ref.at[c], out_ref.at[ci, c], ssem.at[j], rsem.at[j], (*chip, c)).wait_recv()
            cp = _remote(out_ref.at[ci, c], out_ref.at[ci, c], ssem.at[3 + j], rsem.at[3 + j], sib)
            cp.start()
            passed.append(cp)
        for j, chip in enumerate(chips):
            ci = 2 * chip[0] + chip[1]
            _remote(out_ref.at[ci, 1 - c], out_ref.at[ci, 1 - c], ssem.at[3 + j], rsem.at[3 + j], sib).wait_recv()
        for cp in first + passed:
            cp.wait_send()

    return pl.pallas_call(
        body, name="gather_weights", in_specs=[ANY], out_specs=ANY,
        out_shape=SDS((N_CHIPS,) + wp.shape, wp.dtype),
        scratch_shapes=[pltpu.SemaphoreType.DMA((6,)), pltpu.SemaphoreType.DMA((6,))]
        + _copy_scratch(wp.shape[1], wp.shape[2], wp.dtype),
        compiler_params=pltpu.CompilerParams(has_side_effects=True))(wp)


def _swap_halves(g2, tag):
    def body(g_ref, out_ref, ssem, rsem):
        x, y, c, _ = _place()
        cp = _remote(g_ref.at[1 - c], out_ref, ssem, rsem, (x, y, 1 - c))
        cp.start()
        cp.wait()

    return pl.pallas_call(
        body, name="swap_halves_" + tag, in_specs=[ANY], out_specs=ANY, out_shape=SDS(g2.shape[1:], g2.dtype),
        scratch_shapes=[pltpu.SemaphoreType.DMA(()), pltpu.SemaphoreType.DMA(())],
        compiler_params=pltpu.CompilerParams(has_side_effects=True))(g2)


def _add_own_half(g2, other, c, tag):
    _, nch, rows, w = g2.shape
    tr = _row_tile(rows, 512)
    nr = rows // tr

    def body(c_ref, a_ref, b_ref, o_ref):
        o_ref[...] = (a_ref[...].astype(F32) + b_ref[...].astype(F32)).astype(o_ref.dtype)

    grid_spec = pltpu.PrefetchScalarGridSpec(
        num_scalar_prefetch=1, grid=(nch, nr),
        in_specs=[pl.BlockSpec((None, None, tr, w), lambda k, i, c_ref: (c_ref[0], k, i, 0)),
                  pl.BlockSpec((None, tr, w), lambda k, i, c_ref: (k, i, 0))],
        out_specs=pl.BlockSpec((None, tr, w), lambda k, i, c_ref: (k, i, 0)))
    return pl.pallas_call(
        body, name="add_own_half_" + tag, grid_spec=grid_spec, out_shape=SDS(other.shape, other.dtype),
        compiler_params=_cparams(("arbitrary", "arbitrary")))(jnp.reshape(c, (1,)).astype(jnp.int32), g2, other)


def _sum_chips(q, tag):
    nch, rows, w = q.shape
    tr = _row_tile(rows, 512)

    def fn(i, q_ref):
        return [((q_ref[0].astype(F32) + q_ref[1].astype(F32)) + q_ref[2].astype(F32)) + q_ref[3].astype(F32)]

    return _rw("sum_chips_" + tag, fn, rows // tr, [(q, pl.BlockSpec((nch, tr, w), lambda i: (0, i, 0)))],
               [(SDS((rows, w), F32), _rs(tr, w))])[0]


def _chip_copies(src_ref, dst_ref, ssem, rsem, outgoing):
    x, y, c, chips = _place()
    me = 2 * x + y
    cps = []
    for j, chip in enumerate(chips):
        ci = 2 * chip[0] + chip[1]
        cps.append(_remote(src_ref.at[ci], dst_ref.at[me if outgoing else ci], ssem.at[j], rsem.at[j], (*chip, c)))
    return cps, me


def _scatter_side(p):
    def first(ins, outs, scr):
        cps, me = _chip_copies(ins[0], outs[0], scr[0], scr[1], True)
        for cp in cps:
            cp.start()
        pltpu.make_async_copy(ins[0].at[me], outs[0].at[me], scr[2]).start()

    def last(ins, outs, scr):
        for cp in _chip_copies(ins[0], outs[0], scr[0], scr[1], False)[0]:
            cp.wait_recv()
        cps, me = _chip_copies(ins[0], outs[0], scr[0], scr[1], True)
        for cp in cps:
            cp.wait_send()
        pltpu.make_async_copy(ins[0].at[me], outs[0].at[me], scr[2]).wait()

    return _Side((p,), (SDS(p.shape, p.dtype),),
                 (pltpu.SemaphoreType.DMA((3,)), pltpu.SemaphoreType.DMA((3,)), pltpu.SemaphoreType.DMA(())),
                 first, None, last)


def _gather_copies(w_ref, out_ref, ssem, rsem):
    x, y, c, chips = _place()
    me = 2 * x + y
    sib = (x, y, 1 - c)
    sends, arrivals, forwards, from_sib = [], [], [], []
    for j, chip in enumerate(chips):
        ci = 2 * chip[0] + chip[1]
        sends.append(_remote(w_ref.at[c], out_ref.at[me, c], ssem.at[j], rsem.at[j], (*chip, c)))
        arrivals.append(_remote(w_ref.at[c], out_ref.at[ci, c], ssem.at[j], rsem.at[j], (*chip, c)))
        forwards.append(_remote(out_ref.at[ci, c], out_ref.at[ci, c], ssem.at[3 + j], rsem.at[3 + j], sib))
        from_sib.append(_remote(out_ref.at[ci, 1 - c], out_ref.at[ci, 1 - c], ssem.at[3 + j], rsem.at[3 + j], sib))
    return sends, arrivals, forwards, from_sib, me


def _gather_side(wp):
    def first(ins, outs, scr):
        sends, _, _, _, me = _gather_copies(ins[0], outs[0], scr[0], scr[1])
        for cp in sends:
            cp.start()
        pltpu.make_async_copy(ins[0], outs[0].at[me], scr[2]).start()

    def mid(ins, outs, scr):
        _, arrivals, forwards, _, _ = _gather_copies(ins[0], outs[0], scr[0], scr[1])
        for arrived, forward in zip(arrivals, forwards):
            arrived.wait_recv()
            forward.start()

    def last(ins, outs, scr):
        sends, _, forwards, from_sib, me = _gather_copies(ins[0], outs[0], scr[0], scr[1])
        for cp in from_sib:
            cp.wait_recv()
        for cp in sends + forwards:
            cp.wait_send()
        pltpu.make_async_copy(ins[0], outs[0].at[me], scr[2]).wait()

    return _Side((wp,), (SDS((N_CHIPS,) + wp.shape, wp.dtype),),
                 (pltpu.SemaphoreType.DMA((6,)), pltpu.SemaphoreType.DMA((6,)), pltpu.SemaphoreType.DMA(())),
                 first, mid, last)


def _allreduce_small(v, name):
    rows, w = v.shape
    offsets = [(dx, dy, dc) for dx in (0, 1) for dy in (0, 1) for dc in (0, 1)][1:]

    def body(v_ref, o_ref, buf, ssem, rsem):
        x, y, c, _ = _place()
        flip = lambda p, d: 1 - p if d else p
        peers = [(flip(x, dx), flip(y, dy), flip(c, dc)) for dx, dy, dc in offsets]
        index = lambda p: 4 * p[0] + 2 * p[1] + p[2]
        me = index((x, y, c))
        buf[me] = v_ref[...]
        sent = [_remote(v_ref, buf.at[me], ssem.at[q], rsem.at[q], p) for q, p in enumerate(peers)]
        for cp in sent:
            cp.start()
        for q, p in enumerate(peers):
            _remote(v_ref, buf.at[index(p)], ssem.at[q], rsem.at[q], p).wait_recv()
        for cp in sent:
            cp.wait_send()
        acc = buf[0]
        for q in range(1, 8):
            acc = acc + buf[q]
        o_ref[...] = acc

    vm = pl.BlockSpec(memory_space=pltpu.VMEM)
    return pl.pallas_call(
        body, name=name, in_specs=[vm], out_specs=vm, out_shape=SDS((rows, w), F32),
        scratch_shapes=[pltpu.VMEM((8, rows, w), F32), pltpu.SemaphoreType.DMA((7,)), pltpu.SemaphoreType.DMA((7,))],
        compiler_params=pltpu.CompilerParams(has_side_effects=True))(v)


def _join_halves(h, tag):
    def body(h_ref, out_ref, ssem, rsem, buf, isem, osem):
        x, y, c, _ = _place()
        cp = _remote(h_ref, out_ref.at[c], ssem, rsem, (x, y, 1 - c))
        cp.start()
        _copy_through_vmem(h_ref, out_ref.at[c], buf, isem, osem)
        _remote(h_ref, out_ref.at[1 - c], ssem, rsem, (x, y, 1 - c)).wait_recv()
        cp.wait_send()

    return pl.pallas_call(
        body, name="join_halves_" + tag, in_specs=[ANY], out_specs=ANY, out_shape=SDS((2,) + h.shape, h.dtype),
        scratch_shapes=[pltpu.SemaphoreType.DMA(()), pltpu.SemaphoreType.DMA(())]
        + _copy_scratch(h.shape[0], h.shape[1], h.dtype),
        compiler_params=pltpu.CompilerParams(has_side_effects=True))(h)


PACK_W = 1024
SHARDED = ("w_in", "w_ffn_gate", "w_ffn_up", "w_ssm_out", "w_att_out", "w_mix_out", "w_ffn_down")
COL_SHARDED = ("w_in", "w_ffn_gate", "w_ffn_up", "w_att_out")
SMALL = ("norm_mix", "b_gate", "conv_b", "dt_bias", "a_log", "d_skip", "ssm_norm", "norm_ffn", "norm_final")


PACK_ROW_ALIGN = 16


def _rows(n):
    return -(-n // (PACK_W * PACK_ROW_ALIGN)) * PACK_ROW_ALIGN


def _pack_rows(parts, total_rows):
    rows = []
    for p in parts:
        flat = p.reshape(-1)
        pad = _rows(flat.shape[0]) * PACK_W - flat.shape[0]
        if pad:
            flat = jnp.concatenate([flat, jnp.zeros((pad,), flat.dtype)])
        rows.append(flat.reshape(-1, PACK_W))
    used = sum(r.shape[0] for r in rows)
    if total_rows > used:
        rows.append(jnp.zeros((total_rows - used, PACK_W), rows[0].dtype))
    return jnp.concatenate(rows, axis=0)


def _padded_rows(n):
    return -(-n // 32) * 32


def _wire_name(name):
    return name + "_t" if name in COL_SHARDED else name


def _wire_shard(w, name):
    return w.T if name in COL_SHARDED else w


def _group_major(a, axis):
    gw = D_INNER // N_GROUPS
    take = lambda lo, n: lax.slice_in_dim(a, lo, lo + n, axis=axis)
    parts = []
    for g in range(N_GROUPS):
        parts += [take(g * gw, gw), take(D_INNER + g * D_STATE, D_STATE),
                  take(D_INNER + N_GROUPS * D_STATE + g * D_STATE, D_STATE)]
    return jnp.concatenate(parts, axis=axis)


def _group_major_inv(a, axis):
    gw = D_INNER // N_GROUPS
    take = lambda lo, n: lax.slice_in_dim(a, lo, lo + n, axis=axis)
    xs = [take(g * GROUP_W, gw) for g in range(N_GROUPS)]
    bs = [take(g * GROUP_W + gw, D_STATE) for g in range(N_GROUPS)]
    cs = [take(g * GROUP_W + gw + D_STATE, D_STATE) for g in range(N_GROUPS)]
    return jnp.concatenate(xs + bs + cs, axis=axis)


LATE = ("w_ffn_gate_t", "w_ffn_up_t", "w_ssm_out", "w_att_out_t", "w_mix_out", "w_ffn_down")


class _Overlap(NamedTuple):
    gather_side: _Side
    late_weights: Callable
    scatter_side: Callable
    scatter_in: Callable


def _local_step(x, target, wts, overlap):
    nb, seq, d = x.shape
    t = nb * seq
    x = x.reshape(t, d)
    target = target.reshape(t, d)
    hg = HEADS_PER_GROUP

    w_in_t = wts["w_in_t"]
    o1, o2, o3, o4 = D_INNER, D_INNER + CONV_DIM, D_INNER + CONV_DIM + N_HEADS, D_INNER + CONV_DIM + N_HEADS + QKV_DIM
    w_z = w_in_t[:o1]
    w_xbc = _group_major(w_in_t[o1:o2], 0)
    w_dt = jnp.pad(w_in_t[o2:o3], ((0, DT_PAD - N_HEADS), (0, 0)))
    w_qkv = w_in_t[o3:o4]
    w_gate = w_in_t[o4:]
    conv_w = _group_major(wts["conv_w"], 1)
    conv_b = _group_major(wts["conv_b"], 1)

    def per_group_row(p):
        return p.reshape(N_GROUPS, 1, hg)

    def per_group_col(p):
        return p.reshape(N_GROUPS, hg, 1)

    a_neg = -jnp.exp(wts["a_log"])
    bias_r, bias_c = per_group_row(wts["dt_bias"]), per_group_col(wts["dt_bias"])
    a_r, a_c = per_group_row(a_neg), per_group_col(a_neg)
    dskip_r = per_group_row(wts["d_skip"])
    cos, sin = _rope_tables(seq)

    h = _rms_fwd(x, wts["norm_mix"], "rms_mix_fwd")
    z = _mm(h, w_z, "nt", BF16, "proj_z")
    xbc = _mm(h, w_xbc, "nt", F32, "proj_xbc")
    dt_raw = _mm(h, w_dt, "nt", F32, "proj_dt")
    qkv = _mm(h, w_qkv, "nt", BF16, "proj_qkv")
    gate_logits = _mm(h, w_gate, "nt", BF16, "proj_gate")

    xc = _conv_fwd(xbc, conv_w, conv_b, seq)
    dtr = dt_raw[:, :N_HEADS].reshape(t, N_GROUPS, hg).transpose(1, 0, 2)
    dtrt = dt_raw[:, :N_HEADS].reshape(nb, seq, N_GROUPS, hg).transpose(2, 0, 3, 1)
    y, states, *gathered = _ssd_fwd(xc, dtr, dtrt, bias_r, bias_c, a_r, a_c, dskip_r, nb, seq, overlap.gather_side)
    wts = {**wts, **overlap.late_weights(gathered)}
    yn = _gate_norm_fwd(y, z, wts["ssm_norm"])
    y_ssm = _mm(yn, wts["w_ssm_out"], "nn", BF16, "ssm_out")

    groups = range(len(ATT_DILATIONS))
    qg, kg, vg = _rope_fwd(qkv, cos, sin, nb, seq)
    o_g, lse_g = zip(*[_att_fwd(qg[i], kg[i], vg[i], i, seq) for i in groups])
    att = _merge_fwd(o_g, lse_g, nb, seq)
    y_att = _mm(att, wts["w_att_out_t"], "nt", BF16, "att_out")

    mixed = _mix_fwd(gate_logits, wts["b_gate"], y_ssm, y_att)
    x1 = _mm(mixed, wts["w_mix_out"], "nn", F32, "mix_out", add=x)
    h2 = _rms_fwd(x1, wts["norm_ffn"], "rms_ffn_fwd")
    gt = _mm(h2, wts["w_ffn_gate_t"], "nt", BF16, "ffn_gate")
    up = _mm(h2, wts["w_ffn_up_t"], "nt", BF16, "ffn_up")
    act = _swiglu_fwd(gt, up)
    x2 = _mm(act, wts["w_ffn_down"], "nn", F32, "ffn_down", add=x1)

    g = {}
    dx2, dx2_b, g["norm_final"], loss = _final_fwd_bwd(x2, target, wts["norm_final"].reshape(1, d))
    dact = _mm(dx2_b, wts["w_ffn_down"], "nt", BF16, "d_act")
    g["w_ffn_down"] = _mm(act, dx2_b, "tn", BF16, "g_ffn_down")
    dgt, dup = _swiglu_bwd(gt, up, dact)
    g["w_ffn_gate_t"] = _mm(dgt, h2, "tn", BF16, "g_ffn_gate")
    g["w_ffn_up_t"] = _mm(dup, h2, "tn", BF16, "g_ffn_up")
    dh2 = _mm(dgt, wts["w_ffn_gate_t"], "nn", F32, "d_h2_gate")
    dh2 = _mm(dup, wts["w_ffn_up_t"], "nn", F32, "d_h2_up", add=dh2)
    dx1, dx1_b, g["norm_ffn"] = _rms_bwd(x1, dh2, wts["norm_ffn"], dx2, "rms_ffn_bwd")

    dmixed = _mm(dx1_b, wts["w_mix_out"], "nt", F32, "d_mixed")
    g["w_mix_out"] = _mm(mixed, dx1_b, "tn", BF16, "g_mix_out")
    dy_ssm, dy_att, dgate, g["b_gate"] = _mix_bwd(gate_logits, wts["b_gate"], y_ssm, y_att, dmixed)

    datt = _mm(dy_att, wts["w_att_out_t"], "nn", F32, "d_att")
    g["w_att_out_t"] = _mm(dy_att, att, "tn", BF16, "g_att_out")
    do_g, dlt_g = _merge_bwd(o_g, lse_g, datt, nb, seq)
    dq_g, dk_g, dv_g = zip(*[_att_bwd(qg[i], kg[i], vg[i], do_g[i], lse_g[i], dlt_g[i], i, seq) for i in groups])
    dqkv = _rope_bwd(dq_g, dk_g, dv_g, cos, sin, nb, seq)

    dyn = _mm(dy_ssm, wts["w_ssm_out"], "nt", BF16, "d_yn")
    g["w_ssm_out"] = _mm(yn, dy_ssm, "tn", BF16, "g_ssm_out")
    dy, dz, g["ssm_norm"] = _gate_norm_bwd(y, z, wts["ssm_norm"], dyn)
    side = overlap.scatter_side({n: g.pop(n) for n in LATE})
    dxc, ddtr, g_bias, g_alog, g_dskip, *scattered = _ssd_bwd(xc, dtr, dtrt, bias_r, bias_c, a_r, a_c, dskip_r,
                                                               states, dy, nb, seq, side)
    g["dt_bias"] = g_bias.reshape(1, N_HEADS)
    g["a_log"] = g_alog.reshape(1, N_HEADS)
    g["d_skip"] = g_dskip.reshape(1, N_HEADS)
    dpre, g_conv_w, g_conv_b = _conv_bwd_pre(xbc, conv_w, conv_b, dxc, seq)
    g["conv_w"] = _group_major_inv(g_conv_w, 1)
    g["conv_b"] = _group_major_inv(g_conv_b, 1)
    dxbc = _conv_bwd_in(dpre, conv_w, seq)
    ddt = jnp.pad(ddtr.transpose(1, 0, 2).reshape(t, N_HEADS), ((0, 0), (0, DT_PAD - N_HEADS))).astype(BF16)

    g_in_t = jnp.concatenate([
        _mm(dz, h, "tn", BF16, "g_in_z"),
        _group_major_inv(_mm(dxbc, h, "tn", BF16, "g_in_xbc"), 0),
        _mm(ddt, h, "tn", BF16, "g_in_dt")[:N_HEADS],
        _mm(dqkv, h, "tn", BF16, "g_in_qkv"),
        _mm(dgate, h, "tn", BF16, "g_in_gate")], axis=0)
    dh = _mm(dz, w_z, "nn", F32, "d_h_z")
    dh = _mm(dxbc, w_xbc, "nn", F32, "d_h_xbc", add=dh)
    dh = _mm(ddt, w_dt, "nn", F32, "d_h_dt", add=dh)
    dh = _mm(dgate, w_gate, "nn", F32, "d_h_gate", add=dh)
    dh, *scattered_in = _mm(dqkv, w_qkv, "nn", F32, "d_h_qkv", add=dh, side=overlap.scatter_in({"w_in_t": g_in_t}))
    dx, _, g["norm_mix"] = _rms_bwd(x, dh, wts["norm_mix"], dx1, "rms_mix_bwd")
    return loss[0, 0], dx.reshape(nb, seq, d), g, scattered, scattered_in


def kernel(x, norm_mix, w_in, b_gate, conv_w, conv_b, dt_bias, a_log, d_skip, ssm_norm, w_ssm_out, w_att_out, w_mix_out, norm_ffn, w_ffn_gate, w_ffn_up, w_ffn_down, norm_final, loss_target, m_norm_mix, m_w_in, m_b_gate, m_conv_w, m_conv_b, m_dt_bias, m_a_log, m_d_skip, m_ssm_norm, m_w_ssm_out, m_w_att_out, m_w_mix_out, m_norm_ffn, m_w_ffn_gate, m_w_ffn_up, m_w_ffn_down, m_norm_final, v_norm_mix, v_w_in, v_b_gate, v_conv_w, v_conv_b, v_dt_bias, v_a_log, v_d_skip, v_ssm_norm, v_w_ssm_out, v_w_att_out, v_w_mix_out, v_norm_ffn, v_w_ffn_gate, v_w_ffn_up, v_w_ffn_down, v_norm_final):
    names = ("norm_mix", "w_in", "b_gate", "conv_w", "conv_b", "dt_bias", "a_log", "d_skip", "ssm_norm", "w_ssm_out",
             "w_att_out", "w_mix_out", "norm_ffn", "w_ffn_gate", "w_ffn_up", "w_ffn_down", "norm_final")
    w_loc = dict(zip(names, (norm_mix, w_in, b_gate, conv_w, conv_b, dt_bias, a_log, d_skip, ssm_norm, w_ssm_out,
                             w_att_out, w_mix_out, norm_ffn, w_ffn_gate, w_ffn_up, w_ffn_down, norm_final)))
    m_loc = dict(zip(names, (m_norm_mix, m_w_in, m_b_gate, m_conv_w, m_conv_b, m_dt_bias, m_a_log, m_d_skip,
                             m_ssm_norm, m_w_ssm_out, m_w_att_out, m_w_mix_out, m_norm_ffn, m_w_ffn_gate,
                             m_w_ffn_up, m_w_ffn_down, m_norm_final)))
    v_loc = dict(zip(names, (v_norm_mix, v_w_in, v_b_gate, v_conv_w, v_conv_b, v_dt_bias, v_a_log, v_d_skip,
                             v_ssm_norm, v_w_ssm_out, v_w_att_out, v_w_mix_out, v_norm_ffn, v_w_ffn_gate,
                             v_w_ffn_up, v_w_ffn_down, v_norm_final)))
    two_d = lambda a: a.reshape(a.shape[-2:]) if a.ndim >= 2 else a.reshape(1, -1)
    w2 = {n: two_d(a) for n, a in w_loc.items()}
    chip = 2 * lax.axis_index("x") + lax.axis_index("y")
    c = lax.axis_index("c")

    wire_shapes = {n: _wire_shard(w2[n], n).shape for n in SHARDED}
    true_rows = {n: wire_shapes[n][0] * wire_shapes[n][1] // PACK_W for n in SHARDED}
    seg_rows = {n: _rows(wire_shapes[n][0] * wire_shapes[n][1]) for n in SHARDED}
    buckets = {"first": ("w_in",), "late": tuple(n for n in SHARDED if n != "w_in")}
    rows_of = {b: _padded_rows(sum(seg_rows[n] for n in ns)) for b, ns in buckets.items()}

    def pack_shards(b):
        packed = _pack_rows([_wire_shard(w2[n], n).astype(BF16) for n in buckets[b]], rows_of[b])
        return packed.reshape(2, rows_of[b] // 2, PACK_W)

    def unpack_full(gathered, b):
        wg, out, off = gathered.reshape(N_CHIPS, rows_of[b], PACK_W), {}, 0
        for n in buckets[b]:
            rows, cols = wire_shapes[n]
            out[_wire_name(n)] = wg[:, off:off + true_rows[n]].reshape(N_CHIPS * rows, cols)
            off += seg_rows[n]
        return out

    def pack_grads(g, b):
        sections = [_pack_rows([g[_wire_name(n)].reshape(N_CHIPS, true_rows[n], PACK_W)[k] for n in buckets[b]],
                               rows_of[b]) for k in range(N_CHIPS)]
        return jnp.stack(sections).reshape(N_CHIPS, 2, rows_of[b] // 2, PACK_W).transpose(1, 0, 2, 3)

    def chip_sums(g, b):
        g2 = pack_grads(g, b)
        return _add_own_half(g2, _swap_halves(g2, b), c, b)

    def finish(by_source, b):
        reduced = _join_halves(_sum_chips(by_source, b), b).reshape(rows_of[b], PACK_W)
        out, off = {}, 0
        for n in buckets[b]:
            wire = reduced[off:off + true_rows[n]].reshape(wire_shapes[n])
            out[n] = wire.T if n in COL_SHARDED else wire
            off += seg_rows[n]
        return out

    full = unpack_full(_gather_weights(pack_shards("first")), "first")
    for n in SMALL:
        full[n] = w2[n]
    overlap = _Overlap(_gather_side(pack_shards("late")), lambda outs: unpack_full(outs[0], "late"),
                       lambda g: _scatter_side(chip_sums(g, "late")), lambda g: _scatter_side(chip_sums(g, "first")))

    n_conv = w2["conv_w"].shape[1]
    placed = lax.dynamic_update_slice_in_dim(jnp.zeros((CONV_K, N_CHIPS * n_conv), F32), w2["conv_w"], chip * n_conv, 1)
    placed = jnp.where(c == 0, placed, 0.0)
    full["conv_w"] = _allreduce_small(_pack_rows([placed], _rows(int(placed.size))), "gather_conv_w").reshape(
        -1)[:placed.size].reshape(placed.shape)

    loss_sum, grad_x, g_full, scattered, scattered_in = _local_step(x, loss_target, full, overlap)
    loss = lax.psum(loss_sum, ("x", "y", "c"))

    g_shard = {}
    small_names = SMALL + ("conv_w",)
    small_flat = jnp.concatenate([g_full[n].reshape(-1) for n in small_names])
    small = _allreduce_small(_pack_rows([small_flat], _rows(int(small_flat.size))), "allreduce_small").reshape(-1)
    off = 0
    for n in small_names:
        size = int(g_full[n].size)
        g_shard[n] = small[off:off + size].reshape(g_full[n].shape)
        off += size
    g_shard["conv_w"] = lax.dynamic_slice_in_dim(g_shard["conv_w"], chip * n_conv, n_conv, 1)

    g_shard.update(finish(scattered[0], "late"))
    g_shard.update(finish(scattered_in[0], "first"))

    grads, deltas, new_m, new_v = [], [], [], []
    for n in names:
        shape = w_loc[n].shape
        as_rows = (lambda a: a) if len(shape) >= 2 else two_d
        gn = g_shard[n].reshape(as_rows(w_loc[n]).shape)
        d_, m_, v_ = _adamw(as_rows(w_loc[n]), gn, as_rows(m_loc[n]), as_rows(v_loc[n]), "adamw_" + n)
        grads.append(gn.reshape(shape))
        deltas.append(d_.reshape(shape))
        new_m.append(m_.reshape(shape))
        new_v.append(v_.reshape(shape))
    return (loss, grad_x, *grads, *deltas, *new_m, *new_v)
```

```python
import functools
from typing import Callable, NamedTuple, Optional

import jax
import jax.numpy as jnp
from jax import lax
from jax.experimental import pallas as pl
from jax.experimental.pallas import tpu as pltpu

F32 = jnp.float32
BF16 = jnp.bfloat16
SDS = jax.ShapeDtypeStruct
MESH = pl.DeviceIdType.MESH

D_MODEL = 1024
D_INNER = 2048
N_HEADS = 32
HEAD_P = 64
N_GROUPS = 4
HEADS_PER_GROUP = N_HEADS // N_GROUPS
D_STATE = 128
CONV_K = 4
CHUNK = 128
CONV_DIM = D_INNER + 2 * N_GROUPS * D_STATE
GROUP_W = D_INNER // N_GROUPS + 2 * D_STATE
ATT_HEADS = 12
ATT_D = 128
ATT_SLOTS = 4
ATT_W = ATT_SLOTS * ATT_D
ATT_DILATIONS = (1, 4, 16)
ATT_BLOCK = 128
QKV_DIM = 3 * ATT_HEADS * ATT_D
D_FF = 2816
DT_PAD = 128
ROPE_THETA = 10000.0
EPS = 1e-6
N_CHIPS = 4
LANES = 128

ADAM_LR = 0.001
ADAM_B1 = 0.9
ADAM_B2 = 0.999
ADAM_EPS = 1e-08
ADAM_WD = 0.01
ADAM_STEP = 10

VMEM_LIMIT = 48 * 1024 * 1024


def _cparams(semantics):
    return pltpu.CompilerParams(dimension_semantics=semantics, vmem_limit_bytes=VMEM_LIMIT)


def _pick(n, cap):
    best = None
    for t in range(LANES, min(n, cap) + 1, LANES):
        if n % t == 0:
            best = t
    return best or n


def _row_tile(rows, cap):
    best = None
    for t in range(8, min(rows, cap) + 1, 8):
        if rows % t == 0:
            best = t
    return best or rows


def _sigmoid(x):
    return 1.0 / (1.0 + jnp.exp(-x))


def _softplus(x):
    return jnp.maximum(x, 0.0) + jnp.log(1.0 + jnp.exp(-jnp.abs(x)))


def _dot(a, b):
    return jnp.dot(a, b, preferred_element_type=F32)


def _dot_nt(a, b):
    return lax.dot_general(a, b, (((1,), (1,)), ((), ())), preferred_element_type=F32)


def _dot_tn(a, b):
    return lax.dot_general(a, b, (((0,), (0,)), ((), ())), preferred_element_type=F32)


def _mm(a, b, mode, out_dtype, name, add=None, side=None):
    if mode == "nn":
        (m, k), (_, n) = a.shape, b.shape
    elif mode == "nt":
        (m, k), (n, _) = a.shape, b.shape
    else:
        (k, m), (_, n) = a.shape, b.shape
    tm, tn = _pick(m, 1536), _pick(n, 2048)
    tk = k if k <= 2048 else _pick(k, 2048)
    nk = k // tk
    dims = {"nn": ((1,), (0,)), "nt": ((1,), (1,)), "tn": ((0,), (0,))}[mode]

    def partial_product(a_ref, b_ref):
        return lax.dot_general(a_ref[...].astype(BF16), b_ref[...].astype(BF16), (dims, ((), ())),
                               preferred_element_type=F32)

    def body(*refs):
        a_ref, b_ref = refs[:2]
        c_ref = refs[2] if add is not None else None
        o_ref = refs[3] if add is not None else refs[2]

        def finish(r):
            if add is not None:
                r = r + c_ref[...].astype(F32)
            o_ref[...] = r.astype(out_dtype)

        if nk == 1:
            finish(partial_product(a_ref, b_ref))
            return
        acc = refs[-1]
        kk = pl.program_id(2)

        @pl.when(kk == 0)
        def _():
            acc[...] = partial_product(a_ref, b_ref)

        @pl.when((kk > 0) & (kk < nk - 1))
        def _():
            acc[...] += partial_product(a_ref, b_ref)

        @pl.when(kk == nk - 1)
        def _():
            finish(acc[...] + partial_product(a_ref, b_ref))

    a_spec = {"nn": pl.BlockSpec((tm, tk), lambda j, i, q: (i, q)),
              "nt": pl.BlockSpec((tm, tk), lambda j, i, q: (i, q)),
              "tn": pl.BlockSpec((tk, tm), lambda j, i, q: (q, i))}[mode]
    b_spec = {"nn": pl.BlockSpec((tk, tn), lambda j, i, q: (q, j)),
              "nt": pl.BlockSpec((tn, tk), lambda j, i, q: (j, q)),
              "tn": pl.BlockSpec((tk, tn), lambda j, i, q: (q, j))}[mode]
    o_spec = pl.BlockSpec((tm, tn), lambda j, i, q: (i, j))
    ins, specs = [a, b], [a_spec, b_spec]
    if add is not None:
        ins.append(add)
        specs.append(o_spec)
    acc = [pltpu.VMEM((tm, tn), F32)] if nk > 1 else []
    grid = (n // tn, m // tm, nk)
    if side is None:
        return pl.pallas_call(
            body, name=name, grid=grid, in_specs=specs, out_specs=o_spec, out_shape=SDS((m, n), out_dtype),
            scratch_shapes=acc, compiler_params=_cparams(("parallel", "parallel", "arbitrary")))(*ins)
    return pl.pallas_call(
        _attach_side(body, len(ins), 1, side, grid), name=name, grid=grid,
        in_specs=specs + [ANY] * len(side.ins), out_specs=[o_spec] + [ANY] * len(side.out_shapes),
        out_shape=[SDS((m, n), out_dtype)] + list(side.out_shapes), scratch_shapes=acc + list(side.scratch),
        compiler_params=_cparams(("arbitrary", "arbitrary", "arbitrary")))(*ins, *side.ins)


def _rw(name, fn, nsteps, ins, outs, n_acc=0):
    n_in, n_out = len(ins), len(outs)

    def body(*refs):
        i = pl.program_id(0)
        vals = fn(i, *refs[:n_in])
        for q, (r, v) in enumerate(zip(refs[n_in:], vals)):
            if q < n_out - n_acc:
                r[...] = v.astype(r.dtype)
            else:
                @pl.when(i == 0)
                def _(r=r):
                    r[...] = jnp.zeros_like(r)

                r[...] += v

    return pl.pallas_call(
        body, name=name, grid=(nsteps,), in_specs=[s for _, s in ins], out_specs=[s for _, s in outs],
        out_shape=[o for o, _ in outs], compiler_params=_cparams(("arbitrary",)))(*[a for a, _ in ins])


def _rs(tm, w, cb=0):
    return pl.BlockSpec((tm, w), lambda i: (i, cb))


def _fs(shape):
    nd = len(shape)
    return pl.BlockSpec(shape, lambda i: (0,) * nd)


def _colsum(v):
    return jnp.sum(v, axis=0, keepdims=True)


def _rms_fwd(x, g, name):
    t, d = x.shape
    tm = 512

    def fn(i, x_ref, g_ref):
        xv = x_ref[...]
        r = lax.rsqrt(jnp.mean(xv * xv, axis=-1, keepdims=True) + EPS)
        return [xv * r * g_ref[...]]

    return _rw(name, fn, t // tm, [(x, _rs(tm, d)), (g, _fs((1, d)))], [(SDS((t, d), BF16), _rs(tm, d))])[0]


def _rms_bwd(x, dh, g, dres, name):
    t, d = x.shape
    tm = 512

    def fn(i, x_ref, dh_ref, g_ref, dres_ref):
        xv = x_ref[...]
        r = lax.rsqrt(jnp.mean(xv * xv, axis=-1, keepdims=True) + EPS)
        xhat = xv * r
        dhv = dh_ref[...]
        dxhat = dhv * g_ref[...]
        dx = dres_ref[...] + r * (dxhat - xhat * jnp.mean(dxhat * xhat, axis=-1, keepdims=True))
        return [dx, dx, _colsum(dhv * xhat)]

    return _rw(name, fn, t // tm,
               [(x, _rs(tm, d)), (dh, _rs(tm, d)), (g, _fs((1, d))), (dres, _rs(tm, d))],
               [(SDS((t, d), F32), _rs(tm, d)), (SDS((t, d), BF16), _rs(tm, d)), (SDS((1, d), F32), _fs((1, d)))],
               n_acc=1)


def _final_fwd_bwd(x2, target, g):
    t, d = x2.shape
    tm = 512

    def fn(i, x_ref, t_ref, g_ref):
        xv = x_ref[...]
        gv = g_ref[...]
        r = lax.rsqrt(jnp.mean(xv * xv, axis=-1, keepdims=True) + EPS)
        xhat = xv * r
        diff = xhat * gv - t_ref[...]
        lsum = 0.5 * jnp.sum(jnp.sum(diff * diff, axis=-1, keepdims=True) * (1.0 / d), axis=0, keepdims=True)
        dy = diff * (1.0 / d)
        dxhat = dy * gv
        dx = r * (dxhat - xhat * jnp.mean(dxhat * xhat, axis=-1, keepdims=True))
        return [dx, dx, _colsum(dy * xhat), lsum]

    return _rw("final_norm_loss", fn, t // tm,
               [(x2, _rs(tm, d)), (target, _rs(tm, d)), (g, _fs((1, d)))],
               [(SDS((t, d), F32), _rs(tm, d)), (SDS((t, d), BF16), _rs(tm, d)), (SDS((1, d), F32), _fs((1, d))),
                (SDS((1, 1), F32), _fs((1, 1)))], n_acc=2)


CONV_TS = 512
CONV_HALO = 8


def _conv_specs(seq, c):
    ts, tc = CONV_TS, GROUP_W
    hb = ts // CONV_HALO
    u_spec = pl.BlockSpec((ts, tc), lambda j, i: (i, j))
    prev_spec = pl.BlockSpec((CONV_HALO, tc), lambda j, i: (jnp.maximum(i * hb - 1, 0), j))
    w_spec = pl.BlockSpec((CONV_K, tc), lambda j, i: (0, j))
    b_spec = pl.BlockSpec((1, tc), lambda j, i: (0, j))
    return u_spec, prev_spec, w_spec, b_spec


CONV_ROWS = 16


def _conv_pre(i, seq, u_ref, prev_ref, w_ref, b_ref, ext):
    ts = CONV_TS
    first = (i % (seq // ts)) == 0
    ext[0:CONV_HALO, :] = jnp.where(first, 0.0, prev_ref[...])
    ext[CONV_HALO:, :] = u_ref[...]
    acc = jnp.broadcast_to(b_ref[...], u_ref.shape)
    for q in range(CONV_K):
        acc = acc + w_ref[q:q + 1, :] * ext[pl.ds(CONV_HALO - CONV_K + 1 + q, ts), :]
    return acc


def _conv_fwd(u, w, b, seq):
    t, c = u.shape
    ts, tc = CONV_TS, GROUP_W
    u_spec, prev_spec, w_spec, b_spec = _conv_specs(seq, c)

    def body(u_ref, prev_ref, w_ref, b_ref, o_ref, ext):
        pre = _conv_pre(pl.program_id(1), seq, u_ref, prev_ref, w_ref, b_ref, ext)
        o_ref[...] = pre * _sigmoid(pre)

    return pl.pallas_call(
        body, name="conv_fwd", grid=(c // tc, t // ts), in_specs=[u_spec, prev_spec, w_spec, b_spec],
        out_specs=u_spec, out_shape=SDS((t, c), F32), scratch_shapes=[pltpu.VMEM((ts + CONV_HALO, tc), F32)],
        compiler_params=_cparams(("parallel", "arbitrary")))(u, u, w, b)


def _conv_bwd_pre(u, w, b, dxc, seq):
    t, c = u.shape
    ts, tc = CONV_TS, GROUP_W
    u_spec, prev_spec, w_spec, b_spec = _conv_specs(seq, c)

    def body(u_ref, prev_ref, w_ref, b_ref, d_ref, dpre_ref, dw_ref, db_ref, ext):
        i = pl.program_id(1)
        pre = _conv_pre(i, seq, u_ref, prev_ref, w_ref, b_ref, ext)
        sg = _sigmoid(pre)
        dpre = d_ref[...] * sg * (1.0 + pre * (1.0 - sg))
        dpre_ref[...] = dpre

        @pl.when(i == 0)
        def _():
            dw_ref[...] = jnp.zeros_like(dw_ref)
            db_ref[...] = jnp.zeros_like(db_ref)

        db_ref[...] += _colsum(dpre)
        for q in range(CONV_K):
            dw_ref[q:q + 1, :] += _colsum(dpre * ext[pl.ds(CONV_HALO - CONV_K + 1 + q, ts), :])

    return pl.pallas_call(
        body, name="conv_bwd_pre", grid=(c // tc, t // ts),
        in_specs=[u_spec, prev_spec, w_spec, b_spec, u_spec], out_specs=[u_spec, w_spec, b_spec],
        out_shape=[SDS((t, c), F32), SDS((CONV_K, c), F32), SDS((1, c), F32)],
        scratch_shapes=[pltpu.VMEM((ts + CONV_HALO, tc), F32)],
        compiler_params=_cparams(("parallel", "arbitrary")))(u, u, w, b, dxc)


def _conv_bwd_in(dpre, w, seq):
    t, c = dpre.shape
    ts, tc = CONV_TS, GROUP_W
    hb = ts // CONV_HALO
    last = t // CONV_HALO - 1
    d_spec = pl.BlockSpec((ts, tc), lambda j, i: (i, j))
    next_spec = pl.BlockSpec((CONV_HALO, tc), lambda j, i: (jnp.minimum((i + 1) * hb, last), j))
    w_spec = pl.BlockSpec((CONV_K, tc), lambda j, i: (0, j))

    def body(d_ref, next_ref, w_ref, o_ref, ext):
        i = pl.program_id(1)
        nts = seq // ts
        is_last = (i % nts) == nts - 1
        ext[0:ts, :] = d_ref[...]
        ext[ts:, :] = jnp.where(is_last, 0.0, next_ref[...])
        wv = w_ref[...]

        def rows(j, carry):
            r0 = pl.multiple_of(j * CONV_ROWS, CONV_ROWS)
            blk = ext[pl.ds(r0, CONV_ROWS + CONV_HALO), :]
            acc = wv[CONV_K - 1:CONV_K] * blk[0:CONV_ROWS]
            for q in range(CONV_K - 1):
                acc = acc + wv[q:q + 1] * blk[CONV_K - 1 - q:CONV_K - 1 - q + CONV_ROWS]
            o_ref[pl.ds(r0, CONV_ROWS), :] = acc.astype(o_ref.dtype)
            return carry

        lax.fori_loop(0, ts // CONV_ROWS, rows, 0)

    return pl.pallas_call(
        body, name="conv_bwd_in", grid=(c // tc, t // ts), in_specs=[d_spec, next_spec, w_spec],
        out_specs=d_spec, out_shape=SDS((t, c), BF16), scratch_shapes=[pltpu.VMEM((ts + CONV_HALO, tc), F32)],
        compiler_params=_cparams(("parallel", "arbitrary")))(dpre, dpre, w)


def _split3(v):
    hi = v.astype(BF16)
    r1 = v - hi.astype(F32)
    mid = r1.astype(BF16)
    lo = (r1 - mid.astype(F32)).astype(BF16)
    return hi, mid, lo


def _ssd_prelude(dtr_ref, dtrt_ref, bias_ref, biast_ref, a_ref, at_ref):
    dt = _softplus(dtr_ref[...] + bias_ref[...])
    dtt = _softplus(dtrt_ref[...] + biast_ref[...])
    ri = lax.broadcasted_iota(jnp.int32, (CHUNK, CHUNK), 0)
    ci = lax.broadcasted_iota(jnp.int32, (CHUNK, CHUNK), 1)
    lower = ri >= ci
    upper = ri <= ci
    lower_b = jnp.where(lower, 1.0, 0.0).astype(BF16)
    upper_b = jnp.where(upper, 1.0, 0.0).astype(BF16)
    acs = sum(_dot(lower_b, p) for p in _split3(dt * a_ref[...]))
    acst = sum(_dot(p, upper_b) for p in _split3(dtt * at_ref[...]))
    return dt, acs, acst, lower, upper, lower_b, upper_b


SSD_FWD_GPS = 2
SSD_BWD_GPS = 1


def _ssd_specs(seq, gps):
    nc = seq // CHUNK
    hg = HEADS_PER_GROUP
    fwd = lambda c: c
    rev = lambda c: nc - 1 - c

    def specs(cc):
        return dict(
            xc=pl.BlockSpec((CHUNK, gps * GROUP_W), lambda g, b, c: (b * nc + cc(c), g)),
            y=pl.BlockSpec((CHUNK, gps * hg * HEAD_P), lambda g, b, c: (b * nc + cc(c), g)),
            dtr=pl.BlockSpec((gps, CHUNK, hg), lambda g, b, c: (g, b * nc + cc(c), 0)),
            dtrt=pl.BlockSpec((gps, None, hg, CHUNK), lambda g, b, c: (g, b, 0, cc(c))),
            prow=pl.BlockSpec((gps, 1, hg), lambda g, b, c: (g, 0, 0)),
            pcol=pl.BlockSpec((gps, hg, 1), lambda g, b, c: (g, 0, 0)),
            st=pl.BlockSpec((gps, None, None, D_STATE, hg * HEAD_P), lambda g, b, c: (g, b, cc(c), 0, 0)),
        )

    return specs(fwd), specs(rev)


def _group_views(refs, lane_widths, gi):
    return [r.at[:, gi * w:(gi + 1) * w] if w else r.at[gi] for r, w in zip(refs, lane_widths)]


def _head_maps():
    hw = HEADS_PER_GROUP * HEAD_P
    shift = HEAD_P.bit_length() - 1
    hj = lax.broadcasted_iota(jnp.int32, (HEADS_PER_GROUP, hw), 0)
    lq = jnp.right_shift(lax.broadcasted_iota(jnp.int32, (HEADS_PER_GROUP, hw), 1), shift)
    spread = jnp.where(hj == lq, 1.0, 0.0).astype(BF16)
    rq = jnp.right_shift(lax.broadcasted_iota(jnp.int32, (hw, LANES), 0), shift)
    cj = lax.broadcasted_iota(jnp.int32, (hw, LANES), 1)
    gather = jnp.where(rq == cj, 1.0, 0.0).astype(BF16)
    return spread, gather


def _dot01(v, m01):
    hi, mid, _ = _split3(v)
    return _dot(hi, m01) + _dot(mid, m01)


class _Side(NamedTuple):
    ins: tuple
    out_shapes: tuple
    scratch: tuple
    first: Callable
    mid: Optional[Callable]
    last: Callable


NO_SIDE = _Side((), (), (), lambda *refs: None, None, lambda *refs: None)


def _attach_side(body, n_in, n_out, side, grid):
    si, so, ss = len(side.ins), len(side.out_shapes), len(side.scratch)

    def wrapped(*refs):
        ins, s_in = refs[:n_in], refs[n_in:n_in + si]
        outs = refs[n_in + si:n_in + si + n_out]
        s_out = refs[n_in + si + n_out:n_in + si + n_out + so]
        rest = refs[n_in + si + n_out + so:]
        scr, s_scr = rest[:len(rest) - ss], rest[len(rest) - ss:]
        ids = [pl.program_id(a) for a in range(len(grid))]
        inner_first = functools.reduce(lambda p, q: p & q, [i == 0 for i in ids[1:]], ids[0] >= 0)
        at_last = functools.reduce(lambda p, q: p & q, [i == n - 1 for i, n in zip(ids, grid)])

        @pl.when((ids[0] == 0) & inner_first)
        def _():
            side.first(s_in, s_out, s_scr)

        if side.mid is not None:
            outer_last = functools.reduce(lambda p, q: p & q, [i == n - 1 for i, n in zip(ids[:-1], grid[:-1])])

            @pl.when(outer_last & (ids[-1] == 0))
            def _():
                side.mid(s_in, s_out, s_scr)

        body(*ins, *outs, *scr)

        @pl.when(at_last)
        def _():
            side.last(s_in, s_out, s_scr)

    return wrapped


def _ssd_fwd(xc, dtr, dtrt, bias, biast, a, at, dskip, nb, seq, side):
    t = xc.shape[0]
    nc = seq // CHUNK
    hg = HEADS_PER_GROUP
    hw = hg * HEAD_P
    gps = SSD_FWD_GPS
    grid = (N_GROUPS // gps, nb, nc)
    sp, _ = _ssd_specs(seq, gps)

    def body(*refs):
        for gi in range(gps):
            one_group(*_group_views(refs, (GROUP_W, 0, 0, 0, 0, 0, 0, 0, hw, 0, 0), gi))

    def one_group(xc_ref, dtr_ref, dtrt_ref, bias_ref, biast_ref, a_ref, at_ref, d_ref, y_ref, sin_ref, st):
        @pl.when(pl.program_id(2) == 0)
        def _():
            st[...] = jnp.zeros_like(st)

        s_in = st[...]
        sin_ref[...] = s_in
        dt, acs, acst, lower, _, _, _ = _ssd_prelude(dtr_ref, dtrt_ref, bias_ref, biast_ref, a_ref, at_ref)
        spread, _ = _head_maps()
        x = xc_ref[...]
        xs = x[:, :hw]
        b16 = x[:, hw:hw + D_STATE].astype(BF16)
        c16 = x[:, hw + D_STATE:].astype(BF16)
        cb = _dot_nt(c16, b16)
        last = acs[CHUNK - 1:CHUNK, :]
        e_x = _dot01(jnp.exp(acs), spread)
        dec_x = _dot01(jnp.exp(last - acs), spread)
        tot_x = e_x[CHUNK - 1:CHUNK, :]
        d_x = _dot01(jnp.broadcast_to(d_ref[...], (8, hg)), spread)[0:1, :]
        xdtf = xs * _dot01(dt, spread)
        xdt16 = xdtf.astype(BF16)
        yoff = e_x * _dot(c16, s_in.astype(BF16))
        st[...] = tot_x * s_in + _dot_tn(b16, (dec_x * xdtf).astype(BF16))
        parts = []
        for j in range(hg):
            decay = jnp.exp(jnp.where(lower, acs[:, j:j + 1] - acst[j:j + 1, :], -jnp.inf))
            parts.append(_dot((cb * decay).astype(BF16), xdt16[:, HEAD_P * j:HEAD_P * (j + 1)]))
        y_ref[...] = jnp.concatenate(parts, axis=-1) + yoff + d_x * xs

    return pl.pallas_call(
        _attach_side(body, 8, 2, side, grid), name="ssd_fwd", grid=grid,
        in_specs=[sp["xc"], sp["dtr"], sp["dtrt"], sp["prow"], sp["pcol"], sp["prow"], sp["pcol"], sp["prow"]]
        + [ANY] * len(side.ins),
        out_specs=[sp["y"], sp["st"]] + [ANY] * len(side.out_shapes),
        out_shape=[SDS((t, D_INNER), F32), SDS((N_GROUPS, nb, nc, D_STATE, hw), F32)] + list(side.out_shapes),
        scratch_shapes=[pltpu.VMEM((gps, D_STATE, hw), F32)] + list(side.scratch),
        compiler_params=_cparams(("arbitrary", "arbitrary", "arbitrary")))(
            xc, dtr, dtrt, bias, biast, a, at, dskip, *side.ins)


def _ssd_bwd(xc, dtr, dtrt, bias, biast, a, at, dskip, states, dy, nb, seq, side):
    t = xc.shape[0]
    nc = seq // CHUNK
    hg = HEADS_PER_GROUP
    hw = hg * HEAD_P
    gps = SSD_BWD_GPS
    grid = (N_GROUPS // gps, nb, nc)
    _, sp = _ssd_specs(seq, gps)

    def body(*refs):
        for gi in range(gps):
            one_group(*_group_views(refs, (GROUP_W, 0, 0, 0, 0, 0, 0, 0, 0, hw, GROUP_W, 0, 0, 0, 0, 0), gi))

    def one_group(xc_ref, dtr_ref, dtrt_ref, bias_ref, biast_ref, a_ref, at_ref, d_ref, sin_ref, dy_ref,
                  dxc_ref, ddtr_ref, gbias_ref, ga_ref, gd_ref, ds):
        first = (pl.program_id(1) == 0) & (pl.program_id(2) == 0)

        @pl.when(pl.program_id(2) == 0)
        def _():
            ds[...] = jnp.zeros_like(ds)

        @pl.when(first)
        def _():
            gbias_ref[...] = jnp.zeros_like(gbias_ref)
            ga_ref[...] = jnp.zeros_like(ga_ref)
            gd_ref[...] = jnp.zeros_like(gd_ref)

        dt, acs, acst, lower, upper, _, upper_b = _ssd_prelude(dtr_ref, dtrt_ref, bias_ref, biast_ref, a_ref, at_ref)
        spread, gather = _head_maps()
        x = xc_ref[...]
        dy = dy_ref[...]
        xs = x[:, :hw]
        b16 = x[:, hw:hw + D_STATE].astype(BF16)
        c16 = x[:, hw + D_STATE:].astype(BF16)
        dy16 = dy.astype(BF16)
        cb = _dot_nt(c16, b16)
        cbt = _dot_nt(b16, c16)
        last = acs[CHUNK - 1:CHUNK, :]
        e8 = jnp.exp(acs)
        dec8 = jnp.exp(last - acs)
        e_x = _dot01(e8, spread)
        dec_x = _dot01(dec8, spread)
        tot_x = e_x[CHUNK - 1:CHUNK, :]
        dt_x = _dot01(dt, spread)
        d_x = _dot01(jnp.broadcast_to(d_ref[...], (8, hg)), spread)[0:1, :]
        xdtf = xs * dt_x
        xdt16 = xdtf.astype(BF16)
        s_in = sin_ref[...]
        s16 = s_in.astype(BF16)
        ds_out = ds[...]
        ds16 = ds_out.astype(BF16)
        bds = _dot(b16, ds16)
        cs = _dot(c16, s16)
        edy16 = (e_x * dy).astype(BF16)
        ds[...] = tot_x * ds_out + _dot_tn(c16, edy16)
        lane8 = lax.broadcasted_iota(jnp.int32, (CHUNK, hg), 1)
        row8 = lax.broadcasted_iota(jnp.int32, (CHUNK, hg), 0)
        dacs8 = jnp.zeros((CHUNK, hg), F32)
        acc_m = jnp.zeros((CHUNK, CHUNK), F32)
        acc_mt = jnp.zeros((CHUNK, CHUNK), F32)
        dx_parts = []
        for j in range(hg):
            sl = slice(HEAD_P * j, HEAD_P * (j + 1))
            col = acs[:, j:j + 1]
            row = acst[j:j + 1, :]
            decay = jnp.exp(jnp.where(lower, col - row, -jnp.inf))
            decayt = jnp.exp(jnp.where(upper, row - col, -jnp.inf))
            wm = _dot_nt(dy16[:, sl], xdt16[:, sl]) * decay
            wmt = _dot_nt(xdt16[:, sl], dy16[:, sl]) * decayt
            acc_m = acc_m + wm
            acc_mt = acc_mt + wmt
            dacs8 = dacs8 + jnp.where(lane8 == j, jnp.sum(wm * cb, axis=-1, keepdims=True)
                                      - jnp.sum(wmt * cbt, axis=-1, keepdims=True), 0.0)
            dx_parts.append(_dot((cbt * decayt).astype(BF16), dy16[:, sl]))
        dx = jnp.concatenate(dx_parts, axis=-1) + dec_x * bds
        dxc_ref[:, :hw] = dx * dt_x + d_x * dy
        dxc_ref[:, hw:hw + D_STATE] = _dot(acc_mt.astype(BF16), c16) + _dot_nt((dec_x * xdtf).astype(BF16), ds16)
        dxc_ref[:, hw + D_STATE:] = _dot(acc_m.astype(BF16), b16) + _dot_nt(edy16, s16)
        dtot_rows = jnp.broadcast_to(_colsum(ds_out * s_in), (8, hw))
        sums = _dot01(jnp.concatenate([dy * cs, xdtf * bds, dx * xs, dy * xs, dtot_rows], axis=0), gather)
        de8 = sums[0:CHUNK, :hg]
        ddec8 = sums[CHUNK:2 * CHUNK, :hg]
        ddtx8 = sums[2 * CHUNK:3 * CHUNK, :hg]
        gd8 = _colsum(sums[3 * CHUNK:4 * CHUNK, :hg])
        dtot8 = sums[4 * CHUNK:4 * CHUNK + 1, :hg]
        extra = _colsum(ddec8 * dec8) + dtot8 * e8[CHUNK - 1:CHUNK, :]
        dacs8 = dacs8 + de8 * e8 - ddec8 * dec8 + jnp.where(row8 == CHUNK - 1, extra, 0.0)
        da = sum(_dot(upper_b, p) for p in _split3(dacs8))
        av = a_ref[...]
        ddt = da * av + ddtx8
        ddtr = ddt * _sigmoid(dtr_ref[...] + bias_ref[...])
        ddtr_ref[...] = ddtr
        gbias_ref[...] += _colsum(ddtr)
        ga_ref[...] += _colsum(da * dt) * av
        gd_ref[...] += gd8

    return pl.pallas_call(
        _attach_side(body, 10, 5, side, grid), name="ssd_bwd", grid=grid,
        in_specs=[sp["xc"], sp["dtr"], sp["dtrt"], sp["prow"], sp["pcol"], sp["prow"], sp["pcol"], sp["prow"],
                  sp["st"], sp["y"]] + [ANY] * len(side.ins),
        out_specs=[sp["xc"], sp["dtr"], sp["prow"], sp["prow"], sp["prow"]] + [ANY] * len(side.out_shapes),
        out_shape=[SDS((t, N_GROUPS * GROUP_W), F32), SDS((N_GROUPS, t, hg), F32)]
        + [SDS((N_GROUPS, 1, hg), F32)] * 3 + list(side.out_shapes),
        scratch_shapes=[pltpu.VMEM((gps, D_STATE, hw), F32)] + list(side.scratch),
        compiler_params=_cparams(("arbitrary", "arbitrary", "arbitrary")))(
            xc, dtr, dtrt, bias, biast, a, at, dskip, states, dy, *side.ins)


def _group_bcast(v, width, fn):
    parts = []
    for q in range(v.shape[-1] // width):
        s = fn(v[:, q * width:(q + 1) * width])
        parts.append(jnp.broadcast_to(s, (v.shape[0], width)))
    return jnp.concatenate(parts, axis=-1)


def _gate_norm_fwd(y, z, g):
    t, d = y.shape
    tm = 256
    gw = d // N_GROUPS

    def fn(i, y_ref, z_ref, g_ref):
        zv = z_ref[...].astype(F32)
        u = y_ref[...] * (zv * _sigmoid(zv))
        r = lax.rsqrt(_group_bcast(u * u, gw, lambda p: jnp.mean(p, axis=-1, keepdims=True)) + EPS)
        return [u * r * g_ref[...]]

    return _rw("gate_norm_fwd", fn, t // tm, [(y, _rs(tm, d)), (z, _rs(tm, d)), (g, _fs((1, d)))],
               [(SDS((t, d), BF16), _rs(tm, d))])[0]


def _gate_norm_bwd(y, z, g, dyn):
    t, d = y.shape
    tm = 256
    gw = d // N_GROUPS

    def fn(i, y_ref, z_ref, g_ref, dyn_ref):
        zv = z_ref[...].astype(F32)
        yv = y_ref[...]
        sg = _sigmoid(zv)
        sz = zv * sg
        u = yv * sz
        r = lax.rsqrt(_group_bcast(u * u, gw, lambda p: jnp.mean(p, axis=-1, keepdims=True)) + EPS)
        uhat = u * r
        dv = dyn_ref[...].astype(F32)
        duhat = dv * g_ref[...]
        du = r * (duhat - uhat * _group_bcast(duhat * uhat, gw, lambda p: jnp.mean(p, axis=-1, keepdims=True)))
        dz = du * yv * sg * (1.0 + zv * (1.0 - sg))
        return [du * sz, dz, _colsum(dv * uhat)]

    return _rw("gate_norm_bwd", fn, t // tm,
               [(y, _rs(tm, d)), (z, _rs(tm, d)), (g, _fs((1, d))), (dyn, _rs(tm, d))],
               [(SDS((t, d), F32), _rs(tm, d)), (SDS((t, d), BF16), _rs(tm, d)), (SDS((1, d), F32), _fs((1, d)))],
               n_acc=1)


def _rope_tables(seq):
    half = ATT_D // 2
    inv = ROPE_THETA ** (-jnp.arange(half, dtype=F32) / half)
    ang = jnp.arange(seq, dtype=F32)[:, None] * inv[None, :]
    cos, sin = jnp.cos(ang), jnp.sin(ang)
    return jnp.concatenate([cos, cos], axis=-1), jnp.concatenate([-sin, sin], axis=-1)


ATT_TILE = 512
ATT_QB = 4


def _strided_spec(r, mtiles):
    return pl.BlockSpec((None, r, None, ATT_TILE // r, ATT_W), lambda i: (i // mtiles, 0, i % mtiles, 0, 0))


def _strided_shape(nb, r, mtiles, dtype):
    return SDS((nb, r, mtiles, ATT_TILE // r, ATT_W), dtype)


def _to_strided(val, out_ref, lanes, r, sc):
    if r == 1:
        out_ref[0, :, lanes] = val.astype(out_ref.dtype)
        return
    sc[...] = val
    for rr in range(r):
        out_ref[rr, :, lanes] = sc[pl.ds(rr, ATT_TILE // r, stride=r), :].astype(out_ref.dtype)


def _from_strided(in_ref, lanes, r, sc):
    if r == 1:
        return in_ref[0, :, lanes].astype(F32)
    for rr in range(r):
        sc[pl.ds(rr, ATT_TILE // r, stride=r), :] = in_ref[rr, :, lanes].astype(F32)
    return sc[...]


def _rope_fwd(qkv, cos, sin, nb, seq):
    t = qkv.shape[0]
    tm = ATT_TILE
    mtiles = seq // tm
    w = ATT_HEADS * ATT_D
    tab = pl.BlockSpec((tm, ATT_D), lambda i: (i % mtiles, 0))
    ng = len(ATT_DILATIONS)

    def body(q_ref, k_ref, v_ref, cos_ref, sin_ref, *rest):
        outs, sc = rest[:3 * ng], rest[3 * ng]
        c, s = cos_ref[...], sin_ref[...]
        for which, ref in enumerate((q_ref, k_ref, v_ref)):
            for h in range(ATT_HEADS):
                g, slot = divmod(h, ATT_SLOTS)
                p = ref[:, h * ATT_D:(h + 1) * ATT_D].astype(F32)
                if which < 2:
                    p = p * c + pltpu.roll(p, ATT_D // 2, 1) * s
                _to_strided(p, outs[which * ng + g], slice(slot * ATT_D, (slot + 1) * ATT_D), ATT_DILATIONS[g], sc)

    out_specs = [_strided_spec(r, mtiles) for _ in range(3) for r in ATT_DILATIONS]
    out_shape = [_strided_shape(nb, r, mtiles, BF16) for _ in range(3) for r in ATT_DILATIONS]
    outs = pl.pallas_call(
        body, name="rope_fwd", grid=(t // tm,),
        in_specs=[_rs(tm, w, 0), _rs(tm, w, 1), _rs(tm, w, 2), tab, tab], out_specs=out_specs, out_shape=out_shape,
        scratch_shapes=[pltpu.VMEM((tm, ATT_D), F32)], compiler_params=_cparams(("arbitrary",)))(
            qkv, qkv, qkv, cos, sin)
    flat = [o.reshape(t, ATT_W) for o in outs]
    return flat[0:ng], flat[ng:2 * ng], flat[2 * ng:]


def _rope_bwd(dq, dk, dv, cos, sin, nb, seq):
    t = dq[0].shape[0]
    tm = ATT_TILE
    mtiles = seq // tm
    w = ATT_HEADS * ATT_D
    tab = pl.BlockSpec((tm, ATT_D), lambda i: (i % mtiles, 0))
    ng = len(ATT_DILATIONS)

    def body(*refs):
        ins, (cos_ref, sin_ref, o_ref, sc) = refs[:3 * ng], refs[3 * ng:]
        c, s = cos_ref[...], sin_ref[...]
        for which in range(3):
            for h in range(ATT_HEADS):
                g, slot = divmod(h, ATT_SLOTS)
                p = _from_strided(ins[which * ng + g], slice(slot * ATT_D, (slot + 1) * ATT_D), ATT_DILATIONS[g], sc)
                if which < 2:
                    p = p * c - pltpu.roll(p, ATT_D // 2, 1) * s
                o_ref[:, which * w + h * ATT_D:which * w + (h + 1) * ATT_D] = p.astype(o_ref.dtype)

    views = [a.reshape(nb, r, mtiles, tm // r, ATT_W) for grp in (dq, dk, dv) for a, r in zip(grp, ATT_DILATIONS)]
    return pl.pallas_call(
        body, name="rope_bwd", grid=(t // tm,),
        in_specs=[_strided_spec(r, mtiles) for _ in range(3) for r in ATT_DILATIONS] + [tab, tab],
        out_specs=_rs(tm, 3 * w), out_shape=SDS((t, 3 * w), BF16),
        scratch_shapes=[pltpu.VMEM((tm, ATT_D), F32)], compiler_params=_cparams(("arbitrary",)))(*views, cos, sin)


def _att_masks():
    ri = lax.broadcasted_iota(jnp.int32, (ATT_BLOCK, ATT_BLOCK), 0)
    ci = lax.broadcasted_iota(jnp.int32, (ATT_BLOCK, ATT_BLOCK), 1)
    return ci <= ri, ci >= ri


def _att_fwd(q, k, v, g, seq):
    t, w = q.shape
    rows = ATT_QB * ATT_BLOCK
    nbs = seq // ATT_DILATIONS[g] // ATT_BLOCK
    scale = ATT_D ** -0.5
    cur = pl.BlockSpec((rows, w), lambda n: (n, 0))
    prev = pl.BlockSpec((ATT_BLOCK, w), lambda n: (jnp.maximum(n * ATT_QB - 1, 0), 0))

    def body(q_ref, kc_ref, kp_ref, vc_ref, vp_ref, o_ref, lse_ref):
        mcur, mprev = _att_masks()
        for i in range(ATT_QB):
            blk = pl.program_id(0) * ATT_QB + i
            mask = jnp.concatenate([mprev & ((blk % nbs) != 0), mcur], axis=-1)
            own = slice(i * ATT_BLOCK, (i + 1) * ATT_BLOCK)
            for h in range(ATT_SLOTS):
                sl = slice(h * ATT_D, (h + 1) * ATT_D)
                if i == 0:
                    keys = jnp.concatenate([kp_ref[:, sl], kc_ref[own, sl]], axis=0)
                    vals = jnp.concatenate([vp_ref[:, sl], vc_ref[own, sl]], axis=0)
                else:
                    both = slice((i - 1) * ATT_BLOCK, (i + 1) * ATT_BLOCK)
                    keys, vals = kc_ref[both, sl], vc_ref[both, sl]
                s = jnp.where(mask, _dot_nt(q_ref[own, sl], keys) * scale, -jnp.inf)
                m = jnp.max(s, axis=-1, keepdims=True)
                p = jnp.exp(s - m)
                den = jnp.sum(p, axis=-1, keepdims=True)
                o_ref[own, sl] = _dot(p.astype(BF16), vals) / den
                lse_ref[own, sl] = jnp.broadcast_to(m + jnp.log(den), (ATT_BLOCK, ATT_D))

    return pl.pallas_call(
        body, name=f"att_fwd_{g}", grid=(t // rows,), in_specs=[cur, cur, prev, cur, prev], out_specs=[cur, cur],
        out_shape=[SDS((t, w), F32), SDS((t, w), F32)],
        compiler_params=_cparams(("arbitrary",)))(q, k, k, v, v)


def _att_bwd(q, k, v, do, lse, dlt, g, seq):
    t, w = q.shape
    nblk = t // ATT_BLOCK
    rows = ATT_QB * ATT_BLOCK
    nbs = seq // ATT_DILATIONS[g] // ATT_BLOCK
    scale = ATT_D ** -0.5
    cur = pl.BlockSpec((rows, w), lambda n: (n, 0))
    nxt = pl.BlockSpec((ATT_BLOCK, w), lambda n: (jnp.minimum((n + 1) * ATT_QB, nblk - 1), 0))

    def body(qc_ref, qn_ref, k_ref, v_ref, doc_ref, don_ref, lsec_ref, lsen_ref, dltc_ref, dltn_ref,
             dq_ref, dk_ref, dv_ref, carry):
        n = pl.program_id(0)

        @pl.when(n == 0)
        def _():
            carry[...] = jnp.zeros_like(carry)

        mcur, mprev = _att_masks()

        def pair(cur_ref, nxt_ref, i, sl):
            if i + 1 < ATT_QB:
                return cur_ref[i * ATT_BLOCK:(i + 2) * ATT_BLOCK, sl]
            return jnp.concatenate([cur_ref[i * ATT_BLOCK:, sl], nxt_ref[:, sl]], axis=0)

        for h in range(ATT_SLOTS):
            sl = slice(h * ATT_D, (h + 1) * ATT_D)
            from_prev = carry[:, sl]
            for i in range(ATT_QB):
                blk = n * ATT_QB + i
                has_next = (((blk + 1) % nbs) != 0) & (blk + 1 < nblk)
                mask = jnp.concatenate([mcur, mprev & has_next], axis=0)
                own = slice(i * ATT_BLOCK, (i + 1) * ATT_BLOCK)
                kh, vh = k_ref[own, sl], v_ref[own, sl]
                qs, dos = pair(qc_ref, qn_ref, i, sl), pair(doc_ref, don_ref, i, sl)
                lse, dlt = pair(lsec_ref, lsen_ref, i, sl), pair(dltc_ref, dltn_ref, i, sl)
                p = jnp.where(mask, jnp.exp(_dot_nt(qs, kh) * scale - lse), 0.0)
                ds = (p * (_dot_nt(dos, vh) - dlt) * scale).astype(BF16)
                dqs = _dot(ds, kh)
                dq_ref[own, sl] = (from_prev + dqs[:ATT_BLOCK]).astype(dq_ref.dtype)
                from_prev = dqs[ATT_BLOCK:]
                dk_ref[own, sl] = _dot_tn(ds, qs).astype(dk_ref.dtype)
                dv_ref[own, sl] = _dot_tn(p.astype(BF16), dos).astype(dv_ref.dtype)
            carry[:, sl] = from_prev

    return pl.pallas_call(
        body, name=f"att_bwd_{g}", grid=(t // rows,), in_specs=[cur, nxt, cur, cur, cur, nxt, cur, nxt, cur, nxt],
        out_specs=[cur, cur, cur], out_shape=[SDS((t, w), BF16)] * 3,
        scratch_shapes=[pltpu.VMEM((ATT_BLOCK, w), F32)],
        compiler_params=_cparams(("arbitrary",)))(q, q, k, v, do, do, lse, lse, dlt, dlt)


def _merge_weights(ls):
    m = jnp.maximum(jnp.maximum(ls[0], ls[1]), ls[2])
    es = [jnp.exp(v - m) for v in ls]
    den = es[0] + es[1] + es[2]
    return [e / den for e in es]


def _merge_fwd(o, lse, nb, seq):
    t = o[0].shape[0]
    tm = ATT_TILE
    mtiles = seq // tm
    ng = len(ATT_DILATIONS)

    def body(*refs):
        o_refs, l_refs, out_ref, scs = refs[:ng], refs[ng:2 * ng], refs[2 * ng], refs[2 * ng + 1:]
        for slot in range(ATT_SLOTS):
            lanes = slice(slot * ATT_D, (slot + 1) * ATT_D)
            ov = [_from_strided(o_refs[g], lanes, r, scs[2 * g]) for g, r in enumerate(ATT_DILATIONS)]
            ws = _merge_weights([_from_strided(l_refs[g], lanes, r, scs[2 * g + 1])
                                 for g, r in enumerate(ATT_DILATIONS)])
            out_ref[:, lanes] = (ws[0] * ov[0] + ws[1] * ov[1] + ws[2] * ov[2]).astype(out_ref.dtype)

    views = [a.reshape(nb, r, mtiles, tm // r, ATT_W) for grp in (o, lse) for a, r in zip(grp, ATT_DILATIONS)]
    return pl.pallas_call(
        body, name="att_merge_fwd", grid=(t // tm,),
        in_specs=[_strided_spec(r, mtiles) for _ in range(2) for r in ATT_DILATIONS],
        out_specs=_rs(tm, ATT_W), out_shape=SDS((t, ATT_W), BF16),
        scratch_shapes=[pltpu.VMEM((tm, ATT_D), F32)] * (2 * ng), compiler_params=_cparams(("arbitrary",)))(*views)


def _merge_bwd(o, lse, datt, nb, seq):
    t = o[0].shape[0]
    tm = ATT_TILE
    mtiles = seq // tm
    ng = len(ATT_DILATIONS)

    def body(*refs):
        o_refs, l_refs, d_ref = refs[:ng], refs[ng:2 * ng], refs[2 * ng]
        do_refs, dlt_refs = refs[2 * ng + 1:3 * ng + 1], refs[3 * ng + 1:4 * ng + 1]
        scs = refs[4 * ng + 1:]
        for slot in range(ATT_SLOTS):
            lanes = slice(slot * ATT_D, (slot + 1) * ATT_D)
            ov = [_from_strided(o_refs[g], lanes, r, scs[2 * g]) for g, r in enumerate(ATT_DILATIONS)]
            ws = _merge_weights([_from_strided(l_refs[g], lanes, r, scs[2 * g + 1])
                                 for g, r in enumerate(ATT_DILATIONS)])
            dv = d_ref[:, lanes]
            att = ws[0] * ov[0] + ws[1] * ov[1] + ws[2] * ov[2]
            dot = jnp.broadcast_to(jnp.sum(dv * att, axis=-1, keepdims=True), (tm, ATT_D))
            for g, r in enumerate(ATT_DILATIONS):
                _to_strided(ws[g] * dv, do_refs[g], lanes, r, scs[2 * ng])
                _to_strided(ws[g] * dot, dlt_refs[g], lanes, r, scs[2 * ng + 1])

    views = [a.reshape(nb, r, mtiles, tm // r, ATT_W) for grp in (o, lse) for a, r in zip(grp, ATT_DILATIONS)]
    outs = pl.pallas_call(
        body, name="att_merge_bwd", grid=(t // tm,),
        in_specs=[_strided_spec(r, mtiles) for _ in range(2) for r in ATT_DILATIONS] + [_rs(tm, ATT_W)],
        out_specs=[_strided_spec(r, mtiles) for _ in range(2) for r in ATT_DILATIONS],
        out_shape=[_strided_shape(nb, r, mtiles, dt) for dt in (BF16, F32) for r in ATT_DILATIONS],
        scratch_shapes=[pltpu.VMEM((tm, ATT_D), F32)] * (2 * ng + 2), compiler_params=_cparams(("arbitrary",)))(
            *views, datt)
    flat = [a.reshape(t, ATT_W) for a in outs]
    return flat[:ng], flat[ng:]


def _mix_fwd(gate_logits, b_gate, y_ssm, y_att):
    t, d = y_ssm.shape
    tm = 512

    def fn(i, g0_ref, g1_ref, b0_ref, b1_ref, ys_ref, ya_ref):
        g0 = _sigmoid(g0_ref[...].astype(F32) + b0_ref[...])
        g1 = _sigmoid(g1_ref[...].astype(F32) + b1_ref[...])
        return [g0 * ys_ref[...].astype(F32) + g1 * ya_ref[...].astype(F32)]

    b_spec = lambda cb: pl.BlockSpec((1, d), lambda i: (0, cb))
    return _rw("mix_fwd", fn, t // tm,
               [(gate_logits, _rs(tm, d, 0)), (gate_logits, _rs(tm, d, 1)), (b_gate, b_spec(0)), (b_gate, b_spec(1)),
                (y_ssm, _rs(tm, d)), (y_att, _rs(tm, d))],
               [(SDS((t, d), BF16), _rs(tm, d))])[0]


def _mix_bwd(gate_logits, b_gate, y_ssm, y_att, dmixed):
    t, d = y_ssm.shape
    tm = 256

    def fn(i, g0_ref, g1_ref, b0_ref, b1_ref, ys_ref, ya_ref, dm_ref):
        g0 = _sigmoid(g0_ref[...].astype(F32) + b0_ref[...])
        g1 = _sigmoid(g1_ref[...].astype(F32) + b1_ref[...])
        dm = dm_ref[...]
        dg = jnp.concatenate([dm * ys_ref[...].astype(F32) * g0 * (1.0 - g0),
                              dm * ya_ref[...].astype(F32) * g1 * (1.0 - g1)], axis=-1)
        return [dm * g0, dm * g1, dg, _colsum(dg)]

    b_spec = lambda cb: pl.BlockSpec((1, d), lambda i: (0, cb))
    return _rw("mix_bwd", fn, t // tm,
               [(gate_logits, _rs(tm, d, 0)), (gate_logits, _rs(tm, d, 1)), (b_gate, b_spec(0)), (b_gate, b_spec(1)),
                (y_ssm, _rs(tm, d)), (y_att, _rs(tm, d)), (dmixed, _rs(tm, d))],
               [(SDS((t, d), BF16), _rs(tm, d)), (SDS((t, d), BF16), _rs(tm, d)),
                (SDS((t, 2 * d), BF16), _rs(tm, 2 * d)), (SDS((1, 2 * d), F32), _fs((1, 2 * d)))], n_acc=1)


def _swiglu_fwd(gt, up):
    t, f = gt.shape
    tm = 256

    def fn(i, g_ref, u_ref):
        gv = g_ref[...].astype(F32)
        return [gv * _sigmoid(gv) * u_ref[...].astype(F32)]

    return _rw("swiglu_fwd", fn, t // tm, [(gt, _rs(tm, f)), (up, _rs(tm, f))], [(SDS((t, f), BF16), _rs(tm, f))])[0]


def _swiglu_bwd(gt, up, dact):
    t, f = gt.shape
    tm = 256

    def fn(i, g_ref, u_ref, d_ref):
        gv, dv = g_ref[...].astype(F32), d_ref[...].astype(F32)
        sg = _sigmoid(gv)
        return [dv * u_ref[...].astype(F32) * sg * (1.0 + gv * (1.0 - sg)), dv * gv * sg]

    return _rw("swiglu_bwd", fn, t // tm, [(gt, _rs(tm, f)), (up, _rs(tm, f)), (dact, _rs(tm, f))],
               [(SDS((t, f), BF16), _rs(tm, f))] * 2)


def _adamw(w, g, m, v, name):
    r, c = w.shape[-2:]
    lead = w.ndim - 2
    tr = _row_tile(r, max(8, 400_000 // c))
    c1 = 1.0 / (1.0 - ADAM_B1 ** ADAM_STEP)
    c2 = 1.0 / (1.0 - ADAM_B2 ** ADAM_STEP)

    def fn(i, w_ref, g_ref, m_ref, v_ref):
        gv = g_ref[...]
        mn = ADAM_B1 * m_ref[...] + (1.0 - ADAM_B1) * gv
        vn = ADAM_B2 * v_ref[...] + (1.0 - ADAM_B2) * (gv * gv)
        delta = -ADAM_LR * ((mn * c1) / (jnp.sqrt(vn * c2) + ADAM_EPS) + ADAM_WD * w_ref[...])
        return [delta, mn, vn]

    spec = pl.BlockSpec((None,) * lead + (tr, c), lambda i: (0,) * lead + (i, 0))
    return _rw(name, fn, r // tr, [(w, spec), (g, spec), (m, spec), (v, spec)], [(SDS(w.shape, F32), spec)] * 3)


ANY = pl.BlockSpec(memory_space=pl.ANY)


def _place():
    x, y, c = lax.axis_index("x"), lax.axis_index("y"), lax.axis_index("c")
    chips = [(1 - x, y), (x, 1 - y), (1 - x, 1 - y)]
    return x, y, c, chips


def _remote(src, dst, ssem, rsem, to):
    return pltpu.make_async_remote_copy(src_ref=src, dst_ref=dst, send_sem=ssem, recv_sem=rsem, device_id=to,
                                        device_id_type=MESH)


def _copy_through_vmem(src, dst, buf, isem, osem):
    chunk = buf.shape[1]
    n = src.shape[0] // chunk
    load = lambda k: pltpu.make_async_copy(src.at[pl.ds(k * chunk, chunk)], buf.at[k % 2], isem.at[k % 2])
    store = lambda k: pltpu.make_async_copy(buf.at[k % 2], dst.at[pl.ds(k * chunk, chunk)], osem.at[k % 2])
    load(0).start()
    for k in range(n):
        load(k).wait()
        if k + 1 < n:
            if k >= 1:
                store(k - 1).wait()
            load(k + 1).start()
        store(k).start()
    if n >= 2:
        store(n - 2).wait()
    store(n - 1).wait()


def _copy_scratch(rows, width, dtype):
    chunk = _row_tile(rows, 512)
    return [pltpu.VMEM((2, chunk, width), dtype), pltpu.SemaphoreType.DMA((2,)), pltpu.SemaphoreType.DMA((2,))]


def _gather_weights(wp):
    def body(w_ref, out_ref, ssem, rsem, buf, isem, osem):
        x, y, c, chips = _place()
        me = 2 * x + y
        sib = (x, y, 1 - c)
        first = [_remote(w_ref.at[c], out_ref.at[me, c], ssem.at[j], rsem.at[j], (*chip, c))
                 for j, chip in enumerate(chips)]
        for cp in first:
            cp.start()
        for half in range(2):
            _copy_through_vmem(w_ref.at[half], out_ref.at[me, half], buf, isem, osem)
        passed = []
        for j, chip in enumerate(chips):
            ci = 2 * chip[0] + chip[1]
            _remote(w_ref.at[c], out_ref.at[ci, c], ssem.at[j], rsem.at[j], (*chip, c)).wait_recv()
            cp = _remote(out_ref.at[ci, c], out_ref.at[ci, c], ssem.at[3 + j], rsem.at[3 + j], sib)
            cp.start()
            passed.append(cp)
        for j, chip in enumerate(chips):
            ci = 2 * chip[0] + chip[1]
            _remote(out_ref.at[ci, 1 - c], out_ref.at[ci, 1 - c], ssem.at[3 + j], rsem.at[3 + j], sib).wait_recv()
        for cp in first + passed:
            cp.wait_send()

    return pl.pallas_call(
        body, name="gather_weights", in_specs=[ANY], out_specs=ANY,
        out_shape=SDS((N_CHIPS,) + wp.shape, wp.dtype),
        scratch_shapes=[pltpu.SemaphoreType.DMA((6,)), pltpu.SemaphoreType.DMA((6,))]
        + _copy_scratch(wp.shape[1], wp.shape[2], wp.dtype),
        compiler_params=pltpu.CompilerParams(has_side_effects=True))(wp)


def _swap_halves(g2, tag):
    def body(g_ref, out_ref, ssem, rsem):
        x, y, c, _ = _place()
        cp = _remote(g_ref.at[1 - c], out_ref, ssem, rsem, (x, y, 1 - c))
        cp.start()
        cp.wait()

    return pl.pallas_call(
        body, name="swap_halves_" + tag, in_specs=[ANY], out_specs=ANY, out_shape=SDS(g2.shape[1:], g2.dtype),
        scratch_shapes=[pltpu.SemaphoreType.DMA(()), pltpu.SemaphoreType.DMA(())],
        compiler_params=pltpu.CompilerParams(has_side_effects=True))(g2)


def _add_own_half(g2, other, c, tag):
    _, nch, rows, w = g2.shape
    tr = _row_tile(rows, 512)
    nr = rows // tr

    def body(c_ref, a_ref, b_ref, o_ref):
        o_ref[...] = (a_ref[...].astype(F32) + b_ref[...].astype(F32)).astype(o_ref.dtype)

    grid_spec = pltpu.PrefetchScalarGridSpec(
        num_scalar_prefetch=1, grid=(nch, nr),
        in_specs=[pl.BlockSpec((None, None, tr, w), lambda k, i, c_ref: (c_ref[0], k, i, 0)),
                  pl.BlockSpec((None, tr, w), lambda k, i, c_ref: (k, i, 0))],
        out_specs=pl.BlockSpec((None, tr, w), lambda k, i, c_ref: (k, i, 0)))
    return pl.pallas_call(
        body, name="add_own_half_" + tag, grid_spec=grid_spec, out_shape=SDS(other.shape, other.dtype),
        compiler_params=_cparams(("arbitrary", "arbitrary")))(jnp.reshape(c, (1,)).astype(jnp.int32), g2, other)


def _sum_chips(q, tag):
    nch, rows, w = q.shape
    tr = _row_tile(rows, 512)

    def fn(i, q_ref):
        return [((q_ref[0].astype(F32) + q_ref[1].astype(F32)) + q_ref[2].astype(F32)) + q_ref[3].astype(F32)]

    return _rw("sum_chips_" + tag, fn, rows // tr, [(q, pl.BlockSpec((nch, tr, w), lambda i: (0, i, 0)))],
               [(SDS((rows, w), F32), _rs(tr, w))])[0]


def _chip_copies(src_ref, dst_ref, ssem, rsem, outgoing):
    x, y, c, chips = _place()
    me = 2 * x + y
    cps = []
    for j, chip in enumerate(chips):
        ci = 2 * chip[0] + chip[1]
        cps.append(_remote(src_ref.at[ci], dst_ref.at[me if outgoing else ci], ssem.at[j], rsem.at[j], (*chip, c)))
    return cps, me


def _scatter_side(p):
    def first(ins, outs, scr):
        cps, me = _chip_copies(ins[0], outs[0], scr[0], scr[1], True)
        for cp in cps:
            cp.start()
        pltpu.make_async_copy(ins[0].at[me], outs[0].at[me], scr[2]).start()

    def last(ins, outs, scr):
        for cp in _chip_copies(ins[0], outs[0], scr[0], scr[1], False)[0]:
            cp.wait_recv()
        cps, me = _chip_copies(ins[0], outs[0], scr[0], scr[1], True)
        for cp in cps:
            cp.wait_send()
        pltpu.make_async_copy(ins[0].at[me], outs[0].at[me], scr[2]).wait()

    return _Side((p,), (SDS(p.shape, p.dtype),),
                 (pltpu.SemaphoreType.DMA((3,)), pltpu.SemaphoreType.DMA((3,)), pltpu.SemaphoreType.DMA(())),
                 first, None, last)


def _gather_copies(w_ref, out_ref, ssem, rsem):
    x, y, c, chips = _place()
    me = 2 * x + y
    sib = (x, y, 1 - c)
    sends, arrivals, forwards, from_sib = [], [], [], []
    for j, chip in enumerate(chips):
        ci = 2 * chip[0] + chip[1]
        sends.append(_remote(w_ref.at[c], out_ref.at[me, c], ssem.at[j], rsem.at[j], (*chip, c)))
        arrivals.append(_remote(w_ref.at[c], out_ref.at[ci, c], ssem.at[j], rsem.at[j], (*chip, c)))
        forwards.append(_remote(out_ref.at[ci, c], out_ref.at[ci, c], ssem.at[3 + j], rsem.at[3 + j], sib))
        from_sib.append(_remote(out_ref.at[ci, 1 - c], out_ref.at[ci, 1 - c], ssem.at[3 + j], rsem.at[3 + j], sib))
    return sends, arrivals, forwards, from_sib, me


def _gather_side(wp):
    def first(ins, outs, scr):
        sends, _, _, _, me = _gather_copies(ins[0], outs[0], scr[0], scr[1])
        for cp in sends:
            cp.start()
        pltpu.make_async_copy(ins[0], outs[0].at[me], scr[2]).start()

    def mid(ins, outs, scr):
        _, arrivals, forwards, _, _ = _gather_copies(ins[0], outs[0], scr[0], scr[1])
        for arrived, forward in zip(arrivals, forwards):
            arrived.wait_recv()
            forward.start()

    def last(ins, outs, scr):
        sends, _, forwards, from_sib, me = _gather_copies(ins[0], outs[0], scr[0], scr[1])
        for cp in from_sib:
            cp.wait_recv()
        for cp in sends + forwards:
            cp.wait_send()
        pltpu.make_async_copy(ins[0], outs[0].at[me], scr[2]).wait()

    return _Side((wp,), (SDS((N_CHIPS,) + wp.shape, wp.dtype),),
                 (pltpu.SemaphoreType.DMA((6,)), pltpu.SemaphoreType.DMA((6,)), pltpu.SemaphoreType.DMA(())),
                 first, mid, last)


def _allreduce_small(v, name):
    rows, w = v.shape
    offsets = [(dx, dy, dc) for dx in (0, 1) for dy in (0, 1) for dc in (0, 1)][1:]

    def body(v_ref, o_ref, buf, ssem, rsem):
        x, y, c, _ = _place()
        flip = lambda p, d: 1 - p if d else p
        peers = [(flip(x, dx), flip(y, dy), flip(c, dc)) for dx, dy, dc in offsets]
        index = lambda p: 4 * p[0] + 2 * p[1] + p[2]
        me = index((x, y, c))
        buf[me] = v_ref[...]
        sent = [_remote(v_ref, buf.at[me], ssem.at[q], rsem.at[q], p) for q, p in enumerate(peers)]
        for cp in sent:
            cp.start()
        for q, p in enumerate(peers):
            _remote(v_ref, buf.at[index(p)], ssem.at[q], rsem.at[q], p).wait_recv()
        for cp in sent:
            cp.wait_send()
        acc = buf[0]
        for q in range(1, 8):
            acc = acc + buf[q]
        o_ref[...] = acc

    vm = pl.BlockSpec(memory_space=pltpu.VMEM)
    return pl.pallas_call(
        body, name=name, in_specs=[vm], out_specs=vm, out_shape=SDS((rows, w), F32),
        scratch_shapes=[pltpu.VMEM((8, rows, w), F32), pltpu.SemaphoreType.DMA((7,)), pltpu.SemaphoreType.DMA((7,))],
        compiler_params=pltpu.CompilerParams(has_side_effects=True))(v)


def _join_halves(h, tag):
    def body(h_ref, out_ref, ssem, rsem, buf, isem, osem):
        x, y, c, _ = _place()
        cp = _remote(h_ref, out_ref.at[c], ssem, rsem, (x, y, 1 - c))
        cp.start()
        _copy_through_vmem(h_ref, out_ref.at[c], buf, isem, osem)
        _remote(h_ref, out_ref.at[1 - c], ssem, rsem, (x, y, 1 - c)).wait_recv()
        cp.wait_send()

    return pl.pallas_call(
        body, name="join_halves_" + tag, in_specs=[ANY], out_specs=ANY, out_shape=SDS((2,) + h.shape, h.dtype),
        scratch_shapes=[pltpu.SemaphoreType.DMA(()), pltpu.SemaphoreType.DMA(())]
        + _copy_scratch(h.shape[0], h.shape[1], h.dtype),
        compiler_params=pltpu.CompilerParams(has_side_effects=True))(h)


PACK_W = 1024
SHARDED = ("w_in", "w_ffn_gate", "w_ffn_up", "w_ssm_out", "w_att_out", "w_mix_out", "w_ffn_down")
COL_SHARDED = ("w_in", "w_ffn_gate", "w_ffn_up", "w_att_out")
SMALL = ("norm_mix", "b_gate", "conv_b", "dt_bias", "a_log", "d_skip", "ssm_norm", "norm_ffn", "norm_final")


PACK_ROW_ALIGN = 16


def _rows(n):
    return -(-n // (PACK_W * PACK_ROW_ALIGN)) * PACK_ROW_ALIGN


def _pack_rows(parts, total_rows):
    rows = []
    for p in parts:
        flat = p.reshape(-1)
        pad = _rows(flat.shape[0]) * PACK_W - flat.shape[0]
        if pad:
            flat = jnp.concatenate([flat, jnp.zeros((pad,), flat.dtype)])
        rows.append(flat.reshape(-1, PACK_W))
    used = sum(r.shape[0] for r in rows)
    if total_rows > used:
        rows.append(jnp.zeros((total_rows - used, PACK_W), rows[0].dtype))
    return jnp.concatenate(rows, axis=0)


def _padded_rows(n):
    return -(-n // 32) * 32


def _wire_name(name):
    return name + "_t" if name in COL_SHARDED else name


def _wire_shard(w, name):
    return w.T if name in COL_SHARDED else w


def _group_major(a, axis):
    gw = D_INNER // N_GROUPS
    take = lambda lo, n: lax.slice_in_dim(a, lo, lo + n, axis=axis)
    parts = []
    for g in range(N_GROUPS):
        parts += [take(g * gw, gw), take(D_INNER + g * D_STATE, D_STATE),
                  take(D_INNER + N_GROUPS * D_STATE + g * D_STATE, D_STATE)]
    return jnp.concatenate(parts, axis=axis)


def _group_major_inv(a, axis):
    gw = D_INNER // N_GROUPS
    take = lambda lo, n: lax.slice_in_dim(a, lo, lo + n, axis=axis)
    xs = [take(g * GROUP_W, gw) for g in range(N_GROUPS)]
    bs = [take(g * GROUP_W + gw, D_STATE) for g in range(N_GROUPS)]
    cs = [take(g * GROUP_W + gw + D_STATE, D_STATE) for g in range(N_GROUPS)]
    return jnp.concatenate(xs + bs + cs, axis=axis)


LATE = ("w_ffn_gate_t", "w_ffn_up_t", "w_ssm_out", "w_att_out_t", "w_mix_out", "w_ffn_down")


class _Overlap(NamedTuple):
    gather_side: _Side
    late_weights: Callable
    scatter_side: Callable
    scatter_in: Callable


def _local_step(x, target, wts, overlap):
    nb, seq, d = x.shape
    t = nb * seq
    x = x.reshape(t, d)
    target = target.reshape(t, d)
    hg = HEADS_PER_GROUP

    w_in_t = wts["w_in_t"]
    o1, o2, o3, o4 = D_INNER, D_INNER + CONV_DIM, D_INNER + CONV_DIM + N_HEADS, D_INNER + CONV_DIM + N_HEADS + QKV_DIM
    w_z = w_in_t[:o1]
    w_xbc = _group_major(w_in_t[o1:o2], 0)
    w_dt = jnp.pad(w_in_t[o2:o3], ((0, DT_PAD - N_HEADS), (0, 0)))
    w_qkv = w_in_t[o3:o4]
    w_gate = w_in_t[o4:]
    conv_w = _group_major(wts["conv_w"], 1)
    conv_b = _group_major(wts["conv_b"], 1)

    def per_group_row(p):
        return p.reshape(N_GROUPS, 1, hg)

    def per_group_col(p):
        return p.reshape(N_GROUPS, hg, 1)

    a_neg = -jnp.exp(wts["a_log"])
    bias_r, bias_c = per_group_row(wts["dt_bias"]), per_group_col(wts["dt_bias"])
    a_r, a_c = per_group_row(a_neg), per_group_col(a_neg)
    dskip_r = per_group_row(wts["d_skip"])
    cos, sin = _rope_tables(seq)

    h = _rms_fwd(x, wts["norm_mix"], "rms_mix_fwd")
    z = _mm(h, w_z, "nt", BF16, "proj_z")
    xbc = _mm(h, w_xbc, "nt", F32, "proj_xbc")
    dt_raw = _mm(h, w_dt, "nt", F32, "proj_dt")
    qkv = _mm(h, w_qkv, "nt", BF16, "proj_qkv")
    gate_logits = _mm(h, w_gate, "nt", BF16, "proj_gate")

    xc = _conv_fwd(xbc, conv_w, conv_b, seq)
    dtr = dt_raw[:, :N_HEADS].reshape(t, N_GROUPS, hg).transpose(1, 0, 2)
    dtrt = dt_raw[:, :N_HEADS].reshape(nb, seq, N_GROUPS, hg).transpose(2, 0, 3, 1)
    y, states, *gathered = _ssd_fwd(xc, dtr, dtrt, bias_r, bias_c, a_r, a_c, dskip_r, nb, seq, overlap.gather_side)
    wts = {**wts, **overlap.late_weights(gathered)}
    yn = _gate_norm_fwd(y, z, wts["ssm_norm"])
    y_ssm = _mm(yn, wts["w_ssm_out"], "nn", BF16, "ssm_out")

    groups = range(len(ATT_DILATIONS))
    qg, kg, vg = _rope_fwd(qkv, cos, sin, nb, seq)
    o_g, lse_g = zip(*[_att_fwd(qg[i], kg[i], vg[i], i, seq) for i in groups])
    att = _merge_fwd(o_g, lse_g, nb, seq)
    y_att = _mm(att, wts["w_att_out_t"], "nt", BF16, "att_out")

    mixed = _mix_fwd(gate_logits, wts["b_gate"], y_ssm, y_att)
    x1 = _mm(mixed, wts["w_mix_out"], "nn", F32, "mix_out", add=x)
    h2 = _rms_fwd(x1, wts["norm_ffn"], "rms_ffn_fwd")
    gt = _mm(h2, wts["w_ffn_gate_t"], "nt", BF16, "ffn_gate")
    up = _mm(h2, wts["w_ffn_up_t"], "nt", BF16, "ffn_up")
    act = _swiglu_fwd(gt, up)
    x2 = _mm(act, wts["w_ffn_down"], "nn", F32, "ffn_down", add=x1)

    g = {}
    dx2, dx2_b, g["norm_final"], loss = _final_fwd_bwd(x2, target, wts["norm_final"].reshape(1, d))
    dact = _mm(dx2_b, wts["w_ffn_down"], "nt", BF16, "d_act")
    g["w_ffn_down"] = _mm(act, dx2_b, "tn", BF16, "g_ffn_down")
    dgt, dup = _swiglu_bwd(gt, up, dact)
    g["w_ffn_gate_t"] = _mm(dgt, h2, "tn", BF16, "g_ffn_gate")
    g["w_ffn_up_t"] = _mm(dup, h2, "tn", BF16, "g_ffn_up")
    dh2 = _mm(dgt, wts["w_ffn_gate_t"], "nn", F32, "d_h2_gate")
    dh2 = _mm(dup, wts["w_ffn_up_t"], "nn", F32, "d_h2_up", add=dh2)
    dx1, dx1_b, g["norm_ffn"] = _rms_bwd(x1, dh2, wts["norm_ffn"], dx2, "rms_ffn_bwd")

    dmixed = _mm(dx1_b, wts["w_mix_out"], "nt", F32, "d_mixed")
    g["w_mix_out"] = _mm(mixed, dx1_b, "tn", BF16, "g_mix_out")
    dy_ssm, dy_att, dgate, g["b_gate"] = _mix_bwd(gate_logits, wts["b_gate"], y_ssm, y_att, dmixed)

    datt = _mm(dy_att, wts["w_att_out_t"], "nn", F32, "d_att")
    g["w_att_out_t"] = _mm(dy_att, att, "tn", BF16, "g_att_out")
    do_g, dlt_g = _merge_bwd(o_g, lse_g, datt, nb, seq)
    dq_g, dk_g, dv_g = zip(*[_att_bwd(qg[i], kg[i], vg[i], do_g[i], lse_g[i], dlt_g[i], i, seq) for i in groups])
    dqkv = _rope_bwd(dq_g, dk_g, dv_g, cos, sin, nb, seq)

    dyn = _mm(dy_ssm, wts["w_ssm_out"], "nt", BF16, "d_yn")
    g["w_ssm_out"] = _mm(yn, dy_ssm, "tn", BF16, "g_ssm_out")
    dy, dz, g["ssm_norm"] = _gate_norm_bwd(y, z, wts["ssm_norm"], dyn)
    side = overlap.scatter_side({n: g.pop(n) for n in LATE})
    dxc, ddtr, g_bias, g_alog, g_dskip, *scattered = _ssd_bwd(xc, dtr, dtrt, bias_r, bias_c, a_r, a_c, dskip_r,
                                                               states, dy, nb, seq, side)
    g["dt_bias"] = g_bias.reshape(1, N_HEADS)
    g["a_log"] = g_alog.reshape(1, N_HEADS)
    g["d_skip"] = g_dskip.reshape(1, N_HEADS)
    dpre, g_conv_w, g_conv_b = _conv_bwd_pre(xbc, conv_w, conv_b, dxc, seq)
    g["conv_w"] = _group_major_inv(g_conv_w, 1)
    g["conv_b"] = _group_major_inv(g_conv_b, 1)
    dxbc = _conv_bwd_in(dpre, conv_w, seq)
    ddt = jnp.pad(ddtr.transpose(1, 0, 2).reshape(t, N_HEADS), ((0, 0), (0, DT_PAD - N_HEADS))).astype(BF16)

    g_in_t = jnp.concatenate([
        _mm(dz, h, "tn", BF16, "g_in_z"),
        _group_major_inv(_mm(dxbc, h, "tn", BF16, "g_in_xbc"), 0),
        _mm(ddt, h, "tn", BF16, "g_in_dt")[:N_HEADS],
        _mm(dqkv, h, "tn", BF16, "g_in_qkv"),
        _mm(dgate, h, "tn", BF16, "g_in_gate")], axis=0)
    dh = _mm(dz, w_z, "nn", F32, "d_h_z")
    dh = _mm(dxbc, w_xbc, "nn", F32, "d_h_xbc", add=dh)
    dh = _mm(ddt, w_dt, "nn", F32, "d_h_dt", add=dh)
    dh = _mm(dgate, w_gate, "nn", F32, "d_h_gate", add=dh)
    dh, *scattered_in = _mm(dqkv, w_qkv, "nn", F32, "d_h_qkv", add=dh, side=overlap.scatter_in({"w_in_t": g_in_t}))
    dx, _, g["norm_mix"] = _rms_bwd(x, dh, wts["norm_mix"], dx1, "rms_mix_bwd")
    return loss[0, 0], dx.reshape(nb, seq, d), g, scattered, scattered_in


def kernel(x, norm_mix, w_in, b_gate, conv_w, conv_b, dt_bias, a_log, d_skip, ssm_norm, w_ssm_out, w_att_out, w_mix_out, norm_ffn, w_ffn_gate, w_ffn_up, w_ffn_down, norm_final, loss_target, m_norm_mix, m_w_in, m_b_gate, m_conv_w, m_conv_b, m_dt_bias, m_a_log, m_d_skip, m_ssm_norm, m_w_ssm_out, m_w_att_out, m_w_mix_out, m_norm_ffn, m_w_ffn_gate, m_w_ffn_up, m_w_ffn_down, m_norm_final, v_norm_mix, v_w_in, v_b_gate, v_conv_w, v_conv_b, v_dt_bias, v_a_log, v_d_skip, v_ssm_norm, v_w_ssm_out, v_w_att_out, v_w_mix_out, v_norm_ffn, v_w_ffn_gate, v_w_ffn_up, v_w_ffn_down, v_norm_final):
    names = ("norm_mix", "w_in", "b_gate", "conv_w", "conv_b", "dt_bias", "a_log", "d_skip", "ssm_norm", "w_ssm_out",
             "w_att_out", "w_mix_out", "norm_ffn", "w_ffn_gate", "w_ffn_up", "w_ffn_down", "norm_final")
    w_loc = dict(zip(names, (norm_mix, w_in, b_gate, conv_w, conv_b, dt_bias, a_log, d_skip, ssm_norm, w_ssm_out,
                             w_att_out, w_mix_out, norm_ffn, w_ffn_gate, w_ffn_up, w_ffn_down, norm_final)))
    m_loc = dict(zip(names, (m_norm_mix, m_w_in, m_b_gate, m_conv_w, m_conv_b, m_dt_bias, m_a_log, m_d_skip,
                             m_ssm_norm, m_w_ssm_out, m_w_att_out, m_w_mix_out, m_norm_ffn, m_w_ffn_gate,
                             m_w_ffn_up, m_w_ffn_down, m_norm_final)))
    v_loc = dict(zip(names, (v_norm_mix, v_w_in, v_b_gate, v_conv_w, v_conv_b, v_dt_bias, v_a_log, v_d_skip,
                             v_ssm_norm, v_w_ssm_out, v_w_att_out, v_w_mix_out, v_norm_ffn, v_w_ffn_gate,
                             v_w_ffn_up, v_w_ffn_down, v_norm_final)))
    two_d = lambda a: a.reshape(a.shape[-2:]) if a.ndim >= 2 else a.reshape(1, -1)
    w2 = {n: two_d(a) for n, a in w_loc.items()}
    chip = 2 * lax.axis_index("x") + lax.axis_index("y")
    c = lax.axis_index("c")

    wire_shapes = {n: _wire_shard(w2[n], n).shape for n in SHARDED}
    true_rows = {n: wire_shapes[n][0] * wire_shapes[n][1] // PACK_W for n in SHARDED}
    seg_rows = {n: _rows(wire_shapes[n][0] * wire_shapes[n][1]) for n in SHARDED}
    buckets = {"first": ("w_in",), "late": tuple(n for n in SHARDED if n != "w_in")}
    rows_of = {b: _padded_rows(sum(seg_rows[n] for n in ns)) for b, ns in buckets.items()}

    def pack_shards(b):
        packed = _pack_rows([_wire_shard(w2[n], n).astype(BF16) for n in buckets[b]], rows_of[b])
        return packed.reshape(2, rows_of[b] // 2, PACK_W)

    def unpack_full(gathered, b):
        wg, out, off = gathered.reshape(N_CHIPS, rows_of[b], PACK_W), {}, 0
        for n in buckets[b]:
            rows, cols = wire_shapes[n]
            out[_wire_name(n)] = wg[:, off:off + true_rows[n]].reshape(N_CHIPS * rows, cols)
            off += seg_rows[n]
        return out

    def pack_grads(g, b):
        sections = [_pack_rows([g[_wire_name(n)].reshape(N_CHIPS, true_rows[n], PACK_W)[k] for n in buckets[b]],
                               rows_of[b]) for k in range(N_CHIPS)]
        return jnp.stack(sections).reshape(N_CHIPS, 2, rows_of[b] // 2, PACK_W).transpose(1, 0, 2, 3)

    def chip_sums(g, b):
        g2 = pack_grads(g, b)
        return _add_own_half(g2, _swap_halves(g2, b), c, b)

    def finish(by_source, b):
        reduced = _join_halves(_sum_chips(by_source, b), b).reshape(rows_of[b], PACK_W)
        out, off = {}, 0
        for n in buckets[b]:
            wire = reduced[off:off + true_rows[n]].reshape(wire_shapes[n])
            out[n] = wire.T if n in COL_SHARDED else wire
            off += seg_rows[n]
        return out

    full = unpack_full(_gather_weights(pack_shards("first")), "first")
    for n in SMALL:
        full[n] = w2[n]
    overlap = _Overlap(_gather_side(pack_shards("late")), lambda outs: unpack_full(outs[0], "late"),
                       lambda g: _scatter_side(chip_sums(g, "late")), lambda g: _scatter_side(chip_sums(g, "first")))

    n_conv = w2["conv_w"].shape[1]
    placed = lax.dynamic_update_slice_in_dim(jnp.zeros((CONV_K, N_CHIPS * n_conv), F32), w2["conv_w"], chip * n_conv, 1)
    placed = jnp.where(c == 0, placed, 0.0)
    full["conv_w"] = _allreduce_small(_pack_rows([placed], _rows(int(placed.size))), "gather_conv_w").reshape(
        -1)[:placed.size].reshape(placed.shape)

    loss_sum, grad_x, g_full, scattered, scattered_in = _local_step(x, loss_target, full, overlap)
    loss = lax.psum(loss_sum, ("x", "y", "c"))

    g_shard = {}
    small_names = SMALL + ("conv_w",)
    small_flat = jnp.concatenate([g_full[n].reshape(-1) for n in small_names])
    small = _allreduce_small(_pack_rows([small_flat], _rows(int(small_flat.size))), "allreduce_small").reshape(-1)
    off = 0
    for n in small_names:
        size = int(g_full[n].size)
        g_shard[n] = small[off:off + size].reshape(g_full[n].shape)
        off += size
    g_shard["conv_w"] = lax.dynamic_slice_in_dim(g_shard["conv_w"], chip * n_conv, n_conv, 1)

    g_shard.update(finish(scattered[0], "late"))
    g_shard.update(finish(scattered_in[0], "first"))

    grads, deltas, new_m, new_v = [], [], [], []
    for n in names:
        shape = w_loc[n].shape
        as_rows = (lambda a: a) if len(shape) >= 2 else two_d
        gn = g_shard[n].reshape(as_rows(w_loc[n]).shape)
        d_, m_, v_ = _adamw(as_rows(w_loc[n]), gn, as_rows(m_loc[n]), as_rows(v_loc[n]), "adamw_" + n)
        grads.append(gn.reshape(shape))
        deltas.append(d_.reshape(shape))
        new_m.append(m_.reshape(shape))
        new_v.append(v_.reshape(shape))
    return (loss, grad_x, *grads, *deltas, *new_m, *new_v)
```

```python
import functools
from typing import Callable, NamedTuple, Optional

import jax
import jax.numpy as jnp
from jax import lax
from jax.experimental import pallas as pl
from jax.experimental.pallas import tpu as pltpu

F32 = jnp.float32
BF16 = jnp.bfloat16
SDS = jax.ShapeDtypeStruct
MESH = pl.DeviceIdType.MESH

D_MODEL = 1024
D_INNER = 2048
N_HEADS = 32
HEAD_P = 64
N_GROUPS = 4
HEADS_PER_GROUP = N_HEADS // N_GROUPS
D_STATE = 128
CONV_K = 4
CHUNK = 128
CONV_DIM = D_INNER + 2 * N_GROUPS * D_STATE
GROUP_W = D_INNER // N_GROUPS + 2 * D_STATE
ATT_HEADS = 12
ATT_D = 128
ATT_SLOTS = 4
ATT_W = ATT_SLOTS * ATT_D
ATT_DILATIONS = (1, 4, 16)
ATT_BLOCK = 128
QKV_DIM = 3 * ATT_HEADS * ATT_D
D_FF = 2816
DT_PAD = 128
ROPE_THETA = 10000.0
EPS = 1e-6
N_CHIPS = 4
LANES = 128

ADAM_LR = 0.001
ADAM_B1 = 0.9
ADAM_B2 = 0.999
ADAM_EPS = 1e-08
ADAM_WD = 0.01
ADAM_STEP = 10

VMEM_LIMIT = 48 * 1024 * 1024


def _cparams(semantics):
    return pltpu.CompilerParams(dimension_semantics=semantics, vmem_limit_bytes=VMEM_LIMIT)


def _pick(n, cap):
    best = None
    for t in range(LANES, min(n, cap) + 1, LANES):
        if n % t == 0:
            best = t
    return best or n


def _row_tile(rows, cap):
    best = None
    for t in range(8, min(rows, cap) + 1, 8):
        if rows % t == 0:
            best = t
    return best or rows


def _sigmoid(x):
    return 1.0 / (1.0 + jnp.exp(-x))


def _softplus(x):
    return jnp.maximum(x, 0.0) + jnp.log(1.0 + jnp.exp(-jnp.abs(x)))


def _dot(a, b):
    return jnp.dot(a, b, preferred_element_type=F32)


def _dot_nt(a, b):
    return lax.dot_general(a, b, (((1,), (1,)), ((), ())), preferred_element_type=F32)


def _dot_tn(a, b):
    return lax.dot_general(a, b, (((0,), (0,)), ((), ())), preferred_element_type=F32)


def _mm(a, b, mode, out_dtype, name, add=None, side=None):
    if mode == "nn":
        (m, k), (_, n) = a.shape, b.shape
    elif mode == "nt":
        (m, k), (n, _) = a.shape, b.shape
    else:
        (k, m), (_, n) = a.shape, b.shape
    tm, tn = _pick(m, 1536), _pick(n, 2048)
    tk = k if k <= 2048 else _pick(k, 2048)
    nk = k // tk
    dims = {"nn": ((1,), (0,)), "nt": ((1,), (1,)), "tn": ((0,), (0,))}[mode]

    def partial_product(a_ref, b_ref):
        return lax.dot_general(a_ref[...].astype(BF16), b_ref[...].astype(BF16), (dims, ((), ())),
                               preferred_element_type=F32)

    def body(*refs):
        a_ref, b_ref = refs[:2]
        c_ref = refs[2] if add is not None else None
        o_ref = refs[3] if add is not None else refs[2]

        def finish(r):
            if add is not None:
                r = r + c_ref[...].astype(F32)
            o_ref[...] = r.astype(out_dtype)

        if nk == 1:
            finish(partial_product(a_ref, b_ref))
            return
        acc = refs[-1]
        kk = pl.program_id(2)

        @pl.when(kk == 0)
        def _():
            acc[...] = partial_product(a_ref, b_ref)

        @pl.when((kk > 0) & (kk < nk - 1))
        def _():
            acc[...] += partial_product(a_ref, b_ref)

        @pl.when(kk == nk - 1)
        def _():
            finish(acc[...] + partial_product(a_ref, b_ref))

    a_spec = {"nn": pl.BlockSpec((tm, tk), lambda j, i, q: (i, q)),
              "nt": pl.BlockSpec((tm, tk), lambda j, i, q: (i, q)),
              "tn": pl.BlockSpec((tk, tm), lambda j, i, q: (q, i))}[mode]
    b_spec = {"nn": pl.BlockSpec((tk, tn), lambda j, i, q: (q, j)),
              "nt": pl.BlockSpec((tn, tk), lambda j, i, q: (j, q)),
              "tn": pl.BlockSpec((tk, tn), lambda j, i, q: (q, j))}[mode]
    o_spec = pl.BlockSpec((tm, tn), lambda j, i, q: (i, j))
    ins, specs = [a, b], [a_spec, b_spec]
    if add is not None:
        ins.append(add)
        specs.append(o_spec)
    acc = [pltpu.VMEM((tm, tn), F32)] if nk > 1 else []
    grid = (n // tn, m // tm, nk)
    if side is None:
        return pl.pallas_call(
            body, name=name, grid=grid, in_specs=specs, out_specs=o_spec, out_shape=SDS((m, n), out_dtype),
            scratch_shapes=acc, compiler_params=_cparams(("parallel", "parallel", "arbitrary")))(*ins)
    return pl.pallas_call(
        _attach_side(body, len(ins), 1, side, grid), name=name, grid=grid,
        in_specs=specs + [ANY] * len(side.ins), out_specs=[o_spec] + [ANY] * len(side.out_shapes),
        out_shape=[SDS((m, n), out_dtype)] + list(side.out_shapes), scratch_shapes=acc + list(side.scratch),
        compiler_params=_cparams(("arbitrary", "arbitrary", "arbitrary")))(*ins, *side.ins)


def _rw(name, fn, nsteps, ins, outs, n_acc=0):
    n_in, n_out = len(ins), len(outs)

    def body(*refs):
        i = pl.program_id(0)
        vals = fn(i, *refs[:n_in])
        for q, (r, v) in enumerate(zip(refs[n_in:], vals)):
            if q < n_out - n_acc:
                r[...] = v.astype(r.dtype)
            else:
                @pl.when(i == 0)
                def _(r=r):
                    r[...] = jnp.zeros_like(r)

                r[...] += v

    return pl.pallas_call(
        body, name=name, grid=(nsteps,), in_specs=[s for _, s in ins], out_specs=[s for _, s in outs],
        out_shape=[o for o, _ in outs], compiler_params=_cparams(("arbitrary",)))(*[a for a, _ in ins])


def _rs(tm, w, cb=0):
    return pl.BlockSpec((tm, w), lambda i: (i, cb))


def _fs(shape):
    nd = len(shape)
    return pl.BlockSpec(shape, lambda i: (0,) * nd)


def _colsum(v):
    return jnp.sum(v, axis=0, keepdims=True)


def _rms_fwd(x, g, name):
    t, d = x.shape
    tm = 512

    def fn(i, x_ref, g_ref):
        xv = x_ref[...]
        r = lax.rsqrt(jnp.mean(xv * xv, axis=-1, keepdims=True) + EPS)
        return [xv * r * g_ref[...]]

    return _rw(name, fn, t // tm, [(x, _rs(tm, d)), (g, _fs((1, d)))], [(SDS((t, d), BF16), _rs(tm, d))])[0]


def _rms_bwd(x, dh, g, dres, name):
    t, d = x.shape
    tm = 512

    def fn(i, x_ref, dh_ref, g_ref, dres_ref):
        xv = x_ref[...]
        r = lax.rsqrt(jnp.mean(xv * xv, axis=-1, keepdims=True) + EPS)
        xhat = xv * r
        dhv = dh_ref[...]
        dxhat = dhv * g_ref[...]
        dx = dres_ref[...] + r * (dxhat - xhat * jnp.mean(dxhat * xhat, axis=-1, keepdims=True))
        return [dx, dx, _colsum(dhv * xhat)]

    return _rw(name, fn, t // tm,
               [(x, _rs(tm, d)), (dh, _rs(tm, d)), (g, _fs((1, d))), (dres, _rs(tm, d))],
               [(SDS((t, d), F32), _rs(tm, d)), (SDS((t, d), BF16), _rs(tm, d)), (SDS((1, d), F32), _fs((1, d)))],
               n_acc=1)


def _final_fwd_bwd(x2, target, g):
    t, d = x2.shape
    tm = 512

    def fn(i, x_ref, t_ref, g_ref):
        xv = x_ref[...]
        gv = g_ref[...]
        r = lax.rsqrt(jnp.mean(xv * xv, axis=-1, keepdims=True) + EPS)
        xhat = xv * r
        diff = xhat * gv - t_ref[...]
        lsum = 0.5 * jnp.sum(jnp.sum(diff * diff, axis=-1, keepdims=True) * (1.0 / d), axis=0, keepdims=True)
        dy = diff * (1.0 / d)
        dxhat = dy * gv
        dx = r * (dxhat - xhat * jnp.mean(dxhat * xhat, axis=-1, keepdims=True))
        return [dx, dx, _colsum(dy * xhat), lsum]

    return _rw("final_norm_loss", fn, t // tm,
               [(x2, _rs(tm, d)), (target, _rs(tm, d)), (g, _fs((1, d)))],
               [(SDS((t, d), F32), _rs(tm, d)), (SDS((t, d), BF16), _rs(tm, d)), (SDS((1, d), F32), _fs((1, d))),
                (SDS((1, 1), F32), _fs((1, 1)))], n_acc=2)


CONV_TS = 512
CONV_HALO = 8


def _conv_specs(seq, c):
    ts, tc = CONV_TS, GROUP_W
    hb = ts // CONV_HALO
    u_spec = pl.BlockSpec((ts, tc), lambda j, i: (i, j))
    prev_spec = pl.BlockSpec((CONV_HALO, tc), lambda j, i: (jnp.maximum(i * hb - 1, 0), j))
    w_spec = pl.BlockSpec((CONV_K, tc), lambda j, i: (0, j))
    b_spec = pl.BlockSpec((1, tc), lambda j, i: (0, j))
    return u_spec, prev_spec, w_spec, b_spec


CONV_ROWS = 16


def _conv_pre(i, seq, u_ref, prev_ref, w_ref, b_ref, ext):
    ts = CONV_TS
    first = (i % (seq // ts)) == 0
    ext[0:CONV_HALO, :] = jnp.where(first, 0.0, prev_ref[...])
    ext[CONV_HALO:, :] = u_ref[...]
    acc = jnp.broadcast_to(b_ref[...], u_ref.shape)
    for q in range(CONV_K):
        acc = acc + w_ref[q:q + 1, :] * ext[pl.ds(CONV_HALO - CONV_K + 1 + q, ts), :]
    return acc


def _conv_fwd(u, w, b, seq):
    t, c = u.shape
    ts, tc = CONV_TS, GROUP_W
    u_spec, prev_spec, w_spec, b_spec = _conv_specs(seq, c)

    def body(u_ref, prev_ref, w_ref, b_ref, o_ref, ext):
        pre = _conv_pre(pl.program_id(1), seq, u_ref, prev_ref, w_ref, b_ref, ext)
        o_ref[...] = pre * _sigmoid(pre)

    return pl.pallas_call(
        body, name="conv_fwd", grid=(c // tc, t // ts), in_specs=[u_spec, prev_spec, w_spec, b_spec],
        out_specs=u_spec, out_shape=SDS((t, c), F32), scratch_shapes=[pltpu.VMEM((ts + CONV_HALO, tc), F32)],
        compiler_params=_cparams(("parallel", "arbitrary")))(u, u, w, b)


def _conv_bwd_pre(u, w, b, dxc, seq):
    t, c = u.shape
    ts, tc = CONV_TS, GROUP_W
    u_spec, prev_spec, w_spec, b_spec = _conv_specs(seq, c)

    def body(u_ref, prev_ref, w_ref, b_ref, d_ref, dpre_ref, dw_ref, db_ref, ext):
        i = pl.program_id(1)
        pre = _conv_pre(i, seq, u_ref, prev_ref, w_ref, b_ref, ext)
        sg = _sigmoid(pre)
        dpre = d_ref[...] * sg * (1.0 + pre * (1.0 - sg))
        dpre_ref[...] = dpre

        @pl.when(i == 0)
        def _():
            dw_ref[...] = jnp.zeros_like(dw_ref)
            db_ref[...] = jnp.zeros_like(db_ref)

        db_ref[...] += _colsum(dpre)
        for q in range(CONV_K):
            dw_ref[q:q + 1, :] += _colsum(dpre * ext[pl.ds(CONV_HALO - CONV_K + 1 + q, ts), :])

    return pl.pallas_call(
        body, name="conv_bwd_pre", grid=(c // tc, t // ts),
        in_specs=[u_spec, prev_spec, w_spec, b_spec, u_spec], out_specs=[u_spec, w_spec, b_spec],
        out_shape=[SDS((t, c), F32), SDS((CONV_K, c), F32), SDS((1, c), F32)],
        scratch_shapes=[pltpu.VMEM((ts + CONV_HALO, tc), F32)],
        compiler_params=_cparams(("parallel", "arbitrary")))(u, u, w, b, dxc)


def _conv_bwd_in(dpre, w, seq):
    t, c = dpre.shape
    ts, tc = CONV_TS, GROUP_W
    hb = ts // CONV_HALO
    last = t // CONV_HALO - 1
    d_spec = pl.BlockSpec((ts, tc), lambda j, i: (i, j))
    next_spec = pl.BlockSpec((CONV_HALO, tc), lambda j, i: (jnp.minimum((i + 1) * hb, last), j))
    w_spec = pl.BlockSpec((CONV_K, tc), lambda j, i: (0, j))

    def body(d_ref, next_ref, w_ref, o_ref, ext):
        i = pl.program_id(1)
        nts = seq // ts
        is_last = (i % nts) == nts - 1
        ext[0:ts, :] = d_ref[...]
        ext[ts:, :] = jnp.where(is_last, 0.0, next_ref[...])
        wv = w_ref[...]

        def rows(j, carry):
            r0 = pl.multiple_of(j * CONV_ROWS, CONV_ROWS)
            blk = ext[pl.ds(r0, CONV_ROWS + CONV_HALO), :]
            acc = wv[CONV_K - 1:CONV_K] * blk[0:CONV_ROWS]
            for q in range(CONV_K - 1):
                acc = acc + wv[q:q + 1] * blk[CONV_K - 1 - q:CONV_K - 1 - q + CONV_ROWS]
            o_ref[pl.ds(r0, CONV_ROWS), :] = acc.astype(o_ref.dtype)
            return carry

        lax.fori_loop(0, ts // CONV_ROWS, rows, 0)

    return pl.pallas_call(
        body, name="conv_bwd_in", grid=(c // tc, t // ts), in_specs=[d_spec, next_spec, w_spec],
        out_specs=d_spec, out_shape=SDS((t, c), BF16), scratch_shapes=[pltpu.VMEM((ts + CONV_HALO, tc), F32)],
        compiler_params=_cparams(("parallel", "arbitrary")))(dpre, dpre, w)


def _split3(v):
    hi = v.astype(BF16)
    r1 = v - hi.astype(F32)
    mid = r1.astype(BF16)
    lo = (r1 - mid.astype(F32)).astype(BF16)
    return hi, mid, lo


def _ssd_prelude(dtr_ref, dtrt_ref, bias_ref, biast_ref, a_ref, at_ref):
    dt = _softplus(dtr_ref[...] + bias_ref[...])
    dtt = _softplus(dtrt_ref[...] + biast_ref[...])
    ri = lax.broadcasted_iota(jnp.int32, (CHUNK, CHUNK), 0)
    ci = lax.broadcasted_iota(jnp.int32, (CHUNK, CHUNK), 1)
    lower = ri >= ci
    upper = ri <= ci
    lower_b = jnp.where(lower, 1.0, 0.0).astype(BF16)
    upper_b = jnp.where(upper, 1.0, 0.0).astype(BF16)
    acs = sum(_dot(lower_b, p) for p in _split3(dt * a_ref[...]))
    acst = sum(_dot(p, upper_b) for p in _split3(dtt * at_ref[...]))
    return dt, acs, acst, lower, upper, lower_b, upper_b


SSD_FWD_GPS = 2
SSD_BWD_GPS = 1


def _ssd_specs(seq, gps):
    nc = seq // CHUNK
    hg = HEADS_PER_GROUP
    fwd = lambda c: c
    rev = lambda c: nc - 1 - c

    def specs(cc):
        return dict(
            xc=pl.BlockSpec((CHUNK, gps * GROUP_W), lambda g, b, c: (b * nc + cc(c), g)),
            y=pl.BlockSpec((CHUNK, gps * hg * HEAD_P), lambda g, b, c: (b * nc + cc(c), g)),
            dtr=pl.BlockSpec((gps, CHUNK, hg), lambda g, b, c: (g, b * nc + cc(c), 0)),
            dtrt=pl.BlockSpec((gps, None, hg, CHUNK), lambda g, b, c: (g, b, 0, cc(c))),
            prow=pl.BlockSpec((gps, 1, hg), lambda g, b, c: (g, 0, 0)),
            pcol=pl.BlockSpec((gps, hg, 1), lambda g, b, c: (g, 0, 0)),
            st=pl.BlockSpec((gps, None, None, D_STATE, hg * HEAD_P), lambda g, b, c: (g, b, cc(c), 0, 0)),
        )

    return specs(fwd), specs(rev)


def _group_views(refs, lane_widths, gi):
    return [r.at[:, gi * w:(gi + 1) * w] if w else r.at[gi] for r, w in zip(refs, lane_widths)]


def _head_maps():
    hw = HEADS_PER_GROUP * HEAD_P
    shift = HEAD_P.bit_length() - 1
    hj = lax.broadcasted_iota(jnp.int32, (HEADS_PER_GROUP, hw), 0)
    lq = jnp.right_shift(lax.broadcasted_iota(jnp.int32, (HEADS_PER_GROUP, hw), 1), shift)
    spread = jnp.where(hj == lq, 1.0, 0.0).astype(BF16)
    rq = jnp.right_shift(lax.broadcasted_iota(jnp.int32, (hw, LANES), 0), shift)
    cj = lax.broadcasted_iota(jnp.int32, (hw, LANES), 1)
    gather = jnp.where(rq == cj, 1.0, 0.0).astype(BF16)
    return spread, gather


def _dot01(v, m01):
    hi, mid, _ = _split3(v)
    return _dot(hi, m01) + _dot(mid, m01)


class _Side(NamedTuple):
    ins: tuple
    out_shapes: tuple
    scratch: tuple
    first: Callable
    mid: Optional[Callable]
    last: Callable


NO_SIDE = _Side((), (), (), lambda *refs: None, None, lambda *refs: None)


def _attach_side(body, n_in, n_out, side, grid):
    si, so, ss = len(side.ins), len(side.out_shapes), len(side.scratch)

    def wrapped(*refs):
        ins, s_in = refs[:n_in], refs[n_in:n_in + si]
        outs = refs[n_in + si:n_in + si + n_out]
        s_out = refs[n_in + si + n_out:n_in + si + n_out + so]
        rest = refs[n_in + si + n_out + so:]
        scr, s_scr = rest[:len(rest) - ss], rest[len(rest) - ss:]
        ids = [pl.program_id(a) for a in range(len(grid))]
        inner_first = functools.reduce(lambda p, q: p & q, [i == 0 for i in ids[1:]], ids[0] >= 0)
        at_last = functools.reduce(lambda p, q: p & q, [i == n - 1 for i, n in zip(ids, grid)])

        @pl.when((ids[0] == 0) & inner_first)
        def _():
            side.first(s_in, s_out, s_scr)

        if side.mid is not None:
            outer_last = functools.reduce(lambda p, q: p & q, [i == n - 1 for i, n in zip(ids[:-1], grid[:-1])])

            @pl.when(outer_last & (ids[-1] == 0))
            def _():
                side.mid(s_in, s_out, s_scr)

        body(*ins, *outs, *scr)

        @pl.when(at_last)
        def _():
            side.last(s_in, s_out, s_scr)

    return wrapped


def _ssd_fwd(xc, dtr, dtrt, bias, biast, a, at, dskip, nb, seq, side):
    t = xc.shape[0]
    nc = seq // CHUNK
    hg = HEADS_PER_GROUP
    hw = hg * HEAD_P
    gps = SSD_FWD_GPS
    grid = (N_GROUPS // gps, nb, nc)
    sp, _ = _ssd_specs(seq, gps)

    def body(*refs):
        for gi in range(gps):
            one_group(*_group_views(refs, (GROUP_W, 0, 0, 0, 0, 0, 0, 0, hw, 0, 0), gi))

    def one_group(xc_ref, dtr_ref, dtrt_ref, bias_ref, biast_ref, a_ref, at_ref, d_ref, y_ref, sin_ref, st):
        @pl.when(pl.program_id(2) == 0)
        def _():
            st[...] = jnp.zeros_like(st)

        s_in = st[...]
        sin_ref[...] = s_in
        dt, acs, acst, lower, _, _, _ = _ssd_prelude(dtr_ref, dtrt_ref, bias_ref, biast_ref, a_ref, at_ref)
        spread, _ = _head_maps()
        x = xc_ref[...]
        xs = x[:, :hw]
        b16 = x[:, hw:hw + D_STATE].astype(BF16)
        c16 = x[:, hw + D_STATE:].astype(BF16)
        cb = _dot_nt(c16, b16)
        last = acs[CHUNK - 1:CHUNK, :]
        e_x = _dot01(jnp.exp(acs), spread)
        dec_x = _dot01(jnp.exp(last - acs), spread)
        tot_x = e_x[CHUNK - 1:CHUNK, :]
        d_x = _dot01(jnp.broadcast_to(d_ref[...], (8, hg)), spread)[0:1, :]
        xdtf = xs * _dot01(dt, spread)
        xdt16 = xdtf.astype(BF16)
        yoff = e_x * _dot(c16, s_in.astype(BF16))
        st[...] = tot_x * s_in + _dot_tn(b16, (dec_x * xdtf).astype(BF16))
        parts = []
        for j in range(hg):
            decay = jnp.exp(jnp.where(lower, acs[:, j:j + 1] - acst[j:j + 1, :], -jnp.inf))
            parts.append(_dot((cb * decay).astype(BF16), xdt16[:, HEAD_P * j:HEAD_P * (j + 1)]))
        y_ref[...] = jnp.concatenate(parts, axis=-1) + yoff + d_x * xs

    return pl.pallas_call(
        _attach_side(body, 8, 2, side, grid), name="ssd_fwd", grid=grid,
        in_specs=[sp["xc"], sp["dtr"], sp["dtrt"], sp["prow"], sp["pcol"], sp["prow"], sp["pcol"], sp["prow"]]
        + [ANY] * len(side.ins),
        out_specs=[sp["y"], sp["st"]] + [ANY] * len(side.out_shapes),
        out_shape=[SDS((t, D_INNER), F32), SDS((N_GROUPS, nb, nc, D_STATE, hw), F32)] + list(side.out_shapes),
        scratch_shapes=[pltpu.VMEM((gps, D_STATE, hw), F32)] + list(side.scratch),
        compiler_params=_cparams(("arbitrary", "arbitrary", "arbitrary")))(
            xc, dtr, dtrt, bias, biast, a, at, dskip, *side.ins)


def _ssd_bwd(xc, dtr, dtrt, bias, biast, a, at, dskip, states, dy, nb, seq, side):
    t = xc.shape[0]
    nc = seq // CHUNK
    hg = HEADS_PER_GROUP
    hw = hg * HEAD_P
    gps = SSD_BWD_GPS
    grid = (N_GROUPS // gps, nb, nc)
    _, sp = _ssd_specs(seq, gps)

    def body(*refs):
        for gi in range(gps):
            one_group(*_group_views(refs, (GROUP_W, 0, 0, 0, 0, 0, 0, 0, 0, hw, GROUP_W, 0, 0, 0, 0, 0), gi))

    def one_group(xc_ref, dtr_ref, dtrt_ref, bias_ref, biast_ref, a_ref, at_ref, d_ref, sin_ref, dy_ref,
                  dxc_ref, ddtr_ref, gbias_ref, ga_ref, gd_ref, ds):
        first = (pl.program_id(1) == 0) & (pl.program_id(2) == 0)

        @pl.when(pl.program_id(2) == 0)
        def _():
            ds[...] = jnp.zeros_like(ds)

        @pl.when(first)
        def _():
            gbias_ref[...] = jnp.zeros_like(gbias_ref)
            ga_ref[...] = jnp.zeros_like(ga_ref)
            gd_ref[...] = jnp.zeros_like(gd_ref)

        dt, acs, acst, lower, upper, _, upper_b = _ssd_prelude(dtr_ref, dtrt_ref, bias_ref, biast_ref, a_ref, at_ref)
        spread, gather = _head_maps()
        x = xc_ref[...]
        dy = dy_ref[...]
        xs = x[:, :hw]
        b16 = x[:, hw:hw + D_STATE].astype(BF16)
        c16 = x[:, hw + D_STATE:].astype(BF16)
        dy16 = dy.astype(BF16)
        cb = _dot_nt(c16, b16)
        cbt = _dot_nt(b16, c16)
        last = acs[CHUNK - 1:CHUNK, :]
        e8 = jnp.exp(acs)
        dec8 = jnp.exp(last - acs)
        e_x = _dot01(e8, spread)
        dec_x = _dot01(dec8, spread)
        tot_x = e_x[CHUNK - 1:CHUNK, :]
        dt_x = _dot01(dt, spread)
        d_x = _dot01(jnp.broadcast_to(d_ref[...], (8, hg)), spread)[0:1, :]
        xdtf = xs * dt_x
        xdt16 = xdtf.astype(BF16)
        s_in = sin_ref[...]
        s16 = s_in.astype(BF16)
        ds_out = ds[...]
        ds16 = ds_out.astype(BF16)
        bds = _dot(b16, ds16)
        cs = _dot(c16, s16)
        edy16 = (e_x * dy).astype(BF16)
        ds[...] = tot_x * ds_out + _dot_tn(c16, edy16)
        lane8 = lax.broadcasted_iota(jnp.int32, (CHUNK, hg), 1)
        row8 = lax.broadcasted_iota(jnp.int32, (CHUNK, hg), 0)
        dacs8 = jnp.zeros((CHUNK, hg), F32)
        acc_m = jnp.zeros((CHUNK, CHUNK), F32)
        acc_mt = jnp.zeros((CHUNK, CHUNK), F32)
        dx_parts = []
        for j in range(hg):
            sl = slice(HEAD_P * j, HEAD_P * (j + 1))
            col = acs[:, j:j + 1]
            row = acst[j:j + 1, :]
            decay = jnp.exp(jnp.where(lower, col - row, -jnp.inf))
            decayt = jnp.exp(jnp.where(upper, row - col, -jnp.inf))
            wm = _dot_nt(dy16[:, sl], xdt16[:, sl]) * decay
            wmt = _dot_nt(xdt16[:, sl], dy16[:, sl]) * decayt
            acc_m = acc_m + wm
            acc_mt = acc_mt + wmt
            dacs8 = dacs8 + jnp.where(lane8 == j, jnp.sum(wm * cb, axis=-1, keepdims=True)
                                      - jnp.sum(wmt * cbt, axis=-1, keepdims=True), 0.0)
            dx_parts.append(_dot((cbt * decayt).astype(BF16), dy16[:, sl]))
        dx = jnp.concatenate(dx_parts, axis=-1) + dec_x * bds
        dxc_ref[:, :hw] = dx * dt_x + d_x * dy
        dxc_ref[:, hw:hw + D_STATE] = _dot(acc_mt.astype(BF16), c16) + _dot_nt((dec_x * xdtf).astype(BF16), ds16)
        dxc_ref[:, hw + D_STATE:] = _dot(acc_m.astype(BF16), b16) + _dot_nt(edy16, s16)
        dtot_rows = jnp.broadcast_to(_colsum(ds_out * s_in), (8, hw))
        sums = _dot01(jnp.concatenate([dy * cs, xdtf * bds, dx * xs, dy * xs, dtot_rows], axis=0), gather)
        de8 = sums[0:CHUNK, :hg]
        ddec8 = sums[CHUNK:2 * CHUNK, :hg]
        ddtx8 = sums[2 * CHUNK:3 * CHUNK, :hg]
        gd8 = _colsum(sums[3 * CHUNK:4 * CHUNK, :hg])
        dtot8 = sums[4 * CHUNK:4 * CHUNK + 1, :hg]
        extra = _colsum(ddec8 * dec8) + dtot8 * e8[CHUNK - 1:CHUNK, :]
        dacs8 = dacs8 + de8 * e8 - ddec8 * dec8 + jnp.where(row8 == CHUNK - 1, extra, 0.0)
        da = sum(_dot(upper_b, p) for p in _split3(dacs8))
        av = a_ref[...]
        ddt = da * av + ddtx8
        ddtr = ddt * _sigmoid(dtr_ref[...] + bias_ref[...])
        ddtr_ref[...] = ddtr
        gbias_ref[...] += _colsum(ddtr)
        ga_ref[...] += _colsum(da * dt) * av
        gd_ref[...] += gd8

    return pl.pallas_call(
        _attach_side(body, 10, 5, side, grid), name="ssd_bwd", grid=grid,
        in_specs=[sp["xc"], sp["dtr"], sp["dtrt"], sp["prow"], sp["pcol"], sp["prow"], sp["pcol"], sp["prow"],
                  sp["st"], sp["y"]] + [ANY] * len(side.ins),
        out_specs=[sp["xc"], sp["dtr"], sp["prow"], sp["prow"], sp["prow"]] + [ANY] * len(side.out_shapes),
        out_shape=[SDS((t, N_GROUPS * GROUP_W), F32), SDS((N_GROUPS, t, hg), F32)]
        + [SDS((N_GROUPS, 1, hg), F32)] * 3 + list(side.out_shapes),
        scratch_shapes=[pltpu.VMEM((gps, D_STATE, hw), F32)] + list(side.scratch),
        compiler_params=_cparams(("arbitrary", "arbitrary", "arbitrary")))(
            xc, dtr, dtrt, bias, biast, a, at, dskip, states, dy, *side.ins)


def _group_bcast(v, width, fn):
    parts = []
    for q in range(v.shape[-1] // width):
        s = fn(v[:, q * width:(q + 1) * width])
        parts.append(jnp.broadcast_to(s, (v.shape[0], width)))
    return jnp.concatenate(parts, axis=-1)


def _gate_norm_fwd(y, z, g):
    t, d = y.shape
    tm = 256
    gw = d // N_GROUPS

    def fn(i, y_ref, z_ref, g_ref):
        zv = z_ref[...].astype(F32)
        u = y_ref[...] * (zv * _sigmoid(zv))
        r = lax.rsqrt(_group_bcast(u * u, gw, lambda p: jnp.mean(p, axis=-1, keepdims=True)) + EPS)
        return [u * r * g_ref[...]]

    return _rw("gate_norm_fwd", fn, t // tm, [(y, _rs(tm, d)), (z, _rs(tm, d)), (g, _fs((1, d)))],
               [(SDS((t, d), BF16), _rs(tm, d))])[0]


def _gate_norm_bwd(y, z, g, dyn):
    t, d = y.shape
    tm = 256
    gw = d // N_GROUPS

    def fn(i, y_ref, z_ref, g_ref, dyn_ref):
        zv = z_ref[...].astype(F32)
        yv = y_ref[...]
        sg = _sigmoid(zv)
        sz = zv * sg
        u = yv * sz
        r = lax.rsqrt(_group_bcast(u * u, gw, lambda p: jnp.mean(p, axis=-1, keepdims=True)) + EPS)
        uhat = u * r
        dv = dyn_ref[...].astype(F32)
        duhat = dv * g_ref[...]
        du = r * (duhat - uhat * _group_bcast(duhat * uhat, gw, lambda p: jnp.mean(p, axis=-1, keepdims=True)))
        dz = du * yv * sg * (1.0 + zv * (1.0 - sg))
        return [du * sz, dz, _colsum(dv * uhat)]

    return _rw("gate_norm_bwd", fn, t // tm,
               [(y, _rs(tm, d)), (z, _rs(tm, d)), (g, _fs((1, d))), (dyn, _rs(tm, d))],
               [(SDS((t, d), F32), _rs(tm, d)), (SDS((t, d), BF16), _rs(tm, d)), (SDS((1, d), F32), _fs((1, d)))],
               n_acc=1)


def _rope_tables(seq):
    half = ATT_D // 2
    inv = ROPE_THETA ** (-jnp.arange(half, dtype=F32) / half)
    ang = jnp.arange(seq, dtype=F32)[:, None] * inv[None, :]
    cos, sin = jnp.cos(ang), jnp.sin(ang)
    return jnp.concatenate([cos, cos], axis=-1), jnp.concatenate([-sin, sin], axis=-1)


ATT_TILE = 512
ATT_QB = 4


def _strided_spec(r, mtiles):
    return pl.BlockSpec((None, r, None, ATT_TILE // r, ATT_W), lambda i: (i // mtiles, 0, i % mtiles, 0, 0))


def _strided_shape(nb, r, mtiles, dtype):
    return SDS((nb, r, mtiles, ATT_TILE // r, ATT_W), dtype)


def _to_strided(val, out_ref, lanes, r, sc):
    if r == 1:
        out_ref[0, :, lanes] = val.astype(out_ref.dtype)
        return
    sc[...] = val
    for rr in range(r):
        out_ref[rr, :, lanes] = sc[pl.ds(rr, ATT_TILE // r, stride=r), :].astype(out_ref.dtype)


def _from_strided(in_ref, lanes, r, sc):
    if r == 1:
        return in_ref[0, :, lanes].astype(F32)
    for rr in range(r):
        sc[pl.ds(rr, ATT_TILE // r, stride=r), :] = in_ref[rr, :, lanes].astype(F32)
    return sc[...]


def _rope_fwd(qkv, cos, sin, nb, seq):
    t = qkv.shape[0]
    tm = ATT_TILE
    mtiles = seq // tm
    w = ATT_HEADS * ATT_D
    tab = pl.BlockSpec((tm, ATT_D), lambda i: (i % mtiles, 0))
    ng = len(ATT_DILATIONS)

    def body(q_ref, k_ref, v_ref, cos_ref, sin_ref, *rest):
        outs, sc = rest[:3 * ng], rest[3 * ng]
        c, s = cos_ref[...], sin_ref[...]
        for which, ref in enumerate((q_ref, k_ref, v_ref)):
            for h in range(ATT_HEADS):
                g, slot = divmod(h, ATT_SLOTS)
                p = ref[:, h * ATT_D:(h + 1) * ATT_D].astype(F32)
                if which < 2:
                    p = p * c + pltpu.roll(p, ATT_D // 2, 1) * s
                _to_strided(p, outs[which * ng + g], slice(slot * ATT_D, (slot + 1) * ATT_D), ATT_DILATIONS[g], sc)

    out_specs = [_strided_spec(r, mtiles) for _ in range(3) for r in ATT_DILATIONS]
    out_shape = [_strided_shape(nb, r, mtiles, BF16) for _ in range(3) for r in ATT_DILATIONS]
    outs = pl.pallas_call(
        body, name="rope_fwd", grid=(t // tm,),
        in_specs=[_rs(tm, w, 0), _rs(tm, w, 1), _rs(tm, w, 2), tab, tab], out_specs=out_specs, out_shape=out_shape,
        scratch_shapes=[pltpu.VMEM((tm, ATT_D), F32)], compiler_params=_cparams(("arbitrary",)))(
            qkv, qkv, qkv, cos, sin)
    flat = [o.reshape(t, ATT_W) for o in outs]
    return flat[0:ng], flat[ng:2 * ng], flat[2 * ng:]


def _rope_bwd(dq, dk, dv, cos, sin, nb, seq):
    t = dq[0].shape[0]
    tm = ATT_TILE
    mtiles = seq // tm
    w = ATT_HEADS * ATT_D
    tab = pl.BlockSpec((tm, ATT_D), lambda i: (i % mtiles, 0))
    ng = len(ATT_DILATIONS)

    def body(*refs):
        ins, (cos_ref, sin_ref, o_ref, sc) = refs[:3 * ng], refs[3 * ng:]
        c, s = cos_ref[...], sin_ref[...]
        for which in range(3):
            for h in range(ATT_HEADS):
                g, slot = divmod(h, ATT_SLOTS)
                p = _from_strided(ins[which * ng + g], slice(slot * ATT_D, (slot + 1) * ATT_D), ATT_DILATIONS[g], sc)
                if which < 2:
                    p = p * c - pltpu.roll(p, ATT_D // 2, 1) * s
                o_ref[:, which * w + h * ATT_D:which * w + (h + 1) * ATT_D] = p.astype(o_ref.dtype)

    views = [a.reshape(nb, r, mtiles, tm // r, ATT_W) for grp in (dq, dk, dv) for a, r in zip(grp, ATT_DILATIONS)]
    return pl.pallas_call(
        body, name="rope_bwd", grid=(t // tm,),
        in_specs=[_strided_spec(r, mtiles) for _ in range(3) for r in ATT_DILATIONS] + [tab, tab],
        out_specs=_rs(tm, 3 * w), out_shape=SDS((t, 3 * w), BF16),
        scratch_shapes=[pltpu.VMEM((tm, ATT_D), F32)], compiler_params=_cparams(("arbitrary",)))(*views, cos, sin)


def _att_masks():
    ri = lax.broadcasted_iota(jnp.int32, (ATT_BLOCK, ATT_BLOCK), 0)
    ci = lax.broadcasted_iota(jnp.int32, (ATT_BLOCK, ATT_BLOCK), 1)
    return ci <= ri, ci >= ri


def _att_fwd(q, k, v, g, seq):
    t, w = q.shape
    rows = ATT_QB * ATT_BLOCK
    nbs = seq // ATT_DILATIONS[g] // ATT_BLOCK
    scale = ATT_D ** -0.5
    cur = pl.BlockSpec((rows, w), lambda n: (n, 0))
    prev = pl.BlockSpec((ATT_BLOCK, w), lambda n: (jnp.maximum(n * ATT_QB - 1, 0), 0))

    def body(q_ref, kc_ref, kp_ref, vc_ref, vp_ref, o_ref, lse_ref):
        mcur, mprev = _att_masks()
        for i in range(ATT_QB):
            blk = pl.program_id(0) * ATT_QB + i
            mask = jnp.concatenate([mprev & ((blk % nbs) != 0), mcur], axis=-1)
            own = slice(i * ATT_BLOCK, (i + 1) * ATT_BLOCK)
            for h in range(ATT_SLOTS):
                sl = slice(h * ATT_D, (h + 1) * ATT_D)
                if i == 0:
                    keys = jnp.concatenate([kp_ref[:, sl], kc_ref[own, sl]], axis=0)
                    vals = jnp.concatenate([vp_ref[:, sl], vc_ref[own, sl]], axis=0)
                else:
                    both = slice((i - 1) * ATT_BLOCK, (i + 1) * ATT_BLOCK)
                    keys, vals = kc_ref[both, sl], vc_ref[both, sl]
                s = jnp.where(mask, _dot_nt(q_ref[own, sl], keys) * scale, -jnp.inf)
                m = jnp.max(s, axis=-1, keepdims=True)
                p = jnp.exp(s - m)
                den = jnp.sum(p, axis=-1, keepdims=True)
                o_ref[own, sl] = _dot(p.astype(BF16), vals) / den
                lse_ref[own, sl] = jnp.broadcast_to(m + jnp.log(den), (ATT_BLOCK, ATT_D))

    return pl.pallas_call(
        body, name=f"att_fwd_{g}", grid=(t // rows,), in_specs=[cur, cur, prev, cur, prev], out_specs=[cur, cur],
        out_shape=[SDS((t, w), F32), SDS((t, w), F32)],
        compiler_params=_cparams(("arbitrary",)))(q, k, k, v, v)


def _att_bwd(q, k, v, do, lse, dlt, g, seq):
    t, w = q.shape
    nblk = t // ATT_BLOCK
    rows = ATT_QB * ATT_BLOCK
    nbs = seq // ATT_DILATIONS[g] // ATT_BLOCK
    scale = ATT_D ** -0.5
    cur = pl.BlockSpec((rows, w), lambda n: (n, 0))
    nxt = pl.BlockSpec((ATT_BLOCK, w), lambda n: (jnp.minimum((n + 1) * ATT_QB, nblk - 1), 0))

    def body(qc_ref, qn_ref, k_ref, v_ref, doc_ref, don_ref, lsec_ref, lsen_ref, dltc_ref, dltn_ref,
             dq_ref, dk_ref, dv_ref, carry):
        n = pl.program_id(0)

        @pl.when(n == 0)
        def _():
            carry[...] = jnp.zeros_like(carry)

        mcur, mprev = _att_masks()

        def pair(cur_ref, nxt_ref, i, sl):
            if i + 1 < ATT_QB:
                return cur_ref[i * ATT_BLOCK:(i + 2) * ATT_BLOCK, sl]
            return jnp.concatenate([cur_ref[i * ATT_BLOCK:, sl], nxt_ref[:, sl]], axis=0)

        for h in range(ATT_SLOTS):
            sl = slice(h * ATT_D, (h + 1) * ATT_D)
            from_prev = carry[:, sl]
            for i in range(ATT_QB):
                blk = n * ATT_QB + i
                has_next = (((blk + 1) % nbs) != 0) & (blk + 1 < nblk)
                mask = jnp.concatenate([mcur, mprev & has_next], axis=0)
                own = slice(i * ATT_BLOCK, (i + 1) * ATT_BLOCK)
                kh, vh = k_ref[own, sl], v_ref[own, sl]
                qs, dos = pair(qc_ref, qn_ref, i, sl), pair(doc_ref, don_ref, i, sl)
                lse, dlt = pair(lsec_ref, lsen_ref, i, sl), pair(dltc_ref, dltn_ref, i, sl)
                p = jnp.where(mask, jnp.exp(_dot_nt(qs, kh) * scale - lse), 0.0)
                ds = (p * (_dot_nt(dos, vh) - dlt) * scale).astype(BF16)
                dqs = _dot(ds, kh)
                dq_ref[own, sl] = (from_prev + dqs[:ATT_BLOCK]).astype(dq_ref.dtype)
                from_prev = dqs[ATT_BLOCK:]
                dk_ref[own, sl] = _dot_tn(ds, qs).astype(dk_ref.dtype)
                dv_ref[own, sl] = _dot_tn(p.astype(BF16), dos).astype(dv_ref.dtype)
            carry[:, sl] = from_prev

    return pl.pallas_call(
        body, name=f"att_bwd_{g}", grid=(t // rows,), in_specs=[cur, nxt, cur, cur, cur, nxt, cur, nxt, cur, nxt],
        out_specs=[cur, cur, cur], out_shape=[SDS((t, w), BF16)] * 3,
        scratch_shapes=[pltpu.VMEM((ATT_BLOCK, w), F32)],
        compiler_params=_cparams(("arbitrary",)))(q, q, k, v, do, do, lse, lse, dlt, dlt)


def _merge_weights(ls):
    m = jnp.maximum(jnp.maximum(ls[0], ls[1]), ls[2])
    es = [jnp.exp(v - m) for v in ls]
    den = es[0] + es[1] + es[2]
    return [e / den for e in es]


def _merge_fwd(o, lse, nb, seq):
    t = o[0].shape[0]
    tm = ATT_TILE
    mtiles = seq // tm
    ng = len(ATT_DILATIONS)

    def body(*refs):
        o_refs, l_refs, out_ref, scs = refs[:ng], refs[ng:2 * ng], refs[2 * ng], refs[2 * ng + 1:]
        for slot in range(ATT_SLOTS):
            lanes = slice(slot * ATT_D, (slot + 1) * ATT_D)
            ov = [_from_strided(o_refs[g], lanes, r, scs[2 * g]) for g, r in enumerate(ATT_DILATIONS)]
            ws = _merge_weights([_from_strided(l_refs[g], lanes, r, scs[2 * g + 1])
                                 for g, r in enumerate(ATT_DILATIONS)])
            out_ref[:, lanes] = (ws[0] * ov[0] + ws[1] * ov[1] + ws[2] * ov[2]).astype(out_ref.dtype)

    views = [a.reshape(nb, r, mtiles, tm // r, ATT_W) for grp in (o, lse) for a, r in zip(grp, ATT_DILATIONS)]
    return pl.pallas_call(
        body, name="att_merge_fwd", grid=(t // tm,),
        in_specs=[_strided_spec(r, mtiles) for _ in range(2) for r in ATT_DILATIONS],
        out_specs=_rs(tm, ATT_W), out_shape=SDS((t, ATT_W), BF16),
        scratch_shapes=[pltpu.VMEM((tm, ATT_D), F32)] * (2 * ng), compiler_params=_cparams(("arbitrary",)))(*views)


def _merge_bwd(o, lse, datt, nb, seq):
    t = o[0].shape[0]
    tm = ATT_TILE
    mtiles = seq // tm
    ng = len(ATT_DILATIONS)

    def body(*refs):
        o_refs, l_refs, d_ref = refs[:ng], refs[ng:2 * ng], refs[2 * ng]
        do_refs, dlt_refs = refs[2 * ng + 1:3 * ng + 1], refs[3 * ng + 1:4 * ng + 1]
        scs = refs[4 * ng + 1:]
        for slot in range(ATT_SLOTS):
            lanes = slice(slot * ATT_D, (slot + 1) * ATT_D)
            ov = [_from_strided(o_refs[g], lanes, r, scs[2 * g]) for g, r in enumerate(ATT_DILATIONS)]
            ws = _merge_weights([_from_strided(l_refs[g], lanes, r, scs[2 * g + 1])
                                 for g, r in enumerate(ATT_DILATIONS)])
            dv = d_ref[:, lanes]
            att = ws[0] * ov[0] + ws[1] * ov[1] + ws[2] * ov[2]
            dot = jnp.broadcast_to(jnp.sum(dv * att, axis=-1, keepdims=True), (tm, ATT_D))
            for g, r in enumerate(ATT_DILATIONS):
                _to_strided(ws[g] * dv, do_refs[g], lanes, r, scs[2 * ng])
                _to_strided(ws[g] * dot, dlt_refs[g], lanes, r, scs[2 * ng + 1])

    views = [a.reshape(nb, r, mtiles, tm // r, ATT_W) for grp in (o, lse) for a, r in zip(grp, ATT_DILATIONS)]
    outs = pl.pallas_call(
        body, name="att_merge_bwd", grid=(t // tm,),
        in_specs=[_strided_spec(r, mtiles) for _ in range(2) for r in ATT_DILATIONS] + [_rs(tm, ATT_W)],
        out_specs=[_strided_spec(r, mtiles) for _ in range(2) for r in ATT_DILATIONS],
        out_shape=[_strided_shape(nb, r, mtiles, dt) for dt in (BF16, F32) for r in ATT_DILATIONS],
        scratch_shapes=[pltpu.VMEM((tm, ATT_D), F32)] * (2 * ng + 2), compiler_params=_cparams(("arbitrary",)))(
            *views, datt)
    flat = [a.reshape(t, ATT_W) for a in outs]
    return flat[:ng], flat[ng:]


def _mix_fwd(gate_logits, b_gate, y_ssm, y_att):
    t, d = y_ssm.shape
    tm = 512

    def fn(i, g0_ref, g1_ref, b0_ref, b1_ref, ys_ref, ya_ref):
        g0 = _sigmoid(g0_ref[...].astype(F32) + b0_ref[...])
        g1 = _sigmoid(g1_ref[...].astype(F32) + b1_ref[...])
        return [g0 * ys_ref[...].astype(F32) + g1 * ya_ref[...].astype(F32)]

    b_spec = lambda cb: pl.BlockSpec((1, d), lambda i: (0, cb))
    return _rw("mix_fwd", fn, t // tm,
               [(gate_logits, _rs(tm, d, 0)), (gate_logits, _rs(tm, d, 1)), (b_gate, b_spec(0)), (b_gate, b_spec(1)),
                (y_ssm, _rs(tm, d)), (y_att, _rs(tm, d))],
               [(SDS((t, d), BF16), _rs(tm, d))])[0]


def _mix_bwd(gate_logits, b_gate, y_ssm, y_att, dmixed):
    t, d = y_ssm.shape
    tm = 256

    def fn(i, g0_ref, g1_ref, b0_ref, b1_ref, ys_ref, ya_ref, dm_ref):
        g0 = _sigmoid(g0_ref[...].astype(F32) + b0_ref[...])
        g1 = _sigmoid(g1_ref[...].astype(F32) + b1_ref[...])
        dm = dm_ref[...]
        dg = jnp.concatenate([dm * ys_ref[...].astype(F32) * g0 * (1.0 - g0),
                              dm * ya_ref[...].astype(F32) * g1 * (1.0 - g1)], axis=-1)
        return [dm * g0, dm * g1, dg, _colsum(dg)]

    b_spec = lambda cb: pl.BlockSpec((1, d), lambda i: (0, cb))
    return _rw("mix_bwd", fn, t // tm,
               [(gate_logits, _rs(tm, d, 0)), (gate_logits, _rs(tm, d, 1)), (b_gate, b_spec(0)), (b_gate, b_spec(1)),
                (y_ssm, _rs(tm, d)), (y_att, _rs(tm, d)), (dmixed, _rs(tm, d))],
               [(SDS((t, d), BF16), _rs(tm, d)), (SDS((t, d), BF16), _rs(tm, d)),
                (SDS((t, 2 * d), BF16), _rs(tm, 2 * d)), (SDS((1, 2 * d), F32), _fs((1, 2 * d)))], n_acc=1)


def _swiglu_fwd(gt, up):
    t, f = gt.shape
    tm = 256

    def fn(i, g_ref, u_ref):
        gv = g_ref[...].astype(F32)
        return [gv * _sigmoid(gv) * u_ref[...].astype(F32)]

    return _rw("swiglu_fwd", fn, t // tm, [(gt, _rs(tm, f)), (up, _rs(tm, f))], [(SDS((t, f), BF16), _rs(tm, f))])[0]


def _swiglu_bwd(gt, up, dact):
    t, f = gt.shape
    tm = 256

    def fn(i, g_ref, u_ref, d_ref):
        gv, dv = g_ref[...].astype(F32), d_ref[...].astype(F32)
        sg = _sigmoid(gv)
        return [dv * u_ref[...].astype(F32) * sg * (1.0 + gv * (1.0 - sg)), dv * gv * sg]

    return _rw("swiglu_bwd", fn, t // tm, [(gt, _rs(tm, f)), (up, _rs(tm, f)), (dact, _rs(tm, f))],
               [(SDS((t, f), BF16), _rs(tm, f))] * 2)


def _adamw(w, g, m, v, name):
    r, c = w.shape[-2:]
    lead = w.ndim - 2
    tr = _row_tile(r, max(8, 400_000 // c))
    c1 = 1.0 / (1.0 - ADAM_B1 ** ADAM_STEP)
    c2 = 1.0 / (1.0 - ADAM_B2 ** ADAM_STEP)

    def fn(i, w_ref, g_ref, m_ref, v_ref):
        gv = g_ref[...]
        mn = ADAM_B1 * m_ref[...] + (1.0 - ADAM_B1) * gv
        vn = ADAM_B2 * v_ref[...] + (1.0 - ADAM_B2) * (gv * gv)
        delta = -ADAM_LR * ((mn * c1) / (jnp.sqrt(vn * c2) + ADAM_EPS) + ADAM_WD * w_ref[...])
        return [delta, mn, vn]

    spec = pl.BlockSpec((None,) * lead + (tr, c), lambda i: (0,) * lead + (i, 0))
    return _rw(name, fn, r // tr, [(w, spec), (g, spec), (m, spec), (v, spec)], [(SDS(w.shape, F32), spec)] * 3)


ANY = pl.BlockSpec(memory_space=pl.ANY)


def _place():
    x, y, c = lax.axis_index("x"), lax.axis_index("y"), lax.axis_index("c")
    chips = [(1 - x, y), (x, 1 - y), (1 - x, 1 - y)]
    return x, y, c, chips


def _remote(src, dst, ssem, rsem, to):
    return pltpu.make_async_remote_copy(src_ref=src, dst_ref=dst, send_sem=ssem, recv_sem=rsem, device_id=to,
                                        device_id_type=MESH)


def _copy_through_vmem(src, dst, buf, isem, osem):
    chunk = buf.shape[1]
    n = src.shape[0] // chunk
    load = lambda k: pltpu.make_async_copy(src.at[pl.ds(k * chunk, chunk)], buf.at[k % 2], isem.at[k % 2])
    store = lambda k: pltpu.make_async_copy(buf.at[k % 2], dst.at[pl.ds(k * chunk, chunk)], osem.at[k % 2])
    load(0).start()
    for k in range(n):
        load(k).wait()
        if k + 1 < n:
            if k >= 1:
                store(k - 1).wait()
            load(k + 1).start()
        store(k).start()
    if n >= 2:
        store(n - 2).wait()
    store(n - 1).wait()


def _copy_scratch(rows, width, dtype):
    chunk = _row_tile(rows, 512)
    return [pltpu.VMEM((2, chunk, width), dtype), pltpu.SemaphoreType.DMA((2,)), pltpu.SemaphoreType.DMA((2,))]


def _gather_weights(wp):
    def body(w_ref, out_ref, ssem, rsem, buf, isem, osem):
        x, y, c, chips = _place()
        me = 2 * x + y
        sib = (x, y, 1 - c)
        first = [_remote(w_ref.at[c], out_ref.at[me, c], ssem.at[j], rsem.at[j], (*chip, c))
                 for j, chip in enumerate(chips)]
        for cp in first:
            cp.start()
        for half in range(2):
            _copy_through_vmem(w_ref.at[half], out_ref.at[me, half], buf, isem, osem)
        passed = []
        for j, chip in enumerate(chips):
            ci = 2 * chip[0] + chip[1]
            _remote(w_ref.at[c], out_ref.at[ci, c], ssem.at[j], rsem.at[j], (*chip, c)).wait_recv()
            cp = _remote(out_ref.at[ci, c], out_ref.at[ci, c], ssem.at[3 + j], rsem.at[3 + j], sib)
            cp.start()
            passed.append(cp)
        for j, chip in enumerate(chips):
            ci = 2 * chip[0] + chip[1]
            _remote(out_ref.at[ci, 1 - c], out_ref.at[ci, 1 - c], ssem.at[3 + j], rsem.at[3 + j], sib).wait_recv()
        for cp in first + passed:
            cp.wait_send()

    return pl.pallas_call(
        body, name="gather_weights", in_specs=[ANY], out_specs=ANY,
        out_shape=SDS((N_CHIPS,) + wp.shape, wp.dtype),
        scratch_shapes=[pltpu.SemaphoreType.DMA((6,)), pltpu.SemaphoreType.DMA((6,))]
        + _copy_scratch(wp.shape[1], wp.shape[2], wp.dtype),
        compiler_params=pltpu.CompilerParams(has_side_effects=True))(wp)


def _swap_halves(g2, tag):
    def body(g_ref, out_ref, ssem, rsem):
        x, y, c, _ = _place()
        cp = _remote(g_ref.at[1 - c], out_ref, ssem, rsem, (x, y, 1 - c))
        cp.start()
        cp.wait()

    return pl.pallas_call(
        body, name="swap_halves_" + tag, in_specs=[ANY], out_specs=ANY, out_shape=SDS(g2.shape[1:], g2.dtype),
        scratch_shapes=[pltpu.SemaphoreType.DMA(()), pltpu.SemaphoreType.DMA(())],
        compiler_params=pltpu.CompilerParams(has_side_effects=True))(g2)


def _add_own_half(g2, other, c, tag):
    _, nch, rows, w = g2.shape
    tr = _row_tile(rows, 512)
    nr = rows // tr

    def body(c_ref, a_ref, b_ref, o_ref):
        o_ref[...] = (a_ref[...].astype(F32) + b_ref[...].astype(F32)).astype(o_ref.dtype)

    grid_spec = pltpu.PrefetchScalarGridSpec(
        num_scalar_prefetch=1, grid=(nch, nr),
        in_specs=[pl.BlockSpec((None, None, tr, w), lambda k, i, c_ref: (c_ref[0], k, i, 0)),
                  pl.BlockSpec((None, tr, w), lambda k, i, c_ref: (k, i, 0))],
        out_specs=pl.BlockSpec((None, tr, w), lambda k, i, c_ref: (k, i, 0)))
    return pl.pallas_call(
        body, name="add_own_half_" + tag, grid_spec=grid_spec, out_shape=SDS(other.shape, other.dtype),
        compiler_params=_cparams(("arbitrary", "arbitrary")))(jnp.reshape(c, (1,)).astype(jnp.int32), g2, other)


def _sum_chips(q, tag):
    nch, rows, w = q.shape
    tr = _row_tile(rows, 512)

    def fn(i, q_ref):
        return [((q_ref[0].astype(F32) + q_ref[1].astype(F32)) + q_ref[2].astype(F32)) + q_ref[3].astype(F32)]

    return _rw("sum_chips_" + tag, fn, rows // tr, [(q, pl.BlockSpec((nch, tr, w), lambda i: (0, i, 0)))],
               [(SDS((rows, w), F32), _rs(tr, w))])[0]


def _chip_copies(src_ref, dst_ref, ssem, rsem, outgoing):
    x, y, c, chips = _place()
    me = 2 * x + y
    cps = []
    for j, chip in enumerate(chips):
        ci = 2 * chip[0] + chip[1]
        cps.append(_remote(src_ref.at[ci], dst_ref.at[me if outgoing else ci], ssem.at[j], rsem.at[j], (*chip, c)))
    return cps, me


def _scatter_side(p):
    def first(ins, outs, scr):
        cps, me = _chip_copies(ins[0], outs[0], scr[0], scr[1], True)
        for cp in cps:
            cp.start()
        pltpu.make_async_copy(ins[0].at[me], outs[0].at[me], scr[2]).start()

    def last(ins, outs, scr):
        for cp in _chip_copies(ins[0], outs[0], scr[0], scr[1], False)[0]:
            cp.wait_recv()
        cps, me = _chip_copies(ins[0], outs[0], scr[0], scr[1], True)
        for cp in cps:
            cp.wait_send()
        pltpu.make_async_copy(ins[0].at[me], outs[0].at[me], scr[2]).wait()

    return _Side((p,), (SDS(p.shape, p.dtype),),
                 (pltpu.SemaphoreType.DMA((3,)), pltpu.SemaphoreType.DMA((3,)), pltpu.SemaphoreType.DMA(())),
                 first, None, last)


def _gather_copies(w_ref, out_ref, ssem, rsem):
    x, y, c, chips = _place()
    me = 2 * x + y
    sib = (x, y, 1 - c)
    sends, arrivals, forwards, from_sib = [], [], [], []
    for j, chip in enumerate(chips):
        ci = 2 * chip[0] + chip[1]
        sends.append(_remote(w_ref.at[c], out_ref.at[me, c], ssem.at[j], rsem.at[j], (*chip, c)))
        arrivals.append(_remote(w_ref.at[c], out_ref.at[ci, c], ssem.at[j], rsem.at[j], (*chip, c)))
        forwards.append(_remote(out_ref.at[ci, c], out_ref.at[ci, c], ssem.at[3 + j], rsem.at[3 + j], sib))
        from_sib.append(_remote(out_ref.at[ci, 1 - c], out_ref.at[ci, 1 - c], ssem.at[3 + j], rsem.at[3 + j], sib))
    return sends, arrivals, forwards, from_sib, me


def _gather_side(wp):
    def first(ins, outs, scr):
        sends, _, _, _, me = _gather_copies(ins[0], outs[0], scr[0], scr[1])
        for cp in sends:
            cp.start()
        pltpu.make_async_copy(ins[0], outs[0].at[me], scr[2]).start()

    def mid(ins, outs, scr):
        _, arrivals, forwards, _, _ = _gather_copies(ins[0], outs[0], scr[0], scr[1])
        for arrived, forward in zip(arrivals, forwards):
            arrived.wait_recv()
            forward.start()

    def last(ins, outs, scr):
        sends, _, forwards, from_sib, me = _gather_copies(ins[0], outs[0], scr[0], scr[1])
        for cp in from_sib:
            cp.wait_recv()
        for cp in sends + forwards:
            cp.wait_send()
        pltpu.make_async_copy(ins[0], outs[0].at[me], scr[2]).wait()

    return _Side((wp,), (SDS((N_CHIPS,) + wp.shape, wp.dtype),),
                 (pltpu.SemaphoreType.DMA((6,)), pltpu.SemaphoreType.DMA((6,)), pltpu.SemaphoreType.DMA(())),
                 first, mid, last)


def _allreduce_small(v, name):
    rows, w = v.shape
    offsets = [(dx, dy, dc) for dx in (0, 1) for dy in (0, 1) for dc in (0, 1)][1:]

    def body(v_ref, o_ref, buf, ssem, rsem):
        x, y, c, _ = _place()
        flip = lambda p, d: 1 - p if d else p
        peers = [(flip(x, dx), flip(y, dy), flip(c, dc)) for dx, dy, dc in offsets]
        index = lambda p: 4 * p[0] + 2 * p[1] + p[2]
        me = index((x, y, c))
        buf[me] = v_ref[...]
        sent = [_remote(v_ref, buf.at[me], ssem.at[q], rsem.at[q], p) for q, p in enumerate(peers)]
        for cp in sent:
            cp.start()
        for q, p in enumerate(peers):
            _remote(v_ref, buf.at[index(p)], ssem.at[q], rsem.at[q], p).wait_recv()
        for cp in sent:
            cp.wait_send()
        acc = buf[0]
        for q in range(1, 8):
            acc = acc + buf[q]
        o_ref[...] = acc

    vm = pl.BlockSpec(memory_space=pltpu.VMEM)
    return pl.pallas_call(
        body, name=name, in_specs=[vm], out_specs=vm, out_shape=SDS((rows, w), F32),
        scratch_shapes=[pltpu.VMEM((8, rows, w), F32), pltpu.SemaphoreType.DMA((7,)), pltpu.SemaphoreType.DMA((7,))],
        compiler_params=pltpu.CompilerParams(has_side_effects=True))(v)


def _join_halves(h, tag):
    def body(h_ref, out_ref, ssem, rsem, buf, isem, osem):
        x, y, c, _ = _place()
        cp = _remote(h_ref, out_ref.at[c], ssem, rsem, (x, y, 1 - c))
        cp.start()
        _copy_through_vmem(h_ref, out_ref.at[c], buf, isem, osem)
        _remote(h_ref, out_ref.at[1 - c], ssem, rsem, (x, y, 1 - c)).wait_recv()
        cp.wait_send()

    return pl.pallas_call(
        body, name="join_halves_" + tag, in_specs=[ANY], out_specs=ANY, out_shape=SDS((2,) + h.shape, h.dtype),
        scratch_shapes=[pltpu.SemaphoreType.DMA(()), pltpu.SemaphoreType.DMA(())]
        + _copy_scratch(h.shape[0], h.shape[1], h.dtype),
        compiler_params=pltpu.CompilerParams(has_side_effects=True))(h)


PACK_W = 1024
SHARDED = ("w_in", "w_ffn_gate", "w_ffn_up", "w_ssm_out", "w_att_out", "w_mix_out", "w_ffn_down")
COL_SHARDED = ("w_in", "w_ffn_gate", "w_ffn_up", "w_att_out")
SMALL = ("norm_mix", "b_gate", "conv_b", "dt_bias", "a_log", "d_skip", "ssm_norm", "norm_ffn", "norm_final")


PACK_ROW_ALIGN = 16


def _rows(n):
    return -(-n // (PACK_W * PACK_ROW_ALIGN)) * PACK_ROW_ALIGN


def _pack_rows(parts, total_rows):
    rows = []
    for p in parts:
        size = int(p.size)
        if size % PACK_W:
            p = jnp.pad(p.reshape(-1), (0, PACK_W - size % PACK_W))
        p = p.reshape(-1, PACK_W)
        rows.append(jnp.pad(p, ((0, _rows(size) - p.shape[0]), (0, 0))))
    used = sum(r.shape[0] for r in rows)
    if total_rows > used:
        rows.append(jnp.zeros((total_rows - used, PACK_W), rows[0].dtype))
    return jnp.concatenate(rows, axis=0)


def _padded_rows(n):
    return -(-n // 32) * 32


def _wire_name(name):
    return name + "_t" if name in COL_SHARDED else name


def _wire_shard(w, name):
    return w.T if name in COL_SHARDED else w


def _group_major(a, axis):
    gw = D_INNER // N_GROUPS
    take = lambda lo, n: lax.slice_in_dim(a, lo, lo + n, axis=axis)
    parts = []
    for g in range(N_GROUPS):
        parts += [take(g * gw, gw), take(D_INNER + g * D_STATE, D_STATE),
                  take(D_INNER + N_GROUPS * D_STATE + g * D_STATE, D_STATE)]
    return jnp.concatenate(parts, axis=axis)


def _group_major_inv(a, axis):
    gw = D_INNER // N_GROUPS
    take = lambda lo, n: lax.slice_in_dim(a, lo, lo + n, axis=axis)
    xs = [take(g * GROUP_W, gw) for g in range(N_GROUPS)]
    bs = [take(g * GROUP_W + gw, D_STATE) for g in range(N_GROUPS)]
    cs = [take(g * GROUP_W + gw + D_STATE, D_STATE) for g in range(N_GROUPS)]
    return jnp.concatenate(xs + bs + cs, axis=axis)


LATE = ("w_ffn_gate_t", "w_ffn_up_t", "w_ssm_out", "w_att_out_t", "w_mix_out", "w_ffn_down")


class _Overlap(NamedTuple):
    gather_side: _Side
    late_weights: Callable
    scatter_side: Callable
    scatter_in: Callable


def _local_step(x, target, wts, overlap):
    nb, seq, d = x.shape
    t = nb * seq
    x = x.reshape(t, d)
    target = target.reshape(t, d)
    hg = HEADS_PER_GROUP

    o1, o2, o3, o4 = D_INNER, D_INNER + CONV_DIM, D_INNER + CONV_DIM + N_HEADS, D_INNER + CONV_DIM + N_HEADS + QKV_DIM
    n_in = o4 + 2 * D_MODEL

    def in_rows(lo, hi):
        per = n_in // N_CHIPS
        parts = [wts["w_in_t"][k, max(lo, k * per) - k * per:min(hi, (k + 1) * per) - k * per]
                 for k in range(N_CHIPS) if max(lo, k * per) < min(hi, (k + 1) * per)]
        return parts[0] if len(parts) == 1 else jnp.concatenate(parts, axis=0)

    w_z = in_rows(0, o1)
    w_xbc = _group_major(in_rows(o1, o2), 0)
    w_dt = jnp.pad(in_rows(o2, o3), ((0, DT_PAD - N_HEADS), (0, 0)))
    w_qkv = in_rows(o3, o4)
    w_gate = in_rows(o4, n_in)
    conv_w = _group_major(wts["conv_w"], 1)
    conv_b = _group_major(wts["conv_b"], 1)

    def per_group_row(p):
        return p.reshape(N_GROUPS, 1, hg)

    def per_group_col(p):
        return p.reshape(N_GROUPS, hg, 1)

    a_neg = -jnp.exp(wts["a_log"])
    bias_r, bias_c = per_group_row(wts["dt_bias"]), per_group_col(wts["dt_bias"])
    a_r, a_c = per_group_row(a_neg), per_group_col(a_neg)
    dskip_r = per_group_row(wts["d_skip"])
    cos, sin = _rope_tables(seq)

    h = _rms_fwd(x, wts["norm_mix"], "rms_mix_fwd")
    z = _mm(h, w_z, "nt", BF16, "proj_z")
    xbc = _mm(h, w_xbc, "nt", F32, "proj_xbc")
    dt_raw = _mm(h, w_dt, "nt", F32, "proj_dt")
    qkv = _mm(h, w_qkv, "nt", BF16, "proj_qkv")
    gate_logits = _mm(h, w_gate, "nt", BF16, "proj_gate")

    xc = _conv_fwd(xbc, conv_w, conv_b, seq)
    dtr = dt_raw[:, :N_HEADS].reshape(t, N_GROUPS, hg).transpose(1, 0, 2)
    dtrt = dt_raw[:, :N_HEADS].reshape(nb, seq, N_GROUPS, hg).transpose(2, 0, 3, 1)
    y, states, *gathered = _ssd_fwd(xc, dtr, dtrt, bias_r, bias_c, a_r, a_c, dskip_r, nb, seq, overlap.gather_side)
    wts = {**wts, **overlap.late_weights(gathered)}
    yn = _gate_norm_fwd(y, z, wts["ssm_norm"])
    y_ssm = _mm(yn, wts["w_ssm_out"], "nn", BF16, "ssm_out")

    groups = range(len(ATT_DILATIONS))
    qg, kg, vg = _rope_fwd(qkv, cos, sin, nb, seq)
    o_g, lse_g = zip(*[_att_fwd(qg[i], kg[i], vg[i], i, seq) for i in groups])
    att = _merge_fwd(o_g, lse_g, nb, seq)
    y_att = _mm(att, wts["w_att_out_t"], "nt", BF16, "att_out")

    mixed = _mix_fwd(gate_logits, wts["b_gate"], y_ssm, y_att)
    x1 = _mm(mixed, wts["w_mix_out"], "nn", F32, "mix_out", add=x)
    h2 = _rms_fwd(x1, wts["norm_ffn"], "rms_ffn_fwd")
    gt = _mm(h2, wts["w_ffn_gate_t"], "nt", BF16, "ffn_gate")
    up = _mm(h2, wts["w_ffn_up_t"], "nt", BF16, "ffn_up")
    act = _swiglu_fwd(gt, up)
    x2 = _mm(act, wts["w_ffn_down"], "nn", F32, "ffn_down", add=x1)

    g = {}
    dx2, dx2_b, g["norm_final"], loss = _final_fwd_bwd(x2, target, wts["norm_final"].reshape(1, d))
    dact = _mm(dx2_b, wts["w_ffn_down"], "nt", BF16, "d_act")
    g["w_ffn_down"] = _mm(act, dx2_b, "tn", BF16, "g_ffn_down")
    dgt, dup = _swiglu_bwd(gt, up, dact)
    g["w_ffn_gate_t"] = _mm(dgt, h2, "tn", BF16, "g_ffn_gate")
    g["w_ffn_up_t"] = _mm(dup, h2, "tn", BF16, "g_ffn_up")
    dh2 = _mm(dgt, wts["w_ffn_gate_t"], "nn", F32, "d_h2_gate")
    dh2 = _mm(dup, wts["w_ffn_up_t"], "nn", F32, "d_h2_up", add=dh2)
    dx1, dx1_b, g["norm_ffn"] = _rms_bwd(x1, dh2, wts["norm_ffn"], dx2, "rms_ffn_bwd")

    dmixed = _mm(dx1_b, wts["w_mix_out"], "nt", F32, "d_mixed")
    g["w_mix_out"] = _mm(mixed, dx1_b, "tn", BF16, "g_mix_out")
    dy_ssm, dy_att, dgate, g["b_gate"] = _mix_bwd(gate_logits, wts["b_gate"], y_ssm, y_att, dmixed)

    datt = _mm(dy_att, wts["w_att_out_t"], "nn", F32, "d_att")
    g["w_att_out_t"] = _mm(dy_att, att, "tn", BF16, "g_att_out")
    do_g, dlt_g = _merge_bwd(o_g, lse_g, datt, nb, seq)
    dq_g, dk_g, dv_g = zip(*[_att_bwd(qg[i], kg[i], vg[i], do_g[i], lse_g[i], dlt_g[i], i, seq) for i in groups])
    dqkv = _rope_bwd(dq_g, dk_g, dv_g, cos, sin, nb, seq)

    dyn = _mm(dy_ssm, wts["w_ssm_out"], "nt", BF16, "d_yn")
    g["w_ssm_out"] = _mm(yn, dy_ssm, "tn", BF16, "g_ssm_out")
    dy, dz, g["ssm_norm"] = _gate_norm_bwd(y, z, wts["ssm_norm"], dyn)
    side = overlap.scatter_side({n: g.pop(n) for n in LATE})
    dxc, ddtr, g_bias, g_alog, g_dskip, *scattered = _ssd_bwd(xc, dtr, dtrt, bias_r, bias_c, a_r, a_c, dskip_r,
                                                               states, dy, nb, seq, side)
    g["dt_bias"] = g_bias.reshape(1, N_HEADS)
    g["a_log"] = g_alog.reshape(1, N_HEADS)
    g["d_skip"] = g_dskip.reshape(1, N_HEADS)
    dpre, g_conv_w, g_conv_b = _conv_bwd_pre(xbc, conv_w, conv_b, dxc, seq)
    g["conv_w"] = _group_major_inv(g_conv_w, 1)
    g["conv_b"] = _group_major_inv(g_conv_b, 1)
    dxbc = _conv_bwd_in(dpre, conv_w, seq)
    ddt = jnp.pad(ddtr.transpose(1, 0, 2).reshape(t, N_HEADS), ((0, 0), (0, DT_PAD - N_HEADS))).astype(BF16)

    g_in_t = jnp.concatenate([
        _mm(dz, h, "tn", BF16, "g_in_z"),
        _group_major_inv(_mm(dxbc, h, "tn", BF16, "g_in_xbc"), 0),
        _mm(ddt, h, "tn", BF16, "g_in_dt")[:N_HEADS],
        _mm(dqkv, h, "tn", BF16, "g_in_qkv"),
        _mm(dgate, h, "tn", BF16, "g_in_gate")], axis=0)
    dh = _mm(dz, w_z, "nn", F32, "d_h_z")
    dh = _mm(dxbc, w_xbc, "nn", F32, "d_h_xbc", add=dh)
    dh = _mm(ddt, w_dt, "nn", F32, "d_h_dt", add=dh)
    dh = _mm(dgate, w_gate, "nn", F32, "d_h_gate", add=dh)
    dh, *scattered_in = _mm(dqkv, w_qkv, "nn", F32, "d_h_qkv", add=dh, side=overlap.scatter_in({"w_in_t": g_in_t}))
    dx, _, g["norm_mix"] = _rms_bwd(x, dh, wts["norm_mix"], dx1, "rms_mix_bwd")
    return loss[0, 0], dx.reshape(nb, seq, d), g, scattered, scattered_in


def kernel(x, norm_mix, w_in, b_gate, conv_w, conv_b, dt_bias, a_log, d_skip, ssm_norm, w_ssm_out, w_att_out, w_mix_out, norm_ffn, w_ffn_gate, w_ffn_up, w_ffn_down, norm_final, loss_target, m_norm_mix, m_w_in, m_b_gate, m_conv_w, m_conv_b, m_dt_bias, m_a_log, m_d_skip, m_ssm_norm, m_w_ssm_out, m_w_att_out, m_w_mix_out, m_norm_ffn, m_w_ffn_gate, m_w_ffn_up, m_w_ffn_down, m_norm_final, v_norm_mix, v_w_in, v_b_gate, v_conv_w, v_conv_b, v_dt_bias, v_a_log, v_d_skip, v_ssm_norm, v_w_ssm_out, v_w_att_out, v_w_mix_out, v_norm_ffn, v_w_ffn_gate, v_w_ffn_up, v_w_ffn_down, v_norm_final):
    names = ("norm_mix", "w_in", "b_gate", "conv_w", "conv_b", "dt_bias", "a_log", "d_skip", "ssm_norm", "w_ssm_out",
             "w_att_out", "w_mix_out", "norm_ffn", "w_ffn_gate", "w_ffn_up", "w_ffn_down", "norm_final")
    w_loc = dict(zip(names, (norm_mix, w_in, b_gate, conv_w, conv_b, dt_bias, a_log, d_skip, ssm_norm, w_ssm_out,
                             w_att_out, w_mix_out, norm_ffn, w_ffn_gate, w_ffn_up, w_ffn_down, norm_final)))
    m_loc = dict(zip(names, (m_norm_mix, m_w_in, m_b_gate, m_conv_w, m_conv_b, m_dt_bias, m_a_log, m_d_skip,
                             m_ssm_norm, m_w_ssm_out, m_w_att_out, m_w_mix_out, m_norm_ffn, m_w_ffn_gate,
                             m_w_ffn_up, m_w_ffn_down, m_norm_final)))
    v_loc = dict(zip(names, (v_norm_mix, v_w_in, v_b_gate, v_conv_w, v_conv_b, v_dt_bias, v_a_log, v_d_skip,
                             v_ssm_norm, v_w_ssm_out, v_w_att_out, v_w_mix_out, v_norm_ffn, v_w_ffn_gate,
                             v_w_ffn_up, v_w_ffn_down, v_norm_final)))
    two_d = lambda a: a.reshape(a.shape[-2:]) if a.ndim >= 2 else a.reshape(1, -1)
    w2 = {n: two_d(a) for n, a in w_loc.items()}
    chip = 2 * lax.axis_index("x") + lax.axis_index("y")
    c = lax.axis_index("c")

    wire_shapes = {n: _wire_shard(w2[n], n).shape for n in SHARDED}
    true_rows = {n: wire_shapes[n][0] * wire_shapes[n][1] // PACK_W for n in SHARDED}
    seg_rows = {n: _rows(wire_shapes[n][0] * wire_shapes[n][1]) for n in SHARDED}
    buckets = {"first": ("w_in",), "late": tuple(n for n in SHARDED if n != "w_in")}
    rows_of = {b: _padded_rows(sum(seg_rows[n] for n in ns)) for b, ns in buckets.items()}

    def pack_shards(b):
        packed = _pack_rows([_wire_shard(w2[n], n).astype(BF16) for n in buckets[b]], rows_of[b])
        return packed.reshape(2, rows_of[b] // 2, PACK_W)

    def unpack_full(gathered, b):
        wg, out, off = gathered.reshape(N_CHIPS, rows_of[b], PACK_W), {}, 0
        for n in buckets[b]:
            rows, cols = wire_shapes[n]
            out[_wire_name(n)] = wg[:, off:off + true_rows[n]].reshape(N_CHIPS * rows, cols)
            off += seg_rows[n]
        return out

    def pack_grads(g, b):
        sections = [_pack_rows([g[_wire_name(n)].reshape(N_CHIPS, true_rows[n], PACK_W)[k] for n in buckets[b]],
                               rows_of[b]) for k in range(N_CHIPS)]
        return jnp.stack(sections).reshape(N_CHIPS, 2, rows_of[b] // 2, PACK_W).transpose(1, 0, 2, 3)

    def chip_sums(g, b):
        g2 = pack_grads(g, b)
        return _add_own_half(g2, _swap_halves(g2, b), c, b)

    def finish(by_source, b):
        reduced = _join_halves(_sum_chips(by_source, b), b).reshape(rows_of[b], PACK_W)
        out, off = {}, 0
        for n in buckets[b]:
            out[n] = reduced[off:off + true_rows[n]].reshape(wire_shapes[n])
            off += seg_rows[n]
        return out

    full = {"w_in_t": _gather_weights(pack_shards("first")).reshape(N_CHIPS, rows_of["first"], PACK_W)}
    for n in SMALL:
        full[n] = w2[n]
    overlap = _Overlap(_gather_side(pack_shards("late")), lambda outs: unpack_full(outs[0], "late"),
                       lambda g: _scatter_side(chip_sums(g, "late")), lambda g: _scatter_side(chip_sums(g, "first")))

    n_conv = w2["conv_w"].shape[1]
    placed = lax.dynamic_update_slice_in_dim(jnp.zeros((CONV_K, N_CHIPS * n_conv), F32), w2["conv_w"], chip * n_conv, 1)
    placed = jnp.where(c == 0, placed, 0.0)
    full["conv_w"] = _allreduce_small(_pack_rows([placed], _rows(int(placed.size))), "gather_conv_w").reshape(
        -1)[:placed.size].reshape(placed.shape)

    loss_sum, grad_x, g_full, scattered, scattered_in = _local_step(x, loss_target, full, overlap)
    loss = lax.psum(loss_sum, ("x", "y", "c"))

    g_shard = {}
    small_names = SMALL + ("conv_w",)
    small_flat = jnp.concatenate([g_full[n].reshape(-1) for n in small_names])
    small = _allreduce_small(_pack_rows([small_flat], _rows(int(small_flat.size))), "allreduce_small").reshape(-1)
    off = 0
    for n in small_names:
        size = int(g_full[n].size)
        g_shard[n] = small[off:off + size].reshape(g_full[n].shape)
        off += size
    g_shard["conv_w"] = lax.dynamic_slice_in_dim(g_shard["conv_w"], chip * n_conv, n_conv, 1)

    g_shard.update(finish(scattered[0], "late"))
    g_shard.update(finish(scattered_in[0], "first"))

    grads, deltas, new_m, new_v = [], [], [], []
    for n in names:
        shape = w_loc[n].shape
        if n in COL_SHARDED:
            view = unview = lambda a: jnp.swapaxes(a, -1, -2)
        else:
            view, unview = ((lambda a: a) if len(shape) >= 2 else two_d), (lambda a: a.reshape(shape))
        gn = g_shard[n].reshape(view(w_loc[n]).shape)
        outs = _adamw(view(w_loc[n]), gn, view(m_loc[n]), view(v_loc[n]), "adamw_" + n)
        for acc, a in zip((grads, deltas, new_m, new_v), (gn, *outs)):
            acc.append(unview(a))
    return (loss, grad_x, *grads, *deltas, *new_m, *new_v)
```

```python
import functools
from typing import Callable, NamedTuple, Optional

import jax
import jax.numpy as jnp
from jax import lax
from jax.experimental import pallas as pl
from jax.experimental.pallas import tpu as pltpu

F32 = jnp.float32
BF16 = jnp.bfloat16
SDS = jax.ShapeDtypeStruct
MESH = pl.DeviceIdType.MESH

D_MODEL = 1024
D_INNER = 2048
N_HEADS = 32
HEAD_P = 64
N_GROUPS = 4
HEADS_PER_GROUP = N_HEADS // N_GROUPS
D_STATE = 128
CONV_K = 4
CHUNK = 128
CONV_DIM = D_INNER + 2 * N_GROUPS * D_STATE
GROUP_W = D_INNER // N_GROUPS + 2 * D_STATE
ATT_HEADS = 12
ATT_D = 128
ATT_SLOTS = 4
ATT_W = ATT_SLOTS * ATT_D
ATT_DILATIONS = (1, 4, 16)
ATT_BLOCK = 128
QKV_DIM = 3 * ATT_HEADS * ATT_D
D_FF = 2816
DT_PAD = 128
ROPE_THETA = 10000.0
EPS = 1e-6
N_CHIPS = 4
LANES = 128

ADAM_LR = 0.001
ADAM_B1 = 0.9
ADAM_B2 = 0.999
ADAM_EPS = 1e-08
ADAM_WD = 0.01
ADAM_STEP = 10

VMEM_LIMIT = 48 * 1024 * 1024


def _cparams(semantics):
    return pltpu.CompilerParams(dimension_semantics=semantics, vmem_limit_bytes=VMEM_LIMIT)


def _pick(n, cap):
    best = None
    for t in range(LANES, min(n, cap) + 1, LANES):
        if n % t == 0:
            best = t
    return best or n


def _row_tile(rows, cap):
    best = None
    for t in range(8, min(rows, cap) + 1, 8):
        if rows % t == 0:
            best = t
    return best or rows


def _sigmoid(x):
    return 1.0 / (1.0 + jnp.exp(-x))


def _softplus(x):
    return jnp.maximum(x, 0.0) + jnp.log(1.0 + jnp.exp(-jnp.abs(x)))


def _dot(a, b):
    return jnp.dot(a, b, preferred_element_type=F32)


def _dot_nt(a, b):
    return lax.dot_general(a, b, (((1,), (1,)), ((), ())), preferred_element_type=F32)


def _dot_tn(a, b):
    return lax.dot_general(a, b, (((0,), (0,)), ((), ())), preferred_element_type=F32)


def _mm(a, b, mode, out_dtype, name, add=None, side=None):
    if mode == "nn":
        (m, k), (_, n) = a.shape, b.shape
    elif mode == "nt":
        (m, k), (n, _) = a.shape, b.shape
    else:
        (k, m), (_, n) = a.shape, b.shape
    tm, tn = _pick(m, 1536), _pick(n, 2048)
    tk = k if k <= 2048 else _pick(k, 2048)
    nk = k // tk
    dims = {"nn": ((1,), (0,)), "nt": ((1,), (1,)), "tn": ((0,), (0,))}[mode]

    def partial_product(a_ref, b_ref):
        return lax.dot_general(a_ref[...].astype(BF16), b_ref[...].astype(BF16), (dims, ((), ())),
                               preferred_element_type=F32)

    def body(*refs):
        a_ref, b_ref = refs[:2]
        c_ref = refs[2] if add is not None else None
        o_ref = refs[3] if add is not None else refs[2]

        def finish(r):
            if add is not None:
                r = r + c_ref[...].astype(F32)
            o_ref[...] = r.astype(out_dtype)

        if nk == 1:
            finish(partial_product(a_ref, b_ref))
            return
        acc = refs[-1]
        kk = pl.program_id(2)

        @pl.when(kk == 0)
        def _():
            acc[...] = partial_product(a_ref, b_ref)

        @pl.when((kk > 0) & (kk < nk - 1))
        def _():
            acc[...] += partial_product(a_ref, b_ref)

        @pl.when(kk == nk - 1)
        def _():
            finish(acc[...] + partial_product(a_ref, b_ref))

    a_spec = {"nn": pl.BlockSpec((tm, tk), lambda j, i, q: (i, q)),
              "nt": pl.BlockSpec((tm, tk), lambda j, i, q: (i, q)),
              "tn": pl.BlockSpec((tk, tm), lambda j, i, q: (q, i))}[mode]
    b_spec = {"nn": pl.BlockSpec((tk, tn), lambda j, i, q: (q, j)),
              "nt": pl.BlockSpec((tn, tk), lambda j, i, q: (j, q)),
              "tn": pl.BlockSpec((tk, tn), lambda j, i, q: (q, j))}[mode]
    o_spec = pl.BlockSpec((tm, tn), lambda j, i, q: (i, j))
    ins, specs = [a, b], [a_spec, b_spec]
    if add is not None:
        ins.append(add)
        specs.append(o_spec)
    acc = [pltpu.VMEM((tm, tn), F32)] if nk > 1 else []
    grid = (n // tn, m // tm, nk)
    if side is None:
        return pl.pallas_call(
            body, name=name, grid=grid, in_specs=specs, out_specs=o_spec, out_shape=SDS((m, n), out_dtype),
            scratch_shapes=acc, compiler_params=_cparams(("parallel", "parallel", "arbitrary")))(*ins)
    return pl.pallas_call(
        _attach_side(body, len(ins), 1, side, grid), name=name, grid=grid,
        in_specs=specs + [ANY] * len(side.ins), out_specs=[o_spec] + [ANY] * len(side.out_shapes),
        out_shape=[SDS((m, n), out_dtype)] + list(side.out_shapes), scratch_shapes=acc + list(side.scratch),
        compiler_params=_cparams(("arbitrary", "arbitrary", "arbitrary")))(*ins, *side.ins)


def _rw(name, fn, nsteps, ins, outs, n_acc=0):
    n_in, n_out = len(ins), len(outs)

    def body(*refs):
        i = pl.program_id(0)
        vals = fn(i, *refs[:n_in])
        for q, (r, v) in enumerate(zip(refs[n_in:], vals)):
            if q < n_out - n_acc:
                r[...] = v.astype(r.dtype)
            else:
                @pl.when(i == 0)
                def _(r=r):
                    r[...] = jnp.zeros_like(r)

                r[...] += v

    return pl.pallas_call(
        body, name=name, grid=(nsteps,), in_specs=[s for _, s in ins], out_specs=[s for _, s in outs],
        out_shape=[o for o, _ in outs], compiler_params=_cparams(("arbitrary",)))(*[a for a, _ in ins])


def _rs(tm, w, cb=0):
    return pl.BlockSpec((tm, w), lambda i: (i, cb))


def _fs(shape):
    nd = len(shape)
    return pl.BlockSpec(shape, lambda i: (0,) * nd)


def _colsum(v):
    return jnp.sum(v, axis=0, keepdims=True)


def _rms_fwd(x, g, name):
    t, d = x.shape
    tm = 512

    def fn(i, x_ref, g_ref):
        xv = x_ref[...]
        r = lax.rsqrt(jnp.mean(xv * xv, axis=-1, keepdims=True) + EPS)
        return [xv * r * g_ref[...]]

    return _rw(name, fn, t // tm, [(x, _rs(tm, d)), (g, _fs((1, d)))], [(SDS((t, d), BF16), _rs(tm, d))])[0]


def _rms_bwd(x, dh, g, dres, name):
    t, d = x.shape
    tm = 512

    def fn(i, x_ref, dh_ref, g_ref, dres_ref):
        xv = x_ref[...]
        r = lax.rsqrt(jnp.mean(xv * xv, axis=-1, keepdims=True) + EPS)
        xhat = xv * r
        dhv = dh_ref[...]
        dxhat = dhv * g_ref[...]
        dx = dres_ref[...] + r * (dxhat - xhat * jnp.mean(dxhat * xhat, axis=-1, keepdims=True))
        return [dx, dx, _colsum(dhv * xhat)]

    return _rw(name, fn, t // tm,
               [(x, _rs(tm, d)), (dh, _rs(tm, d)), (g, _fs((1, d))), (dres, _rs(tm, d))],
               [(SDS((t, d), F32), _rs(tm, d)), (SDS((t, d), BF16), _rs(tm, d)), (SDS((1, d), F32), _fs((1, d)))],
               n_acc=1)


def _final_fwd_bwd(x2, target, g):
    t, d = x2.shape
    tm = 512

    def fn(i, x_ref, t_ref, g_ref):
        xv = x_ref[...]
        gv = g_ref[...]
        r = lax.rsqrt(jnp.mean(xv * xv, axis=-1, keepdims=True) + EPS)
        xhat = xv * r
        diff = xhat * gv - t_ref[...]
        lsum = 0.5 * jnp.sum(jnp.sum(diff * diff, axis=-1, keepdims=True) * (1.0 / d), axis=0, keepdims=True)
        dy = diff * (1.0 / d)
        dxhat = dy * gv
        dx = r * (dxhat - xhat * jnp.mean(dxhat * xhat, axis=-1, keepdims=True))
        return [dx, dx, _colsum(dy * xhat), lsum]

    return _rw("final_norm_loss", fn, t // tm,
               [(x2, _rs(tm, d)), (target, _rs(tm, d)), (g, _fs((1, d)))],
               [(SDS((t, d), F32), _rs(tm, d)), (SDS((t, d), BF16), _rs(tm, d)), (SDS((1, d), F32), _fs((1, d))),
                (SDS((1, 1), F32), _fs((1, 1)))], n_acc=2)


CONV_TS = 512
CONV_HALO = 8


def _conv_specs(seq, c):
    ts, tc = CONV_TS, GROUP_W
    hb = ts // CONV_HALO
    u_spec = pl.BlockSpec((ts, tc), lambda j, i: (i, j))
    prev_spec = pl.BlockSpec((CONV_HALO, tc), lambda j, i: (jnp.maximum(i * hb - 1, 0), j))
    w_spec = pl.BlockSpec((CONV_K, tc), lambda j, i: (0, j))
    b_spec = pl.BlockSpec((1, tc), lambda j, i: (0, j))
    return u_spec, prev_spec, w_spec, b_spec


CONV_ROWS = 16


def _conv_pre(i, seq, u_ref, prev_ref, w_ref, b_ref, ext):
    ts = CONV_TS
    first = (i % (seq // ts)) == 0
    ext[0:CONV_HALO, :] = jnp.where(first, 0.0, prev_ref[...])
    ext[CONV_HALO:, :] = u_ref[...]
    acc = jnp.broadcast_to(b_ref[...], u_ref.shape)
    for q in range(CONV_K):
        acc = acc + w_ref[q:q + 1, :] * ext[pl.ds(CONV_HALO - CONV_K + 1 + q, ts), :]
    return acc


def _conv_fwd(u, w, b, seq):
    t, c = u.shape
    ts, tc = CONV_TS, GROUP_W
    u_spec, prev_spec, w_spec, b_spec = _conv_specs(seq, c)

    def body(u_ref, prev_ref, w_ref, b_ref, o_ref, ext):
        pre = _conv_pre(pl.program_id(1), seq, u_ref, prev_ref, w_ref, b_ref, ext)
        o_ref[...] = pre * _sigmoid(pre)

    return pl.pallas_call(
        body, name="conv_fwd", grid=(c // tc, t // ts), in_specs=[u_spec, prev_spec, w_spec, b_spec],
        out_specs=u_spec, out_shape=SDS((t, c), F32), scratch_shapes=[pltpu.VMEM((ts + CONV_HALO, tc), F32)],
        compiler_params=_cparams(("parallel", "arbitrary")))(u, u, w, b)


def _conv_bwd_pre(u, w, b, dxc, seq):
    t, c = u.shape
    ts, tc = CONV_TS, GROUP_W
    u_spec, prev_spec, w_spec, b_spec = _conv_specs(seq, c)

    def body(u_ref, prev_ref, w_ref, b_ref, d_ref, dpre_ref, dw_ref, db_ref, ext):
        i = pl.program_id(1)
        pre = _conv_pre(i, seq, u_ref, prev_ref, w_ref, b_ref, ext)
        sg = _sigmoid(pre)
        dpre = d_ref[...] * sg * (1.0 + pre * (1.0 - sg))
        dpre_ref[...] = dpre

        @pl.when(i == 0)
        def _():
            dw_ref[...] = jnp.zeros_like(dw_ref)
            db_ref[...] = jnp.zeros_like(db_ref)

        db_ref[...] += _colsum(dpre)
        for q in range(CONV_K):
            dw_ref[q:q + 1, :] += _colsum(dpre * ext[pl.ds(CONV_HALO - CONV_K + 1 + q, ts), :])

    return pl.pallas_call(
        body, name="conv_bwd_pre", grid=(c // tc, t // ts),
        in_specs=[u_spec, prev_spec, w_spec, b_spec, u_spec], out_specs=[u_spec, w_spec, b_spec],
        out_shape=[SDS((t, c), F32), SDS((CONV_K, c), F32), SDS((1, c), F32)],
        scratch_shapes=[pltpu.VMEM((ts + CONV_HALO, tc), F32)],
        compiler_params=_cparams(("parallel", "arbitrary")))(u, u, w, b, dxc)


def _conv_bwd_in(dpre, w, seq):
    t, c = dpre.shape
    ts, tc = CONV_TS, GROUP_W
    hb = ts // CONV_HALO
    last = t // CONV_HALO - 1
    d_spec = pl.BlockSpec((ts, tc), lambda j, i: (i, j))
    next_spec = pl.BlockSpec((CONV_HALO, tc), lambda j, i: (jnp.minimum((i + 1) * hb, last), j))
    w_spec = pl.BlockSpec((CONV_K, tc), lambda j, i: (0, j))

    def body(d_ref, next_ref, w_ref, o_ref, ext):
        i = pl.program_id(1)
        nts = seq // ts
        is_last = (i % nts) == nts - 1
        ext[0:ts, :] = d_ref[...]
        ext[ts:, :] = jnp.where(is_last, 0.0, next_ref[...])
        wv = w_ref[...]

        def rows(j, carry):
            r0 = pl.multiple_of(j * CONV_ROWS, CONV_ROWS)
            blk = ext[pl.ds(r0, CONV_ROWS + CONV_HALO), :]
            acc = wv[CONV_K - 1:CONV_K] * blk[0:CONV_ROWS]
            for q in range(CONV_K - 1):
                acc = acc + wv[q:q + 1] * blk[CONV_K - 1 - q:CONV_K - 1 - q + CONV_ROWS]
            o_ref[pl.ds(r0, CONV_ROWS), :] = acc.astype(o_ref.dtype)
            return carry

        lax.fori_loop(0, ts // CONV_ROWS, rows, 0)

    return pl.pallas_call(
        body, name="conv_bwd_in", grid=(c // tc, t // ts), in_specs=[d_spec, next_spec, w_spec],
        out_specs=d_spec, out_shape=SDS((t, c), BF16), scratch_shapes=[pltpu.VMEM((ts + CONV_HALO, tc), F32)],
        compiler_params=_cparams(("parallel", "arbitrary")))(dpre, dpre, w)


def _split3(v):
    hi = v.astype(BF16)
    r1 = v - hi.astype(F32)
    mid = r1.astype(BF16)
    lo = (r1 - mid.astype(F32)).astype(BF16)
    return hi, mid, lo


def _ssd_prelude(dtr_ref, dtrt_ref, bias_ref, biast_ref, a_ref, at_ref):
    dt = _softplus(dtr_ref[...] + bias_ref[...])
    dtt = _softplus(dtrt_ref[...] + biast_ref[...])
    ri = lax.broadcasted_iota(jnp.int32, (CHUNK, CHUNK), 0)
    ci = lax.broadcasted_iota(jnp.int32, (CHUNK, CHUNK), 1)
    lower = ri >= ci
    upper = ri <= ci
    lower_b = jnp.where(lower, 1.0, 0.0).astype(BF16)
    upper_b = jnp.where(upper, 1.0, 0.0).astype(BF16)
    acs = sum(_dot(lower_b, p) for p in _split3(dt * a_ref[...]))
    acst = sum(_dot(p, upper_b) for p in _split3(dtt * at_ref[...]))
    return dt, acs, acst, lower, upper, lower_b, upper_b


SSD_FWD_GPS = 2
SSD_BWD_GPS = 1


def _ssd_specs(seq, gps):
    nc = seq // CHUNK
    hg = HEADS_PER_GROUP
    fwd = lambda c: c
    rev = lambda c: nc - 1 - c

    def specs(cc):
        return dict(
            xc=pl.BlockSpec((CHUNK, gps * GROUP_W), lambda g, b, c: (b * nc + cc(c), g)),
            y=pl.BlockSpec((CHUNK, gps * hg * HEAD_P), lambda g, b, c: (b * nc + cc(c), g)),
            dtr=pl.BlockSpec((gps, CHUNK, hg), lambda g, b, c: (g, b * nc + cc(c), 0)),
            dtrt=pl.BlockSpec((gps, None, hg, CHUNK), lambda g, b, c: (g, b, 0, cc(c))),
            prow=pl.BlockSpec((gps, 1, hg), lambda g, b, c: (g, 0, 0)),
            pcol=pl.BlockSpec((gps, hg, 1), lambda g, b, c: (g, 0, 0)),
            st=pl.BlockSpec((gps, None, None, D_STATE, hg * HEAD_P), lambda g, b, c: (g, b, cc(c), 0, 0)),
        )

    return specs(fwd), specs(rev)


def _group_views(refs, lane_widths, gi):
    return [r.at[:, gi * w:(gi + 1) * w] if w else r.at[gi] for r, w in zip(refs, lane_widths)]


def _head_maps():
    hw = HEADS_PER_GROUP * HEAD_P
    shift = HEAD_P.bit_length() - 1
    hj = lax.broadcasted_iota(jnp.int32, (HEADS_PER_GROUP, hw), 0)
    lq = jnp.right_shift(lax.broadcasted_iota(jnp.int32, (HEADS_PER_GROUP, hw), 1), shift)
    spread = jnp.where(hj == lq, 1.0, 0.0).astype(BF16)
    rq = jnp.right_shift(lax.broadcasted_iota(jnp.int32, (hw, LANES), 0), shift)
    cj = lax.broadcasted_iota(jnp.int32, (hw, LANES), 1)
    gather = jnp.where(rq == cj, 1.0, 0.0).astype(BF16)
    return spread, gather


def _dot01(v, m01):
    hi, mid, _ = _split3(v)
    return _dot(hi, m01) + _dot(mid, m01)


class _Side(NamedTuple):
    ins: tuple
    out_shapes: tuple
    scratch: tuple
    first: Callable
    mid: Optional[Callable]
    last: Callable


NO_SIDE = _Side((), (), (), lambda *refs: None, None, lambda *refs: None)


def _attach_side(body, n_in, n_out, side, grid):
    si, so, ss = len(side.ins), len(side.out_shapes), len(side.scratch)

    def wrapped(*refs):
        ins, s_in = refs[:n_in], refs[n_in:n_in + si]
        outs = refs[n_in + si:n_in + si + n_out]
        s_out = refs[n_in + si + n_out:n_in + si + n_out + so]
        rest = refs[n_in + si + n_out + so:]
        scr, s_scr = rest[:len(rest) - ss], rest[len(rest) - ss:]
        ids = [pl.program_id(a) for a in range(len(grid))]
        inner_first = functools.reduce(lambda p, q: p & q, [i == 0 for i in ids[1:]], ids[0] >= 0)
        at_last = functools.reduce(lambda p, q: p & q, [i == n - 1 for i, n in zip(ids, grid)])

        @pl.when((ids[0] == 0) & inner_first)
        def _():
            side.first(s_in, s_out, s_scr)

        if side.mid is not None:
            outer_last = functools.reduce(lambda p, q: p & q, [i == n - 1 for i, n in zip(ids[:-1], grid[:-1])])

            @pl.when(outer_last & (ids[-1] == 0))
            def _():
                side.mid(s_in, s_out, s_scr)

        body(*ins, *outs, *scr)

        @pl.when(at_last)
        def _():
            side.last(s_in, s_out, s_scr)

    return wrapped


def _ssd_fwd(xc, dtr, dtrt, bias, biast, a, at, dskip, nb, seq, side):
    t = xc.shape[0]
    nc = seq // CHUNK
    hg = HEADS_PER_GROUP
    hw = hg * HEAD_P
    gps = SSD_FWD_GPS
    grid = (N_GROUPS // gps, nb, nc)
    sp, _ = _ssd_specs(seq, gps)

    def body(*refs):
        for gi in range(gps):
            one_group(*_group_views(refs, (GROUP_W, 0, 0, 0, 0, 0, 0, 0, hw, 0, 0), gi))

    def one_group(xc_ref, dtr_ref, dtrt_ref, bias_ref, biast_ref, a_ref, at_ref, d_ref, y_ref, sin_ref, st):
        @pl.when(pl.program_id(2) == 0)
        def _():
            st[...] = jnp.zeros_like(st)

        s_in = st[...]
        sin_ref[...] = s_in
        dt, acs, acst, lower, _, _, _ = _ssd_prelude(dtr_ref, dtrt_ref, bias_ref, biast_ref, a_ref, at_ref)
        spread, _ = _head_maps()
        x = xc_ref[...]
        xs = x[:, :hw]
        b16 = x[:, hw:hw + D_STATE].astype(BF16)
        c16 = x[:, hw + D_STATE:].astype(BF16)
        cb = _dot_nt(c16, b16)
        last = acs[CHUNK - 1:CHUNK, :]
        e_x = _dot01(jnp.exp(acs), spread)
        dec_x = _dot01(jnp.exp(last - acs), spread)
        tot_x = e_x[CHUNK - 1:CHUNK, :]
        d_x = _dot01(jnp.broadcast_to(d_ref[...], (8, hg)), spread)[0:1, :]
        xdtf = xs * _dot01(dt, spread)
        xdt16 = xdtf.astype(BF16)
        yoff = e_x * _dot(c16, s_in.astype(BF16))
        st[...] = tot_x * s_in + _dot_tn(b16, (dec_x * xdtf).astype(BF16))
        parts = []
        for j in range(hg):
            decay = jnp.exp(jnp.where(lower, acs[:, j:j + 1] - acst[j:j + 1, :], -jnp.inf))
            parts.append(_dot((cb * decay).astype(BF16), xdt16[:, HEAD_P * j:HEAD_P * (j + 1)]))
        y_ref[...] = jnp.concatenate(parts, axis=-1) + yoff + d_x * xs

    return pl.pallas_call(
        _attach_side(body, 8, 2, side, grid), name="ssd_fwd", grid=grid,
        in_specs=[sp["xc"], sp["dtr"], sp["dtrt"], sp["prow"], sp["pcol"], sp["prow"], sp["pcol"], sp["prow"]]
        + [ANY] * len(side.ins),
        out_specs=[sp["y"], sp["st"]] + [ANY] * len(side.out_shapes),
        out_shape=[SDS((t, D_INNER), F32), SDS((N_GROUPS, nb, nc, D_STATE, hw), F32)] + list(side.out_shapes),
        scratch_shapes=[pltpu.VMEM((gps, D_STATE, hw), F32)] + list(side.scratch),
        compiler_params=_cparams(("arbitrary", "arbitrary", "arbitrary")))(
            xc, dtr, dtrt, bias, biast, a, at, dskip, *side.ins)


def _ssd_bwd(xc, dtr, dtrt, bias, biast, a, at, dskip, states, dy, nb, seq, side):
    t = xc.shape[0]
    nc = seq // CHUNK
    hg = HEADS_PER_GROUP
    hw = hg * HEAD_P
    gps = SSD_BWD_GPS
    grid = (N_GROUPS // gps, nb, nc)
    _, sp = _ssd_specs(seq, gps)

    def body(*refs):
        for gi in range(gps):
            one_group(*_group_views(refs, (GROUP_W, 0, 0, 0, 0, 0, 0, 0, 0, hw, GROUP_W, 0, 0, 0, 0, 0), gi))

    def one_group(xc_ref, dtr_ref, dtrt_ref, bias_ref, biast_ref, a_ref, at_ref, d_ref, sin_ref, dy_ref,
                  dxc_ref, ddtr_ref, gbias_ref, ga_ref, gd_ref, ds):
        first = (pl.program_id(1) == 0) & (pl.program_id(2) == 0)

        @pl.when(pl.program_id(2) == 0)
        def _():
            ds[...] = jnp.zeros_like(ds)

        @pl.when(first)
        def _():
            gbias_ref[...] = jnp.zeros_like(gbias_ref)
            ga_ref[...] = jnp.zeros_like(ga_ref)
            gd_ref[...] = jnp.zeros_like(gd_ref)

        dt, acs, acst, lower, upper, _, upper_b = _ssd_prelude(dtr_ref, dtrt_ref, bias_ref, biast_ref, a_ref, at_ref)
        spread, gather = _head_maps()
        x = xc_ref[...]
        dy = dy_ref[...]
        xs = x[:, :hw]
        b16 = x[:, hw:hw + D_STATE].astype(BF16)
        c16 = x[:, hw + D_STATE:].astype(BF16)
        dy16 = dy.astype(BF16)
        cb = _dot_nt(c16, b16)
        cbt = _dot_nt(b16, c16)
        last = acs[CHUNK - 1:CHUNK, :]
        e8 = jnp.exp(acs)
        dec8 = jnp.exp(last - acs)
        e_x = _dot01(e8, spread)
        dec_x = _dot01(dec8, spread)
        tot_x = e_x[CHUNK - 1:CHUNK, :]
        dt_x = _dot01(dt, spread)
        d_x = _dot01(jnp.broadcast_to(d_ref[...], (8, hg)), spread)[0:1, :]
        xdtf = xs * dt_x
        xdt16 = xdtf.astype(BF16)
        s_in = sin_ref[...]
        s16 = s_in.astype(BF16)
        ds_out = ds[...]
        ds16 = ds_out.astype(BF16)
        bds = _dot(b16, ds16)
        cs = _dot(c16, s16)
        edy16 = (e_x * dy).astype(BF16)
        ds[...] = tot_x * ds_out + _dot_tn(c16, edy16)
        lane8 = lax.broadcasted_iota(jnp.int32, (CHUNK, hg), 1)
        row8 = lax.broadcasted_iota(jnp.int32, (CHUNK, hg), 0)
        dacs8 = jnp.zeros((CHUNK, hg), F32)
        acc_m = jnp.zeros((CHUNK, CHUNK), F32)
        acc_mt = jnp.zeros((CHUNK, CHUNK), F32)
        dx_parts = []
        for j in range(hg):
            sl = slice(HEAD_P * j, HEAD_P * (j + 1))
            col = acs[:, j:j + 1]
            row = acst[j:j + 1, :]
            decay = jnp.exp(jnp.where(lower, col - row, -jnp.inf))
            decayt = jnp.exp(jnp.where(upper, row - col, -jnp.inf))
            wm = _dot_nt(dy16[:, sl], xdt16[:, sl]) * decay
            wmt = _dot_nt(xdt16[:, sl], dy16[:, sl]) * decayt
            acc_m = acc_m + wm
            acc_mt = acc_mt + wmt
            dacs8 = dacs8 + jnp.where(lane8 == j, jnp.sum(wm * cb, axis=-1, keepdims=True)
                                      - jnp.sum(wmt * cbt, axis=-1, keepdims=True), 0.0)
            dx_parts.append(_dot((cbt * decayt).astype(BF16), dy16[:, sl]))
        dx = jnp.concatenate(dx_parts, axis=-1) + dec_x * bds
        dxc_ref[:, :hw] = dx * dt_x + d_x * dy
        dxc_ref[:, hw:hw + D_STATE] = _dot(acc_mt.astype(BF16), c16) + _dot_nt((dec_x * xdtf).astype(BF16), ds16)
        dxc_ref[:, hw + D_STATE:] = _dot(acc_m.astype(BF16), b16) + _dot_nt(edy16, s16)
        dtot_rows = jnp.broadcast_to(_colsum(ds_out * s_in), (8, hw))
        sums = _dot01(jnp.concatenate([dy * cs, xdtf * bds, dx * xs, dy * xs, dtot_rows], axis=0), gather)
        de8 = sums[0:CHUNK, :hg]
        ddec8 = sums[CHUNK:2 * CHUNK, :hg]
        ddtx8 = sums[2 * CHUNK:3 * CHUNK, :hg]
        gd8 = _colsum(sums[3 * CHUNK:4 * CHUNK, :hg])
        dtot8 = sums[4 * CHUNK:4 * CHUNK + 1, :hg]
        extra = _colsum(ddec8 * dec8) + dtot8 * e8[CHUNK - 1:CHUNK, :]
        dacs8 = dacs8 + de8 * e8 - ddec8 * dec8 + jnp.where(row8 == CHUNK - 1, extra, 0.0)
        da = sum(_dot(upper_b, p) for p in _split3(dacs8))
        av = a_ref[...]
        ddt = da * av + ddtx8
        ddtr = ddt * _sigmoid(dtr_ref[...] + bias_ref[...])
        ddtr_ref[...] = ddtr
        gbias_ref[...] += _colsum(ddtr)
        ga_ref[...] += _colsum(da * dt) * av
        gd_ref[...] += gd8

    return pl.pallas_call(
        _attach_side(body, 10, 5, side, grid), name="ssd_bwd", grid=grid,
        in_specs=[sp["xc"], sp["dtr"], sp["dtrt"], sp["prow"], sp["pcol"], sp["prow"], sp["pcol"], sp["prow"],
                  sp["st"], sp["y"]] + [ANY] * len(side.ins),
        out_specs=[sp["xc"], sp["dtr"], sp["prow"], sp["prow"], sp["prow"]] + [ANY] * len(side.out_shapes),
        out_shape=[SDS((t, N_GROUPS * GROUP_W), F32), SDS((N_GROUPS, t, hg), F32)]
        + [SDS((N_GROUPS, 1, hg), F32)] * 3 + list(side.out_shapes),
        scratch_shapes=[pltpu.VMEM((gps, D_STATE, hw), F32)] + list(side.scratch),
        compiler_params=_cparams(("arbitrary", "arbitrary", "arbitrary")))(
            xc, dtr, dtrt, bias, biast, a, at, dskip, states, dy, *side.ins)


def _group_bcast(v, width, fn):
    parts = []
    for q in range(v.shape[-1] // width):
        s = fn(v[:, q * width:(q + 1) * width])
        parts.append(jnp.broadcast_to(s, (v.shape[0], width)))
    return jnp.concatenate(parts, axis=-1)


def _gate_norm_fwd(y, z, g):
    t, d = y.shape
    tm = 256
    gw = d // N_GROUPS

    def fn(i, y_ref, z_ref, g_ref):
        zv = z_ref[...].astype(F32)
        u = y_ref[...] * (zv * _sigmoid(zv))
        r = lax.rsqrt(_group_bcast(u * u, gw, lambda p: jnp.mean(p, axis=-1, keepdims=True)) + EPS)
        return [u * r * g_ref[...]]

    return _rw("gate_norm_fwd", fn, t // tm, [(y, _rs(tm, d)), (z, _rs(tm, d)), (g, _fs((1, d)))],
               [(SDS((t, d), BF16), _rs(tm, d))])[0]


def _gate_norm_bwd(y, z, g, dyn):
    t, d = y.shape
    tm = 256
    gw = d // N_GROUPS

    def fn(i, y_ref, z_ref, g_ref, dyn_ref):
        zv = z_ref[...].astype(F32)
        yv = y_ref[...]
        sg = _sigmoid(zv)
        sz = zv * sg
        u = yv * sz
        r = lax.rsqrt(_group_bcast(u * u, gw, lambda p: jnp.mean(p, axis=-1, keepdims=True)) + EPS)
        uhat = u * r
        dv = dyn_ref[...].astype(F32)
        duhat = dv * g_ref[...]
        du = r * (duhat - uhat * _group_bcast(duhat * uhat, gw, lambda p: jnp.mean(p, axis=-1, keepdims=True)))
        dz = du * yv * sg * (1.0 + zv * (1.0 - sg))
        return [du * sz, dz, _colsum(dv * uhat)]

    return _rw("gate_norm_bwd", fn, t // tm,
               [(y, _rs(tm, d)), (z, _rs(tm, d)), (g, _fs((1, d))), (dyn, _rs(tm, d))],
               [(SDS((t, d), F32), _rs(tm, d)), (SDS((t, d), BF16), _rs(tm, d)), (SDS((1, d), F32), _fs((1, d)))],
               n_acc=1)


def _rope_tables(seq):
    half = ATT_D // 2
    inv = ROPE_THETA ** (-jnp.arange(half, dtype=F32) / half)
    ang = jnp.arange(seq, dtype=F32)[:, None] * inv[None, :]
    cos, sin = jnp.cos(ang), jnp.sin(ang)
    return jnp.concatenate([cos, cos], axis=-1), jnp.concatenate([-sin, sin], axis=-1)


ATT_TILE = 512
ATT_QB = 8


def _strided_spec(r, mtiles):
    return pl.BlockSpec((None, r, None, ATT_TILE // r, ATT_W), lambda i: (i // mtiles, 0, i % mtiles, 0, 0))


def _strided_shape(nb, r, mtiles, dtype):
    return SDS((nb, r, mtiles, ATT_TILE // r, ATT_W), dtype)


def _to_strided(val, out_ref, lanes, r, sc):
    if r == 1:
        out_ref[0, :, lanes] = val.astype(out_ref.dtype)
        return
    sc[...] = val
    for rr in range(r):
        out_ref[rr, :, lanes] = sc[pl.ds(rr, ATT_TILE // r, stride=r), :].astype(out_ref.dtype)


def _from_strided(in_ref, lanes, r, sc):
    if r == 1:
        return in_ref[0, :, lanes].astype(F32)
    for rr in range(r):
        sc[pl.ds(rr, ATT_TILE // r, stride=r), :] = in_ref[rr, :, lanes].astype(F32)
    return sc[...]


def _rope_fwd(qkv, cos, sin, nb, seq):
    t = qkv.shape[0]
    tm = ATT_TILE
    mtiles = seq // tm
    w = ATT_HEADS * ATT_D
    tab = pl.BlockSpec((tm, ATT_D), lambda i: (i % mtiles, 0))
    ng = len(ATT_DILATIONS)

    def body(q_ref, k_ref, v_ref, cos_ref, sin_ref, *rest):
        outs, sc = rest[:3 * ng], rest[3 * ng]
        c, s = cos_ref[...], sin_ref[...]
        for which, ref in enumerate((q_ref, k_ref, v_ref)):
            for h in range(ATT_HEADS):
                g, slot = divmod(h, ATT_SLOTS)
                p = ref[:, h * ATT_D:(h + 1) * ATT_D].astype(F32)
                if which < 2:
                    p = p * c + pltpu.roll(p, ATT_D // 2, 1) * s
                _to_strided(p, outs[which * ng + g], slice(slot * ATT_D, (slot + 1) * ATT_D), ATT_DILATIONS[g], sc)

    out_specs = [_strided_spec(r, mtiles) for _ in range(3) for r in ATT_DILATIONS]
    out_shape = [_strided_shape(nb, r, mtiles, BF16) for _ in range(3) for r in ATT_DILATIONS]
    outs = pl.pallas_call(
        body, name="rope_fwd", grid=(t // tm,),
        in_specs=[_rs(tm, w, 0), _rs(tm, w, 1), _rs(tm, w, 2), tab, tab], out_specs=out_specs, out_shape=out_shape,
        scratch_shapes=[pltpu.VMEM((tm, ATT_D), F32)], compiler_params=_cparams(("arbitrary",)))(
            qkv, qkv, qkv, cos, sin)
    flat = [o.reshape(t, ATT_W) for o in outs]
    return flat[0:ng], flat[ng:2 * ng], flat[2 * ng:]


def _rope_bwd(dq, dk, dv, cos, sin, nb, seq):
    t = dq[0].shape[0]
    tm = ATT_TILE
    mtiles = seq // tm
    w = ATT_HEADS * ATT_D
    tab = pl.BlockSpec((tm, ATT_D), lambda i: (i % mtiles, 0))
    ng = len(ATT_DILATIONS)

    def body(*refs):
        ins, (cos_ref, sin_ref, o_ref, sc) = refs[:3 * ng], refs[3 * ng:]
        c, s = cos_ref[...], sin_ref[...]
        for which in range(3):
            for h in range(ATT_HEADS):
                g, slot = divmod(h, ATT_SLOTS)
                p = _from_strided(ins[which * ng + g], slice(slot * ATT_D, (slot + 1) * ATT_D), ATT_DILATIONS[g], sc)
                if which < 2:
                    p = p * c - pltpu.roll(p, ATT_D // 2, 1) * s
                o_ref[:, which * w + h * ATT_D:which * w + (h + 1) * ATT_D] = p.astype(o_ref.dtype)

    views = [a.reshape(nb, r, mtiles, tm // r, ATT_W) for grp in (dq, dk, dv) for a, r in zip(grp, ATT_DILATIONS)]
    return pl.pallas_call(
        body, name="rope_bwd", grid=(t // tm,),
        in_specs=[_strided_spec(r, mtiles) for _ in range(3) for r in ATT_DILATIONS] + [tab, tab],
        out_specs=_rs(tm, 3 * w), out_shape=SDS((t, 3 * w), BF16),
        scratch_shapes=[pltpu.VMEM((tm, ATT_D), F32)], compiler_params=_cparams(("arbitrary",)))(*views, cos, sin)


def _att_masks():
    ri = lax.broadcasted_iota(jnp.int32, (ATT_BLOCK, ATT_BLOCK), 0)
    ci = lax.broadcasted_iota(jnp.int32, (ATT_BLOCK, ATT_BLOCK), 1)
    return ci <= ri, ci >= ri


def _att_fwd(q, k, v, g, seq):
    t, w = q.shape
    rows = ATT_QB * ATT_BLOCK
    nbs = seq // ATT_DILATIONS[g] // ATT_BLOCK
    scale = ATT_D ** -0.5
    cur = pl.BlockSpec((rows, w), lambda n: (n, 0))
    prev = pl.BlockSpec((ATT_BLOCK, w), lambda n: (jnp.maximum(n * ATT_QB - 1, 0), 0))

    def body(q_ref, kc_ref, kp_ref, vc_ref, vp_ref, o_ref, lse_ref):
        mcur, mprev = _att_masks()
        for i in range(ATT_QB):
            blk = pl.program_id(0) * ATT_QB + i
            mask = jnp.concatenate([mprev & ((blk % nbs) != 0), mcur], axis=-1)
            own = slice(i * ATT_BLOCK, (i + 1) * ATT_BLOCK)
            for h in range(ATT_SLOTS):
                sl = slice(h * ATT_D, (h + 1) * ATT_D)
                if i == 0:
                    keys = jnp.concatenate([kp_ref[:, sl], kc_ref[own, sl]], axis=0)
                    vals = jnp.concatenate([vp_ref[:, sl], vc_ref[own, sl]], axis=0)
                else:
                    both = slice((i - 1) * ATT_BLOCK, (i + 1) * ATT_BLOCK)
                    keys, vals = kc_ref[both, sl], vc_ref[both, sl]
                s = jnp.where(mask, _dot_nt(q_ref[own, sl], keys) * scale, -jnp.inf)
                m = jnp.max(s, axis=-1, keepdims=True)
                p = jnp.exp(s - m)
                den = jnp.sum(p, axis=-1, keepdims=True)
                o_ref[own, sl] = _dot(p.astype(BF16), vals) / den
                lse_ref[own, sl] = jnp.broadcast_to(m + jnp.log(den), (ATT_BLOCK, ATT_D))

    return pl.pallas_call(
        body, name=f"att_fwd_{g}", grid=(t // rows,), in_specs=[cur, cur, prev, cur, prev], out_specs=[cur, cur],
        out_shape=[SDS((t, w), F32), SDS((t, w), F32)],
        compiler_params=_cparams(("arbitrary",)))(q, k, k, v, v)


def _att_bwd(q, k, v, do, lse, dlt, g, seq):
    t, w = q.shape
    nblk = t // ATT_BLOCK
    rows = ATT_QB * ATT_BLOCK
    nbs = seq // ATT_DILATIONS[g] // ATT_BLOCK
    scale = ATT_D ** -0.5
    cur = pl.BlockSpec((rows, w), lambda n: (n, 0))
    nxt = pl.BlockSpec((ATT_BLOCK, w), lambda n: (jnp.minimum((n + 1) * ATT_QB, nblk - 1), 0))

    def body(qc_ref, qn_ref, k_ref, v_ref, doc_ref, don_ref, lsec_ref, lsen_ref, dltc_ref, dltn_ref,
             dq_ref, dk_ref, dv_ref, carry):
        n = pl.program_id(0)

        @pl.when(n == 0)
        def _():
            carry[...] = jnp.zeros_like(carry)

        mcur, mprev = _att_masks()

        def pair(cur_ref, nxt_ref, i, sl):
            if i + 1 < ATT_QB:
                return cur_ref[i * ATT_BLOCK:(i + 2) * ATT_BLOCK, sl]
            return jnp.concatenate([cur_ref[i * ATT_BLOCK:, sl], nxt_ref[:, sl]], axis=0)

        for h in range(ATT_SLOTS):
            sl = slice(h * ATT_D, (h + 1) * ATT_D)
            from_prev = carry[:, sl]
            for i in range(ATT_QB):
                blk = n * ATT_QB + i
                has_next = (((blk + 1) % nbs) != 0) & (blk + 1 < nblk)
                mask = jnp.concatenate([mcur, mprev & has_next], axis=0)
                own = slice(i * ATT_BLOCK, (i + 1) * ATT_BLOCK)
                kh, vh = k_ref[own, sl], v_ref[own, sl]
                qs, dos = pair(qc_ref, qn_ref, i, sl), pair(doc_ref, don_ref, i, sl)
                lse, dlt = pair(lsec_ref, lsen_ref, i, sl), pair(dltc_ref, dltn_ref, i, sl)
                p = jnp.where(mask, jnp.exp(_dot_nt(qs, kh) * scale - lse), 0.0)
                ds = (p * (_dot_nt(dos, vh) - dlt) * scale).astype(BF16)
                dqs = _dot(ds, kh)
                dq_ref[own, sl] = (from_prev + dqs[:ATT_BLOCK]).astype(dq_ref.dtype)
                from_prev = dqs[ATT_BLOCK:]
                dk_ref[own, sl] = _dot_tn(ds, qs).astype(dk_ref.dtype)
                dv_ref[own, sl] = _dot_tn(p.astype(BF16), dos).astype(dv_ref.dtype)
            carry[:, sl] = from_prev

    return pl.pallas_call(
        body, name=f"att_bwd_{g}", grid=(t // rows,), in_specs=[cur, nxt, cur, cur, cur, nxt, cur, nxt, cur, nxt],
        out_specs=[cur, cur, cur], out_shape=[SDS((t, w), BF16)] * 3,
        scratch_shapes=[pltpu.VMEM((ATT_BLOCK, w), F32)],
        compiler_params=_cparams(("arbitrary",)))(q, q, k, v, do, do, lse, lse, dlt, dlt)


def _merge_weights(ls):
    m = jnp.maximum(jnp.maximum(ls[0], ls[1]), ls[2])
    es = [jnp.exp(v - m) for v in ls]
    den = es[0] + es[1] + es[2]
    return [e / den for e in es]


def _merge_fwd(o, lse, nb, seq):
    t = o[0].shape[0]
    tm = ATT_TILE
    mtiles = seq // tm
    ng = len(ATT_DILATIONS)

    def body(*refs):
        o_refs, l_refs, out_ref, scs = refs[:ng], refs[ng:2 * ng], refs[2 * ng], refs[2 * ng + 1:]
        for slot in range(ATT_SLOTS):
            lanes = slice(slot * ATT_D, (slot + 1) * ATT_D)
            ov = [_from_strided(o_refs[g], lanes, r, scs[2 * g]) for g, r in enumerate(ATT_DILATIONS)]
            ws = _merge_weights([_from_strided(l_refs[g], lanes, r, scs[2 * g + 1])
                                 for g, r in enumerate(ATT_DILATIONS)])
            out_ref[:, lanes] = (ws[0] * ov[0] + ws[1] * ov[1] + ws[2] * ov[2]).astype(out_ref.dtype)

    views = [a.reshape(nb, r, mtiles, tm // r, ATT_W) for grp in (o, lse) for a, r in zip(grp, ATT_DILATIONS)]
    return pl.pallas_call(
        body, name="att_merge_fwd", grid=(t // tm,),
        in_specs=[_strided_spec(r, mtiles) for _ in range(2) for r in ATT_DILATIONS],
        out_specs=_rs(tm, ATT_W), out_shape=SDS((t, ATT_W), BF16),
        scratch_shapes=[pltpu.VMEM((tm, ATT_D), F32)] * (2 * ng), compiler_params=_cparams(("arbitrary",)))(*views)


def _merge_bwd(o, lse, datt, nb, seq):
    t = o[0].shape[0]
    tm = ATT_TILE
    mtiles = seq // tm
    ng = len(ATT_DILATIONS)

    def body(*refs):
        o_refs, l_refs, d_ref = refs[:ng], refs[ng:2 * ng], refs[2 * ng]
        do_refs, dlt_refs = refs[2 * ng + 1:3 * ng + 1], refs[3 * ng + 1:4 * ng + 1]
        scs = refs[4 * ng + 1:]
        for slot in range(ATT_SLOTS):
            lanes = slice(slot * ATT_D, (slot + 1) * ATT_D)
            ov = [_from_strided(o_refs[g], lanes, r, scs[2 * g]) for g, r in enumerate(ATT_DILATIONS)]
            ws = _merge_weights([_from_strided(l_refs[g], lanes, r, scs[2 * g + 1])
                                 for g, r in enumerate(ATT_DILATIONS)])
            dv = d_ref[:, lanes]
            att = ws[0] * ov[0] + ws[1] * ov[1] + ws[2] * ov[2]
            dot = jnp.broadcast_to(jnp.sum(dv * att, axis=-1, keepdims=True), (tm, ATT_D))
            for g, r in enumerate(ATT_DILATIONS):
                _to_strided(ws[g] * dv, do_refs[g], lanes, r, scs[2 * ng])
                _to_strided(ws[g] * dot, dlt_refs[g], lanes, r, scs[2 * ng + 1])

    views = [a.reshape(nb, r, mtiles, tm // r, ATT_W) for grp in (o, lse) for a, r in zip(grp, ATT_DILATIONS)]
    outs = pl.pallas_call(
        body, name="att_merge_bwd", grid=(t // tm,),
        in_specs=[_strided_spec(r, mtiles) for _ in range(2) for r in ATT_DILATIONS] + [_rs(tm, ATT_W)],
        out_specs=[_strided_spec(r, mtiles) for _ in range(2) for r in ATT_DILATIONS],
        out_shape=[_strided_shape(nb, r, mtiles, dt) for dt in (BF16, F32) for r in ATT_DILATIONS],
        scratch_shapes=[pltpu.VMEM((tm, ATT_D), F32)] * (2 * ng + 2), compiler_params=_cparams(("arbitrary",)))(
            *views, datt)
    flat = [a.reshape(t, ATT_W) for a in outs]
    return flat[:ng], flat[ng:]


def _mix_fwd(gate_logits, b_gate, y_ssm, y_att):
    t, d = y_ssm.shape
    tm = 512

    def fn(i, g0_ref, g1_ref, b0_ref, b1_ref, ys_ref, ya_ref):
        g0 = _sigmoid(g0_ref[...].astype(F32) + b0_ref[...])
        g1 = _sigmoid(g1_ref[...].astype(F32) + b1_ref[...])
        return [g0 * ys_ref[...].astype(F32) + g1 * ya_ref[...].astype(F32)]

    b_spec = lambda cb: pl.BlockSpec((1, d), lambda i: (0, cb))
    return _rw("mix_fwd", fn, t // tm,
               [(gate_logits, _rs(tm, d, 0)), (gate_logits, _rs(tm, d, 1)), (b_gate, b_spec(0)), (b_gate, b_spec(1)),
                (y_ssm, _rs(tm, d)), (y_att, _rs(tm, d))],
               [(SDS((t, d), BF16), _rs(tm, d))])[0]


def _mix_bwd(gate_logits, b_gate, y_ssm, y_att, dmixed):
    t, d = y_ssm.shape
    tm = 256

    def fn(i, g0_ref, g1_ref, b0_ref, b1_ref, ys_ref, ya_ref, dm_ref):
        g0 = _sigmoid(g0_ref[...].astype(F32) + b0_ref[...])
        g1 = _sigmoid(g1_ref[...].astype(F32) + b1_ref[...])
        dm = dm_ref[...]
        dg = jnp.concatenate([dm * ys_ref[...].astype(F32) * g0 * (1.0 - g0),
                              dm * ya_ref[...].astype(F32) * g1 * (1.0 - g1)], axis=-1)
        return [dm * g0, dm * g1, dg, _colsum(dg)]

    b_spec = lambda cb: pl.BlockSpec((1, d), lambda i: (0, cb))
    return _rw("mix_bwd", fn, t // tm,
               [(gate_logits, _rs(tm, d, 0)), (gate_logits, _rs(tm, d, 1)), (b_gate, b_spec(0)), (b_gate, b_spec(1)),
                (y_ssm, _rs(tm, d)), (y_att, _rs(tm, d)), (dmixed, _rs(tm, d))],
               [(SDS((t, d), BF16), _rs(tm, d)), (SDS((t, d), BF16), _rs(tm, d)),
                (SDS((t, 2 * d), BF16), _rs(tm, 2 * d)), (SDS((1, 2 * d), F32), _fs((1, 2 * d)))], n_acc=1)


FFN_TM = 512


def _ffn_in(h2, wg_t, wu_t):
    t, d = h2.shape
    f = wg_t.shape[0]
    tm, tn = FFN_TM, _pick(f, 1536)

    def body(a_ref, g_ref, u_ref, gt_ref, up_ref, act_ref):
        a = a_ref[...]
        gt = _dot_nt(a, g_ref[...])
        up = _dot_nt(a, u_ref[...])
        gt_ref[...] = gt.astype(BF16)
        up_ref[...] = up.astype(BF16)
        act_ref[...] = (gt * _sigmoid(gt) * up).astype(BF16)

    a_spec = pl.BlockSpec((tm, d), lambda j, i: (i, 0))
    w_spec = pl.BlockSpec((tn, d), lambda j, i: (j, 0))
    o_spec = pl.BlockSpec((tm, tn), lambda j, i: (i, j))
    return pl.pallas_call(
        body, name="ffn_in", grid=(f // tn, t // tm), in_specs=[a_spec, w_spec, w_spec],
        out_specs=[o_spec] * 3, out_shape=[SDS((t, f), BF16)] * 3,
        compiler_params=_cparams(("parallel", "arbitrary")))(h2, wg_t, wu_t)


def _ffn_bwd_in(dx2, w_down, gt, up):
    t, d = dx2.shape
    f = w_down.shape[0]
    tm, tn = FFN_TM, _pick(f, 1536)

    def body(a_ref, w_ref, g_ref, u_ref, dgt_ref, dup_ref):
        dv = _dot_nt(a_ref[...], w_ref[...])
        gv = g_ref[...].astype(F32)
        sg = _sigmoid(gv)
        dgt_ref[...] = (dv * u_ref[...].astype(F32) * sg * (1.0 + gv * (1.0 - sg))).astype(BF16)
        dup_ref[...] = (dv * gv * sg).astype(BF16)

    a_spec = pl.BlockSpec((tm, d), lambda j, i: (i, 0))
    w_spec = pl.BlockSpec((tn, d), lambda j, i: (j, 0))
    o_spec = pl.BlockSpec((tm, tn), lambda j, i: (i, j))
    return pl.pallas_call(
        body, name="ffn_bwd_in", grid=(f // tn, t // tm), in_specs=[a_spec, w_spec, o_spec, o_spec],
        out_specs=[o_spec] * 2, out_shape=[SDS((t, f), BF16)] * 2,
        compiler_params=_cparams(("parallel", "arbitrary")))(dx2, w_down, gt, up)


def _adamw(w, g, m, v, name):
    r, c = w.shape[-2:]
    lead = w.ndim - 2
    tr = _row_tile(r, max(8, 400_000 // c))
    c1 = 1.0 / (1.0 - ADAM_B1 ** ADAM_STEP)
    c2 = 1.0 / (1.0 - ADAM_B2 ** ADAM_STEP)

    def fn(i, w_ref, g_ref, m_ref, v_ref):
        gv = g_ref[...]
        mn = ADAM_B1 * m_ref[...] + (1.0 - ADAM_B1) * gv
        vn = ADAM_B2 * v_ref[...] + (1.0 - ADAM_B2) * (gv * gv)
        delta = -ADAM_LR * ((mn * c1) / (jnp.sqrt(vn * c2) + ADAM_EPS) + ADAM_WD * w_ref[...])
        return [delta, mn, vn]

    spec = pl.BlockSpec((None,) * lead + (tr, c), lambda i: (0,) * lead + (i, 0))
    return _rw(name, fn, r // tr, [(w, spec), (g, spec), (m, spec), (v, spec)], [(SDS(w.shape, F32), spec)] * 3)


ANY = pl.BlockSpec(memory_space=pl.ANY)


def _place():
    x, y, c = lax.axis_index("x"), lax.axis_index("y"), lax.axis_index("c")
    chips = [(1 - x, y), (x, 1 - y), (1 - x, 1 - y)]
    return x, y, c, chips


def _remote(src, dst, ssem, rsem, to):
    return pltpu.make_async_remote_copy(src_ref=src, dst_ref=dst, send_sem=ssem, recv_sem=rsem, device_id=to,
                                        device_id_type=MESH)


def _copy_through_vmem(src, dst, buf, isem, osem):
    chunk = buf.shape[1]
    n = src.shape[0] // chunk
    load = lambda k: pltpu.make_async_copy(src.at[pl.ds(k * chunk, chunk)], buf.at[k % 2], isem.at[k % 2])
    store = lambda k: pltpu.make_async_copy(buf.at[k % 2], dst.at[pl.ds(k * chunk, chunk)], osem.at[k % 2])
    load(0).start()
    for k in range(n):
        load(k).wait()
        if k + 1 < n:
            if k >= 1:
                store(k - 1).wait()
            load(k + 1).start()
        store(k).start()
    if n >= 2:
        store(n - 2).wait()
    store(n - 1).wait()


def _copy_scratch(rows, width, dtype):
    chunk = _row_tile(rows, 512)
    return [pltpu.VMEM((2, chunk, width), dtype), pltpu.SemaphoreType.DMA((2,)), pltpu.SemaphoreType.DMA((2,))]


def _gather_weights(wp):
    def body(w_ref, out_ref, ssem, rsem, buf, isem, osem):
        x, y, c, chips = _place()
        me = 2 * x + y
        sib = (x, y, 1 - c)
        first = [_remote(w_ref.at[c], out_ref.at[me, c], ssem.at[j], rsem.at[j], (*chip, c))
                 for j, chip in enumerate(chips)]
        for cp in first:
            cp.start()
        for half in range(2):
            _copy_through_vmem(w_ref.at[half], out_ref.at[me, half], buf, isem, osem)
        passed = []
        for j, chip in enumerate(chips):
            ci = 2 * chip[0] + chip[1]
            _remote(w_ref.at[c], out_ref.at[ci, c], ssem.at[j], rsem.at[j], (*chip, c)).wait_recv()
            cp = _remote(out_ref.at[ci, c], out_ref.at[ci, c], ssem.at[3 + j], rsem.at[3 + j], sib)
            cp.start()
            passed.append(cp)
        for j, chip in enumerate(chips):
            ci = 2 * chip[0] + chip[1]
            _remote(out_ref.at[ci, 1 - c], out_ref.at[ci, 1 - c], ssem.at[3 + j], rsem.at[3 + j], sib).wait_recv()
        for cp in first + passed:
            cp.wait_send()

    return pl.pallas_call(
        body, name="gather_weights", in_specs=[ANY], out_specs=ANY,
        out_shape=SDS((N_CHIPS,) + wp.shape, wp.dtype),
        scratch_shapes=[pltpu.SemaphoreType.DMA((6,)), pltpu.SemaphoreType.DMA((6,))]
        + _copy_scratch(wp.shape[1], wp.shape[2], wp.dtype),
        compiler_params=pltpu.CompilerParams(has_side_effects=True))(wp)


def _swap_halves(g2, tag):
    def body(g_ref, out_ref, ssem, rsem):
        x, y, c, _ = _place()
        cp = _remote(g_ref.at[1 - c], out_ref, ssem, rsem, (x, y, 1 - c))
        cp.start()
        cp.wait()

    return pl.pallas_call(
        body, name="swap_halves_" + tag, in_specs=[ANY], out_specs=ANY, out_shape=SDS(g2.shape[1:], g2.dtype),
        scratch_shapes=[pltpu.SemaphoreType.DMA(()), pltpu.SemaphoreType.DMA(())],
        compiler_params=pltpu.CompilerParams(has_side_effects=True))(g2)


def _add_own_half(g2, other, c, tag):
    _, nch, rows, w = g2.shape
    tr = _row_tile(rows, 512)
    nr = rows // tr

    def body(c_ref, a_ref, b_ref, o_ref):
        o_ref[...] = (a_ref[...].astype(F32) + b_ref[...].astype(F32)).astype(o_ref.dtype)

    grid_spec = pltpu.PrefetchScalarGridSpec(
        num_scalar_prefetch=1, grid=(nch, nr),
        in_specs=[pl.BlockSpec((None, None, tr, w), lambda k, i, c_ref: (c_ref[0], k, i, 0)),
                  pl.BlockSpec((None, tr, w), lambda k, i, c_ref: (k, i, 0))],
        out_specs=pl.BlockSpec((None, tr, w), lambda k, i, c_ref: (k, i, 0)))
    return pl.pallas_call(
        body, name="add_own_half_" + tag, grid_spec=grid_spec, out_shape=SDS(other.shape, other.dtype),
        compiler_params=_cparams(("arbitrary", "arbitrary")))(jnp.reshape(c, (1,)).astype(jnp.int32), g2, other)


def _sum_chips(q, tag):
    nch, rows, w = q.shape
    tr = _row_tile(rows, 512)

    def fn(i, q_ref):
        return [((q_ref[0].astype(F32) + q_ref[1].astype(F32)) + q_ref[2].astype(F32)) + q_ref[3].astype(F32)]

    return _rw("sum_chips_" + tag, fn, rows // tr, [(q, pl.BlockSpec((nch, tr, w), lambda i: (0, i, 0)))],
               [(SDS((rows, w), F32), _rs(tr, w))])[0]


def _chip_copies(src_ref, dst_ref, ssem, rsem, outgoing):
    x, y, c, chips = _place()
    me = 2 * x + y
    cps = []
    for j, chip in enumerate(chips):
        ci = 2 * chip[0] + chip[1]
        cps.append(_remote(src_ref.at[ci], dst_ref.at[me if outgoing else ci], ssem.at[j], rsem.at[j], (*chip, c)))
    return cps, me


def _scatter_side(p):
    def first(ins, outs, scr):
        cps, me = _chip_copies(ins[0], outs[0], scr[0], scr[1], True)
        for cp in cps:
            cp.start()
        pltpu.make_async_copy(ins[0].at[me], outs[0].at[me], scr[2]).start()

    def last(ins, outs, scr):
        for cp in _chip_copies(ins[0], outs[0], scr[0], scr[1], False)[0]:
            cp.wait_recv()
        cps, me = _chip_copies(ins[0], outs[0], scr[0], scr[1], True)
        for cp in cps:
            cp.wait_send()
        pltpu.make_async_copy(ins[0].at[me], outs[0].at[me], scr[2]).wait()

    return _Side((p,), (SDS(p.shape, p.dtype),),
                 (pltpu.SemaphoreType.DMA((3,)), pltpu.SemaphoreType.DMA((3,)), pltpu.SemaphoreType.DMA(())),
                 first, None, last)


def _gather_copies(w_ref, out_ref, ssem, rsem):
    x, y, c, chips = _place()
    me = 2 * x + y
    sib = (x, y, 1 - c)
    sends, arrivals, forwards, from_sib = [], [], [], []
    for j, chip in enumerate(chips):
        ci = 2 * chip[0] + chip[1]
        sends.append(_remote(w_ref.at[c], out_ref.at[me, c], ssem.at[j], rsem.at[j], (*chip, c)))
        arrivals.append(_remote(w_ref.at[c], out_ref.at[ci, c], ssem.at[j], rsem.at[j], (*chip, c)))
        forwards.append(_remote(out_ref.at[ci, c], out_ref.at[ci, c], ssem.at[3 + j], rsem.at[3 + j], sib))
        from_sib.append(_remote(out_ref.at[ci, 1 - c], out_ref.at[ci, 1 - c], ssem.at[3 + j], rsem.at[3 + j], sib))
    return sends, arrivals, forwards, from_sib, me


def _gather_side(wp):
    def first(ins, outs, scr):
        sends, _, _, _, me = _gather_copies(ins[0], outs[0], scr[0], scr[1])
        for cp in sends:
            cp.start()
        pltpu.make_async_copy(ins[0], outs[0].at[me], scr[2]).start()

    def mid(ins, outs, scr):
        _, arrivals, forwards, _, _ = _gather_copies(ins[0], outs[0], scr[0], scr[1])
        for arrived, forward in zip(arrivals, forwards):
            arrived.wait_recv()
            forward.start()

    def last(ins, outs, scr):
        sends, _, forwards, from_sib, me = _gather_copies(ins[0], outs[0], scr[0], scr[1])
        for cp in from_sib:
            cp.wait_recv()
        for cp in sends + forwards:
            cp.wait_send()
        pltpu.make_async_copy(ins[0], outs[0].at[me], scr[2]).wait()

    return _Side((wp,), (SDS((N_CHIPS,) + wp.shape, wp.dtype),),
                 (pltpu.SemaphoreType.DMA((6,)), pltpu.SemaphoreType.DMA((6,)), pltpu.SemaphoreType.DMA(())),
                 first, mid, last)


def _allreduce_small(v, name):
    rows, w = v.shape
    offsets = [(dx, dy, dc) for dx in (0, 1) for dy in (0, 1) for dc in (0, 1)][1:]

    def body(v_ref, o_ref, buf, ssem, rsem):
        x, y, c, _ = _place()
        flip = lambda p, d: 1 - p if d else p
        peers = [(flip(x, dx), flip(y, dy), flip(c, dc)) for dx, dy, dc in offsets]
        index = lambda p: 4 * p[0] + 2 * p[1] + p[2]
        me = index((x, y, c))
        buf[me] = v_ref[...]
        sent = [_remote(v_ref, buf.at[me], ssem.at[q], rsem.at[q], p) for q, p in enumerate(peers)]
        for cp in sent:
            cp.start()
        for q, p in enumerate(peers):
            _remote(v_ref, buf.at[index(p)], ssem.at[q], rsem.at[q], p).wait_recv()
        for cp in sent:
            cp.wait_send()
        acc = buf[0]
        for q in range(1, 8):
            acc = acc + buf[q]
        o_ref[...] = acc

    vm = pl.BlockSpec(memory_space=pltpu.VMEM)
    return pl.pallas_call(
        body, name=name, in_specs=[vm], out_specs=vm, out_shape=SDS((rows, w), F32),
        scratch_shapes=[pltpu.VMEM((8, rows, w), F32), pltpu.SemaphoreType.DMA((7,)), pltpu.SemaphoreType.DMA((7,))],
        compiler_params=pltpu.CompilerParams(has_side_effects=True))(v)


def _join_halves(h, tag):
    def body(h_ref, out_ref, ssem, rsem, buf, isem, osem):
        x, y, c, _ = _place()
        cp = _remote(h_ref, out_ref.at[c], ssem, rsem, (x, y, 1 - c))
        cp.start()
        _copy_through_vmem(h_ref, out_ref.at[c], buf, isem, osem)
        _remote(h_ref, out_ref.at[1 - c], ssem, rsem, (x, y, 1 - c)).wait_recv()
        cp.wait_send()

    return pl.pallas_call(
        body, name="join_halves_" + tag, in_specs=[ANY], out_specs=ANY, out_shape=SDS((2,) + h.shape, h.dtype),
        scratch_shapes=[pltpu.SemaphoreType.DMA(()), pltpu.SemaphoreType.DMA(())]
        + _copy_scratch(h.shape[0], h.shape[1], h.dtype),
        compiler_params=pltpu.CompilerParams(has_side_effects=True))(h)


PACK_W = 1024
SHARDED = ("w_in", "w_ffn_gate", "w_ffn_up", "w_ssm_out", "w_att_out", "w_mix_out", "w_ffn_down")
COL_SHARDED = ("w_in", "w_ffn_gate", "w_ffn_up", "w_att_out")
SMALL = ("norm_mix", "b_gate", "conv_b", "dt_bias", "a_log", "d_skip", "ssm_norm", "norm_ffn", "norm_final")


PACK_ROW_ALIGN = 16


def _rows(n):
    return -(-n // (PACK_W * PACK_ROW_ALIGN)) * PACK_ROW_ALIGN


def _pack_rows(parts, total_rows):
    rows = []
    for p in parts:
        size = int(p.size)
        if size % PACK_W:
            p = jnp.pad(p.reshape(-1), (0, PACK_W - size % PACK_W))
        p = p.reshape(-1, PACK_W)
        rows.append(jnp.pad(p, ((0, _rows(size) - p.shape[0]), (0, 0))))
    used = sum(r.shape[0] for r in rows)
    if total_rows > used:
        rows.append(jnp.zeros((total_rows - used, PACK_W), rows[0].dtype))
    return jnp.concatenate(rows, axis=0)


def _padded_rows(n):
    return -(-n // 32) * 32


def _wire_name(name):
    return name + "_t" if name in COL_SHARDED else name


def _wire_shard(w, name):
    return w.T if name in COL_SHARDED else w


def _group_major(a, axis):
    gw = D_INNER // N_GROUPS
    take = lambda lo, n: lax.slice_in_dim(a, lo, lo + n, axis=axis)
    parts = []
    for g in range(N_GROUPS):
        parts += [take(g * gw, gw), take(D_INNER + g * D_STATE, D_STATE),
                  take(D_INNER + N_GROUPS * D_STATE + g * D_STATE, D_STATE)]
    return jnp.concatenate(parts, axis=axis)


def _group_major_inv(a, axis):
    gw = D_INNER // N_GROUPS
    take = lambda lo, n: lax.slice_in_dim(a, lo, lo + n, axis=axis)
    xs = [take(g * GROUP_W, gw) for g in range(N_GROUPS)]
    bs = [take(g * GROUP_W + gw, D_STATE) for g in range(N_GROUPS)]
    cs = [take(g * GROUP_W + gw + D_STATE, D_STATE) for g in range(N_GROUPS)]
    return jnp.concatenate(xs + bs + cs, axis=axis)


LATE = ("w_ffn_gate_t", "w_ffn_up_t", "w_ssm_out", "w_att_out_t", "w_mix_out", "w_ffn_down")


class _Overlap(NamedTuple):
    gather_side: _Side
    late_weights: Callable
    scatter_side: Callable
    scatter_in: Callable


def _local_step(x, target, wts, overlap):
    nb, seq, d = x.shape
    t = nb * seq
    x = x.reshape(t, d)
    target = target.reshape(t, d)
    hg = HEADS_PER_GROUP

    o1, o2, o3, o4 = D_INNER, D_INNER + CONV_DIM, D_INNER + CONV_DIM + N_HEADS, D_INNER + CONV_DIM + N_HEADS + QKV_DIM
    n_in = o4 + 2 * D_MODEL

    def in_rows(lo, hi):
        per = n_in // N_CHIPS
        parts = [wts["w_in_t"][k, max(lo, k * per) - k * per:min(hi, (k + 1) * per) - k * per]
                 for k in range(N_CHIPS) if max(lo, k * per) < min(hi, (k + 1) * per)]
        return parts[0] if len(parts) == 1 else jnp.concatenate(parts, axis=0)

    w_z = in_rows(0, o1)
    w_xbc = _group_major(in_rows(o1, o2), 0)
    w_dt = jnp.pad(in_rows(o2, o3), ((0, DT_PAD - N_HEADS), (0, 0)))
    w_qkv = in_rows(o3, o4)
    w_gate = in_rows(o4, n_in)
    conv_w = _group_major(wts["conv_w"], 1)
    conv_b = _group_major(wts["conv_b"], 1)

    def per_group_row(p):
        return p.reshape(N_GROUPS, 1, hg)

    def per_group_col(p):
        return p.reshape(N_GROUPS, hg, 1)

    a_neg = -jnp.exp(wts["a_log"])
    bias_r, bias_c = per_group_row(wts["dt_bias"]), per_group_col(wts["dt_bias"])
    a_r, a_c = per_group_row(a_neg), per_group_col(a_neg)
    dskip_r = per_group_row(wts["d_skip"])
    cos, sin = _rope_tables(seq)

    h = _rms_fwd(x, wts["norm_mix"], "rms_mix_fwd")
    z = _mm(h, w_z, "nt", BF16, "proj_z")
    xbc = _mm(h, w_xbc, "nt", F32, "proj_xbc")
    dt_raw = _mm(h, w_dt, "nt", F32, "proj_dt")
    qkv = _mm(h, w_qkv, "nt", BF16, "proj_qkv")
    gate_logits = _mm(h, w_gate, "nt", BF16, "proj_gate")

    xc = _conv_fwd(xbc, conv_w, conv_b, seq)
    dtr = dt_raw[:, :N_HEADS].reshape(t, N_GROUPS, hg).transpose(1, 0, 2)
    dtrt = dt_raw[:, :N_HEADS].reshape(nb, seq, N_GROUPS, hg).transpose(2, 0, 3, 1)
    y, states, *gathered = _ssd_fwd(xc, dtr, dtrt, bias_r, bias_c, a_r, a_c, dskip_r, nb, seq, overlap.gather_side)
    wts = {**wts, **overlap.late_weights(gathered)}
    yn = _gate_norm_fwd(y, z, wts["ssm_norm"])
    y_ssm = _mm(yn, wts["w_ssm_out"], "nn", BF16, "ssm_out")

    groups = range(len(ATT_DILATIONS))
    qg, kg, vg = _rope_fwd(qkv, cos, sin, nb, seq)
    o_g, lse_g = zip(*[_att_fwd(qg[i], kg[i], vg[i], i, seq) for i in groups])
    att = _merge_fwd(o_g, lse_g, nb, seq)
    y_att = _mm(att, wts["w_att_out_t"], "nt", BF16, "att_out")

    mixed = _mix_fwd(gate_logits, wts["b_gate"], y_ssm, y_att)
    x1 = _mm(mixed, wts["w_mix_out"], "nn", F32, "mix_out", add=x)
    h2 = _rms_fwd(x1, wts["norm_ffn"], "rms_ffn_fwd")
    gt, up, act = _ffn_in(h2, wts["w_ffn_gate_t"], wts["w_ffn_up_t"])
    x2 = _mm(act, wts["w_ffn_down"], "nn", F32, "ffn_down", add=x1)

    g = {}
    dx2, dx2_b, g["norm_final"], loss = _final_fwd_bwd(x2, target, wts["norm_final"].reshape(1, d))
    g["w_ffn_down"] = _mm(act, dx2_b, "tn", BF16, "g_ffn_down")
    dgt, dup = _ffn_bwd_in(dx2_b, wts["w_ffn_down"], gt, up)
    g["w_ffn_gate_t"] = _mm(dgt, h2, "tn", BF16, "g_ffn_gate")
    g["w_ffn_up_t"] = _mm(dup, h2, "tn", BF16, "g_ffn_up")
    dh2 = _mm(dgt, wts["w_ffn_gate_t"], "nn", F32, "d_h2_gate")
    dh2 = _mm(dup, wts["w_ffn_up_t"], "nn", F32, "d_h2_up", add=dh2)
    dx1, dx1_b, g["norm_ffn"] = _rms_bwd(x1, dh2, wts["norm_ffn"], dx2, "rms_ffn_bwd")

    dmixed = _mm(dx1_b, wts["w_mix_out"], "nt", F32, "d_mixed")
    g["w_mix_out"] = _mm(mixed, dx1_b, "tn", BF16, "g_mix_out")
    dy_ssm, dy_att, dgate, g["b_gate"] = _mix_bwd(gate_logits, wts["b_gate"], y_ssm, y_att, dmixed)

    datt = _mm(dy_att, wts["w_att_out_t"], "nn", F32, "d_att")
    g["w_att_out_t"] = _mm(dy_att, att, "tn", BF16, "g_att_out")
    do_g, dlt_g = _merge_bwd(o_g, lse_g, datt, nb, seq)
    dq_g, dk_g, dv_g = zip(*[_att_bwd(qg[i], kg[i], vg[i], do_g[i], lse_g[i], dlt_g[i], i, seq) for i in groups])
    dqkv = _rope_bwd(dq_g, dk_g, dv_g, cos, sin, nb, seq)

    dyn = _mm(dy_ssm, wts["w_ssm_out"], "nt", BF16, "d_yn")
    g["w_ssm_out"] = _mm(yn, dy_ssm, "tn", BF16, "g_ssm_out")
    dy, dz, g["ssm_norm"] = _gate_norm_bwd(y, z, wts["ssm_norm"], dyn)
    side = overlap.scatter_side({n: g.pop(n) for n in LATE})
    dxc, ddtr, g_bias, g_alog, g_dskip, *scattered = _ssd_bwd(xc, dtr, dtrt, bias_r, bias_c, a_r, a_c, dskip_r,
                                                               states, dy, nb, seq, side)
    g["dt_bias"] = g_bias.reshape(1, N_HEADS)
    g["a_log"] = g_alog.reshape(1, N_HEADS)
    g["d_skip"] = g_dskip.reshape(1, N_HEADS)
    dpre, g_conv_w, g_conv_b = _conv_bwd_pre(xbc, conv_w, conv_b, dxc, seq)
    g["conv_w"] = _group_major_inv(g_conv_w, 1)
    g["conv_b"] = _group_major_inv(g_conv_b, 1)
    dxbc = _conv_bwd_in(dpre, conv_w, seq)
    ddt = jnp.pad(ddtr.transpose(1, 0, 2).reshape(t, N_HEADS), ((0, 0), (0, DT_PAD - N_HEADS))).astype(BF16)

    g_in_t = jnp.concatenate([
        _mm(dz, h, "tn", BF16, "g_in_z"),
        _group_major_inv(_mm(dxbc, h, "tn", BF16, "g_in_xbc"), 0),
        _mm(ddt, h, "tn", BF16, "g_in_dt")[:N_HEADS],
        _mm(dqkv, h, "tn", BF16, "g_in_qkv"),
        _mm(dgate, h, "tn", BF16, "g_in_gate")], axis=0)
    dh = _mm(dz, w_z, "nn", F32, "d_h_z")
    dh = _mm(dxbc, w_xbc, "nn", F32, "d_h_xbc", add=dh)
    dh = _mm(ddt, w_dt, "nn", F32, "d_h_dt", add=dh)
    dh = _mm(dgate, w_gate, "nn", F32, "d_h_gate", add=dh)
    dh, *scattered_in = _mm(dqkv, w_qkv, "nn", F32, "d_h_qkv", add=dh, side=overlap.scatter_in({"w_in_t": g_in_t}))
    dx, _, g["norm_mix"] = _rms_bwd(x, dh, wts["norm_mix"], dx1, "rms_mix_bwd")
    return loss[0, 0], dx.reshape(nb, seq, d), g, scattered, scattered_in


def kernel(x, norm_mix, w_in, b_gate, conv_w, conv_b, dt_bias, a_log, d_skip, ssm_norm, w_ssm_out, w_att_out, w_mix_out, norm_ffn, w_ffn_gate, w_ffn_up, w_ffn_down, norm_final, loss_target, m_norm_mix, m_w_in, m_b_gate, m_conv_w, m_conv_b, m_dt_bias, m_a_log, m_d_skip, m_ssm_norm, m_w_ssm_out, m_w_att_out, m_w_mix_out, m_norm_ffn, m_w_ffn_gate, m_w_ffn_up, m_w_ffn_down, m_norm_final, v_norm_mix, v_w_in, v_b_gate, v_conv_w, v_conv_b, v_dt_bias, v_a_log, v_d_skip, v_ssm_norm, v_w_ssm_out, v_w_att_out, v_w_mix_out, v_norm_ffn, v_w_ffn_gate, v_w_ffn_up, v_w_ffn_down, v_norm_final):
    names = ("norm_mix", "w_in", "b_gate", "conv_w", "conv_b", "dt_bias", "a_log", "d_skip", "ssm_norm", "w_ssm_out",
             "w_att_out", "w_mix_out", "norm_ffn", "w_ffn_gate", "w_ffn_up", "w_ffn_down", "norm_final")
    w_loc = dict(zip(names, (norm_mix, w_in, b_gate, conv_w, conv_b, dt_bias, a_log, d_skip, ssm_norm, w_ssm_out,
                             w_att_out, w_mix_out, norm_ffn, w_ffn_gate, w_ffn_up, w_ffn_down, norm_final)))
    m_loc = dict(zip(names, (m_norm_mix, m_w_in, m_b_gate, m_conv_w, m_conv_b, m_dt_bias, m_a_log, m_d_skip,
                             m_ssm_norm, m_w_ssm_out, m_w_att_out, m_w_mix_out, m_norm_ffn, m_w_ffn_gate,
                             m_w_ffn_up, m_w_ffn_down, m_norm_final)))
    v_loc = dict(zip(names, (v_norm_mix, v_w_in, v_b_gate, v_conv_w, v_conv_b, v_dt_bias, v_a_log, v_d_skip,
                             v_ssm_norm, v_w_ssm_out, v_w_att_out, v_w_mix_out, v_norm_ffn, v_w_ffn_gate,
                             v_w_ffn_up, v_w_ffn_down, v_norm_final)))
    two_d = lambda a: a.reshape(a.shape[-2:]) if a.ndim >= 2 else a.reshape(1, -1)
    w2 = {n: two_d(a) for n, a in w_loc.items()}
    chip = 2 * lax.axis_index("x") + lax.axis_index("y")
    c = lax.axis_index("c")

    wire_shapes = {n: _wire_shard(w2[n], n).shape for n in SHARDED}
    true_rows = {n: wire_shapes[n][0] * wire_shapes[n][1] // PACK_W for n in SHARDED}
    seg_rows = {n: _rows(wire_shapes[n][0] * wire_shapes[n][1]) for n in SHARDED}
    buckets = {"first": ("w_in",), "late": tuple(n for n in SHARDED if n != "w_in")}
    rows_of = {b: _padded_rows(sum(seg_rows[n] for n in ns)) for b, ns in buckets.items()}

    def pack_shards(b):
        packed = _pack_rows([_wire_shard(w2[n], n).astype(BF16) for n in buckets[b]], rows_of[b])
        return packed.reshape(2, rows_of[b] // 2, PACK_W)

    def unpack_full(gathered, b):
        wg, out, off = gathered.reshape(N_CHIPS, rows_of[b], PACK_W), {}, 0
        for n in buckets[b]:
            rows, cols = wire_shapes[n]
            out[_wire_name(n)] = wg[:, off:off + true_rows[n]].reshape(N_CHIPS * rows, cols)
            off += seg_rows[n]
        return out

    def pack_grads(g, b):
        sections = [_pack_rows([g[_wire_name(n)].reshape(N_CHIPS, true_rows[n], PACK_W)[k] for n in buckets[b]],
                               rows_of[b]) for k in range(N_CHIPS)]
        return jnp.stack(sections).reshape(N_CHIPS, 2, rows_of[b] // 2, PACK_W).transpose(1, 0, 2, 3)

    def chip_sums(g, b):
        g2 = pack_grads(g, b)
        return _add_own_half(g2, _swap_halves(g2, b), c, b)

    def finish(by_source, b):
        reduced = _join_halves(_sum_chips(by_source, b), b).reshape(rows_of[b], PACK_W)
        out, off = {}, 0
        for n in buckets[b]:
            out[n] = reduced[off:off + true_rows[n]].reshape(wire_shapes[n])
            off += seg_rows[n]
        return out

    full = {"w_in_t": _gather_weights(pack_shards("first")).reshape(N_CHIPS, rows_of["first"], PACK_W)}
    for n in SMALL:
        full[n] = w2[n]
    overlap = _Overlap(_gather_side(pack_shards("late")), lambda outs: unpack_full(outs[0], "late"),
                       lambda g: _scatter_side(chip_sums(g, "late")), lambda g: _scatter_side(chip_sums(g, "first")))

    n_conv = w2["conv_w"].shape[1]
    placed = lax.dynamic_update_slice_in_dim(jnp.zeros((CONV_K, N_CHIPS * n_conv), F32), w2["conv_w"], chip * n_conv, 1)
    placed = jnp.where(c == 0, placed, 0.0)
    full["conv_w"] = _allreduce_small(_pack_rows([placed], _rows(int(placed.size))), "gather_conv_w").reshape(
        -1)[:placed.size].reshape(placed.shape)

    loss_sum, grad_x, g_full, scattered, scattered_in = _local_step(x, loss_target, full, overlap)
    loss = lax.psum(loss_sum, ("x", "y", "c"))

    g_shard = {}
    small_names = SMALL + ("conv_w",)
    small_flat = jnp.concatenate([g_full[n].reshape(-1) for n in small_names])
    small = _allreduce_small(_pack_rows([small_flat], _rows(int(small_flat.size))), "allreduce_small").reshape(-1)
    off = 0
    for n in small_names:
        size = int(g_full[n].size)
        g_shard[n] = small[off:off + size].reshape(g_full[n].shape)
        off += size
    g_shard["conv_w"] = lax.dynamic_slice_in_dim(g_shard["conv_w"], chip * n_conv, n_conv, 1)

    g_shard.update(finish(scattered[0], "late"))
    g_shard.update(finish(scattered_in[0], "first"))

    grads, deltas, new_m, new_v = [], [], [], []
    for n in names:
        shape = w_loc[n].shape
        if n in COL_SHARDED:
            view = unview = lambda a: jnp.swapaxes(a, -1, -2)
        else:
            view, unview = ((lambda a: a) if len(shape) >= 2 else two_d), (lambda a: a.reshape(shape))
        gn = g_shard[n].reshape(view(w_loc[n]).shape)
        outs = _adamw(view(w_loc[n]), gn, view(m_loc[n]), view(v_loc[n]), "adamw_" + n)
        for acc, a in zip((grads, deltas, new_m, new_v), (gn, *outs)):
            acc.append(unview(a))
    return (loss, grad_x, *grads, *deltas, *new_m, *new_v)
```

```python
import functools
from typing import Callable, NamedTuple, Optional

import jax
import jax.numpy as jnp
from jax import lax
from jax.experimental import pallas as pl
from jax.experimental.pallas import tpu as pltpu

F32 = jnp.float32
BF16 = jnp.bfloat16
SDS = jax.ShapeDtypeStruct
MESH = pl.DeviceIdType.MESH

D_MODEL = 1024
D_INNER = 2048
N_HEADS = 32
HEAD_P = 64
N_GROUPS = 4
HEADS_PER_GROUP = N_HEADS // N_GROUPS
D_STATE = 128
CONV_K = 4
CHUNK = 128
CONV_DIM = D_INNER + 2 * N_GROUPS * D_STATE
GROUP_W = D_INNER // N_GROUPS + 2 * D_STATE
ATT_HEADS = 12
ATT_D = 128
ATT_SLOTS = 4
ATT_W = ATT_SLOTS * ATT_D
ATT_DILATIONS = (1, 4, 16)
ATT_BLOCK = 128
QKV_DIM = 3 * ATT_HEADS * ATT_D
D_FF = 2816
DT_PAD = 128
ROPE_THETA = 10000.0
EPS = 1e-6
N_CHIPS = 4
LANES = 128

ADAM_LR = 0.001
ADAM_B1 = 0.9
ADAM_B2 = 0.999
ADAM_EPS = 1e-08
ADAM_WD = 0.01
ADAM_STEP = 10

VMEM_LIMIT = 48 * 1024 * 1024


def _cparams(semantics):
    return pltpu.CompilerParams(dimension_semantics=semantics, vmem_limit_bytes=VMEM_LIMIT)


def _pick(n, cap):
    best = None
    for t in range(LANES, min(n, cap) + 1, LANES):
        if n % t == 0:
            best = t
    return best or n


def _row_tile(rows, cap):
    best = None
    for t in range(8, min(rows, cap) + 1, 8):
        if rows % t == 0:
            best = t
    return best or rows


def _sigmoid(x):
    return 1.0 / (1.0 + jnp.exp(-x))


def _softplus(x):
    return jnp.maximum(x, 0.0) + jnp.log(1.0 + jnp.exp(-jnp.abs(x)))


def _dot(a, b):
    return jnp.dot(a, b, preferred_element_type=F32)


def _dot_nt(a, b):
    return lax.dot_general(a, b, (((1,), (1,)), ((), ())), preferred_element_type=F32)


def _dot_tn(a, b):
    return lax.dot_general(a, b, (((0,), (0,)), ((), ())), preferred_element_type=F32)


def _mm(a, b, mode, out_dtype, name, add=None, side=None):
    if mode == "nn":
        (m, k), (_, n) = a.shape, b.shape
    elif mode == "nt":
        (m, k), (n, _) = a.shape, b.shape
    else:
        (k, m), (_, n) = a.shape, b.shape
    tm, tn = _pick(m, 1536), _pick(n, 2048)
    tk = k if k <= 2048 else _pick(k, 2048)
    nk = k // tk
    dims = {"nn": ((1,), (0,)), "nt": ((1,), (1,)), "tn": ((0,), (0,))}[mode]

    def partial_product(a_ref, b_ref):
        return lax.dot_general(a_ref[...].astype(BF16), b_ref[...].astype(BF16), (dims, ((), ())),
                               preferred_element_type=F32)

    def body(*refs):
        a_ref, b_ref = refs[:2]
        c_ref = refs[2] if add is not None else None
        o_ref = refs[3] if add is not None else refs[2]

        def finish(r):
            if add is not None:
                r = r + c_ref[...].astype(F32)
            o_ref[...] = r.astype(out_dtype)

        if nk == 1:
            finish(partial_product(a_ref, b_ref))
            return
        acc = refs[-1]
        kk = pl.program_id(2)

        @pl.when(kk == 0)
        def _():
            acc[...] = partial_product(a_ref, b_ref)

        @pl.when((kk > 0) & (kk < nk - 1))
        def _():
            acc[...] += partial_product(a_ref, b_ref)

        @pl.when(kk == nk - 1)
        def _():
            finish(acc[...] + partial_product(a_ref, b_ref))

    a_spec = {"nn": pl.BlockSpec((tm, tk), lambda j, i, q: (i, q)),
              "nt": pl.BlockSpec((tm, tk), lambda j, i, q: (i, q)),
              "tn": pl.BlockSpec((tk, tm), lambda j, i, q: (q, i))}[mode]
    b_spec = {"nn": pl.BlockSpec((tk, tn), lambda j, i, q: (q, j)),
              "nt": pl.BlockSpec((tn, tk), lambda j, i, q: (j, q)),
              "tn": pl.BlockSpec((tk, tn), lambda j, i, q: (q, j))}[mode]
    o_spec = pl.BlockSpec((tm, tn), lambda j, i, q: (i, j))
    ins, specs = [a, b], [a_spec, b_spec]
    if add is not None:
        ins.append(add)
        specs.append(o_spec)
    acc = [pltpu.VMEM((tm, tn), F32)] if nk > 1 else []
    grid = (n // tn, m // tm, nk)
    if side is None:
        return pl.pallas_call(
            body, name=name, grid=grid, in_specs=specs, out_specs=o_spec, out_shape=SDS((m, n), out_dtype),
            scratch_shapes=acc, compiler_params=_cparams(("parallel", "parallel", "arbitrary")))(*ins)
    return pl.pallas_call(
        _attach_side(body, len(ins), 1, side, grid), name=name, grid=grid,
        in_specs=specs + [ANY] * len(side.ins), out_specs=[o_spec] + [ANY] * len(side.out_shapes),
        out_shape=[SDS((m, n), out_dtype)] + list(side.out_shapes), scratch_shapes=acc + list(side.scratch),
        compiler_params=_cparams(("arbitrary", "arbitrary", "arbitrary")))(*ins, *side.ins)


def _mm_fused(a, b, mode, name, tm, epilogue, row_ins, full_ins, outs, n_acc=0, add=None):
    (m, k), n = a.shape, (b.shape[1] if mode == "nn" else b.shape[0])
    tk = k if k <= 2048 else _pick(k, 2048)
    nk = k // tk
    dims = {"nn": ((1,), (0,)), "nt": ((1,), (1,))}[mode]
    n_row, n_full, n_out = len(row_ins), len(full_ins), len(outs)

    def partial_product(a_ref, b_ref):
        return lax.dot_general(a_ref[...], b_ref[...], (dims, ((), ())), preferred_element_type=F32)

    def body(*refs):
        a_ref, b_ref = refs[:2]
        pos = 3 if add is not None else 2
        row_refs, full_refs = refs[pos:pos + n_row], refs[pos + n_row:pos + n_row + n_full]
        out_refs = refs[pos + n_row + n_full:pos + n_row + n_full + n_out]
        i, kk = pl.program_id(0), pl.program_id(1)

        def finish(r):
            if add is not None:
                r = r + refs[2][...].astype(F32)
            for q, (o_ref, v) in enumerate(zip(out_refs, epilogue(r, row_refs, full_refs))):
                if q < n_out - n_acc:
                    o_ref[...] = v.astype(o_ref.dtype)
                else:
                    @pl.when(i == 0)
                    def _(o_ref=o_ref, v=v):
                        o_ref[...] = v

                    @pl.when(i > 0)
                    def _(o_ref=o_ref, v=v):
                        o_ref[...] += v

        if nk == 1:
            finish(partial_product(a_ref, b_ref))
            return
        acc = refs[-1]

        @pl.when(kk == 0)
        def _():
            acc[...] = partial_product(a_ref, b_ref)

        @pl.when((kk > 0) & (kk < nk - 1))
        def _():
            acc[...] += partial_product(a_ref, b_ref)

        @pl.when(kk == nk - 1)
        def _():
            finish(acc[...] + partial_product(a_ref, b_ref))

    tile = lambda w, cb: pl.BlockSpec((tm, w), lambda i, q: (i, cb))
    b_spec = (pl.BlockSpec((tk, n), lambda i, q: (q, 0)) if mode == "nn" else pl.BlockSpec((n, tk), lambda i, q: (0, q)))
    specs = [pl.BlockSpec((tm, tk), lambda i, q: (i, q)), b_spec] + ([tile(n, 0)] if add is not None else [])
    specs += [tile(w, cb) for _, w, cb in row_ins]
    vec = lambda w, cb: pl.BlockSpec((1, w), lambda i, q: (0, cb))
    specs += [vec(w, cb) for _, w, cb in full_ins]
    out_specs = [tile(w, 0) for w, _ in outs[:n_out - n_acc]] + [vec(w, 0) for w, _ in outs[n_out - n_acc:]]
    out_shape = [SDS((m, w), dt) for w, dt in outs[:n_out - n_acc]] + [SDS((1, w), F32) for w, _ in outs[n_out - n_acc:]]
    ins = [a, b] + ([add] if add is not None else []) + [x for x, _, _ in row_ins] + [x for x, _, _ in full_ins]
    return pl.pallas_call(
        body, name=name, grid=(m // tm, nk), in_specs=specs, out_specs=out_specs, out_shape=out_shape,
        scratch_shapes=[pltpu.VMEM((tm, n), F32)] if nk > 1 else [],
        compiler_params=_cparams(("arbitrary", "arbitrary")))(*ins)


def _rw(name, fn, nsteps, ins, outs, n_acc=0):
    n_in, n_out = len(ins), len(outs)

    def body(*refs):
        i = pl.program_id(0)
        vals = fn(i, *refs[:n_in])
        for q, (r, v) in enumerate(zip(refs[n_in:], vals)):
            if q < n_out - n_acc:
                r[...] = v.astype(r.dtype)
            else:
                @pl.when(i == 0)
                def _(r=r):
                    r[...] = jnp.zeros_like(r)

                r[...] += v

    return pl.pallas_call(
        body, name=name, grid=(nsteps,), in_specs=[s for _, s in ins], out_specs=[s for _, s in outs],
        out_shape=[o for o, _ in outs], compiler_params=_cparams(("arbitrary",)))(*[a for a, _ in ins])


def _rs(tm, w, cb=0):
    return pl.BlockSpec((tm, w), lambda i: (i, cb))


def _fs(shape):
    nd = len(shape)
    return pl.BlockSpec(shape, lambda i: (0,) * nd)


def _colsum(v):
    return jnp.sum(v, axis=0, keepdims=True)


def _rms_fwd(x, g, name):
    t, d = x.shape
    tm = 512

    def fn(i, x_ref, g_ref):
        xv = x_ref[...]
        r = lax.rsqrt(jnp.mean(xv * xv, axis=-1, keepdims=True) + EPS)
        return [xv * r * g_ref[...]]

    return _rw(name, fn, t // tm, [(x, _rs(tm, d)), (g, _fs((1, d)))], [(SDS((t, d), BF16), _rs(tm, d))])[0]


def _rms_bwd_values(xv, dhv, gv, dres):
    r = lax.rsqrt(jnp.mean(xv * xv, axis=-1, keepdims=True) + EPS)
    xhat = xv * r
    dxhat = dhv * gv
    dx = dres + r * (dxhat - xhat * jnp.mean(dxhat * xhat, axis=-1, keepdims=True))
    return [dx, dx, _colsum(dhv * xhat)]


def _rms_bwd(x, dh, g, dres, name):
    t, d = x.shape
    tm = 512

    def fn(i, x_ref, dh_ref, g_ref, dres_ref):
        return _rms_bwd_values(x_ref[...], dh_ref[...], g_ref[...], dres_ref[...])

    return _rw(name, fn, t // tm,
               [(x, _rs(tm, d)), (dh, _rs(tm, d)), (g, _fs((1, d))), (dres, _rs(tm, d))],
               [(SDS((t, d), F32), _rs(tm, d)), (SDS((t, d), BF16), _rs(tm, d)), (SDS((1, d), F32), _fs((1, d)))],
               n_acc=1)


def _final_values(xv, target, gv):
    d = xv.shape[-1]
    r = lax.rsqrt(jnp.mean(xv * xv, axis=-1, keepdims=True) + EPS)
    xhat = xv * r
    diff = xhat * gv - target
    lsum = 0.5 * jnp.sum(jnp.sum(diff * diff, axis=-1, keepdims=True) * (1.0 / d), axis=0, keepdims=True)
    dy = diff * (1.0 / d)
    dxhat = dy * gv
    dx = r * (dxhat - xhat * jnp.mean(dxhat * xhat, axis=-1, keepdims=True))
    return [dx, dx, _colsum(dy * xhat), lsum]


CONV_TS = 512
CONV_HALO = 8


def _conv_specs(seq, c):
    ts, tc = CONV_TS, GROUP_W
    hb = ts // CONV_HALO
    u_spec = pl.BlockSpec((ts, tc), lambda j, i: (i, j))
    prev_spec = pl.BlockSpec((CONV_HALO, tc), lambda j, i: (jnp.maximum(i * hb - 1, 0), j))
    w_spec = pl.BlockSpec((CONV_K, tc), lambda j, i: (0, j))
    b_spec = pl.BlockSpec((1, tc), lambda j, i: (0, j))
    return u_spec, prev_spec, w_spec, b_spec


CONV_ROWS = 16


def _conv_pre(i, seq, u_ref, prev_ref, w_ref, b_ref, ext):
    ts = CONV_TS
    first = (i % (seq // ts)) == 0
    ext[0:CONV_HALO, :] = jnp.where(first, 0.0, prev_ref[...])
    ext[CONV_HALO:, :] = u_ref[...]
    acc = jnp.broadcast_to(b_ref[...], u_ref.shape)
    for q in range(CONV_K):
        acc = acc + w_ref[q:q + 1, :] * ext[pl.ds(CONV_HALO - CONV_K + 1 + q, ts), :]
    return acc


def _conv_fwd(u, w, b, seq):
    t, c = u.shape
    ts, tc = CONV_TS, GROUP_W
    u_spec, prev_spec, w_spec, b_spec = _conv_specs(seq, c)

    def body(u_ref, prev_ref, w_ref, b_ref, o_ref, ext):
        pre = _conv_pre(pl.program_id(1), seq, u_ref, prev_ref, w_ref, b_ref, ext)
        o_ref[...] = pre * _sigmoid(pre)

    return pl.pallas_call(
        body, name="conv_fwd", grid=(c // tc, t // ts), in_specs=[u_spec, prev_spec, w_spec, b_spec],
        out_specs=u_spec, out_shape=SDS((t, c), F32), scratch_shapes=[pltpu.VMEM((ts + CONV_HALO, tc), F32)],
        compiler_params=_cparams(("parallel", "arbitrary")))(u, u, w, b)


def _conv_bwd_pre(u, w, b, dxc, seq):
    t, c = u.shape
    ts, tc = CONV_TS, GROUP_W
    u_spec, prev_spec, w_spec, b_spec = _conv_specs(seq, c)

    def body(u_ref, prev_ref, w_ref, b_ref, d_ref, dpre_ref, dw_ref, db_ref, ext):
        i = pl.program_id(1)
        pre = _conv_pre(i, seq, u_ref, prev_ref, w_ref, b_ref, ext)
        sg = _sigmoid(pre)
        dpre = d_ref[...] * sg * (1.0 + pre * (1.0 - sg))
        dpre_ref[...] = dpre

        @pl.when(i == 0)
        def _():
            dw_ref[...] = jnp.zeros_like(dw_ref)
            db_ref[...] = jnp.zeros_like(db_ref)

        db_ref[...] += _colsum(dpre)
        for q in range(CONV_K):
            dw_ref[q:q + 1, :] += _colsum(dpre * ext[pl.ds(CONV_HALO - CONV_K + 1 + q, ts), :])

    return pl.pallas_call(
        body, name="conv_bwd_pre", grid=(c // tc, t // ts),
        in_specs=[u_spec, prev_spec, w_spec, b_spec, u_spec], out_specs=[u_spec, w_spec, b_spec],
        out_shape=[SDS((t, c), F32), SDS((CONV_K, c), F32), SDS((1, c), F32)],
        scratch_shapes=[pltpu.VMEM((ts + CONV_HALO, tc), F32)],
        compiler_params=_cparams(("parallel", "arbitrary")))(u, u, w, b, dxc)


def _conv_bwd_in(dpre, w, seq):
    t, c = dpre.shape
    ts, tc = CONV_TS, GROUP_W
    hb = ts // CONV_HALO
    last = t // CONV_HALO - 1
    d_spec = pl.BlockSpec((ts, tc), lambda j, i: (i, j))
    next_spec = pl.BlockSpec((CONV_HALO, tc), lambda j, i: (jnp.minimum((i + 1) * hb, last), j))
    w_spec = pl.BlockSpec((CONV_K, tc), lambda j, i: (0, j))

    def body(d_ref, next_ref, w_ref, o_ref, ext):
        i = pl.program_id(1)
        nts = seq // ts
        is_last = (i % nts) == nts - 1
        ext[0:ts, :] = d_ref[...]
        ext[ts:, :] = jnp.where(is_last, 0.0, next_ref[...])
        wv = w_ref[...]

        def rows(j, carry):
            r0 = pl.multiple_of(j * CONV_ROWS, CONV_ROWS)
            blk = ext[pl.ds(r0, CONV_ROWS + CONV_HALO), :]
            acc = wv[CONV_K - 1:CONV_K] * blk[0:CONV_ROWS]
            for q in range(CONV_K - 1):
                acc = acc + wv[q:q + 1] * blk[CONV_K - 1 - q:CONV_K - 1 - q + CONV_ROWS]
            o_ref[pl.ds(r0, CONV_ROWS), :] = acc.astype(o_ref.dtype)
            return carry

        lax.fori_loop(0, ts // CONV_ROWS, rows, 0)

    return pl.pallas_call(
        body, name="conv_bwd_in", grid=(c // tc, t // ts), in_specs=[d_spec, next_spec, w_spec],
        out_specs=d_spec, out_shape=SDS((t, c), BF16), scratch_shapes=[pltpu.VMEM((ts + CONV_HALO, tc), F32)],
        compiler_params=_cparams(("parallel", "arbitrary")))(dpre, dpre, w)


def _split3(v):
    hi = v.astype(BF16)
    r1 = v - hi.astype(F32)
    mid = r1.astype(BF16)
    lo = (r1 - mid.astype(F32)).astype(BF16)
    return hi, mid, lo


def _ssd_prelude(dtr_ref, dtrt_ref, bias_ref, biast_ref, a_ref, at_ref):
    dt = _softplus(dtr_ref[...] + bias_ref[...])
    dtt = _softplus(dtrt_ref[...] + biast_ref[...])
    ri = lax.broadcasted_iota(jnp.int32, (CHUNK, CHUNK), 0)
    ci = lax.broadcasted_iota(jnp.int32, (CHUNK, CHUNK), 1)
    lower = ri >= ci
    upper = ri <= ci
    lower_b = jnp.where(lower, 1.0, 0.0).astype(BF16)
    upper_b = jnp.where(upper, 1.0, 0.0).astype(BF16)
    acs = sum(_dot(lower_b, p) for p in _split3(dt * a_ref[...]))
    acst = sum(_dot(p, upper_b) for p in _split3(dtt * at_ref[...]))
    return dt, acs, acst, lower, upper, lower_b, upper_b


SSD_FWD_GPS = 2
SSD_BWD_GPS = 1


def _ssd_specs(seq, gps):
    nc = seq // CHUNK
    hg = HEADS_PER_GROUP
    fwd = lambda c: c
    rev = lambda c: nc - 1 - c

    def specs(cc):
        return dict(
            xc=pl.BlockSpec((CHUNK, gps * GROUP_W), lambda g, b, c: (b * nc + cc(c), g)),
            y=pl.BlockSpec((CHUNK, gps * hg * HEAD_P), lambda g, b, c: (b * nc + cc(c), g)),
            dtr=pl.BlockSpec((gps, CHUNK, hg), lambda g, b, c: (g, b * nc + cc(c), 0)),
            dtrt=pl.BlockSpec((gps, None, hg, CHUNK), lambda g, b, c: (g, b, 0, cc(c))),
            prow=pl.BlockSpec((gps, 1, hg), lambda g, b, c: (g, 0, 0)),
            pcol=pl.BlockSpec((gps, hg, 1), lambda g, b, c: (g, 0, 0)),
            st=pl.BlockSpec((gps, None, None, D_STATE, hg * HEAD_P), lambda g, b, c: (g, b, cc(c), 0, 0)),
        )

    return specs(fwd), specs(rev)


def _group_views(refs, lane_widths, gi):
    return [r.at[:, gi * w:(gi + 1) * w] if w else r.at[gi] for r, w in zip(refs, lane_widths)]


def _head_maps():
    hw = HEADS_PER_GROUP * HEAD_P
    shift = HEAD_P.bit_length() - 1
    hj = lax.broadcasted_iota(jnp.int32, (HEADS_PER_GROUP, hw), 0)
    lq = jnp.right_shift(lax.broadcasted_iota(jnp.int32, (HEADS_PER_GROUP, hw), 1), shift)
    spread = jnp.where(hj == lq, 1.0, 0.0).astype(BF16)
    rq = jnp.right_shift(lax.broadcasted_iota(jnp.int32, (hw, LANES), 0), shift)
    cj = lax.broadcasted_iota(jnp.int32, (hw, LANES), 1)
    gather = jnp.where(rq == cj, 1.0, 0.0).astype(BF16)
    return spread, gather


def _dot01(v, m01):
    hi, mid, _ = _split3(v)
    return _dot(hi, m01) + _dot(mid, m01)


class _Side(NamedTuple):
    ins: tuple
    out_shapes: tuple
    scratch: tuple
    first: Callable
    mid: Optional[Callable]
    last: Callable


NO_SIDE = _Side((), (), (), lambda *refs: None, None, lambda *refs: None)


def _attach_side(body, n_in, n_out, side, grid):
    si, so, ss = len(side.ins), len(side.out_shapes), len(side.scratch)

    def wrapped(*refs):
        ins, s_in = refs[:n_in], refs[n_in:n_in + si]
        outs = refs[n_in + si:n_in + si + n_out]
        s_out = refs[n_in + si + n_out:n_in + si + n_out + so]
        rest = refs[n_in + si + n_out + so:]
        scr, s_scr = rest[:len(rest) - ss], rest[len(rest) - ss:]
        ids = [pl.program_id(a) for a in range(len(grid))]
        inner_first = functools.reduce(lambda p, q: p & q, [i == 0 for i in ids[1:]], ids[0] >= 0)
        at_last = functools.reduce(lambda p, q: p & q, [i == n - 1 for i, n in zip(ids, grid)])

        @pl.when((ids[0] == 0) & inner_first)
        def _():
            side.first(s_in, s_out, s_scr)

        if side.mid is not None:
            outer_last = functools.reduce(lambda p, q: p & q, [i == n - 1 for i, n in zip(ids[:-1], grid[:-1])])

            @pl.when(outer_last & (ids[-1] == 0))
            def _():
                side.mid(s_in, s_out, s_scr)

        body(*ins, *outs, *scr)

        @pl.when(at_last)
        def _():
            side.last(s_in, s_out, s_scr)

    return wrapped


def _ssd_fwd(xc, dtr, dtrt, bias, biast, a, at, dskip, nb, seq, side):
    t = xc.shape[0]
    nc = seq // CHUNK
    hg = HEADS_PER_GROUP
    hw = hg * HEAD_P
    gps = SSD_FWD_GPS
    grid = (N_GROUPS // gps, nb, nc)
    sp, _ = _ssd_specs(seq, gps)

    def body(*refs):
        for gi in range(gps):
            one_group(*_group_views(refs, (GROUP_W, 0, 0, 0, 0, 0, 0, 0, hw, 0, 0), gi))

    def one_group(xc_ref, dtr_ref, dtrt_ref, bias_ref, biast_ref, a_ref, at_ref, d_ref, y_ref, sin_ref, st):
        @pl.when(pl.program_id(2) == 0)
        def _():
            st[...] = jnp.zeros_like(st)

        s_in = st[...]
        sin_ref[...] = s_in
        dt, acs, acst, lower, _, _, _ = _ssd_prelude(dtr_ref, dtrt_ref, bias_ref, biast_ref, a_ref, at_ref)
        spread, _ = _head_maps()
        x = xc_ref[...]
        xs = x[:, :hw]
        b16 = x[:, hw:hw + D_STATE].astype(BF16)
        c16 = x[:, hw + D_STATE:].astype(BF16)
        cb = _dot_nt(c16, b16)
        last = acs[CHUNK - 1:CHUNK, :]
        e_x = _dot01(jnp.exp(acs), spread)
        dec_x = _dot01(jnp.exp(last - acs), spread)
        tot_x = e_x[CHUNK - 1:CHUNK, :]
        d_x = _dot01(jnp.broadcast_to(d_ref[...], (8, hg)), spread)[0:1, :]
        xdtf = xs * _dot01(dt, spread)
        xdt16 = xdtf.astype(BF16)
        yoff = e_x * _dot(c16, s_in.astype(BF16))
        st[...] = tot_x * s_in + _dot_tn(b16, (dec_x * xdtf).astype(BF16))
        parts = []
        for j in range(hg):
            decay = jnp.exp(jnp.where(lower, acs[:, j:j + 1] - acst[j:j + 1, :], -jnp.inf))
            parts.append(_dot((cb * decay).astype(BF16), xdt16[:, HEAD_P * j:HEAD_P * (j + 1)]))
        y_ref[...] = jnp.concatenate(parts, axis=-1) + yoff + d_x * xs

    return pl.pallas_call(
        _attach_side(body, 8, 2, side, grid), name="ssd_fwd", grid=grid,
        in_specs=[sp["xc"], sp["dtr"], sp["dtrt"], sp["prow"], sp["pcol"], sp["prow"], sp["pcol"], sp["prow"]]
        + [ANY] * len(side.ins),
        out_specs=[sp["y"], sp["st"]] + [ANY] * len(side.out_shapes),
        out_shape=[SDS((t, D_INNER), F32), SDS((N_GROUPS, nb, nc, D_STATE, hw), F32)] + list(side.out_shapes),
        scratch_shapes=[pltpu.VMEM((gps, D_STATE, hw), F32)] + list(side.scratch),
        compiler_params=_cparams(("arbitrary", "arbitrary", "arbitrary")))(
            xc, dtr, dtrt, bias, biast, a, at, dskip, *side.ins)


def _ssd_bwd(xc, dtr, dtrt, bias, biast, a, at, dskip, states, dy, nb, seq, side):
    t = xc.shape[0]
    nc = seq // CHUNK
    hg = HEADS_PER_GROUP
    hw = hg * HEAD_P
    gps = SSD_BWD_GPS
    grid = (N_GROUPS // gps, nb, nc)
    _, sp = _ssd_specs(seq, gps)

    def body(*refs):
        for gi in range(gps):
            one_group(*_group_views(refs, (GROUP_W, 0, 0, 0, 0, 0, 0, 0, 0, hw, GROUP_W, 0, 0, 0, 0, 0), gi))

    def one_group(xc_ref, dtr_ref, dtrt_ref, bias_ref, biast_ref, a_ref, at_ref, d_ref, sin_ref, dy_ref,
                  dxc_ref, ddtr_ref, gbias_ref, ga_ref, gd_ref, ds):
        first = (pl.program_id(1) == 0) & (pl.program_id(2) == 0)

        @pl.when(pl.program_id(2) == 0)
        def _():
            ds[...] = jnp.zeros_like(ds)

        @pl.when(first)
        def _():
            gbias_ref[...] = jnp.zeros_like(gbias_ref)
            ga_ref[...] = jnp.zeros_like(ga_ref)
            gd_ref[...] = jnp.zeros_like(gd_ref)

        dt, acs, acst, lower, upper, _, upper_b = _ssd_prelude(dtr_ref, dtrt_ref, bias_ref, biast_ref, a_ref, at_ref)
        spread, gather = _head_maps()
        x = xc_ref[...]
        dy = dy_ref[...]
        xs = x[:, :hw]
        b16 = x[:, hw:hw + D_STATE].astype(BF16)
        c16 = x[:, hw + D_STATE:].astype(BF16)
        dy16 = dy.astype(BF16)
        cb = _dot_nt(c16, b16)
        cbt = _dot_nt(b16, c16)
        last = acs[CHUNK - 1:CHUNK, :]
        e8 = jnp.exp(acs)
        dec8 = jnp.exp(last - acs)
        e_x = _dot01(e8, spread)
        dec_x = _dot01(dec8, spread)
        tot_x = e_x[CHUNK - 1:CHUNK, :]
        dt_x = _dot01(dt, spread)
        d_x = _dot01(jnp.broadcast_to(d_ref[...], (8, hg)), spread)[0:1, :]
        xdtf = xs * dt_x
        xdt16 = xdtf.astype(BF16)
        s_in = sin_ref[...]
        s16 = s_in.astype(BF16)
        ds_out = ds[...]
        ds16 = ds_out.astype(BF16)
        bds = _dot(b16, ds16)
        cs = _dot(c16, s16)
        edy16 = (e_x * dy).astype(BF16)
        ds[...] = tot_x * ds_out + _dot_tn(c16, edy16)
        lane8 = lax.broadcasted_iota(jnp.int32, (CHUNK, hg), 1)
        row8 = lax.broadcasted_iota(jnp.int32, (CHUNK, hg), 0)
        dacs8 = jnp.zeros((CHUNK, hg), F32)
        acc_m = jnp.zeros((CHUNK, CHUNK), F32)
        acc_mt = jnp.zeros((CHUNK, CHUNK), F32)
        dx_parts = []
        for j in range(hg):
            sl = slice(HEAD_P * j, HEAD_P * (j + 1))
            col = acs[:, j:j + 1]
            row = acst[j:j + 1, :]
            decay = jnp.exp(jnp.where(lower, col - row, -jnp.inf))
            decayt = jnp.exp(jnp.where(upper, row - col, -jnp.inf))
            wm = _dot_nt(dy16[:, sl], xdt16[:, sl]) * decay
            wmt = _dot_nt(xdt16[:, sl], dy16[:, sl]) * decayt
            acc_m = acc_m + wm
            acc_mt = acc_mt + wmt
            dacs8 = dacs8 + jnp.where(lane8 == j, jnp.sum(wm * cb, axis=-1, keepdims=True)
                                      - jnp.sum(wmt * cbt, axis=-1, keepdims=True), 0.0)
            dx_parts.append(_dot((cbt * decayt).astype(BF16), dy16[:, sl]))
        dx = jnp.concatenate(dx_parts, axis=-1) + dec_x * bds
        dxc_ref[:, :hw] = dx * dt_x + d_x * dy
        dxc_ref[:, hw:hw + D_STATE] = _dot(acc_mt.astype(BF16), c16) + _dot_nt((dec_x * xdtf).astype(BF16), ds16)
        dxc_ref[:, hw + D_STATE:] = _dot(acc_m.astype(BF16), b16) + _dot_nt(edy16, s16)
        dtot_rows = jnp.broadcast_to(_colsum(ds_out * s_in), (8, hw))
        sums = _dot01(jnp.concatenate([dy * cs, xdtf * bds, dx * xs, dy * xs, dtot_rows], axis=0), gather)
        de8 = sums[0:CHUNK, :hg]
        ddec8 = sums[CHUNK:2 * CHUNK, :hg]
        ddtx8 = sums[2 * CHUNK:3 * CHUNK, :hg]
        gd8 = _colsum(sums[3 * CHUNK:4 * CHUNK, :hg])
        dtot8 = sums[4 * CHUNK:4 * CHUNK + 1, :hg]
        extra = _colsum(ddec8 * dec8) + dtot8 * e8[CHUNK - 1:CHUNK, :]
        dacs8 = dacs8 + de8 * e8 - ddec8 * dec8 + jnp.where(row8 == CHUNK - 1, extra, 0.0)
        da = sum(_dot(upper_b, p) for p in _split3(dacs8))
        av = a_ref[...]
        ddt = da * av + ddtx8
        ddtr = ddt * _sigmoid(dtr_ref[...] + bias_ref[...])
        ddtr_ref[...] = ddtr
        gbias_ref[...] += _colsum(ddtr)
        ga_ref[...] += _colsum(da * dt) * av
        gd_ref[...] += gd8

    return pl.pallas_call(
        _attach_side(body, 10, 5, side, grid), name="ssd_bwd", grid=grid,
        in_specs=[sp["xc"], sp["dtr"], sp["dtrt"], sp["prow"], sp["pcol"], sp["prow"], sp["pcol"], sp["prow"],
                  sp["st"], sp["y"]] + [ANY] * len(side.ins),
        out_specs=[sp["xc"], sp["dtr"], sp["prow"], sp["prow"], sp["prow"]] + [ANY] * len(side.out_shapes),
        out_shape=[SDS((t, N_GROUPS * GROUP_W), F32), SDS((N_GROUPS, t, hg), F32)]
        + [SDS((N_GROUPS, 1, hg), F32)] * 3 + list(side.out_shapes),
        scratch_shapes=[pltpu.VMEM((gps, D_STATE, hw), F32)] + list(side.scratch),
        compiler_params=_cparams(("arbitrary", "arbitrary", "arbitrary")))(
            xc, dtr, dtrt, bias, biast, a, at, dskip, states, dy, *side.ins)


def _group_bcast(v, width, fn):
    parts = []
    for q in range(v.shape[-1] // width):
        s = fn(v[:, q * width:(q + 1) * width])
        parts.append(jnp.broadcast_to(s, (v.shape[0], width)))
    return jnp.concatenate(parts, axis=-1)


def _gate_norm_fwd(y, z, g):
    t, d = y.shape
    tm = 256
    gw = d // N_GROUPS

    def fn(i, y_ref, z_ref, g_ref):
        zv = z_ref[...].astype(F32)
        u = y_ref[...] * (zv * _sigmoid(zv))
        r = lax.rsqrt(_group_bcast(u * u, gw, lambda p: jnp.mean(p, axis=-1, keepdims=True)) + EPS)
        return [u * r * g_ref[...]]

    return _rw("gate_norm_fwd", fn, t // tm, [(y, _rs(tm, d)), (z, _rs(tm, d)), (g, _fs((1, d)))],
               [(SDS((t, d), BF16), _rs(tm, d))])[0]


def _gate_norm_bwd_epilogue(dv, rows, fulls):
    yv, zv = rows[0][...], rows[1][...].astype(F32)
    gw = yv.shape[-1] // N_GROUPS
    sg = _sigmoid(zv)
    sz = zv * sg
    u = yv * sz
    r = lax.rsqrt(_group_bcast(u * u, gw, lambda p: jnp.mean(p, axis=-1, keepdims=True)) + EPS)
    uhat = u * r
    duhat = dv * fulls[0][...]
    du = r * (duhat - uhat * _group_bcast(duhat * uhat, gw, lambda p: jnp.mean(p, axis=-1, keepdims=True)))
    dz = du * yv * sg * (1.0 + zv * (1.0 - sg))
    return [du * sz, dz, _colsum(dv * uhat)]


def _rope_tables(seq):
    half = ATT_D // 2
    inv = ROPE_THETA ** (-jnp.arange(half, dtype=F32) / half)
    ang = jnp.arange(seq, dtype=F32)[:, None] * inv[None, :]
    cos, sin = jnp.cos(ang), jnp.sin(ang)
    return jnp.concatenate([cos, cos], axis=-1), jnp.concatenate([-sin, sin], axis=-1)


ATT_TILE = 512
ATT_QB = 8


def _strided_spec(r, mtiles):
    return pl.BlockSpec((None, r, None, ATT_TILE // r, ATT_W), lambda i: (i // mtiles, 0, i % mtiles, 0, 0))


def _strided_shape(nb, r, mtiles, dtype):
    return SDS((nb, r, mtiles, ATT_TILE // r, ATT_W), dtype)


def _to_strided(val, out_ref, lanes, r, sc):
    if r == 1:
        out_ref[0, :, lanes] = val.astype(out_ref.dtype)
        return
    sc[...] = val
    for rr in range(r):
        out_ref[rr, :, lanes] = sc[pl.ds(rr, ATT_TILE // r, stride=r), :].astype(out_ref.dtype)


def _from_strided(in_ref, lanes, r, sc):
    if r == 1:
        return in_ref[0, :, lanes].astype(F32)
    for rr in range(r):
        sc[pl.ds(rr, ATT_TILE // r, stride=r), :] = in_ref[rr, :, lanes].astype(F32)
    return sc[...]


def _rope_fwd(qkv, cos, sin, nb, seq):
    t = qkv.shape[0]
    tm = ATT_TILE
    mtiles = seq // tm
    w = ATT_HEADS * ATT_D
    tab = pl.BlockSpec((tm, ATT_D), lambda i: (i % mtiles, 0))
    ng = len(ATT_DILATIONS)

    def body(q_ref, k_ref, v_ref, cos_ref, sin_ref, *rest):
        outs, sc = rest[:3 * ng], rest[3 * ng]
        c, s = cos_ref[...], sin_ref[...]
        for which, ref in enumerate((q_ref, k_ref, v_ref)):
            for h in range(ATT_HEADS):
                g, slot = divmod(h, ATT_SLOTS)
                p = ref[:, h * ATT_D:(h + 1) * ATT_D].astype(F32)
                if which < 2:
                    p = p * c + pltpu.roll(p, ATT_D // 2, 1) * s
                _to_strided(p, outs[which * ng + g], slice(slot * ATT_D, (slot + 1) * ATT_D), ATT_DILATIONS[g], sc)

    out_specs = [_strided_spec(r, mtiles) for _ in range(3) for r in ATT_DILATIONS]
    out_shape = [_strided_shape(nb, r, mtiles, BF16) for _ in range(3) for r in ATT_DILATIONS]
    outs = pl.pallas_call(
        body, name="rope_fwd", grid=(t // tm,),
        in_specs=[_rs(tm, w, 0), _rs(tm, w, 1), _rs(tm, w, 2), tab, tab], out_specs=out_specs, out_shape=out_shape,
        scratch_shapes=[pltpu.VMEM((tm, ATT_D), F32)], compiler_params=_cparams(("arbitrary",)))(
            qkv, qkv, qkv, cos, sin)
    flat = [o.reshape(t, ATT_W) for o in outs]
    return flat[0:ng], flat[ng:2 * ng], flat[2 * ng:]


def _rope_bwd(dq, dk, dv, cos, sin, nb, seq):
    t = dq[0].shape[0]
    tm = ATT_TILE
    mtiles = seq // tm
    w = ATT_HEADS * ATT_D
    tab = pl.BlockSpec((tm, ATT_D), lambda i: (i % mtiles, 0))
    ng = len(ATT_DILATIONS)

    def body(*refs):
        ins, (cos_ref, sin_ref, o_ref, sc) = refs[:3 * ng], refs[3 * ng:]
        c, s = cos_ref[...], sin_ref[...]
        for which in range(3):
            for h in range(ATT_HEADS):
                g, slot = divmod(h, ATT_SLOTS)
                p = _from_strided(ins[which * ng + g], slice(slot * ATT_D, (slot + 1) * ATT_D), ATT_DILATIONS[g], sc)
                if which < 2:
                    p = p * c - pltpu.roll(p, ATT_D // 2, 1) * s
                o_ref[:, which * w + h * ATT_D:which * w + (h + 1) * ATT_D] = p.astype(o_ref.dtype)

    views = [a.reshape(nb, r, mtiles, tm // r, ATT_W) for grp in (dq, dk, dv) for a, r in zip(grp, ATT_DILATIONS)]
    return pl.pallas_call(
        body, name="rope_bwd", grid=(t // tm,),
        in_specs=[_strided_spec(r, mtiles) for _ in range(3) for r in ATT_DILATIONS] + [tab, tab],
        out_specs=_rs(tm, 3 * w), out_shape=SDS((t, 3 * w), BF16),
        scratch_shapes=[pltpu.VMEM((tm, ATT_D), F32)], compiler_params=_cparams(("arbitrary",)))(*views, cos, sin)


def _att_masks():
    ri = lax.broadcasted_iota(jnp.int32, (ATT_BLOCK, ATT_BLOCK), 0)
    ci = lax.broadcasted_iota(jnp.int32, (ATT_BLOCK, ATT_BLOCK), 1)
    return ci <= ri, ci >= ri


def _att_fwd(q, k, v, g, seq):
    t, w = q.shape
    rows = ATT_QB * ATT_BLOCK
    nbs = seq // ATT_DILATIONS[g] // ATT_BLOCK
    scale = ATT_D ** -0.5
    cur = pl.BlockSpec((rows, w), lambda n: (n, 0))
    prev = pl.BlockSpec((ATT_BLOCK, w), lambda n: (jnp.maximum(n * ATT_QB - 1, 0), 0))

    def body(q_ref, kc_ref, kp_ref, vc_ref, vp_ref, o_ref, lse_ref):
        mcur, mprev = _att_masks()
        for i in range(ATT_QB):
            blk = pl.program_id(0) * ATT_QB + i
            mask = jnp.concatenate([mprev & ((blk % nbs) != 0), mcur], axis=-1)
            own = slice(i * ATT_BLOCK, (i + 1) * ATT_BLOCK)
            for h in range(ATT_SLOTS):
                sl = slice(h * ATT_D, (h + 1) * ATT_D)
                if i == 0:
                    keys = jnp.concatenate([kp_ref[:, sl], kc_ref[own, sl]], axis=0)
                    vals = jnp.concatenate([vp_ref[:, sl], vc_ref[own, sl]], axis=0)
                else:
                    both = slice((i - 1) * ATT_BLOCK, (i + 1) * ATT_BLOCK)
                    keys, vals = kc_ref[both, sl], vc_ref[both, sl]
                s = jnp.where(mask, _dot_nt(q_ref[own, sl], keys) * scale, -jnp.inf)
                m = jnp.max(s, axis=-1, keepdims=True)
                p = jnp.exp(s - m)
                den = jnp.sum(p, axis=-1, keepdims=True)
                o_ref[own, sl] = _dot(p.astype(BF16), vals) / den
                lse_ref[own, sl] = jnp.broadcast_to(m + jnp.log(den), (ATT_BLOCK, ATT_D))

    return pl.pallas_call(
        body, name=f"att_fwd_{g}", grid=(t // rows,), in_specs=[cur, cur, prev, cur, prev], out_specs=[cur, cur],
        out_shape=[SDS((t, w), F32), SDS((t, w), F32)],
        compiler_params=_cparams(("arbitrary",)))(q, k, k, v, v)


def _att_bwd(q, k, v, do, lse, dlt, g, seq):
    t, w = q.shape
    nblk = t // ATT_BLOCK
    rows = ATT_QB * ATT_BLOCK
    nbs = seq // ATT_DILATIONS[g] // ATT_BLOCK
    scale = ATT_D ** -0.5
    cur = pl.BlockSpec((rows, w), lambda n: (n, 0))
    nxt = pl.BlockSpec((ATT_BLOCK, w), lambda n: (jnp.minimum((n + 1) * ATT_QB, nblk - 1), 0))

    def body(qc_ref, qn_ref, k_ref, v_ref, doc_ref, don_ref, lsec_ref, lsen_ref, dltc_ref, dltn_ref,
             dq_ref, dk_ref, dv_ref, carry):
        n = pl.program_id(0)

        @pl.when(n == 0)
        def _():
            carry[...] = jnp.zeros_like(carry)

        mcur, mprev = _att_masks()

        def pair(cur_ref, nxt_ref, i, sl):
            if i + 1 < ATT_QB:
                return cur_ref[i * ATT_BLOCK:(i + 2) * ATT_BLOCK, sl]
            return jnp.concatenate([cur_ref[i * ATT_BLOCK:, sl], nxt_ref[:, sl]], axis=0)

        for h in range(ATT_SLOTS):
            sl = slice(h * ATT_D, (h + 1) * ATT_D)
            from_prev = carry[:, sl]
            for i in range(ATT_QB):
                blk = n * ATT_QB + i
                has_next = (((blk + 1) % nbs) != 0) & (blk + 1 < nblk)
                mask = jnp.concatenate([mcur, mprev & has_next], axis=0)
                own = slice(i * ATT_BLOCK, (i + 1) * ATT_BLOCK)
                kh, vh = k_ref[own, sl], v_ref[own, sl]
                qs, dos = pair(qc_ref, qn_ref, i, sl), pair(doc_ref, don_ref, i, sl)
                lse, dlt = pair(lsec_ref, lsen_ref, i, sl), pair(dltc_ref, dltn_ref, i, sl)
                p = jnp.where(mask, jnp.exp(_dot_nt(qs, kh) * scale - lse), 0.0)
                ds = (p * (_dot_nt(dos, vh) - dlt) * scale).astype(BF16)
                dqs = _dot(ds, kh)
                dq_ref[own, sl] = (from_prev + dqs[:ATT_BLOCK]).astype(dq_ref.dtype)
                from_prev = dqs[ATT_BLOCK:]
                dk_ref[own, sl] = _dot_tn(ds, qs).astype(dk_ref.dtype)
                dv_ref[own, sl] = _dot_tn(p.astype(BF16), dos).astype(dv_ref.dtype)
            carry[:, sl] = from_prev

    return pl.pallas_call(
        body, name=f"att_bwd_{g}", grid=(t // rows,), in_specs=[cur, nxt, cur, cur, cur, nxt, cur, nxt, cur, nxt],
        out_specs=[cur, cur, cur], out_shape=[SDS((t, w), BF16)] * 3,
        scratch_shapes=[pltpu.VMEM((ATT_BLOCK, w), F32)],
        compiler_params=_cparams(("arbitrary",)))(q, q, k, v, do, do, lse, lse, dlt, dlt)


def _merge_weights(ls):
    m = jnp.maximum(jnp.maximum(ls[0], ls[1]), ls[2])
    es = [jnp.exp(v - m) for v in ls]
    den = es[0] + es[1] + es[2]
    return [e / den for e in es]


def _merge_fwd(o, lse, nb, seq):
    t = o[0].shape[0]
    tm = ATT_TILE
    mtiles = seq // tm
    ng = len(ATT_DILATIONS)

    def body(*refs):
        o_refs, l_refs, out_ref, scs = refs[:ng], refs[ng:2 * ng], refs[2 * ng], refs[2 * ng + 1:]
        for slot in range(ATT_SLOTS):
            lanes = slice(slot * ATT_D, (slot + 1) * ATT_D)
            ov = [_from_strided(o_refs[g], lanes, r, scs[2 * g]) for g, r in enumerate(ATT_DILATIONS)]
            ws = _merge_weights([_from_strided(l_refs[g], lanes, r, scs[2 * g + 1])
                                 for g, r in enumerate(ATT_DILATIONS)])
            out_ref[:, lanes] = (ws[0] * ov[0] + ws[1] * ov[1] + ws[2] * ov[2]).astype(out_ref.dtype)

    views = [a.reshape(nb, r, mtiles, tm // r, ATT_W) for grp in (o, lse) for a, r in zip(grp, ATT_DILATIONS)]
    return pl.pallas_call(
        body, name="att_merge_fwd", grid=(t // tm,),
        in_specs=[_strided_spec(r, mtiles) for _ in range(2) for r in ATT_DILATIONS],
        out_specs=_rs(tm, ATT_W), out_shape=SDS((t, ATT_W), BF16),
        scratch_shapes=[pltpu.VMEM((tm, ATT_D), F32)] * (2 * ng), compiler_params=_cparams(("arbitrary",)))(*views)


def _merge_bwd(o, lse, datt, nb, seq):
    t = o[0].shape[0]
    tm = ATT_TILE
    mtiles = seq // tm
    ng = len(ATT_DILATIONS)

    def body(*refs):
        o_refs, l_refs, d_ref = refs[:ng], refs[ng:2 * ng], refs[2 * ng]
        do_refs, dlt_refs = refs[2 * ng + 1:3 * ng + 1], refs[3 * ng + 1:4 * ng + 1]
        scs = refs[4 * ng + 1:]
        for slot in range(ATT_SLOTS):
            lanes = slice(slot * ATT_D, (slot + 1) * ATT_D)
            ov = [_from_strided(o_refs[g], lanes, r, scs[2 * g]) for g, r in enumerate(ATT_DILATIONS)]
            ws = _merge_weights([_from_strided(l_refs[g], lanes, r, scs[2 * g + 1])
                                 for g, r in enumerate(ATT_DILATIONS)])
            dv = d_ref[:, lanes]
            att = ws[0] * ov[0] + ws[1] * ov[1] + ws[2] * ov[2]
            dot = jnp.broadcast_to(jnp.sum(dv * att, axis=-1, keepdims=True), (tm, ATT_D))
            for g, r in enumerate(ATT_DILATIONS):
                _to_strided(ws[g] * dv, do_refs[g], lanes, r, scs[2 * ng])
                _to_strided(ws[g] * dot, dlt_refs[g], lanes, r, scs[2 * ng + 1])

    views = [a.reshape(nb, r, mtiles, tm // r, ATT_W) for grp in (o, lse) for a, r in zip(grp, ATT_DILATIONS)]
    outs = pl.pallas_call(
        body, name="att_merge_bwd", grid=(t // tm,),
        in_specs=[_strided_spec(r, mtiles) for _ in range(2) for r in ATT_DILATIONS] + [_rs(tm, ATT_W)],
        out_specs=[_strided_spec(r, mtiles) for _ in range(2) for r in ATT_DILATIONS],
        out_shape=[_strided_shape(nb, r, mtiles, dt) for dt in (BF16, F32) for r in ATT_DILATIONS],
        scratch_shapes=[pltpu.VMEM((tm, ATT_D), F32)] * (2 * ng + 2), compiler_params=_cparams(("arbitrary",)))(
            *views, datt)
    flat = [a.reshape(t, ATT_W) for a in outs]
    return flat[:ng], flat[ng:]


def _branch_gates(rows, fulls):
    return (_sigmoid(rows[0][...].astype(F32) + fulls[0][...]), _sigmoid(rows[1][...].astype(F32) + fulls[1][...]))


def _mix_fwd_epilogue(y_att, rows, fulls):
    g0, g1 = _branch_gates(rows, fulls)
    return [y_att, g0 * rows[2][...].astype(F32) + g1 * y_att]


def _mix_bwd_epilogue(dm, rows, fulls):
    g0, g1 = _branch_gates(rows, fulls)
    dg = jnp.concatenate([dm * rows[2][...].astype(F32) * g0 * (1.0 - g0),
                          dm * rows[3][...].astype(F32) * g1 * (1.0 - g1)], axis=-1)
    return [dm * g0, dm * g1, dg, _colsum(dg)]


FFN_TM = 512


def _ffn_in(h2, wg_t, wu_t):
    t, d = h2.shape
    f = wg_t.shape[0]
    tm, tn = FFN_TM, _pick(f, 1536)

    def body(a_ref, g_ref, u_ref, gt_ref, up_ref, act_ref):
        a = a_ref[...]
        gt = _dot_nt(a, g_ref[...])
        up = _dot_nt(a, u_ref[...])
        gt_ref[...] = gt.astype(BF16)
        up_ref[...] = up.astype(BF16)
        act_ref[...] = (gt * _sigmoid(gt) * up).astype(BF16)

    a_spec = pl.BlockSpec((tm, d), lambda j, i: (i, 0))
    w_spec = pl.BlockSpec((tn, d), lambda j, i: (j, 0))
    o_spec = pl.BlockSpec((tm, tn), lambda j, i: (i, j))
    return pl.pallas_call(
        body, name="ffn_in", grid=(f // tn, t // tm), in_specs=[a_spec, w_spec, w_spec],
        out_specs=[o_spec] * 3, out_shape=[SDS((t, f), BF16)] * 3,
        compiler_params=_cparams(("parallel", "arbitrary")))(h2, wg_t, wu_t)


def _ffn_bwd_in(dx2, w_down, gt, up):
    t, d = dx2.shape
    f = w_down.shape[0]
    tm, tn = FFN_TM, _pick(f, 1536)

    def body(a_ref, w_ref, g_ref, u_ref, dgt_ref, dup_ref):
        dv = _dot_nt(a_ref[...], w_ref[...])
        gv = g_ref[...].astype(F32)
        sg = _sigmoid(gv)
        dgt_ref[...] = (dv * u_ref[...].astype(F32) * sg * (1.0 + gv * (1.0 - sg))).astype(BF16)
        dup_ref[...] = (dv * gv * sg).astype(BF16)

    a_spec = pl.BlockSpec((tm, d), lambda j, i: (i, 0))
    w_spec = pl.BlockSpec((tn, d), lambda j, i: (j, 0))
    o_spec = pl.BlockSpec((tm, tn), lambda j, i: (i, j))
    return pl.pallas_call(
        body, name="ffn_bwd_in", grid=(f // tn, t // tm), in_specs=[a_spec, w_spec, o_spec, o_spec],
        out_specs=[o_spec] * 2, out_shape=[SDS((t, f), BF16)] * 2,
        compiler_params=_cparams(("parallel", "arbitrary")))(dx2, w_down, gt, up)


def _adamw(w, g, m, v, name):
    r, c = w.shape[-2:]
    lead = w.ndim - 2
    tr = _row_tile(r, max(8, 400_000 // c))
    c1 = 1.0 / (1.0 - ADAM_B1 ** ADAM_STEP)
    c2 = 1.0 / (1.0 - ADAM_B2 ** ADAM_STEP)

    def fn(i, w_ref, g_ref, m_ref, v_ref):
        gv = g_ref[...]
        mn = ADAM_B1 * m_ref[...] + (1.0 - ADAM_B1) * gv
        vn = ADAM_B2 * v_ref[...] + (1.0 - ADAM_B2) * (gv * gv)
        delta = -ADAM_LR * ((mn * c1) / (jnp.sqrt(vn * c2) + ADAM_EPS) + ADAM_WD * w_ref[...])
        return [delta, mn, vn]

    spec = pl.BlockSpec((None,) * lead + (tr, c), lambda i: (0,) * lead + (i, 0))
    return _rw(name, fn, r // tr, [(w, spec), (g, spec), (m, spec), (v, spec)], [(SDS(w.shape, F32), spec)] * 3)


ANY = pl.BlockSpec(memory_space=pl.ANY)


def _place():
    x, y, c = lax.axis_index("x"), lax.axis_index("y"), lax.axis_index("c")
    chips = [(1 - x, y), (x, 1 - y), (1 - x, 1 - y)]
    return x, y, c, chips


def _remote(src, dst, ssem, rsem, to):
    return pltpu.make_async_remote_copy(src_ref=src, dst_ref=dst, send_sem=ssem, recv_sem=rsem, device_id=to,
                                        device_id_type=MESH)


def _copy_through_vmem(src, dst, buf, isem, osem):
    chunk = buf.shape[1]
    n = src.shape[0] // chunk
    load = lambda k: pltpu.make_async_copy(src.at[pl.ds(k * chunk, chunk)], buf.at[k % 2], isem.at[k % 2])
    store = lambda k: pltpu.make_async_copy(buf.at[k % 2], dst.at[pl.ds(k * chunk, chunk)], osem.at[k % 2])
    load(0).start()
    for k in range(n):
        load(k).wait()
        if k + 1 < n:
            if k >= 1:
                store(k - 1).wait()
            load(k + 1).start()
        store(k).start()
    if n >= 2:
        store(n - 2).wait()
    store(n - 1).wait()


def _copy_scratch(rows, width, dtype):
    chunk = _row_tile(rows, 512)
    return [pltpu.VMEM((2, chunk, width), dtype), pltpu.SemaphoreType.DMA((2,)), pltpu.SemaphoreType.DMA((2,))]


def _gather_weights(wp):
    def body(w_ref, out_ref, ssem, rsem, buf, isem, osem):
        x, y, c, chips = _place()
        me = 2 * x + y
        sib = (x, y, 1 - c)
        first = [_remote(w_ref.at[c], out_ref.at[me, c], ssem.at[j], rsem.at[j], (*chip, c))
                 for j, chip in enumerate(chips)]
        for cp in first:
            cp.start()
        for half in range(2):
            _copy_through_vmem(w_ref.at[half], out_ref.at[me, half], buf, isem, osem)
        passed = []
        for j, chip in enumerate(chips):
            ci = 2 * chip[0] + chip[1]
            _remote(w_ref.at[c], out_ref.at[ci, c], ssem.at[j], rsem.at[j], (*chip, c)).wait_recv()
            cp = _remote(out_ref.at[ci, c], out_ref.at[ci, c], ssem.at[3 + j], rsem.at[3 + j], sib)
            cp.start()
            passed.append(cp)
        for j, chip in enumerate(chips):
            ci = 2 * chip[0] + chip[1]
            _remote(out_ref.at[ci, 1 - c], out_ref.at[ci, 1 - c], ssem.at[3 + j], rsem.at[3 + j], sib).wait_recv()
        for cp in first + passed:
            cp.wait_send()

    return pl.pallas_call(
        body, name="gather_weights", in_specs=[ANY], out_specs=ANY,
        out_shape=SDS((N_CHIPS,) + wp.shape, wp.dtype),
        scratch_shapes=[pltpu.SemaphoreType.DMA((6,)), pltpu.SemaphoreType.DMA((6,))]
        + _copy_scratch(wp.shape[1], wp.shape[2], wp.dtype),
        compiler_params=pltpu.CompilerParams(has_side_effects=True))(wp)


def _swap_halves(g2, tag):
    def body(g_ref, out_ref, ssem, rsem):
        x, y, c, _ = _place()
        cp = _remote(g_ref.at[1 - c], out_ref, ssem, rsem, (x, y, 1 - c))
        cp.start()
        cp.wait()

    return pl.pallas_call(
        body, name="swap_halves_" + tag, in_specs=[ANY], out_specs=ANY, out_shape=SDS(g2.shape[1:], g2.dtype),
        scratch_shapes=[pltpu.SemaphoreType.DMA(()), pltpu.SemaphoreType.DMA(())],
        compiler_params=pltpu.CompilerParams(has_side_effects=True))(g2)


def _add_own_half(g2, other, c, tag):
    _, nch, rows, w = g2.shape
    tr = _row_tile(rows, 512)
    nr = rows // tr

    def body(c_ref, a_ref, b_ref, o_ref):
        o_ref[...] = (a_ref[...].astype(F32) + b_ref[...].astype(F32)).astype(o_ref.dtype)

    grid_spec = pltpu.PrefetchScalarGridSpec(
        num_scalar_prefetch=1, grid=(nch, nr),
        in_specs=[pl.BlockSpec((None, None, tr, w), lambda k, i, c_ref: (c_ref[0], k, i, 0)),
                  pl.BlockSpec((None, tr, w), lambda k, i, c_ref: (k, i, 0))],
        out_specs=pl.BlockSpec((None, tr, w), lambda k, i, c_ref: (k, i, 0)))
    return pl.pallas_call(
        body, name="add_own_half_" + tag, grid_spec=grid_spec, out_shape=SDS(other.shape, other.dtype),
        compiler_params=_cparams(("arbitrary", "arbitrary")))(jnp.reshape(c, (1,)).astype(jnp.int32), g2, other)


def _sum_chips(q, tag):
    nch, rows, w = q.shape
    tr = _row_tile(rows, 512)

    def fn(i, q_ref):
        return [((q_ref[0].astype(F32) + q_ref[1].astype(F32)) + q_ref[2].astype(F32)) + q_ref[3].astype(F32)]

    return _rw("sum_chips_" + tag, fn, rows // tr, [(q, pl.BlockSpec((nch, tr, w), lambda i: (0, i, 0)))],
               [(SDS((rows, w), F32), _rs(tr, w))])[0]


def _chip_copies(src_ref, dst_ref, ssem, rsem, outgoing):
    x, y, c, chips = _place()
    me = 2 * x + y
    cps = []
    for j, chip in enumerate(chips):
        ci = 2 * chip[0] + chip[1]
        cps.append(_remote(src_ref.at[ci], dst_ref.at[me if outgoing else ci], ssem.at[j], rsem.at[j], (*chip, c)))
    return cps, me


def _scatter_side(p):
    def first(ins, outs, scr):
        cps, me = _chip_copies(ins[0], outs[0], scr[0], scr[1], True)
        for cp in cps:
            cp.start()
        pltpu.make_async_copy(ins[0].at[me], outs[0].at[me], scr[2]).start()

    def last(ins, outs, scr):
        for cp in _chip_copies(ins[0], outs[0], scr[0], scr[1], False)[0]:
            cp.wait_recv()
        cps, me = _chip_copies(ins[0], outs[0], scr[0], scr[1], True)
        for cp in cps:
            cp.wait_send()
        pltpu.make_async_copy(ins[0].at[me], outs[0].at[me], scr[2]).wait()

    return _Side((p,), (SDS(p.shape, p.dtype),),
                 (pltpu.SemaphoreType.DMA((3,)), pltpu.SemaphoreType.DMA((3,)), pltpu.SemaphoreType.DMA(())),
                 first, None, last)


def _gather_copies(w_ref, out_ref, ssem, rsem):
    x, y, c, chips = _place()
    me = 2 * x + y
    sib = (x, y, 1 - c)
    sends, arrivals, forwards, from_sib = [], [], [], []
    for j, chip in enumerate(chips):
        ci = 2 * chip[0] + chip[1]
        sends.append(_remote(w_ref.at[c], out_ref.at[me, c], ssem.at[j], rsem.at[j], (*chip, c)))
        arrivals.append(_remote(w_ref.at[c], out_ref.at[ci, c], ssem.at[j], rsem.at[j], (*chip, c)))
        forwards.append(_remote(out_ref.at[ci, c], out_ref.at[ci, c], ssem.at[3 + j], rsem.at[3 + j], sib))
        from_sib.append(_remote(out_ref.at[ci, 1 - c], out_ref.at[ci, 1 - c], ssem.at[3 + j], rsem.at[3 + j], sib))
    return sends, arrivals, forwards, from_sib, me


def _gather_side(wp):
    def first(ins, outs, scr):
        sends, _, _, _, me = _gather_copies(ins[0], outs[0], scr[0], scr[1])
        for cp in sends:
            cp.start()
        pltpu.make_async_copy(ins[0], outs[0].at[me], scr[2]).start()

    def mid(ins, outs, scr):
        _, arrivals, forwards, _, _ = _gather_copies(ins[0], outs[0], scr[0], scr[1])
        for arrived, forward in zip(arrivals, forwards):
            arrived.wait_recv()
            forward.start()

    def last(ins, outs, scr):
        sends, _, forwards, from_sib, me = _gather_copies(ins[0], outs[0], scr[0], scr[1])
        for cp in from_sib:
            cp.wait_recv()
        for cp in sends + forwards:
            cp.wait_send()
        pltpu.make_async_copy(ins[0], outs[0].at[me], scr[2]).wait()

    return _Side((wp,), (SDS((N_CHIPS,) + wp.shape, wp.dtype),),
                 (pltpu.SemaphoreType.DMA((6,)), pltpu.SemaphoreType.DMA((6,)), pltpu.SemaphoreType.DMA(())),
                 first, mid, last)


def _allreduce_small(v, name):
    rows, w = v.shape
    offsets = [(dx, dy, dc) for dx in (0, 1) for dy in (0, 1) for dc in (0, 1)][1:]

    def body(v_ref, o_ref, buf, ssem, rsem):
        x, y, c, _ = _place()
        flip = lambda p, d: 1 - p if d else p
        peers = [(flip(x, dx), flip(y, dy), flip(c, dc)) for dx, dy, dc in offsets]
        index = lambda p: 4 * p[0] + 2 * p[1] + p[2]
        me = index((x, y, c))
        buf[me] = v_ref[...]
        sent = [_remote(v_ref, buf.at[me], ssem.at[q], rsem.at[q], p) for q, p in enumerate(peers)]
        for cp in sent:
            cp.start()
        for q, p in enumerate(peers):
            _remote(v_ref, buf.at[index(p)], ssem.at[q], rsem.at[q], p).wait_recv()
        for cp in sent:
            cp.wait_send()
        acc = buf[0]
        for q in range(1, 8):
            acc = acc + buf[q]
        o_ref[...] = acc

    vm = pl.BlockSpec(memory_space=pltpu.VMEM)
    return pl.pallas_call(
        body, name=name, in_specs=[vm], out_specs=vm, out_shape=SDS((rows, w), F32),
        scratch_shapes=[pltpu.VMEM((8, rows, w), F32), pltpu.SemaphoreType.DMA((7,)), pltpu.SemaphoreType.DMA((7,))],
        compiler_params=pltpu.CompilerParams(has_side_effects=True))(v)


def _join_halves(h, tag):
    def body(h_ref, out_ref, ssem, rsem, buf, isem, osem):
        x, y, c, _ = _place()
        cp = _remote(h_ref, out_ref.at[c], ssem, rsem, (x, y, 1 - c))
        cp.start()
        _copy_through_vmem(h_ref, out_ref.at[c], buf, isem, osem)
        _remote(h_ref, out_ref.at[1 - c], ssem, rsem, (x, y, 1 - c)).wait_recv()
        cp.wait_send()

    return pl.pallas_call(
        body, name="join_halves_" + tag, in_specs=[ANY], out_specs=ANY, out_shape=SDS((2,) + h.shape, h.dtype),
        scratch_shapes=[pltpu.SemaphoreType.DMA(()), pltpu.SemaphoreType.DMA(())]
        + _copy_scratch(h.shape[0], h.shape[1], h.dtype),
        compiler_params=pltpu.CompilerParams(has_side_effects=True))(h)


PACK_W = 1024
SHARDED = ("w_in", "w_ffn_gate", "w_ffn_up", "w_ssm_out", "w_att_out", "w_mix_out", "w_ffn_down")
COL_SHARDED = ("w_in", "w_ffn_gate", "w_ffn_up", "w_att_out")
SMALL = ("norm_mix", "b_gate", "conv_b", "dt_bias", "a_log", "d_skip", "ssm_norm", "norm_ffn", "norm_final")


PACK_ROW_ALIGN = 16


def _rows(n):
    return -(-n // (PACK_W * PACK_ROW_ALIGN)) * PACK_ROW_ALIGN


def _pack_rows(parts, total_rows):
    rows = []
    for p in parts:
        size = int(p.size)
        if size % PACK_W:
            p = jnp.pad(p.reshape(-1), (0, PACK_W - size % PACK_W))
        p = p.reshape(-1, PACK_W)
        rows.append(jnp.pad(p, ((0, _rows(size) - p.shape[0]), (0, 0))))
    used = sum(r.shape[0] for r in rows)
    if total_rows > used:
        rows.append(jnp.zeros((total_rows - used, PACK_W), rows[0].dtype))
    return jnp.concatenate(rows, axis=0)


def _padded_rows(n):
    return -(-n // 32) * 32


def _wire_name(name):
    return name + "_t" if name in COL_SHARDED else name


def _wire_shard(w, name):
    return w.T if name in COL_SHARDED else w


def _group_major(a, axis):
    gw = D_INNER // N_GROUPS
    take = lambda lo, n: lax.slice_in_dim(a, lo, lo + n, axis=axis)
    parts = []
    for g in range(N_GROUPS):
        parts += [take(g * gw, gw), take(D_INNER + g * D_STATE, D_STATE),
                  take(D_INNER + N_GROUPS * D_STATE + g * D_STATE, D_STATE)]
    return jnp.concatenate(parts, axis=axis)


def _group_major_inv(a, axis):
    gw = D_INNER // N_GROUPS
    take = lambda lo, n: lax.slice_in_dim(a, lo, lo + n, axis=axis)
    xs = [take(g * GROUP_W, gw) for g in range(N_GROUPS)]
    bs = [take(g * GROUP_W + gw, D_STATE) for g in range(N_GROUPS)]
    cs = [take(g * GROUP_W + gw + D_STATE, D_STATE) for g in range(N_GROUPS)]
    return jnp.concatenate(xs + bs + cs, axis=axis)


LATE = ("w_ffn_gate_t", "w_ffn_up_t", "w_ssm_out", "w_att_out_t", "w_mix_out", "w_ffn_down")


class _Overlap(NamedTuple):
    gather_side: _Side
    late_weights: Callable
    scatter_side: Callable
    scatter_in: Callable


def _local_step(x, target, wts, overlap):
    nb, seq, d = x.shape
    t = nb * seq
    x = x.reshape(t, d)
    target = target.reshape(t, d)
    hg = HEADS_PER_GROUP

    o1, o2, o3, o4 = D_INNER, D_INNER + CONV_DIM, D_INNER + CONV_DIM + N_HEADS, D_INNER + CONV_DIM + N_HEADS + QKV_DIM
    n_in = o4 + 2 * D_MODEL

    def in_rows(lo, hi):
        per = n_in // N_CHIPS
        parts = [wts["w_in_t"][k, max(lo, k * per) - k * per:min(hi, (k + 1) * per) - k * per]
                 for k in range(N_CHIPS) if max(lo, k * per) < min(hi, (k + 1) * per)]
        return parts[0] if len(parts) == 1 else jnp.concatenate(parts, axis=0)

    w_z = in_rows(0, o1)
    w_xbc = _group_major(in_rows(o1, o2), 0)
    w_dt = jnp.pad(in_rows(o2, o3), ((0, DT_PAD - N_HEADS), (0, 0)))
    w_qkv = in_rows(o3, o4)
    w_gate = in_rows(o4, n_in)
    conv_w = _group_major(wts["conv_w"], 1)
    conv_b = _group_major(wts["conv_b"], 1)

    def per_group_row(p):
        return p.reshape(N_GROUPS, 1, hg)

    def per_group_col(p):
        return p.reshape(N_GROUPS, hg, 1)

    a_neg = -jnp.exp(wts["a_log"])
    bias_r, bias_c = per_group_row(wts["dt_bias"]), per_group_col(wts["dt_bias"])
    a_r, a_c = per_group_row(a_neg), per_group_col(a_neg)
    dskip_r = per_group_row(wts["d_skip"])
    cos, sin = _rope_tables(seq)

    h = _rms_fwd(x, wts["norm_mix"], "rms_mix_fwd")
    z = _mm(h, w_z, "nt", BF16, "proj_z")
    xbc = _mm(h, w_xbc, "nt", F32, "proj_xbc")
    dt_raw = _mm(h, w_dt, "nt", F32, "proj_dt")
    qkv = _mm(h, w_qkv, "nt", BF16, "proj_qkv")
    gate_logits = _mm(h, w_gate, "nt", BF16, "proj_gate")

    xc = _conv_fwd(xbc, conv_w, conv_b, seq)
    dtr = dt_raw[:, :N_HEADS].reshape(t, N_GROUPS, hg).transpose(1, 0, 2)
    dtrt = dt_raw[:, :N_HEADS].reshape(nb, seq, N_GROUPS, hg).transpose(2, 0, 3, 1)
    y, states, *gathered = _ssd_fwd(xc, dtr, dtrt, bias_r, bias_c, a_r, a_c, dskip_r, nb, seq, overlap.gather_side)
    wts = {**wts, **overlap.late_weights(gathered)}
    yn = _gate_norm_fwd(y, z, wts["ssm_norm"])
    y_ssm = _mm(yn, wts["w_ssm_out"], "nn", BF16, "ssm_out")

    groups = range(len(ATT_DILATIONS))
    qg, kg, vg = _rope_fwd(qkv, cos, sin, nb, seq)
    o_g, lse_g = zip(*[_att_fwd(qg[i], kg[i], vg[i], i, seq) for i in groups])
    att = _merge_fwd(o_g, lse_g, nb, seq)
    gate_halves = [(gate_logits, d, 0), (gate_logits, d, 1)]
    b_gate_halves = [(wts["b_gate"], d, 0), (wts["b_gate"], d, 1)]
    y_att, mixed = _mm_fused(att, wts["w_att_out_t"], "nt", "att_out_mix", 512, _mix_fwd_epilogue,
                             gate_halves + [(y_ssm, d, 0)], b_gate_halves, [(d, BF16), (d, BF16)])

    def residual_and_norm(xv, rows, fulls):
        return [xv, xv * lax.rsqrt(jnp.mean(xv * xv, axis=-1, keepdims=True) + EPS) * fulls[0][...]]

    x1, h2 = _mm_fused(mixed, wts["w_mix_out"], "nn", "mix_out_norm", 512, residual_and_norm, [],
                       [(wts["norm_ffn"], d, 0)], [(d, F32), (d, BF16)], add=x)
    gt, up, act = _ffn_in(h2, wts["w_ffn_gate_t"], wts["w_ffn_up_t"])

    g = {}
    dx2, dx2_b, g["norm_final"], loss = _mm_fused(
        act, wts["w_ffn_down"], "nn", "ffn_down_loss", 512,
        lambda x2, rows, fulls: _final_values(x2, rows[0][...], fulls[0][...]),
        [(target, d, 0)], [(wts["norm_final"].reshape(1, d), d, 0)], [(d, F32), (d, BF16), (d, F32), (1, F32)],
        n_acc=2, add=x1)
    g["w_ffn_down"] = _mm(act, dx2_b, "tn", BF16, "g_ffn_down")
    dgt, dup = _ffn_bwd_in(dx2_b, wts["w_ffn_down"], gt, up)
    g["w_ffn_gate_t"] = _mm(dgt, h2, "tn", BF16, "g_ffn_gate")
    g["w_ffn_up_t"] = _mm(dup, h2, "tn", BF16, "g_ffn_up")
    dh2 = _mm(dgt, wts["w_ffn_gate_t"], "nn", F32, "d_h2_gate")
    dx1, dx1_b, g["norm_ffn"] = _mm_fused(
        dup, wts["w_ffn_up_t"], "nn", "d_h2_up_norm", 512,
        lambda dh, rows, fulls: _rms_bwd_values(rows[0][...], dh, fulls[0][...], rows[1][...]),
        [(x1, d, 0), (dx2, d, 0)], [(wts["norm_ffn"], d, 0)], [(d, F32), (d, BF16), (d, F32)], n_acc=1, add=dh2)

    g["w_mix_out"] = _mm(mixed, dx1_b, "tn", BF16, "g_mix_out")
    dy_ssm, dy_att, dgate, g["b_gate"] = _mm_fused(
        dx1_b, wts["w_mix_out"], "nt", "d_mixed_gates", 512, _mix_bwd_epilogue,
        gate_halves + [(y_ssm, d, 0), (y_att, d, 0)], b_gate_halves,
        [(d, BF16), (d, BF16), (2 * d, BF16), (2 * d, F32)], n_acc=1)

    datt = _mm(dy_att, wts["w_att_out_t"], "nn", F32, "d_att")
    g["w_att_out_t"] = _mm(dy_att, att, "tn", BF16, "g_att_out")
    do_g, dlt_g = _merge_bwd(o_g, lse_g, datt, nb, seq)
    dq_g, dk_g, dv_g = zip(*[_att_bwd(qg[i], kg[i], vg[i], do_g[i], lse_g[i], dlt_g[i], i, seq) for i in groups])
    dqkv = _rope_bwd(dq_g, dk_g, dv_g, cos, sin, nb, seq)

    g["w_ssm_out"] = _mm(yn, dy_ssm, "tn", BF16, "g_ssm_out")
    dy, dz, g["ssm_norm"] = _mm_fused(
        dy_ssm, wts["w_ssm_out"], "nt", "d_yn_norm", 256, _gate_norm_bwd_epilogue,
        [(y, D_INNER, 0), (z, D_INNER, 0)], [(wts["ssm_norm"], D_INNER, 0)],
        [(D_INNER, F32), (D_INNER, BF16), (D_INNER, F32)], n_acc=1)
    side = overlap.scatter_side({n: g.pop(n) for n in LATE})
    dxc, ddtr, g_bias, g_alog, g_dskip, *scattered = _ssd_bwd(xc, dtr, dtrt, bias_r, bias_c, a_r, a_c, dskip_r,
                                                               states, dy, nb, seq, side)
    g["dt_bias"] = g_bias.reshape(1, N_HEADS)
    g["a_log"] = g_alog.reshape(1, N_HEADS)
    g["d_skip"] = g_dskip.reshape(1, N_HEADS)
    dpre, g_conv_w, g_conv_b = _conv_bwd_pre(xbc, conv_w, conv_b, dxc, seq)
    g["conv_w"] = _group_major_inv(g_conv_w, 1)
    g["conv_b"] = _group_major_inv(g_conv_b, 1)
    dxbc = _conv_bwd_in(dpre, conv_w, seq)
    ddt = jnp.pad(ddtr.transpose(1, 0, 2).reshape(t, N_HEADS), ((0, 0), (0, DT_PAD - N_HEADS))).astype(BF16)

    g_in_t = jnp.concatenate([
        _mm(dz, h, "tn", BF16, "g_in_z"),
        _group_major_inv(_mm(dxbc, h, "tn", BF16, "g_in_xbc"), 0),
        _mm(ddt, h, "tn", BF16, "g_in_dt")[:N_HEADS],
        _mm(dqkv, h, "tn", BF16, "g_in_qkv"),
        _mm(dgate, h, "tn", BF16, "g_in_gate")], axis=0)
    dh = _mm(dz, w_z, "nn", F32, "d_h_z")
    dh = _mm(dxbc, w_xbc, "nn", F32, "d_h_xbc", add=dh)
    dh = _mm(ddt, w_dt, "nn", F32, "d_h_dt", add=dh)
    dh = _mm(dgate, w_gate, "nn", F32, "d_h_gate", add=dh)
    dh, *scattered_in = _mm(dqkv, w_qkv, "nn", F32, "d_h_qkv", add=dh, side=overlap.scatter_in({"w_in_t": g_in_t}))
    dx, _, g["norm_mix"] = _rms_bwd(x, dh, wts["norm_mix"], dx1, "rms_mix_bwd")
    return loss[0, 0], dx.reshape(nb, seq, d), g, scattered, scattered_in


def kernel(x, norm_mix, w_in, b_gate, conv_w, conv_b, dt_bias, a_log, d_skip, ssm_norm, w_ssm_out, w_att_out, w_mix_out, norm_ffn, w_ffn_gate, w_ffn_up, w_ffn_down, norm_final, loss_target, m_norm_mix, m_w_in, m_b_gate, m_conv_w, m_conv_b, m_dt_bias, m_a_log, m_d_skip, m_ssm_norm, m_w_ssm_out, m_w_att_out, m_w_mix_out, m_norm_ffn, m_w_ffn_gate, m_w_ffn_up, m_w_ffn_down, m_norm_final, v_norm_mix, v_w_in, v_b_gate, v_conv_w, v_conv_b, v_dt_bias, v_a_log, v_d_skip, v_ssm_norm, v_w_ssm_out, v_w_att_out, v_w_mix_out, v_norm_ffn, v_w_ffn_gate, v_w_ffn_up, v_w_ffn_down, v_norm_final):
    names = ("norm_mix", "w_in", "b_gate", "conv_w", "conv_b", "dt_bias", "a_log", "d_skip", "ssm_norm", "w_ssm_out",
             "w_att_out", "w_mix_out", "norm_ffn", "w_ffn_gate", "w_ffn_up", "w_ffn_down", "norm_final")
    w_loc = dict(zip(names, (norm_mix, w_in, b_gate, conv_w, conv_b, dt_bias, a_log, d_skip, ssm_norm, w_ssm_out,
                             w_att_out, w_mix_out, norm_ffn, w_ffn_gate, w_ffn_up, w_ffn_down, norm_final)))
    m_loc = dict(zip(names, (m_norm_mix, m_w_in, m_b_gate, m_conv_w, m_conv_b, m_dt_bias, m_a_log, m_d_skip,
                             m_ssm_norm, m_w_ssm_out, m_w_att_out, m_w_mix_out, m_norm_ffn, m_w_ffn_gate,
                             m_w_ffn_up, m_w_ffn_down, m_norm_final)))
    v_loc = dict(zip(names, (v_norm_mix, v_w_in, v_b_gate, v_conv_w, v_conv_b, v_dt_bias, v_a_log, v_d_skip,
                             v_ssm_norm, v_w_ssm_out, v_w_att_out, v_w_mix_out, v_norm_ffn, v_w_ffn_gate,
                             v_w_ffn_up, v_w_ffn_down, v_norm_final)))
    two_d = lambda a: a.reshape(a.shape[-2:]) if a.ndim >= 2 else a.reshape(1, -1)
    w2 = {n: two_d(a) for n, a in w_loc.items()}
    chip = 2 * lax.axis_index("x") + lax.axis_index("y")
    c = lax.axis_index("c")

    wire_shapes = {n: _wire_shard(w2[n], n).shape for n in SHARDED}
    true_rows = {n: wire_shapes[n][0] * wire_shapes[n][1] // PACK_W for n in SHARDED}
    seg_rows = {n: _rows(wire_shapes[n][0] * wire_shapes[n][1]) for n in SHARDED}
    buckets = {"first": ("w_in",), "late": tuple(n for n in SHARDED if n != "w_in")}
    rows_of = {b: _padded_rows(sum(seg_rows[n] for n in ns)) for b, ns in buckets.items()}

    def pack_shards(b):
        packed = _pack_rows([_wire_shard(w2[n], n).astype(BF16) for n in buckets[b]], rows_of[b])
        return packed.reshape(2, rows_of[b] // 2, PACK_W)

    def unpack_full(gathered, b):
        wg, out, off = gathered.reshape(N_CHIPS, rows_of[b], PACK_W), {}, 0
        for n in buckets[b]:
            rows, cols = wire_shapes[n]
            out[_wire_name(n)] = wg[:, off:off + true_rows[n]].reshape(N_CHIPS * rows, cols)
            off += seg_rows[n]
        return out

    def pack_grads(g, b):
        sections = [_pack_rows([g[_wire_name(n)].reshape(N_CHIPS, true_rows[n], PACK_W)[k] for n in buckets[b]],
                               rows_of[b]) for k in range(N_CHIPS)]
        return jnp.stack(sections).reshape(N_CHIPS, 2, rows_of[b] // 2, PACK_W).transpose(1, 0, 2, 3)

    def chip_sums(g, b):
        g2 = pack_grads(g, b)
        return _add_own_half(g2, _swap_halves(g2, b), c, b)

    def finish(by_source, b):
        reduced = _join_halves(_sum_chips(by_source, b), b).reshape(rows_of[b], PACK_W)
        out, off = {}, 0
        for n in buckets[b]:
            out[n] = reduced[off:off + true_rows[n]].reshape(wire_shapes[n])
            off += seg_rows[n]
        return out

    full = {"w_in_t": _gather_weights(pack_shards("first")).reshape(N_CHIPS, rows_of["first"], PACK_W)}
    for n in SMALL:
        full[n] = w2[n]
    overlap = _Overlap(_gather_side(pack_shards("late")), lambda outs: unpack_full(outs[0], "late"),
                       lambda g: _scatter_side(chip_sums(g, "late")), lambda g: _scatter_side(chip_sums(g, "first")))

    n_conv = w2["conv_w"].shape[1]
    placed = lax.dynamic_update_slice_in_dim(jnp.zeros((CONV_K, N_CHIPS * n_conv), F32), w2["conv_w"], chip * n_conv, 1)
    placed = jnp.where(c == 0, placed, 0.0)
    full["conv_w"] = _allreduce_small(_pack_rows([placed], _rows(int(placed.size))), "gather_conv_w").reshape(
        -1)[:placed.size].reshape(placed.shape)

    loss_sum, grad_x, g_full, scattered, scattered_in = _local_step(x, loss_target, full, overlap)
    loss = lax.psum(loss_sum, ("x", "y", "c"))

    g_shard = {}
    small_names = SMALL + ("conv_w",)
    small_flat = jnp.concatenate([g_full[n].reshape(-1) for n in small_names])
    small = _allreduce_small(_pack_rows([small_flat], _rows(int(small_flat.size))), "allreduce_small").reshape(-1)
    off = 0
    for n in small_names:
        size = int(g_full[n].size)
        g_shard[n] = small[off:off + size].reshape(g_full[n].shape)
        off += size
    g_shard["conv_w"] = lax.dynamic_slice_in_dim(g_shard["conv_w"], chip * n_conv, n_conv, 1)

    g_shard.update(finish(scattered[0], "late"))
    g_shard.update(finish(scattered_in[0], "first"))

    grads, deltas, new_m, new_v = [], [], [], []
    for n in names:
        shape = w_loc[n].shape
        if n in COL_SHARDED:
            view = unview = lambda a: jnp.swapaxes(a, -1, -2)
        else:
            view, unview = ((lambda a: a) if len(shape) >= 2 else two_d), (lambda a: a.reshape(shape))
        gn = g_shard[n].reshape(view(w_loc[n]).shape)
        outs = _adamw(view(w_loc[n]), gn, view(m_loc[n]), view(v_loc[n]), "adamw_" + n)
        for acc, a in zip((grads, deltas, new_m, new_v), (gn, *outs)):
            acc.append(unview(a))
    return (loss, grad_x, *grads, *deltas, *new_m, *new_v)
```

```python
import functools
from typing import Callable, NamedTuple, Optional

import jax
import jax.numpy as jnp
from jax import lax
from jax.experimental import pallas as pl
from jax.experimental.pallas import tpu as pltpu

F32 = jnp.float32
BF16 = jnp.bfloat16
SDS = jax.ShapeDtypeStruct
MESH = pl.DeviceIdType.MESH

D_MODEL = 1024
D_INNER = 2048
N_HEADS = 32
HEAD_P = 64
N_GROUPS = 4
HEADS_PER_GROUP = N_HEADS // N_GROUPS
D_STATE = 128
CONV_K = 4
CHUNK = 128
CONV_DIM = D_INNER + 2 * N_GROUPS * D_STATE
GROUP_W = D_INNER // N_GROUPS + 2 * D_STATE
ATT_HEADS = 12
ATT_D = 128
ATT_SLOTS = 4
ATT_W = ATT_SLOTS * ATT_D
ATT_DILATIONS = (1, 4, 16)
ATT_BLOCK = 128
QKV_DIM = 3 * ATT_HEADS * ATT_D
D_FF = 2816
DT_PAD = 128
ROPE_THETA = 10000.0
EPS = 1e-6
N_CHIPS = 4
LANES = 128

ADAM_LR = 0.001
ADAM_B1 = 0.9
ADAM_B2 = 0.999
ADAM_EPS = 1e-08
ADAM_WD = 0.01
ADAM_STEP = 10

VMEM_LIMIT = 48 * 1024 * 1024


def _cparams(semantics):
    return pltpu.CompilerParams(dimension_semantics=semantics, vmem_limit_bytes=VMEM_LIMIT)


def _pick(n, cap):
    best = None
    for t in range(LANES, min(n, cap) + 1, LANES):
        if n % t == 0:
            best = t
    return best or n


def _row_tile(rows, cap):
    best = None
    for t in range(8, min(rows, cap) + 1, 8):
        if rows % t == 0:
            best = t
    return best or rows


def _sigmoid(x):
    return pl.reciprocal(1.0 + jnp.exp(-x), approx=True)


def _softplus(x):
    return jnp.maximum(x, 0.0) + jnp.log(1.0 + jnp.exp(-jnp.abs(x)))


def _dot(a, b):
    return jnp.dot(a, b, preferred_element_type=F32)


def _dot_nt(a, b):
    return lax.dot_general(a, b, (((1,), (1,)), ((), ())), preferred_element_type=F32)


def _dot_tn(a, b):
    return lax.dot_general(a, b, (((0,), (0,)), ((), ())), preferred_element_type=F32)


def _mm(a, b, mode, out_dtype, name, add=None, side=None):
    if mode == "nn":
        (m, k), (_, n) = a.shape, b.shape
    elif mode == "nt":
        (m, k), (n, _) = a.shape, b.shape
    else:
        (k, m), (_, n) = a.shape, b.shape
    tm, tn = _pick(m, 1536), _pick(n, 2048)
    tk = k if k <= 2048 else _pick(k, 2048)
    nk = k // tk
    dims = {"nn": ((1,), (0,)), "nt": ((1,), (1,)), "tn": ((0,), (0,))}[mode]

    def partial_product(a_ref, b_ref):
        return lax.dot_general(a_ref[...].astype(BF16), b_ref[...].astype(BF16), (dims, ((), ())),
                               preferred_element_type=F32)

    def body(*refs):
        a_ref, b_ref = refs[:2]
        c_ref = refs[2] if add is not None else None
        o_ref = refs[3] if add is not None else refs[2]

        def finish(r):
            if add is not None:
                r = r + c_ref[...].astype(F32)
            o_ref[...] = r.astype(out_dtype)

        if nk == 1:
            finish(partial_product(a_ref, b_ref))
            return
        acc = refs[-1]
        kk = pl.program_id(2)

        @pl.when(kk == 0)
        def _():
            acc[...] = partial_product(a_ref, b_ref)

        @pl.when((kk > 0) & (kk < nk - 1))
        def _():
            acc[...] += partial_product(a_ref, b_ref)

        @pl.when(kk == nk - 1)
        def _():
            finish(acc[...] + partial_product(a_ref, b_ref))

    a_spec = {"nn": pl.BlockSpec((tm, tk), lambda j, i, q: (i, q)),
              "nt": pl.BlockSpec((tm, tk), lambda j, i, q: (i, q)),
              "tn": pl.BlockSpec((tk, tm), lambda j, i, q: (q, i))}[mode]
    b_spec = {"nn": pl.BlockSpec((tk, tn), lambda j, i, q: (q, j)),
              "nt": pl.BlockSpec((tn, tk), lambda j, i, q: (j, q)),
              "tn": pl.BlockSpec((tk, tn), lambda j, i, q: (q, j))}[mode]
    o_spec = pl.BlockSpec((tm, tn), lambda j, i, q: (i, j))
    ins, specs = [a, b], [a_spec, b_spec]
    if add is not None:
        ins.append(add)
        specs.append(o_spec)
    acc = [pltpu.VMEM((tm, tn), F32)] if nk > 1 else []
    grid = (n // tn, m // tm, nk)
    if side is None:
        return pl.pallas_call(
            body, name=name, grid=grid, in_specs=specs, out_specs=o_spec, out_shape=SDS((m, n), out_dtype),
            scratch_shapes=acc, compiler_params=_cparams(("parallel", "parallel", "arbitrary")))(*ins)
    return pl.pallas_call(
        _attach_side(body, len(ins), 1, side, grid), name=name, grid=grid,
        in_specs=specs + [ANY] * len(side.ins), out_specs=[o_spec] + [ANY] * len(side.out_shapes),
        out_shape=[SDS((m, n), out_dtype)] + list(side.out_shapes), scratch_shapes=acc + list(side.scratch),
        compiler_params=_cparams(("arbitrary", "arbitrary", "arbitrary")))(*ins, *side.ins)


def _mm_fused(a, b, mode, name, tm, epilogue, row_ins, full_ins, outs, n_acc=0, add=None):
    (m, k), n = a.shape, (b.shape[1] if mode == "nn" else b.shape[0])
    tk = k if k <= 2048 else _pick(k, 2048)
    nk = k // tk
    dims = {"nn": ((1,), (0,)), "nt": ((1,), (1,))}[mode]
    n_row, n_full, n_out = len(row_ins), len(full_ins), len(outs)

    def partial_product(a_ref, b_ref):
        return lax.dot_general(a_ref[...], b_ref[...], (dims, ((), ())), preferred_element_type=F32)

    def body(*refs):
        a_ref, b_ref = refs[:2]
        pos = 3 if add is not None else 2
        row_refs, full_refs = refs[pos:pos + n_row], refs[pos + n_row:pos + n_row + n_full]
        out_refs = refs[pos + n_row + n_full:pos + n_row + n_full + n_out]
        i, kk = pl.program_id(0), pl.program_id(1)

        def finish(r):
            if add is not None:
                r = r + refs[2][...].astype(F32)
            for q, (o_ref, v) in enumerate(zip(out_refs, epilogue(r, row_refs, full_refs))):
                if q < n_out - n_acc:
                    o_ref[...] = v.astype(o_ref.dtype)
                else:
                    @pl.when(i == 0)
                    def _(o_ref=o_ref, v=v):
                        o_ref[...] = v

                    @pl.when(i > 0)
                    def _(o_ref=o_ref, v=v):
                        o_ref[...] += v

        if nk == 1:
            finish(partial_product(a_ref, b_ref))
            return
        acc = refs[-1]

        @pl.when(kk == 0)
        def _():
            acc[...] = partial_product(a_ref, b_ref)

        @pl.when((kk > 0) & (kk < nk - 1))
        def _():
            acc[...] += partial_product(a_ref, b_ref)

        @pl.when(kk == nk - 1)
        def _():
            finish(acc[...] + partial_product(a_ref, b_ref))

    tile = lambda w, cb: pl.BlockSpec((tm, w), lambda i, q: (i, cb))
    b_spec = (pl.BlockSpec((tk, n), lambda i, q: (q, 0)) if mode == "nn" else pl.BlockSpec((n, tk), lambda i, q: (0, q)))
    specs = [pl.BlockSpec((tm, tk), lambda i, q: (i, q)), b_spec] + ([tile(n, 0)] if add is not None else [])
    specs += [tile(w, cb) for _, w, cb in row_ins]
    vec = lambda w, cb: pl.BlockSpec((1, w), lambda i, q: (0, cb))
    specs += [vec(w, cb) for _, w, cb in full_ins]
    out_specs = [tile(w, 0) for w, _ in outs[:n_out - n_acc]] + [vec(w, 0) for w, _ in outs[n_out - n_acc:]]
    out_shape = [SDS((m, w), dt) for w, dt in outs[:n_out - n_acc]] + [SDS((1, w), F32) for w, _ in outs[n_out - n_acc:]]
    ins = [a, b] + ([add] if add is not None else []) + [x for x, _, _ in row_ins] + [x for x, _, _ in full_ins]
    return pl.pallas_call(
        body, name=name, grid=(m // tm, nk), in_specs=specs, out_specs=out_specs, out_shape=out_shape,
        scratch_shapes=[pltpu.VMEM((tm, n), F32)] if nk > 1 else [],
        compiler_params=_cparams(("arbitrary", "arbitrary")))(*ins)


def _rw(name, fn, nsteps, ins, outs, n_acc=0):
    n_in, n_out = len(ins), len(outs)

    def body(*refs):
        i = pl.program_id(0)
        vals = fn(i, *refs[:n_in])
        for q, (r, v) in enumerate(zip(refs[n_in:], vals)):
            if q < n_out - n_acc:
                r[...] = v.astype(r.dtype)
            else:
                @pl.when(i == 0)
                def _(r=r):
                    r[...] = jnp.zeros_like(r)

                r[...] += v

    return pl.pallas_call(
        body, name=name, grid=(nsteps,), in_specs=[s for _, s in ins], out_specs=[s for _, s in outs],
        out_shape=[o for o, _ in outs], compiler_params=_cparams(("arbitrary",)))(*[a for a, _ in ins])


def _rs(tm, w, cb=0):
    return pl.BlockSpec((tm, w), lambda i: (i, cb))


def _fs(shape):
    nd = len(shape)
    return pl.BlockSpec(shape, lambda i: (0,) * nd)


def _colsum(v):
    return jnp.sum(v, axis=0, keepdims=True)


def _rms_fwd(x, g, name):
    t, d = x.shape
    tm = 512

    def fn(i, x_ref, g_ref):
        xv = x_ref[...]
        r = lax.rsqrt(jnp.mean(xv * xv, axis=-1, keepdims=True) + EPS)
        return [xv * r * g_ref[...]]

    return _rw(name, fn, t // tm, [(x, _rs(tm, d)), (g, _fs((1, d)))], [(SDS((t, d), BF16), _rs(tm, d))])[0]


def _rms_bwd_values(xv, dhv, gv, dres):
    r = lax.rsqrt(jnp.mean(xv * xv, axis=-1, keepdims=True) + EPS)
    xhat = xv * r
    dxhat = dhv * gv
    dx = dres + r * (dxhat - xhat * jnp.mean(dxhat * xhat, axis=-1, keepdims=True))
    return [dx, dx, _colsum(dhv * xhat)]


def _final_values(xv, target, gv):
    d = xv.shape[-1]
    r = lax.rsqrt(jnp.mean(xv * xv, axis=-1, keepdims=True) + EPS)
    xhat = xv * r
    diff = xhat * gv - target
    lsum = 0.5 * jnp.sum(jnp.sum(diff * diff, axis=-1, keepdims=True) * (1.0 / d), axis=0, keepdims=True)
    dy = diff * (1.0 / d)
    dxhat = dy * gv
    dx = r * (dxhat - xhat * jnp.mean(dxhat * xhat, axis=-1, keepdims=True))
    return [dx, dx, _colsum(dy * xhat), lsum]


CONV_TS = 512
CONV_HALO = 8


def _conv_specs(seq, c):
    ts, tc = CONV_TS, GROUP_W
    hb = ts // CONV_HALO
    u_spec = pl.BlockSpec((ts, tc), lambda j, i: (i, j))
    prev_spec = pl.BlockSpec((CONV_HALO, tc), lambda j, i: (jnp.maximum(i * hb - 1, 0), j))
    w_spec = pl.BlockSpec((CONV_K, tc), lambda j, i: (0, j))
    b_spec = pl.BlockSpec((1, tc), lambda j, i: (0, j))
    return u_spec, prev_spec, w_spec, b_spec


CONV_ROWS = 16


CONV_PIECE = 32


def _conv_fill(i, seq, u_ref, prev_ref, ext):
    first = (i % (seq // CONV_TS)) == 0
    ext[0:CONV_HALO, :] = jnp.where(first, 0.0, prev_ref[...])
    ext[CONV_HALO:, :] = u_ref[...]


def _conv_piece(ext, r0, wv, bv):
    lo = r0 + CONV_HALO - CONV_K + 1
    taps = [ext[lo + q:lo + q + CONV_PIECE, :] for q in range(CONV_K)]
    pre = bv
    for q, tap in enumerate(taps):
        pre = pre + wv[q:q + 1] * tap
    return taps, pre


def _conv_fwd(u, w, b, seq):
    t, c = u.shape
    ts, tc = CONV_TS, GROUP_W
    u_spec, prev_spec, w_spec, b_spec = _conv_specs(seq, c)

    def body(u_ref, prev_ref, w_ref, b_ref, o_ref, ext):
        _conv_fill(pl.program_id(1), seq, u_ref, prev_ref, ext)
        wv, bv = w_ref[...], b_ref[...]
        for r0 in range(0, ts, CONV_PIECE):
            _, pre = _conv_piece(ext, r0, wv, bv)
            o_ref[r0:r0 + CONV_PIECE, :] = pre * _sigmoid(pre)

    return pl.pallas_call(
        body, name="conv_fwd", grid=(c // tc, t // ts), in_specs=[u_spec, prev_spec, w_spec, b_spec],
        out_specs=u_spec, out_shape=SDS((t, c), F32), scratch_shapes=[pltpu.VMEM((ts + CONV_HALO, tc), F32)],
        compiler_params=_cparams(("parallel", "arbitrary")))(u, u, w, b)


def _conv_bwd_pre(u, w, b, dxc, seq):
    t, c = u.shape
    ts, tc = CONV_TS, GROUP_W
    u_spec, prev_spec, w_spec, b_spec = _conv_specs(seq, c)

    def body(u_ref, prev_ref, w_ref, b_ref, d_ref, dpre_ref, dw_ref, db_ref, ext):
        i = pl.program_id(1)
        _conv_fill(i, seq, u_ref, prev_ref, ext)
        wv, bv = w_ref[...], b_ref[...]
        sums = [jnp.zeros((1, tc), F32)] * (CONV_K + 1)
        for r0 in range(0, ts, CONV_PIECE):
            taps, pre = _conv_piece(ext, r0, wv, bv)
            sg = _sigmoid(pre)
            dpre = d_ref[r0:r0 + CONV_PIECE, :] * sg * (1.0 + pre * (1.0 - sg))
            dpre_ref[r0:r0 + CONV_PIECE, :] = dpre
            sums = [s + _colsum(dpre * f) for s, f in zip(sums, taps + [1.0])]

        @pl.when(i == 0)
        def _():
            dw_ref[...] = jnp.zeros_like(dw_ref)
            db_ref[...] = jnp.zeros_like(db_ref)

        db_ref[...] += sums[CONV_K]
        for q in range(CONV_K):
            dw_ref[q:q + 1, :] += sums[q]

    return pl.pallas_call(
        body, name="conv_bwd_pre", grid=(c // tc, t // ts),
        in_specs=[u_spec, prev_spec, w_spec, b_spec, u_spec], out_specs=[u_spec, w_spec, b_spec],
        out_shape=[SDS((t, c), F32), SDS((CONV_K, c), F32), SDS((1, c), F32)],
        scratch_shapes=[pltpu.VMEM((ts + CONV_HALO, tc), F32)],
        compiler_params=_cparams(("parallel", "arbitrary")))(u, u, w, b, dxc)


def _conv_bwd_in(dpre, w, seq):
    t, c = dpre.shape
    ts, tc = CONV_TS, GROUP_W
    hb = ts // CONV_HALO
    last = t // CONV_HALO - 1
    d_spec = pl.BlockSpec((ts, tc), lambda j, i: (i, j))
    next_spec = pl.BlockSpec((CONV_HALO, tc), lambda j, i: (jnp.minimum((i + 1) * hb, last), j))
    w_spec = pl.BlockSpec((CONV_K, tc), lambda j, i: (0, j))

    def body(d_ref, next_ref, w_ref, o_ref, ext):
        i = pl.program_id(1)
        nts = seq // ts
        is_last = (i % nts) == nts - 1
        ext[0:ts, :] = d_ref[...]
        ext[ts:, :] = jnp.where(is_last, 0.0, next_ref[...])
        wv = w_ref[...]

        def rows(j, carry):
            r0 = pl.multiple_of(j * CONV_ROWS, CONV_ROWS)
            blk = ext[pl.ds(r0, CONV_ROWS + CONV_HALO), :]
            acc = wv[CONV_K - 1:CONV_K] * blk[0:CONV_ROWS]
            for q in range(CONV_K - 1):
                acc = acc + wv[q:q + 1] * blk[CONV_K - 1 - q:CONV_K - 1 - q + CONV_ROWS]
            o_ref[pl.ds(r0, CONV_ROWS), :] = acc.astype(o_ref.dtype)
            return carry

        lax.fori_loop(0, ts // CONV_ROWS, rows, 0)

    return pl.pallas_call(
        body, name="conv_bwd_in", grid=(c // tc, t // ts), in_specs=[d_spec, next_spec, w_spec],
        out_specs=d_spec, out_shape=SDS((t, c), BF16), scratch_shapes=[pltpu.VMEM((ts + CONV_HALO, tc), F32)],
        compiler_params=_cparams(("parallel", "arbitrary")))(dpre, dpre, w)


def _split3(v):
    hi = v.astype(BF16)
    r1 = v - hi.astype(F32)
    mid = r1.astype(BF16)
    lo = (r1 - mid.astype(F32)).astype(BF16)
    return hi, mid, lo


def _ssd_prelude(dtr_ref, dtrt_ref, bias_ref, biast_ref, a_ref, at_ref):
    dt = _softplus(dtr_ref[...] + bias_ref[...])
    dtt = _softplus(dtrt_ref[...] + biast_ref[...])
    ri = lax.broadcasted_iota(jnp.int32, (CHUNK, CHUNK), 0)
    ci = lax.broadcasted_iota(jnp.int32, (CHUNK, CHUNK), 1)
    lower = ri >= ci
    upper = ri <= ci
    lower_b = jnp.where(lower, 1.0, 0.0).astype(BF16)
    upper_b = jnp.where(upper, 1.0, 0.0).astype(BF16)
    acs = sum(_dot(lower_b, p) for p in _split3(dt * a_ref[...]))
    acst = sum(_dot(p, upper_b) for p in _split3(dtt * at_ref[...]))
    return dt, acs, acst, lower, upper, lower_b, upper_b


SSD_FWD_GPS = 2
SSD_BWD_GPS = 1


def _ssd_specs(seq, gps):
    nc = seq // CHUNK
    hg = HEADS_PER_GROUP
    fwd = lambda c: c
    rev = lambda c: nc - 1 - c

    def specs(cc):
        return dict(
            xc=pl.BlockSpec((CHUNK, gps * GROUP_W), lambda g, b, c: (b * nc + cc(c), g)),
            y=pl.BlockSpec((CHUNK, gps * hg * HEAD_P), lambda g, b, c: (b * nc + cc(c), g)),
            dtr=pl.BlockSpec((gps, CHUNK, hg), lambda g, b, c: (g, b * nc + cc(c), 0)),
            dtrt=pl.BlockSpec((gps, None, hg, CHUNK), lambda g, b, c: (g, b, 0, cc(c))),
            prow=pl.BlockSpec((gps, 1, hg), lambda g, b, c: (g, 0, 0)),
            pcol=pl.BlockSpec((gps, hg, 1), lambda g, b, c: (g, 0, 0)),
            st=pl.BlockSpec((gps, None, None, D_STATE, hg * HEAD_P), lambda g, b, c: (g, b, cc(c), 0, 0)),
        )

    return specs(fwd), specs(rev)


def _group_views(refs, lane_widths, gi):
    return [r.at[:, gi * w:(gi + 1) * w] if w else r.at[gi] for r, w in zip(refs, lane_widths)]


def _head_maps():
    hw = HEADS_PER_GROUP * HEAD_P
    shift = HEAD_P.bit_length() - 1
    hj = lax.broadcasted_iota(jnp.int32, (HEADS_PER_GROUP, hw), 0)
    lq = jnp.right_shift(lax.broadcasted_iota(jnp.int32, (HEADS_PER_GROUP, hw), 1), shift)
    spread = jnp.where(hj == lq, 1.0, 0.0).astype(BF16)
    rq = jnp.right_shift(lax.broadcasted_iota(jnp.int32, (hw, LANES), 0), shift)
    cj = lax.broadcasted_iota(jnp.int32, (hw, LANES), 1)
    gather = jnp.where(rq == cj, 1.0, 0.0).astype(BF16)
    return spread, gather


def _dot01(v, m01):
    hi, mid, _ = _split3(v)
    return _dot(hi, m01) + _dot(mid, m01)


class _Side(NamedTuple):
    ins: tuple
    out_shapes: tuple
    scratch: tuple
    first: Callable
    mid: Optional[Callable]
    last: Callable


NO_SIDE = _Side((), (), (), lambda *refs: None, None, lambda *refs: None)


def _attach_side(body, n_in, n_out, side, grid):
    si, so, ss = len(side.ins), len(side.out_shapes), len(side.scratch)

    def wrapped(*refs):
        ins, s_in = refs[:n_in], refs[n_in:n_in + si]
        outs = refs[n_in + si:n_in + si + n_out]
        s_out = refs[n_in + si + n_out:n_in + si + n_out + so]
        rest = refs[n_in + si + n_out + so:]
        scr, s_scr = rest[:len(rest) - ss], rest[len(rest) - ss:]
        ids = [pl.program_id(a) for a in range(len(grid))]
        inner_first = functools.reduce(lambda p, q: p & q, [i == 0 for i in ids[1:]], ids[0] >= 0)
        at_last = functools.reduce(lambda p, q: p & q, [i == n - 1 for i, n in zip(ids, grid)])

        @pl.when((ids[0] == 0) & inner_first)
        def _():
            side.first(s_in, s_out, s_scr)

        if side.mid is not None:
            outer_last = functools.reduce(lambda p, q: p & q, [i == n - 1 for i, n in zip(ids[:-1], grid[:-1])])

            @pl.when(outer_last & (ids[-1] == 0))
            def _():
                side.mid(s_in, s_out, s_scr)

        body(*ins, *outs, *scr)

        @pl.when(at_last)
        def _():
            side.last(s_in, s_out, s_scr)

    return wrapped


def _ssd_fwd(xc, dtr, dtrt, bias, biast, a, at, dskip, nb, seq, side):
    t = xc.shape[0]
    nc = seq // CHUNK
    hg = HEADS_PER_GROUP
    hw = hg * HEAD_P
    gps = SSD_FWD_GPS
    grid = (N_GROUPS // gps, nb, nc)
    sp, _ = _ssd_specs(seq, gps)

    def body(*refs):
        for gi in range(gps):
            one_group(*_group_views(refs, (GROUP_W, 0, 0, 0, 0, 0, 0, 0, hw, 0, 0), gi))

    def one_group(xc_ref, dtr_ref, dtrt_ref, bias_ref, biast_ref, a_ref, at_ref, d_ref, y_ref, sin_ref, st):
        @pl.when(pl.program_id(2) == 0)
        def _():
            st[...] = jnp.zeros_like(st)

        s_in = st[...]
        sin_ref[...] = s_in
        dt, acs, acst, lower, _, _, _ = _ssd_prelude(dtr_ref, dtrt_ref, bias_ref, biast_ref, a_ref, at_ref)
        spread, _ = _head_maps()
        x = xc_ref[...]
        xs = x[:, :hw]
        b16 = x[:, hw:hw + D_STATE].astype(BF16)
        c16 = x[:, hw + D_STATE:].astype(BF16)
        cb = _dot_nt(c16, b16)
        last = acs[CHUNK - 1:CHUNK, :]
        e_x = _dot01(jnp.exp(acs), spread)
        dec_x = _dot01(jnp.exp(last - acs), spread)
        tot_x = e_x[CHUNK - 1:CHUNK, :]
        d_x = _dot01(jnp.broadcast_to(d_ref[...], (8, hg)), spread)[0:1, :]
        xdtf = xs * _dot01(dt, spread)
        xdt16 = xdtf.astype(BF16)
        yoff = e_x * _dot(c16, s_in.astype(BF16))
        st[...] = tot_x * s_in + _dot_tn(b16, (dec_x * xdtf).astype(BF16))
        parts = []
        for j in range(hg):
            decay = jnp.exp(jnp.where(lower, acs[:, j:j + 1] - acst[j:j + 1, :], -jnp.inf))
            parts.append(_dot((cb * decay).astype(BF16), xdt16[:, HEAD_P * j:HEAD_P * (j + 1)]))
        y_ref[...] = jnp.concatenate(parts, axis=-1) + yoff + d_x * xs

    return pl.pallas_call(
        _attach_side(body, 8, 2, side, grid), name="ssd_fwd", grid=grid,
        in_specs=[sp["xc"], sp["dtr"], sp["dtrt"], sp["prow"], sp["pcol"], sp["prow"], sp["pcol"], sp["prow"]]
        + [ANY] * len(side.ins),
        out_specs=[sp["y"], sp["st"]] + [ANY] * len(side.out_shapes),
        out_shape=[SDS((t, D_INNER), F32), SDS((N_GROUPS, nb, nc, D_STATE, hw), F32)] + list(side.out_shapes),
        scratch_shapes=[pltpu.VMEM((gps, D_STATE, hw), F32)] + list(side.scratch),
        compiler_params=_cparams(("arbitrary", "arbitrary", "arbitrary")))(
            xc, dtr, dtrt, bias, biast, a, at, dskip, *side.ins)


def _ssd_bwd(xc, dtr, dtrt, bias, biast, a, at, dskip, states, dy, nb, seq, side):
    t = xc.shape[0]
    nc = seq // CHUNK
    hg = HEADS_PER_GROUP
    hw = hg * HEAD_P
    gps = SSD_BWD_GPS
    grid = (N_GROUPS // gps, nb, nc)
    _, sp = _ssd_specs(seq, gps)

    def body(*refs):
        for gi in range(gps):
            one_group(*_group_views(refs, (GROUP_W, 0, 0, 0, 0, 0, 0, 0, 0, hw, GROUP_W, 0, 0, 0, 0, 0), gi))

    def one_group(xc_ref, dtr_ref, dtrt_ref, bias_ref, biast_ref, a_ref, at_ref, d_ref, sin_ref, dy_ref,
                  dxc_ref, ddtr_ref, gbias_ref, ga_ref, gd_ref, ds):
        first = (pl.program_id(1) == 0) & (pl.program_id(2) == 0)

        @pl.when(pl.program_id(2) == 0)
        def _():
            ds[...] = jnp.zeros_like(ds)

        @pl.when(first)
        def _():
            gbias_ref[...] = jnp.zeros_like(gbias_ref)
            ga_ref[...] = jnp.zeros_like(ga_ref)
            gd_ref[...] = jnp.zeros_like(gd_ref)

        dt, acs, acst, lower, upper, _, upper_b = _ssd_prelude(dtr_ref, dtrt_ref, bias_ref, biast_ref, a_ref, at_ref)
        spread, gather = _head_maps()
        x = xc_ref[...]
        dy = dy_ref[...]
        xs = x[:, :hw]
        b16 = x[:, hw:hw + D_STATE].astype(BF16)
        c16 = x[:, hw + D_STATE:].astype(BF16)
        dy16 = dy.astype(BF16)
        cb = _dot_nt(c16, b16)
        cbt = _dot_nt(b16, c16)
        last = acs[CHUNK - 1:CHUNK, :]
        e8 = jnp.exp(acs)
        dec8 = jnp.exp(last - acs)
        e_x = _dot01(e8, spread)
        dec_x = _dot01(dec8, spread)
        tot_x = e_x[CHUNK - 1:CHUNK, :]
        dt_x = _dot01(dt, spread)
        d_x = _dot01(jnp.broadcast_to(d_ref[...], (8, hg)), spread)[0:1, :]
        xdtf = xs * dt_x
        xdt16 = xdtf.astype(BF16)
        s_in = sin_ref[...]
        s16 = s_in.astype(BF16)
        ds_out = ds[...]
        ds16 = ds_out.astype(BF16)
        bds = _dot(b16, ds16)
        cs = _dot(c16, s16)
        edy16 = (e_x * dy).astype(BF16)
        ds[...] = tot_x * ds_out + _dot_tn(c16, edy16)
        lane8 = lax.broadcasted_iota(jnp.int32, (CHUNK, hg), 1)
        row8 = lax.broadcasted_iota(jnp.int32, (CHUNK, hg), 0)
        dacs8 = jnp.zeros((CHUNK, hg), F32)
        acc_m = jnp.zeros((CHUNK, CHUNK), F32)
        acc_mt = jnp.zeros((CHUNK, CHUNK), F32)
        dx_parts = []
        for j in range(hg):
            sl = slice(HEAD_P * j, HEAD_P * (j + 1))
            col = acs[:, j:j + 1]
            row = acst[j:j + 1, :]
            decay = jnp.exp(jnp.where(lower, col - row, -jnp.inf))
            decayt = jnp.exp(jnp.where(upper, row - col, -jnp.inf))
            wm = _dot_nt(dy16[:, sl], xdt16[:, sl]) * decay
            wmt = _dot_nt(xdt16[:, sl], dy16[:, sl]) * decayt
            acc_m = acc_m + wm
            acc_mt = acc_mt + wmt
            dacs8 = dacs8 + jnp.where(lane8 == j, jnp.sum(wm * cb, axis=-1, keepdims=True)
                                      - jnp.sum(wmt * cbt, axis=-1, keepdims=True), 0.0)
            dx_parts.append(_dot((cbt * decayt).astype(BF16), dy16[:, sl]))
        dx = jnp.concatenate(dx_parts, axis=-1) + dec_x * bds
        dxc_ref[:, :hw] = dx * dt_x + d_x * dy
        dxc_ref[:, hw:hw + D_STATE] = _dot(acc_mt.astype(BF16), c16) + _dot_nt((dec_x * xdtf).astype(BF16), ds16)
        dxc_ref[:, hw + D_STATE:] = _dot(acc_m.astype(BF16), b16) + _dot_nt(edy16, s16)
        dtot_rows = jnp.broadcast_to(_colsum(ds_out * s_in), (8, hw))
        sums = _dot01(jnp.concatenate([dy * cs, xdtf * bds, dx * xs, dy * xs, dtot_rows], axis=0), gather)
        de8 = sums[0:CHUNK, :hg]
        ddec8 = sums[CHUNK:2 * CHUNK, :hg]
        ddtx8 = sums[2 * CHUNK:3 * CHUNK, :hg]
        gd8 = _colsum(sums[3 * CHUNK:4 * CHUNK, :hg])
        dtot8 = sums[4 * CHUNK:4 * CHUNK + 1, :hg]
        extra = _colsum(ddec8 * dec8) + dtot8 * e8[CHUNK - 1:CHUNK, :]
        dacs8 = dacs8 + de8 * e8 - ddec8 * dec8 + jnp.where(row8 == CHUNK - 1, extra, 0.0)
        da = sum(_dot(upper_b, p) for p in _split3(dacs8))
        av = a_ref[...]
        ddt = da * av + ddtx8
        ddtr = ddt * _sigmoid(dtr_ref[...] + bias_ref[...])
        ddtr_ref[...] = ddtr
        gbias_ref[...] += _colsum(ddtr)
        ga_ref[...] += _colsum(da * dt) * av
        gd_ref[...] += gd8

    return pl.pallas_call(
        _attach_side(body, 10, 5, side, grid), name="ssd_bwd", grid=grid,
        in_specs=[sp["xc"], sp["dtr"], sp["dtrt"], sp["prow"], sp["pcol"], sp["prow"], sp["pcol"], sp["prow"],
                  sp["st"], sp["y"]] + [ANY] * len(side.ins),
        out_specs=[sp["xc"], sp["dtr"], sp["prow"], sp["prow"], sp["prow"]] + [ANY] * len(side.out_shapes),
        out_shape=[SDS((t, N_GROUPS * GROUP_W), F32), SDS((N_GROUPS, t, hg), F32)]
        + [SDS((N_GROUPS, 1, hg), F32)] * 3 + list(side.out_shapes),
        scratch_shapes=[pltpu.VMEM((gps, D_STATE, hw), F32)] + list(side.scratch),
        compiler_params=_cparams(("arbitrary", "arbitrary", "arbitrary")))(
            xc, dtr, dtrt, bias, biast, a, at, dskip, states, dy, *side.ins)


def _group_bcast(v, width, fn):
    parts = []
    for q in range(v.shape[-1] // width):
        s = fn(v[:, q * width:(q + 1) * width])
        parts.append(jnp.broadcast_to(s, (v.shape[0], width)))
    return jnp.concatenate(parts, axis=-1)


def _gate_norm_fwd(y, z, g):
    t, d = y.shape
    tm = 256
    gw = d // N_GROUPS

    def fn(i, y_ref, z_ref, g_ref):
        zv = z_ref[...].astype(F32)
        u = y_ref[...] * (zv * _sigmoid(zv))
        r = lax.rsqrt(_group_bcast(u * u, gw, lambda p: jnp.mean(p, axis=-1, keepdims=True)) + EPS)
        return [u * r * g_ref[...]]

    return _rw("gate_norm_fwd", fn, t // tm, [(y, _rs(tm, d)), (z, _rs(tm, d)), (g, _fs((1, d)))],
               [(SDS((t, d), BF16), _rs(tm, d))])[0]


def _gate_norm_bwd_epilogue(dv, rows, fulls):
    yv, zv = rows[0][...], rows[1][...].astype(F32)
    gw = yv.shape[-1] // N_GROUPS
    sg = _sigmoid(zv)
    sz = zv * sg
    u = yv * sz
    r = lax.rsqrt(_group_bcast(u * u, gw, lambda p: jnp.mean(p, axis=-1, keepdims=True)) + EPS)
    uhat = u * r
    duhat = dv * fulls[0][...]
    du = r * (duhat - uhat * _group_bcast(duhat * uhat, gw, lambda p: jnp.mean(p, axis=-1, keepdims=True)))
    dz = du * yv * sg * (1.0 + zv * (1.0 - sg))
    return [du * sz, dz, _colsum(dv * uhat)]


def _rope_tables(seq):
    half = ATT_D // 2
    inv = ROPE_THETA ** (-jnp.arange(half, dtype=F32) / half)
    ang = jnp.arange(seq, dtype=F32)[:, None] * inv[None, :]
    cos, sin = jnp.cos(ang), jnp.sin(ang)
    return jnp.concatenate([cos, cos], axis=-1), jnp.concatenate([-sin, sin], axis=-1)


ATT_TILE = 512
ATT_QB = 8


def _strided_spec(r, mtiles):
    return pl.BlockSpec((None, r, None, ATT_TILE // r, ATT_W), lambda i: (i // mtiles, 0, i % mtiles, 0, 0))


def _strided_shape(nb, r, mtiles, dtype):
    return SDS((nb, r, mtiles, ATT_TILE // r, ATT_W), dtype)


def _to_strided(val, out_ref, lanes, r, sc):
    if r == 1:
        out_ref[0, :, lanes] = val.astype(out_ref.dtype)
        return
    sc[...] = val
    for rr in range(r):
        out_ref[rr, :, lanes] = sc[pl.ds(rr, ATT_TILE // r, stride=r), :].astype(out_ref.dtype)


def _from_strided(in_ref, lanes, r, sc):
    if r == 1:
        return in_ref[0, :, lanes].astype(F32)
    for rr in range(r):
        sc[pl.ds(rr, ATT_TILE // r, stride=r), :] = in_ref[rr, :, lanes].astype(F32)
    return sc[...]


def _rope_fwd(qkv, cos, sin, nb, seq):
    t = qkv.shape[0]
    tm = ATT_TILE
    mtiles = seq // tm
    w = ATT_HEADS * ATT_D
    tab = pl.BlockSpec((tm, ATT_D), lambda i: (i % mtiles, 0))
    ng = len(ATT_DILATIONS)

    def body(q_ref, k_ref, v_ref, cos_ref, sin_ref, *rest):
        outs, sc = rest[:3 * ng], rest[3 * ng]
        c, s = cos_ref[...], sin_ref[...]
        for which, ref in enumerate((q_ref, k_ref, v_ref)):
            for h in range(ATT_HEADS):
                g, slot = divmod(h, ATT_SLOTS)
                p = ref[:, h * ATT_D:(h + 1) * ATT_D].astype(F32)
                if which < 2:
                    p = p * c + pltpu.roll(p, ATT_D // 2, 1) * s
                _to_strided(p, outs[which * ng + g], slice(slot * ATT_D, (slot + 1) * ATT_D), ATT_DILATIONS[g], sc)

    out_specs = [_strided_spec(r, mtiles) for _ in range(3) for r in ATT_DILATIONS]
    out_shape = [_strided_shape(nb, r, mtiles, BF16) for _ in range(3) for r in ATT_DILATIONS]
    outs = pl.pallas_call(
        body, name="rope_fwd", grid=(t // tm,),
        in_specs=[_rs(tm, w, 0), _rs(tm, w, 1), _rs(tm, w, 2), tab, tab], out_specs=out_specs, out_shape=out_shape,
        scratch_shapes=[pltpu.VMEM((tm, ATT_D), F32)], compiler_params=_cparams(("arbitrary",)))(
            qkv, qkv, qkv, cos, sin)
    flat = [o.reshape(t, ATT_W) for o in outs]
    return flat[0:ng], flat[ng:2 * ng], flat[2 * ng:]


def _rope_bwd(dq, dk, dv, cos, sin, nb, seq):
    t = dq[0].shape[0]
    tm = ATT_TILE
    mtiles = seq // tm
    w = ATT_HEADS * ATT_D
    tab = pl.BlockSpec((tm, ATT_D), lambda i: (i % mtiles, 0))
    ng = len(ATT_DILATIONS)

    def body(*refs):
        ins, (cos_ref, sin_ref, o_ref, sc) = refs[:3 * ng], refs[3 * ng:]
        c, s = cos_ref[...], sin_ref[...]
        for which in range(3):
            for h in range(ATT_HEADS):
                g, slot = divmod(h, ATT_SLOTS)
                p = _from_strided(ins[which * ng + g], slice(slot * ATT_D, (slot + 1) * ATT_D), ATT_DILATIONS[g], sc)
                if which < 2:
                    p = p * c - pltpu.roll(p, ATT_D // 2, 1) * s
                o_ref[:, which * w + h * ATT_D:which * w + (h + 1) * ATT_D] = p.astype(o_ref.dtype)

    views = [a.reshape(nb, r, mtiles, tm // r, ATT_W) for grp in (dq, dk, dv) for a, r in zip(grp, ATT_DILATIONS)]
    return pl.pallas_call(
        body, name="rope_bwd", grid=(t // tm,),
        in_specs=[_strided_spec(r, mtiles) for _ in range(3) for r in ATT_DILATIONS] + [tab, tab],
        out_specs=_rs(tm, 3 * w), out_shape=SDS((t, 3 * w), BF16),
        scratch_shapes=[pltpu.VMEM((tm, ATT_D), F32)], compiler_params=_cparams(("arbitrary",)))(*views, cos, sin)


def _att_masks():
    ri = lax.broadcasted_iota(jnp.int32, (ATT_BLOCK, ATT_BLOCK), 0)
    ci = lax.broadcasted_iota(jnp.int32, (ATT_BLOCK, ATT_BLOCK), 1)
    return ci <= ri, ci >= ri


def _att_fwd(q, k, v, g, seq):
    t, w = q.shape
    rows = ATT_QB * ATT_BLOCK
    nbs = seq // ATT_DILATIONS[g] // ATT_BLOCK
    scale = ATT_D ** -0.5
    cur = pl.BlockSpec((rows, w), lambda n: (n, 0))
    prev = pl.BlockSpec((ATT_BLOCK, w), lambda n: (jnp.maximum(n * ATT_QB - 1, 0), 0))

    def body(q_ref, kc_ref, kp_ref, vc_ref, vp_ref, o_ref, lse_ref):
        mcur, mprev = _att_masks()
        for i in range(ATT_QB):
            blk = pl.program_id(0) * ATT_QB + i
            mask = jnp.concatenate([mprev & ((blk % nbs) != 0), mcur], axis=-1)
            own = slice(i * ATT_BLOCK, (i + 1) * ATT_BLOCK)
            for h in range(ATT_SLOTS):
                sl = slice(h * ATT_D, (h + 1) * ATT_D)
                if i == 0:
                    keys = jnp.concatenate([kp_ref[:, sl], kc_ref[own, sl]], axis=0)
                    vals = jnp.concatenate([vp_ref[:, sl], vc_ref[own, sl]], axis=0)
                else:
                    both = slice((i - 1) * ATT_BLOCK, (i + 1) * ATT_BLOCK)
                    keys, vals = kc_ref[both, sl], vc_ref[both, sl]
                s = jnp.where(mask, _dot_nt(q_ref[own, sl], keys) * scale, -jnp.inf)
                m = jnp.max(s, axis=-1, keepdims=True)
                p = jnp.exp(s - m)
                den = jnp.sum(p, axis=-1, keepdims=True)
                o_ref[own, sl] = _dot(p.astype(BF16), vals) / den
                lse_ref[own, sl] = jnp.broadcast_to(m + jnp.log(den), (ATT_BLOCK, ATT_D))

    return pl.pallas_call(
        body, name=f"att_fwd_{g}", grid=(t // rows,), in_specs=[cur, cur, prev, cur, prev], out_specs=[cur, cur],
        out_shape=[SDS((t, w), F32), SDS((t, w), F32)],
        compiler_params=_cparams(("arbitrary",)))(q, k, k, v, v)


def _att_bwd(q, k, v, do, lse, dlt, g, seq):
    t, w = q.shape
    nblk = t // ATT_BLOCK
    rows = ATT_QB * ATT_BLOCK
    nbs = seq // ATT_DILATIONS[g] // ATT_BLOCK
    scale = ATT_D ** -0.5
    cur = pl.BlockSpec((rows, w), lambda n: (n, 0))
    nxt = pl.BlockSpec((ATT_BLOCK, w), lambda n: (jnp.minimum((n + 1) * ATT_QB, nblk - 1), 0))

    def body(qc_ref, qn_ref, k_ref, v_ref, doc_ref, don_ref, lsec_ref, lsen_ref, dltc_ref, dltn_ref,
             dq_ref, dk_ref, dv_ref, carry):
        n = pl.program_id(0)

        @pl.when(n == 0)
        def _():
            carry[...] = jnp.zeros_like(carry)

        mcur, mprev = _att_masks()

        def pair(cur_ref, nxt_ref, i, sl):
            if i + 1 < ATT_QB:
                return cur_ref[i * ATT_BLOCK:(i + 2) * ATT_BLOCK, sl]
            return jnp.concatenate([cur_ref[i * ATT_BLOCK:, sl], nxt_ref[:, sl]], axis=0)

        for h in range(ATT_SLOTS):
            sl = slice(h * ATT_D, (h + 1) * ATT_D)
            from_prev = carry[:, sl]
            for i in range(ATT_QB):
                blk = n * ATT_QB + i
                has_next = (((blk + 1) % nbs) != 0) & (blk + 1 < nblk)
                mask = jnp.concatenate([mcur, mprev & has_next], axis=0)
                own = slice(i * ATT_BLOCK, (i + 1) * ATT_BLOCK)
                kh, vh = k_ref[own, sl], v_ref[own, sl]
                qs, dos = pair(qc_ref, qn_ref, i, sl), pair(doc_ref, don_ref, i, sl)
                lse, dlt = pair(lsec_ref, lsen_ref, i, sl), pair(dltc_ref, dltn_ref, i, sl)
                p = jnp.where(mask, jnp.exp(_dot_nt(qs, kh) * scale - lse), 0.0)
                ds = (p * (_dot_nt(dos, vh) - dlt) * scale).astype(BF16)
                dqs = _dot(ds, kh)
                dq_ref[own, sl] = (from_prev + dqs[:ATT_BLOCK]).astype(dq_ref.dtype)
                from_prev = dqs[ATT_BLOCK:]
                dk_ref[own, sl] = _dot_tn(ds, qs).astype(dk_ref.dtype)
                dv_ref[own, sl] = _dot_tn(p.astype(BF16), dos).astype(dv_ref.dtype)
            carry[:, sl] = from_prev

    return pl.pallas_call(
        body, name=f"att_bwd_{g}", grid=(t // rows,), in_specs=[cur, nxt, cur, cur, cur, nxt, cur, nxt, cur, nxt],
        out_specs=[cur, cur, cur], out_shape=[SDS((t, w), BF16)] * 3,
        scratch_shapes=[pltpu.VMEM((ATT_BLOCK, w), F32)],
        compiler_params=_cparams(("arbitrary",)))(q, q, k, v, do, do, lse, lse, dlt, dlt)


def _merge_weights(ls):
    m = jnp.maximum(jnp.maximum(ls[0], ls[1]), ls[2])
    es = [jnp.exp(v - m) for v in ls]
    den = es[0] + es[1] + es[2]
    return [e / den for e in es]


def _merge_fwd(o, lse, nb, seq):
    t = o[0].shape[0]
    tm = ATT_TILE
    mtiles = seq // tm
    ng = len(ATT_DILATIONS)

    def body(*refs):
        o_refs, l_refs, out_ref, scs = refs[:ng], refs[ng:2 * ng], refs[2 * ng], refs[2 * ng + 1:]
        for slot in range(ATT_SLOTS):
            lanes = slice(slot * ATT_D, (slot + 1) * ATT_D)
            ov = [_from_strided(o_refs[g], lanes, r, scs[2 * g]) for g, r in enumerate(ATT_DILATIONS)]
            ws = _merge_weights([_from_strided(l_refs[g], lanes, r, scs[2 * g + 1])
                                 for g, r in enumerate(ATT_DILATIONS)])
            out_ref[:, lanes] = (ws[0] * ov[0] + ws[1] * ov[1] + ws[2] * ov[2]).astype(out_ref.dtype)

    views = [a.reshape(nb, r, mtiles, tm // r, ATT_W) for grp in (o, lse) for a, r in zip(grp, ATT_DILATIONS)]
    return pl.pallas_call(
        body, name="att_merge_fwd", grid=(t // tm,),
        in_specs=[_strided_spec(r, mtiles) for _ in range(2) for r in ATT_DILATIONS],
        out_specs=_rs(tm, ATT_W), out_shape=SDS((t, ATT_W), BF16),
        scratch_shapes=[pltpu.VMEM((tm, ATT_D), F32)] * (2 * ng), compiler_params=_cparams(("arbitrary",)))(*views)


def _merge_bwd(o, lse, datt, nb, seq):
    t = o[0].shape[0]
    tm = ATT_TILE
    mtiles = seq // tm
    ng = len(ATT_DILATIONS)

    def body(*refs):
        o_refs, l_refs, d_ref = refs[:ng], refs[ng:2 * ng], refs[2 * ng]
        do_refs, dlt_refs = refs[2 * ng + 1:3 * ng + 1], refs[3 * ng + 1:4 * ng + 1]
        scs = refs[4 * ng + 1:]
        for slot in range(ATT_SLOTS):
            lanes = slice(slot * ATT_D, (slot + 1) * ATT_D)
            ov = [_from_strided(o_refs[g], lanes, r, scs[2 * g]) for g, r in enumerate(ATT_DILATIONS)]
            ws = _merge_weights([_from_strided(l_refs[g], lanes, r, scs[2 * g + 1])
                                 for g, r in enumerate(ATT_DILATIONS)])
            dv = d_ref[:, lanes]
            att = ws[0] * ov[0] + ws[1] * ov[1] + ws[2] * ov[2]
            dot = jnp.broadcast_to(jnp.sum(dv * att, axis=-1, keepdims=True), (tm, ATT_D))
            for g, r in enumerate(ATT_DILATIONS):
                _to_strided(ws[g] * dv, do_refs[g], lanes, r, scs[2 * ng])
                _to_strided(ws[g] * dot, dlt_refs[g], lanes, r, scs[2 * ng + 1])

    views = [a.reshape(nb, r, mtiles, tm // r, ATT_W) for grp in (o, lse) for a, r in zip(grp, ATT_DILATIONS)]
    outs = pl.pallas_call(
        body, name="att_merge_bwd", grid=(t // tm,),
        in_specs=[_strided_spec(r, mtiles) for _ in range(2) for r in ATT_DILATIONS] + [_rs(tm, ATT_W)],
        out_specs=[_strided_spec(r, mtiles) for _ in range(2) for r in ATT_DILATIONS],
        out_shape=[_strided_shape(nb, r, mtiles, dt) for dt in (BF16, F32) for r in ATT_DILATIONS],
        scratch_shapes=[pltpu.VMEM((tm, ATT_D), F32)] * (2 * ng + 2), compiler_params=_cparams(("arbitrary",)))(
            *views, datt)
    flat = [a.reshape(t, ATT_W) for a in outs]
    return flat[:ng], flat[ng:]


def _branch_gates(rows, fulls):
    return (_sigmoid(rows[0][...].astype(F32) + fulls[0][...]), _sigmoid(rows[1][...].astype(F32) + fulls[1][...]))


def _mix_fwd_epilogue(y_att, rows, fulls):
    g0, g1 = _branch_gates(rows, fulls)
    return [y_att, g0 * rows[2][...].astype(F32) + g1 * y_att]


def _mix_bwd_epilogue(dm, rows, fulls):
    g0, g1 = _branch_gates(rows, fulls)
    dg = jnp.concatenate([dm * rows[2][...].astype(F32) * g0 * (1.0 - g0),
                          dm * rows[3][...].astype(F32) * g1 * (1.0 - g1)], axis=-1)
    return [dm * g0, dm * g1, dg, _colsum(dg)]


FFN_TM = 512


def _ffn_in(h2, wg_t, wu_t):
    t, d = h2.shape
    f = wg_t.shape[0]
    tm, tn = FFN_TM, _pick(f, 1536)

    def body(a_ref, g_ref, u_ref, gt_ref, up_ref, act_ref):
        a = a_ref[...]
        gt = _dot_nt(a, g_ref[...])
        up = _dot_nt(a, u_ref[...])
        gt_ref[...] = gt.astype(BF16)
        up_ref[...] = up.astype(BF16)
        act_ref[...] = (gt * _sigmoid(gt) * up).astype(BF16)

    a_spec = pl.BlockSpec((tm, d), lambda j, i: (i, 0))
    w_spec = pl.BlockSpec((tn, d), lambda j, i: (j, 0))
    o_spec = pl.BlockSpec((tm, tn), lambda j, i: (i, j))
    return pl.pallas_call(
        body, name="ffn_in", grid=(f // tn, t // tm), in_specs=[a_spec, w_spec, w_spec],
        out_specs=[o_spec] * 3, out_shape=[SDS((t, f), BF16)] * 3,
        compiler_params=_cparams(("parallel", "arbitrary")))(h2, wg_t, wu_t)


def _ffn_bwd_in(dx2, w_down, gt, up):
    t, d = dx2.shape
    f = w_down.shape[0]
    tm, tn = FFN_TM, _pick(f, 1536)

    def body(a_ref, w_ref, g_ref, u_ref, dgt_ref, dup_ref):
        dv = _dot_nt(a_ref[...], w_ref[...])
        gv = g_ref[...].astype(F32)
        sg = _sigmoid(gv)
        dgt_ref[...] = (dv * u_ref[...].astype(F32) * sg * (1.0 + gv * (1.0 - sg))).astype(BF16)
        dup_ref[...] = (dv * gv * sg).astype(BF16)

    a_spec = pl.BlockSpec((tm, d), lambda j, i: (i, 0))
    w_spec = pl.BlockSpec((tn, d), lambda j, i: (j, 0))
    o_spec = pl.BlockSpec((tm, tn), lambda j, i: (i, j))
    return pl.pallas_call(
        body, name="ffn_bwd_in", grid=(f // tn, t // tm), in_specs=[a_spec, w_spec, o_spec, o_spec],
        out_specs=[o_spec] * 2, out_shape=[SDS((t, f), BF16)] * 2,
        compiler_params=_cparams(("parallel", "arbitrary")))(dx2, w_down, gt, up)


def _adamw(w, g, m, v, name):
    r, c = w.shape[-2:]
    lead = w.ndim - 2
    tr = _row_tile(r, max(8, 400_000 // c))
    c1 = 1.0 / (1.0 - ADAM_B1 ** ADAM_STEP)
    c2 = 1.0 / (1.0 - ADAM_B2 ** ADAM_STEP)

    def fn(i, w_ref, g_ref, m_ref, v_ref):
        gv = g_ref[...]
        mn = ADAM_B1 * m_ref[...] + (1.0 - ADAM_B1) * gv
        vn = ADAM_B2 * v_ref[...] + (1.0 - ADAM_B2) * (gv * gv)
        delta = -ADAM_LR * ((mn * c1) / (jnp.sqrt(vn * c2) + ADAM_EPS) + ADAM_WD * w_ref[...])
        return [delta, mn, vn]

    spec = pl.BlockSpec((None,) * lead + (tr, c), lambda i: (0,) * lead + (i, 0))
    return _rw(name, fn, r // tr, [(w, spec), (g, spec), (m, spec), (v, spec)], [(SDS(w.shape, F32), spec)] * 3)


ANY = pl.BlockSpec(memory_space=pl.ANY)


def _place():
    x, y, c = lax.axis_index("x"), lax.axis_index("y"), lax.axis_index("c")
    chips = [(1 - x, y), (x, 1 - y), (1 - x, 1 - y)]
    return x, y, c, chips


def _remote(src, dst, ssem, rsem, to):
    return pltpu.make_async_remote_copy(src_ref=src, dst_ref=dst, send_sem=ssem, recv_sem=rsem, device_id=to,
                                        device_id_type=MESH)


def _copy_through_vmem(src, dst, buf, isem, osem):
    chunk = buf.shape[1]
    n = src.shape[0] // chunk
    load = lambda k: pltpu.make_async_copy(src.at[pl.ds(k * chunk, chunk)], buf.at[k % 2], isem.at[k % 2])
    store = lambda k: pltpu.make_async_copy(buf.at[k % 2], dst.at[pl.ds(k * chunk, chunk)], osem.at[k % 2])
    load(0).start()
    for k in range(n):
        load(k).wait()
        if k + 1 < n:
            if k >= 1:
                store(k - 1).wait()
            load(k + 1).start()
        store(k).start()
    if n >= 2:
        store(n - 2).wait()
    store(n - 1).wait()


def _copy_scratch(rows, width, dtype):
    chunk = _row_tile(rows, 512)
    return [pltpu.VMEM((2, chunk, width), dtype), pltpu.SemaphoreType.DMA((2,)), pltpu.SemaphoreType.DMA((2,))]


def _gather_weights(wp):
    def body(w_ref, out_ref, ssem, rsem, buf, isem, osem):
        x, y, c, chips = _place()
        me = 2 * x + y
        sib = (x, y, 1 - c)
        first = [_remote(w_ref.at[c], out_ref.at[me, c], ssem.at[j], rsem.at[j], (*chip, c))
                 for j, chip in enumerate(chips)]
        for cp in first:
            cp.start()
        for half in range(2):
            _copy_through_vmem(w_ref.at[half], out_ref.at[me, half], buf, isem, osem)
        passed = []
        for j, chip in enumerate(chips):
            ci = 2 * chip[0] + chip[1]
            _remote(w_ref.at[c], out_ref.at[ci, c], ssem.at[j], rsem.at[j], (*chip, c)).wait_recv()
            cp = _remote(out_ref.at[ci, c], out_ref.at[ci, c], ssem.at[3 + j], rsem.at[3 + j], sib)
            cp.start()
            passed.append(cp)
        for j, chip in enumerate(chips):
            ci = 2 * chip[0] + chip[1]
            _remote(out_ref.at[ci, 1 - c], out_ref.at[ci, 1 - c], ssem.at[3 + j], rsem.at[3 + j], sib).wait_recv()
        for cp in first + passed:
            cp.wait_send()

    return pl.pallas_call(
        body, name="gather_weights", in_specs=[ANY], out_specs=ANY,
        out_shape=SDS((N_CHIPS,) + wp.shape, wp.dtype),
        scratch_shapes=[pltpu.SemaphoreType.DMA((6,)), pltpu.SemaphoreType.DMA((6,))]
        + _copy_scratch(wp.shape[1], wp.shape[2], wp.dtype),
        compiler_params=pltpu.CompilerParams(has_side_effects=True))(wp)


def _swap_halves(g2, tag):
    def body(g_ref, out_ref, ssem, rsem):
        x, y, c, _ = _place()
        cp = _remote(g_ref.at[1 - c], out_ref, ssem, rsem, (x, y, 1 - c))
        cp.start()
        cp.wait()

    return pl.pallas_call(
        body, name="swap_halves_" + tag, in_specs=[ANY], out_specs=ANY, out_shape=SDS(g2.shape[1:], g2.dtype),
        scratch_shapes=[pltpu.SemaphoreType.DMA(()), pltpu.SemaphoreType.DMA(())],
        compiler_params=pltpu.CompilerParams(has_side_effects=True))(g2)


def _add_own_half(g2, other, c, tag):
    _, nch, rows, w = g2.shape
    tr = _row_tile(rows, 512)
    nr = rows // tr

    def body(c_ref, a_ref, b_ref, o_ref):
        o_ref[...] = (a_ref[...].astype(F32) + b_ref[...].astype(F32)).astype(o_ref.dtype)

    grid_spec = pltpu.PrefetchScalarGridSpec(
        num_scalar_prefetch=1, grid=(nch, nr),
        in_specs=[pl.BlockSpec((None, None, tr, w), lambda k, i, c_ref: (c_ref[0], k, i, 0)),
                  pl.BlockSpec((None, tr, w), lambda k, i, c_ref: (k, i, 0))],
        out_specs=pl.BlockSpec((None, tr, w), lambda k, i, c_ref: (k, i, 0)))
    return pl.pallas_call(
        body, name="add_own_half_" + tag, grid_spec=grid_spec, out_shape=SDS(other.shape, other.dtype),
        compiler_params=_cparams(("arbitrary", "arbitrary")))(jnp.reshape(c, (1,)).astype(jnp.int32), g2, other)


def _sum_chips(q, tag):
    nch, rows, w = q.shape
    tr = _row_tile(rows, 512)

    def fn(i, q_ref):
        return [((q_ref[0].astype(F32) + q_ref[1].astype(F32)) + q_ref[2].astype(F32)) + q_ref[3].astype(F32)]

    return _rw("sum_chips_" + tag, fn, rows // tr, [(q, pl.BlockSpec((nch, tr, w), lambda i: (0, i, 0)))],
               [(SDS((rows, w), F32), _rs(tr, w))])[0]


def _chip_copies(src_ref, dst_ref, ssem, rsem, outgoing):
    x, y, c, chips = _place()
    me = 2 * x + y
    cps = []
    for j, chip in enumerate(chips):
        ci = 2 * chip[0] + chip[1]
        cps.append(_remote(src_ref.at[ci], dst_ref.at[me if outgoing else ci], ssem.at[j], rsem.at[j], (*chip, c)))
    return cps, me


def _scatter_side(p):
    def first(ins, outs, scr):
        cps, me = _chip_copies(ins[0], outs[0], scr[0], scr[1], True)
        for cp in cps:
            cp.start()
        pltpu.make_async_copy(ins[0].at[me], outs[0].at[me], scr[2]).start()

    def last(ins, outs, scr):
        for cp in _chip_copies(ins[0], outs[0], scr[0], scr[1], False)[0]:
            cp.wait_recv()
        cps, me = _chip_copies(ins[0], outs[0], scr[0], scr[1], True)
        for cp in cps:
            cp.wait_send()
        pltpu.make_async_copy(ins[0].at[me], outs[0].at[me], scr[2]).wait()

    return _Side((p,), (SDS(p.shape, p.dtype),),
                 (pltpu.SemaphoreType.DMA((3,)), pltpu.SemaphoreType.DMA((3,)), pltpu.SemaphoreType.DMA(())),
                 first, None, last)


def _gather_copies(w_ref, out_ref, ssem, rsem):
    x, y, c, chips = _place()
    me = 2 * x + y
    sib = (x, y, 1 - c)
    sends, arrivals, forwards, from_sib = [], [], [], []
    for j, chip in enumerate(chips):
        ci = 2 * chip[0] + chip[1]
        sends.append(_remote(w_ref.at[c], out_ref.at[me, c], ssem.at[j], rsem.at[j], (*chip, c)))
        arrivals.append(_remote(w_ref.at[c], out_ref.at[ci, c], ssem.at[j], rsem.at[j], (*chip, c)))
        forwards.append(_remote(out_ref.at[ci, c], out_ref.at[ci, c], ssem.at[3 + j], rsem.at[3 + j], sib))
        from_sib.append(_remote(out_ref.at[ci, 1 - c], out_ref.at[ci, 1 - c], ssem.at[3 + j], rsem.at[3 + j], sib))
    return sends, arrivals, forwards, from_sib, me


def _gather_side(wp):
    def first(ins, outs, scr):
        sends, _, _, _, me = _gather_copies(ins[0], outs[0], scr[0], scr[1])
        for cp in sends:
            cp.start()
        pltpu.make_async_copy(ins[0], outs[0].at[me], scr[2]).start()

    def mid(ins, outs, scr):
        _, arrivals, forwards, _, _ = _gather_copies(ins[0], outs[0], scr[0], scr[1])
        for arrived, forward in zip(arrivals, forwards):
            arrived.wait_recv()
            forward.start()

    def last(ins, outs, scr):
        sends, _, forwards, from_sib, me = _gather_copies(ins[0], outs[0], scr[0], scr[1])
        for cp in from_sib:
            cp.wait_recv()
        for cp in sends + forwards:
            cp.wait_send()
        pltpu.make_async_copy(ins[0], outs[0].at[me], scr[2]).wait()

    return _Side((wp,), (SDS((N_CHIPS,) + wp.shape, wp.dtype),),
                 (pltpu.SemaphoreType.DMA((6,)), pltpu.SemaphoreType.DMA((6,)), pltpu.SemaphoreType.DMA(())),
                 first, mid, last)


def _allreduce_small(v, name):
    rows, w = v.shape
    offsets = [(dx, dy, dc) for dx in (0, 1) for dy in (0, 1) for dc in (0, 1)][1:]

    def body(v_ref, o_ref, buf, ssem, rsem):
        x, y, c, _ = _place()
        flip = lambda p, d: 1 - p if d else p
        peers = [(flip(x, dx), flip(y, dy), flip(c, dc)) for dx, dy, dc in offsets]
        index = lambda p: 4 * p[0] + 2 * p[1] + p[2]
        me = index((x, y, c))
        buf[me] = v_ref[...]
        sent = [_remote(v_ref, buf.at[me], ssem.at[q], rsem.at[q], p) for q, p in enumerate(peers)]
        for cp in sent:
            cp.start()
        for q, p in enumerate(peers):
            _remote(v_ref, buf.at[index(p)], ssem.at[q], rsem.at[q], p).wait_recv()
        for cp in sent:
            cp.wait_send()
        acc = buf[0]
        for q in range(1, 8):
            acc = acc + buf[q]
        o_ref[...] = acc

    vm = pl.BlockSpec(memory_space=pltpu.VMEM)
    return pl.pallas_call(
        body, name=name, in_specs=[vm], out_specs=vm, out_shape=SDS((rows, w), F32),
        scratch_shapes=[pltpu.VMEM((8, rows, w), F32), pltpu.SemaphoreType.DMA((7,)), pltpu.SemaphoreType.DMA((7,))],
        compiler_params=pltpu.CompilerParams(has_side_effects=True))(v)


def _join_halves(h, tag):
    def body(h_ref, out_ref, ssem, rsem, buf, isem, osem):
        x, y, c, _ = _place()
        cp = _remote(h_ref, out_ref.at[c], ssem, rsem, (x, y, 1 - c))
        cp.start()
        _copy_through_vmem(h_ref, out_ref.at[c], buf, isem, osem)
        _remote(h_ref, out_ref.at[1 - c], ssem, rsem, (x, y, 1 - c)).wait_recv()
        cp.wait_send()

    return pl.pallas_call(
        body, name="join_halves_" + tag, in_specs=[ANY], out_specs=ANY, out_shape=SDS((2,) + h.shape, h.dtype),
        scratch_shapes=[pltpu.SemaphoreType.DMA(()), pltpu.SemaphoreType.DMA(())]
        + _copy_scratch(h.shape[0], h.shape[1], h.dtype),
        compiler_params=pltpu.CompilerParams(has_side_effects=True))(h)


PACK_W = 1024
SHARDED = ("w_in", "w_ffn_gate", "w_ffn_up", "w_ssm_out", "w_att_out", "w_mix_out", "w_ffn_down")
COL_SHARDED = ("w_in", "w_ffn_gate", "w_ffn_up", "w_att_out")
SMALL = ("norm_mix", "b_gate", "conv_b", "dt_bias", "a_log", "d_skip", "ssm_norm", "norm_ffn", "norm_final")


PACK_ROW_ALIGN = 16


def _rows(n):
    return -(-n // (PACK_W * PACK_ROW_ALIGN)) * PACK_ROW_ALIGN


def _pack_rows(parts, total_rows):
    rows = []
    for p in parts:
        size = int(p.size)
        if size % PACK_W:
            p = jnp.pad(p.reshape(-1), (0, PACK_W - size % PACK_W))
        p = p.reshape(-1, PACK_W)
        rows.append(jnp.pad(p, ((0, _rows(size) - p.shape[0]), (0, 0))))
    used = sum(r.shape[0] for r in rows)
    if total_rows > used:
        rows.append(jnp.zeros((total_rows - used, PACK_W), rows[0].dtype))
    return jnp.concatenate(rows, axis=0)


def _padded_rows(n):
    return -(-n // 32) * 32


def _wire_name(name):
    return name + "_t" if name in COL_SHARDED else name


def _wire_shard(w, name):
    return w.T if name in COL_SHARDED else w


def _group_major(a, axis):
    gw = D_INNER // N_GROUPS
    take = lambda lo, n: lax.slice_in_dim(a, lo, lo + n, axis=axis)
    parts = []
    for g in range(N_GROUPS):
        parts += [take(g * gw, gw), take(D_INNER + g * D_STATE, D_STATE),
                  take(D_INNER + N_GROUPS * D_STATE + g * D_STATE, D_STATE)]
    return jnp.concatenate(parts, axis=axis)


def _group_major_inv(a, axis):
    gw = D_INNER // N_GROUPS
    take = lambda lo, n: lax.slice_in_dim(a, lo, lo + n, axis=axis)
    xs = [take(g * GROUP_W, gw) for g in range(N_GROUPS)]
    bs = [take(g * GROUP_W + gw, D_STATE) for g in range(N_GROUPS)]
    cs = [take(g * GROUP_W + gw + D_STATE, D_STATE) for g in range(N_GROUPS)]
    return jnp.concatenate(xs + bs + cs, axis=axis)


LATE = ("w_ffn_gate_t", "w_ffn_up_t", "w_ssm_out", "w_att_out_t", "w_mix_out", "w_ffn_down")


class _Overlap(NamedTuple):
    gather_side: _Side
    late_weights: Callable
    scatter_side: Callable
    scatter_in: Callable


def _local_step(x, target, wts, overlap):
    nb, seq, d = x.shape
    t = nb * seq
    x = x.reshape(t, d)
    target = target.reshape(t, d)
    hg = HEADS_PER_GROUP

    o1, o2, o3, o4 = D_INNER, D_INNER + CONV_DIM, D_INNER + CONV_DIM + N_HEADS, D_INNER + CONV_DIM + N_HEADS + QKV_DIM
    n_in = o4 + 2 * D_MODEL

    def in_rows(lo, hi):
        per = n_in // N_CHIPS
        parts = [wts["w_in_t"][k, max(lo, k * per) - k * per:min(hi, (k + 1) * per) - k * per]
                 for k in range(N_CHIPS) if max(lo, k * per) < min(hi, (k + 1) * per)]
        return parts[0] if len(parts) == 1 else jnp.concatenate(parts, axis=0)

    w_z = in_rows(0, o1)
    w_xbc = _group_major(in_rows(o1, o2), 0)
    w_dt = jnp.pad(in_rows(o2, o3), ((0, DT_PAD - N_HEADS), (0, 0)))
    w_qkv = in_rows(o3, o4)
    w_gate = in_rows(o4, n_in)
    conv_w = _group_major(wts["conv_w"], 1)
    conv_b = _group_major(wts["conv_b"], 1)

    def per_group_row(p):
        return p.reshape(N_GROUPS, 1, hg)

    def per_group_col(p):
        return p.reshape(N_GROUPS, hg, 1)

    a_neg = -jnp.exp(wts["a_log"])
    bias_r, bias_c = per_group_row(wts["dt_bias"]), per_group_col(wts["dt_bias"])
    a_r, a_c = per_group_row(a_neg), per_group_col(a_neg)
    dskip_r = per_group_row(wts["d_skip"])
    cos, sin = _rope_tables(seq)

    h = _rms_fwd(x, wts["norm_mix"], "rms_mix_fwd")
    z = _mm(h, w_z, "nt", BF16, "proj_z")
    xbc = _mm(h, w_xbc, "nt", F32, "proj_xbc")
    dt_raw = _mm(h, w_dt, "nt", F32, "proj_dt")
    qkv = _mm(h, w_qkv, "nt", BF16, "proj_qkv")
    gate_logits = _mm(h, w_gate, "nt", BF16, "proj_gate")

    xc = _conv_fwd(xbc, conv_w, conv_b, seq)
    dtr = dt_raw[:, :N_HEADS].reshape(t, N_GROUPS, hg).transpose(1, 0, 2)
    dtrt = dt_raw[:, :N_HEADS].reshape(nb, seq, N_GROUPS, hg).transpose(2, 0, 3, 1)
    y, states, *gathered = _ssd_fwd(xc, dtr, dtrt, bias_r, bias_c, a_r, a_c, dskip_r, nb, seq, overlap.gather_side)
    wts = {**wts, **overlap.late_weights(gathered)}
    yn = _gate_norm_fwd(y, z, wts["ssm_norm"])
    y_ssm = _mm(yn, wts["w_ssm_out"], "nn", BF16, "ssm_out")

    groups = range(len(ATT_DILATIONS))
    qg, kg, vg = _rope_fwd(qkv, cos, sin, nb, seq)
    o_g, lse_g = zip(*[_att_fwd(qg[i], kg[i], vg[i], i, seq) for i in groups])
    att = _merge_fwd(o_g, lse_g, nb, seq)
    gate_halves = [(gate_logits, d, 0), (gate_logits, d, 1)]
    b_gate_halves = [(wts["b_gate"], d, 0), (wts["b_gate"], d, 1)]
    y_att, mixed = _mm_fused(att, wts["w_att_out_t"], "nt", "att_out_mix", 512, _mix_fwd_epilogue,
                             gate_halves + [(y_ssm, d, 0)], b_gate_halves, [(d, BF16), (d, BF16)])

    def residual_and_norm(xv, rows, fulls):
        return [xv, xv * lax.rsqrt(jnp.mean(xv * xv, axis=-1, keepdims=True) + EPS) * fulls[0][...]]

    x1, h2 = _mm_fused(mixed, wts["w_mix_out"], "nn", "mix_out_norm", 512, residual_and_norm, [],
                       [(wts["norm_ffn"], d, 0)], [(d, F32), (d, BF16)], add=x)
    gt, up, act = _ffn_in(h2, wts["w_ffn_gate_t"], wts["w_ffn_up_t"])

    g = {}
    dx2, dx2_b, g["norm_final"], loss = _mm_fused(
        act, wts["w_ffn_down"], "nn", "ffn_down_loss", 512,
        lambda x2, rows, fulls: _final_values(x2, rows[0][...], fulls[0][...]),
        [(target, d, 0)], [(wts["norm_final"].reshape(1, d), d, 0)], [(d, F32), (d, BF16), (d, F32), (1, F32)],
        n_acc=2, add=x1)
    g["w_ffn_down"] = _mm(act, dx2_b, "tn", BF16, "g_ffn_down")
    dgt, dup = _ffn_bwd_in(dx2_b, wts["w_ffn_down"], gt, up)
    g["w_ffn_gate_t"] = _mm(dgt, h2, "tn", BF16, "g_ffn_gate")
    g["w_ffn_up_t"] = _mm(dup, h2, "tn", BF16, "g_ffn_up")
    dh2 = _mm(dgt, wts["w_ffn_gate_t"], "nn", F32, "d_h2_gate")
    dx1, dx1_b, g["norm_ffn"] = _mm_fused(
        dup, wts["w_ffn_up_t"], "nn", "d_h2_up_norm", 512,
        lambda dh, rows, fulls: _rms_bwd_values(rows[0][...], dh, fulls[0][...], rows[1][...]),
        [(x1, d, 0), (dx2, d, 0)], [(wts["norm_ffn"], d, 0)], [(d, F32), (d, BF16), (d, F32)], n_acc=1, add=dh2)

    g["w_mix_out"] = _mm(mixed, dx1_b, "tn", BF16, "g_mix_out")
    dy_ssm, dy_att, dgate, g["b_gate"] = _mm_fused(
        dx1_b, wts["w_mix_out"], "nt", "d_mixed_gates", 512, _mix_bwd_epilogue,
        gate_halves + [(y_ssm, d, 0), (y_att, d, 0)], b_gate_halves,
        [(d, BF16), (d, BF16), (2 * d, BF16), (2 * d, F32)], n_acc=1)

    datt = _mm(dy_att, wts["w_att_out_t"], "nn", F32, "d_att")
    g["w_att_out_t"] = _mm(dy_att, att, "tn", BF16, "g_att_out")
    do_g, dlt_g = _merge_bwd(o_g, lse_g, datt, nb, seq)
    dq_g, dk_g, dv_g = zip(*[_att_bwd(qg[i], kg[i], vg[i], do_g[i], lse_g[i], dlt_g[i], i, seq) for i in groups])
    dqkv = _rope_bwd(dq_g, dk_g, dv_g, cos, sin, nb, seq)

    g["w_ssm_out"] = _mm(yn, dy_ssm, "tn", BF16, "g_ssm_out")
    dy, dz, g["ssm_norm"] = _mm_fused(
        dy_ssm, wts["w_ssm_out"], "nt", "d_yn_norm", 256, _gate_norm_bwd_epilogue,
        [(y, D_INNER, 0), (z, D_INNER, 0)], [(wts["ssm_norm"], D_INNER, 0)],
        [(D_INNER, F32), (D_INNER, BF16), (D_INNER, F32)], n_acc=1)
    side = overlap.scatter_side({n: g.pop(n) for n in LATE})
    dxc, ddtr, g_bias, g_alog, g_dskip, *scattered = _ssd_bwd(xc, dtr, dtrt, bias_r, bias_c, a_r, a_c, dskip_r,
                                                               states, dy, nb, seq, side)
    g["dt_bias"] = g_bias.reshape(1, N_HEADS)
    g["a_log"] = g_alog.reshape(1, N_HEADS)
    g["d_skip"] = g_dskip.reshape(1, N_HEADS)
    dpre, g_conv_w, g_conv_b = _conv_bwd_pre(xbc, conv_w, conv_b, dxc, seq)
    g["conv_w"] = _group_major_inv(g_conv_w, 1)
    g["conv_b"] = _group_major_inv(g_conv_b, 1)
    dxbc = _conv_bwd_in(dpre, conv_w, seq)
    ddt = jnp.pad(ddtr.transpose(1, 0, 2).reshape(t, N_HEADS), ((0, 0), (0, DT_PAD - N_HEADS))).astype(BF16)

    g_in_t = jnp.concatenate([
        _mm(dz, h, "tn", BF16, "g_in_z"),
        _group_major_inv(_mm(dxbc, h, "tn", BF16, "g_in_xbc"), 0),
        _mm(ddt, h, "tn", BF16, "g_in_dt")[:N_HEADS],
        _mm(dqkv, h, "tn", BF16, "g_in_qkv"),
        _mm(dgate, h, "tn", BF16, "g_in_gate")], axis=0)
    dh = _mm(dz, w_z, "nn", F32, "d_h_z")
    dh = _mm(dxbc, w_xbc, "nn", F32, "d_h_xbc", add=dh)
    dh = _mm(ddt, w_dt, "nn", F32, "d_h_dt", add=dh)
    dh, *scattered_in = _mm(dqkv, w_qkv, "nn", F32, "d_h_qkv", add=dh, side=overlap.scatter_in({"w_in_t": g_in_t}))
    dx, _, g["norm_mix"] = _mm_fused(
        dgate, w_gate, "nn", "d_h_gate_norm", 512,
        lambda dhv, rows, fulls: _rms_bwd_values(rows[0][...], dhv, fulls[0][...], rows[1][...]),
        [(x, d, 0), (dx1, d, 0)], [(wts["norm_mix"], d, 0)], [(d, F32), (d, BF16), (d, F32)], n_acc=1, add=dh)
    return loss[0, 0], dx.reshape(nb, seq, d), g, scattered, scattered_in


def kernel(x, norm_mix, w_in, b_gate, conv_w, conv_b, dt_bias, a_log, d_skip, ssm_norm, w_ssm_out, w_att_out, w_mix_out, norm_ffn, w_ffn_gate, w_ffn_up, w_ffn_down, norm_final, loss_target, m_norm_mix, m_w_in, m_b_gate, m_conv_w, m_conv_b, m_dt_bias, m_a_log, m_d_skip, m_ssm_norm, m_w_ssm_out, m_w_att_out, m_w_mix_out, m_norm_ffn, m_w_ffn_gate, m_w_ffn_up, m_w_ffn_down, m_norm_final, v_norm_mix, v_w_in, v_b_gate, v_conv_w, v_conv_b, v_dt_bias, v_a_log, v_d_skip, v_ssm_norm, v_w_ssm_out, v_w_att_out, v_w_mix_out, v_norm_ffn, v_w_ffn_gate, v_w_ffn_up, v_w_ffn_down, v_norm_final):
    names = ("norm_mix", "w_in", "b_gate", "conv_w", "conv_b", "dt_bias", "a_log", "d_skip", "ssm_norm", "w_ssm_out",
             "w_att_out", "w_mix_out", "norm_ffn", "w_ffn_gate", "w_ffn_up", "w_ffn_down", "norm_final")
    w_loc = dict(zip(names, (norm_mix, w_in, b_gate, conv_w, conv_b, dt_bias, a_log, d_skip, ssm_norm, w_ssm_out,
                             w_att_out, w_mix_out, norm_ffn, w_ffn_gate, w_ffn_up, w_ffn_down, norm_final)))
    m_loc = dict(zip(names, (m_norm_mix, m_w_in, m_b_gate, m_conv_w, m_conv_b, m_dt_bias, m_a_log, m_d_skip,
                             m_ssm_norm, m_w_ssm_out, m_w_att_out, m_w_mix_out, m_norm_ffn, m_w_ffn_gate,
                             m_w_ffn_up, m_w_ffn_down, m_norm_final)))
    v_loc = dict(zip(names, (v_norm_mix, v_w_in, v_b_gate, v_conv_w, v_conv_b, v_dt_bias, v_a_log, v_d_skip,
                             v_ssm_norm, v_w_ssm_out, v_w_att_out, v_w_mix_out, v_norm_ffn, v_w_ffn_gate,
                             v_w_ffn_up, v_w_ffn_down, v_norm_final)))
    two_d = lambda a: a.reshape(a.shape[-2:]) if a.ndim >= 2 else a.reshape(1, -1)
    w2 = {n: two_d(a) for n, a in w_loc.items()}
    chip = 2 * lax.axis_index("x") + lax.axis_index("y")
    c = lax.axis_index("c")

    wire_shapes = {n: _wire_shard(w2[n], n).shape for n in SHARDED}
    true_rows = {n: wire_shapes[n][0] * wire_shapes[n][1] // PACK_W for n in SHARDED}
    seg_rows = {n: _rows(wire_shapes[n][0] * wire_shapes[n][1]) for n in SHARDED}
    buckets = {"first": ("w_in",), "late": tuple(n for n in SHARDED if n != "w_in")}
    rows_of = {b: _padded_rows(sum(seg_rows[n] for n in ns)) for b, ns in buckets.items()}

    def pack_shards(b):
        packed = _pack_rows([_wire_shard(w2[n], n).astype(BF16) for n in buckets[b]], rows_of[b])
        return packed.reshape(2, rows_of[b] // 2, PACK_W)

    def unpack_full(gathered, b):
        wg, out, off = gathered.reshape(N_CHIPS, rows_of[b], PACK_W), {}, 0
        for n in buckets[b]:
            rows, cols = wire_shapes[n]
            out[_wire_name(n)] = wg[:, off:off + true_rows[n]].reshape(N_CHIPS * rows, cols)
            off += seg_rows[n]
        return out

    def pack_grads(g, b):
        sections = [_pack_rows([g[_wire_name(n)].reshape(N_CHIPS, true_rows[n], PACK_W)[k] for n in buckets[b]],
                               rows_of[b]) for k in range(N_CHIPS)]
        return jnp.stack(sections).reshape(N_CHIPS, 2, rows_of[b] // 2, PACK_W).transpose(1, 0, 2, 3)

    def chip_sums(g, b):
        g2 = pack_grads(g, b)
        return _add_own_half(g2, _swap_halves(g2, b), c, b)

    def finish(by_source, b):
        reduced = _join_halves(_sum_chips(by_source, b), b).reshape(rows_of[b], PACK_W)
        out, off = {}, 0
        for n in buckets[b]:
            out[n] = reduced[off:off + true_rows[n]].reshape(wire_shapes[n])
            off += seg_rows[n]
        return out

    full = {"w_in_t": _gather_weights(pack_shards("first")).reshape(N_CHIPS, rows_of["first"], PACK_W)}
    for n in SMALL:
        full[n] = w2[n]
    overlap = _Overlap(_gather_side(pack_shards("late")), lambda outs: unpack_full(outs[0], "late"),
                       lambda g: _scatter_side(chip_sums(g, "late")), lambda g: _scatter_side(chip_sums(g, "first")))

    n_conv = w2["conv_w"].shape[1]
    placed = lax.dynamic_update_slice_in_dim(jnp.zeros((CONV_K, N_CHIPS * n_conv), F32), w2["conv_w"], chip * n_conv, 1)
    placed = jnp.where(c == 0, placed, 0.0)
    full["conv_w"] = _allreduce_small(_pack_rows([placed], _rows(int(placed.size))), "gather_conv_w").reshape(
        -1)[:placed.size].reshape(placed.shape)

    loss_sum, grad_x, g_full, scattered, scattered_in = _local_step(x, loss_target, full, overlap)
    loss = lax.psum(loss_sum, ("x", "y", "c"))

    g_shard = {}
    small_names = SMALL + ("conv_w",)
    small_flat = jnp.concatenate([g_full[n].reshape(-1) for n in small_names])
    small = _allreduce_small(_pack_rows([small_flat], _rows(int(small_flat.size))), "allreduce_small").reshape(-1)
    off = 0
    for n in small_names:
        size = int(g_full[n].size)
        g_shard[n] = small[off:off + size].reshape(g_full[n].shape)
        off += size
    g_shard["conv_w"] = lax.dynamic_slice_in_dim(g_shard["conv_w"], chip * n_conv, n_conv, 1)

    g_shard.update(finish(scattered[0], "late"))
    g_shard.update(finish(scattered_in[0], "first"))

    grads, deltas, new_m, new_v = [], [], [], []
    for n in names:
        shape = w_loc[n].shape
        if n in COL_SHARDED:
            view = unview = lambda a: jnp.swapaxes(a, -1, -2)
        else:
            view, unview = ((lambda a: a) if len(shape) >= 2 else two_d), (lambda a: a.reshape(shape))
        gn = g_shard[n].reshape(view(w_loc[n]).shape)
        outs = _adamw(view(w_loc[n]), gn, view(m_loc[n]), view(v_loc[n]), "adamw_" + n)
        for acc, a in zip((grads, deltas, new_m, new_v), (gn, *outs)):
            acc.append(unview(a))
    return (loss, grad_x, *grads, *deltas, *new_m, *new_v)
```

```python
import functools
from typing import Callable, NamedTuple, Optional

import jax
import jax.numpy as jnp
from jax import lax
from jax.experimental import pallas as pl
from jax.experimental.pallas import tpu as pltpu

F32 = jnp.float32
BF16 = jnp.bfloat16
SDS = jax.ShapeDtypeStruct
MESH = pl.DeviceIdType.MESH

D_MODEL = 1024
D_INNER = 2048
N_HEADS = 32
HEAD_P = 64
N_GROUPS = 4
HEADS_PER_GROUP = N_HEADS // N_GROUPS
D_STATE = 128
CONV_K = 4
CHUNK = 128
CONV_DIM = D_INNER + 2 * N_GROUPS * D_STATE
GROUP_W = D_INNER // N_GROUPS + 2 * D_STATE
ATT_HEADS = 12
ATT_D = 128
ATT_SLOTS = 4
ATT_W = ATT_SLOTS * ATT_D
ATT_DILATIONS = (1, 4, 16)
ATT_BLOCK = 128
QKV_DIM = 3 * ATT_HEADS * ATT_D
D_FF = 2816
DT_PAD = 128
ROPE_THETA = 10000.0
EPS = 1e-6
N_CHIPS = 4
LANES = 128

ADAM_LR = 0.001
ADAM_B1 = 0.9
ADAM_B2 = 0.999
ADAM_EPS = 1e-08
ADAM_WD = 0.01
ADAM_STEP = 10

VMEM_LIMIT = 48 * 1024 * 1024


def _cparams(semantics):
    return pltpu.CompilerParams(dimension_semantics=semantics, vmem_limit_bytes=VMEM_LIMIT)


def _pick(n, cap):
    best = None
    for t in range(LANES, min(n, cap) + 1, LANES):
        if n % t == 0:
            best = t
    return best or n


def _row_tile(rows, cap):
    best = None
    for t in range(8, min(rows, cap) + 1, 8):
        if rows % t == 0:
            best = t
    return best or rows


def _sigmoid(x):
    return pl.reciprocal(1.0 + jnp.exp(-x), approx=True)


def _softplus(x):
    return jnp.maximum(x, 0.0) + jnp.log(1.0 + jnp.exp(-jnp.abs(x)))


def _dot(a, b):
    return jnp.dot(a, b, preferred_element_type=F32)


def _dot_nt(a, b):
    return lax.dot_general(a, b, (((1,), (1,)), ((), ())), preferred_element_type=F32)


def _dot_tn(a, b):
    return lax.dot_general(a, b, (((0,), (0,)), ((), ())), preferred_element_type=F32)


def _mm(a, b, mode, out_dtype, name, add=None, side=None):
    if mode == "nn":
        (m, k), (_, n) = a.shape, b.shape
    elif mode == "nt":
        (m, k), (n, _) = a.shape, b.shape
    else:
        (k, m), (_, n) = a.shape, b.shape
    tm, tn = _pick(m, 1536), _pick(n, 2048)
    tk = k if k <= 2048 else _pick(k, 2048)
    nk = k // tk
    dims = {"nn": ((1,), (0,)), "nt": ((1,), (1,)), "tn": ((0,), (0,))}[mode]

    def partial_product(a_ref, b_ref):
        return lax.dot_general(a_ref[...].astype(BF16), b_ref[...].astype(BF16), (dims, ((), ())),
                               preferred_element_type=F32)

    def body(*refs):
        a_ref, b_ref = refs[:2]
        c_ref = refs[2] if add is not None else None
        o_ref = refs[3] if add is not None else refs[2]

        def finish(r):
            if add is not None:
                r = r + c_ref[...].astype(F32)
            o_ref[...] = r.astype(out_dtype)

        if nk == 1:
            finish(partial_product(a_ref, b_ref))
            return
        acc = refs[-1]
        kk = pl.program_id(2)

        @pl.when(kk == 0)
        def _():
            acc[...] = partial_product(a_ref, b_ref)

        @pl.when((kk > 0) & (kk < nk - 1))
        def _():
            acc[...] += partial_product(a_ref, b_ref)

        @pl.when(kk == nk - 1)
        def _():
            finish(acc[...] + partial_product(a_ref, b_ref))

    a_spec = {"nn": pl.BlockSpec((tm, tk), lambda j, i, q: (i, q)),
              "nt": pl.BlockSpec((tm, tk), lambda j, i, q: (i, q)),
              "tn": pl.BlockSpec((tk, tm), lambda j, i, q: (q, i))}[mode]
    b_spec = {"nn": pl.BlockSpec((tk, tn), lambda j, i, q: (q, j)),
              "nt": pl.BlockSpec((tn, tk), lambda j, i, q: (j, q)),
              "tn": pl.BlockSpec((tk, tn), lambda j, i, q: (q, j))}[mode]
    o_spec = pl.BlockSpec((tm, tn), lambda j, i, q: (i, j))
    ins, specs = [a, b], [a_spec, b_spec]
    if add is not None:
        ins.append(add)
        specs.append(o_spec)
    acc = [pltpu.VMEM((tm, tn), F32)] if nk > 1 else []
    grid = (n // tn, m // tm, nk)
    if side is None:
        return pl.pallas_call(
            body, name=name, grid=grid, in_specs=specs, out_specs=o_spec, out_shape=SDS((m, n), out_dtype),
            scratch_shapes=acc, compiler_params=_cparams(("parallel", "parallel", "arbitrary")))(*ins)
    return pl.pallas_call(
        _attach_side(body, len(ins), 1, side, grid), name=name, grid=grid,
        in_specs=specs + [ANY] * len(side.ins), out_specs=[o_spec] + [ANY] * len(side.out_shapes),
        out_shape=[SDS((m, n), out_dtype)] + list(side.out_shapes), scratch_shapes=acc + list(side.scratch),
        compiler_params=_cparams(("arbitrary", "arbitrary", "arbitrary")))(*ins, *side.ins)


def _mm_fused(a, b, mode, name, tm, epilogue, row_ins, full_ins, outs, n_acc=0, add=None):
    (m, k), n = a.shape, (b.shape[1] if mode == "nn" else b.shape[0])
    tk = k if k <= 2048 else _pick(k, 2048)
    nk = k // tk
    dims = {"nn": ((1,), (0,)), "nt": ((1,), (1,))}[mode]
    n_row, n_full, n_out = len(row_ins), len(full_ins), len(outs)

    def partial_product(a_ref, b_ref):
        return lax.dot_general(a_ref[...], b_ref[...], (dims, ((), ())), preferred_element_type=F32)

    def body(*refs):
        a_ref, b_ref = refs[:2]
        pos = 3 if add is not None else 2
        row_refs, full_refs = refs[pos:pos + n_row], refs[pos + n_row:pos + n_row + n_full]
        out_refs = refs[pos + n_row + n_full:pos + n_row + n_full + n_out]
        i, kk = pl.program_id(0), pl.program_id(1)

        def finish(r):
            if add is not None:
                r = r + refs[2][...].astype(F32)
            for q, (o_ref, v) in enumerate(zip(out_refs, epilogue(r, row_refs, full_refs))):
                if q < n_out - n_acc:
                    o_ref[...] = v.astype(o_ref.dtype)
                else:
                    @pl.when(i == 0)
                    def _(o_ref=o_ref, v=v):
                        o_ref[...] = v

                    @pl.when(i > 0)
                    def _(o_ref=o_ref, v=v):
                        o_ref[...] += v

        if nk == 1:
            finish(partial_product(a_ref, b_ref))
            return
        acc = refs[-1]

        @pl.when(kk == 0)
        def _():
            acc[...] = partial_product(a_ref, b_ref)

        @pl.when((kk > 0) & (kk < nk - 1))
        def _():
            acc[...] += partial_product(a_ref, b_ref)

        @pl.when(kk == nk - 1)
        def _():
            finish(acc[...] + partial_product(a_ref, b_ref))

    tile = lambda w, cb: pl.BlockSpec((tm, w), lambda i, q: (i, cb))
    b_spec = (pl.BlockSpec((tk, n), lambda i, q: (q, 0)) if mode == "nn" else pl.BlockSpec((n, tk), lambda i, q: (0, q)))
    specs = [pl.BlockSpec((tm, tk), lambda i, q: (i, q)), b_spec] + ([tile(n, 0)] if add is not None else [])
    specs += [tile(w, cb) for _, w, cb in row_ins]
    vec = lambda w, cb: pl.BlockSpec((1, w), lambda i, q: (0, cb))
    specs += [vec(w, cb) for _, w, cb in full_ins]
    out_specs = [tile(w, 0) for w, _ in outs[:n_out - n_acc]] + [vec(w, 0) for w, _ in outs[n_out - n_acc:]]
    out_shape = [SDS((m, w), dt) for w, dt in outs[:n_out - n_acc]] + [SDS((1, w), F32) for w, _ in outs[n_out - n_acc:]]
    ins = [a, b] + ([add] if add is not None else []) + [x for x, _, _ in row_ins] + [x for x, _, _ in full_ins]
    return pl.pallas_call(
        body, name=name, grid=(m // tm, nk), in_specs=specs, out_specs=out_specs, out_shape=out_shape,
        scratch_shapes=[pltpu.VMEM((tm, n), F32)] if nk > 1 else [],
        compiler_params=_cparams(("arbitrary", "arbitrary")))(*ins)


def _rw(name, fn, nsteps, ins, outs, n_acc=0):
    n_in, n_out = len(ins), len(outs)

    def body(*refs):
        i = pl.program_id(0)
        vals = fn(i, *refs[:n_in])
        for q, (r, v) in enumerate(zip(refs[n_in:], vals)):
            if q < n_out - n_acc:
                r[...] = v.astype(r.dtype)
            else:
                @pl.when(i == 0)
                def _(r=r):
                    r[...] = jnp.zeros_like(r)

                r[...] += v

    return pl.pallas_call(
        body, name=name, grid=(nsteps,), in_specs=[s for _, s in ins], out_specs=[s for _, s in outs],
        out_shape=[o for o, _ in outs], compiler_params=_cparams(("arbitrary",)))(*[a for a, _ in ins])


def _rs(tm, w, cb=0):
    return pl.BlockSpec((tm, w), lambda i: (i, cb))


def _fs(shape):
    nd = len(shape)
    return pl.BlockSpec(shape, lambda i: (0,) * nd)


def _colsum(v):
    return jnp.sum(v, axis=0, keepdims=True)


def _rms_fwd(x, g, name):
    t, d = x.shape
    tm = 512

    def fn(i, x_ref, g_ref):
        xv = x_ref[...]
        r = lax.rsqrt(jnp.mean(xv * xv, axis=-1, keepdims=True) + EPS)
        return [xv * r * g_ref[...]]

    return _rw(name, fn, t // tm, [(x, _rs(tm, d)), (g, _fs((1, d)))], [(SDS((t, d), BF16), _rs(tm, d))])[0]


def _rms_bwd_values(xv, dhv, gv, dres):
    r = lax.rsqrt(jnp.mean(xv * xv, axis=-1, keepdims=True) + EPS)
    xhat = xv * r
    dxhat = dhv * gv
    dx = dres + r * (dxhat - xhat * jnp.mean(dxhat * xhat, axis=-1, keepdims=True))
    return [dx, dx, _colsum(dhv * xhat)]


def _final_values(xv, target, gv):
    d = xv.shape[-1]
    r = lax.rsqrt(jnp.mean(xv * xv, axis=-1, keepdims=True) + EPS)
    xhat = xv * r
    diff = xhat * gv - target
    lsum = 0.5 * jnp.sum(jnp.sum(diff * diff, axis=-1, keepdims=True) * (1.0 / d), axis=0, keepdims=True)
    dy = diff * (1.0 / d)
    dxhat = dy * gv
    dx = r * (dxhat - xhat * jnp.mean(dxhat * xhat, axis=-1, keepdims=True))
    return [dx, dx, _colsum(dy * xhat), lsum]


CONV_TS = 512
CONV_HALO = 8


def _conv_specs(seq, c):
    ts, tc = CONV_TS, GROUP_W
    hb = ts // CONV_HALO
    u_spec = pl.BlockSpec((ts, tc), lambda j, i: (i, j))
    prev_spec = pl.BlockSpec((CONV_HALO, tc), lambda j, i: (jnp.maximum(i * hb - 1, 0), j))
    w_spec = pl.BlockSpec((CONV_K, tc), lambda j, i: (0, j))
    b_spec = pl.BlockSpec((1, tc), lambda j, i: (0, j))
    return u_spec, prev_spec, w_spec, b_spec


CONV_PIECE = 32


def _conv_fill(i, seq, u_ref, prev_ref, ext):
    first = (i % (seq // CONV_TS)) == 0
    ext[0:CONV_HALO, :] = jnp.where(first, 0.0, prev_ref[...])
    ext[CONV_HALO:, :] = u_ref[...]


def _conv_piece(ext, r0, wv, bv):
    lo = r0 + CONV_HALO - CONV_K + 1
    taps = [ext[lo + q:lo + q + CONV_PIECE, :] for q in range(CONV_K)]
    pre = bv
    for q, tap in enumerate(taps):
        pre = pre + wv[q:q + 1] * tap
    return taps, pre


def _conv_fwd(u, w, b, seq):
    t, c = u.shape
    ts, tc = CONV_TS, GROUP_W
    u_spec, prev_spec, w_spec, b_spec = _conv_specs(seq, c)

    def body(u_ref, prev_ref, w_ref, b_ref, o_ref, ext):
        _conv_fill(pl.program_id(1), seq, u_ref, prev_ref, ext)
        wv, bv = w_ref[...], b_ref[...]
        for r0 in range(0, ts, CONV_PIECE):
            _, pre = _conv_piece(ext, r0, wv, bv)
            o_ref[r0:r0 + CONV_PIECE, :] = pre * _sigmoid(pre)

    return pl.pallas_call(
        body, name="conv_fwd", grid=(c // tc, t // ts), in_specs=[u_spec, prev_spec, w_spec, b_spec],
        out_specs=u_spec, out_shape=SDS((t, c), F32), scratch_shapes=[pltpu.VMEM((ts + CONV_HALO, tc), F32)],
        compiler_params=_cparams(("parallel", "arbitrary")))(u, u, w, b)


def _conv_bwd_pre(u, w, b, dxc, seq):
    t, c = u.shape
    ts, tc = CONV_TS, GROUP_W
    u_spec, prev_spec, w_spec, b_spec = _conv_specs(seq, c)

    def body(u_ref, prev_ref, w_ref, b_ref, d_ref, dpre_ref, dw_ref, db_ref, ext):
        i = pl.program_id(1)
        _conv_fill(i, seq, u_ref, prev_ref, ext)
        wv, bv = w_ref[...], b_ref[...]
        sums = [jnp.zeros((1, tc), F32)] * (CONV_K + 1)
        for r0 in range(0, ts, CONV_PIECE):
            taps, pre = _conv_piece(ext, r0, wv, bv)
            sg = _sigmoid(pre)
            dpre = d_ref[r0:r0 + CONV_PIECE, :] * sg * (1.0 + pre * (1.0 - sg))
            dpre_ref[r0:r0 + CONV_PIECE, :] = dpre
            sums = [s + _colsum(dpre * f) for s, f in zip(sums, taps + [1.0])]

        @pl.when(i == 0)
        def _():
            dw_ref[...] = jnp.zeros_like(dw_ref)
            db_ref[...] = jnp.zeros_like(db_ref)

        db_ref[...] += sums[CONV_K]
        for q in range(CONV_K):
            dw_ref[q:q + 1, :] += sums[q]

    return pl.pallas_call(
        body, name="conv_bwd_pre", grid=(c // tc, t // ts),
        in_specs=[u_spec, prev_spec, w_spec, b_spec, u_spec], out_specs=[u_spec, w_spec, b_spec],
        out_shape=[SDS((t, c), F32), SDS((CONV_K, c), F32), SDS((1, c), F32)],
        scratch_shapes=[pltpu.VMEM((ts + CONV_HALO, tc), F32)],
        compiler_params=_cparams(("parallel", "arbitrary")))(u, u, w, b, dxc)


def _conv_bwd_in(dpre, w, seq):
    t, c = dpre.shape
    ts, tc = CONV_TS, GROUP_W
    hb = ts // CONV_HALO
    last = t // CONV_HALO - 1
    d_spec = pl.BlockSpec((ts, tc), lambda j, i: (i, j))
    next_spec = pl.BlockSpec((CONV_HALO, tc), lambda j, i: (jnp.minimum((i + 1) * hb, last), j))
    w_spec = pl.BlockSpec((CONV_K, tc), lambda j, i: (0, j))

    def body(d_ref, next_ref, w_ref, o_ref, ext):
        i = pl.program_id(1)
        nts = seq // ts
        is_last = (i % nts) == nts - 1
        ext[0:ts, :] = d_ref[...]
        ext[ts:, :] = jnp.where(is_last, 0.0, next_ref[...])
        wv = w_ref[...]
        for r0 in range(0, ts, CONV_PIECE):
            acc = wv[CONV_K - 1:CONV_K] * ext[r0:r0 + CONV_PIECE, :]
            for q in range(CONV_K - 1):
                lo = r0 + CONV_K - 1 - q
                acc = acc + wv[q:q + 1] * ext[lo:lo + CONV_PIECE, :]
            o_ref[r0:r0 + CONV_PIECE, :] = acc.astype(o_ref.dtype)

    return pl.pallas_call(
        body, name="conv_bwd_in", grid=(c // tc, t // ts), in_specs=[d_spec, next_spec, w_spec],
        out_specs=d_spec, out_shape=SDS((t, c), BF16), scratch_shapes=[pltpu.VMEM((ts + CONV_HALO, tc), F32)],
        compiler_params=_cparams(("parallel", "arbitrary")))(dpre, dpre, w)


def _split3(v):
    hi = v.astype(BF16)
    r1 = v - hi.astype(F32)
    mid = r1.astype(BF16)
    lo = (r1 - mid.astype(F32)).astype(BF16)
    return hi, mid, lo


def _ssd_prelude(dtr_ref, dtrt_ref, bias_ref, biast_ref, a_ref, at_ref):
    dt = _softplus(dtr_ref[...] + bias_ref[...])
    dtt = _softplus(dtrt_ref[...] + biast_ref[...])
    ri = lax.broadcasted_iota(jnp.int32, (CHUNK, CHUNK), 0)
    ci = lax.broadcasted_iota(jnp.int32, (CHUNK, CHUNK), 1)
    lower = ri >= ci
    upper = ri <= ci
    lower_b = jnp.where(lower, 1.0, 0.0).astype(BF16)
    upper_b = jnp.where(upper, 1.0, 0.0).astype(BF16)
    acs = sum(_dot(lower_b, p) for p in _split3(dt * a_ref[...]))
    acst = sum(_dot(p, upper_b) for p in _split3(dtt * at_ref[...]))
    return dt, acs, acst, lower, upper, lower_b, upper_b


SSD_FWD_GPS = 2
SSD_BWD_GPS = 1


def _ssd_specs(seq, gps):
    nc = seq // CHUNK
    hg = HEADS_PER_GROUP
    fwd = lambda c: c
    rev = lambda c: nc - 1 - c

    def specs(cc):
        return dict(
            xc=pl.BlockSpec((CHUNK, gps * GROUP_W), lambda g, b, c: (b * nc + cc(c), g)),
            y=pl.BlockSpec((CHUNK, gps * hg * HEAD_P), lambda g, b, c: (b * nc + cc(c), g)),
            dtr=pl.BlockSpec((gps, CHUNK, hg), lambda g, b, c: (g, b * nc + cc(c), 0)),
            dtrt=pl.BlockSpec((gps, None, hg, CHUNK), lambda g, b, c: (g, b, 0, cc(c))),
            prow=pl.BlockSpec((gps, 1, hg), lambda g, b, c: (g, 0, 0)),
            pcol=pl.BlockSpec((gps, hg, 1), lambda g, b, c: (g, 0, 0)),
            st=pl.BlockSpec((gps, None, None, D_STATE, hg * HEAD_P), lambda g, b, c: (g, b, cc(c), 0, 0)),
        )

    return specs(fwd), specs(rev)


def _group_views(refs, lane_widths, gi):
    return [r.at[:, gi * w:(gi + 1) * w] if w else r.at[gi] for r, w in zip(refs, lane_widths)]


def _head_maps():
    hw = HEADS_PER_GROUP * HEAD_P
    shift = HEAD_P.bit_length() - 1
    hj = lax.broadcasted_iota(jnp.int32, (HEADS_PER_GROUP, hw), 0)
    lq = jnp.right_shift(lax.broadcasted_iota(jnp.int32, (HEADS_PER_GROUP, hw), 1), shift)
    spread = jnp.where(hj == lq, 1.0, 0.0).astype(BF16)
    rq = jnp.right_shift(lax.broadcasted_iota(jnp.int32, (hw, LANES), 0), shift)
    cj = lax.broadcasted_iota(jnp.int32, (hw, LANES), 1)
    gather = jnp.where(rq == cj, 1.0, 0.0).astype(BF16)
    return spread, gather


def _dot01(v, m01):
    hi, mid, _ = _split3(v)
    return _dot(hi, m01) + _dot(mid, m01)


class _Side(NamedTuple):
    ins: tuple
    out_shapes: tuple
    scratch: tuple
    first: Callable
    mid: Optional[Callable]
    last: Callable


NO_SIDE = _Side((), (), (), lambda *refs: None, None, lambda *refs: None)


def _attach_side(body, n_in, n_out, side, grid):
    si, so, ss = len(side.ins), len(side.out_shapes), len(side.scratch)

    def wrapped(*refs):
        ins, s_in = refs[:n_in], refs[n_in:n_in + si]
        outs = refs[n_in + si:n_in + si + n_out]
        s_out = refs[n_in + si + n_out:n_in + si + n_out + so]
        rest = refs[n_in + si + n_out + so:]
        scr, s_scr = rest[:len(rest) - ss], rest[len(rest) - ss:]
        ids = [pl.program_id(a) for a in range(len(grid))]
        inner_first = functools.reduce(lambda p, q: p & q, [i == 0 for i in ids[1:]], ids[0] >= 0)
        at_last = functools.reduce(lambda p, q: p & q, [i == n - 1 for i, n in zip(ids, grid)])

        @pl.when((ids[0] == 0) & inner_first)
        def _():
            side.first(s_in, s_out, s_scr)

        if side.mid is not None:
            outer_last = functools.reduce(lambda p, q: p & q, [i == n - 1 for i, n in zip(ids[:-1], grid[:-1])])

            @pl.when(outer_last & (ids[-1] == 0))
            def _():
                side.mid(s_in, s_out, s_scr)

        body(*ins, *outs, *scr)

        @pl.when(at_last)
        def _():
            side.last(s_in, s_out, s_scr)

    return wrapped


def _ssd_fwd(xc, dtr, dtrt, bias, biast, a, at, dskip, nb, seq, side):
    t = xc.shape[0]
    nc = seq // CHUNK
    hg = HEADS_PER_GROUP
    hw = hg * HEAD_P
    gps = SSD_FWD_GPS
    grid = (N_GROUPS // gps, nb, nc)
    sp, _ = _ssd_specs(seq, gps)

    def body(*refs):
        for gi in range(gps):
            one_group(*_group_views(refs, (GROUP_W, 0, 0, 0, 0, 0, 0, 0, hw, 0, 0), gi))

    def one_group(xc_ref, dtr_ref, dtrt_ref, bias_ref, biast_ref, a_ref, at_ref, d_ref, y_ref, sin_ref, st):
        @pl.when(pl.program_id(2) == 0)
        def _():
            st[...] = jnp.zeros_like(st)

        s_in = st[...]
        sin_ref[...] = s_in
        dt, acs, acst, lower, _, _, _ = _ssd_prelude(dtr_ref, dtrt_ref, bias_ref, biast_ref, a_ref, at_ref)
        spread, _ = _head_maps()
        x = xc_ref[...]
        xs = x[:, :hw]
        b16 = x[:, hw:hw + D_STATE].astype(BF16)
        c16 = x[:, hw + D_STATE:].astype(BF16)
        cb = _dot_nt(c16, b16)
        last = acs[CHUNK - 1:CHUNK, :]
        e_x = _dot01(jnp.exp(acs), spread)
        dec_x = _dot01(jnp.exp(last - acs), spread)
        tot_x = e_x[CHUNK - 1:CHUNK, :]
        d_x = _dot01(jnp.broadcast_to(d_ref[...], (8, hg)), spread)[0:1, :]
        xdtf = xs * _dot01(dt, spread)
        xdt16 = xdtf.astype(BF16)
        yoff = e_x * _dot(c16, s_in.astype(BF16))
        st[...] = tot_x * s_in + _dot_tn(b16, (dec_x * xdtf).astype(BF16))
        parts = []
        for j in range(hg):
            decay = jnp.exp(jnp.where(lower, acs[:, j:j + 1] - acst[j:j + 1, :], -jnp.inf))
            parts.append(_dot((cb * decay).astype(BF16), xdt16[:, HEAD_P * j:HEAD_P * (j + 1)]))
        y_ref[...] = (jnp.concatenate(parts, axis=-1) + yoff + d_x * xs).astype(y_ref.dtype)

    return pl.pallas_call(
        _attach_side(body, 8, 2, side, grid), name="ssd_fwd", grid=grid,
        in_specs=[sp["xc"], sp["dtr"], sp["dtrt"], sp["prow"], sp["pcol"], sp["prow"], sp["pcol"], sp["prow"]]
        + [ANY] * len(side.ins),
        out_specs=[sp["y"], sp["st"]] + [ANY] * len(side.out_shapes),
        out_shape=[SDS((t, D_INNER), BF16), SDS((N_GROUPS, nb, nc, D_STATE, hw), F32)] + list(side.out_shapes),
        scratch_shapes=[pltpu.VMEM((gps, D_STATE, hw), F32)] + list(side.scratch),
        compiler_params=_cparams(("arbitrary", "arbitrary", "arbitrary")))(
            xc, dtr, dtrt, bias, biast, a, at, dskip, *side.ins)


def _ssd_bwd(xc, dtr, dtrt, bias, biast, a, at, dskip, states, dy, nb, seq, side):
    t = xc.shape[0]
    nc = seq // CHUNK
    hg = HEADS_PER_GROUP
    hw = hg * HEAD_P
    gps = SSD_BWD_GPS
    grid = (N_GROUPS // gps, nb, nc)
    _, sp = _ssd_specs(seq, gps)

    def body(*refs):
        for gi in range(gps):
            one_group(*_group_views(refs, (GROUP_W, 0, 0, 0, 0, 0, 0, 0, 0, hw, GROUP_W, 0, 0, 0, 0, 0), gi))

    def one_group(xc_ref, dtr_ref, dtrt_ref, bias_ref, biast_ref, a_ref, at_ref, d_ref, sin_ref, dy_ref,
                  dxc_ref, ddtr_ref, gbias_ref, ga_ref, gd_ref, ds):
        first = (pl.program_id(1) == 0) & (pl.program_id(2) == 0)

        @pl.when(pl.program_id(2) == 0)
        def _():
            ds[...] = jnp.zeros_like(ds)

        @pl.when(first)
        def _():
            gbias_ref[...] = jnp.zeros_like(gbias_ref)
            ga_ref[...] = jnp.zeros_like(ga_ref)
            gd_ref[...] = jnp.zeros_like(gd_ref)

        dt, acs, acst, lower, upper, _, upper_b = _ssd_prelude(dtr_ref, dtrt_ref, bias_ref, biast_ref, a_ref, at_ref)
        spread, gather = _head_maps()
        x = xc_ref[...]
        dy = dy_ref[...].astype(F32)
        xs = x[:, :hw]
        b16 = x[:, hw:hw + D_STATE].astype(BF16)
        c16 = x[:, hw + D_STATE:].astype(BF16)
        dy16 = dy.astype(BF16)
        cb = _dot_nt(c16, b16)
        cbt = _dot_nt(b16, c16)
        last = acs[CHUNK - 1:CHUNK, :]
        e8 = jnp.exp(acs)
        dec8 = jnp.exp(last - acs)
        e_x = _dot01(e8, spread)
        dec_x = _dot01(dec8, spread)
        tot_x = e_x[CHUNK - 1:CHUNK, :]
        dt_x = _dot01(dt, spread)
        d_x = _dot01(jnp.broadcast_to(d_ref[...], (8, hg)), spread)[0:1, :]
        xdtf = xs * dt_x
        xdt16 = xdtf.astype(BF16)
        s_in = sin_ref[...]
        s16 = s_in.astype(BF16)
        ds_out = ds[...]
        ds16 = ds_out.astype(BF16)
        bds = _dot(b16, ds16)
        cs = _dot(c16, s16)
        edy16 = (e_x * dy).astype(BF16)
        ds[...] = tot_x * ds_out + _dot_tn(c16, edy16)
        lane8 = lax.broadcasted_iota(jnp.int32, (CHUNK, hg), 1)
        row8 = lax.broadcasted_iota(jnp.int32, (CHUNK, hg), 0)
        dacs8 = jnp.zeros((CHUNK, hg), F32)
        acc_m = jnp.zeros((CHUNK, CHUNK), F32)
        acc_mt = jnp.zeros((CHUNK, CHUNK), F32)
        dx_parts = []
        for j in range(hg):
            sl = slice(HEAD_P * j, HEAD_P * (j + 1))
            col = acs[:, j:j + 1]
            row = acst[j:j + 1, :]
            decay = jnp.exp(jnp.where(lower, col - row, -jnp.inf))
            decayt = jnp.exp(jnp.where(upper, row - col, -jnp.inf))
            wm = _dot_nt(dy16[:, sl], xdt16[:, sl]) * decay
            wmt = _dot_nt(xdt16[:, sl], dy16[:, sl]) * decayt
            acc_m = acc_m + wm
            acc_mt = acc_mt + wmt
            dacs8 = dacs8 + jnp.where(lane8 == j, jnp.sum(wm * cb, axis=-1, keepdims=True)
                                      - jnp.sum(wmt * cbt, axis=-1, keepdims=True), 0.0)
            dx_parts.append(_dot((cbt * decayt).astype(BF16), dy16[:, sl]))
        dx = jnp.concatenate(dx_parts, axis=-1) + dec_x * bds
        dxc_ref[:, :hw] = dx * dt_x + d_x * dy
        dxc_ref[:, hw:hw + D_STATE] = _dot(acc_mt.astype(BF16), c16) + _dot_nt((dec_x * xdtf).astype(BF16), ds16)
        dxc_ref[:, hw + D_STATE:] = _dot(acc_m.astype(BF16), b16) + _dot_nt(edy16, s16)
        dtot_rows = jnp.broadcast_to(_colsum(ds_out * s_in), (8, hw))
        sums = _dot01(jnp.concatenate([dy * cs, xdtf * bds, dx * xs, dy * xs, dtot_rows], axis=0), gather)
        de8 = sums[0:CHUNK, :hg]
        ddec8 = sums[CHUNK:2 * CHUNK, :hg]
        ddtx8 = sums[2 * CHUNK:3 * CHUNK, :hg]
        gd8 = _colsum(sums[3 * CHUNK:4 * CHUNK, :hg])
        dtot8 = sums[4 * CHUNK:4 * CHUNK + 1, :hg]
        extra = _colsum(ddec8 * dec8) + dtot8 * e8[CHUNK - 1:CHUNK, :]
        dacs8 = dacs8 + de8 * e8 - ddec8 * dec8 + jnp.where(row8 == CHUNK - 1, extra, 0.0)
        da = sum(_dot(upper_b, p) for p in _split3(dacs8))
        av = a_ref[...]
        ddt = da * av + ddtx8
        ddtr = ddt * _sigmoid(dtr_ref[...] + bias_ref[...])
        ddtr_ref[...] = ddtr
        gbias_ref[...] += _colsum(ddtr)
        ga_ref[...] += _colsum(da * dt) * av
        gd_ref[...] += gd8

    return pl.pallas_call(
        _attach_side(body, 10, 5, side, grid), name="ssd_bwd", grid=grid,
        in_specs=[sp["xc"], sp["dtr"], sp["dtrt"], sp["prow"], sp["pcol"], sp["prow"], sp["pcol"], sp["prow"],
                  sp["st"], sp["y"]] + [ANY] * len(side.ins),
        out_specs=[sp["xc"], sp["dtr"], sp["prow"], sp["prow"], sp["prow"]] + [ANY] * len(side.out_shapes),
        out_shape=[SDS((t, N_GROUPS * GROUP_W), F32), SDS((N_GROUPS, t, hg), F32)]
        + [SDS((N_GROUPS, 1, hg), F32)] * 3 + list(side.out_shapes),
        scratch_shapes=[pltpu.VMEM((gps, D_STATE, hw), F32)] + list(side.scratch),
        compiler_params=_cparams(("arbitrary", "arbitrary", "arbitrary")))(
            xc, dtr, dtrt, bias, biast, a, at, dskip, states, dy, *side.ins)


def _group_bcast(v, width, fn):
    parts = []
    for q in range(v.shape[-1] // width):
        s = fn(v[:, q * width:(q + 1) * width])
        parts.append(jnp.broadcast_to(s, (v.shape[0], width)))
    return jnp.concatenate(parts, axis=-1)


def _gate_norm_fwd(y, z, g):
    t, d = y.shape
    tm = 256
    gw = d // N_GROUPS

    def fn(i, y_ref, z_ref, g_ref):
        zv = z_ref[...].astype(F32)
        u = y_ref[...].astype(F32) * (zv * _sigmoid(zv))
        r = lax.rsqrt(_group_bcast(u * u, gw, lambda p: jnp.mean(p, axis=-1, keepdims=True)) + EPS)
        return [u * r * g_ref[...]]

    return _rw("gate_norm_fwd", fn, t // tm, [(y, _rs(tm, d)), (z, _rs(tm, d)), (g, _fs((1, d)))],
               [(SDS((t, d), BF16), _rs(tm, d))])[0]


def _gate_norm_bwd_epilogue(dv, rows, fulls):
    yv, zv = rows[0][...].astype(F32), rows[1][...].astype(F32)
    gw = yv.shape[-1] // N_GROUPS
    sg = _sigmoid(zv)
    sz = zv * sg
    u = yv * sz
    r = lax.rsqrt(_group_bcast(u * u, gw, lambda p: jnp.mean(p, axis=-1, keepdims=True)) + EPS)
    uhat = u * r
    duhat = dv * fulls[0][...]
    du = r * (duhat - uhat * _group_bcast(duhat * uhat, gw, lambda p: jnp.mean(p, axis=-1, keepdims=True)))
    dz = du * yv * sg * (1.0 + zv * (1.0 - sg))
    return [du * sz, dz, _colsum(dv * uhat)]


def _rope_tables(seq):
    half = ATT_D // 2
    inv = ROPE_THETA ** (-jnp.arange(half, dtype=F32) / half)
    ang = jnp.arange(seq, dtype=F32)[:, None] * inv[None, :]
    cos, sin = jnp.cos(ang), jnp.sin(ang)
    return jnp.concatenate([cos, cos], axis=-1), jnp.concatenate([-sin, sin], axis=-1)


ATT_TILE = 512
ATT_QB = 8


def _strided_spec(r, mtiles):
    return pl.BlockSpec((None, r, None, ATT_TILE // r, ATT_W), lambda i: (i // mtiles, 0, i % mtiles, 0, 0))


def _strided_shape(nb, r, mtiles, dtype):
    return SDS((nb, r, mtiles, ATT_TILE // r, ATT_W), dtype)


def _to_strided(val, out_ref, lanes, r, sc):
    if r == 1:
        out_ref[0, :, lanes] = val.astype(out_ref.dtype)
        return
    sc[...] = val
    for rr in range(r):
        out_ref[rr, :, lanes] = sc[pl.ds(rr, ATT_TILE // r, stride=r), :].astype(out_ref.dtype)


def _from_strided(in_ref, lanes, r, sc):
    if r == 1:
        return in_ref[0, :, lanes].astype(F32)
    for rr in range(r):
        sc[pl.ds(rr, ATT_TILE // r, stride=r), :] = in_ref[rr, :, lanes].astype(F32)
    return sc[...]


def _rope_fwd(qkv, cos, sin, nb, seq):
    t = qkv.shape[0]
    tm = ATT_TILE
    mtiles = seq // tm
    w = ATT_HEADS * ATT_D
    tab = pl.BlockSpec((tm, ATT_D), lambda i: (i % mtiles, 0))
    ng = len(ATT_DILATIONS)

    def body(q_ref, k_ref, v_ref, cos_ref, sin_ref, *rest):
        outs, sc = rest[:3 * ng], rest[3 * ng]
        c, s = cos_ref[...], sin_ref[...]
        for which, ref in enumerate((q_ref, k_ref, v_ref)):
            for h in range(ATT_HEADS):
                g, slot = divmod(h, ATT_SLOTS)
                p = ref[:, h * ATT_D:(h + 1) * ATT_D].astype(F32)
                if which < 2:
                    p = p * c + pltpu.roll(p, ATT_D // 2, 1) * s
                _to_strided(p, outs[which * ng + g], slice(slot * ATT_D, (slot + 1) * ATT_D), ATT_DILATIONS[g], sc)

    out_specs = [_strided_spec(r, mtiles) for _ in range(3) for r in ATT_DILATIONS]
    out_shape = [_strided_shape(nb, r, mtiles, BF16) for _ in range(3) for r in ATT_DILATIONS]
    outs = pl.pallas_call(
        body, name="rope_fwd", grid=(t // tm,),
        in_specs=[_rs(tm, w, 0), _rs(tm, w, 1), _rs(tm, w, 2), tab, tab], out_specs=out_specs, out_shape=out_shape,
        scratch_shapes=[pltpu.VMEM((tm, ATT_D), F32)], compiler_params=_cparams(("arbitrary",)))(
            qkv, qkv, qkv, cos, sin)
    flat = [o.reshape(t, ATT_W) for o in outs]
    return flat[0:ng], flat[ng:2 * ng], flat[2 * ng:]


def _rope_bwd(dq, dk, dv, cos, sin, nb, seq):
    t = dq[0].shape[0]
    tm = ATT_TILE
    mtiles = seq // tm
    w = ATT_HEADS * ATT_D
    tab = pl.BlockSpec((tm, ATT_D), lambda i: (i % mtiles, 0))
    ng = len(ATT_DILATIONS)

    def body(*refs):
        ins, (cos_ref, sin_ref, o_ref, sc) = refs[:3 * ng], refs[3 * ng:]
        c, s = cos_ref[...], sin_ref[...]
        for which in range(3):
            for h in range(ATT_HEADS):
                g, slot = divmod(h, ATT_SLOTS)
                p = _from_strided(ins[which * ng + g], slice(slot * ATT_D, (slot + 1) * ATT_D), ATT_DILATIONS[g], sc)
                if which < 2:
                    p = p * c - pltpu.roll(p, ATT_D // 2, 1) * s
                o_ref[:, which * w + h * ATT_D:which * w + (h + 1) * ATT_D] = p.astype(o_ref.dtype)

    views = [a.reshape(nb, r, mtiles, tm // r, ATT_W) for grp in (dq, dk, dv) for a, r in zip(grp, ATT_DILATIONS)]
    return pl.pallas_call(
        body, name="rope_bwd", grid=(t // tm,),
        in_specs=[_strided_spec(r, mtiles) for _ in range(3) for r in ATT_DILATIONS] + [tab, tab],
        out_specs=_rs(tm, 3 * w), out_shape=SDS((t, 3 * w), BF16),
        scratch_shapes=[pltpu.VMEM((tm, ATT_D), F32)], compiler_params=_cparams(("arbitrary",)))(*views, cos, sin)


def _att_masks():
    ri = lax.broadcasted_iota(jnp.int32, (ATT_BLOCK, ATT_BLOCK), 0)
    ci = lax.broadcasted_iota(jnp.int32, (ATT_BLOCK, ATT_BLOCK), 1)
    return ci <= ri, ci >= ri


def _att_fwd(q, k, v, g, seq):
    t, w = q.shape
    rows = ATT_QB * ATT_BLOCK
    nbs = seq // ATT_DILATIONS[g] // ATT_BLOCK
    scale = ATT_D ** -0.5
    cur = pl.BlockSpec((rows, w), lambda n: (n, 0))
    prev = pl.BlockSpec((ATT_BLOCK, w), lambda n: (jnp.maximum(n * ATT_QB - 1, 0), 0))

    def body(q_ref, kc_ref, kp_ref, vc_ref, vp_ref, o_ref, lse_ref):
        mcur, mprev = _att_masks()
        for i in range(ATT_QB):
            blk = pl.program_id(0) * ATT_QB + i
            mask = jnp.concatenate([mprev & ((blk % nbs) != 0), mcur], axis=-1)
            own = slice(i * ATT_BLOCK, (i + 1) * ATT_BLOCK)
            for h in range(ATT_SLOTS):
                sl = slice(h * ATT_D, (h + 1) * ATT_D)
                if i == 0:
                    keys = jnp.concatenate([kp_ref[:, sl], kc_ref[own, sl]], axis=0)
                    vals = jnp.concatenate([vp_ref[:, sl], vc_ref[own, sl]], axis=0)
                else:
                    both = slice((i - 1) * ATT_BLOCK, (i + 1) * ATT_BLOCK)
                    keys, vals = kc_ref[both, sl], vc_ref[both, sl]
                s = jnp.where(mask, _dot_nt(q_ref[own, sl], keys) * scale, -jnp.inf)
                m = jnp.max(s, axis=-1, keepdims=True)
                p = jnp.exp(s - m)
                den = jnp.sum(p, axis=-1, keepdims=True)
                o_ref[own, sl] = _dot(p.astype(BF16), vals) / den
                lse_ref[own, sl] = jnp.broadcast_to(m + jnp.log(den), (ATT_BLOCK, ATT_D))

    return pl.pallas_call(
        body, name=f"att_fwd_{g}", grid=(t // rows,), in_specs=[cur, cur, prev, cur, prev], out_specs=[cur, cur],
        out_shape=[SDS((t, w), F32), SDS((t, w), F32)],
        compiler_params=_cparams(("arbitrary",)))(q, k, k, v, v)


def _att_bwd(q, k, v, do, lse, dlt, g, seq):
    t, w = q.shape
    nblk = t // ATT_BLOCK
    rows = ATT_QB * ATT_BLOCK
    nbs = seq // ATT_DILATIONS[g] // ATT_BLOCK
    scale = ATT_D ** -0.5
    cur = pl.BlockSpec((rows, w), lambda n: (n, 0))
    nxt = pl.BlockSpec((ATT_BLOCK, w), lambda n: (jnp.minimum((n + 1) * ATT_QB, nblk - 1), 0))

    def body(qc_ref, qn_ref, k_ref, v_ref, doc_ref, don_ref, lsec_ref, lsen_ref, dltc_ref, dltn_ref,
             dq_ref, dk_ref, dv_ref, carry):
        n = pl.program_id(0)

        @pl.when(n == 0)
        def _():
            carry[...] = jnp.zeros_like(carry)

        mcur, mprev = _att_masks()

        def pair(cur_ref, nxt_ref, i, sl):
            if i + 1 < ATT_QB:
                return cur_ref[i * ATT_BLOCK:(i + 2) * ATT_BLOCK, sl]
            return jnp.concatenate([cur_ref[i * ATT_BLOCK:, sl], nxt_ref[:, sl]], axis=0)

        for h in range(ATT_SLOTS):
            sl = slice(h * ATT_D, (h + 1) * ATT_D)
            from_prev = carry[:, sl]
            for i in range(ATT_QB):
                blk = n * ATT_QB + i
                has_next = (((blk + 1) % nbs) != 0) & (blk + 1 < nblk)
                mask = jnp.concatenate([mcur, mprev & has_next], axis=0)
                own = slice(i * ATT_BLOCK, (i + 1) * ATT_BLOCK)
                kh, vh = k_ref[own, sl], v_ref[own, sl]
                qs, dos = pair(qc_ref, qn_ref, i, sl), pair(doc_ref, don_ref, i, sl)
                lse, dlt = pair(lsec_ref, lsen_ref, i, sl), pair(dltc_ref, dltn_ref, i, sl)
                p = jnp.where(mask, jnp.exp(_dot_nt(qs, kh) * scale - lse), 0.0)
                ds = (p * (_dot_nt(dos, vh) - dlt) * scale).astype(BF16)
                dqs = _dot(ds, kh)
                dq_ref[own, sl] = (from_prev + dqs[:ATT_BLOCK]).astype(dq_ref.dtype)
                from_prev = dqs[ATT_BLOCK:]
                dk_ref[own, sl] = _dot_tn(ds, qs).astype(dk_ref.dtype)
                dv_ref[own, sl] = _dot_tn(p.astype(BF16), dos).astype(dv_ref.dtype)
            carry[:, sl] = from_prev

    return pl.pallas_call(
        body, name=f"att_bwd_{g}", grid=(t // rows,), in_specs=[cur, nxt, cur, cur, cur, nxt, cur, nxt, cur, nxt],
        out_specs=[cur, cur, cur], out_shape=[SDS((t, w), BF16)] * 3,
        scratch_shapes=[pltpu.VMEM((ATT_BLOCK, w), F32)],
        compiler_params=_cparams(("arbitrary",)))(q, q, k, v, do, do, lse, lse, dlt, dlt)


def _merge_weights(ls):
    m = jnp.maximum(jnp.maximum(ls[0], ls[1]), ls[2])
    es = [jnp.exp(v - m) for v in ls]
    den = es[0] + es[1] + es[2]
    return [e / den for e in es]


def _merge_fwd(o, lse, nb, seq):
    t = o[0].shape[0]
    tm = ATT_TILE
    mtiles = seq // tm
    ng = len(ATT_DILATIONS)

    def body(*refs):
        o_refs, l_refs, out_ref, scs = refs[:ng], refs[ng:2 * ng], refs[2 * ng], refs[2 * ng + 1:]
        for slot in range(ATT_SLOTS):
            lanes = slice(slot * ATT_D, (slot + 1) * ATT_D)
            ov = [_from_strided(o_refs[g], lanes, r, scs[2 * g]) for g, r in enumerate(ATT_DILATIONS)]
            ws = _merge_weights([_from_strided(l_refs[g], lanes, r, scs[2 * g + 1])
                                 for g, r in enumerate(ATT_DILATIONS)])
            out_ref[:, lanes] = (ws[0] * ov[0] + ws[1] * ov[1] + ws[2] * ov[2]).astype(out_ref.dtype)

    views = [a.reshape(nb, r, mtiles, tm // r, ATT_W) for grp in (o, lse) for a, r in zip(grp, ATT_DILATIONS)]
    return pl.pallas_call(
        body, name="att_merge_fwd", grid=(t // tm,),
        in_specs=[_strided_spec(r, mtiles) for _ in range(2) for r in ATT_DILATIONS],
        out_specs=_rs(tm, ATT_W), out_shape=SDS((t, ATT_W), BF16),
        scratch_shapes=[pltpu.VMEM((tm, ATT_D), F32)] * (2 * ng), compiler_params=_cparams(("arbitrary",)))(*views)


def _merge_bwd(o, lse, datt, nb, seq):
    t = o[0].shape[0]
    tm = ATT_TILE
    mtiles = seq // tm
    ng = len(ATT_DILATIONS)

    def body(*refs):
        o_refs, l_refs, d_ref = refs[:ng], refs[ng:2 * ng], refs[2 * ng]
        do_refs, dlt_refs = refs[2 * ng + 1:3 * ng + 1], refs[3 * ng + 1:4 * ng + 1]
        scs = refs[4 * ng + 1:]
        for slot in range(ATT_SLOTS):
            lanes = slice(slot * ATT_D, (slot + 1) * ATT_D)
            ov = [_from_strided(o_refs[g], lanes, r, scs[2 * g]) for g, r in enumerate(ATT_DILATIONS)]
            ws = _merge_weights([_from_strided(l_refs[g], lanes, r, scs[2 * g + 1])
                                 for g, r in enumerate(ATT_DILATIONS)])
            dv = d_ref[:, lanes]
            att = ws[0] * ov[0] + ws[1] * ov[1] + ws[2] * ov[2]
            dot = jnp.broadcast_to(jnp.sum(dv * att, axis=-1, keepdims=True), (tm, ATT_D))
            for g, r in enumerate(ATT_DILATIONS):
                _to_strided(ws[g] * dv, do_refs[g], lanes, r, scs[2 * ng])
                _to_strided(ws[g] * dot, dlt_refs[g], lanes, r, scs[2 * ng + 1])

    views = [a.reshape(nb, r, mtiles, tm // r, ATT_W) for grp in (o, lse) for a, r in zip(grp, ATT_DILATIONS)]
    outs = pl.pallas_call(
        body, name="att_merge_bwd", grid=(t // tm,),
        in_specs=[_strided_spec(r, mtiles) for _ in range(2) for r in ATT_DILATIONS] + [_rs(tm, ATT_W)],
        out_specs=[_strided_spec(r, mtiles) for _ in range(2) for r in ATT_DILATIONS],
        out_shape=[_strided_shape(nb, r, mtiles, dt) for dt in (BF16, F32) for r in ATT_DILATIONS],
        scratch_shapes=[pltpu.VMEM((tm, ATT_D), F32)] * (2 * ng + 2), compiler_params=_cparams(("arbitrary",)))(
            *views, datt)
    flat = [a.reshape(t, ATT_W) for a in outs]
    return flat[:ng], flat[ng:]


def _branch_gates(rows, fulls):
    return (_sigmoid(rows[0][...].astype(F32) + fulls[0][...]), _sigmoid(rows[1][...].astype(F32) + fulls[1][...]))


def _mix_fwd_epilogue(y_att, rows, fulls):
    g0, g1 = _branch_gates(rows, fulls)
    return [y_att, g0 * rows[2][...].astype(F32) + g1 * y_att]


def _mix_bwd_epilogue(dm, rows, fulls):
    g0, g1 = _branch_gates(rows, fulls)
    dg = jnp.concatenate([dm * rows[2][...].astype(F32) * g0 * (1.0 - g0),
                          dm * rows[3][...].astype(F32) * g1 * (1.0 - g1)], axis=-1)
    return [dm * g0, dm * g1, dg, _colsum(dg)]


FFN_TM = 512


def _ffn_in(h2, wg_t, wu_t):
    t, d = h2.shape
    f = wg_t.shape[0]
    tm, tn = FFN_TM, _pick(f, 1536)

    def body(a_ref, g_ref, u_ref, gt_ref, up_ref, act_ref):
        a = a_ref[...]
        gt = _dot_nt(a, g_ref[...])
        up = _dot_nt(a, u_ref[...])
        gt_ref[...] = gt.astype(BF16)
        up_ref[...] = up.astype(BF16)
        act_ref[...] = (gt * _sigmoid(gt) * up).astype(BF16)

    a_spec = pl.BlockSpec((tm, d), lambda j, i: (i, 0))
    w_spec = pl.BlockSpec((tn, d), lambda j, i: (j, 0))
    o_spec = pl.BlockSpec((tm, tn), lambda j, i: (i, j))
    return pl.pallas_call(
        body, name="ffn_in", grid=(f // tn, t // tm), in_specs=[a_spec, w_spec, w_spec],
        out_specs=[o_spec] * 3, out_shape=[SDS((t, f), BF16)] * 3,
        compiler_params=_cparams(("parallel", "arbitrary")))(h2, wg_t, wu_t)


def _ffn_bwd_in(dx2, w_down, gt, up):
    t, d = dx2.shape
    f = w_down.shape[0]
    tm, tn = FFN_TM, _pick(f, 1536)

    def body(a_ref, w_ref, g_ref, u_ref, dgt_ref, dup_ref):
        dv = _dot_nt(a_ref[...], w_ref[...])
        gv = g_ref[...].astype(F32)
        sg = _sigmoid(gv)
        dgt_ref[...] = (dv * u_ref[...].astype(F32) * sg * (1.0 + gv * (1.0 - sg))).astype(BF16)
        dup_ref[...] = (dv * gv * sg).astype(BF16)

    a_spec = pl.BlockSpec((tm, d), lambda j, i: (i, 0))
    w_spec = pl.BlockSpec((tn, d), lambda j, i: (j, 0))
    o_spec = pl.BlockSpec((tm, tn), lambda j, i: (i, j))
    return pl.pallas_call(
        body, name="ffn_bwd_in", grid=(f // tn, t // tm), in_specs=[a_spec, w_spec, o_spec, o_spec],
        out_specs=[o_spec] * 2, out_shape=[SDS((t, f), BF16)] * 2,
        compiler_params=_cparams(("parallel", "arbitrary")))(dx2, w_down, gt, up)


def _adamw(w, g, m, v, name):
    r, c = w.shape[-2:]
    lead = w.ndim - 2
    tr = _row_tile(r, max(8, 400_000 // c))
    c1 = 1.0 / (1.0 - ADAM_B1 ** ADAM_STEP)
    c2 = 1.0 / (1.0 - ADAM_B2 ** ADAM_STEP)

    def fn(i, w_ref, g_ref, m_ref, v_ref):
        gv = g_ref[...]
        mn = ADAM_B1 * m_ref[...] + (1.0 - ADAM_B1) * gv
        vn = ADAM_B2 * v_ref[...] + (1.0 - ADAM_B2) * (gv * gv)
        delta = -ADAM_LR * ((mn * c1) / (jnp.sqrt(vn * c2) + ADAM_EPS) + ADAM_WD * w_ref[...])
        return [delta, mn, vn]

    spec = pl.BlockSpec((None,) * lead + (tr, c), lambda i: (0,) * lead + (i, 0))
    return _rw(name, fn, r // tr, [(w, spec), (g, spec), (m, spec), (v, spec)], [(SDS(w.shape, F32), spec)] * 3)


ANY = pl.BlockSpec(memory_space=pl.ANY)


def _place():
    x, y, c = lax.axis_index("x"), lax.axis_index("y"), lax.axis_index("c")
    chips = [(1 - x, y), (x, 1 - y), (1 - x, 1 - y)]
    return x, y, c, chips


def _remote(src, dst, ssem, rsem, to):
    return pltpu.make_async_remote_copy(src_ref=src, dst_ref=dst, send_sem=ssem, recv_sem=rsem, device_id=to,
                                        device_id_type=MESH)


def _copy_through_vmem(src, dst, buf, isem, osem):
    chunk = buf.shape[1]
    n = src.shape[0] // chunk
    load = lambda k: pltpu.make_async_copy(src.at[pl.ds(k * chunk, chunk)], buf.at[k % 2], isem.at[k % 2])
    store = lambda k: pltpu.make_async_copy(buf.at[k % 2], dst.at[pl.ds(k * chunk, chunk)], osem.at[k % 2])
    load(0).start()
    for k in range(n):
        load(k).wait()
        if k + 1 < n:
            if k >= 1:
                store(k - 1).wait()
            load(k + 1).start()
        store(k).start()
    if n >= 2:
        store(n - 2).wait()
    store(n - 1).wait()


def _copy_scratch(rows, width, dtype):
    chunk = _row_tile(rows, 512)
    return [pltpu.VMEM((2, chunk, width), dtype), pltpu.SemaphoreType.DMA((2,)), pltpu.SemaphoreType.DMA((2,))]


def _gather_weights(wp):
    def body(w_ref, out_ref, ssem, rsem, buf, isem, osem):
        x, y, c, chips = _place()
        me = 2 * x + y
        sib = (x, y, 1 - c)
        first = [_remote(w_ref.at[c], out_ref.at[me, c], ssem.at[j], rsem.at[j], (*chip, c))
                 for j, chip in enumerate(chips)]
        for cp in first:
            cp.start()
        for half in range(2):
            _copy_through_vmem(w_ref.at[half], out_ref.at[me, half], buf, isem, osem)
        passed = []
        for j, chip in enumerate(chips):
            ci = 2 * chip[0] + chip[1]
            _remote(w_ref.at[c], out_ref.at[ci, c], ssem.at[j], rsem.at[j], (*chip, c)).wait_recv()
            cp = _remote(out_ref.at[ci, c], out_ref.at[ci, c], ssem.at[3 + j], rsem.at[3 + j], sib)
            cp.start()
            passed.append(cp)
        for j, chip in enumerate(chips):
            ci = 2 * chip[0] + chip[1]
            _remote(out_ref.at[ci, 1 - c], out_ref.at[ci, 1 - c], ssem.at[3 + j], rsem.at[3 + j], sib).wait_recv()
        for cp in first + passed:
            cp.wait_send()

    return pl.pallas_call(
        body, name="gather_weights", in_specs=[ANY], out_specs=ANY,
        out_shape=SDS((N_CHIPS,) + wp.shape, wp.dtype),
        scratch_shapes=[pltpu.SemaphoreType.DMA((6,)), pltpu.SemaphoreType.DMA((6,))]
        + _copy_scratch(wp.shape[1], wp.shape[2], wp.dtype),
        compiler_params=pltpu.CompilerParams(has_side_effects=True))(wp)


def _swap_halves(g2, tag):
    def body(g_ref, out_ref, ssem, rsem):
        x, y, c, _ = _place()
        cp = _remote(g_ref.at[1 - c], out_ref, ssem, rsem, (x, y, 1 - c))
        cp.start()
        cp.wait()

    return pl.pallas_call(
        body, name="swap_halves_" + tag, in_specs=[ANY], out_specs=ANY, out_shape=SDS(g2.shape[1:], g2.dtype),
        scratch_shapes=[pltpu.SemaphoreType.DMA(()), pltpu.SemaphoreType.DMA(())],
        compiler_params=pltpu.CompilerParams(has_side_effects=True))(g2)


def _add_own_half(g2, other, c, tag):
    _, nch, rows, w = g2.shape
    tr = _row_tile(rows, 512)
    nr = rows // tr

    def body(c_ref, a_ref, b_ref, o_ref):
        o_ref[...] = (a_ref[...].astype(F32) + b_ref[...].astype(F32)).astype(o_ref.dtype)

    grid_spec = pltpu.PrefetchScalarGridSpec(
        num_scalar_prefetch=1, grid=(nch, nr),
        in_specs=[pl.BlockSpec((None, None, tr, w), lambda k, i, c_ref: (c_ref[0], k, i, 0)),
                  pl.BlockSpec((None, tr, w), lambda k, i, c_ref: (k, i, 0))],
        out_specs=pl.BlockSpec((None, tr, w), lambda k, i, c_ref: (k, i, 0)))
    return pl.pallas_call(
        body, name="add_own_half_" + tag, grid_spec=grid_spec, out_shape=SDS(other.shape, other.dtype),
        compiler_params=_cparams(("arbitrary", "arbitrary")))(jnp.reshape(c, (1,)).astype(jnp.int32), g2, other)


def _sum_chips(q, tag):
    nch, rows, w = q.shape
    tr = _row_tile(rows, 512)

    def fn(i, q_ref):
        return [((q_ref[0].astype(F32) + q_ref[1].astype(F32)) + q_ref[2].astype(F32)) + q_ref[3].astype(F32)]

    return _rw("sum_chips_" + tag, fn, rows // tr, [(q, pl.BlockSpec((nch, tr, w), lambda i: (0, i, 0)))],
               [(SDS((rows, w), F32), _rs(tr, w))])[0]


def _chip_copies(src_ref, dst_ref, ssem, rsem, outgoing):
    x, y, c, chips = _place()
    me = 2 * x + y
    cps = []
    for j, chip in enumerate(chips):
        ci = 2 * chip[0] + chip[1]
        cps.append(_remote(src_ref.at[ci], dst_ref.at[me if outgoing else ci], ssem.at[j], rsem.at[j], (*chip, c)))
    return cps, me


def _scatter_side(p):
    def first(ins, outs, scr):
        cps, me = _chip_copies(ins[0], outs[0], scr[0], scr[1], True)
        for cp in cps:
            cp.start()
        pltpu.make_async_copy(ins[0].at[me], outs[0].at[me], scr[2]).start()

    def last(ins, outs, scr):
        for cp in _chip_copies(ins[0], outs[0], scr[0], scr[1], False)[0]:
            cp.wait_recv()
        cps, me = _chip_copies(ins[0], outs[0], scr[0], scr[1], True)
        for cp in cps:
            cp.wait_send()
        pltpu.make_async_copy(ins[0].at[me], outs[0].at[me], scr[2]).wait()

    return _Side((p,), (SDS(p.shape, p.dtype),),
                 (pltpu.SemaphoreType.DMA((3,)), pltpu.SemaphoreType.DMA((3,)), pltpu.SemaphoreType.DMA(())),
                 first, None, last)


def _gather_copies(w_ref, out_ref, ssem, rsem):
    x, y, c, chips = _place()
    me = 2 * x + y
    sib = (x, y, 1 - c)
    sends, arrivals, forwards, from_sib = [], [], [], []
    for j, chip in enumerate(chips):
        ci = 2 * chip[0] + chip[1]
        sends.append(_remote(w_ref.at[c], out_ref.at[me, c], ssem.at[j], rsem.at[j], (*chip, c)))
        arrivals.append(_remote(w_ref.at[c], out_ref.at[ci, c], ssem.at[j], rsem.at[j], (*chip, c)))
        forwards.append(_remote(out_ref.at[ci, c], out_ref.at[ci, c], ssem.at[3 + j], rsem.at[3 + j], sib))
        from_sib.append(_remote(out_ref.at[ci, 1 - c], out_ref.at[ci, 1 - c], ssem.at[3 + j], rsem.at[3 + j], sib))
    return sends, arrivals, forwards, from_sib, me


def _gather_side(wp):
    def first(ins, outs, scr):
        sends, _, _, _, me = _gather_copies(ins[0], outs[0], scr[0], scr[1])
        for cp in sends:
            cp.start()
        pltpu.make_async_copy(ins[0], outs[0].at[me], scr[2]).start()

    def mid(ins, outs, scr):
        _, arrivals, forwards, _, _ = _gather_copies(ins[0], outs[0], scr[0], scr[1])
        for arrived, forward in zip(arrivals, forwards):
            arrived.wait_recv()
            forward.start()

    def last(ins, outs, scr):
        sends, _, forwards, from_sib, me = _gather_copies(ins[0], outs[0], scr[0], scr[1])
        for cp in from_sib:
            cp.wait_recv()
        for cp in sends + forwards:
            cp.wait_send()
        pltpu.make_async_copy(ins[0], outs[0].at[me], scr[2]).wait()

    return _Side((wp,), (SDS((N_CHIPS,) + wp.shape, wp.dtype),),
                 (pltpu.SemaphoreType.DMA((6,)), pltpu.SemaphoreType.DMA((6,)), pltpu.SemaphoreType.DMA(())),
                 first, mid, last)


def _allreduce_small(v, name):
    rows, w = v.shape
    offsets = [(dx, dy, dc) for dx in (0, 1) for dy in (0, 1) for dc in (0, 1)][1:]

    def body(v_ref, o_ref, buf, ssem, rsem):
        x, y, c, _ = _place()
        flip = lambda p, d: 1 - p if d else p
        peers = [(flip(x, dx), flip(y, dy), flip(c, dc)) for dx, dy, dc in offsets]
        index = lambda p: 4 * p[0] + 2 * p[1] + p[2]
        me = index((x, y, c))
        buf[me] = v_ref[...]
        sent = [_remote(v_ref, buf.at[me], ssem.at[q], rsem.at[q], p) for q, p in enumerate(peers)]
        for cp in sent:
            cp.start()
        for q, p in enumerate(peers):
            _remote(v_ref, buf.at[index(p)], ssem.at[q], rsem.at[q], p).wait_recv()
        for cp in sent:
            cp.wait_send()
        acc = buf[0]
        for q in range(1, 8):
            acc = acc + buf[q]
        o_ref[...] = acc

    vm = pl.BlockSpec(memory_space=pltpu.VMEM)
    return pl.pallas_call(
        body, name=name, in_specs=[vm], out_specs=vm, out_shape=SDS((rows, w), F32),
        scratch_shapes=[pltpu.VMEM((8, rows, w), F32), pltpu.SemaphoreType.DMA((7,)), pltpu.SemaphoreType.DMA((7,))],
        compiler_params=pltpu.CompilerParams(has_side_effects=True))(v)


def _join_halves(h, tag):
    def body(h_ref, out_ref, ssem, rsem, buf, isem, osem):
        x, y, c, _ = _place()
        cp = _remote(h_ref, out_ref.at[c], ssem, rsem, (x, y, 1 - c))
        cp.start()
        _copy_through_vmem(h_ref, out_ref.at[c], buf, isem, osem)
        _remote(h_ref, out_ref.at[1 - c], ssem, rsem, (x, y, 1 - c)).wait_recv()
        cp.wait_send()

    return pl.pallas_call(
        body, name="join_halves_" + tag, in_specs=[ANY], out_specs=ANY, out_shape=SDS((2,) + h.shape, h.dtype),
        scratch_shapes=[pltpu.SemaphoreType.DMA(()), pltpu.SemaphoreType.DMA(())]
        + _copy_scratch(h.shape[0], h.shape[1], h.dtype),
        compiler_params=pltpu.CompilerParams(has_side_effects=True))(h)


PACK_W = 1024
SHARDED = ("w_in", "w_ffn_gate", "w_ffn_up", "w_ssm_out", "w_att_out", "w_mix_out", "w_ffn_down")
COL_SHARDED = ("w_in", "w_ffn_gate", "w_ffn_up", "w_att_out")
SMALL = ("norm_mix", "b_gate", "conv_b", "dt_bias", "a_log", "d_skip", "ssm_norm", "norm_ffn", "norm_final")


PACK_ROW_ALIGN = 16


def _rows(n):
    return -(-n // (PACK_W * PACK_ROW_ALIGN)) * PACK_ROW_ALIGN


def _pack_rows(parts, total_rows):
    rows = []
    for p in parts:
        size = int(p.size)
        if size % PACK_W:
            p = jnp.pad(p.reshape(-1), (0, PACK_W - size % PACK_W))
        p = p.reshape(-1, PACK_W)
        rows.append(jnp.pad(p, ((0, _rows(size) - p.shape[0]), (0, 0))))
    used = sum(r.shape[0] for r in rows)
    if total_rows > used:
        rows.append(jnp.zeros((total_rows - used, PACK_W), rows[0].dtype))
    return jnp.concatenate(rows, axis=0)


def _padded_rows(n):
    return -(-n // 32) * 32


def _wire_name(name):
    return name + "_t" if name in COL_SHARDED else name


def _wire_shard(w, name):
    return w.T if name in COL_SHARDED else w


def _group_major(a, axis):
    gw = D_INNER // N_GROUPS
    take = lambda lo, n: lax.slice_in_dim(a, lo, lo + n, axis=axis)
    parts = []
    for g in range(N_GROUPS):
        parts += [take(g * gw, gw), take(D_INNER + g * D_STATE, D_STATE),
                  take(D_INNER + N_GROUPS * D_STATE + g * D_STATE, D_STATE)]
    return jnp.concatenate(parts, axis=axis)


def _group_major_inv(a, axis):
    gw = D_INNER // N_GROUPS
    take = lambda lo, n: lax.slice_in_dim(a, lo, lo + n, axis=axis)
    xs = [take(g * GROUP_W, gw) for g in range(N_GROUPS)]
    bs = [take(g * GROUP_W + gw, D_STATE) for g in range(N_GROUPS)]
    cs = [take(g * GROUP_W + gw + D_STATE, D_STATE) for g in range(N_GROUPS)]
    return jnp.concatenate(xs + bs + cs, axis=axis)


LATE = ("w_ffn_gate_t", "w_ffn_up_t", "w_ssm_out", "w_att_out_t", "w_mix_out", "w_ffn_down")


class _Overlap(NamedTuple):
    gather_side: _Side
    late_weights: Callable
    scatter_side: Callable
    scatter_in: Callable


def _local_step(x, target, wts, overlap):
    nb, seq, d = x.shape
    t = nb * seq
    x = x.reshape(t, d)
    target = target.reshape(t, d)
    hg = HEADS_PER_GROUP

    o1, o2, o3, o4 = D_INNER, D_INNER + CONV_DIM, D_INNER + CONV_DIM + N_HEADS, D_INNER + CONV_DIM + N_HEADS + QKV_DIM
    n_in = o4 + 2 * D_MODEL

    def in_rows(lo, hi):
        per = n_in // N_CHIPS
        parts = [wts["w_in_t"][k, max(lo, k * per) - k * per:min(hi, (k + 1) * per) - k * per]
                 for k in range(N_CHIPS) if max(lo, k * per) < min(hi, (k + 1) * per)]
        return parts[0] if len(parts) == 1 else jnp.concatenate(parts, axis=0)

    w_z = in_rows(0, o1)
    w_xbc = _group_major(in_rows(o1, o2), 0)
    w_dt = jnp.pad(in_rows(o2, o3), ((0, DT_PAD - N_HEADS), (0, 0)))
    w_qkv = in_rows(o3, o4)
    w_gate = in_rows(o4, n_in)
    conv_w = _group_major(wts["conv_w"], 1)
    conv_b = _group_major(wts["conv_b"], 1)

    def per_group_row(p):
        return p.reshape(N_GROUPS, 1, hg)

    def per_group_col(p):
        return p.reshape(N_GROUPS, hg, 1)

    a_neg = -jnp.exp(wts["a_log"])
    bias_r, bias_c = per_group_row(wts["dt_bias"]), per_group_col(wts["dt_bias"])
    a_r, a_c = per_group_row(a_neg), per_group_col(a_neg)
    dskip_r = per_group_row(wts["d_skip"])
    cos, sin = _rope_tables(seq)

    h = _rms_fwd(x, wts["norm_mix"], "rms_mix_fwd")
    z = _mm(h, w_z, "nt", BF16, "proj_z")
    xbc = _mm(h, w_xbc, "nt", F32, "proj_xbc")
    dt_raw = _mm(h, w_dt, "nt", F32, "proj_dt")
    qkv = _mm(h, w_qkv, "nt", BF16, "proj_qkv")
    gate_logits = _mm(h, w_gate, "nt", BF16, "proj_gate")

    xc = _conv_fwd(xbc, conv_w, conv_b, seq)
    dtr = dt_raw[:, :N_HEADS].reshape(t, N_GROUPS, hg).transpose(1, 0, 2)
    dtrt = dt_raw[:, :N_HEADS].reshape(nb, seq, N_GROUPS, hg).transpose(2, 0, 3, 1)
    y, states, *gathered = _ssd_fwd(xc, dtr, dtrt, bias_r, bias_c, a_r, a_c, dskip_r, nb, seq, overlap.gather_side)
    wts = {**wts, **overlap.late_weights(gathered)}
    yn = _gate_norm_fwd(y, z, wts["ssm_norm"])
    y_ssm = _mm(yn, wts["w_ssm_out"], "nn", BF16, "ssm_out")

    groups = range(len(ATT_DILATIONS))
    qg, kg, vg = _rope_fwd(qkv, cos, sin, nb, seq)
    o_g, lse_g = zip(*[_att_fwd(qg[i], kg[i], vg[i], i, seq) for i in groups])
    att = _merge_fwd(o_g, lse_g, nb, seq)
    gate_halves = [(gate_logits, d, 0), (gate_logits, d, 1)]
    b_gate_halves = [(wts["b_gate"], d, 0), (wts["b_gate"], d, 1)]
    y_att, mixed = _mm_fused(att, wts["w_att_out_t"], "nt", "att_out_mix", 512, _mix_fwd_epilogue,
                             gate_halves + [(y_ssm, d, 0)], b_gate_halves, [(d, BF16), (d, BF16)])

    def residual_and_norm(xv, rows, fulls):
        return [xv, xv * lax.rsqrt(jnp.mean(xv * xv, axis=-1, keepdims=True) + EPS) * fulls[0][...]]

    x1, h2 = _mm_fused(mixed, wts["w_mix_out"], "nn", "mix_out_norm", 512, residual_and_norm, [],
                       [(wts["norm_ffn"], d, 0)], [(d, F32), (d, BF16)], add=x)
    gt, up, act = _ffn_in(h2, wts["w_ffn_gate_t"], wts["w_ffn_up_t"])

    g = {}
    dx2, dx2_b, g["norm_final"], loss = _mm_fused(
        act, wts["w_ffn_down"], "nn", "ffn_down_loss", 512,
        lambda x2, rows, fulls: _final_values(x2, rows[0][...], fulls[0][...]),
        [(target, d, 0)], [(wts["norm_final"].reshape(1, d), d, 0)], [(d, F32), (d, BF16), (d, F32), (1, F32)],
        n_acc=2, add=x1)
    g["w_ffn_down"] = _mm(act, dx2_b, "tn", BF16, "g_ffn_down")
    dgt, dup = _ffn_bwd_in(dx2_b, wts["w_ffn_down"], gt, up)
    g["w_ffn_gate_t"] = _mm(dgt, h2, "tn", BF16, "g_ffn_gate")
    g["w_ffn_up_t"] = _mm(dup, h2, "tn", BF16, "g_ffn_up")
    dh2 = _mm(dgt, wts["w_ffn_gate_t"], "nn", F32, "d_h2_gate")
    dx1, dx1_b, g["norm_ffn"] = _mm_fused(
        dup, wts["w_ffn_up_t"], "nn", "d_h2_up_norm", 512,
        lambda dh, rows, fulls: _rms_bwd_values(rows[0][...], dh, fulls[0][...], rows[1][...]),
        [(x1, d, 0), (dx2, d, 0)], [(wts["norm_ffn"], d, 0)], [(d, F32), (d, BF16), (d, F32)], n_acc=1, add=dh2)

    g["w_mix_out"] = _mm(mixed, dx1_b, "tn", BF16, "g_mix_out")
    dy_ssm, dy_att, dgate, g["b_gate"] = _mm_fused(
        dx1_b, wts["w_mix_out"], "nt", "d_mixed_gates", 512, _mix_bwd_epilogue,
        gate_halves + [(y_ssm, d, 0), (y_att, d, 0)], b_gate_halves,
        [(d, BF16), (d, BF16), (2 * d, BF16), (2 * d, F32)], n_acc=1)

    datt = _mm(dy_att, wts["w_att_out_t"], "nn", F32, "d_att")
    g["w_att_out_t"] = _mm(dy_att, att, "tn", BF16, "g_att_out")
    do_g, dlt_g = _merge_bwd(o_g, lse_g, datt, nb, seq)
    dq_g, dk_g, dv_g = zip(*[_att_bwd(qg[i], kg[i], vg[i], do_g[i], lse_g[i], dlt_g[i], i, seq) for i in groups])
    dqkv = _rope_bwd(dq_g, dk_g, dv_g, cos, sin, nb, seq)

    g["w_ssm_out"] = _mm(yn, dy_ssm, "tn", BF16, "g_ssm_out")
    dy, dz, g["ssm_norm"] = _mm_fused(
        dy_ssm, wts["w_ssm_out"], "nt", "d_yn_norm", 256, _gate_norm_bwd_epilogue,
        [(y, D_INNER, 0), (z, D_INNER, 0)], [(wts["ssm_norm"], D_INNER, 0)],
        [(D_INNER, BF16), (D_INNER, BF16), (D_INNER, F32)], n_acc=1)
    side = overlap.scatter_side({n: g.pop(n) for n in LATE})
    dxc, ddtr, g_bias, g_alog, g_dskip, *scattered = _ssd_bwd(xc, dtr, dtrt, bias_r, bias_c, a_r, a_c, dskip_r,
                                                               states, dy, nb, seq, side)
    g["dt_bias"] = g_bias.reshape(1, N_HEADS)
    g["a_log"] = g_alog.reshape(1, N_HEADS)
    g["d_skip"] = g_dskip.reshape(1, N_HEADS)
    dpre, g_conv_w, g_conv_b = _conv_bwd_pre(xbc, conv_w, conv_b, dxc, seq)
    g["conv_w"] = _group_major_inv(g_conv_w, 1)
    g["conv_b"] = _group_major_inv(g_conv_b, 1)
    dxbc = _conv_bwd_in(dpre, conv_w, seq)
    ddt = jnp.pad(ddtr.transpose(1, 0, 2).reshape(t, N_HEADS), ((0, 0), (0, DT_PAD - N_HEADS))).astype(BF16)

    g_in_t = jnp.concatenate([
        _mm(dz, h, "tn", BF16, "g_in_z"),
        _group_major_inv(_mm(dxbc, h, "tn", BF16, "g_in_xbc"), 0),
        _mm(ddt, h, "tn", BF16, "g_in_dt")[:N_HEADS],
        _mm(dqkv, h, "tn", BF16, "g_in_qkv"),
        _mm(dgate, h, "tn", BF16, "g_in_gate")], axis=0)
    dh = _mm(dz, w_z, "nn", F32, "d_h_z")
    dh = _mm(dxbc, w_xbc, "nn", F32, "d_h_xbc", add=dh)
    dh = _mm(ddt, w_dt, "nn", F32, "d_h_dt", add=dh)
    dh, *scattered_in = _mm(dqkv, w_qkv, "nn", F32, "d_h_qkv", add=dh, side=overlap.scatter_in({"w_in_t": g_in_t}))
    dx, _, g["norm_mix"] = _mm_fused(
        dgate, w_gate, "nn", "d_h_gate_norm", 512,
        lambda dhv, rows, fulls: _rms_bwd_values(rows[0][...], dhv, fulls[0][...], rows[1][...]),
        [(x, d, 0), (dx1, d, 0)], [(wts["norm_mix"], d, 0)], [(d, F32), (d, BF16), (d, F32)], n_acc=1, add=dh)
    return loss[0, 0], dx.reshape(nb, seq, d), g, scattered, scattered_in


def kernel(x, norm_mix, w_in, b_gate, conv_w, conv_b, dt_bias, a_log, d_skip, ssm_norm, w_ssm_out, w_att_out, w_mix_out, norm_ffn, w_ffn_gate, w_ffn_up, w_ffn_down, norm_final, loss_target, m_norm_mix, m_w_in, m_b_gate, m_conv_w, m_conv_b, m_dt_bias, m_a_log, m_d_skip, m_ssm_norm, m_w_ssm_out, m_w_att_out, m_w_mix_out, m_norm_ffn, m_w_ffn_gate, m_w_ffn_up, m_w_ffn_down, m_norm_final, v_norm_mix, v_w_in, v_b_gate, v_conv_w, v_conv_b, v_dt_bias, v_a_log, v_d_skip, v_ssm_norm, v_w_ssm_out, v_w_att_out, v_w_mix_out, v_norm_ffn, v_w_ffn_gate, v_w_ffn_up, v_w_ffn_down, v_norm_final):
    names = ("norm_mix", "w_in", "b_gate", "conv_w", "conv_b", "dt_bias", "a_log", "d_skip", "ssm_norm", "w_ssm_out",
             "w_att_out", "w_mix_out", "norm_ffn", "w_ffn_gate", "w_ffn_up", "w_ffn_down", "norm_final")
    w_loc = dict(zip(names, (norm_mix, w_in, b_gate, conv_w, conv_b, dt_bias, a_log, d_skip, ssm_norm, w_ssm_out,
                             w_att_out, w_mix_out, norm_ffn, w_ffn_gate, w_ffn_up, w_ffn_down, norm_final)))
    m_loc = dict(zip(names, (m_norm_mix, m_w_in, m_b_gate, m_conv_w, m_conv_b, m_dt_bias, m_a_log, m_d_skip,
                             m_ssm_norm, m_w_ssm_out, m_w_att_out, m_w_mix_out, m_norm_ffn, m_w_ffn_gate,
                             m_w_ffn_up, m_w_ffn_down, m_norm_final)))
    v_loc = dict(zip(names, (v_norm_mix, v_w_in, v_b_gate, v_conv_w, v_conv_b, v_dt_bias, v_a_log, v_d_skip,
                             v_ssm_norm, v_w_ssm_out, v_w_att_out, v_w_mix_out, v_norm_ffn, v_w_ffn_gate,
                             v_w_ffn_up, v_w_ffn_down, v_norm_final)))
    two_d = lambda a: a.reshape(a.shape[-2:]) if a.ndim >= 2 else a.reshape(1, -1)
    w2 = {n: two_d(a) for n, a in w_loc.items()}
    chip = 2 * lax.axis_index("x") + lax.axis_index("y")
    c = lax.axis_index("c")

    wire_shapes = {n: _wire_shard(w2[n], n).shape for n in SHARDED}
    true_rows = {n: wire_shapes[n][0] * wire_shapes[n][1] // PACK_W for n in SHARDED}
    seg_rows = {n: _rows(wire_shapes[n][0] * wire_shapes[n][1]) for n in SHARDED}
    buckets = {"first": ("w_in",), "late": tuple(n for n in SHARDED if n != "w_in")}
    rows_of = {b: _padded_rows(sum(seg_rows[n] for n in ns)) for b, ns in buckets.items()}

    def pack_shards(b):
        packed = _pack_rows([_wire_shard(w2[n], n).astype(BF16) for n in buckets[b]], rows_of[b])
        return packed.reshape(2, rows_of[b] // 2, PACK_W)

    def unpack_full(gathered, b):
        wg, out, off = gathered.reshape(N_CHIPS, rows_of[b], PACK_W), {}, 0
        for n in buckets[b]:
            rows, cols = wire_shapes[n]
            out[_wire_name(n)] = wg[:, off:off + true_rows[n]].reshape(N_CHIPS * rows, cols)
            off += seg_rows[n]
        return out

    def pack_grads(g, b):
        sections = [_pack_rows([g[_wire_name(n)].reshape(N_CHIPS, true_rows[n], PACK_W)[k] for n in buckets[b]],
                               rows_of[b]) for k in range(N_CHIPS)]
        return jnp.stack(sections).reshape(N_CHIPS, 2, rows_of[b] // 2, PACK_W).transpose(1, 0, 2, 3)

    def chip_sums(g, b):
        g2 = pack_grads(g, b)
        return _add_own_half(g2, _swap_halves(g2, b), c, b)

    def finish(by_source, b):
        reduced = _join_halves(_sum_chips(by_source, b), b).reshape(rows_of[b], PACK_W)
        out, off = {}, 0
        for n in buckets[b]:
            out[n] = reduced[off:off + true_rows[n]].reshape(wire_shapes[n])
            off += seg_rows[n]
        return out

    full = {"w_in_t": _gather_weights(pack_shards("first")).reshape(N_CHIPS, rows_of["first"], PACK_W)}
    for n in SMALL:
        full[n] = w2[n]
    overlap = _Overlap(_gather_side(pack_shards("late")), lambda outs: unpack_full(outs[0], "late"),
                       lambda g: _scatter_side(chip_sums(g, "late")), lambda g: _scatter_side(chip_sums(g, "first")))

    n_conv = w2["conv_w"].shape[1]
    placed = lax.dynamic_update_slice_in_dim(jnp.zeros((CONV_K, N_CHIPS * n_conv), F32), w2["conv_w"], chip * n_conv, 1)
    placed = jnp.where(c == 0, placed, 0.0)
    full["conv_w"] = _allreduce_small(_pack_rows([placed], _rows(int(placed.size))), "gather_conv_w").reshape(
        -1)[:placed.size].reshape(placed.shape)

    loss_sum, grad_x, g_full, scattered, scattered_in = _local_step(x, loss_target, full, overlap)
    loss = lax.psum(loss_sum, ("x", "y", "c"))

    g_shard = {}
    small_names = SMALL + ("conv_w",)
    small_flat = jnp.concatenate([g_full[n].reshape(-1) for n in small_names])
    small = _allreduce_small(_pack_rows([small_flat], _rows(int(small_flat.size))), "allreduce_small").reshape(-1)
    off = 0
    for n in small_names:
        size = int(g_full[n].size)
        g_shard[n] = small[off:off + size].reshape(g_full[n].shape)
        off += size
    g_shard["conv_w"] = lax.dynamic_slice_in_dim(g_shard["conv_w"], chip * n_conv, n_conv, 1)

    g_shard.update(finish(scattered[0], "late"))
    g_shard.update(finish(scattered_in[0], "first"))

    grads, deltas, new_m, new_v = [], [], [], []
    for n in names:
        shape = w_loc[n].shape
        if n in COL_SHARDED:
            view = unview = lambda a: jnp.swapaxes(a, -1, -2)
        else:
            view, unview = ((lambda a: a) if len(shape) >= 2 else two_d), (lambda a: a.reshape(shape))
        gn = g_shard[n].reshape(view(w_loc[n]).shape)
        outs = _adamw(view(w_loc[n]), gn, view(m_loc[n]), view(v_loc[n]), "adamw_" + n)
        for acc, a in zip((grads, deltas, new_m, new_v), (gn, *outs)):
            acc.append(unview(a))
    return (loss, grad_x, *grads, *deltas, *new_m, *new_v)
```

```python
import functools
from typing import Callable, NamedTuple, Optional

import jax
import jax.numpy as jnp
from jax import lax
from jax.experimental import pallas as pl
from jax.experimental.pallas import tpu as pltpu

F32 = jnp.float32
BF16 = jnp.bfloat16
SDS = jax.ShapeDtypeStruct
MESH = pl.DeviceIdType.MESH

D_MODEL = 1024
D_INNER = 2048
N_HEADS = 32
HEAD_P = 64
N_GROUPS = 4
HEADS_PER_GROUP = N_HEADS // N_GROUPS
D_STATE = 128
CONV_K = 4
CHUNK = 128
CONV_DIM = D_INNER + 2 * N_GROUPS * D_STATE
GROUP_W = D_INNER // N_GROUPS + 2 * D_STATE
ATT_HEADS = 12
ATT_D = 128
ATT_SLOTS = 4
ATT_W = ATT_SLOTS * ATT_D
ATT_DILATIONS = (1, 4, 16)
ATT_BLOCK = 128
QKV_DIM = 3 * ATT_HEADS * ATT_D
D_FF = 2816
DT_PAD = 128
ROPE_THETA = 10000.0
EPS = 1e-6
N_CHIPS = 4
LANES = 128

ADAM_LR = 0.001
ADAM_B1 = 0.9
ADAM_B2 = 0.999
ADAM_EPS = 1e-08
ADAM_WD = 0.01
ADAM_STEP = 10

VMEM_LIMIT = 48 * 1024 * 1024


def _cparams(semantics):
    return pltpu.CompilerParams(dimension_semantics=semantics, vmem_limit_bytes=VMEM_LIMIT)


def _pick(n, cap):
    best = None
    for t in range(LANES, min(n, cap) + 1, LANES):
        if n % t == 0:
            best = t
    return best or n


def _row_tile(rows, cap):
    best = None
    for t in range(8, min(rows, cap) + 1, 8):
        if rows % t == 0:
            best = t
    return best or rows


def _sigmoid(x):
    return pl.reciprocal(1.0 + jnp.exp(-x), approx=True)


def _softplus(x):
    return jnp.maximum(x, 0.0) + jnp.log(1.0 + jnp.exp(-jnp.abs(x)))


def _dot(a, b):
    return jnp.dot(a, b, preferred_element_type=F32)


def _dot_nt(a, b):
    return lax.dot_general(a, b, (((1,), (1,)), ((), ())), preferred_element_type=F32)


def _dot_tn(a, b):
    return lax.dot_general(a, b, (((0,), (0,)), ((), ())), preferred_element_type=F32)


def _mm(a, b, mode, out_dtype, name, add=None, side=None):
    if mode == "nn":
        (m, k), (_, n) = a.shape, b.shape
    elif mode == "nt":
        (m, k), (n, _) = a.shape, b.shape
    else:
        (k, m), (_, n) = a.shape, b.shape
    tm, tn = _pick(m, 1536), _pick(n, 2048)
    tk = k if k <= 2048 else _pick(k, 2048)
    nk = k // tk
    dims = {"nn": ((1,), (0,)), "nt": ((1,), (1,)), "tn": ((0,), (0,))}[mode]

    def partial_product(a_ref, b_ref):
        return lax.dot_general(a_ref[...].astype(BF16), b_ref[...].astype(BF16), (dims, ((), ())),
                               preferred_element_type=F32)

    def body(*refs):
        a_ref, b_ref = refs[:2]
        c_ref = refs[2] if add is not None else None
        o_ref = refs[3] if add is not None else refs[2]

        def finish(r):
            if add is not None:
                r = r + c_ref[...].astype(F32)
            o_ref[...] = r.astype(out_dtype)

        if nk == 1:
            finish(partial_product(a_ref, b_ref))
            return
        acc = refs[-1]
        kk = pl.program_id(2)

        @pl.when(kk == 0)
        def _():
            acc[...] = partial_product(a_ref, b_ref)

        @pl.when((kk > 0) & (kk < nk - 1))
        def _():
            acc[...] += partial_product(a_ref, b_ref)

        @pl.when(kk == nk - 1)
        def _():
            finish(acc[...] + partial_product(a_ref, b_ref))

    a_spec = {"nn": pl.BlockSpec((tm, tk), lambda j, i, q: (i, q)),
              "nt": pl.BlockSpec((tm, tk), lambda j, i, q: (i, q)),
              "tn": pl.BlockSpec((tk, tm), lambda j, i, q: (q, i))}[mode]
    b_spec = {"nn": pl.BlockSpec((tk, tn), lambda j, i, q: (q, j)),
              "nt": pl.BlockSpec((tn, tk), lambda j, i, q: (j, q)),
              "tn": pl.BlockSpec((tk, tn), lambda j, i, q: (q, j))}[mode]
    o_spec = pl.BlockSpec((tm, tn), lambda j, i, q: (i, j))
    ins, specs = [a, b], [a_spec, b_spec]
    if add is not None:
        ins.append(add)
        specs.append(o_spec)
    acc = [pltpu.VMEM((tm, tn), F32)] if nk > 1 else []
    grid = (n // tn, m // tm, nk)
    if side is None:
        return pl.pallas_call(
            body, name=name, grid=grid, in_specs=specs, out_specs=o_spec, out_shape=SDS((m, n), out_dtype),
            scratch_shapes=acc, compiler_params=_cparams(("parallel", "parallel", "arbitrary")))(*ins)
    return pl.pallas_call(
        _attach_side(body, len(ins), 1, side, grid), name=name, grid=grid,
        in_specs=specs + [ANY] * len(side.ins), out_specs=[o_spec] + [ANY] * len(side.out_shapes),
        out_shape=[SDS((m, n), out_dtype)] + list(side.out_shapes), scratch_shapes=acc + list(side.scratch),
        compiler_params=_cparams(("arbitrary", "arbitrary", "arbitrary")))(*ins, *side.ins)


def _mm_fused(a, b, mode, name, tm, epilogue, row_ins, full_ins, outs, n_acc=0, add=None):
    (m, k), n = a.shape, (b.shape[1] if mode == "nn" else b.shape[0])
    tk = k if k <= 2048 else _pick(k, 2048)
    nk = k // tk
    dims = {"nn": ((1,), (0,)), "nt": ((1,), (1,))}[mode]
    n_row, n_full, n_out = len(row_ins), len(full_ins), len(outs)

    def partial_product(a_ref, b_ref):
        return lax.dot_general(a_ref[...], b_ref[...], (dims, ((), ())), preferred_element_type=F32)

    def body(*refs):
        a_ref, b_ref = refs[:2]
        pos = 3 if add is not None else 2
        row_refs, full_refs = refs[pos:pos + n_row], refs[pos + n_row:pos + n_row + n_full]
        out_refs = refs[pos + n_row + n_full:pos + n_row + n_full + n_out]
        i, kk = pl.program_id(0), pl.program_id(1)

        def finish(r):
            if add is not None:
                r = r + refs[2][...].astype(F32)
            for q, (o_ref, v) in enumerate(zip(out_refs, epilogue(r, row_refs, full_refs))):
                if q < n_out - n_acc:
                    o_ref[...] = v.astype(o_ref.dtype)
                else:
                    @pl.when(i == 0)
                    def _(o_ref=o_ref, v=v):
                        o_ref[...] = v

                    @pl.when(i > 0)
                    def _(o_ref=o_ref, v=v):
                        o_ref[...] += v

        if nk == 1:
            finish(partial_product(a_ref, b_ref))
            return
        acc = refs[-1]

        @pl.when(kk == 0)
        def _():
            acc[...] = partial_product(a_ref, b_ref)

        @pl.when((kk > 0) & (kk < nk - 1))
        def _():
            acc[...] += partial_product(a_ref, b_ref)

        @pl.when(kk == nk - 1)
        def _():
            finish(acc[...] + partial_product(a_ref, b_ref))

    tile = lambda w, cb: pl.BlockSpec((tm, w), lambda i, q: (i, cb))
    b_spec = (pl.BlockSpec((tk, n), lambda i, q: (q, 0)) if mode == "nn" else pl.BlockSpec((n, tk), lambda i, q: (0, q)))
    specs = [pl.BlockSpec((tm, tk), lambda i, q: (i, q)), b_spec] + ([tile(n, 0)] if add is not None else [])
    specs += [tile(w, cb) for _, w, cb in row_ins]
    vec = lambda w, cb: pl.BlockSpec((1, w), lambda i, q: (0, cb))
    specs += [vec(w, cb) for _, w, cb in full_ins]
    out_specs = [tile(w, 0) for w, _ in outs[:n_out - n_acc]] + [vec(w, 0) for w, _ in outs[n_out - n_acc:]]
    out_shape = [SDS((m, w), dt) for w, dt in outs[:n_out - n_acc]] + [SDS((1, w), F32) for w, _ in outs[n_out - n_acc:]]
    ins = [a, b] + ([add] if add is not None else []) + [x for x, _, _ in row_ins] + [x for x, _, _ in full_ins]
    return pl.pallas_call(
        body, name=name, grid=(m // tm, nk), in_specs=specs, out_specs=out_specs, out_shape=out_shape,
        scratch_shapes=[pltpu.VMEM((tm, n), F32)] if nk > 1 else [],
        compiler_params=_cparams(("arbitrary", "arbitrary")))(*ins)


def _rw(name, fn, nsteps, ins, outs, n_acc=0):
    n_in, n_out = len(ins), len(outs)

    def body(*refs):
        i = pl.program_id(0)
        vals = fn(i, *refs[:n_in])
        for q, (r, v) in enumerate(zip(refs[n_in:], vals)):
            if q < n_out - n_acc:
                r[...] = v.astype(r.dtype)
            else:
                @pl.when(i == 0)
                def _(r=r):
                    r[...] = jnp.zeros_like(r)

                r[...] += v

    return pl.pallas_call(
        body, name=name, grid=(nsteps,), in_specs=[s for _, s in ins], out_specs=[s for _, s in outs],
        out_shape=[o for o, _ in outs], compiler_params=_cparams(("arbitrary",)))(*[a for a, _ in ins])


def _rs(tm, w, cb=0):
    return pl.BlockSpec((tm, w), lambda i: (i, cb))


def _fs(shape):
    nd = len(shape)
    return pl.BlockSpec(shape, lambda i: (0,) * nd)


def _colsum(v):
    return jnp.sum(v, axis=0, keepdims=True)


def _rms_fwd(x, g, name):
    t, d = x.shape
    tm = 512

    def fn(i, x_ref, g_ref):
        xv = x_ref[...]
        r = lax.rsqrt(jnp.mean(xv * xv, axis=-1, keepdims=True) + EPS)
        return [xv * r * g_ref[...]]

    return _rw(name, fn, t // tm, [(x, _rs(tm, d)), (g, _fs((1, d)))], [(SDS((t, d), BF16), _rs(tm, d))])[0]


def _rms_bwd_values(xv, dhv, gv, dres):
    r = lax.rsqrt(jnp.mean(xv * xv, axis=-1, keepdims=True) + EPS)
    xhat = xv * r
    dxhat = dhv * gv
    dx = dres + r * (dxhat - xhat * jnp.mean(dxhat * xhat, axis=-1, keepdims=True))
    return [dx, dx, _colsum(dhv * xhat)]


def _final_values(xv, target, gv):
    d = xv.shape[-1]
    r = lax.rsqrt(jnp.mean(xv * xv, axis=-1, keepdims=True) + EPS)
    xhat = xv * r
    diff = xhat * gv - target
    lsum = 0.5 * jnp.sum(jnp.sum(diff * diff, axis=-1, keepdims=True) * (1.0 / d), axis=0, keepdims=True)
    dy = diff * (1.0 / d)
    dxhat = dy * gv
    dx = r * (dxhat - xhat * jnp.mean(dxhat * xhat, axis=-1, keepdims=True))
    return [dx, dx, _colsum(dy * xhat), lsum]


CONV_TS = 512
CONV_HALO = 8


def _conv_specs(seq, c):
    ts, tc = CONV_TS, GROUP_W
    hb = ts // CONV_HALO
    u_spec = pl.BlockSpec((ts, tc), lambda j, i: (i, j))
    prev_spec = pl.BlockSpec((CONV_HALO, tc), lambda j, i: (jnp.maximum(i * hb - 1, 0), j))
    w_spec = pl.BlockSpec((CONV_K, tc), lambda j, i: (0, j))
    b_spec = pl.BlockSpec((1, tc), lambda j, i: (0, j))
    return u_spec, prev_spec, w_spec, b_spec


CONV_PIECE = 32


def _conv_fill(i, seq, u_ref, prev_ref, ext):
    first = (i % (seq // CONV_TS)) == 0
    ext[0:CONV_HALO, :] = jnp.where(first, 0.0, prev_ref[...])
    ext[CONV_HALO:, :] = u_ref[...]


def _conv_piece(ext, r0, wv, bv):
    lo = r0 + CONV_HALO - CONV_K + 1
    taps = [ext[lo + q:lo + q + CONV_PIECE, :] for q in range(CONV_K)]
    pre = bv
    for q, tap in enumerate(taps):
        pre = pre + wv[q:q + 1] * tap
    return taps, pre


def _conv_fwd(u, w, b, seq):
    t, c = u.shape
    ts, tc = CONV_TS, GROUP_W
    u_spec, prev_spec, w_spec, b_spec = _conv_specs(seq, c)

    def body(u_ref, prev_ref, w_ref, b_ref, o_ref, ext):
        _conv_fill(pl.program_id(1), seq, u_ref, prev_ref, ext)
        wv, bv = w_ref[...], b_ref[...]
        for r0 in range(0, ts, CONV_PIECE):
            _, pre = _conv_piece(ext, r0, wv, bv)
            o_ref[r0:r0 + CONV_PIECE, :] = pre * _sigmoid(pre)

    return pl.pallas_call(
        body, name="conv_fwd", grid=(c // tc, t // ts), in_specs=[u_spec, prev_spec, w_spec, b_spec],
        out_specs=u_spec, out_shape=SDS((t, c), F32), scratch_shapes=[pltpu.VMEM((ts + CONV_HALO, tc), F32)],
        compiler_params=_cparams(("parallel", "arbitrary")))(u, u, w, b)


def _conv_bwd_pre(u, w, b, dxc, seq):
    t, c = u.shape
    ts, tc = CONV_TS, GROUP_W
    u_spec, prev_spec, w_spec, b_spec = _conv_specs(seq, c)

    def body(u_ref, prev_ref, w_ref, b_ref, d_ref, dpre_ref, dw_ref, db_ref, ext):
        i = pl.program_id(1)
        _conv_fill(i, seq, u_ref, prev_ref, ext)
        wv, bv = w_ref[...], b_ref[...]
        sums = [jnp.zeros((1, tc), F32)] * (CONV_K + 1)
        for r0 in range(0, ts, CONV_PIECE):
            taps, pre = _conv_piece(ext, r0, wv, bv)
            sg = _sigmoid(pre)
            dpre = d_ref[r0:r0 + CONV_PIECE, :] * sg * (1.0 + pre * (1.0 - sg))
            dpre_ref[r0:r0 + CONV_PIECE, :] = dpre
            sums = [s + _colsum(dpre * f) for s, f in zip(sums, taps + [1.0])]

        @pl.when(i == 0)
        def _():
            dw_ref[...] = jnp.zeros_like(dw_ref)
            db_ref[...] = jnp.zeros_like(db_ref)

        db_ref[...] += sums[CONV_K]
        for q in range(CONV_K):
            dw_ref[q:q + 1, :] += sums[q]

    return pl.pallas_call(
        body, name="conv_bwd_pre", grid=(c // tc, t // ts),
        in_specs=[u_spec, prev_spec, w_spec, b_spec, u_spec], out_specs=[u_spec, w_spec, b_spec],
        out_shape=[SDS((t, c), F32), SDS((CONV_K, c), F32), SDS((1, c), F32)],
        scratch_shapes=[pltpu.VMEM((ts + CONV_HALO, tc), F32)],
        compiler_params=_cparams(("parallel", "arbitrary")))(u, u, w, b, dxc)


def _conv_bwd_in(dpre, w, seq):
    t, c = dpre.shape
    ts, tc = CONV_TS, GROUP_W
    hb = ts // CONV_HALO
    last = t // CONV_HALO - 1
    d_spec = pl.BlockSpec((ts, tc), lambda j, i: (i, j))
    next_spec = pl.BlockSpec((CONV_HALO, tc), lambda j, i: (jnp.minimum((i + 1) * hb, last), j))
    w_spec = pl.BlockSpec((CONV_K, tc), lambda j, i: (0, j))

    def body(d_ref, next_ref, w_ref, o_ref, ext):
        i = pl.program_id(1)
        nts = seq // ts
        is_last = (i % nts) == nts - 1
        ext[0:ts, :] = d_ref[...]
        ext[ts:, :] = jnp.where(is_last, 0.0, next_ref[...])
        wv = w_ref[...]
        for r0 in range(0, ts, CONV_PIECE):
            acc = wv[CONV_K - 1:CONV_K] * ext[r0:r0 + CONV_PIECE, :]
            for q in range(CONV_K - 1):
                lo = r0 + CONV_K - 1 - q
                acc = acc + wv[q:q + 1] * ext[lo:lo + CONV_PIECE, :]
            o_ref[r0:r0 + CONV_PIECE, :] = acc.astype(o_ref.dtype)

    return pl.pallas_call(
        body, name="conv_bwd_in", grid=(c // tc, t // ts), in_specs=[d_spec, next_spec, w_spec],
        out_specs=d_spec, out_shape=SDS((t, c), BF16), scratch_shapes=[pltpu.VMEM((ts + CONV_HALO, tc), F32)],
        compiler_params=_cparams(("parallel", "arbitrary")))(dpre, dpre, w)


def _split3(v):
    hi = v.astype(BF16)
    r1 = v - hi.astype(F32)
    mid = r1.astype(BF16)
    lo = (r1 - mid.astype(F32)).astype(BF16)
    return hi, mid, lo


def _ssd_prelude(dtr_ref, dtrt_ref, bias_ref, biast_ref, a_ref, at_ref):
    dt = _softplus(dtr_ref[...] + bias_ref[...])
    dtt = _softplus(dtrt_ref[...] + biast_ref[...])
    ri = lax.broadcasted_iota(jnp.int32, (CHUNK, CHUNK), 0)
    ci = lax.broadcasted_iota(jnp.int32, (CHUNK, CHUNK), 1)
    lower = ri >= ci
    upper = ri <= ci
    lower_b = jnp.where(lower, 1.0, 0.0).astype(BF16)
    upper_b = jnp.where(upper, 1.0, 0.0).astype(BF16)
    acs = sum(_dot(lower_b, p) for p in _split3(dt * a_ref[...]))
    acst = sum(_dot(p, upper_b) for p in _split3(dtt * at_ref[...]))
    return dt, acs, acst, lower, upper, lower_b, upper_b


SSD_FWD_GPS = 2
SSD_BWD_GPS = 1


def _ssd_specs(seq, gps):
    nc = seq // CHUNK
    hg = HEADS_PER_GROUP
    fwd = lambda c: c
    rev = lambda c: nc - 1 - c

    def specs(cc):
        return dict(
            xc=pl.BlockSpec((CHUNK, gps * GROUP_W), lambda g, b, c: (b * nc + cc(c), g)),
            y=pl.BlockSpec((CHUNK, gps * hg * HEAD_P), lambda g, b, c: (b * nc + cc(c), g)),
            dtr=pl.BlockSpec((gps, CHUNK, hg), lambda g, b, c: (g, b * nc + cc(c), 0)),
            dtrt=pl.BlockSpec((gps, None, hg, CHUNK), lambda g, b, c: (g, b, 0, cc(c))),
            prow=pl.BlockSpec((gps, 1, hg), lambda g, b, c: (g, 0, 0)),
            pcol=pl.BlockSpec((gps, hg, 1), lambda g, b, c: (g, 0, 0)),
            st=pl.BlockSpec((gps, None, None, D_STATE, hg * HEAD_P), lambda g, b, c: (g, b, cc(c), 0, 0)),
        )

    return specs(fwd), specs(rev)


def _group_views(refs, lane_widths, gi):
    return [r.at[:, gi * w:(gi + 1) * w] if w else r.at[gi] for r, w in zip(refs, lane_widths)]


def _head_maps():
    hw = HEADS_PER_GROUP * HEAD_P
    shift = HEAD_P.bit_length() - 1
    hj = lax.broadcasted_iota(jnp.int32, (HEADS_PER_GROUP, hw), 0)
    lq = jnp.right_shift(lax.broadcasted_iota(jnp.int32, (HEADS_PER_GROUP, hw), 1), shift)
    spread = jnp.where(hj == lq, 1.0, 0.0).astype(BF16)
    rq = jnp.right_shift(lax.broadcasted_iota(jnp.int32, (hw, LANES), 0), shift)
    cj = lax.broadcasted_iota(jnp.int32, (hw, LANES), 1)
    gather = jnp.where(rq == cj, 1.0, 0.0).astype(BF16)
    return spread, gather


def _dot01(v, m01):
    hi, mid, _ = _split3(v)
    return _dot(hi, m01) + _dot(mid, m01)


class _Side(NamedTuple):
    ins: tuple
    out_shapes: tuple
    scratch: tuple
    first: Callable
    mid: Optional[Callable]
    last: Callable


NO_SIDE = _Side((), (), (), lambda *refs: None, None, lambda *refs: None)


def _attach_side(body, n_in, n_out, side, grid):
    si, so, ss = len(side.ins), len(side.out_shapes), len(side.scratch)

    def wrapped(*refs):
        ins, s_in = refs[:n_in], refs[n_in:n_in + si]
        outs = refs[n_in + si:n_in + si + n_out]
        s_out = refs[n_in + si + n_out:n_in + si + n_out + so]
        rest = refs[n_in + si + n_out + so:]
        scr, s_scr = rest[:len(rest) - ss], rest[len(rest) - ss:]
        ids = [pl.program_id(a) for a in range(len(grid))]
        inner_first = functools.reduce(lambda p, q: p & q, [i == 0 for i in ids[1:]], ids[0] >= 0)
        at_last = functools.reduce(lambda p, q: p & q, [i == n - 1 for i, n in zip(ids, grid)])

        @pl.when((ids[0] == 0) & inner_first)
        def _():
            side.first(s_in, s_out, s_scr)

        if side.mid is not None:
            outer_last = functools.reduce(lambda p, q: p & q, [i == n - 1 for i, n in zip(ids[:-1], grid[:-1])])

            @pl.when(outer_last & (ids[-1] == 0))
            def _():
                side.mid(s_in, s_out, s_scr)

        body(*ins, *outs, *scr)

        @pl.when(at_last)
        def _():
            side.last(s_in, s_out, s_scr)

    return wrapped


def _ssd_fwd(xc, dtr, dtrt, bias, biast, a, at, dskip, nb, seq, side):
    t = xc.shape[0]
    nc = seq // CHUNK
    hg = HEADS_PER_GROUP
    hw = hg * HEAD_P
    gps = SSD_FWD_GPS
    grid = (N_GROUPS // gps, nb, nc)
    sp, _ = _ssd_specs(seq, gps)

    def body(*refs):
        for gi in range(gps):
            one_group(*_group_views(refs, (GROUP_W, 0, 0, 0, 0, 0, 0, 0, hw, 0, 0), gi))

    def one_group(xc_ref, dtr_ref, dtrt_ref, bias_ref, biast_ref, a_ref, at_ref, d_ref, y_ref, sin_ref, st):
        @pl.when(pl.program_id(2) == 0)
        def _():
            st[...] = jnp.zeros_like(st)

        s_in = st[...]
        sin_ref[...] = s_in
        dt, acs, acst, lower, _, _, _ = _ssd_prelude(dtr_ref, dtrt_ref, bias_ref, biast_ref, a_ref, at_ref)
        spread, _ = _head_maps()
        x = xc_ref[...]
        xs = x[:, :hw]
        b16 = x[:, hw:hw + D_STATE].astype(BF16)
        c16 = x[:, hw + D_STATE:].astype(BF16)
        cb = _dot_nt(c16, b16)
        last = acs[CHUNK - 1:CHUNK, :]
        e_x = _dot01(jnp.exp(acs), spread)
        dec_x = _dot01(jnp.exp(last - acs), spread)
        tot_x = e_x[CHUNK - 1:CHUNK, :]
        d_x = _dot01(jnp.broadcast_to(d_ref[...], (8, hg)), spread)[0:1, :]
        xdtf = xs * _dot01(dt, spread)
        xdt16 = xdtf.astype(BF16)
        yoff = e_x * _dot(c16, s_in.astype(BF16))
        st[...] = tot_x * s_in + _dot_tn(b16, (dec_x * xdtf).astype(BF16))
        parts = []
        for j in range(hg):
            decay = jnp.exp(jnp.where(lower, acs[:, j:j + 1] - acst[j:j + 1, :], -jnp.inf))
            parts.append(_dot((cb * decay).astype(BF16), xdt16[:, HEAD_P * j:HEAD_P * (j + 1)]))
        y_ref[...] = (jnp.concatenate(parts, axis=-1) + yoff + d_x * xs).astype(y_ref.dtype)

    return pl.pallas_call(
        _attach_side(body, 8, 2, side, grid), name="ssd_fwd", grid=grid,
        in_specs=[sp["xc"], sp["dtr"], sp["dtrt"], sp["prow"], sp["pcol"], sp["prow"], sp["pcol"], sp["prow"]]
        + [ANY] * len(side.ins),
        out_specs=[sp["y"], sp["st"]] + [ANY] * len(side.out_shapes),
        out_shape=[SDS((t, D_INNER), BF16), SDS((N_GROUPS, nb, nc, D_STATE, hw), F32)] + list(side.out_shapes),
        scratch_shapes=[pltpu.VMEM((gps, D_STATE, hw), F32)] + list(side.scratch),
        compiler_params=_cparams(("arbitrary", "arbitrary", "arbitrary")))(
            xc, dtr, dtrt, bias, biast, a, at, dskip, *side.ins)


def _ssd_bwd(xc, dtr, dtrt, bias, biast, a, at, dskip, states, dy, nb, seq, side):
    t = xc.shape[0]
    nc = seq // CHUNK
    hg = HEADS_PER_GROUP
    hw = hg * HEAD_P
    gps = SSD_BWD_GPS
    grid = (N_GROUPS // gps, nb, nc)
    _, sp = _ssd_specs(seq, gps)

    def body(*refs):
        for gi in range(gps):
            one_group(*_group_views(refs, (GROUP_W, 0, 0, 0, 0, 0, 0, 0, 0, hw, GROUP_W, 0, 0, 0, 0, 0), gi))

    def one_group(xc_ref, dtr_ref, dtrt_ref, bias_ref, biast_ref, a_ref, at_ref, d_ref, sin_ref, dy_ref,
                  dxc_ref, ddtr_ref, gbias_ref, ga_ref, gd_ref, ds):
        first = (pl.program_id(1) == 0) & (pl.program_id(2) == 0)

        @pl.when(pl.program_id(2) == 0)
        def _():
            ds[...] = jnp.zeros_like(ds)

        @pl.when(first)
        def _():
            gbias_ref[...] = jnp.zeros_like(gbias_ref)
            ga_ref[...] = jnp.zeros_like(ga_ref)
            gd_ref[...] = jnp.zeros_like(gd_ref)

        dt, acs, acst, lower, upper, _, upper_b = _ssd_prelude(dtr_ref, dtrt_ref, bias_ref, biast_ref, a_ref, at_ref)
        spread, gather = _head_maps()
        x = xc_ref[...]
        dy = dy_ref[...].astype(F32)
        xs = x[:, :hw]
        b16 = x[:, hw:hw + D_STATE].astype(BF16)
        c16 = x[:, hw + D_STATE:].astype(BF16)
        dy16 = dy.astype(BF16)
        cb = _dot_nt(c16, b16)
        cbt = _dot_nt(b16, c16)
        last = acs[CHUNK - 1:CHUNK, :]
        e8 = jnp.exp(acs)
        dec8 = jnp.exp(last - acs)
        e_x = _dot01(e8, spread)
        dec_x = _dot01(dec8, spread)
        tot_x = e_x[CHUNK - 1:CHUNK, :]
        dt_x = _dot01(dt, spread)
        d_x = _dot01(jnp.broadcast_to(d_ref[...], (8, hg)), spread)[0:1, :]
        xdtf = xs * dt_x
        xdt16 = xdtf.astype(BF16)
        s_in = sin_ref[...]
        s16 = s_in.astype(BF16)
        ds_out = ds[...]
        ds16 = ds_out.astype(BF16)
        bds = _dot(b16, ds16)
        cs = _dot(c16, s16)
        edy16 = (e_x * dy).astype(BF16)
        ds[...] = tot_x * ds_out + _dot_tn(c16, edy16)
        lane8 = lax.broadcasted_iota(jnp.int32, (CHUNK, hg), 1)
        row8 = lax.broadcasted_iota(jnp.int32, (CHUNK, hg), 0)
        dacs8 = jnp.zeros((CHUNK, hg), F32)
        acc_m = jnp.zeros((CHUNK, CHUNK), F32)
        acc_mt = jnp.zeros((CHUNK, CHUNK), F32)
        dx_parts = []
        for j in range(hg):
            sl = slice(HEAD_P * j, HEAD_P * (j + 1))
            col = acs[:, j:j + 1]
            row = acst[j:j + 1, :]
            decay = jnp.exp(jnp.where(lower, col - row, -jnp.inf))
            decayt = jnp.exp(jnp.where(upper, row - col, -jnp.inf))
            wm = _dot_nt(dy16[:, sl], xdt16[:, sl]) * decay
            wmt = _dot_nt(xdt16[:, sl], dy16[:, sl]) * decayt
            acc_m = acc_m + wm
            acc_mt = acc_mt + wmt
            dacs8 = dacs8 + jnp.where(lane8 == j, jnp.sum(wm * cb, axis=-1, keepdims=True)
                                      - jnp.sum(wmt * cbt, axis=-1, keepdims=True), 0.0)
            dx_parts.append(_dot((cbt * decayt).astype(BF16), dy16[:, sl]))
        dx = jnp.concatenate(dx_parts, axis=-1) + dec_x * bds
        dxc_ref[:, :hw] = dx * dt_x + d_x * dy
        dxc_ref[:, hw:hw + D_STATE] = _dot(acc_mt.astype(BF16), c16) + _dot_nt((dec_x * xdtf).astype(BF16), ds16)
        dxc_ref[:, hw + D_STATE:] = _dot(acc_m.astype(BF16), b16) + _dot_nt(edy16, s16)
        dtot_rows = jnp.broadcast_to(_colsum(ds_out * s_in), (8, hw))
        sums = _dot01(jnp.concatenate([dy * cs, xdtf * bds, dx * xs, dy * xs, dtot_rows], axis=0), gather)
        de8 = sums[0:CHUNK, :hg]
        ddec8 = sums[CHUNK:2 * CHUNK, :hg]
        ddtx8 = sums[2 * CHUNK:3 * CHUNK, :hg]
        gd8 = _colsum(sums[3 * CHUNK:4 * CHUNK, :hg])
        dtot8 = sums[4 * CHUNK:4 * CHUNK + 1, :hg]
        extra = _colsum(ddec8 * dec8) + dtot8 * e8[CHUNK - 1:CHUNK, :]
        dacs8 = dacs8 + de8 * e8 - ddec8 * dec8 + jnp.where(row8 == CHUNK - 1, extra, 0.0)
        da = sum(_dot(upper_b, p) for p in _split3(dacs8))
        av = a_ref[...]
        ddt = da * av + ddtx8
        ddtr = ddt * _sigmoid(dtr_ref[...] + bias_ref[...])
        ddtr_ref[...] = ddtr
        gbias_ref[...] += _colsum(ddtr)
        ga_ref[...] += _colsum(da * dt) * av
        gd_ref[...] += gd8

    return pl.pallas_call(
        _attach_side(body, 10, 5, side, grid), name="ssd_bwd", grid=grid,
        in_specs=[sp["xc"], sp["dtr"], sp["dtrt"], sp["prow"], sp["pcol"], sp["prow"], sp["pcol"], sp["prow"],
                  sp["st"], sp["y"]] + [ANY] * len(side.ins),
        out_specs=[sp["xc"], sp["dtr"], sp["prow"], sp["prow"], sp["prow"]] + [ANY] * len(side.out_shapes),
        out_shape=[SDS((t, N_GROUPS * GROUP_W), F32), SDS((N_GROUPS, t, hg), F32)]
        + [SDS((N_GROUPS, 1, hg), F32)] * 3 + list(side.out_shapes),
        scratch_shapes=[pltpu.VMEM((gps, D_STATE, hw), F32)] + list(side.scratch),
        compiler_params=_cparams(("arbitrary", "arbitrary", "arbitrary")))(
            xc, dtr, dtrt, bias, biast, a, at, dskip, states, dy, *side.ins)


def _group_bcast(v, width, fn):
    parts = []
    for q in range(v.shape[-1] // width):
        s = fn(v[:, q * width:(q + 1) * width])
        parts.append(jnp.broadcast_to(s, (v.shape[0], width)))
    return jnp.concatenate(parts, axis=-1)


def _gate_norm_out(y, z, g, w):
    t, d = y.shape
    n = w.shape[1]
    tm = 256
    gw = d // N_GROUPS

    def body(y_ref, z_ref, g_ref, w_ref, yn_ref, o_ref):
        zv = z_ref[...].astype(F32)
        u = y_ref[...].astype(F32) * (zv * _sigmoid(zv))
        r = lax.rsqrt(_group_bcast(u * u, gw, lambda p: jnp.mean(p, axis=-1, keepdims=True)) + EPS)
        yn = (u * r * g_ref[...]).astype(yn_ref.dtype)
        yn_ref[...] = yn
        o_ref[...] = _dot(yn, w_ref[...]).astype(o_ref.dtype)

    return pl.pallas_call(
        body, name="gate_norm_ssm_out", grid=(t // tm,),
        in_specs=[_rs(tm, d), _rs(tm, d), _fs((1, d)), _fs((d, n))], out_specs=[_rs(tm, d), _rs(tm, n)],
        out_shape=[SDS((t, d), BF16), SDS((t, n), BF16)], compiler_params=_cparams(("arbitrary",)))(y, z, g, w)


def _gate_norm_bwd_epilogue(dv, rows, fulls):
    yv, zv = rows[0][...].astype(F32), rows[1][...].astype(F32)
    gw = yv.shape[-1] // N_GROUPS
    sg = _sigmoid(zv)
    sz = zv * sg
    u = yv * sz
    r = lax.rsqrt(_group_bcast(u * u, gw, lambda p: jnp.mean(p, axis=-1, keepdims=True)) + EPS)
    uhat = u * r
    duhat = dv * fulls[0][...]
    du = r * (duhat - uhat * _group_bcast(duhat * uhat, gw, lambda p: jnp.mean(p, axis=-1, keepdims=True)))
    dz = du * yv * sg * (1.0 + zv * (1.0 - sg))
    return [du * sz, dz, _colsum(dv * uhat)]


def _rope_tables(seq):
    half = ATT_D // 2
    inv = ROPE_THETA ** (-jnp.arange(half, dtype=F32) / half)
    ang = jnp.arange(seq, dtype=F32)[:, None] * inv[None, :]
    cos, sin = jnp.cos(ang), jnp.sin(ang)
    return jnp.concatenate([cos, cos], axis=-1), jnp.concatenate([-sin, sin], axis=-1)


ATT_TILE = 512
ATT_QB = 8


def _strided_spec(r, mtiles):
    return pl.BlockSpec((None, r, None, ATT_TILE // r, ATT_W), lambda i: (i // mtiles, 0, i % mtiles, 0, 0))


def _strided_shape(nb, r, mtiles, dtype):
    return SDS((nb, r, mtiles, ATT_TILE // r, ATT_W), dtype)


def _to_strided(val, out_ref, lanes, r, sc):
    if r == 1:
        out_ref[0, :, lanes] = val.astype(out_ref.dtype)
        return
    sc[...] = val
    for rr in range(r):
        out_ref[rr, :, lanes] = sc[pl.ds(rr, ATT_TILE // r, stride=r), :].astype(out_ref.dtype)


def _from_strided(in_ref, lanes, r, sc):
    if r == 1:
        return in_ref[0, :, lanes].astype(F32)
    for rr in range(r):
        sc[pl.ds(rr, ATT_TILE // r, stride=r), :] = in_ref[rr, :, lanes].astype(F32)
    return sc[...]


def _rope_fwd(qkv, cos, sin, nb, seq):
    t = qkv.shape[0]
    tm = ATT_TILE
    mtiles = seq // tm
    w = ATT_HEADS * ATT_D
    tab = pl.BlockSpec((tm, ATT_D), lambda i: (i % mtiles, 0))
    ng = len(ATT_DILATIONS)

    def body(q_ref, k_ref, v_ref, cos_ref, sin_ref, *rest):
        outs, sc = rest[:3 * ng], rest[3 * ng]
        c, s = cos_ref[...], sin_ref[...]
        for which, ref in enumerate((q_ref, k_ref, v_ref)):
            for h in range(ATT_HEADS):
                g, slot = divmod(h, ATT_SLOTS)
                p = ref[:, h * ATT_D:(h + 1) * ATT_D].astype(F32)
                if which < 2:
                    p = p * c + pltpu.roll(p, ATT_D // 2, 1) * s
                _to_strided(p, outs[which * ng + g], slice(slot * ATT_D, (slot + 1) * ATT_D), ATT_DILATIONS[g], sc)

    out_specs = [_strided_spec(r, mtiles) for _ in range(3) for r in ATT_DILATIONS]
    out_shape = [_strided_shape(nb, r, mtiles, BF16) for _ in range(3) for r in ATT_DILATIONS]
    outs = pl.pallas_call(
        body, name="rope_fwd", grid=(t // tm,),
        in_specs=[_rs(tm, w, 0), _rs(tm, w, 1), _rs(tm, w, 2), tab, tab], out_specs=out_specs, out_shape=out_shape,
        scratch_shapes=[pltpu.VMEM((tm, ATT_D), F32)], compiler_params=_cparams(("arbitrary",)))(
            qkv, qkv, qkv, cos, sin)
    flat = [o.reshape(t, ATT_W) for o in outs]
    return flat[0:ng], flat[ng:2 * ng], flat[2 * ng:]


def _rope_bwd(dq, dk, dv, cos, sin, nb, seq):
    t = dq[0].shape[0]
    tm = ATT_TILE
    mtiles = seq // tm
    w = ATT_HEADS * ATT_D
    tab = pl.BlockSpec((tm, ATT_D), lambda i: (i % mtiles, 0))
    ng = len(ATT_DILATIONS)

    def body(*refs):
        ins, (cos_ref, sin_ref, o_ref, sc) = refs[:3 * ng], refs[3 * ng:]
        c, s = cos_ref[...], sin_ref[...]
        for which in range(3):
            for h in range(ATT_HEADS):
                g, slot = divmod(h, ATT_SLOTS)
                p = _from_strided(ins[which * ng + g], slice(slot * ATT_D, (slot + 1) * ATT_D), ATT_DILATIONS[g], sc)
                if which < 2:
                    p = p * c - pltpu.roll(p, ATT_D // 2, 1) * s
                o_ref[:, which * w + h * ATT_D:which * w + (h + 1) * ATT_D] = p.astype(o_ref.dtype)

    views = [a.reshape(nb, r, mtiles, tm // r, ATT_W) for grp in (dq, dk, dv) for a, r in zip(grp, ATT_DILATIONS)]
    return pl.pallas_call(
        body, name="rope_bwd", grid=(t // tm,),
        in_specs=[_strided_spec(r, mtiles) for _ in range(3) for r in ATT_DILATIONS] + [tab, tab],
        out_specs=_rs(tm, 3 * w), out_shape=SDS((t, 3 * w), BF16),
        scratch_shapes=[pltpu.VMEM((tm, ATT_D), F32)], compiler_params=_cparams(("arbitrary",)))(*views, cos, sin)


def _att_masks():
    ri = lax.broadcasted_iota(jnp.int32, (ATT_BLOCK, ATT_BLOCK), 0)
    ci = lax.broadcasted_iota(jnp.int32, (ATT_BLOCK, ATT_BLOCK), 1)
    return ci <= ri, ci >= ri


def _att_fwd(q, k, v, g, seq):
    t, w = q.shape
    rows = ATT_QB * ATT_BLOCK
    nbs = seq // ATT_DILATIONS[g] // ATT_BLOCK
    scale = ATT_D ** -0.5
    cur = pl.BlockSpec((rows, w), lambda n: (n, 0))
    prev = pl.BlockSpec((ATT_BLOCK, w), lambda n: (jnp.maximum(n * ATT_QB - 1, 0), 0))

    def body(q_ref, kc_ref, kp_ref, vc_ref, vp_ref, o_ref, lse_ref):
        mcur, mprev = _att_masks()
        for i in range(ATT_QB):
            blk = pl.program_id(0) * ATT_QB + i
            mask = jnp.concatenate([mprev & ((blk % nbs) != 0), mcur], axis=-1)
            own = slice(i * ATT_BLOCK, (i + 1) * ATT_BLOCK)
            for h in range(ATT_SLOTS):
                sl = slice(h * ATT_D, (h + 1) * ATT_D)
                if i == 0:
                    keys = jnp.concatenate([kp_ref[:, sl], kc_ref[own, sl]], axis=0)
                    vals = jnp.concatenate([vp_ref[:, sl], vc_ref[own, sl]], axis=0)
                else:
                    both = slice((i - 1) * ATT_BLOCK, (i + 1) * ATT_BLOCK)
                    keys, vals = kc_ref[both, sl], vc_ref[both, sl]
                s = jnp.where(mask, _dot_nt(q_ref[own, sl], keys) * scale, -jnp.inf)
                m = jnp.max(s, axis=-1, keepdims=True)
                p = jnp.exp(s - m)
                den = jnp.sum(p, axis=-1, keepdims=True)
                o_ref[own, sl] = _dot(p.astype(BF16), vals) / den
                lse_ref[own, sl] = jnp.broadcast_to(m + jnp.log(den), (ATT_BLOCK, ATT_D))

    return pl.pallas_call(
        body, name=f"att_fwd_{g}", grid=(t // rows,), in_specs=[cur, cur, prev, cur, prev], out_specs=[cur, cur],
        out_shape=[SDS((t, w), F32), SDS((t, w), F32)],
        compiler_params=_cparams(("arbitrary",)))(q, k, k, v, v)


def _att_bwd(q, k, v, do, lse, dlt, g, seq):
    t, w = q.shape
    nblk = t // ATT_BLOCK
    rows = ATT_QB * ATT_BLOCK
    nbs = seq // ATT_DILATIONS[g] // ATT_BLOCK
    scale = ATT_D ** -0.5
    cur = pl.BlockSpec((rows, w), lambda n: (n, 0))
    nxt = pl.BlockSpec((ATT_BLOCK, w), lambda n: (jnp.minimum((n + 1) * ATT_QB, nblk - 1), 0))

    def body(qc_ref, qn_ref, k_ref, v_ref, doc_ref, don_ref, lsec_ref, lsen_ref, dltc_ref, dltn_ref,
             dq_ref, dk_ref, dv_ref, carry):
        n = pl.program_id(0)

        @pl.when(n == 0)
        def _():
            carry[...] = jnp.zeros_like(carry)

        mcur, mprev = _att_masks()

        def pair(cur_ref, nxt_ref, i, sl):
            if i + 1 < ATT_QB:
                return cur_ref[i * ATT_BLOCK:(i + 2) * ATT_BLOCK, sl]
            return jnp.concatenate([cur_ref[i * ATT_BLOCK:, sl], nxt_ref[:, sl]], axis=0)

        for h in range(ATT_SLOTS):
            sl = slice(h * ATT_D, (h + 1) * ATT_D)
            from_prev = carry[:, sl]
            for i in range(ATT_QB):
                blk = n * ATT_QB + i
                has_next = (((blk + 1) % nbs) != 0) & (blk + 1 < nblk)
                mask = jnp.concatenate([mcur, mprev & has_next], axis=0)
                own = slice(i * ATT_BLOCK, (i + 1) * ATT_BLOCK)
                kh, vh = k_ref[own, sl], v_ref[own, sl]
                qs, dos = pair(qc_ref, qn_ref, i, sl), pair(doc_ref, don_ref, i, sl)
                lse, dlt = pair(lsec_ref, lsen_ref, i, sl), pair(dltc_ref, dltn_ref, i, sl)
                p = jnp.where(mask, jnp.exp(_dot_nt(qs, kh) * scale - lse), 0.0)
                ds = (p * (_dot_nt(dos, vh) - dlt) * scale).astype(BF16)
                dqs = _dot(ds, kh)
                dq_ref[own, sl] = (from_prev + dqs[:ATT_BLOCK]).astype(dq_ref.dtype)
                from_prev = dqs[ATT_BLOCK:]
                dk_ref[own, sl] = _dot_tn(ds, qs).astype(dk_ref.dtype)
                dv_ref[own, sl] = _dot_tn(p.astype(BF16), dos).astype(dv_ref.dtype)
            carry[:, sl] = from_prev

    return pl.pallas_call(
        body, name=f"att_bwd_{g}", grid=(t // rows,), in_specs=[cur, nxt, cur, cur, cur, nxt, cur, nxt, cur, nxt],
        out_specs=[cur, cur, cur], out_shape=[SDS((t, w), BF16)] * 3,
        scratch_shapes=[pltpu.VMEM((ATT_BLOCK, w), F32)],
        compiler_params=_cparams(("arbitrary",)))(q, q, k, v, do, do, lse, lse, dlt, dlt)


def _merge_weights(ls):
    m = jnp.maximum(jnp.maximum(ls[0], ls[1]), ls[2])
    es = [jnp.exp(v - m) for v in ls]
    den = es[0] + es[1] + es[2]
    return [e / den for e in es]


def _merge_fwd(o, lse, nb, seq):
    t = o[0].shape[0]
    tm = ATT_TILE
    mtiles = seq // tm
    ng = len(ATT_DILATIONS)

    def body(*refs):
        o_refs, l_refs, out_ref, scs = refs[:ng], refs[ng:2 * ng], refs[2 * ng], refs[2 * ng + 1:]
        for slot in range(ATT_SLOTS):
            lanes = slice(slot * ATT_D, (slot + 1) * ATT_D)
            ov = [_from_strided(o_refs[g], lanes, r, scs[2 * g]) for g, r in enumerate(ATT_DILATIONS)]
            ws = _merge_weights([_from_strided(l_refs[g], lanes, r, scs[2 * g + 1])
                                 for g, r in enumerate(ATT_DILATIONS)])
            out_ref[:, lanes] = (ws[0] * ov[0] + ws[1] * ov[1] + ws[2] * ov[2]).astype(out_ref.dtype)

    views = [a.reshape(nb, r, mtiles, tm // r, ATT_W) for grp in (o, lse) for a, r in zip(grp, ATT_DILATIONS)]
    return pl.pallas_call(
        body, name="att_merge_fwd", grid=(t // tm,),
        in_specs=[_strided_spec(r, mtiles) for _ in range(2) for r in ATT_DILATIONS],
        out_specs=_rs(tm, ATT_W), out_shape=SDS((t, ATT_W), BF16),
        scratch_shapes=[pltpu.VMEM((tm, ATT_D), F32)] * (2 * ng), compiler_params=_cparams(("arbitrary",)))(*views)


def _merge_bwd(o, lse, datt, nb, seq):
    t = o[0].shape[0]
    tm = ATT_TILE
    mtiles = seq // tm
    ng = len(ATT_DILATIONS)

    def body(*refs):
        o_refs, l_refs, d_ref = refs[:ng], refs[ng:2 * ng], refs[2 * ng]
        do_refs, dlt_refs = refs[2 * ng + 1:3 * ng + 1], refs[3 * ng + 1:4 * ng + 1]
        scs = refs[4 * ng + 1:]
        for slot in range(ATT_SLOTS):
            lanes = slice(slot * ATT_D, (slot + 1) * ATT_D)
            ov = [_from_strided(o_refs[g], lanes, r, scs[2 * g]) for g, r in enumerate(ATT_DILATIONS)]
            ws = _merge_weights([_from_strided(l_refs[g], lanes, r, scs[2 * g + 1])
                                 for g, r in enumerate(ATT_DILATIONS)])
            dv = d_ref[:, lanes]
            att = ws[0] * ov[0] + ws[1] * ov[1] + ws[2] * ov[2]
            dot = jnp.broadcast_to(jnp.sum(dv * att, axis=-1, keepdims=True), (tm, ATT_D))
            for g, r in enumerate(ATT_DILATIONS):
                _to_strided(ws[g] * dv, do_refs[g], lanes, r, scs[2 * ng])
                _to_strided(ws[g] * dot, dlt_refs[g], lanes, r, scs[2 * ng + 1])

    views = [a.reshape(nb, r, mtiles, tm // r, ATT_W) for grp in (o, lse) for a, r in zip(grp, ATT_DILATIONS)]
    outs = pl.pallas_call(
        body, name="att_merge_bwd", grid=(t // tm,),
        in_specs=[_strided_spec(r, mtiles) for _ in range(2) for r in ATT_DILATIONS] + [_rs(tm, ATT_W)],
        out_specs=[_strided_spec(r, mtiles) for _ in range(2) for r in ATT_DILATIONS],
        out_shape=[_strided_shape(nb, r, mtiles, dt) for dt in (BF16, F32) for r in ATT_DILATIONS],
        scratch_shapes=[pltpu.VMEM((tm, ATT_D), F32)] * (2 * ng + 2), compiler_params=_cparams(("arbitrary",)))(
            *views, datt)
    flat = [a.reshape(t, ATT_W) for a in outs]
    return flat[:ng], flat[ng:]


def _branch_gates(rows, fulls):
    return (_sigmoid(rows[0][...].astype(F32) + fulls[0][...]), _sigmoid(rows[1][...].astype(F32) + fulls[1][...]))


def _mix_fwd_epilogue(y_att, rows, fulls):
    g0, g1 = _branch_gates(rows, fulls)
    return [y_att, g0 * rows[2][...].astype(F32) + g1 * y_att]


def _mix_bwd_epilogue(dm, rows, fulls):
    g0, g1 = _branch_gates(rows, fulls)
    dg = jnp.concatenate([dm * rows[2][...].astype(F32) * g0 * (1.0 - g0),
                          dm * rows[3][...].astype(F32) * g1 * (1.0 - g1)], axis=-1)
    return [dm * g0, dm * g1, dg, _colsum(dg)]


FFN_TM = 512


def _ffn_in(h2, wg_t, wu_t):
    t, d = h2.shape
    f = wg_t.shape[0]
    tm, tn = FFN_TM, _pick(f, 1536)

    def body(a_ref, g_ref, u_ref, gt_ref, up_ref, act_ref):
        a = a_ref[...]
        gt = _dot_nt(a, g_ref[...])
        up = _dot_nt(a, u_ref[...])
        gt_ref[...] = gt.astype(BF16)
        up_ref[...] = up.astype(BF16)
        act_ref[...] = (gt * _sigmoid(gt) * up).astype(BF16)

    a_spec = pl.BlockSpec((tm, d), lambda j, i: (i, 0))
    w_spec = pl.BlockSpec((tn, d), lambda j, i: (j, 0))
    o_spec = pl.BlockSpec((tm, tn), lambda j, i: (i, j))
    return pl.pallas_call(
        body, name="ffn_in", grid=(f // tn, t // tm), in_specs=[a_spec, w_spec, w_spec],
        out_specs=[o_spec] * 3, out_shape=[SDS((t, f), BF16)] * 3,
        compiler_params=_cparams(("parallel", "arbitrary")))(h2, wg_t, wu_t)


def _ffn_bwd_in(dx2, w_down, gt, up):
    t, d = dx2.shape
    f = w_down.shape[0]
    tm, tn = FFN_TM, _pick(f, 1536)

    def body(a_ref, w_ref, g_ref, u_ref, dgt_ref, dup_ref):
        dv = _dot_nt(a_ref[...], w_ref[...])
        gv = g_ref[...].astype(F32)
        sg = _sigmoid(gv)
        dgt_ref[...] = (dv * u_ref[...].astype(F32) * sg * (1.0 + gv * (1.0 - sg))).astype(BF16)
        dup_ref[...] = (dv * gv * sg).astype(BF16)

    a_spec = pl.BlockSpec((tm, d), lambda j, i: (i, 0))
    w_spec = pl.BlockSpec((tn, d), lambda j, i: (j, 0))
    o_spec = pl.BlockSpec((tm, tn), lambda j, i: (i, j))
    return pl.pallas_call(
        body, name="ffn_bwd_in", grid=(f // tn, t // tm), in_specs=[a_spec, w_spec, o_spec, o_spec],
        out_specs=[o_spec] * 2, out_shape=[SDS((t, f), BF16)] * 2,
        compiler_params=_cparams(("parallel", "arbitrary")))(dx2, w_down, gt, up)


def _adamw(w, g, m, v, name):
    r, c = w.shape[-2:]
    lead = w.ndim - 2
    tr = _row_tile(r, max(8, 400_000 // c))
    c1 = 1.0 / (1.0 - ADAM_B1 ** ADAM_STEP)
    c2 = 1.0 / (1.0 - ADAM_B2 ** ADAM_STEP)

    def fn(i, w_ref, g_ref, m_ref, v_ref):
        gv = g_ref[...]
        mn = ADAM_B1 * m_ref[...] + (1.0 - ADAM_B1) * gv
        vn = ADAM_B2 * v_ref[...] + (1.0 - ADAM_B2) * (gv * gv)
        delta = -ADAM_LR * ((mn * c1) / (jnp.sqrt(vn * c2) + ADAM_EPS) + ADAM_WD * w_ref[...])
        return [delta, mn, vn]

    spec = pl.BlockSpec((None,) * lead + (tr, c), lambda i: (0,) * lead + (i, 0))
    return _rw(name, fn, r // tr, [(w, spec), (g, spec), (m, spec), (v, spec)], [(SDS(w.shape, F32), spec)] * 3)


ANY = pl.BlockSpec(memory_space=pl.ANY)


def _place():
    x, y, c = lax.axis_index("x"), lax.axis_index("y"), lax.axis_index("c")
    chips = [(1 - x, y), (x, 1 - y), (1 - x, 1 - y)]
    return x, y, c, chips


def _remote(src, dst, ssem, rsem, to):
    return pltpu.make_async_remote_copy(src_ref=src, dst_ref=dst, send_sem=ssem, recv_sem=rsem, device_id=to,
                                        device_id_type=MESH)


def _copy_through_vmem(src, dst, buf, isem, osem):
    chunk = buf.shape[1]
    n = src.shape[0] // chunk
    load = lambda k: pltpu.make_async_copy(src.at[pl.ds(k * chunk, chunk)], buf.at[k % 2], isem.at[k % 2])
    store = lambda k: pltpu.make_async_copy(buf.at[k % 2], dst.at[pl.ds(k * chunk, chunk)], osem.at[k % 2])
    load(0).start()
    for k in range(n):
        load(k).wait()
        if k + 1 < n:
            if k >= 1:
                store(k - 1).wait()
            load(k + 1).start()
        store(k).start()
    if n >= 2:
        store(n - 2).wait()
    store(n - 1).wait()


def _copy_scratch(rows, width, dtype):
    chunk = _row_tile(rows, 512)
    return [pltpu.VMEM((2, chunk, width), dtype), pltpu.SemaphoreType.DMA((2,)), pltpu.SemaphoreType.DMA((2,))]


def _gather_weights(wp):
    def body(w_ref, out_ref, ssem, rsem, buf, isem, osem):
        x, y, c, chips = _place()
        me = 2 * x + y
        sib = (x, y, 1 - c)
        first = [_remote(w_ref.at[c], out_ref.at[me, c], ssem.at[j], rsem.at[j], (*chip, c))
                 for j, chip in enumerate(chips)]
        for cp in first:
            cp.start()
        for half in range(2):
            _copy_through_vmem(w_ref.at[half], out_ref.at[me, half], buf, isem, osem)
        passed = []
        for j, chip in enumerate(chips):
            ci = 2 * chip[0] + chip[1]
            _remote(w_ref.at[c], out_ref.at[ci, c], ssem.at[j], rsem.at[j], (*chip, c)).wait_recv()
            cp = _remote(out_ref.at[ci, c], out_ref.at[ci, c], ssem.at[3 + j], rsem.at[3 + j], sib)
            cp.start()
            passed.append(cp)
        for j, chip in enumerate(chips):
            ci = 2 * chip[0] + chip[1]
            _remote(out_ref.at[ci, 1 - c], out_ref.at[ci, 1 - c], ssem.at[3 + j], rsem.at[3 + j], sib).wait_recv()
        for cp in first + passed:
            cp.wait_send()

    return pl.pallas_call(
        body, name="gather_weights", in_specs=[ANY], out_specs=ANY,
        out_shape=SDS((N_CHIPS,) + wp.shape, wp.dtype),
        scratch_shapes=[pltpu.SemaphoreType.DMA((6,)), pltpu.SemaphoreType.DMA((6,))]
        + _copy_scratch(wp.shape[1], wp.shape[2], wp.dtype),
        compiler_params=pltpu.CompilerParams(has_side_effects=True))(wp)


def _swap_halves(g2, tag):
    nch = g2.shape[0]

    def body(g_ref, out_ref, ssem, rsem):
        x, y, c, _ = _place()
        cps = [_remote(g_ref.at[k, 1 - c], out_ref.at[k], ssem.at[k], rsem.at[k], (x, y, 1 - c)) for k in range(nch)]
        for cp in cps:
            cp.start()
        for cp in cps:
            cp.wait()

    return pl.pallas_call(
        body, name="swap_halves_" + tag, in_specs=[ANY], out_specs=ANY,
        out_shape=SDS((nch,) + g2.shape[2:], g2.dtype),
        scratch_shapes=[pltpu.SemaphoreType.DMA((nch,)), pltpu.SemaphoreType.DMA((nch,))],
        compiler_params=pltpu.CompilerParams(has_side_effects=True))(g2)


def _add_own_half(g2, other, c, tag):
    nch, _, rows, w = g2.shape
    tr = _row_tile(rows, 512)
    nr = rows // tr

    def body(c_ref, a_ref, b_ref, o_ref):
        o_ref[...] = (a_ref[...].astype(F32) + b_ref[...].astype(F32)).astype(o_ref.dtype)

    grid_spec = pltpu.PrefetchScalarGridSpec(
        num_scalar_prefetch=1, grid=(nch, nr),
        in_specs=[pl.BlockSpec((None, None, tr, w), lambda k, i, c_ref: (k, c_ref[0], i, 0)),
                  pl.BlockSpec((None, tr, w), lambda k, i, c_ref: (k, i, 0))],
        out_specs=pl.BlockSpec((None, tr, w), lambda k, i, c_ref: (k, i, 0)))
    return pl.pallas_call(
        body, name="add_own_half_" + tag, grid_spec=grid_spec, out_shape=SDS(other.shape, other.dtype),
        compiler_params=_cparams(("arbitrary", "arbitrary")))(jnp.reshape(c, (1,)).astype(jnp.int32), g2, other)


def _sum_chips(q, tag):
    nch, rows, w = q.shape
    tr = _row_tile(rows, 512)

    def fn(i, q_ref):
        return [((q_ref[0].astype(F32) + q_ref[1].astype(F32)) + q_ref[2].astype(F32)) + q_ref[3].astype(F32)]

    return _rw("sum_chips_" + tag, fn, rows // tr, [(q, pl.BlockSpec((nch, tr, w), lambda i: (0, i, 0)))],
               [(SDS((rows, w), F32), _rs(tr, w))])[0]


def _chip_copies(src_ref, dst_ref, ssem, rsem, outgoing):
    x, y, c, chips = _place()
    me = 2 * x + y
    cps = []
    for j, chip in enumerate(chips):
        ci = 2 * chip[0] + chip[1]
        cps.append(_remote(src_ref.at[ci], dst_ref.at[me if outgoing else ci], ssem.at[j], rsem.at[j], (*chip, c)))
    return cps, me


def _scatter_side(p):
    def first(ins, outs, scr):
        cps, me = _chip_copies(ins[0], outs[0], scr[0], scr[1], True)
        for cp in cps:
            cp.start()
        pltpu.make_async_copy(ins[0].at[me], outs[0].at[me], scr[2]).start()

    def last(ins, outs, scr):
        for cp in _chip_copies(ins[0], outs[0], scr[0], scr[1], False)[0]:
            cp.wait_recv()
        cps, me = _chip_copies(ins[0], outs[0], scr[0], scr[1], True)
        for cp in cps:
            cp.wait_send()
        pltpu.make_async_copy(ins[0].at[me], outs[0].at[me], scr[2]).wait()

    return _Side((p,), (SDS(p.shape, p.dtype),),
                 (pltpu.SemaphoreType.DMA((3,)), pltpu.SemaphoreType.DMA((3,)), pltpu.SemaphoreType.DMA(())),
                 first, None, last)


def _gather_copies(w_ref, out_ref, ssem, rsem):
    x, y, c, chips = _place()
    me = 2 * x + y
    sib = (x, y, 1 - c)
    sends, arrivals, forwards, from_sib = [], [], [], []
    for j, chip in enumerate(chips):
        ci = 2 * chip[0] + chip[1]
        sends.append(_remote(w_ref.at[c], out_ref.at[me, c], ssem.at[j], rsem.at[j], (*chip, c)))
        arrivals.append(_remote(w_ref.at[c], out_ref.at[ci, c], ssem.at[j], rsem.at[j], (*chip, c)))
        forwards.append(_remote(out_ref.at[ci, c], out_ref.at[ci, c], ssem.at[3 + j], rsem.at[3 + j], sib))
        from_sib.append(_remote(out_ref.at[ci, 1 - c], out_ref.at[ci, 1 - c], ssem.at[3 + j], rsem.at[3 + j], sib))
    return sends, arrivals, forwards, from_sib, me


def _gather_side(wp):
    def first(ins, outs, scr):
        sends, _, _, _, me = _gather_copies(ins[0], outs[0], scr[0], scr[1])
        for cp in sends:
            cp.start()
        pltpu.make_async_copy(ins[0], outs[0].at[me], scr[2]).start()

    def mid(ins, outs, scr):
        _, arrivals, forwards, _, _ = _gather_copies(ins[0], outs[0], scr[0], scr[1])
        for arrived, forward in zip(arrivals, forwards):
            arrived.wait_recv()
            forward.start()

    def last(ins, outs, scr):
        sends, _, forwards, from_sib, me = _gather_copies(ins[0], outs[0], scr[0], scr[1])
        for cp in from_sib:
            cp.wait_recv()
        for cp in sends + forwards:
            cp.wait_send()
        pltpu.make_async_copy(ins[0], outs[0].at[me], scr[2]).wait()

    return _Side((wp,), (SDS((N_CHIPS,) + wp.shape, wp.dtype),),
                 (pltpu.SemaphoreType.DMA((6,)), pltpu.SemaphoreType.DMA((6,)), pltpu.SemaphoreType.DMA(())),
                 first, mid, last)


def _allreduce_small(v, name):
    rows, w = v.shape
    offsets = [(dx, dy, dc) for dx in (0, 1) for dy in (0, 1) for dc in (0, 1)][1:]

    def body(v_ref, o_ref, buf, ssem, rsem):
        x, y, c, _ = _place()
        flip = lambda p, d: 1 - p if d else p
        peers = [(flip(x, dx), flip(y, dy), flip(c, dc)) for dx, dy, dc in offsets]
        index = lambda p: 4 * p[0] + 2 * p[1] + p[2]
        me = index((x, y, c))
        buf[me] = v_ref[...]
        sent = [_remote(v_ref, buf.at[me], ssem.at[q], rsem.at[q], p) for q, p in enumerate(peers)]
        for cp in sent:
            cp.start()
        for q, p in enumerate(peers):
            _remote(v_ref, buf.at[index(p)], ssem.at[q], rsem.at[q], p).wait_recv()
        for cp in sent:
            cp.wait_send()
        acc = buf[0]
        for q in range(1, 8):
            acc = acc + buf[q]
        o_ref[...] = acc

    vm = pl.BlockSpec(memory_space=pltpu.VMEM)
    return pl.pallas_call(
        body, name=name, in_specs=[vm], out_specs=vm, out_shape=SDS((rows, w), F32),
        scratch_shapes=[pltpu.VMEM((8, rows, w), F32), pltpu.SemaphoreType.DMA((7,)), pltpu.SemaphoreType.DMA((7,))],
        compiler_params=pltpu.CompilerParams(has_side_effects=True))(v)


def _join_halves(h, tag):
    def body(h_ref, out_ref, ssem, rsem, buf, isem, osem):
        x, y, c, _ = _place()
        cp = _remote(h_ref, out_ref.at[c], ssem, rsem, (x, y, 1 - c))
        cp.start()
        _copy_through_vmem(h_ref, out_ref.at[c], buf, isem, osem)
        _remote(h_ref, out_ref.at[1 - c], ssem, rsem, (x, y, 1 - c)).wait_recv()
        cp.wait_send()

    return pl.pallas_call(
        body, name="join_halves_" + tag, in_specs=[ANY], out_specs=ANY, out_shape=SDS((2,) + h.shape, h.dtype),
        scratch_shapes=[pltpu.SemaphoreType.DMA(()), pltpu.SemaphoreType.DMA(())]
        + _copy_scratch(h.shape[0], h.shape[1], h.dtype),
        compiler_params=pltpu.CompilerParams(has_side_effects=True))(h)


PACK_W = 1024
SHARDED = ("w_in", "w_ffn_gate", "w_ffn_up", "w_ssm_out", "w_att_out", "w_mix_out", "w_ffn_down")
COL_SHARDED = ("w_in", "w_ffn_gate", "w_ffn_up", "w_att_out")
SMALL = ("norm_mix", "b_gate", "conv_b", "dt_bias", "a_log", "d_skip", "ssm_norm", "norm_ffn", "norm_final")


PACK_ROW_ALIGN = 16


def _rows(n):
    return -(-n // (PACK_W * PACK_ROW_ALIGN)) * PACK_ROW_ALIGN


def _pack_rows(parts, total_rows):
    rows = []
    for p in parts:
        size = int(p.size)
        if size % PACK_W:
            p = jnp.pad(p.reshape(-1), (0, PACK_W - size % PACK_W))
        p = p.reshape(-1, PACK_W)
        rows.append(jnp.pad(p, ((0, _rows(size) - p.shape[0]), (0, 0))))
    used = sum(r.shape[0] for r in rows)
    if total_rows > used:
        rows.append(jnp.zeros((total_rows - used, PACK_W), rows[0].dtype))
    return jnp.concatenate(rows, axis=0)


def _padded_rows(n):
    return -(-n // 32) * 32


def _wire_name(name):
    return name + "_t" if name in COL_SHARDED else name


def _wire_shard(w, name):
    return w.T if name in COL_SHARDED else w


def _group_major(a, axis):
    gw = D_INNER // N_GROUPS
    take = lambda lo, n: lax.slice_in_dim(a, lo, lo + n, axis=axis)
    parts = []
    for g in range(N_GROUPS):
        parts += [take(g * gw, gw), take(D_INNER + g * D_STATE, D_STATE),
                  take(D_INNER + N_GROUPS * D_STATE + g * D_STATE, D_STATE)]
    return jnp.concatenate(parts, axis=axis)


def _group_major_inv(a, axis):
    gw = D_INNER // N_GROUPS
    take = lambda lo, n: lax.slice_in_dim(a, lo, lo + n, axis=axis)
    xs = [take(g * GROUP_W, gw) for g in range(N_GROUPS)]
    bs = [take(g * GROUP_W + gw, D_STATE) for g in range(N_GROUPS)]
    cs = [take(g * GROUP_W + gw + D_STATE, D_STATE) for g in range(N_GROUPS)]
    return jnp.concatenate(xs + bs + cs, axis=axis)


LATE = ("w_ffn_gate_t", "w_ffn_up_t", "w_ssm_out", "w_att_out_t", "w_mix_out", "w_ffn_down")


class _Overlap(NamedTuple):
    gather_side: _Side
    late_weights: Callable
    scatter_side: Callable
    scatter_in: Callable


def _local_step(x, target, wts, overlap):
    nb, seq, d = x.shape
    t = nb * seq
    x = x.reshape(t, d)
    target = target.reshape(t, d)
    hg = HEADS_PER_GROUP

    o1, o2, o3, o4 = D_INNER, D_INNER + CONV_DIM, D_INNER + CONV_DIM + N_HEADS, D_INNER + CONV_DIM + N_HEADS + QKV_DIM
    n_in = o4 + 2 * D_MODEL

    def in_rows(lo, hi):
        per = n_in // N_CHIPS
        parts = [wts["w_in_t"][k, max(lo, k * per) - k * per:min(hi, (k + 1) * per) - k * per]
                 for k in range(N_CHIPS) if max(lo, k * per) < min(hi, (k + 1) * per)]
        return parts[0] if len(parts) == 1 else jnp.concatenate(parts, axis=0)

    w_z = in_rows(0, o1)
    w_xbc = _group_major(in_rows(o1, o2), 0)
    w_dt = jnp.pad(in_rows(o2, o3), ((0, DT_PAD - N_HEADS), (0, 0)))
    w_qkv = in_rows(o3, o4)
    w_gate = in_rows(o4, n_in)
    conv_w = _group_major(wts["conv_w"], 1)
    conv_b = _group_major(wts["conv_b"], 1)

    def per_group_row(p):
        return p.reshape(N_GROUPS, 1, hg)

    def per_group_col(p):
        return p.reshape(N_GROUPS, hg, 1)

    a_neg = -jnp.exp(wts["a_log"])
    bias_r, bias_c = per_group_row(wts["dt_bias"]), per_group_col(wts["dt_bias"])
    a_r, a_c = per_group_row(a_neg), per_group_col(a_neg)
    dskip_r = per_group_row(wts["d_skip"])
    cos, sin = _rope_tables(seq)

    h = _rms_fwd(x, wts["norm_mix"], "rms_mix_fwd")
    z = _mm(h, w_z, "nt", BF16, "proj_z")
    xbc = _mm(h, w_xbc, "nt", F32, "proj_xbc")
    dt_raw = _mm(h, w_dt, "nt", F32, "proj_dt")
    qkv = _mm(h, w_qkv, "nt", BF16, "proj_qkv")
    gate_logits = _mm(h, w_gate, "nt", BF16, "proj_gate")

    xc = _conv_fwd(xbc, conv_w, conv_b, seq)
    dtr = dt_raw[:, :N_HEADS].reshape(t, N_GROUPS, hg).transpose(1, 0, 2)
    dtrt = dt_raw[:, :N_HEADS].reshape(nb, seq, N_GROUPS, hg).transpose(2, 0, 3, 1)
    y, states, *gathered = _ssd_fwd(xc, dtr, dtrt, bias_r, bias_c, a_r, a_c, dskip_r, nb, seq, overlap.gather_side)
    wts = {**wts, **overlap.late_weights(gathered)}
    yn, y_ssm = _gate_norm_out(y, z, wts["ssm_norm"], wts["w_ssm_out"])

    groups = range(len(ATT_DILATIONS))
    qg, kg, vg = _rope_fwd(qkv, cos, sin, nb, seq)
    o_g, lse_g = zip(*[_att_fwd(qg[i], kg[i], vg[i], i, seq) for i in groups])
    att = _merge_fwd(o_g, lse_g, nb, seq)
    gate_halves = [(gate_logits, d, 0), (gate_logits, d, 1)]
    b_gate_halves = [(wts["b_gate"], d, 0), (wts["b_gate"], d, 1)]
    y_att, mixed = _mm_fused(att, wts["w_att_out_t"], "nt", "att_out_mix", 512, _mix_fwd_epilogue,
                             gate_halves + [(y_ssm, d, 0)], b_gate_halves, [(d, BF16), (d, BF16)])

    def residual_and_norm(xv, rows, fulls):
        return [xv, xv * lax.rsqrt(jnp.mean(xv * xv, axis=-1, keepdims=True) + EPS) * fulls[0][...]]

    x1, h2 = _mm_fused(mixed, wts["w_mix_out"], "nn", "mix_out_norm", 512, residual_and_norm, [],
                       [(wts["norm_ffn"], d, 0)], [(d, F32), (d, BF16)], add=x)
    gt, up, act = _ffn_in(h2, wts["w_ffn_gate_t"], wts["w_ffn_up_t"])

    g = {}
    dx2, dx2_b, g["norm_final"], loss = _mm_fused(
        act, wts["w_ffn_down"], "nn", "ffn_down_loss", 512,
        lambda x2, rows, fulls: _final_values(x2, rows[0][...], fulls[0][...]),
        [(target, d, 0)], [(wts["norm_final"].reshape(1, d), d, 0)], [(d, F32), (d, BF16), (d, F32), (1, F32)],
        n_acc=2, add=x1)
    g["w_ffn_down"] = _mm(act, dx2_b, "tn", BF16, "g_ffn_down")
    dgt, dup = _ffn_bwd_in(dx2_b, wts["w_ffn_down"], gt, up)
    g["w_ffn_gate_t"] = _mm(dgt, h2, "tn", BF16, "g_ffn_gate")
    g["w_ffn_up_t"] = _mm(dup, h2, "tn", BF16, "g_ffn_up")
    dh2 = _mm(dgt, wts["w_ffn_gate_t"], "nn", F32, "d_h2_gate")
    dx1, dx1_b, g["norm_ffn"] = _mm_fused(
        dup, wts["w_ffn_up_t"], "nn", "d_h2_up_norm", 512,
        lambda dh, rows, fulls: _rms_bwd_values(rows[0][...], dh, fulls[0][...], rows[1][...]),
        [(x1, d, 0), (dx2, d, 0)], [(wts["norm_ffn"], d, 0)], [(d, F32), (d, BF16), (d, F32)], n_acc=1, add=dh2)

    g["w_mix_out"] = _mm(mixed, dx1_b, "tn", BF16, "g_mix_out")
    dy_ssm, dy_att, dgate, g["b_gate"] = _mm_fused(
        dx1_b, wts["w_mix_out"], "nt", "d_mixed_gates", 512, _mix_bwd_epilogue,
        gate_halves + [(y_ssm, d, 0), (y_att, d, 0)], b_gate_halves,
        [(d, BF16), (d, BF16), (2 * d, BF16), (2 * d, F32)], n_acc=1)

    datt = _mm(dy_att, wts["w_att_out_t"], "nn", F32, "d_att")
    g["w_att_out_t"] = _mm(dy_att, att, "tn", BF16, "g_att_out")
    do_g, dlt_g = _merge_bwd(o_g, lse_g, datt, nb, seq)
    dq_g, dk_g, dv_g = zip(*[_att_bwd(qg[i], kg[i], vg[i], do_g[i], lse_g[i], dlt_g[i], i, seq) for i in groups])
    dqkv = _rope_bwd(dq_g, dk_g, dv_g, cos, sin, nb, seq)

    g["w_ssm_out"] = _mm(yn, dy_ssm, "tn", BF16, "g_ssm_out")
    dy, dz, g["ssm_norm"] = _mm_fused(
        dy_ssm, wts["w_ssm_out"], "nt", "d_yn_norm", 256, _gate_norm_bwd_epilogue,
        [(y, D_INNER, 0), (z, D_INNER, 0)], [(wts["ssm_norm"], D_INNER, 0)],
        [(D_INNER, BF16), (D_INNER, BF16), (D_INNER, F32)], n_acc=1)
    side = overlap.scatter_side({n: g.pop(n) for n in LATE})
    dxc, ddtr, g_bias, g_alog, g_dskip, *scattered = _ssd_bwd(xc, dtr, dtrt, bias_r, bias_c, a_r, a_c, dskip_r,
                                                               states, dy, nb, seq, side)
    g["dt_bias"] = g_bias.reshape(1, N_HEADS)
    g["a_log"] = g_alog.reshape(1, N_HEADS)
    g["d_skip"] = g_dskip.reshape(1, N_HEADS)
    dpre, g_conv_w, g_conv_b = _conv_bwd_pre(xbc, conv_w, conv_b, dxc, seq)
    g["conv_w"] = _group_major_inv(g_conv_w, 1)
    g["conv_b"] = _group_major_inv(g_conv_b, 1)
    dxbc = _conv_bwd_in(dpre, conv_w, seq)
    ddt = jnp.pad(ddtr.transpose(1, 0, 2).reshape(t, N_HEADS), ((0, 0), (0, DT_PAD - N_HEADS))).astype(BF16)

    g_in_t = jnp.concatenate([
        _mm(dz, h, "tn", BF16, "g_in_z"),
        _group_major_inv(_mm(dxbc, h, "tn", BF16, "g_in_xbc"), 0),
        _mm(ddt, h, "tn", BF16, "g_in_dt")[:N_HEADS],
        _mm(dqkv, h, "tn", BF16, "g_in_qkv"),
        _mm(dgate, h, "tn", BF16, "g_in_gate")], axis=0)
    dh = _mm(dz, w_z, "nn", F32, "d_h_z")
    dh = _mm(dxbc, w_xbc, "nn", F32, "d_h_xbc", add=dh)
    dh = _mm(ddt, w_dt, "nn", F32, "d_h_dt", add=dh)
    dh, *scattered_in = _mm(dqkv, w_qkv, "nn", F32, "d_h_qkv", add=dh, side=overlap.scatter_in({"w_in_t": g_in_t}))
    dx, _, g["norm_mix"] = _mm_fused(
        dgate, w_gate, "nn", "d_h_gate_norm", 512,
        lambda dhv, rows, fulls: _rms_bwd_values(rows[0][...], dhv, fulls[0][...], rows[1][...]),
        [(x, d, 0), (dx1, d, 0)], [(wts["norm_mix"], d, 0)], [(d, F32), (d, BF16), (d, F32)], n_acc=1, add=dh)
    return loss[0, 0], dx.reshape(nb, seq, d), g, scattered, scattered_in


def kernel(x, norm_mix, w_in, b_gate, conv_w, conv_b, dt_bias, a_log, d_skip, ssm_norm, w_ssm_out, w_att_out, w_mix_out, norm_ffn, w_ffn_gate, w_ffn_up, w_ffn_down, norm_final, loss_target, m_norm_mix, m_w_in, m_b_gate, m_conv_w, m_conv_b, m_dt_bias, m_a_log, m_d_skip, m_ssm_norm, m_w_ssm_out, m_w_att_out, m_w_mix_out, m_norm_ffn, m_w_ffn_gate, m_w_ffn_up, m_w_ffn_down, m_norm_final, v_norm_mix, v_w_in, v_b_gate, v_conv_w, v_conv_b, v_dt_bias, v_a_log, v_d_skip, v_ssm_norm, v_w_ssm_out, v_w_att_out, v_w_mix_out, v_norm_ffn, v_w_ffn_gate, v_w_ffn_up, v_w_ffn_down, v_norm_final):
    names = ("norm_mix", "w_in", "b_gate", "conv_w", "conv_b", "dt_bias", "a_log", "d_skip", "ssm_norm", "w_ssm_out",
             "w_att_out", "w_mix_out", "norm_ffn", "w_ffn_gate", "w_ffn_up", "w_ffn_down", "norm_final")
    w_loc = dict(zip(names, (norm_mix, w_in, b_gate, conv_w, conv_b, dt_bias, a_log, d_skip, ssm_norm, w_ssm_out,
                             w_att_out, w_mix_out, norm_ffn, w_ffn_gate, w_ffn_up, w_ffn_down, norm_final)))
    m_loc = dict(zip(names, (m_norm_mix, m_w_in, m_b_gate, m_conv_w, m_conv_b, m_dt_bias, m_a_log, m_d_skip,
                             m_ssm_norm, m_w_ssm_out, m_w_att_out, m_w_mix_out, m_norm_ffn, m_w_ffn_gate,
                             m_w_ffn_up, m_w_ffn_down, m_norm_final)))
    v_loc = dict(zip(names, (v_norm_mix, v_w_in, v_b_gate, v_conv_w, v_conv_b, v_dt_bias, v_a_log, v_d_skip,
                             v_ssm_norm, v_w_ssm_out, v_w_att_out, v_w_mix_out, v_norm_ffn, v_w_ffn_gate,
                             v_w_ffn_up, v_w_ffn_down, v_norm_final)))
    two_d = lambda a: a.reshape(a.shape[-2:]) if a.ndim >= 2 else a.reshape(1, -1)
    w2 = {n: two_d(a) for n, a in w_loc.items()}
    chip = 2 * lax.axis_index("x") + lax.axis_index("y")
    c = lax.axis_index("c")

    wire_shapes = {n: _wire_shard(w2[n], n).shape for n in SHARDED}
    true_rows = {n: wire_shapes[n][0] * wire_shapes[n][1] // PACK_W for n in SHARDED}
    seg_rows = {n: _rows(wire_shapes[n][0] * wire_shapes[n][1]) for n in SHARDED}
    buckets = {"first": ("w_in",), "late": tuple(n for n in SHARDED if n != "w_in")}
    rows_of = {b: _padded_rows(sum(seg_rows[n] for n in ns)) for b, ns in buckets.items()}

    def pack_shards(b):
        packed = _pack_rows([_wire_shard(w2[n], n).astype(BF16) for n in buckets[b]], rows_of[b])
        return packed.reshape(2, rows_of[b] // 2, PACK_W)

    def unpack_full(gathered, b):
        wg, out, off = gathered.reshape(N_CHIPS, rows_of[b], PACK_W), {}, 0
        for n in buckets[b]:
            rows, cols = wire_shapes[n]
            out[_wire_name(n)] = wg[:, off:off + true_rows[n]].reshape(N_CHIPS * rows, cols)
            off += seg_rows[n]
        return out

    def pack_grads(g, b):
        sections = [_pack_rows([g[_wire_name(n)].reshape(N_CHIPS, true_rows[n], PACK_W)[k] for n in buckets[b]],
                               rows_of[b]) for k in range(N_CHIPS)]
        return jnp.stack(sections).reshape(N_CHIPS, 2, rows_of[b] // 2, PACK_W)

    def chip_sums(g, b):
        g2 = pack_grads(g, b)
        return _add_own_half(g2, _swap_halves(g2, b), c, b)

    def finish(by_source, b):
        reduced = _join_halves(_sum_chips(by_source, b), b).reshape(rows_of[b], PACK_W)
        out, off = {}, 0
        for n in buckets[b]:
            out[n] = reduced[off:off + true_rows[n]].reshape(wire_shapes[n])
            off += seg_rows[n]
        return out

    full = {"w_in_t": _gather_weights(pack_shards("first")).reshape(N_CHIPS, rows_of["first"], PACK_W)}
    for n in SMALL:
        full[n] = w2[n]
    overlap = _Overlap(_gather_side(pack_shards("late")), lambda outs: unpack_full(outs[0], "late"),
                       lambda g: _scatter_side(chip_sums(g, "late")), lambda g: _scatter_side(chip_sums(g, "first")))

    n_conv = w2["conv_w"].shape[1]
    placed = lax.dynamic_update_slice_in_dim(jnp.zeros((CONV_K, N_CHIPS * n_conv), F32), w2["conv_w"], chip * n_conv, 1)
    placed = jnp.where(c == 0, placed, 0.0)
    full["conv_w"] = _allreduce_small(_pack_rows([placed], _rows(int(placed.size))), "gather_conv_w").reshape(
        -1)[:placed.size].reshape(placed.shape)

    loss_sum, grad_x, g_full, scattered, scattered_in = _local_step(x, loss_target, full, overlap)
    loss = lax.psum(loss_sum, ("x", "y", "c"))

    g_shard = {}
    small_names = SMALL + ("conv_w",)
    small_flat = jnp.concatenate([g_full[n].reshape(-1) for n in small_names])
    small = _allreduce_small(_pack_rows([small_flat], _rows(int(small_flat.size))), "allreduce_small").reshape(-1)
    off = 0
    for n in small_names:
        size = int(g_full[n].size)
        g_shard[n] = small[off:off + size].reshape(g_full[n].shape)
        off += size
    g_shard["conv_w"] = lax.dynamic_slice_in_dim(g_shard["conv_w"], chip * n_conv, n_conv, 1)

    g_shard.update(finish(scattered[0], "late"))
    g_shard.update(finish(scattered_in[0], "first"))

    grads, deltas, new_m, new_v = [], [], [], []
    for n in names:
        shape = w_loc[n].shape
        if n in COL_SHARDED:
            view = unview = lambda a: jnp.swapaxes(a, -1, -2)
        else:
            view, unview = ((lambda a: a) if len(shape) >= 2 else two_d), (lambda a: a.reshape(shape))
        gn = g_shard[n].reshape(view(w_loc[n]).shape)
        outs = _adamw(view(w_loc[n]), gn, view(m_loc[n]), view(v_loc[n]), "adamw_" + n)
        for acc, a in zip((grads, deltas, new_m, new_v), (gn, *outs)):
            acc.append(unview(a))
    return (loss, grad_x, *grads, *deltas, *new_m, *new_v)
```

```python
import functools
from typing import Callable, NamedTuple, Optional

import jax
import jax.numpy as jnp
from jax import lax
from jax.experimental import pallas as pl
from jax.experimental.pallas import tpu as pltpu

F32 = jnp.float32
BF16 = jnp.bfloat16
SDS = jax.ShapeDtypeStruct
MESH = pl.DeviceIdType.MESH

D_MODEL = 1024
D_INNER = 2048
N_HEADS = 32
HEAD_P = 64
N_GROUPS = 4
HEADS_PER_GROUP = N_HEADS // N_GROUPS
D_STATE = 128
CONV_K = 4
CHUNK = 128
CONV_DIM = D_INNER + 2 * N_GROUPS * D_STATE
GROUP_W = D_INNER // N_GROUPS + 2 * D_STATE
ATT_HEADS = 12
ATT_D = 128
ATT_SLOTS = 4
ATT_W = ATT_SLOTS * ATT_D
ATT_DILATIONS = (1, 4, 16)
ATT_BLOCK = 128
QKV_DIM = 3 * ATT_HEADS * ATT_D
D_FF = 2816
DT_PAD = 128
ROPE_THETA = 10000.0
EPS = 1e-6
N_CHIPS = 4
LANES = 128

ADAM_LR = 0.001
ADAM_B1 = 0.9
ADAM_B2 = 0.999
ADAM_EPS = 1e-08
ADAM_WD = 0.01
ADAM_STEP = 10

VMEM_LIMIT = 48 * 1024 * 1024


def _cparams(semantics):
    return pltpu.CompilerParams(dimension_semantics=semantics, vmem_limit_bytes=VMEM_LIMIT)


def _pick(n, cap):
    best = None
    for t in range(LANES, min(n, cap) + 1, LANES):
        if n % t == 0:
            best = t
    return best or n


def _row_tile(rows, cap):
    best = None
    for t in range(8, min(rows, cap) + 1, 8):
        if rows % t == 0:
            best = t
    return best or rows


def _sigmoid(x):
    return pl.reciprocal(1.0 + jnp.exp(-x), approx=True)


def _softplus(x):
    return jnp.maximum(x, 0.0) + jnp.log(1.0 + jnp.exp(-jnp.abs(x)))


def _dot(a, b):
    return jnp.dot(a, b, preferred_element_type=F32)


def _dot_nt(a, b):
    return lax.dot_general(a, b, (((1,), (1,)), ((), ())), preferred_element_type=F32)


def _dot_tn(a, b):
    return lax.dot_general(a, b, (((0,), (0,)), ((), ())), preferred_element_type=F32)


def _mm(a, b, mode, out_dtype, name, add=None, side=None):
    if mode == "nn":
        (m, k), (_, n) = a.shape, b.shape
    elif mode == "nt":
        (m, k), (n, _) = a.shape, b.shape
    else:
        (k, m), (_, n) = a.shape, b.shape
    tm, tn = _pick(m, 1536), _pick(n, 2048)
    tk = k if k <= 2048 else _pick(k, 2048)
    nk = k // tk
    dims = {"nn": ((1,), (0,)), "nt": ((1,), (1,)), "tn": ((0,), (0,))}[mode]

    def partial_product(a_ref, b_ref):
        return lax.dot_general(a_ref[...].astype(BF16), b_ref[...].astype(BF16), (dims, ((), ())),
                               preferred_element_type=F32)

    def body(*refs):
        a_ref, b_ref = refs[:2]
        c_ref = refs[2] if add is not None else None
        o_ref = refs[3] if add is not None else refs[2]

        def finish(r):
            if add is not None:
                r = r + c_ref[...].astype(F32)
            o_ref[...] = r.astype(out_dtype)

        if nk == 1:
            finish(partial_product(a_ref, b_ref))
            return
        acc = refs[-1]
        kk = pl.program_id(2)

        @pl.when(kk == 0)
        def _():
            acc[...] = partial_product(a_ref, b_ref)

        @pl.when((kk > 0) & (kk < nk - 1))
        def _():
            acc[...] += partial_product(a_ref, b_ref)

        @pl.when(kk == nk - 1)
        def _():
            finish(acc[...] + partial_product(a_ref, b_ref))

    a_spec = {"nn": pl.BlockSpec((tm, tk), lambda j, i, q: (i, q)),
              "nt": pl.BlockSpec((tm, tk), lambda j, i, q: (i, q)),
              "tn": pl.BlockSpec((tk, tm), lambda j, i, q: (q, i))}[mode]
    b_spec = {"nn": pl.BlockSpec((tk, tn), lambda j, i, q: (q, j)),
              "nt": pl.BlockSpec((tn, tk), lambda j, i, q: (j, q)),
              "tn": pl.BlockSpec((tk, tn), lambda j, i, q: (q, j))}[mode]
    o_spec = pl.BlockSpec((tm, tn), lambda j, i, q: (i, j))
    ins, specs = [a, b], [a_spec, b_spec]
    if add is not None:
        ins.append(add)
        specs.append(o_spec)
    acc = [pltpu.VMEM((tm, tn), F32)] if nk > 1 else []
    grid = (n // tn, m // tm, nk)
    if side is None:
        return pl.pallas_call(
            body, name=name, grid=grid, in_specs=specs, out_specs=o_spec, out_shape=SDS((m, n), out_dtype),
            scratch_shapes=acc, compiler_params=_cparams(("parallel", "parallel", "arbitrary")))(*ins)
    return pl.pallas_call(
        _attach_side(body, len(ins), 1, side, grid), name=name, grid=grid,
        in_specs=specs + [ANY] * len(side.ins), out_specs=[o_spec] + [ANY] * len(side.out_shapes),
        out_shape=[SDS((m, n), out_dtype)] + list(side.out_shapes), scratch_shapes=acc + list(side.scratch),
        compiler_params=_cparams(("arbitrary", "arbitrary", "arbitrary")))(*ins, *side.ins)


def _mm_fused(a, b, mode, name, tm, epilogue, row_ins, full_ins, outs, n_acc=0, add=None):
    (m, k), n = a.shape, (b.shape[1] if mode == "nn" else b.shape[0])
    tk = k if k <= 2048 else _pick(k, 2048)
    nk = k // tk
    dims = {"nn": ((1,), (0,)), "nt": ((1,), (1,))}[mode]
    n_row, n_full, n_out = len(row_ins), len(full_ins), len(outs)

    def partial_product(a_ref, b_ref):
        return lax.dot_general(a_ref[...], b_ref[...], (dims, ((), ())), preferred_element_type=F32)

    def body(*refs):
        a_ref, b_ref = refs[:2]
        pos = 3 if add is not None else 2
        row_refs, full_refs = refs[pos:pos + n_row], refs[pos + n_row:pos + n_row + n_full]
        out_refs = refs[pos + n_row + n_full:pos + n_row + n_full + n_out]
        i, kk = pl.program_id(0), pl.program_id(1)

        def finish(r):
            if add is not None:
                r = r + refs[2][...].astype(F32)
            for q, (o_ref, v) in enumerate(zip(out_refs, epilogue(r, row_refs, full_refs))):
                if q < n_out - n_acc:
                    o_ref[...] = v.astype(o_ref.dtype)
                else:
                    @pl.when(i == 0)
                    def _(o_ref=o_ref, v=v):
                        o_ref[...] = v

                    @pl.when(i > 0)
                    def _(o_ref=o_ref, v=v):
                        o_ref[...] += v

        if nk == 1:
            finish(partial_product(a_ref, b_ref))
            return
        acc = refs[-1]

        @pl.when(kk == 0)
        def _():
            acc[...] = partial_product(a_ref, b_ref)

        @pl.when((kk > 0) & (kk < nk - 1))
        def _():
            acc[...] += partial_product(a_ref, b_ref)

        @pl.when(kk == nk - 1)
        def _():
            finish(acc[...] + partial_product(a_ref, b_ref))

    tile = lambda w, cb: pl.BlockSpec((tm, w), lambda i, q: (i, cb))
    b_spec = (pl.BlockSpec((tk, n), lambda i, q: (q, 0)) if mode == "nn" else pl.BlockSpec((n, tk), lambda i, q: (0, q)))
    specs = [pl.BlockSpec((tm, tk), lambda i, q: (i, q)), b_spec] + ([tile(n, 0)] if add is not None else [])
    specs += [tile(w, cb) for _, w, cb in row_ins]
    vec = lambda w, cb: pl.BlockSpec((1, w), lambda i, q: (0, cb))
    specs += [vec(w, cb) for _, w, cb in full_ins]
    out_specs = [tile(w, 0) for w, _ in outs[:n_out - n_acc]] + [vec(w, 0) for w, _ in outs[n_out - n_acc:]]
    out_shape = [SDS((m, w), dt) for w, dt in outs[:n_out - n_acc]] + [SDS((1, w), F32) for w, _ in outs[n_out - n_acc:]]
    ins = [a, b] + ([add] if add is not None else []) + [x for x, _, _ in row_ins] + [x for x, _, _ in full_ins]
    return pl.pallas_call(
        body, name=name, grid=(m // tm, nk), in_specs=specs, out_specs=out_specs, out_shape=out_shape,
        scratch_shapes=[pltpu.VMEM((tm, n), F32)] if nk > 1 else [],
        compiler_params=_cparams(("arbitrary", "arbitrary")))(*ins)


def _rw(name, fn, nsteps, ins, outs, n_acc=0):
    n_in, n_out = len(ins), len(outs)

    def body(*refs):
        i = pl.program_id(0)
        vals = fn(i, *refs[:n_in])
        for q, (r, v) in enumerate(zip(refs[n_in:], vals)):
            if q < n_out - n_acc:
                r[...] = v.astype(r.dtype)
            else:
                @pl.when(i == 0)
                def _(r=r):
                    r[...] = jnp.zeros_like(r)

                r[...] += v

    return pl.pallas_call(
        body, name=name, grid=(nsteps,), in_specs=[s for _, s in ins], out_specs=[s for _, s in outs],
        out_shape=[o for o, _ in outs], compiler_params=_cparams(("arbitrary",)))(*[a for a, _ in ins])


def _rs(tm, w, cb=0):
    return pl.BlockSpec((tm, w), lambda i: (i, cb))


def _fs(shape):
    nd = len(shape)
    return pl.BlockSpec(shape, lambda i: (0,) * nd)


def _colsum(v):
    return jnp.sum(v, axis=0, keepdims=True)


def _rms_fwd(x, g, name):
    t, d = x.shape
    tm = 512

    def fn(i, x_ref, g_ref):
        xv = x_ref[...]
        r = lax.rsqrt(jnp.mean(xv * xv, axis=-1, keepdims=True) + EPS)
        return [xv * r * g_ref[...]]

    return _rw(name, fn, t // tm, [(x, _rs(tm, d)), (g, _fs((1, d)))], [(SDS((t, d), BF16), _rs(tm, d))])[0]


def _rms_bwd_values(xv, dhv, gv, dres):
    r = lax.rsqrt(jnp.mean(xv * xv, axis=-1, keepdims=True) + EPS)
    xhat = xv * r
    dxhat = dhv * gv
    dx = dres + r * (dxhat - xhat * jnp.mean(dxhat * xhat, axis=-1, keepdims=True))
    return [dx, dx, _colsum(dhv * xhat)]


def _final_values(xv, target, gv):
    d = xv.shape[-1]
    r = lax.rsqrt(jnp.mean(xv * xv, axis=-1, keepdims=True) + EPS)
    xhat = xv * r
    diff = xhat * gv - target
    lsum = 0.5 * jnp.sum(jnp.sum(diff * diff, axis=-1, keepdims=True) * (1.0 / d), axis=0, keepdims=True)
    dy = diff * (1.0 / d)
    dxhat = dy * gv
    dx = r * (dxhat - xhat * jnp.mean(dxhat * xhat, axis=-1, keepdims=True))
    return [dx, dx, _colsum(dy * xhat), lsum]


CONV_TS = 512
CONV_HALO = 8


def _conv_specs(seq, c):
    ts, tc = CONV_TS, GROUP_W
    hb = ts // CONV_HALO
    u_spec = pl.BlockSpec((ts, tc), lambda j, i: (i, j))
    prev_spec = pl.BlockSpec((CONV_HALO, tc), lambda j, i: (jnp.maximum(i * hb - 1, 0), j))
    w_spec = pl.BlockSpec((CONV_K, tc), lambda j, i: (0, j))
    b_spec = pl.BlockSpec((1, tc), lambda j, i: (0, j))
    return u_spec, prev_spec, w_spec, b_spec


CONV_PIECE = 32


def _conv_fill(i, seq, u_ref, prev_ref, ext):
    first = (i % (seq // CONV_TS)) == 0
    ext[0:CONV_HALO, :] = jnp.where(first, 0.0, prev_ref[...])
    ext[CONV_HALO:, :] = u_ref[...]


def _conv_piece(ext, r0, wv, bv):
    lo = r0 + CONV_HALO - CONV_K + 1
    taps = [ext[lo + q:lo + q + CONV_PIECE, :] for q in range(CONV_K)]
    pre = bv
    for q, tap in enumerate(taps):
        pre = pre + wv[q:q + 1] * tap
    return taps, pre


def _conv_fwd(u, w, b, seq):
    t, c = u.shape
    ts, tc = CONV_TS, GROUP_W
    u_spec, prev_spec, w_spec, b_spec = _conv_specs(seq, c)

    def body(u_ref, prev_ref, w_ref, b_ref, o_ref, ext):
        _conv_fill(pl.program_id(1), seq, u_ref, prev_ref, ext)
        wv, bv = w_ref[...], b_ref[...]
        for r0 in range(0, ts, CONV_PIECE):
            _, pre = _conv_piece(ext, r0, wv, bv)
            o_ref[r0:r0 + CONV_PIECE, :] = pre * _sigmoid(pre)

    return pl.pallas_call(
        body, name="conv_fwd", grid=(c // tc, t // ts), in_specs=[u_spec, prev_spec, w_spec, b_spec],
        out_specs=u_spec, out_shape=SDS((t, c), F32), scratch_shapes=[pltpu.VMEM((ts + CONV_HALO, tc), F32)],
        compiler_params=_cparams(("parallel", "arbitrary")))(u, u, w, b)


def _conv_bwd_pre(u, w, b, dxc, seq):
    t, c = u.shape
    ts, tc = CONV_TS, GROUP_W
    u_spec, prev_spec, w_spec, b_spec = _conv_specs(seq, c)

    def body(u_ref, prev_ref, w_ref, b_ref, d_ref, dpre_ref, dw_ref, db_ref, ext):
        i = pl.program_id(1)
        _conv_fill(i, seq, u_ref, prev_ref, ext)
        wv, bv = w_ref[...], b_ref[...]
        fold = lambda v: sum(v[8 * s:8 * (s + 1)] for s in range(CONV_PIECE // 8))
        sums = [jnp.zeros((8, tc), F32)] * (CONV_K + 1)
        for r0 in range(0, ts, CONV_PIECE):
            taps, pre = _conv_piece(ext, r0, wv, bv)
            sg = _sigmoid(pre)
            dpre = d_ref[r0:r0 + CONV_PIECE, :] * sg * (1.0 + pre * (1.0 - sg))
            dpre_ref[r0:r0 + CONV_PIECE, :] = dpre
            sums = [s + fold(dpre * f) for s, f in zip(sums, taps + [1.0])]

        @pl.when(i == 0)
        def _():
            dw_ref[...] = jnp.zeros_like(dw_ref)
            db_ref[...] = jnp.zeros_like(db_ref)

        db_ref[...] += _colsum(sums[CONV_K])
        for q in range(CONV_K):
            dw_ref[q:q + 1, :] += _colsum(sums[q])

    return pl.pallas_call(
        body, name="conv_bwd_pre", grid=(c // tc, t // ts),
        in_specs=[u_spec, prev_spec, w_spec, b_spec, u_spec], out_specs=[u_spec, w_spec, b_spec],
        out_shape=[SDS((t, c), F32), SDS((CONV_K, c), F32), SDS((1, c), F32)],
        scratch_shapes=[pltpu.VMEM((ts + CONV_HALO, tc), F32)],
        compiler_params=_cparams(("parallel", "arbitrary")))(u, u, w, b, dxc)


def _conv_bwd_in(dpre, w, seq):
    t, c = dpre.shape
    ts, tc = CONV_TS, GROUP_W
    hb = ts // CONV_HALO
    last = t // CONV_HALO - 1
    d_spec = pl.BlockSpec((ts, tc), lambda j, i: (i, j))
    next_spec = pl.BlockSpec((CONV_HALO, tc), lambda j, i: (jnp.minimum((i + 1) * hb, last), j))
    w_spec = pl.BlockSpec((CONV_K, tc), lambda j, i: (0, j))

    def body(d_ref, next_ref, w_ref, o_ref, ext):
        i = pl.program_id(1)
        nts = seq // ts
        is_last = (i % nts) == nts - 1
        ext[0:ts, :] = d_ref[...]
        ext[ts:, :] = jnp.where(is_last, 0.0, next_ref[...])
        wv = w_ref[...]
        for r0 in range(0, ts, CONV_PIECE):
            acc = wv[CONV_K - 1:CONV_K] * ext[r0:r0 + CONV_PIECE, :]
            for q in range(CONV_K - 1):
                lo = r0 + CONV_K - 1 - q
                acc = acc + wv[q:q + 1] * ext[lo:lo + CONV_PIECE, :]
            o_ref[r0:r0 + CONV_PIECE, :] = acc.astype(o_ref.dtype)

    return pl.pallas_call(
        body, name="conv_bwd_in", grid=(c // tc, t // ts), in_specs=[d_spec, next_spec, w_spec],
        out_specs=d_spec, out_shape=SDS((t, c), BF16), scratch_shapes=[pltpu.VMEM((ts + CONV_HALO, tc), F32)],
        compiler_params=_cparams(("parallel", "arbitrary")))(dpre, dpre, w)


def _split3(v):
    hi = v.astype(BF16)
    r1 = v - hi.astype(F32)
    mid = r1.astype(BF16)
    lo = (r1 - mid.astype(F32)).astype(BF16)
    return hi, mid, lo


def _ssd_prelude(dtr_ref, dtrt_ref, bias_ref, biast_ref, a_ref, at_ref):
    dt = _softplus(dtr_ref[...] + bias_ref[...])
    dtt = _softplus(dtrt_ref[...] + biast_ref[...])
    ri = lax.broadcasted_iota(jnp.int32, (CHUNK, CHUNK), 0)
    ci = lax.broadcasted_iota(jnp.int32, (CHUNK, CHUNK), 1)
    lower = ri >= ci
    upper = ri <= ci
    lower_b = jnp.where(lower, 1.0, 0.0).astype(BF16)
    upper_b = jnp.where(upper, 1.0, 0.0).astype(BF16)
    acs = sum(_dot(lower_b, p) for p in _split3(dt * a_ref[...]))
    acst = sum(_dot(p, upper_b) for p in _split3(dtt * at_ref[...]))
    return dt, acs, acst, lower, upper, lower_b, upper_b


SSD_FWD_GPS = 2
SSD_BWD_GPS = 1


def _ssd_specs(seq, gps):
    nc = seq // CHUNK
    hg = HEADS_PER_GROUP
    fwd = lambda c: c
    rev = lambda c: nc - 1 - c

    def specs(cc):
        return dict(
            xc=pl.BlockSpec((CHUNK, gps * GROUP_W), lambda g, b, c: (b * nc + cc(c), g)),
            y=pl.BlockSpec((CHUNK, gps * hg * HEAD_P), lambda g, b, c: (b * nc + cc(c), g)),
            dtr=pl.BlockSpec((gps, CHUNK, hg), lambda g, b, c: (g, b * nc + cc(c), 0)),
            dtrt=pl.BlockSpec((gps, None, hg, CHUNK), lambda g, b, c: (g, b, 0, cc(c))),
            prow=pl.BlockSpec((gps, 1, hg), lambda g, b, c: (g, 0, 0)),
            pcol=pl.BlockSpec((gps, hg, 1), lambda g, b, c: (g, 0, 0)),
            st=pl.BlockSpec((gps, None, None, D_STATE, hg * HEAD_P), lambda g, b, c: (g, b, cc(c), 0, 0)),
        )

    return specs(fwd), specs(rev)


def _group_views(refs, lane_widths, gi):
    return [r.at[:, gi * w:(gi + 1) * w] if w else r.at[gi] for r, w in zip(refs, lane_widths)]


def _head_maps():
    hw = HEADS_PER_GROUP * HEAD_P
    shift = HEAD_P.bit_length() - 1
    hj = lax.broadcasted_iota(jnp.int32, (HEADS_PER_GROUP, hw), 0)
    lq = jnp.right_shift(lax.broadcasted_iota(jnp.int32, (HEADS_PER_GROUP, hw), 1), shift)
    spread = jnp.where(hj == lq, 1.0, 0.0).astype(BF16)
    rq = jnp.right_shift(lax.broadcasted_iota(jnp.int32, (hw, LANES), 0), shift)
    cj = lax.broadcasted_iota(jnp.int32, (hw, LANES), 1)
    gather = jnp.where(rq == cj, 1.0, 0.0).astype(BF16)
    return spread, gather


def _dot01(v, m01):
    hi, mid, _ = _split3(v)
    return _dot(hi, m01) + _dot(mid, m01)


class _Side(NamedTuple):
    ins: tuple
    out_shapes: tuple
    scratch: tuple
    first: Callable
    mid: Optional[Callable]
    last: Callable


NO_SIDE = _Side((), (), (), lambda *refs: None, None, lambda *refs: None)


def _attach_side(body, n_in, n_out, side, grid):
    si, so, ss = len(side.ins), len(side.out_shapes), len(side.scratch)

    def wrapped(*refs):
        ins, s_in = refs[:n_in], refs[n_in:n_in + si]
        outs = refs[n_in + si:n_in + si + n_out]
        s_out = refs[n_in + si + n_out:n_in + si + n_out + so]
        rest = refs[n_in + si + n_out + so:]
        scr, s_scr = rest[:len(rest) - ss], rest[len(rest) - ss:]
        ids = [pl.program_id(a) for a in range(len(grid))]
        inner_first = functools.reduce(lambda p, q: p & q, [i == 0 for i in ids[1:]], ids[0] >= 0)
        at_last = functools.reduce(lambda p, q: p & q, [i == n - 1 for i, n in zip(ids, grid)])

        @pl.when((ids[0] == 0) & inner_first)
        def _():
            side.first(s_in, s_out, s_scr)

        if side.mid is not None:
            outer_last = functools.reduce(lambda p, q: p & q, [i == n - 1 for i, n in zip(ids[:-1], grid[:-1])])

            @pl.when(outer_last & (ids[-1] == 0))
            def _():
                side.mid(s_in, s_out, s_scr)

        body(*ins, *outs, *scr)

        @pl.when(at_last)
        def _():
            side.last(s_in, s_out, s_scr)

    return wrapped


def _ssd_fwd(xc, dtr, dtrt, bias, biast, a, at, dskip, nb, seq, side):
    t = xc.shape[0]
    nc = seq // CHUNK
    hg = HEADS_PER_GROUP
    hw = hg * HEAD_P
    gps = SSD_FWD_GPS
    grid = (N_GROUPS // gps, nb, nc)
    sp, _ = _ssd_specs(seq, gps)

    def body(*refs):
        for gi in range(gps):
            one_group(*_group_views(refs, (GROUP_W, 0, 0, 0, 0, 0, 0, 0, hw, 0, 0), gi))

    def one_group(xc_ref, dtr_ref, dtrt_ref, bias_ref, biast_ref, a_ref, at_ref, d_ref, y_ref, sin_ref, st):
        @pl.when(pl.program_id(2) == 0)
        def _():
            st[...] = jnp.zeros_like(st)

        s_in = st[...]
        sin_ref[...] = s_in
        dt, acs, acst, lower, _, _, _ = _ssd_prelude(dtr_ref, dtrt_ref, bias_ref, biast_ref, a_ref, at_ref)
        spread, _ = _head_maps()
        x = xc_ref[...]
        xs = x[:, :hw]
        b16 = x[:, hw:hw + D_STATE].astype(BF16)
        c16 = x[:, hw + D_STATE:].astype(BF16)
        cb = _dot_nt(c16, b16)
        last = acs[CHUNK - 1:CHUNK, :]
        e_x = _dot01(jnp.exp(acs), spread)
        dec_x = _dot01(jnp.exp(last - acs), spread)
        tot_x = e_x[CHUNK - 1:CHUNK, :]
        d_x = _dot01(jnp.broadcast_to(d_ref[...], (8, hg)), spread)[0:1, :]
        xdtf = xs * _dot01(dt, spread)
        xdt16 = xdtf.astype(BF16)
        yoff = e_x * _dot(c16, s_in.astype(BF16))
        st[...] = tot_x * s_in + _dot_tn(b16, (dec_x * xdtf).astype(BF16))
        parts = []
        for j in range(hg):
            decay = jnp.exp(jnp.where(lower, acs[:, j:j + 1] - acst[j:j + 1, :], -jnp.inf))
            parts.append(_dot((cb * decay).astype(BF16), xdt16[:, HEAD_P * j:HEAD_P * (j + 1)]))
        y_ref[...] = (jnp.concatenate(parts, axis=-1) + yoff + d_x * xs).astype(y_ref.dtype)

    return pl.pallas_call(
        _attach_side(body, 8, 2, side, grid), name="ssd_fwd", grid=grid,
        in_specs=[sp["xc"], sp["dtr"], sp["dtrt"], sp["prow"], sp["pcol"], sp["prow"], sp["pcol"], sp["prow"]]
        + [ANY] * len(side.ins),
        out_specs=[sp["y"], sp["st"]] + [ANY] * len(side.out_shapes),
        out_shape=[SDS((t, D_INNER), BF16), SDS((N_GROUPS, nb, nc, D_STATE, hw), F32)] + list(side.out_shapes),
        scratch_shapes=[pltpu.VMEM((gps, D_STATE, hw), F32)] + list(side.scratch),
        compiler_params=_cparams(("arbitrary", "arbitrary", "arbitrary")))(
            xc, dtr, dtrt, bias, biast, a, at, dskip, *side.ins)


def _ssd_bwd(xc, dtr, dtrt, bias, biast, a, at, dskip, states, dy, nb, seq, side):
    t = xc.shape[0]
    nc = seq // CHUNK
    hg = HEADS_PER_GROUP
    hw = hg * HEAD_P
    gps = SSD_BWD_GPS
    grid = (N_GROUPS // gps, nb, nc)
    _, sp = _ssd_specs(seq, gps)

    def body(*refs):
        for gi in range(gps):
            one_group(*_group_views(refs, (GROUP_W, 0, 0, 0, 0, 0, 0, 0, 0, hw, GROUP_W, 0, 0, 0, 0, 0), gi))

    def one_group(xc_ref, dtr_ref, dtrt_ref, bias_ref, biast_ref, a_ref, at_ref, d_ref, sin_ref, dy_ref,
                  dxc_ref, ddtr_ref, gbias_ref, ga_ref, gd_ref, ds):
        first = (pl.program_id(1) == 0) & (pl.program_id(2) == 0)

        @pl.when(pl.program_id(2) == 0)
        def _():
            ds[...] = jnp.zeros_like(ds)

        @pl.when(first)
        def _():
            gbias_ref[...] = jnp.zeros_like(gbias_ref)
            ga_ref[...] = jnp.zeros_like(ga_ref)
            gd_ref[...] = jnp.zeros_like(gd_ref)

        dt, acs, acst, lower, upper, _, upper_b = _ssd_prelude(dtr_ref, dtrt_ref, bias_ref, biast_ref, a_ref, at_ref)
        spread, gather = _head_maps()
        x = xc_ref[...]
        dy = dy_ref[...].astype(F32)
        xs = x[:, :hw]
        b16 = x[:, hw:hw + D_STATE].astype(BF16)
        c16 = x[:, hw + D_STATE:].astype(BF16)
        dy16 = dy.astype(BF16)
        cb = _dot_nt(c16, b16)
        cbt = _dot_nt(b16, c16)
        last = acs[CHUNK - 1:CHUNK, :]
        e8 = jnp.exp(acs)
        dec8 = jnp.exp(last - acs)
        e_x = _dot01(e8, spread)
        dec_x = _dot01(dec8, spread)
        tot_x = e_x[CHUNK - 1:CHUNK, :]
        dt_x = _dot01(dt, spread)
        d_x = _dot01(jnp.broadcast_to(d_ref[...], (8, hg)), spread)[0:1, :]
        xdtf = xs * dt_x
        xdt16 = xdtf.astype(BF16)
        s_in = sin_ref[...]
        s16 = s_in.astype(BF16)
        ds_out = ds[...]
        ds16 = ds_out.astype(BF16)
        bds = _dot(b16, ds16)
        cs = _dot(c16, s16)
        edy16 = (e_x * dy).astype(BF16)
        ds[...] = tot_x * ds_out + _dot_tn(c16, edy16)
        lane8 = lax.broadcasted_iota(jnp.int32, (CHUNK, hg), 1)
        row8 = lax.broadcasted_iota(jnp.int32, (CHUNK, hg), 0)
        dacs8 = jnp.zeros((CHUNK, hg), F32)
        acc_m = jnp.zeros((CHUNK, CHUNK), F32)
        acc_mt = jnp.zeros((CHUNK, CHUNK), F32)
        dx_parts = []
        for j in range(hg):
            sl = slice(HEAD_P * j, HEAD_P * (j + 1))
            col = acs[:, j:j + 1]
            row = acst[j:j + 1, :]
            decay = jnp.exp(jnp.where(lower, col - row, -jnp.inf))
            decayt = jnp.exp(jnp.where(upper, row - col, -jnp.inf))
            wm = _dot_nt(dy16[:, sl], xdt16[:, sl]) * decay
            wmt = _dot_nt(xdt16[:, sl], dy16[:, sl]) * decayt
            acc_m = acc_m + wm
            acc_mt = acc_mt + wmt
            dacs8 = dacs8 + jnp.where(lane8 == j, jnp.sum(wm * cb, axis=-1, keepdims=True)
                                      - jnp.sum(wmt * cbt, axis=-1, keepdims=True), 0.0)
            dx_parts.append(_dot((cbt * decayt).astype(BF16), dy16[:, sl]))
        dx = jnp.concatenate(dx_parts, axis=-1) + dec_x * bds
        dxc_ref[:, :hw] = dx * dt_x + d_x * dy
        dxc_ref[:, hw:hw + D_STATE] = _dot(acc_mt.astype(BF16), c16) + _dot_nt((dec_x * xdtf).astype(BF16), ds16)
        dxc_ref[:, hw + D_STATE:] = _dot(acc_m.astype(BF16), b16) + _dot_nt(edy16, s16)
        dtot_rows = jnp.broadcast_to(_colsum(ds_out * s_in), (8, hw))
        sums = _dot01(jnp.concatenate([dy * cs, xdtf * bds, dx * xs, dy * xs, dtot_rows], axis=0), gather)
        de8 = sums[0:CHUNK, :hg]
        ddec8 = sums[CHUNK:2 * CHUNK, :hg]
        ddtx8 = sums[2 * CHUNK:3 * CHUNK, :hg]
        gd8 = _colsum(sums[3 * CHUNK:4 * CHUNK, :hg])
        dtot8 = sums[4 * CHUNK:4 * CHUNK + 1, :hg]
        extra = _colsum(ddec8 * dec8) + dtot8 * e8[CHUNK - 1:CHUNK, :]
        dacs8 = dacs8 + de8 * e8 - ddec8 * dec8 + jnp.where(row8 == CHUNK - 1, extra, 0.0)
        da = sum(_dot(upper_b, p) for p in _split3(dacs8))
        av = a_ref[...]
        ddt = da * av + ddtx8
        ddtr = ddt * _sigmoid(dtr_ref[...] + bias_ref[...])
        ddtr_ref[...] = ddtr
        gbias_ref[...] += _colsum(ddtr)
        ga_ref[...] += _colsum(da * dt) * av
        gd_ref[...] += gd8

    return pl.pallas_call(
        _attach_side(body, 10, 5, side, grid), name="ssd_bwd", grid=grid,
        in_specs=[sp["xc"], sp["dtr"], sp["dtrt"], sp["prow"], sp["pcol"], sp["prow"], sp["pcol"], sp["prow"],
                  sp["st"], sp["y"]] + [ANY] * len(side.ins),
        out_specs=[sp["xc"], sp["dtr"], sp["prow"], sp["prow"], sp["prow"]] + [ANY] * len(side.out_shapes),
        out_shape=[SDS((t, N_GROUPS * GROUP_W), F32), SDS((N_GROUPS, t, hg), F32)]
        + [SDS((N_GROUPS, 1, hg), F32)] * 3 + list(side.out_shapes),
        scratch_shapes=[pltpu.VMEM((gps, D_STATE, hw), F32)] + list(side.scratch),
        compiler_params=_cparams(("arbitrary", "arbitrary", "arbitrary")))(
            xc, dtr, dtrt, bias, biast, a, at, dskip, states, dy, *side.ins)


def _group_bcast(v, width, fn):
    parts = []
    for q in range(v.shape[-1] // width):
        s = fn(v[:, q * width:(q + 1) * width])
        parts.append(jnp.broadcast_to(s, (v.shape[0], width)))
    return jnp.concatenate(parts, axis=-1)


def _gate_norm_out(y, z, g, w):
    t, d = y.shape
    n = w.shape[1]
    tm = 256
    gw = d // N_GROUPS

    def body(y_ref, z_ref, g_ref, w_ref, yn_ref, o_ref):
        zv = z_ref[...].astype(F32)
        u = y_ref[...].astype(F32) * (zv * _sigmoid(zv))
        r = lax.rsqrt(_group_bcast(u * u, gw, lambda p: jnp.mean(p, axis=-1, keepdims=True)) + EPS)
        yn = (u * r * g_ref[...]).astype(yn_ref.dtype)
        yn_ref[...] = yn
        o_ref[...] = _dot(yn, w_ref[...]).astype(o_ref.dtype)

    return pl.pallas_call(
        body, name="gate_norm_ssm_out", grid=(t // tm,),
        in_specs=[_rs(tm, d), _rs(tm, d), _fs((1, d)), _fs((d, n))], out_specs=[_rs(tm, d), _rs(tm, n)],
        out_shape=[SDS((t, d), BF16), SDS((t, n), BF16)], compiler_params=_cparams(("arbitrary",)))(y, z, g, w)


def _gate_norm_bwd_epilogue(dv, rows, fulls):
    yv, zv = rows[0][...].astype(F32), rows[1][...].astype(F32)
    gw = yv.shape[-1] // N_GROUPS
    sg = _sigmoid(zv)
    sz = zv * sg
    u = yv * sz
    r = lax.rsqrt(_group_bcast(u * u, gw, lambda p: jnp.mean(p, axis=-1, keepdims=True)) + EPS)
    uhat = u * r
    duhat = dv * fulls[0][...]
    du = r * (duhat - uhat * _group_bcast(duhat * uhat, gw, lambda p: jnp.mean(p, axis=-1, keepdims=True)))
    dz = du * yv * sg * (1.0 + zv * (1.0 - sg))
    return [du * sz, dz, _colsum(dv * uhat)]


def _rope_tables(seq):
    half = ATT_D // 2
    inv = ROPE_THETA ** (-jnp.arange(half, dtype=F32) / half)
    ang = jnp.arange(seq, dtype=F32)[:, None] * inv[None, :]
    cos, sin = jnp.cos(ang), jnp.sin(ang)
    return jnp.concatenate([cos, cos], axis=-1), jnp.concatenate([-sin, sin], axis=-1)


ATT_TILE = 512
ATT_QB = 8


def _strided_spec(r, mtiles):
    return pl.BlockSpec((None, r, None, ATT_TILE // r, ATT_W), lambda i: (i // mtiles, 0, i % mtiles, 0, 0))


def _strided_shape(nb, r, mtiles, dtype):
    return SDS((nb, r, mtiles, ATT_TILE // r, ATT_W), dtype)


def _to_strided(val, out_ref, lanes, r, sc):
    if r == 1:
        out_ref[0, :, lanes] = val.astype(out_ref.dtype)
        return
    sc[...] = val
    for rr in range(r):
        out_ref[rr, :, lanes] = sc[pl.ds(rr, ATT_TILE // r, stride=r), :].astype(out_ref.dtype)


def _from_strided(in_ref, lanes, r, sc):
    if r == 1:
        return in_ref[0, :, lanes].astype(F32)
    for rr in range(r):
        sc[pl.ds(rr, ATT_TILE // r, stride=r), :] = in_ref[rr, :, lanes].astype(F32)
    return sc[...]


def _rope_fwd(qkv, cos, sin, nb, seq):
    t = qkv.shape[0]
    tm = ATT_TILE
    mtiles = seq // tm
    w = ATT_HEADS * ATT_D
    tab = pl.BlockSpec((tm, ATT_D), lambda i: (i % mtiles, 0))
    ng = len(ATT_DILATIONS)

    def body(q_ref, k_ref, v_ref, cos_ref, sin_ref, *rest):
        outs, sc = rest[:3 * ng], rest[3 * ng]
        c, s = cos_ref[...], sin_ref[...]
        for which, ref in enumerate((q_ref, k_ref, v_ref)):
            for h in range(ATT_HEADS):
                g, slot = divmod(h, ATT_SLOTS)
                p = ref[:, h * ATT_D:(h + 1) * ATT_D].astype(F32)
                if which < 2:
                    p = p * c + pltpu.roll(p, ATT_D // 2, 1) * s
                _to_strided(p, outs[which * ng + g], slice(slot * ATT_D, (slot + 1) * ATT_D), ATT_DILATIONS[g], sc)

    out_specs = [_strided_spec(r, mtiles) for _ in range(3) for r in ATT_DILATIONS]
    out_shape = [_strided_shape(nb, r, mtiles, BF16) for _ in range(3) for r in ATT_DILATIONS]
    outs = pl.pallas_call(
        body, name="rope_fwd", grid=(t // tm,),
        in_specs=[_rs(tm, w, 0), _rs(tm, w, 1), _rs(tm, w, 2), tab, tab], out_specs=out_specs, out_shape=out_shape,
        scratch_shapes=[pltpu.VMEM((tm, ATT_D), F32)], compiler_params=_cparams(("arbitrary",)))(
            qkv, qkv, qkv, cos, sin)
    flat = [o.reshape(t, ATT_W) for o in outs]
    return flat[0:ng], flat[ng:2 * ng], flat[2 * ng:]


def _rope_bwd(dq, dk, dv, cos, sin, nb, seq):
    t = dq[0].shape[0]
    tm = ATT_TILE
    mtiles = seq // tm
    w = ATT_HEADS * ATT_D
    tab = pl.BlockSpec((tm, ATT_D), lambda i: (i % mtiles, 0))
    ng = len(ATT_DILATIONS)

    def body(*refs):
        ins, (cos_ref, sin_ref, o_ref, sc) = refs[:3 * ng], refs[3 * ng:]
        c, s = cos_ref[...], sin_ref[...]
        for which in range(3):
            for h in range(ATT_HEADS):
                g, slot = divmod(h, ATT_SLOTS)
                p = _from_strided(ins[which * ng + g], slice(slot * ATT_D, (slot + 1) * ATT_D), ATT_DILATIONS[g], sc)
                if which < 2:
                    p = p * c - pltpu.roll(p, ATT_D // 2, 1) * s
                o_ref[:, which * w + h * ATT_D:which * w + (h + 1) * ATT_D] = p.astype(o_ref.dtype)

    views = [a.reshape(nb, r, mtiles, tm // r, ATT_W) for grp in (dq, dk, dv) for a, r in zip(grp, ATT_DILATIONS)]
    return pl.pallas_call(
        body, name="rope_bwd", grid=(t // tm,),
        in_specs=[_strided_spec(r, mtiles) for _ in range(3) for r in ATT_DILATIONS] + [tab, tab],
        out_specs=_rs(tm, 3 * w), out_shape=SDS((t, 3 * w), BF16),
        scratch_shapes=[pltpu.VMEM((tm, ATT_D), F32)], compiler_params=_cparams(("arbitrary",)))(*views, cos, sin)


def _att_masks():
    ri = lax.broadcasted_iota(jnp.int32, (ATT_BLOCK, ATT_BLOCK), 0)
    ci = lax.broadcasted_iota(jnp.int32, (ATT_BLOCK, ATT_BLOCK), 1)
    return ci <= ri, ci >= ri


def _att_fwd(q, k, v, g, seq):
    t, w = q.shape
    rows = ATT_QB * ATT_BLOCK
    nbs = seq // ATT_DILATIONS[g] // ATT_BLOCK
    scale = ATT_D ** -0.5
    cur = pl.BlockSpec((rows, w), lambda n: (n, 0))
    prev = pl.BlockSpec((ATT_BLOCK, w), lambda n: (jnp.maximum(n * ATT_QB - 1, 0), 0))

    def body(q_ref, kc_ref, kp_ref, vc_ref, vp_ref, o_ref, lse_ref):
        mcur, mprev = _att_masks()
        for i in range(ATT_QB):
            blk = pl.program_id(0) * ATT_QB + i
            mask = jnp.concatenate([mprev & ((blk % nbs) != 0), mcur], axis=-1)
            own = slice(i * ATT_BLOCK, (i + 1) * ATT_BLOCK)
            for h in range(ATT_SLOTS):
                sl = slice(h * ATT_D, (h + 1) * ATT_D)
                if i == 0:
                    keys = jnp.concatenate([kp_ref[:, sl], kc_ref[own, sl]], axis=0)
                    vals = jnp.concatenate([vp_ref[:, sl], vc_ref[own, sl]], axis=0)
                else:
                    both = slice((i - 1) * ATT_BLOCK, (i + 1) * ATT_BLOCK)
                    keys, vals = kc_ref[both, sl], vc_ref[both, sl]
                s = jnp.where(mask, _dot_nt(q_ref[own, sl], keys) * scale, -jnp.inf)
                m = jnp.max(s, axis=-1, keepdims=True)
                p = jnp.exp(s - m)
                den = jnp.sum(p, axis=-1, keepdims=True)
                o_ref[own, sl] = _dot(p.astype(BF16), vals) / den
                lse_ref[own, sl] = jnp.broadcast_to(m + jnp.log(den), (ATT_BLOCK, ATT_D))

    return pl.pallas_call(
        body, name=f"att_fwd_{g}", grid=(t // rows,), in_specs=[cur, cur, prev, cur, prev], out_specs=[cur, cur],
        out_shape=[SDS((t, w), F32), SDS((t, w), F32)],
        compiler_params=_cparams(("arbitrary",)))(q, k, k, v, v)


def _att_bwd(q, k, v, do, lse, dlt, g, seq):
    t, w = q.shape
    nblk = t // ATT_BLOCK
    rows = ATT_QB * ATT_BLOCK
    nbs = seq // ATT_DILATIONS[g] // ATT_BLOCK
    scale = ATT_D ** -0.5
    cur = pl.BlockSpec((rows, w), lambda n: (n, 0))
    nxt = pl.BlockSpec((ATT_BLOCK, w), lambda n: (jnp.minimum((n + 1) * ATT_QB, nblk - 1), 0))

    def body(qc_ref, qn_ref, k_ref, v_ref, doc_ref, don_ref, lsec_ref, lsen_ref, dltc_ref, dltn_ref,
             dq_ref, dk_ref, dv_ref, carry):
        n = pl.program_id(0)

        @pl.when(n == 0)
        def _():
            carry[...] = jnp.zeros_like(carry)

        mcur, mprev = _att_masks()

        def pair(cur_ref, nxt_ref, i, sl):
            if i + 1 < ATT_QB:
                return cur_ref[i * ATT_BLOCK:(i + 2) * ATT_BLOCK, sl]
            return jnp.concatenate([cur_ref[i * ATT_BLOCK:, sl], nxt_ref[:, sl]], axis=0)

        for h in range(ATT_SLOTS):
            sl = slice(h * ATT_D, (h + 1) * ATT_D)
            from_prev = carry[:, sl]
            for i in range(ATT_QB):
                blk = n * ATT_QB + i
                has_next = (((blk + 1) % nbs) != 0) & (blk + 1 < nblk)
                mask = jnp.concatenate([mcur, mprev & has_next], axis=0)
                own = slice(i * ATT_BLOCK, (i + 1) * ATT_BLOCK)
                kh, vh = k_ref[own, sl], v_ref[own, sl]
                qs, dos = pair(qc_ref, qn_ref, i, sl), pair(doc_ref, don_ref, i, sl)
                lse, dlt = pair(lsec_ref, lsen_ref, i, sl), pair(dltc_ref, dltn_ref, i, sl)
                p = jnp.where(mask, jnp.exp(_dot_nt(qs, kh) * scale - lse), 0.0)
                ds = (p * (_dot_nt(dos, vh) - dlt) * scale).astype(BF16)
                dqs = _dot(ds, kh)
                dq_ref[own, sl] = (from_prev + dqs[:ATT_BLOCK]).astype(dq_ref.dtype)
                from_prev = dqs[ATT_BLOCK:]
                dk_ref[own, sl] = _dot_tn(ds, qs).astype(dk_ref.dtype)
                dv_ref[own, sl] = _dot_tn(p.astype(BF16), dos).astype(dv_ref.dtype)
            carry[:, sl] = from_prev

    return pl.pallas_call(
        body, name=f"att_bwd_{g}", grid=(t // rows,), in_specs=[cur, nxt, cur, cur, cur, nxt, cur, nxt, cur, nxt],
        out_specs=[cur, cur, cur], out_shape=[SDS((t, w), BF16)] * 3,
        scratch_shapes=[pltpu.VMEM((ATT_BLOCK, w), F32)],
        compiler_params=_cparams(("arbitrary",)))(q, q, k, v, do, do, lse, lse, dlt, dlt)


def _merge_weights(ls):
    m = jnp.maximum(jnp.maximum(ls[0], ls[1]), ls[2])
    es = [jnp.exp(v - m) for v in ls]
    den = es[0] + es[1] + es[2]
    return [e / den for e in es]


def _merge_fwd(o, lse, nb, seq):
    t = o[0].shape[0]
    tm = ATT_TILE
    mtiles = seq // tm
    ng = len(ATT_DILATIONS)

    def body(*refs):
        o_refs, l_refs, out_ref, scs = refs[:ng], refs[ng:2 * ng], refs[2 * ng], refs[2 * ng + 1:]
        for slot in range(ATT_SLOTS):
            lanes = slice(slot * ATT_D, (slot + 1) * ATT_D)
            ov = [_from_strided(o_refs[g], lanes, r, scs[2 * g]) for g, r in enumerate(ATT_DILATIONS)]
            ws = _merge_weights([_from_strided(l_refs[g], lanes, r, scs[2 * g + 1])
                                 for g, r in enumerate(ATT_DILATIONS)])
            out_ref[:, lanes] = (ws[0] * ov[0] + ws[1] * ov[1] + ws[2] * ov[2]).astype(out_ref.dtype)

    views = [a.reshape(nb, r, mtiles, tm // r, ATT_W) for grp in (o, lse) for a, r in zip(grp, ATT_DILATIONS)]
    return pl.pallas_call(
        body, name="att_merge_fwd", grid=(t // tm,),
        in_specs=[_strided_spec(r, mtiles) for _ in range(2) for r in ATT_DILATIONS],
        out_specs=_rs(tm, ATT_W), out_shape=SDS((t, ATT_W), BF16),
        scratch_shapes=[pltpu.VMEM((tm, ATT_D), F32)] * (2 * ng), compiler_params=_cparams(("arbitrary",)))(*views)


def _merge_bwd(o, lse, datt, nb, seq):
    t = o[0].shape[0]
    tm = ATT_TILE
    mtiles = seq // tm
    ng = len(ATT_DILATIONS)

    def body(*refs):
        o_refs, l_refs, d_ref = refs[:ng], refs[ng:2 * ng], refs[2 * ng]
        do_refs, dlt_refs = refs[2 * ng + 1:3 * ng + 1], refs[3 * ng + 1:4 * ng + 1]
        scs = refs[4 * ng + 1:]
        for slot in range(ATT_SLOTS):
            lanes = slice(slot * ATT_D, (slot + 1) * ATT_D)
            ov = [_from_strided(o_refs[g], lanes, r, scs[2 * g]) for g, r in enumerate(ATT_DILATIONS)]
            ws = _merge_weights([_from_strided(l_refs[g], lanes, r, scs[2 * g + 1])
                                 for g, r in enumerate(ATT_DILATIONS)])
            dv = d_ref[:, lanes]
            att = ws[0] * ov[0] + ws[1] * ov[1] + ws[2] * ov[2]
            dot = jnp.broadcast_to(jnp.sum(dv * att, axis=-1, keepdims=True), (tm, ATT_D))
            for g, r in enumerate(ATT_DILATIONS):
                _to_strided(ws[g] * dv, do_refs[g], lanes, r, scs[2 * ng])
                _to_strided(ws[g] * dot, dlt_refs[g], lanes, r, scs[2 * ng + 1])

    views = [a.reshape(nb, r, mtiles, tm // r, ATT_W) for grp in (o, lse) for a, r in zip(grp, ATT_DILATIONS)]
    outs = pl.pallas_call(
        body, name="att_merge_bwd", grid=(t // tm,),
        in_specs=[_strided_spec(r, mtiles) for _ in range(2) for r in ATT_DILATIONS] + [_rs(tm, ATT_W)],
        out_specs=[_strided_spec(r, mtiles) for _ in range(2) for r in ATT_DILATIONS],
        out_shape=[_strided_shape(nb, r, mtiles, dt) for dt in (BF16, F32) for r in ATT_DILATIONS],
        scratch_shapes=[pltpu.VMEM((tm, ATT_D), F32)] * (2 * ng + 2), compiler_params=_cparams(("arbitrary",)))(
            *views, datt)
    flat = [a.reshape(t, ATT_W) for a in outs]
    return flat[:ng], flat[ng:]


def _branch_gates(rows, fulls):
    return (_sigmoid(rows[0][...].astype(F32) + fulls[0][...]), _sigmoid(rows[1][...].astype(F32) + fulls[1][...]))


def _mix_fwd_epilogue(y_att, rows, fulls):
    g0, g1 = _branch_gates(rows, fulls)
    return [y_att, g0 * rows[2][...].astype(F32) + g1 * y_att]


def _mix_bwd_epilogue(dm, rows, fulls):
    g0, g1 = _branch_gates(rows, fulls)
    dg = jnp.concatenate([dm * rows[2][...].astype(F32) * g0 * (1.0 - g0),
                          dm * rows[3][...].astype(F32) * g1 * (1.0 - g1)], axis=-1)
    return [dm * g0, dm * g1, dg, _colsum(dg)]


FFN_TM = 512


def _ffn_in(h2, wg_t, wu_t):
    t, d = h2.shape
    f = wg_t.shape[0]
    tm, tn = FFN_TM, _pick(f, 1536)

    def body(a_ref, g_ref, u_ref, gt_ref, up_ref, act_ref):
        a = a_ref[...]
        gt = _dot_nt(a, g_ref[...])
        up = _dot_nt(a, u_ref[...])
        gt_ref[...] = gt.astype(BF16)
        up_ref[...] = up.astype(BF16)
        act_ref[...] = (gt * _sigmoid(gt) * up).astype(BF16)

    a_spec = pl.BlockSpec((tm, d), lambda j, i: (i, 0))
    w_spec = pl.BlockSpec((tn, d), lambda j, i: (j, 0))
    o_spec = pl.BlockSpec((tm, tn), lambda j, i: (i, j))
    return pl.pallas_call(
        body, name="ffn_in", grid=(f // tn, t // tm), in_specs=[a_spec, w_spec, w_spec],
        out_specs=[o_spec] * 3, out_shape=[SDS((t, f), BF16)] * 3,
        compiler_params=_cparams(("parallel", "arbitrary")))(h2, wg_t, wu_t)


def _ffn_bwd_in(dx2, w_down, gt, up):
    t, d = dx2.shape
    f = w_down.shape[0]
    tm, tn = FFN_TM, _pick(f, 1536)

    def body(a_ref, w_ref, g_ref, u_ref, dgt_ref, dup_ref):
        dv = _dot_nt(a_ref[...], w_ref[...])
        gv = g_ref[...].astype(F32)
        sg = _sigmoid(gv)
        dgt_ref[...] = (dv * u_ref[...].astype(F32) * sg * (1.0 + gv * (1.0 - sg))).astype(BF16)
        dup_ref[...] = (dv * gv * sg).astype(BF16)

    a_spec = pl.BlockSpec((tm, d), lambda j, i: (i, 0))
    w_spec = pl.BlockSpec((tn, d), lambda j, i: (j, 0))
    o_spec = pl.BlockSpec((tm, tn), lambda j, i: (i, j))
    return pl.pallas_call(
        body, name="ffn_bwd_in", grid=(f // tn, t // tm), in_specs=[a_spec, w_spec, o_spec, o_spec],
        out_specs=[o_spec] * 2, out_shape=[SDS((t, f), BF16)] * 2,
        compiler_params=_cparams(("parallel", "arbitrary")))(dx2, w_down, gt, up)


def _adamw(w, g, m, v, name):
    r, c = w.shape[-2:]
    lead = w.ndim - 2
    tr = _row_tile(r, max(8, 400_000 // c))
    c1 = 1.0 / (1.0 - ADAM_B1 ** ADAM_STEP)
    c2 = 1.0 / (1.0 - ADAM_B2 ** ADAM_STEP)

    def fn(i, w_ref, g_ref, m_ref, v_ref):
        gv = g_ref[...]
        mn = ADAM_B1 * m_ref[...] + (1.0 - ADAM_B1) * gv
        vn = ADAM_B2 * v_ref[...] + (1.0 - ADAM_B2) * (gv * gv)
        delta = -ADAM_LR * ((mn * c1) / (jnp.sqrt(vn * c2) + ADAM_EPS) + ADAM_WD * w_ref[...])
        return [delta, mn, vn]

    spec = pl.BlockSpec((None,) * lead + (tr, c), lambda i: (0,) * lead + (i, 0))
    return _rw(name, fn, r // tr, [(w, spec), (g, spec), (m, spec), (v, spec)], [(SDS(w.shape, F32), spec)] * 3)


ANY = pl.BlockSpec(memory_space=pl.ANY)


def _place():
    x, y, c = lax.axis_index("x"), lax.axis_index("y"), lax.axis_index("c")
    chips = [(1 - x, y), (x, 1 - y), (1 - x, 1 - y)]
    return x, y, c, chips


def _remote(src, dst, ssem, rsem, to):
    return pltpu.make_async_remote_copy(src_ref=src, dst_ref=dst, send_sem=ssem, recv_sem=rsem, device_id=to,
                                        device_id_type=MESH)


def _copy_through_vmem(src, dst, buf, isem, osem):
    chunk = buf.shape[1]
    n = src.shape[0] // chunk
    load = lambda k: pltpu.make_async_copy(src.at[pl.ds(k * chunk, chunk)], buf.at[k % 2], isem.at[k % 2])
    store = lambda k: pltpu.make_async_copy(buf.at[k % 2], dst.at[pl.ds(k * chunk, chunk)], osem.at[k % 2])
    load(0).start()
    for k in range(n):
        load(k).wait()
        if k + 1 < n:
            if k >= 1:
                store(k - 1).wait()
            load(k + 1).start()
        store(k).start()
    if n >= 2:
        store(n - 2).wait()
    store(n - 1).wait()


def _copy_scratch(rows, width, dtype):
    chunk = _row_tile(rows, 512)
    return [pltpu.VMEM((2, chunk, width), dtype), pltpu.SemaphoreType.DMA((2,)), pltpu.SemaphoreType.DMA((2,))]


def _gather_weights(wp):
    def body(w_ref, out_ref, ssem, rsem, buf, isem, osem):
        x, y, c, chips = _place()
        me = 2 * x + y
        sib = (x, y, 1 - c)
        first = [_remote(w_ref.at[c], out_ref.at[me, c], ssem.at[j], rsem.at[j], (*chip, c))
                 for j, chip in enumerate(chips)]
        for cp in first:
            cp.start()
        for half in range(2):
            _copy_through_vmem(w_ref.at[half], out_ref.at[me, half], buf, isem, osem)
        passed = []
        for j, chip in enumerate(chips):
            ci = 2 * chip[0] + chip[1]
            _remote(w_ref.at[c], out_ref.at[ci, c], ssem.at[j], rsem.at[j], (*chip, c)).wait_recv()
            cp = _remote(out_ref.at[ci, c], out_ref.at[ci, c], ssem.at[3 + j], rsem.at[3 + j], sib)
            cp.start()
            passed.append(cp)
        for j, chip in enumerate(chips):
            ci = 2 * chip[0] + chip[1]
            _remote(out_ref.at[ci, 1 - c], out_ref.at[ci, 1 - c], ssem.at[3 + j], rsem.at[3 + j], sib).wait_recv()
        for cp in first + passed:
            cp.wait_send()

    return pl.pallas_call(
        body, name="gather_weights", in_specs=[ANY], out_specs=ANY,
        out_shape=SDS((N_CHIPS,) + wp.shape, wp.dtype),
        scratch_shapes=[pltpu.SemaphoreType.DMA((6,)), pltpu.SemaphoreType.DMA((6,))]
        + _copy_scratch(wp.shape[1], wp.shape[2], wp.dtype),
        compiler_params=pltpu.CompilerParams(has_side_effects=True))(wp)


def _swap_halves(g2, tag):
    nch = g2.shape[0]

    def body(g_ref, out_ref, ssem, rsem):
        x, y, c, _ = _place()
        cps = [_remote(g_ref.at[k, 1 - c], out_ref.at[k], ssem.at[k], rsem.at[k], (x, y, 1 - c)) for k in range(nch)]
        for cp in cps:
            cp.start()
        for cp in cps:
            cp.wait()

    return pl.pallas_call(
        body, name="swap_halves_" + tag, in_specs=[ANY], out_specs=ANY,
        out_shape=SDS((nch,) + g2.shape[2:], g2.dtype),
        scratch_shapes=[pltpu.SemaphoreType.DMA((nch,)), pltpu.SemaphoreType.DMA((nch,))],
        compiler_params=pltpu.CompilerParams(has_side_effects=True))(g2)


def _add_own_half(g2, other, c, tag):
    nch, _, rows, w = g2.shape
    tr = _row_tile(rows, 512)
    nr = rows // tr

    def body(c_ref, a_ref, b_ref, o_ref):
        o_ref[...] = (a_ref[...].astype(F32) + b_ref[...].astype(F32)).astype(o_ref.dtype)

    grid_spec = pltpu.PrefetchScalarGridSpec(
        num_scalar_prefetch=1, grid=(nch, nr),
        in_specs=[pl.BlockSpec((None, None, tr, w), lambda k, i, c_ref: (k, c_ref[0], i, 0)),
                  pl.BlockSpec((None, tr, w), lambda k, i, c_ref: (k, i, 0))],
        out_specs=pl.BlockSpec((None, tr, w), lambda k, i, c_ref: (k, i, 0)))
    return pl.pallas_call(
        body, name="add_own_half_" + tag, grid_spec=grid_spec, out_shape=SDS(other.shape, other.dtype),
        compiler_params=_cparams(("arbitrary", "arbitrary")))(jnp.reshape(c, (1,)).astype(jnp.int32), g2, other)


def _sum_chips(q, tag):
    nch, rows, w = q.shape
    tr = _row_tile(rows, 512)

    def fn(i, q_ref):
        return [((q_ref[0].astype(F32) + q_ref[1].astype(F32)) + q_ref[2].astype(F32)) + q_ref[3].astype(F32)]

    return _rw("sum_chips_" + tag, fn, rows // tr, [(q, pl.BlockSpec((nch, tr, w), lambda i: (0, i, 0)))],
               [(SDS((rows, w), F32), _rs(tr, w))])[0]


def _chip_copies(src_ref, dst_ref, ssem, rsem, outgoing):
    x, y, c, chips = _place()
    me = 2 * x + y
    cps = []
    for j, chip in enumerate(chips):
        ci = 2 * chip[0] + chip[1]
        cps.append(_remote(src_ref.at[ci], dst_ref.at[me if outgoing else ci], ssem.at[j], rsem.at[j], (*chip, c)))
    return cps, me


def _scatter_side(p):
    def first(ins, outs, scr):
        cps, me = _chip_copies(ins[0], outs[0], scr[0], scr[1], True)
        for cp in cps:
            cp.start()
        pltpu.make_async_copy(ins[0].at[me], outs[0].at[me], scr[2]).start()

    def last(ins, outs, scr):
        for cp in _chip_copies(ins[0], outs[0], scr[0], scr[1], False)[0]:
            cp.wait_recv()
        cps, me = _chip_copies(ins[0], outs[0], scr[0], scr[1], True)
        for cp in cps:
            cp.wait_send()
        pltpu.make_async_copy(ins[0].at[me], outs[0].at[me], scr[2]).wait()

    return _Side((p,), (SDS(p.shape, p.dtype),),
                 (pltpu.SemaphoreType.DMA((3,)), pltpu.SemaphoreType.DMA((3,)), pltpu.SemaphoreType.DMA(())),
                 first, None, last)


def _gather_copies(w_ref, out_ref, ssem, rsem):
    x, y, c, chips = _place()
    me = 2 * x + y
    sib = (x, y, 1 - c)
    sends, arrivals, forwards, from_sib = [], [], [], []
    for j, chip in enumerate(chips):
        ci = 2 * chip[0] + chip[1]
        sends.append(_remote(w_ref.at[c], out_ref.at[me, c], ssem.at[j], rsem.at[j], (*chip, c)))
        arrivals.append(_remote(w_ref.at[c], out_ref.at[ci, c], ssem.at[j], rsem.at[j], (*chip, c)))
        forwards.append(_remote(out_ref.at[ci, c], out_ref.at[ci, c], ssem.at[3 + j], rsem.at[3 + j], sib))
        from_sib.append(_remote(out_ref.at[ci, 1 - c], out_ref.at[ci, 1 - c], ssem.at[3 + j], rsem.at[3 + j], sib))
    return sends, arrivals, forwards, from_sib, me


def _gather_side(wp):
    def first(ins, outs, scr):
        sends, _, _, _, me = _gather_copies(ins[0], outs[0], scr[0], scr[1])
        for cp in sends:
            cp.start()
        pltpu.make_async_copy(ins[0], outs[0].at[me], scr[2]).start()

    def mid(ins, outs, scr):
        _, arrivals, forwards, _, _ = _gather_copies(ins[0], outs[0], scr[0], scr[1])
        for arrived, forward in zip(arrivals, forwards):
            arrived.wait_recv()
            forward.start()

    def last(ins, outs, scr):
        sends, _, forwards, from_sib, me = _gather_copies(ins[0], outs[0], scr[0], scr[1])
        for cp in from_sib:
            cp.wait_recv()
        for cp in sends + forwards:
            cp.wait_send()
        pltpu.make_async_copy(ins[0], outs[0].at[me], scr[2]).wait()

    return _Side((wp,), (SDS((N_CHIPS,) + wp.shape, wp.dtype),),
                 (pltpu.SemaphoreType.DMA((6,)), pltpu.SemaphoreType.DMA((6,)), pltpu.SemaphoreType.DMA(())),
                 first, mid, last)


def _allreduce_small(v, name):
    rows, w = v.shape
    offsets = [(dx, dy, dc) for dx in (0, 1) for dy in (0, 1) for dc in (0, 1)][1:]

    def body(v_ref, o_ref, buf, ssem, rsem):
        x, y, c, _ = _place()
        flip = lambda p, d: 1 - p if d else p
        peers = [(flip(x, dx), flip(y, dy), flip(c, dc)) for dx, dy, dc in offsets]
        index = lambda p: 4 * p[0] + 2 * p[1] + p[2]
        me = index((x, y, c))
        buf[me] = v_ref[...]
        sent = [_remote(v_ref, buf.at[me], ssem.at[q], rsem.at[q], p) for q, p in enumerate(peers)]
        for cp in sent:
            cp.start()
        for q, p in enumerate(peers):
            _remote(v_ref, buf.at[index(p)], ssem.at[q], rsem.at[q], p).wait_recv()
        for cp in sent:
            cp.wait_send()
        acc = buf[0]
        for q in range(1, 8):
            acc = acc + buf[q]
        o_ref[...] = acc

    vm = pl.BlockSpec(memory_space=pltpu.VMEM)
    return pl.pallas_call(
        body, name=name, in_specs=[vm], out_specs=vm, out_shape=SDS((rows, w), F32),
        scratch_shapes=[pltpu.VMEM((8, rows, w), F32), pltpu.SemaphoreType.DMA((7,)), pltpu.SemaphoreType.DMA((7,))],
        compiler_params=pltpu.CompilerParams(has_side_effects=True))(v)


def _join_halves(h, tag):
    def body(h_ref, out_ref, ssem, rsem, buf, isem, osem):
        x, y, c, _ = _place()
        cp = _remote(h_ref, out_ref.at[c], ssem, rsem, (x, y, 1 - c))
        cp.start()
        _copy_through_vmem(h_ref, out_ref.at[c], buf, isem, osem)
        _remote(h_ref, out_ref.at[1 - c], ssem, rsem, (x, y, 1 - c)).wait_recv()
        cp.wait_send()

    return pl.pallas_call(
        body, name="join_halves_" + tag, in_specs=[ANY], out_specs=ANY, out_shape=SDS((2,) + h.shape, h.dtype),
        scratch_shapes=[pltpu.SemaphoreType.DMA(()), pltpu.SemaphoreType.DMA(())]
        + _copy_scratch(h.shape[0], h.shape[1], h.dtype),
        compiler_params=pltpu.CompilerParams(has_side_effects=True))(h)


PACK_W = 1024
SHARDED = ("w_in", "w_ffn_gate", "w_ffn_up", "w_ssm_out", "w_att_out", "w_mix_out", "w_ffn_down")
COL_SHARDED = ("w_in", "w_ffn_gate", "w_ffn_up", "w_att_out")
SMALL = ("norm_mix", "b_gate", "conv_b", "dt_bias", "a_log", "d_skip", "ssm_norm", "norm_ffn", "norm_final")


PACK_ROW_ALIGN = 16


def _rows(n):
    return -(-n // (PACK_W * PACK_ROW_ALIGN)) * PACK_ROW_ALIGN


def _pack_rows(parts, total_rows):
    rows = []
    for p in parts:
        size = int(p.size)
        if size % PACK_W:
            p = jnp.pad(p.reshape(-1), (0, PACK_W - size % PACK_W))
        p = p.reshape(-1, PACK_W)
        rows.append(jnp.pad(p, ((0, _rows(size) - p.shape[0]), (0, 0))))
    used = sum(r.shape[0] for r in rows)
    if total_rows > used:
        rows.append(jnp.zeros((total_rows - used, PACK_W), rows[0].dtype))
    return jnp.concatenate(rows, axis=0)


def _padded_rows(n):
    return -(-n // 32) * 32


def _wire_name(name):
    return name + "_t" if name in COL_SHARDED else name


def _wire_shard(w, name):
    return w.T if name in COL_SHARDED else w


def _group_major(a, axis):
    gw = D_INNER // N_GROUPS
    take = lambda lo, n: lax.slice_in_dim(a, lo, lo + n, axis=axis)
    parts = []
    for g in range(N_GROUPS):
        parts += [take(g * gw, gw), take(D_INNER + g * D_STATE, D_STATE),
                  take(D_INNER + N_GROUPS * D_STATE + g * D_STATE, D_STATE)]
    return jnp.concatenate(parts, axis=axis)


def _group_major_inv(a, axis):
    gw = D_INNER // N_GROUPS
    take = lambda lo, n: lax.slice_in_dim(a, lo, lo + n, axis=axis)
    xs = [take(g * GROUP_W, gw) for g in range(N_GROUPS)]
    bs = [take(g * GROUP_W + gw, D_STATE) for g in range(N_GROUPS)]
    cs = [take(g * GROUP_W + gw + D_STATE, D_STATE) for g in range(N_GROUPS)]
    return jnp.concatenate(xs + bs + cs, axis=axis)


LATE = ("w_ffn_gate_t", "w_ffn_up_t", "w_ssm_out", "w_att_out_t", "w_mix_out", "w_ffn_down")


class _Overlap(NamedTuple):
    gather_side: _Side
    late_weights: Callable
    scatter_side: Callable
    scatter_in: Callable


def _local_step(x, target, wts, overlap):
    nb, seq, d = x.shape
    t = nb * seq
    x = x.reshape(t, d)
    target = target.reshape(t, d)
    hg = HEADS_PER_GROUP

    o1, o2, o3, o4 = D_INNER, D_INNER + CONV_DIM, D_INNER + CONV_DIM + N_HEADS, D_INNER + CONV_DIM + N_HEADS + QKV_DIM
    n_in = o4 + 2 * D_MODEL

    def in_rows(lo, hi):
        per = n_in // N_CHIPS
        parts = [wts["w_in_t"][k, max(lo, k * per) - k * per:min(hi, (k + 1) * per) - k * per]
                 for k in range(N_CHIPS) if max(lo, k * per) < min(hi, (k + 1) * per)]
        return parts[0] if len(parts) == 1 else jnp.concatenate(parts, axis=0)

    w_z = in_rows(0, o1)
    w_xbc = _group_major(in_rows(o1, o2), 0)
    w_dt = jnp.pad(in_rows(o2, o3), ((0, DT_PAD - N_HEADS), (0, 0)))
    w_qkv = in_rows(o3, o4)
    w_gate = in_rows(o4, n_in)
    conv_w = _group_major(wts["conv_w"], 1)
    conv_b = _group_major(wts["conv_b"], 1)

    def per_group_row(p):
        return p.reshape(N_GROUPS, 1, hg)

    def per_group_col(p):
        return p.reshape(N_GROUPS, hg, 1)

    a_neg = -jnp.exp(wts["a_log"])
    bias_r, bias_c = per_group_row(wts["dt_bias"]), per_group_col(wts["dt_bias"])
    a_r, a_c = per_group_row(a_neg), per_group_col(a_neg)
    dskip_r = per_group_row(wts["d_skip"])
    cos, sin = _rope_tables(seq)

    h = _rms_fwd(x, wts["norm_mix"], "rms_mix_fwd")
    z = _mm(h, w_z, "nt", BF16, "proj_z")
    xbc = _mm(h, w_xbc, "nt", F32, "proj_xbc")
    dt_raw = _mm(h, w_dt, "nt", F32, "proj_dt")
    qkv = _mm(h, w_qkv, "nt", BF16, "proj_qkv")
    gate_logits = _mm(h, w_gate, "nt", BF16, "proj_gate")

    xc = _conv_fwd(xbc, conv_w, conv_b, seq)
    dtr = dt_raw[:, :N_HEADS].reshape(t, N_GROUPS, hg).transpose(1, 0, 2)
    dtrt = dt_raw[:, :N_HEADS].reshape(nb, seq, N_GROUPS, hg).transpose(2, 0, 3, 1)
    y, states, *gathered = _ssd_fwd(xc, dtr, dtrt, bias_r, bias_c, a_r, a_c, dskip_r, nb, seq, overlap.gather_side)
    wts = {**wts, **overlap.late_weights(gathered)}
    yn, y_ssm = _gate_norm_out(y, z, wts["ssm_norm"], wts["w_ssm_out"])

    groups = range(len(ATT_DILATIONS))
    qg, kg, vg = _rope_fwd(qkv, cos, sin, nb, seq)
    o_g, lse_g = zip(*[_att_fwd(qg[i], kg[i], vg[i], i, seq) for i in groups])
    att = _merge_fwd(o_g, lse_g, nb, seq)
    gate_halves = [(gate_logits, d, 0), (gate_logits, d, 1)]
    b_gate_halves = [(wts["b_gate"], d, 0), (wts["b_gate"], d, 1)]
    y_att, mixed = _mm_fused(att, wts["w_att_out_t"], "nt", "att_out_mix", 512, _mix_fwd_epilogue,
                             gate_halves + [(y_ssm, d, 0)], b_gate_halves, [(d, BF16), (d, BF16)])

    def residual_and_norm(xv, rows, fulls):
        return [xv, xv * lax.rsqrt(jnp.mean(xv * xv, axis=-1, keepdims=True) + EPS) * fulls[0][...]]

    x1, h2 = _mm_fused(mixed, wts["w_mix_out"], "nn", "mix_out_norm", 512, residual_and_norm, [],
                       [(wts["norm_ffn"], d, 0)], [(d, F32), (d, BF16)], add=x)
    gt, up, act = _ffn_in(h2, wts["w_ffn_gate_t"], wts["w_ffn_up_t"])

    g = {}
    dx2, dx2_b, g["norm_final"], loss = _mm_fused(
        act, wts["w_ffn_down"], "nn", "ffn_down_loss", 512,
        lambda x2, rows, fulls: _final_values(x2, rows[0][...], fulls[0][...]),
        [(target, d, 0)], [(wts["norm_final"].reshape(1, d), d, 0)], [(d, F32), (d, BF16), (d, F32), (1, F32)],
        n_acc=2, add=x1)
    g["w_ffn_down"] = _mm(act, dx2_b, "tn", BF16, "g_ffn_down")
    dgt, dup = _ffn_bwd_in(dx2_b, wts["w_ffn_down"], gt, up)
    g["w_ffn_gate_t"] = _mm(dgt, h2, "tn", BF16, "g_ffn_gate")
    g["w_ffn_up_t"] = _mm(dup, h2, "tn", BF16, "g_ffn_up")
    dh2 = _mm(dgt, wts["w_ffn_gate_t"], "nn", F32, "d_h2_gate")
    dx1, dx1_b, g["norm_ffn"] = _mm_fused(
        dup, wts["w_ffn_up_t"], "nn", "d_h2_up_norm", 512,
        lambda dh, rows, fulls: _rms_bwd_values(rows[0][...], dh, fulls[0][...], rows[1][...]),
        [(x1, d, 0), (dx2, d, 0)], [(wts["norm_ffn"], d, 0)], [(d, F32), (d, BF16), (d, F32)], n_acc=1, add=dh2)

    g["w_mix_out"] = _mm(mixed, dx1_b, "tn", BF16, "g_mix_out")
    dy_ssm, dy_att, dgate, g["b_gate"] = _mm_fused(
        dx1_b, wts["w_mix_out"], "nt", "d_mixed_gates", 512, _mix_bwd_epilogue,
        gate_halves + [(y_ssm, d, 0), (y_att, d, 0)], b_gate_halves,
        [(d, BF16), (d, BF16), (2 * d, BF16), (2 * d, F32)], n_acc=1)

    datt = _mm(dy_att, wts["w_att_out_t"], "nn", F32, "d_att")
    g["w_att_out_t"] = _mm(dy_att, att, "tn", BF16, "g_att_out")
    do_g, dlt_g = _merge_bwd(o_g, lse_g, datt, nb, seq)
    dq_g, dk_g, dv_g = zip(*[_att_bwd(qg[i], kg[i], vg[i], do_g[i], lse_g[i], dlt_g[i], i, seq) for i in groups])
    dqkv = _rope_bwd(dq_g, dk_g, dv_g, cos, sin, nb, seq)

    g["w_ssm_out"] = _mm(yn, dy_ssm, "tn", BF16, "g_ssm_out")
    dy, dz, g["ssm_norm"] = _mm_fused(
        dy_ssm, wts["w_ssm_out"], "nt", "d_yn_norm", 256, _gate_norm_bwd_epilogue,
        [(y, D_INNER, 0), (z, D_INNER, 0)], [(wts["ssm_norm"], D_INNER, 0)],
        [(D_INNER, BF16), (D_INNER, BF16), (D_INNER, F32)], n_acc=1)
    side = overlap.scatter_side({n: g.pop(n) for n in LATE})
    dxc, ddtr, g_bias, g_alog, g_dskip, *scattered = _ssd_bwd(xc, dtr, dtrt, bias_r, bias_c, a_r, a_c, dskip_r,
                                                               states, dy, nb, seq, side)
    g["dt_bias"] = g_bias.reshape(1, N_HEADS)
    g["a_log"] = g_alog.reshape(1, N_HEADS)
    g["d_skip"] = g_dskip.reshape(1, N_HEADS)
    dpre, g_conv_w, g_conv_b = _conv_bwd_pre(xbc, conv_w, conv_b, dxc, seq)
    g["conv_w"] = _group_major_inv(g_conv_w, 1)
    g["conv_b"] = _group_major_inv(g_conv_b, 1)
    dxbc = _conv_bwd_in(dpre, conv_w, seq)
    ddt = jnp.pad(ddtr.transpose(1, 0, 2).reshape(t, N_HEADS), ((0, 0), (0, DT_PAD - N_HEADS))).astype(BF16)

    g_in_t = jnp.concatenate([
        _mm(dz, h, "tn", BF16, "g_in_z"),
        _group_major_inv(_mm(dxbc, h, "tn", BF16, "g_in_xbc"), 0),
        _mm(ddt, h, "tn", BF16, "g_in_dt")[:N_HEADS],
        _mm(dqkv, h, "tn", BF16, "g_in_qkv"),
        _mm(dgate, h, "tn", BF16, "g_in_gate")], axis=0)
    dh = _mm(dz, w_z, "nn", F32, "d_h_z")
    dh = _mm(dxbc, w_xbc, "nn", F32, "d_h_xbc", add=dh)
    dh = _mm(ddt, w_dt, "nn", F32, "d_h_dt", add=dh)
    dh, *scattered_in = _mm(dqkv, w_qkv, "nn", F32, "d_h_qkv", add=dh, side=overlap.scatter_in({"w_in_t": g_in_t}))
    dx, _, g["norm_mix"] = _mm_fused(
        dgate, w_gate, "nn", "d_h_gate_norm", 512,
        lambda dhv, rows, fulls: _rms_bwd_values(rows[0][...], dhv, fulls[0][...], rows[1][...]),
        [(x, d, 0), (dx1, d, 0)], [(wts["norm_mix"], d, 0)], [(d, F32), (d, BF16), (d, F32)], n_acc=1, add=dh)
    return loss[0, 0], dx.reshape(nb, seq, d), g, scattered, scattered_in


def kernel(x, norm_mix, w_in, b_gate, conv_w, conv_b, dt_bias, a_log, d_skip, ssm_norm, w_ssm_out, w_att_out, w_mix_out, norm_ffn, w_ffn_gate, w_ffn_up, w_ffn_down, norm_final, loss_target, m_norm_mix, m_w_in, m_b_gate, m_conv_w, m_conv_b, m_dt_bias, m_a_log, m_d_skip, m_ssm_norm, m_w_ssm_out, m_w_att_out, m_w_mix_out, m_norm_ffn, m_w_ffn_gate, m_w_ffn_up, m_w_ffn_down, m_norm_final, v_norm_mix, v_w_in, v_b_gate, v_conv_w, v_conv_b, v_dt_bias, v_a_log, v_d_skip, v_ssm_norm, v_w_ssm_out, v_w_att_out, v_w_mix_out, v_norm_ffn, v_w_ffn_gate, v_w_ffn_up, v_w_ffn_down, v_norm_final):
    names = ("norm_mix", "w_in", "b_gate", "conv_w", "conv_b", "dt_bias", "a_log", "d_skip", "ssm_norm", "w_ssm_out",
             "w_att_out", "w_mix_out", "norm_ffn", "w_ffn_gate", "w_ffn_up", "w_ffn_down", "norm_final")
    w_loc = dict(zip(names, (norm_mix, w_in, b_gate, conv_w, conv_b, dt_bias, a_log, d_skip, ssm_norm, w_ssm_out,
                             w_att_out, w_mix_out, norm_ffn, w_ffn_gate, w_ffn_up, w_ffn_down, norm_final)))
    m_loc = dict(zip(names, (m_norm_mix, m_w_in, m_b_gate, m_conv_w, m_conv_b, m_dt_bias, m_a_log, m_d_skip,
                             m_ssm_norm, m_w_ssm_out, m_w_att_out, m_w_mix_out, m_norm_ffn, m_w_ffn_gate,
                             m_w_ffn_up, m_w_ffn_down, m_norm_final)))
    v_loc = dict(zip(names, (v_norm_mix, v_w_in, v_b_gate, v_conv_w, v_conv_b, v_dt_bias, v_a_log, v_d_skip,
                             v_ssm_norm, v_w_ssm_out, v_w_att_out, v_w_mix_out, v_norm_ffn, v_w_ffn_gate,
                             v_w_ffn_up, v_w_ffn_down, v_norm_final)))
    two_d = lambda a: a.reshape(a.shape[-2:]) if a.ndim >= 2 else a.reshape(1, -1)
    w2 = {n: two_d(a) for n, a in w_loc.items()}
    chip = 2 * lax.axis_index("x") + lax.axis_index("y")
    c = lax.axis_index("c")

    wire_shapes = {n: _wire_shard(w2[n], n).shape for n in SHARDED}
    true_rows = {n: wire_shapes[n][0] * wire_shapes[n][1] // PACK_W for n in SHARDED}
    seg_rows = {n: _rows(wire_shapes[n][0] * wire_shapes[n][1]) for n in SHARDED}
    buckets = {"first": ("w_in",), "late": tuple(n for n in SHARDED if n != "w_in")}
    rows_of = {b: _padded_rows(sum(seg_rows[n] for n in ns)) for b, ns in buckets.items()}

    def pack_shards(b):
        packed = _pack_rows([_wire_shard(w2[n], n).astype(BF16) for n in buckets[b]], rows_of[b])
        return packed.reshape(2, rows_of[b] // 2, PACK_W)

    def unpack_full(gathered, b):
        wg, out, off = gathered.reshape(N_CHIPS, rows_of[b], PACK_W), {}, 0
        for n in buckets[b]:
            rows, cols = wire_shapes[n]
            out[_wire_name(n)] = wg[:, off:off + true_rows[n]].reshape(N_CHIPS * rows, cols)
            off += seg_rows[n]
        return out

    def pack_grads(g, b):
        sections = [_pack_rows([g[_wire_name(n)].reshape(N_CHIPS, true_rows[n], PACK_W)[k] for n in buckets[b]],
                               rows_of[b]) for k in range(N_CHIPS)]
        return jnp.stack(sections).reshape(N_CHIPS, 2, rows_of[b] // 2, PACK_W)

    def chip_sums(g, b):
        g2 = pack_grads(g, b)
        return _add_own_half(g2, _swap_halves(g2, b), c, b)

    def finish(by_source, b):
        reduced = _join_halves(_sum_chips(by_source, b), b).reshape(rows_of[b], PACK_W)
        out, off = {}, 0
        for n in buckets[b]:
            out[n] = reduced[off:off + true_rows[n]].reshape(wire_shapes[n])
            off += seg_rows[n]
        return out

    full = {"w_in_t": _gather_weights(pack_shards("first")).reshape(N_CHIPS, rows_of["first"], PACK_W)}
    for n in SMALL:
        full[n] = w2[n]
    overlap = _Overlap(_gather_side(pack_shards("late")), lambda outs: unpack_full(outs[0], "late"),
                       lambda g: _scatter_side(chip_sums(g, "late")), lambda g: _scatter_side(chip_sums(g, "first")))

    n_conv = w2["conv_w"].shape[1]
    placed = lax.dynamic_update_slice_in_dim(jnp.zeros((CONV_K, N_CHIPS * n_conv), F32), w2["conv_w"], chip * n_conv, 1)
    placed = jnp.where(c == 0, placed, 0.0)
    full["conv_w"] = _allreduce_small(_pack_rows([placed], _rows(int(placed.size))), "gather_conv_w").reshape(
        -1)[:placed.size].reshape(placed.shape)

    loss_sum, grad_x, g_full, scattered, scattered_in = _local_step(x, loss_target, full, overlap)
    loss = lax.psum(loss_sum, ("x", "y", "c"))

    g_shard = {}
    small_names = SMALL + ("conv_w",)
    small_flat = jnp.concatenate([g_full[n].reshape(-1) for n in small_names])
    small = _allreduce_small(_pack_rows([small_flat], _rows(int(small_flat.size))), "allreduce_small").reshape(-1)
    off = 0
    for n in small_names:
        size = int(g_full[n].size)
        g_shard[n] = small[off:off + size].reshape(g_full[n].shape)
        off += size
    g_shard["conv_w"] = lax.dynamic_slice_in_dim(g_shard["conv_w"], chip * n_conv, n_conv, 1)

    g_shard.update(finish(scattered[0], "late"))
    g_shard.update(finish(scattered_in[0], "first"))

    grads, deltas, new_m, new_v = [], [], [], []
    for n in names:
        shape = w_loc[n].shape
        if n in COL_SHARDED:
            view = unview = lambda a: jnp.swapaxes(a, -1, -2)
        else:
            view, unview = ((lambda a: a) if len(shape) >= 2 else two_d), (lambda a: a.reshape(shape))
        gn = g_shard[n].reshape(view(w_loc[n]).shape)
        outs = _adamw(view(w_loc[n]), gn, view(m_loc[n]), view(v_loc[n]), "adamw_" + n)
        for acc, a in zip((grads, deltas, new_m, new_v), (gn, *outs)):
            acc.append(unview(a))
    return (loss, grad_x, *grads, *deltas, *new_m, *new_v)
```

```python
import functools
from typing import Callable, NamedTuple, Optional

import jax
import jax.numpy as jnp
from jax import lax
from jax.experimental import pallas as pl
from jax.experimental.pallas import tpu as pltpu

F32 = jnp.float32
BF16 = jnp.bfloat16
SDS = jax.ShapeDtypeStruct
MESH = pl.DeviceIdType.MESH

D_MODEL = 1024
D_INNER = 2048
N_HEADS = 32
HEAD_P = 64
N_GROUPS = 4
HEADS_PER_GROUP = N_HEADS // N_GROUPS
D_STATE = 128
CONV_K = 4
CHUNK = 128
CONV_DIM = D_INNER + 2 * N_GROUPS * D_STATE
GROUP_W = D_INNER // N_GROUPS + 2 * D_STATE
ATT_HEADS = 12
ATT_D = 128
ATT_SLOTS = 4
ATT_W = ATT_SLOTS * ATT_D
ATT_DILATIONS = (1, 4, 16)
ATT_BLOCK = 128
QKV_DIM = 3 * ATT_HEADS * ATT_D
D_FF = 2816
DT_PAD = 128
ROPE_THETA = 10000.0
EPS = 1e-6
N_CHIPS = 4
LANES = 128

ADAM_LR = 0.001
ADAM_B1 = 0.9
ADAM_B2 = 0.999
ADAM_EPS = 1e-08
ADAM_WD = 0.01
ADAM_STEP = 10

VMEM_LIMIT = 48 * 1024 * 1024


def _cparams(semantics):
    return pltpu.CompilerParams(dimension_semantics=semantics, vmem_limit_bytes=VMEM_LIMIT)


def _pick(n, cap):
    best = None
    for t in range(LANES, min(n, cap) + 1, LANES):
        if n % t == 0:
            best = t
    return best or n


def _row_tile(rows, cap):
    best = None
    for t in range(8, min(rows, cap) + 1, 8):
        if rows % t == 0:
            best = t
    return best or rows


def _sigmoid(x):
    return pl.reciprocal(1.0 + jnp.exp(-x), approx=True)


def _softplus(x):
    return jnp.maximum(x, 0.0) + jnp.log(1.0 + jnp.exp(-jnp.abs(x)))


def _dot(a, b):
    return jnp.dot(a, b, preferred_element_type=F32)


def _dot_nt(a, b):
    return lax.dot_general(a, b, (((1,), (1,)), ((), ())), preferred_element_type=F32)


def _dot_tn(a, b):
    return lax.dot_general(a, b, (((0,), (0,)), ((), ())), preferred_element_type=F32)


def _mm(a, b, mode, out_dtype, name, add=None, side=None):
    if mode == "nn":
        (m, k), (_, n) = a.shape, b.shape
    elif mode == "nt":
        (m, k), (n, _) = a.shape, b.shape
    else:
        (k, m), (_, n) = a.shape, b.shape
    tm, tn = _pick(m, 1536), _pick(n, 2048)
    tk = k if k <= 2048 else _pick(k, 2048)
    nk = k // tk
    dims = {"nn": ((1,), (0,)), "nt": ((1,), (1,)), "tn": ((0,), (0,))}[mode]

    def partial_product(a_ref, b_ref):
        return lax.dot_general(a_ref[...].astype(BF16), b_ref[...].astype(BF16), (dims, ((), ())),
                               preferred_element_type=F32)

    def body(*refs):
        a_ref, b_ref = refs[:2]
        c_ref = refs[2] if add is not None else None
        o_ref = refs[3] if add is not None else refs[2]

        def finish(r):
            if add is not None:
                r = r + c_ref[...].astype(F32)
            o_ref[...] = r.astype(out_dtype)

        if nk == 1:
            finish(partial_product(a_ref, b_ref))
            return
        acc = refs[-1]
        kk = pl.program_id(2)

        @pl.when(kk == 0)
        def _():
            acc[...] = partial_product(a_ref, b_ref)

        @pl.when((kk > 0) & (kk < nk - 1))
        def _():
            acc[...] += partial_product(a_ref, b_ref)

        @pl.when(kk == nk - 1)
        def _():
            finish(acc[...] + partial_product(a_ref, b_ref))

    a_spec = {"nn": pl.BlockSpec((tm, tk), lambda j, i, q: (i, q)),
              "nt": pl.BlockSpec((tm, tk), lambda j, i, q: (i, q)),
              "tn": pl.BlockSpec((tk, tm), lambda j, i, q: (q, i))}[mode]
    b_spec = {"nn": pl.BlockSpec((tk, tn), lambda j, i, q: (q, j)),
              "nt": pl.BlockSpec((tn, tk), lambda j, i, q: (j, q)),
              "tn": pl.BlockSpec((tk, tn), lambda j, i, q: (q, j))}[mode]
    o_spec = pl.BlockSpec((tm, tn), lambda j, i, q: (i, j))
    ins, specs = [a, b], [a_spec, b_spec]
    if add is not None:
        ins.append(add)
        specs.append(o_spec)
    acc = [pltpu.VMEM((tm, tn), F32)] if nk > 1 else []
    grid = (n // tn, m // tm, nk)
    if side is None:
        return pl.pallas_call(
            body, name=name, grid=grid, in_specs=specs, out_specs=o_spec, out_shape=SDS((m, n), out_dtype),
            scratch_shapes=acc, compiler_params=_cparams(("parallel", "parallel", "arbitrary")))(*ins)
    return pl.pallas_call(
        _attach_side(body, len(ins), 1, side, grid), name=name, grid=grid,
        in_specs=specs + [ANY] * len(side.ins), out_specs=[o_spec] + [ANY] * len(side.out_shapes),
        out_shape=[SDS((m, n), out_dtype)] + list(side.out_shapes), scratch_shapes=acc + list(side.scratch),
        compiler_params=_cparams(("arbitrary", "arbitrary", "arbitrary")))(*ins, *side.ins)


def _mm_fused(a, b, mode, name, tm, epilogue, row_ins, full_ins, outs, n_acc=0, add=None):
    (m, k), n = a.shape, (b.shape[1] if mode == "nn" else b.shape[0])
    tk = k if k <= 2048 else _pick(k, 2048)
    nk = k // tk
    dims = {"nn": ((1,), (0,)), "nt": ((1,), (1,))}[mode]
    n_row, n_full, n_out = len(row_ins), len(full_ins), len(outs)

    def partial_product(a_ref, b_ref):
        return lax.dot_general(a_ref[...], b_ref[...], (dims, ((), ())), preferred_element_type=F32)

    def body(*refs):
        a_ref, b_ref = refs[:2]
        pos = 3 if add is not None else 2
        row_refs, full_refs = refs[pos:pos + n_row], refs[pos + n_row:pos + n_row + n_full]
        out_refs = refs[pos + n_row + n_full:pos + n_row + n_full + n_out]
        i, kk = pl.program_id(0), pl.program_id(1)

        def finish(r):
            if add is not None:
                r = r + refs[2][...].astype(F32)
            for q, (o_ref, v) in enumerate(zip(out_refs, epilogue(r, row_refs, full_refs))):
                if q < n_out - n_acc:
                    o_ref[...] = v.astype(o_ref.dtype)
                else:
                    @pl.when(i == 0)
                    def _(o_ref=o_ref, v=v):
                        o_ref[...] = v

                    @pl.when(i > 0)
                    def _(o_ref=o_ref, v=v):
                        o_ref[...] += v

        if nk == 1:
            finish(partial_product(a_ref, b_ref))
            return
        acc = refs[-1]

        @pl.when(kk == 0)
        def _():
            acc[...] = partial_product(a_ref, b_ref)

        @pl.when((kk > 0) & (kk < nk - 1))
        def _():
            acc[...] += partial_product(a_ref, b_ref)

        @pl.when(kk == nk - 1)
        def _():
            finish(acc[...] + partial_product(a_ref, b_ref))

    tile = lambda w, cb: pl.BlockSpec((tm, w), lambda i, q: (i, cb))
    b_spec = (pl.BlockSpec((tk, n), lambda i, q: (q, 0)) if mode == "nn" else pl.BlockSpec((n, tk), lambda i, q: (0, q)))
    specs = [pl.BlockSpec((tm, tk), lambda i, q: (i, q)), b_spec] + ([tile(n, 0)] if add is not None else [])
    specs += [tile(w, cb) for _, w, cb in row_ins]
    vec = lambda w, cb: pl.BlockSpec((1, w), lambda i, q: (0, cb))
    specs += [vec(w, cb) for _, w, cb in full_ins]
    out_specs = [tile(w, 0) for w, _ in outs[:n_out - n_acc]] + [vec(w, 0) for w, _ in outs[n_out - n_acc:]]
    out_shape = [SDS((m, w), dt) for w, dt in outs[:n_out - n_acc]] + [SDS((1, w), F32) for w, _ in outs[n_out - n_acc:]]
    ins = [a, b] + ([add] if add is not None else []) + [x for x, _, _ in row_ins] + [x for x, _, _ in full_ins]
    return pl.pallas_call(
        body, name=name, grid=(m // tm, nk), in_specs=specs, out_specs=out_specs, out_shape=out_shape,
        scratch_shapes=[pltpu.VMEM((tm, n), F32)] if nk > 1 else [],
        compiler_params=_cparams(("arbitrary", "arbitrary")))(*ins)


def _rw(name, fn, nsteps, ins, outs, n_acc=0):
    n_in, n_out = len(ins), len(outs)

    def body(*refs):
        i = pl.program_id(0)
        vals = fn(i, *refs[:n_in])
        for q, (r, v) in enumerate(zip(refs[n_in:], vals)):
            if q < n_out - n_acc:
                r[...] = v.astype(r.dtype)
            else:
                @pl.when(i == 0)
                def _(r=r):
                    r[...] = jnp.zeros_like(r)

                r[...] += v

    return pl.pallas_call(
        body, name=name, grid=(nsteps,), in_specs=[s for _, s in ins], out_specs=[s for _, s in outs],
        out_shape=[o for o, _ in outs], compiler_params=_cparams(("arbitrary",)))(*[a for a, _ in ins])


def _rs(tm, w, cb=0):
    return pl.BlockSpec((tm, w), lambda i: (i, cb))


def _fs(shape):
    nd = len(shape)
    return pl.BlockSpec(shape, lambda i: (0,) * nd)


def _colsum(v):
    return jnp.sum(v, axis=0, keepdims=True)


def _rms_fwd(x, g, name):
    t, d = x.shape
    tm = 512

    def fn(i, x_ref, g_ref):
        xv = x_ref[...]
        r = lax.rsqrt(jnp.mean(xv * xv, axis=-1, keepdims=True) + EPS)
        return [xv * r * g_ref[...]]

    return _rw(name, fn, t // tm, [(x, _rs(tm, d)), (g, _fs((1, d)))], [(SDS((t, d), BF16), _rs(tm, d))])[0]


def _rms_bwd_values(xv, dhv, gv, dres):
    r = lax.rsqrt(jnp.mean(xv * xv, axis=-1, keepdims=True) + EPS)
    xhat = xv * r
    dxhat = dhv * gv
    dx = dres + r * (dxhat - xhat * jnp.mean(dxhat * xhat, axis=-1, keepdims=True))
    return [dx, dx, _colsum(dhv * xhat)]


def _final_values(xv, target, gv):
    d = xv.shape[-1]
    r = lax.rsqrt(jnp.mean(xv * xv, axis=-1, keepdims=True) + EPS)
    xhat = xv * r
    diff = xhat * gv - target
    lsum = 0.5 * jnp.sum(jnp.sum(diff * diff, axis=-1, keepdims=True) * (1.0 / d), axis=0, keepdims=True)
    dy = diff * (1.0 / d)
    dxhat = dy * gv
    dx = r * (dxhat - xhat * jnp.mean(dxhat * xhat, axis=-1, keepdims=True))
    return [dx, dx, _colsum(dy * xhat), lsum]


CONV_TS = 512
CONV_HALO = 8


def _conv_specs(seq, c):
    ts, tc = CONV_TS, GROUP_W
    hb = ts // CONV_HALO
    u_spec = pl.BlockSpec((ts, tc), lambda j, i: (i, j))
    prev_spec = pl.BlockSpec((CONV_HALO, tc), lambda j, i: (jnp.maximum(i * hb - 1, 0), j))
    w_spec = pl.BlockSpec((CONV_K, tc), lambda j, i: (0, j))
    b_spec = pl.BlockSpec((1, tc), lambda j, i: (0, j))
    return u_spec, prev_spec, w_spec, b_spec


CONV_PIECE = 32


def _conv_fill(i, seq, u_ref, prev_ref, ext):
    first = (i % (seq // CONV_TS)) == 0
    ext[0:CONV_HALO, :] = jnp.where(first, 0.0, prev_ref[...])
    ext[CONV_HALO:, :] = u_ref[...]


def _conv_piece(ext, r0, wv, bv):
    lo = r0 + CONV_HALO - CONV_K + 1
    taps = [ext[lo + q:lo + q + CONV_PIECE, :] for q in range(CONV_K)]
    pre = bv
    for q, tap in enumerate(taps):
        pre = pre + wv[q:q + 1] * tap
    return taps, pre


def _conv_fwd(u, w, b, seq):
    t, c = u.shape
    ts, tc = CONV_TS, GROUP_W
    u_spec, prev_spec, w_spec, b_spec = _conv_specs(seq, c)

    def body(u_ref, prev_ref, w_ref, b_ref, o_ref, ext):
        _conv_fill(pl.program_id(1), seq, u_ref, prev_ref, ext)
        wv, bv = w_ref[...], b_ref[...]
        for r0 in range(0, ts, CONV_PIECE):
            _, pre = _conv_piece(ext, r0, wv, bv)
            o_ref[r0:r0 + CONV_PIECE, :] = pre * _sigmoid(pre)

    return pl.pallas_call(
        body, name="conv_fwd", grid=(c // tc, t // ts), in_specs=[u_spec, prev_spec, w_spec, b_spec],
        out_specs=u_spec, out_shape=SDS((t, c), F32), scratch_shapes=[pltpu.VMEM((ts + CONV_HALO, tc), F32)],
        compiler_params=_cparams(("parallel", "arbitrary")))(u, u, w, b)


def _conv_bwd_pre(u, w, b, dxc, seq):
    t, c = u.shape
    ts, tc = CONV_TS, GROUP_W
    u_spec, prev_spec, w_spec, b_spec = _conv_specs(seq, c)

    def body(u_ref, prev_ref, w_ref, b_ref, d_ref, dpre_ref, dw_ref, db_ref, ext):
        i = pl.program_id(1)
        _conv_fill(i, seq, u_ref, prev_ref, ext)
        wv, bv = w_ref[...], b_ref[...]
        fold = lambda v: sum(v[8 * s:8 * (s + 1)] for s in range(CONV_PIECE // 8))
        sums = [jnp.zeros((8, tc), F32)] * (CONV_K + 1)
        for r0 in range(0, ts, CONV_PIECE):
            taps, pre = _conv_piece(ext, r0, wv, bv)
            sg = _sigmoid(pre)
            dpre = d_ref[r0:r0 + CONV_PIECE, :] * sg * (1.0 + pre * (1.0 - sg))
            dpre_ref[r0:r0 + CONV_PIECE, :] = dpre
            sums = [s + fold(dpre * f) for s, f in zip(sums, taps + [1.0])]

        @pl.when(i == 0)
        def _():
            dw_ref[...] = jnp.zeros_like(dw_ref)
            db_ref[...] = jnp.zeros_like(db_ref)

        db_ref[...] += _colsum(sums[CONV_K])
        for q in range(CONV_K):
            dw_ref[q:q + 1, :] += _colsum(sums[q])

    return pl.pallas_call(
        body, name="conv_bwd_pre", grid=(c // tc, t // ts),
        in_specs=[u_spec, prev_spec, w_spec, b_spec, u_spec], out_specs=[u_spec, w_spec, b_spec],
        out_shape=[SDS((t, c), F32), SDS((CONV_K, c), F32), SDS((1, c), F32)],
        scratch_shapes=[pltpu.VMEM((ts + CONV_HALO, tc), F32)],
        compiler_params=_cparams(("parallel", "arbitrary")))(u, u, w, b, dxc)


def _conv_bwd_in(dpre, w, seq):
    t, c = dpre.shape
    ts, tc = CONV_TS, GROUP_W
    hb = ts // CONV_HALO
    last = t // CONV_HALO - 1
    d_spec = pl.BlockSpec((ts, tc), lambda j, i: (i, j))
    next_spec = pl.BlockSpec((CONV_HALO, tc), lambda j, i: (jnp.minimum((i + 1) * hb, last), j))
    w_spec = pl.BlockSpec((CONV_K, tc), lambda j, i: (0, j))

    def body(d_ref, next_ref, w_ref, o_ref, ext):
        i = pl.program_id(1)
        nts = seq // ts
        is_last = (i % nts) == nts - 1
        ext[0:ts, :] = d_ref[...]
        ext[ts:, :] = jnp.where(is_last, 0.0, next_ref[...])
        wv = w_ref[...]
        for r0 in range(0, ts, CONV_PIECE):
            acc = wv[CONV_K - 1:CONV_K] * ext[r0:r0 + CONV_PIECE, :]
            for q in range(CONV_K - 1):
                lo = r0 + CONV_K - 1 - q
                acc = acc + wv[q:q + 1] * ext[lo:lo + CONV_PIECE, :]
            o_ref[r0:r0 + CONV_PIECE, :] = acc.astype(o_ref.dtype)

    return pl.pallas_call(
        body, name="conv_bwd_in", grid=(c // tc, t // ts), in_specs=[d_spec, next_spec, w_spec],
        out_specs=d_spec, out_shape=SDS((t, c), BF16), scratch_shapes=[pltpu.VMEM((ts + CONV_HALO, tc), F32)],
        compiler_params=_cparams(("parallel", "arbitrary")))(dpre, dpre, w)


def _split3(v):
    hi = v.astype(BF16)
    r1 = v - hi.astype(F32)
    mid = r1.astype(BF16)
    lo = (r1 - mid.astype(F32)).astype(BF16)
    return hi, mid, lo


def _ssd_prelude(dtr_ref, dtrt_ref, bias_ref, biast_ref, a_ref, at_ref):
    dt = _softplus(dtr_ref[...] + bias_ref[...])
    dtt = _softplus(dtrt_ref[...] + biast_ref[...])
    ri = lax.broadcasted_iota(jnp.int32, (CHUNK, CHUNK), 0)
    ci = lax.broadcasted_iota(jnp.int32, (CHUNK, CHUNK), 1)
    lower = ri >= ci
    upper = ri <= ci
    lower_b = jnp.where(lower, 1.0, 0.0).astype(BF16)
    upper_b = jnp.where(upper, 1.0, 0.0).astype(BF16)
    acs = sum(_dot(lower_b, p) for p in _split3(dt * a_ref[...]))
    acst = sum(_dot(p, upper_b) for p in _split3(dtt * at_ref[...]))
    return dt, acs, acst, lower, upper, lower_b, upper_b


SSD_FWD_GPS = 2
SSD_BWD_GPS = 1


def _ssd_specs(seq, gps):
    nc = seq // CHUNK
    hg = HEADS_PER_GROUP
    fwd = lambda c: c
    rev = lambda c: nc - 1 - c

    def specs(cc):
        return dict(
            xc=pl.BlockSpec((CHUNK, gps * GROUP_W), lambda g, b, c: (b * nc + cc(c), g)),
            y=pl.BlockSpec((CHUNK, gps * hg * HEAD_P), lambda g, b, c: (b * nc + cc(c), g)),
            dtr=pl.BlockSpec((gps, CHUNK, hg), lambda g, b, c: (g, b * nc + cc(c), 0)),
            dtrt=pl.BlockSpec((gps, None, hg, CHUNK), lambda g, b, c: (g, b, 0, cc(c))),
            prow=pl.BlockSpec((gps, 1, hg), lambda g, b, c: (g, 0, 0)),
            pcol=pl.BlockSpec((gps, hg, 1), lambda g, b, c: (g, 0, 0)),
            st=pl.BlockSpec((gps, None, None, D_STATE, hg * HEAD_P), lambda g, b, c: (g, b, cc(c), 0, 0)),
        )

    return specs(fwd), specs(rev)


def _group_views(refs, lane_widths, gi):
    return [r.at[:, gi * w:(gi + 1) * w] if w else r.at[gi] for r, w in zip(refs, lane_widths)]


def _head_maps():
    hw = HEADS_PER_GROUP * HEAD_P
    shift = HEAD_P.bit_length() - 1
    hj = lax.broadcasted_iota(jnp.int32, (HEADS_PER_GROUP, hw), 0)
    lq = jnp.right_shift(lax.broadcasted_iota(jnp.int32, (HEADS_PER_GROUP, hw), 1), shift)
    spread = jnp.where(hj == lq, 1.0, 0.0).astype(BF16)
    rq = jnp.right_shift(lax.broadcasted_iota(jnp.int32, (hw, LANES), 0), shift)
    cj = lax.broadcasted_iota(jnp.int32, (hw, LANES), 1)
    gather = jnp.where(rq == cj, 1.0, 0.0).astype(BF16)
    return spread, gather


def _dot01(v, m01):
    hi, mid, _ = _split3(v)
    return _dot(hi, m01) + _dot(mid, m01)


class _Side(NamedTuple):
    ins: tuple
    out_shapes: tuple
    scratch: tuple
    first: Callable
    mid: Optional[Callable]
    last: Callable


NO_SIDE = _Side((), (), (), lambda *refs: None, None, lambda *refs: None)


def _attach_side(body, n_in, n_out, side, grid):
    si, so, ss = len(side.ins), len(side.out_shapes), len(side.scratch)

    def wrapped(*refs):
        ins, s_in = refs[:n_in], refs[n_in:n_in + si]
        outs = refs[n_in + si:n_in + si + n_out]
        s_out = refs[n_in + si + n_out:n_in + si + n_out + so]
        rest = refs[n_in + si + n_out + so:]
        scr, s_scr = rest[:len(rest) - ss], rest[len(rest) - ss:]
        ids = [pl.program_id(a) for a in range(len(grid))]
        inner_first = functools.reduce(lambda p, q: p & q, [i == 0 for i in ids[1:]], ids[0] >= 0)
        at_last = functools.reduce(lambda p, q: p & q, [i == n - 1 for i, n in zip(ids, grid)])

        @pl.when((ids[0] == 0) & inner_first)
        def _():
            side.first(s_in, s_out, s_scr)

        if side.mid is not None:
            outer_last = functools.reduce(lambda p, q: p & q, [i == n - 1 for i, n in zip(ids[:-1], grid[:-1])])

            @pl.when(outer_last & (ids[-1] == 0))
            def _():
                side.mid(s_in, s_out, s_scr)

        body(*ins, *outs, *scr)

        @pl.when(at_last)
        def _():
            side.last(s_in, s_out, s_scr)

    return wrapped


def _ssd_fwd(xc, dtr, dtrt, bias, biast, a, at, dskip, nb, seq, side):
    t = xc.shape[0]
    nc = seq // CHUNK
    hg = HEADS_PER_GROUP
    hw = hg * HEAD_P
    gps = SSD_FWD_GPS
    grid = (N_GROUPS // gps, nb, nc)
    sp, _ = _ssd_specs(seq, gps)

    def body(*refs):
        for gi in range(gps):
            one_group(*_group_views(refs, (GROUP_W, 0, 0, 0, 0, 0, 0, 0, hw, 0, 0), gi))

    def one_group(xc_ref, dtr_ref, dtrt_ref, bias_ref, biast_ref, a_ref, at_ref, d_ref, y_ref, sin_ref, st):
        @pl.when(pl.program_id(2) == 0)
        def _():
            st[...] = jnp.zeros_like(st)

        s_in = st[...]
        sin_ref[...] = s_in
        dt, acs, acst, lower, _, _, _ = _ssd_prelude(dtr_ref, dtrt_ref, bias_ref, biast_ref, a_ref, at_ref)
        spread, _ = _head_maps()
        x = xc_ref[...]
        xs = x[:, :hw]
        b16 = x[:, hw:hw + D_STATE].astype(BF16)
        c16 = x[:, hw + D_STATE:].astype(BF16)
        cb = _dot_nt(c16, b16)
        last = acs[CHUNK - 1:CHUNK, :]
        e_x = _dot01(jnp.exp(acs), spread)
        dec_x = _dot01(jnp.exp(last - acs), spread)
        tot_x = e_x[CHUNK - 1:CHUNK, :]
        d_x = _dot01(jnp.broadcast_to(d_ref[...], (8, hg)), spread)[0:1, :]
        xdtf = xs * _dot01(dt, spread)
        xdt16 = xdtf.astype(BF16)
        yoff = e_x * _dot(c16, s_in.astype(BF16))
        st[...] = tot_x * s_in + _dot_tn(b16, (dec_x * xdtf).astype(BF16))
        parts = []
        for j in range(hg):
            decay = jnp.exp(jnp.where(lower, acs[:, j:j + 1] - acst[j:j + 1, :], -jnp.inf))
            parts.append(_dot((cb * decay).astype(BF16), xdt16[:, HEAD_P * j:HEAD_P * (j + 1)]))
        y_ref[...] = (jnp.concatenate(parts, axis=-1) + yoff + d_x * xs).astype(y_ref.dtype)

    return pl.pallas_call(
        _attach_side(body, 8, 2, side, grid), name="ssd_fwd", grid=grid,
        in_specs=[sp["xc"], sp["dtr"], sp["dtrt"], sp["prow"], sp["pcol"], sp["prow"], sp["pcol"], sp["prow"]]
        + [ANY] * len(side.ins),
        out_specs=[sp["y"], sp["st"]] + [ANY] * len(side.out_shapes),
        out_shape=[SDS((t, D_INNER), BF16), SDS((N_GROUPS, nb, nc, D_STATE, hw), F32)] + list(side.out_shapes),
        scratch_shapes=[pltpu.VMEM((gps, D_STATE, hw), F32)] + list(side.scratch),
        compiler_params=_cparams(("arbitrary", "arbitrary", "arbitrary")))(
            xc, dtr, dtrt, bias, biast, a, at, dskip, *side.ins)


def _ssd_bwd(xc, dtr, dtrt, bias, biast, a, at, dskip, states, dy, nb, seq, side):
    t = xc.shape[0]
    nc = seq // CHUNK
    hg = HEADS_PER_GROUP
    hw = hg * HEAD_P
    gps = SSD_BWD_GPS
    grid = (N_GROUPS // gps, nb, nc)
    _, sp = _ssd_specs(seq, gps)

    def body(*refs):
        for gi in range(gps):
            one_group(*_group_views(refs, (GROUP_W, 0, 0, 0, 0, 0, 0, 0, 0, hw, GROUP_W, 0, 0, 0, 0, 0), gi))

    def one_group(xc_ref, dtr_ref, dtrt_ref, bias_ref, biast_ref, a_ref, at_ref, d_ref, sin_ref, dy_ref,
                  dxc_ref, ddtr_ref, gbias_ref, ga_ref, gd_ref, ds):
        first = (pl.program_id(1) == 0) & (pl.program_id(2) == 0)

        @pl.when(pl.program_id(2) == 0)
        def _():
            ds[...] = jnp.zeros_like(ds)

        @pl.when(first)
        def _():
            gbias_ref[...] = jnp.zeros_like(gbias_ref)
            ga_ref[...] = jnp.zeros_like(ga_ref)
            gd_ref[...] = jnp.zeros_like(gd_ref)

        dt, acs, acst, lower, upper, _, upper_b = _ssd_prelude(dtr_ref, dtrt_ref, bias_ref, biast_ref, a_ref, at_ref)
        spread, gather = _head_maps()
        x = xc_ref[...]
        dy = dy_ref[...].astype(F32)
        xs = x[:, :hw]
        b16 = x[:, hw:hw + D_STATE].astype(BF16)
        c16 = x[:, hw + D_STATE:].astype(BF16)
        dy16 = dy.astype(BF16)
        cb = _dot_nt(c16, b16)
        cbt = _dot_nt(b16, c16)
        last = acs[CHUNK - 1:CHUNK, :]
        e8 = jnp.exp(acs)
        dec8 = jnp.exp(last - acs)
        e_x = _dot01(e8, spread)
        dec_x = _dot01(dec8, spread)
        tot_x = e_x[CHUNK - 1:CHUNK, :]
        dt_x = _dot01(dt, spread)
        d_x = _dot01(jnp.broadcast_to(d_ref[...], (8, hg)), spread)[0:1, :]
        xdtf = xs * dt_x
        xdt16 = xdtf.astype(BF16)
        s_in = sin_ref[...]
        s16 = s_in.astype(BF16)
        ds_out = ds[...]
        ds16 = ds_out.astype(BF16)
        bds = _dot(b16, ds16)
        cs = _dot(c16, s16)
        edy16 = (e_x * dy).astype(BF16)
        ds[...] = tot_x * ds_out + _dot_tn(c16, edy16)
        lane8 = lax.broadcasted_iota(jnp.int32, (CHUNK, hg), 1)
        row8 = lax.broadcasted_iota(jnp.int32, (CHUNK, hg), 0)
        dacs8 = jnp.zeros((CHUNK, hg), F32)
        acc_m = jnp.zeros((CHUNK, CHUNK), F32)
        acc_mt = jnp.zeros((CHUNK, CHUNK), F32)
        dx_parts = []
        for j in range(hg):
            sl = slice(HEAD_P * j, HEAD_P * (j + 1))
            col = acs[:, j:j + 1]
            row = acst[j:j + 1, :]
            decay = jnp.exp(jnp.where(lower, col - row, -jnp.inf))
            decayt = jnp.exp(jnp.where(upper, row - col, -jnp.inf))
            wm = _dot_nt(dy16[:, sl], xdt16[:, sl]) * decay
            wmt = _dot_nt(xdt16[:, sl], dy16[:, sl]) * decayt
            acc_m = acc_m + wm
            acc_mt = acc_mt + wmt
            dacs8 = dacs8 + jnp.where(lane8 == j, jnp.sum(wm * cb, axis=-1, keepdims=True)
                                      - jnp.sum(wmt * cbt, axis=-1, keepdims=True), 0.0)
            dx_parts.append(_dot((cbt * decayt).astype(BF16), dy16[:, sl]))
        dx = jnp.concatenate(dx_parts, axis=-1) + dec_x * bds
        dxc_ref[:, :hw] = dx * dt_x + d_x * dy
        dxc_ref[:, hw:hw + D_STATE] = _dot(acc_mt.astype(BF16), c16) + _dot_nt((dec_x * xdtf).astype(BF16), ds16)
        dxc_ref[:, hw + D_STATE:] = _dot(acc_m.astype(BF16), b16) + _dot_nt(edy16, s16)
        dtot_rows = jnp.broadcast_to(_colsum(ds_out * s_in), (8, hw))
        sums = _dot01(jnp.concatenate([dy * cs, xdtf * bds, dx * xs, dy * xs, dtot_rows], axis=0), gather)
        de8 = sums[0:CHUNK, :hg]
        ddec8 = sums[CHUNK:2 * CHUNK, :hg]
        ddtx8 = sums[2 * CHUNK:3 * CHUNK, :hg]
        gd8 = _colsum(sums[3 * CHUNK:4 * CHUNK, :hg])
        dtot8 = sums[4 * CHUNK:4 * CHUNK + 1, :hg]
        extra = _colsum(ddec8 * dec8) + dtot8 * e8[CHUNK - 1:CHUNK, :]
        dacs8 = dacs8 + de8 * e8 - ddec8 * dec8 + jnp.where(row8 == CHUNK - 1, extra, 0.0)
        da = sum(_dot(upper_b, p) for p in _split3(dacs8))
        av = a_ref[...]
        ddt = da * av + ddtx8
        ddtr = ddt * _sigmoid(dtr_ref[...] + bias_ref[...])
        ddtr_ref[...] = ddtr
        gbias_ref[...] += _colsum(ddtr)
        ga_ref[...] += _colsum(da * dt) * av
        gd_ref[...] += gd8

    return pl.pallas_call(
        _attach_side(body, 10, 5, side, grid), name="ssd_bwd", grid=grid,
        in_specs=[sp["xc"], sp["dtr"], sp["dtrt"], sp["prow"], sp["pcol"], sp["prow"], sp["pcol"], sp["prow"],
                  sp["st"], sp["y"]] + [ANY] * len(side.ins),
        out_specs=[sp["xc"], sp["dtr"], sp["prow"], sp["prow"], sp["prow"]] + [ANY] * len(side.out_shapes),
        out_shape=[SDS((t, N_GROUPS * GROUP_W), F32), SDS((N_GROUPS, t, hg), F32)]
        + [SDS((N_GROUPS, 1, hg), F32)] * 3 + list(side.out_shapes),
        scratch_shapes=[pltpu.VMEM((gps, D_STATE, hw), F32)] + list(side.scratch),
        compiler_params=_cparams(("arbitrary", "arbitrary", "arbitrary")))(
            xc, dtr, dtrt, bias, biast, a, at, dskip, states, dy, *side.ins)


def _group_bcast(v, width, fn):
    parts = []
    for q in range(v.shape[-1] // width):
        s = fn(v[:, q * width:(q + 1) * width])
        parts.append(jnp.broadcast_to(s, (v.shape[0], width)))
    return jnp.concatenate(parts, axis=-1)


def _gate_norm_out(y, z, g, w):
    t, d = y.shape
    n = w.shape[1]
    tm = 256
    gw = d // N_GROUPS

    def body(y_ref, z_ref, g_ref, w_ref, yn_ref, o_ref):
        zv = z_ref[...].astype(F32)
        u = y_ref[...].astype(F32) * (zv * _sigmoid(zv))
        r = lax.rsqrt(_group_bcast(u * u, gw, lambda p: jnp.mean(p, axis=-1, keepdims=True)) + EPS)
        yn = (u * r * g_ref[...]).astype(yn_ref.dtype)
        yn_ref[...] = yn
        o_ref[...] = _dot(yn, w_ref[...]).astype(o_ref.dtype)

    return pl.pallas_call(
        body, name="gate_norm_ssm_out", grid=(t // tm,),
        in_specs=[_rs(tm, d), _rs(tm, d), _fs((1, d)), _fs((d, n))], out_specs=[_rs(tm, d), _rs(tm, n)],
        out_shape=[SDS((t, d), BF16), SDS((t, n), BF16)], compiler_params=_cparams(("arbitrary",)))(y, z, g, w)


def _gate_norm_bwd_epilogue(dv, rows, fulls):
    yv, zv = rows[0][...].astype(F32), rows[1][...].astype(F32)
    gw = yv.shape[-1] // N_GROUPS
    sg = _sigmoid(zv)
    sz = zv * sg
    u = yv * sz
    r = lax.rsqrt(_group_bcast(u * u, gw, lambda p: jnp.mean(p, axis=-1, keepdims=True)) + EPS)
    uhat = u * r
    duhat = dv * fulls[0][...]
    du = r * (duhat - uhat * _group_bcast(duhat * uhat, gw, lambda p: jnp.mean(p, axis=-1, keepdims=True)))
    dz = du * yv * sg * (1.0 + zv * (1.0 - sg))
    return [du * sz, dz, _colsum(dv * uhat)]


def _rope_tables(seq):
    half = ATT_D // 2
    inv = ROPE_THETA ** (-jnp.arange(half, dtype=F32) / half)
    ang = jnp.arange(seq, dtype=F32)[:, None] * inv[None, :]
    cos, sin = jnp.cos(ang), jnp.sin(ang)
    return jnp.concatenate([cos, cos], axis=-1), jnp.concatenate([-sin, sin], axis=-1)


ATT_TILE = 512
ATT_QB = 8


def _strided_spec(r, mtiles):
    return pl.BlockSpec((None, r, None, ATT_TILE // r, ATT_W), lambda i: (i // mtiles, 0, i % mtiles, 0, 0))


def _strided_shape(nb, r, mtiles, dtype):
    return SDS((nb, r, mtiles, ATT_TILE // r, ATT_W), dtype)


def _to_strided(val, out_ref, lanes, r, sc):
    if r == 1:
        out_ref[0, :, lanes] = val.astype(out_ref.dtype)
        return
    sc[...] = val
    for rr in range(r):
        out_ref[rr, :, lanes] = sc[pl.ds(rr, ATT_TILE // r, stride=r), :].astype(out_ref.dtype)


def _from_strided(in_ref, lanes, r, sc):
    if r == 1:
        return in_ref[0, :, lanes].astype(F32)
    for rr in range(r):
        sc[pl.ds(rr, ATT_TILE // r, stride=r), :] = in_ref[rr, :, lanes].astype(F32)
    return sc[...]


def _rope_fwd(qkv, cos, sin, nb, seq):
    t = qkv.shape[0]
    tm = ATT_TILE
    mtiles = seq // tm
    w = ATT_HEADS * ATT_D
    tab = pl.BlockSpec((tm, ATT_D), lambda i: (i % mtiles, 0))
    ng = len(ATT_DILATIONS)

    def body(q_ref, k_ref, v_ref, cos_ref, sin_ref, *rest):
        outs, sc = rest[:3 * ng], rest[3 * ng]
        c, s = cos_ref[...], sin_ref[...]
        for which, ref in enumerate((q_ref, k_ref, v_ref)):
            for h in range(ATT_HEADS):
                g, slot = divmod(h, ATT_SLOTS)
                p = ref[:, h * ATT_D:(h + 1) * ATT_D].astype(F32)
                if which < 2:
                    p = p * c + pltpu.roll(p, ATT_D // 2, 1) * s
                _to_strided(p, outs[which * ng + g], slice(slot * ATT_D, (slot + 1) * ATT_D), ATT_DILATIONS[g], sc)

    out_specs = [_strided_spec(r, mtiles) for _ in range(3) for r in ATT_DILATIONS]
    out_shape = [_strided_shape(nb, r, mtiles, BF16) for _ in range(3) for r in ATT_DILATIONS]
    outs = pl.pallas_call(
        body, name="rope_fwd", grid=(t // tm,),
        in_specs=[_rs(tm, w, 0), _rs(tm, w, 1), _rs(tm, w, 2), tab, tab], out_specs=out_specs, out_shape=out_shape,
        scratch_shapes=[pltpu.VMEM((tm, ATT_D), F32)], compiler_params=_cparams(("arbitrary",)))(
            qkv, qkv, qkv, cos, sin)
    flat = [o.reshape(t, ATT_W) for o in outs]
    return flat[0:ng], flat[ng:2 * ng], flat[2 * ng:]


def _rope_bwd(dq, dk, dv, cos, sin, nb, seq):
    t = dq[0].shape[0]
    tm = ATT_TILE
    mtiles = seq // tm
    w = ATT_HEADS * ATT_D
    tab = pl.BlockSpec((tm, ATT_D), lambda i: (i % mtiles, 0))
    ng = len(ATT_DILATIONS)

    def body(*refs):
        ins, (cos_ref, sin_ref, o_ref, sc) = refs[:3 * ng], refs[3 * ng:]
        c, s = cos_ref[...], sin_ref[...]
        for which in range(3):
            for h in range(ATT_HEADS):
                g, slot = divmod(h, ATT_SLOTS)
                p = _from_strided(ins[which * ng + g], slice(slot * ATT_D, (slot + 1) * ATT_D), ATT_DILATIONS[g], sc)
                if which < 2:
                    p = p * c - pltpu.roll(p, ATT_D // 2, 1) * s
                o_ref[:, which * w + h * ATT_D:which * w + (h + 1) * ATT_D] = p.astype(o_ref.dtype)

    views = [a.reshape(nb, r, mtiles, tm // r, ATT_W) for grp in (dq, dk, dv) for a, r in zip(grp, ATT_DILATIONS)]
    return pl.pallas_call(
        body, name="rope_bwd", grid=(t // tm,),
        in_specs=[_strided_spec(r, mtiles) for _ in range(3) for r in ATT_DILATIONS] + [tab, tab],
        out_specs=_rs(tm, 3 * w), out_shape=SDS((t, 3 * w), BF16),
        scratch_shapes=[pltpu.VMEM((tm, ATT_D), F32)], compiler_params=_cparams(("arbitrary",)))(*views, cos, sin)


def _att_masks():
    ri = lax.broadcasted_iota(jnp.int32, (ATT_BLOCK, ATT_BLOCK), 0)
    ci = lax.broadcasted_iota(jnp.int32, (ATT_BLOCK, ATT_BLOCK), 1)
    return ci <= ri, ci >= ri


def _att_fwd(q, k, v, g, seq):
    t, w = q.shape
    rows = ATT_QB * ATT_BLOCK
    nbs = seq // ATT_DILATIONS[g] // ATT_BLOCK
    scale = ATT_D ** -0.5
    cur = pl.BlockSpec((rows, w), lambda n: (n, 0))
    prev = pl.BlockSpec((ATT_BLOCK, w), lambda n: (jnp.maximum(n * ATT_QB - 1, 0), 0))

    def body(q_ref, kc_ref, kp_ref, vc_ref, vp_ref, o_ref, lse_ref):
        mcur, mprev = _att_masks()
        for i in range(ATT_QB):
            blk = pl.program_id(0) * ATT_QB + i
            mask = jnp.concatenate([mprev & ((blk % nbs) != 0), mcur], axis=-1)
            own = slice(i * ATT_BLOCK, (i + 1) * ATT_BLOCK)
            for h in range(ATT_SLOTS):
                sl = slice(h * ATT_D, (h + 1) * ATT_D)
                if i == 0:
                    keys = jnp.concatenate([kp_ref[:, sl], kc_ref[own, sl]], axis=0)
                    vals = jnp.concatenate([vp_ref[:, sl], vc_ref[own, sl]], axis=0)
                else:
                    both = slice((i - 1) * ATT_BLOCK, (i + 1) * ATT_BLOCK)
                    keys, vals = kc_ref[both, sl], vc_ref[both, sl]
                s = jnp.where(mask, _dot_nt(q_ref[own, sl], keys) * scale, -jnp.inf)
                m = jnp.max(s, axis=-1, keepdims=True)
                p = jnp.exp(s - m)
                den = jnp.sum(p, axis=-1, keepdims=True)
                o_ref[own, sl] = _dot(p.astype(BF16), vals) / den
                lse_ref[own, sl] = jnp.broadcast_to(m + jnp.log(den), (ATT_BLOCK, ATT_D))

    return pl.pallas_call(
        body, name=f"att_fwd_{g}", grid=(t // rows,), in_specs=[cur, cur, prev, cur, prev], out_specs=[cur, cur],
        out_shape=[SDS((t, w), F32), SDS((t, w), F32)],
        compiler_params=_cparams(("arbitrary",)))(q, k, k, v, v)


def _att_bwd(q, k, v, do, lse, dlt, g, seq):
    t, w = q.shape
    nblk = t // ATT_BLOCK
    rows = ATT_QB * ATT_BLOCK
    nbs = seq // ATT_DILATIONS[g] // ATT_BLOCK
    scale = ATT_D ** -0.5
    cur = pl.BlockSpec((rows, w), lambda n: (n, 0))
    nxt = pl.BlockSpec((ATT_BLOCK, w), lambda n: (jnp.minimum((n + 1) * ATT_QB, nblk - 1), 0))

    def body(qc_ref, qn_ref, k_ref, v_ref, doc_ref, don_ref, lsec_ref, lsen_ref, dltc_ref, dltn_ref,
             dq_ref, dk_ref, dv_ref, carry):
        n = pl.program_id(0)

        @pl.when(n == 0)
        def _():
            carry[...] = jnp.zeros_like(carry)

        mcur, mprev = _att_masks()

        def pair(cur_ref, nxt_ref, i, sl):
            if i + 1 < ATT_QB:
                return cur_ref[i * ATT_BLOCK:(i + 2) * ATT_BLOCK, sl]
            return jnp.concatenate([cur_ref[i * ATT_BLOCK:, sl], nxt_ref[:, sl]], axis=0)

        for h in range(ATT_SLOTS):
            sl = slice(h * ATT_D, (h + 1) * ATT_D)
            from_prev = carry[:, sl]
            for i in range(ATT_QB):
                blk = n * ATT_QB + i
                has_next = (((blk + 1) % nbs) != 0) & (blk + 1 < nblk)
                mask = jnp.concatenate([mcur, mprev & has_next], axis=0)
                own = slice(i * ATT_BLOCK, (i + 1) * ATT_BLOCK)
                kh, vh = k_ref[own, sl], v_ref[own, sl]
                qs, dos = pair(qc_ref, qn_ref, i, sl), pair(doc_ref, don_ref, i, sl)
                lse, dlt = pair(lsec_ref, lsen_ref, i, sl), pair(dltc_ref, dltn_ref, i, sl)
                p = jnp.where(mask, jnp.exp(_dot_nt(qs, kh) * scale - lse), 0.0)
                ds = (p * (_dot_nt(dos, vh) - dlt) * scale).astype(BF16)
                dqs = _dot(ds, kh)
                dq_ref[own, sl] = (from_prev + dqs[:ATT_BLOCK]).astype(dq_ref.dtype)
                from_prev = dqs[ATT_BLOCK:]
                dk_ref[own, sl] = _dot_tn(ds, qs).astype(dk_ref.dtype)
                dv_ref[own, sl] = _dot_tn(p.astype(BF16), dos).astype(dv_ref.dtype)
            carry[:, sl] = from_prev

    return pl.pallas_call(
        body, name=f"att_bwd_{g}", grid=(t // rows,), in_specs=[cur, nxt, cur, cur, cur, nxt, cur, nxt, cur, nxt],
        out_specs=[cur, cur, cur], out_shape=[SDS((t, w), BF16)] * 3,
        scratch_shapes=[pltpu.VMEM((ATT_BLOCK, w), F32)],
        compiler_params=_cparams(("arbitrary",)))(q, q, k, v, do, do, lse, lse, dlt, dlt)


def _merge_weights(ls):
    m = jnp.maximum(jnp.maximum(ls[0], ls[1]), ls[2])
    es = [jnp.exp(v - m) for v in ls]
    den = es[0] + es[1] + es[2]
    return [e / den for e in es]


def _merge_fwd(o, lse, nb, seq):
    t = o[0].shape[0]
    tm = ATT_TILE
    mtiles = seq // tm
    ng = len(ATT_DILATIONS)

    def body(*refs):
        o_refs, l_refs, out_ref, scs = refs[:ng], refs[ng:2 * ng], refs[2 * ng], refs[2 * ng + 1:]
        for slot in range(ATT_SLOTS):
            lanes = slice(slot * ATT_D, (slot + 1) * ATT_D)
            ov = [_from_strided(o_refs[g], lanes, r, scs[2 * g]) for g, r in enumerate(ATT_DILATIONS)]
            ws = _merge_weights([_from_strided(l_refs[g], lanes, r, scs[2 * g + 1])
                                 for g, r in enumerate(ATT_DILATIONS)])
            out_ref[:, lanes] = (ws[0] * ov[0] + ws[1] * ov[1] + ws[2] * ov[2]).astype(out_ref.dtype)

    views = [a.reshape(nb, r, mtiles, tm // r, ATT_W) for grp in (o, lse) for a, r in zip(grp, ATT_DILATIONS)]
    return pl.pallas_call(
        body, name="att_merge_fwd", grid=(t // tm,),
        in_specs=[_strided_spec(r, mtiles) for _ in range(2) for r in ATT_DILATIONS],
        out_specs=_rs(tm, ATT_W), out_shape=SDS((t, ATT_W), BF16),
        scratch_shapes=[pltpu.VMEM((tm, ATT_D), F32)] * (2 * ng), compiler_params=_cparams(("arbitrary",)))(*views)


def _merge_bwd(o, lse, datt, nb, seq):
    t = o[0].shape[0]
    tm = ATT_TILE
    mtiles = seq // tm
    ng = len(ATT_DILATIONS)

    def body(*refs):
        o_refs, l_refs, d_ref = refs[:ng], refs[ng:2 * ng], refs[2 * ng]
        do_refs, dlt_refs = refs[2 * ng + 1:3 * ng + 1], refs[3 * ng + 1:4 * ng + 1]
        scs = refs[4 * ng + 1:]
        for slot in range(ATT_SLOTS):
            lanes = slice(slot * ATT_D, (slot + 1) * ATT_D)
            ov = [_from_strided(o_refs[g], lanes, r, scs[2 * g]) for g, r in enumerate(ATT_DILATIONS)]
            ws = _merge_weights([_from_strided(l_refs[g], lanes, r, scs[2 * g + 1])
                                 for g, r in enumerate(ATT_DILATIONS)])
            dv = d_ref[:, lanes]
            att = ws[0] * ov[0] + ws[1] * ov[1] + ws[2] * ov[2]
            dot = jnp.broadcast_to(jnp.sum(dv * att, axis=-1, keepdims=True), (tm, ATT_D))
            for g, r in enumerate(ATT_DILATIONS):
                _to_strided(ws[g] * dv, do_refs[g], lanes, r, scs[2 * ng])
                _to_strided(ws[g] * dot, dlt_refs[g], lanes, r, scs[2 * ng + 1])

    views = [a.reshape(nb, r, mtiles, tm // r, ATT_W) for grp in (o, lse) for a, r in zip(grp, ATT_DILATIONS)]
    outs = pl.pallas_call(
        body, name="att_merge_bwd", grid=(t // tm,),
        in_specs=[_strided_spec(r, mtiles) for _ in range(2) for r in ATT_DILATIONS] + [_rs(tm, ATT_W)],
        out_specs=[_strided_spec(r, mtiles) for _ in range(2) for r in ATT_DILATIONS],
        out_shape=[_strided_shape(nb, r, mtiles, dt) for dt in (BF16, F32) for r in ATT_DILATIONS],
        scratch_shapes=[pltpu.VMEM((tm, ATT_D), F32)] * (2 * ng + 2), compiler_params=_cparams(("arbitrary",)))(
            *views, datt)
    flat = [a.reshape(t, ATT_W) for a in outs]
    return flat[:ng], flat[ng:]


def _branch_gates(rows, fulls):
    return (_sigmoid(rows[0][...].astype(F32) + fulls[0][...]), _sigmoid(rows[1][...].astype(F32) + fulls[1][...]))


def _mix_fwd_epilogue(y_att, rows, fulls):
    g0, g1 = _branch_gates(rows, fulls)
    return [y_att, g0 * rows[2][...].astype(F32) + g1 * y_att]


def _mix_bwd_epilogue(dm, rows, fulls):
    g0, g1 = _branch_gates(rows, fulls)
    dg = jnp.concatenate([dm * rows[2][...].astype(F32) * g0 * (1.0 - g0),
                          dm * rows[3][...].astype(F32) * g1 * (1.0 - g1)], axis=-1)
    return [dm * g0, dm * g1, dg, _colsum(dg)]


FFN_TM = 512


def _ffn_in(h2, wg_t, wu_t):
    t, d = h2.shape
    f = wg_t.shape[0]
    tm, tn = FFN_TM, _pick(f, 1536)

    def body(a_ref, g_ref, u_ref, gt_ref, up_ref, act_ref):
        a = a_ref[...]
        gt = _dot_nt(a, g_ref[...])
        up = _dot_nt(a, u_ref[...])
        gt_ref[...] = gt.astype(BF16)
        up_ref[...] = up.astype(BF16)
        act_ref[...] = (gt * _sigmoid(gt) * up).astype(BF16)

    a_spec = pl.BlockSpec((tm, d), lambda j, i: (i, 0))
    w_spec = pl.BlockSpec((tn, d), lambda j, i: (j, 0))
    o_spec = pl.BlockSpec((tm, tn), lambda j, i: (i, j))
    return pl.pallas_call(
        body, name="ffn_in", grid=(f // tn, t // tm), in_specs=[a_spec, w_spec, w_spec],
        out_specs=[o_spec] * 3, out_shape=[SDS((t, f), BF16)] * 3,
        compiler_params=_cparams(("parallel", "arbitrary")))(h2, wg_t, wu_t)


def _ffn_bwd_in(dx2, w_down, gt, up):
    t, d = dx2.shape
    f = w_down.shape[0]
    tm, tn = FFN_TM, _pick(f, 1536)

    def body(a_ref, w_ref, g_ref, u_ref, dgt_ref, dup_ref):
        dv = _dot_nt(a_ref[...], w_ref[...])
        gv = g_ref[...].astype(F32)
        sg = _sigmoid(gv)
        dgt_ref[...] = (dv * u_ref[...].astype(F32) * sg * (1.0 + gv * (1.0 - sg))).astype(BF16)
        dup_ref[...] = (dv * gv * sg).astype(BF16)

    a_spec = pl.BlockSpec((tm, d), lambda j, i: (i, 0))
    w_spec = pl.BlockSpec((tn, d), lambda j, i: (j, 0))
    o_spec = pl.BlockSpec((tm, tn), lambda j, i: (i, j))
    return pl.pallas_call(
        body, name="ffn_bwd_in", grid=(f // tn, t // tm), in_specs=[a_spec, w_spec, o_spec, o_spec],
        out_specs=[o_spec] * 2, out_shape=[SDS((t, f), BF16)] * 2,
        compiler_params=_cparams(("parallel", "arbitrary")))(dx2, w_down, gt, up)


def _adamw(w, g, m, v, name):
    r, c = w.shape[-2:]
    lead = w.ndim - 2
    tr = _row_tile(r, max(8, 400_000 // c))
    c1 = 1.0 / (1.0 - ADAM_B1 ** ADAM_STEP)
    c2 = 1.0 / (1.0 - ADAM_B2 ** ADAM_STEP)

    def fn(i, w_ref, g_ref, m_ref, v_ref):
        gv = g_ref[...]
        mn = ADAM_B1 * m_ref[...] + (1.0 - ADAM_B1) * gv
        vn = ADAM_B2 * v_ref[...] + (1.0 - ADAM_B2) * (gv * gv)
        delta = -ADAM_LR * ((mn * c1) / (jnp.sqrt(vn * c2) + ADAM_EPS) + ADAM_WD * w_ref[...])
        return [delta, mn, vn]

    spec = pl.BlockSpec((None,) * lead + (tr, c), lambda i: (0,) * lead + (i, 0))
    return _rw(name, fn, r // tr, [(w, spec), (g, spec), (m, spec), (v, spec)], [(SDS(w.shape, F32), spec)] * 3)


ANY = pl.BlockSpec(memory_space=pl.ANY)


def _place():
    x, y, c = lax.axis_index("x"), lax.axis_index("y"), lax.axis_index("c")
    chips = [(1 - x, y), (x, 1 - y), (1 - x, 1 - y)]
    return x, y, c, chips


def _remote(src, dst, ssem, rsem, to):
    return pltpu.make_async_remote_copy(src_ref=src, dst_ref=dst, send_sem=ssem, recv_sem=rsem, device_id=to,
                                        device_id_type=MESH)


def _copy_through_vmem(src, dst, buf, isem, osem):
    chunk = buf.shape[1]
    n = src.shape[0] // chunk
    load = lambda k: pltpu.make_async_copy(src.at[pl.ds(k * chunk, chunk)], buf.at[k % 2], isem.at[k % 2])
    store = lambda k: pltpu.make_async_copy(buf.at[k % 2], dst.at[pl.ds(k * chunk, chunk)], osem.at[k % 2])
    load(0).start()
    for k in range(n):
        load(k).wait()
        if k + 1 < n:
            if k >= 1:
                store(k - 1).wait()
            load(k + 1).start()
        store(k).start()
    if n >= 2:
        store(n - 2).wait()
    store(n - 1).wait()


def _copy_scratch(rows, width, dtype):
    chunk = _row_tile(rows, 512)
    return [pltpu.VMEM((2, chunk, width), dtype), pltpu.SemaphoreType.DMA((2,)), pltpu.SemaphoreType.DMA((2,))]


def _gather_weights(wp):
    def body(w_ref, out_ref, ssem, rsem, buf, isem, osem):
        x, y, c, chips = _place()
        me = 2 * x + y
        sib = (x, y, 1 - c)
        first = [_remote(w_ref.at[c], out_ref.at[me, c], ssem.at[j], rsem.at[j], (*chip, c))
                 for j, chip in enumerate(chips)]
        for cp in first:
            cp.start()
        for half in range(2):
            _copy_through_vmem(w_ref.at[half], out_ref.at[me, half], buf, isem, osem)
        passed = []
        for j, chip in enumerate(chips):
            ci = 2 * chip[0] + chip[1]
            _remote(w_ref.at[c], out_ref.at[ci, c], ssem.at[j], rsem.at[j], (*chip, c)).wait_recv()
            cp = _remote(out_ref.at[ci, c], out_ref.at[ci, c], ssem.at[3 + j], rsem.at[3 + j], sib)
            cp.start()
            passed.append(cp)
        for j, chip in enumerate(chips):
            ci = 2 * chip[0] + chip[1]
            _remote(out_ref.at[ci, 1 - c], out_ref.at[ci, 1 - c], ssem.at[3 + j], rsem.at[3 + j], sib).wait_recv()
        for cp in first + passed:
            cp.wait_send()

    return pl.pallas_call(
        body, name="gather_weights", in_specs=[ANY], out_specs=ANY,
        out_shape=SDS((N_CHIPS,) + wp.shape, wp.dtype),
        scratch_shapes=[pltpu.SemaphoreType.DMA((6,)), pltpu.SemaphoreType.DMA((6,))]
        + _copy_scratch(wp.shape[1], wp.shape[2], wp.dtype),
        compiler_params=pltpu.CompilerParams(has_side_effects=True))(wp)


def _swap_halves(g2, tag):
    nch = g2.shape[0]

    def body(g_ref, out_ref, ssem, rsem):
        x, y, c, _ = _place()
        cps = [_remote(g_ref.at[k, 1 - c], out_ref.at[k], ssem.at[k], rsem.at[k], (x, y, 1 - c)) for k in range(nch)]
        for cp in cps:
            cp.start()
        for cp in cps:
            cp.wait()

    return pl.pallas_call(
        body, name="swap_halves_" + tag, in_specs=[ANY], out_specs=ANY,
        out_shape=SDS((nch,) + g2.shape[2:], g2.dtype),
        scratch_shapes=[pltpu.SemaphoreType.DMA((nch,)), pltpu.SemaphoreType.DMA((nch,))],
        compiler_params=pltpu.CompilerParams(has_side_effects=True))(g2)


def _swap_side(g2):
    nch = g2.shape[0]

    def copies(ins, outs, scr):
        x, y, c, _ = _place()
        return [_remote(ins[0].at[k, 1 - c], outs[0].at[k], scr[0].at[k], scr[1].at[k], (x, y, 1 - c))
                for k in range(nch)]

    def first(ins, outs, scr):
        for cp in copies(ins, outs, scr):
            cp.start()

    def last(ins, outs, scr):
        for cp in copies(ins, outs, scr):
            cp.wait()

    return _Side((g2,), (SDS((nch,) + g2.shape[2:], g2.dtype),),
                 (pltpu.SemaphoreType.DMA((nch,)), pltpu.SemaphoreType.DMA((nch,))), first, None, last)


def _add_own_half(g2, other, c, tag):
    nch, _, rows, w = g2.shape
    tr = _row_tile(rows, 512)
    nr = rows // tr

    def body(c_ref, a_ref, b_ref, o_ref):
        o_ref[...] = (a_ref[...].astype(F32) + b_ref[...].astype(F32)).astype(o_ref.dtype)

    grid_spec = pltpu.PrefetchScalarGridSpec(
        num_scalar_prefetch=1, grid=(nch, nr),
        in_specs=[pl.BlockSpec((None, None, tr, w), lambda k, i, c_ref: (k, c_ref[0], i, 0)),
                  pl.BlockSpec((None, tr, w), lambda k, i, c_ref: (k, i, 0))],
        out_specs=pl.BlockSpec((None, tr, w), lambda k, i, c_ref: (k, i, 0)))
    return pl.pallas_call(
        body, name="add_own_half_" + tag, grid_spec=grid_spec, out_shape=SDS(other.shape, other.dtype),
        compiler_params=_cparams(("arbitrary", "arbitrary")))(jnp.reshape(c, (1,)).astype(jnp.int32), g2, other)


def _sum_chips(q, tag):
    nch, rows, w = q.shape
    tr = _row_tile(rows, 512)

    def fn(i, q_ref):
        return [((q_ref[0].astype(F32) + q_ref[1].astype(F32)) + q_ref[2].astype(F32)) + q_ref[3].astype(F32)]

    return _rw("sum_chips_" + tag, fn, rows // tr, [(q, pl.BlockSpec((nch, tr, w), lambda i: (0, i, 0)))],
               [(SDS((rows, w), F32), _rs(tr, w))])[0]


def _chip_copies(src_ref, dst_ref, ssem, rsem, outgoing):
    x, y, c, chips = _place()
    me = 2 * x + y
    cps = []
    for j, chip in enumerate(chips):
        ci = 2 * chip[0] + chip[1]
        cps.append(_remote(src_ref.at[ci], dst_ref.at[me if outgoing else ci], ssem.at[j], rsem.at[j], (*chip, c)))
    return cps, me


def _scatter_side(p):
    def first(ins, outs, scr):
        cps, me = _chip_copies(ins[0], outs[0], scr[0], scr[1], True)
        for cp in cps:
            cp.start()
        pltpu.make_async_copy(ins[0].at[me], outs[0].at[me], scr[2]).start()

    def last(ins, outs, scr):
        for cp in _chip_copies(ins[0], outs[0], scr[0], scr[1], False)[0]:
            cp.wait_recv()
        cps, me = _chip_copies(ins[0], outs[0], scr[0], scr[1], True)
        for cp in cps:
            cp.wait_send()
        pltpu.make_async_copy(ins[0].at[me], outs[0].at[me], scr[2]).wait()

    return _Side((p,), (SDS(p.shape, p.dtype),),
                 (pltpu.SemaphoreType.DMA((3,)), pltpu.SemaphoreType.DMA((3,)), pltpu.SemaphoreType.DMA(())),
                 first, None, last)


def _gather_copies(w_ref, out_ref, ssem, rsem):
    x, y, c, chips = _place()
    me = 2 * x + y
    sib = (x, y, 1 - c)
    sends, arrivals, forwards, from_sib = [], [], [], []
    for j, chip in enumerate(chips):
        ci = 2 * chip[0] + chip[1]
        sends.append(_remote(w_ref.at[c], out_ref.at[me, c], ssem.at[j], rsem.at[j], (*chip, c)))
        arrivals.append(_remote(w_ref.at[c], out_ref.at[ci, c], ssem.at[j], rsem.at[j], (*chip, c)))
        forwards.append(_remote(out_ref.at[ci, c], out_ref.at[ci, c], ssem.at[3 + j], rsem.at[3 + j], sib))
        from_sib.append(_remote(out_ref.at[ci, 1 - c], out_ref.at[ci, 1 - c], ssem.at[3 + j], rsem.at[3 + j], sib))
    return sends, arrivals, forwards, from_sib, me


def _gather_side(wp):
    def first(ins, outs, scr):
        sends, _, _, _, me = _gather_copies(ins[0], outs[0], scr[0], scr[1])
        for cp in sends:
            cp.start()
        pltpu.make_async_copy(ins[0], outs[0].at[me], scr[2]).start()

    def mid(ins, outs, scr):
        _, arrivals, forwards, _, _ = _gather_copies(ins[0], outs[0], scr[0], scr[1])
        for arrived, forward in zip(arrivals, forwards):
            arrived.wait_recv()
            forward.start()

    def last(ins, outs, scr):
        sends, _, forwards, from_sib, me = _gather_copies(ins[0], outs[0], scr[0], scr[1])
        for cp in from_sib:
            cp.wait_recv()
        for cp in sends + forwards:
            cp.wait_send()
        pltpu.make_async_copy(ins[0], outs[0].at[me], scr[2]).wait()

    return _Side((wp,), (SDS((N_CHIPS,) + wp.shape, wp.dtype),),
                 (pltpu.SemaphoreType.DMA((6,)), pltpu.SemaphoreType.DMA((6,)), pltpu.SemaphoreType.DMA(())),
                 first, mid, last)


def _allreduce_small(v, name):
    rows, w = v.shape
    offsets = [(dx, dy, dc) for dx in (0, 1) for dy in (0, 1) for dc in (0, 1)][1:]

    def body(v_ref, o_ref, buf, ssem, rsem):
        x, y, c, _ = _place()
        flip = lambda p, d: 1 - p if d else p
        peers = [(flip(x, dx), flip(y, dy), flip(c, dc)) for dx, dy, dc in offsets]
        index = lambda p: 4 * p[0] + 2 * p[1] + p[2]
        me = index((x, y, c))
        buf[me] = v_ref[...]
        sent = [_remote(v_ref, buf.at[me], ssem.at[q], rsem.at[q], p) for q, p in enumerate(peers)]
        for cp in sent:
            cp.start()
        for q, p in enumerate(peers):
            _remote(v_ref, buf.at[index(p)], ssem.at[q], rsem.at[q], p).wait_recv()
        for cp in sent:
            cp.wait_send()
        acc = buf[0]
        for q in range(1, 8):
            acc = acc + buf[q]
        o_ref[...] = acc

    vm = pl.BlockSpec(memory_space=pltpu.VMEM)
    return pl.pallas_call(
        body, name=name, in_specs=[vm], out_specs=vm, out_shape=SDS((rows, w), F32),
        scratch_shapes=[pltpu.VMEM((8, rows, w), F32), pltpu.SemaphoreType.DMA((7,)), pltpu.SemaphoreType.DMA((7,))],
        compiler_params=pltpu.CompilerParams(has_side_effects=True))(v)


def _join_halves(h, tag):
    def body(h_ref, out_ref, ssem, rsem, buf, isem, osem):
        x, y, c, _ = _place()
        cp = _remote(h_ref, out_ref.at[c], ssem, rsem, (x, y, 1 - c))
        cp.start()
        _copy_through_vmem(h_ref, out_ref.at[c], buf, isem, osem)
        _remote(h_ref, out_ref.at[1 - c], ssem, rsem, (x, y, 1 - c)).wait_recv()
        cp.wait_send()

    return pl.pallas_call(
        body, name="join_halves_" + tag, in_specs=[ANY], out_specs=ANY, out_shape=SDS((2,) + h.shape, h.dtype),
        scratch_shapes=[pltpu.SemaphoreType.DMA(()), pltpu.SemaphoreType.DMA(())]
        + _copy_scratch(h.shape[0], h.shape[1], h.dtype),
        compiler_params=pltpu.CompilerParams(has_side_effects=True))(h)


PACK_W = 1024
SHARDED = ("w_in", "w_ffn_gate", "w_ffn_up", "w_ssm_out", "w_att_out", "w_mix_out", "w_ffn_down")
COL_SHARDED = ("w_in", "w_ffn_gate", "w_ffn_up", "w_att_out")
SMALL = ("norm_mix", "b_gate", "conv_b", "dt_bias", "a_log", "d_skip", "ssm_norm", "norm_ffn", "norm_final")


PACK_ROW_ALIGN = 16


def _rows(n):
    return -(-n // (PACK_W * PACK_ROW_ALIGN)) * PACK_ROW_ALIGN


def _pack_rows(parts, total_rows):
    rows = []
    for p in parts:
        size = int(p.size)
        if size % PACK_W:
            p = jnp.pad(p.reshape(-1), (0, PACK_W - size % PACK_W))
        p = p.reshape(-1, PACK_W)
        rows.append(jnp.pad(p, ((0, _rows(size) - p.shape[0]), (0, 0))))
    used = sum(r.shape[0] for r in rows)
    if total_rows > used:
        rows.append(jnp.zeros((total_rows - used, PACK_W), rows[0].dtype))
    return jnp.concatenate(rows, axis=0)


def _padded_rows(n):
    return -(-n // 32) * 32


def _wire_name(name):
    return name + "_t" if name in COL_SHARDED else name


def _wire_shard(w, name):
    return w.T if name in COL_SHARDED else w


def _group_major(a, axis):
    gw = D_INNER // N_GROUPS
    take = lambda lo, n: lax.slice_in_dim(a, lo, lo + n, axis=axis)
    parts = []
    for g in range(N_GROUPS):
        parts += [take(g * gw, gw), take(D_INNER + g * D_STATE, D_STATE),
                  take(D_INNER + N_GROUPS * D_STATE + g * D_STATE, D_STATE)]
    return jnp.concatenate(parts, axis=axis)


def _group_major_inv(a, axis):
    gw = D_INNER // N_GROUPS
    take = lambda lo, n: lax.slice_in_dim(a, lo, lo + n, axis=axis)
    xs = [take(g * GROUP_W, gw) for g in range(N_GROUPS)]
    bs = [take(g * GROUP_W + gw, D_STATE) for g in range(N_GROUPS)]
    cs = [take(g * GROUP_W + gw + D_STATE, D_STATE) for g in range(N_GROUPS)]
    return jnp.concatenate(xs + bs + cs, axis=axis)


LATE = ("w_ffn_gate_t", "w_ffn_up_t", "w_ssm_out", "w_att_out_t", "w_mix_out", "w_ffn_down")


class _Overlap(NamedTuple):
    gather_side: _Side
    late_weights: Callable
    scatter_side: Callable
    swap_in: Callable
    scatter_in: Callable


def _local_step(x, target, wts, overlap):
    nb, seq, d = x.shape
    t = nb * seq
    x = x.reshape(t, d)
    target = target.reshape(t, d)
    hg = HEADS_PER_GROUP

    o1, o2, o3, o4 = D_INNER, D_INNER + CONV_DIM, D_INNER + CONV_DIM + N_HEADS, D_INNER + CONV_DIM + N_HEADS + QKV_DIM
    n_in = o4 + 2 * D_MODEL

    def in_rows(lo, hi):
        per = n_in // N_CHIPS
        parts = [wts["w_in_t"][k, max(lo, k * per) - k * per:min(hi, (k + 1) * per) - k * per]
                 for k in range(N_CHIPS) if max(lo, k * per) < min(hi, (k + 1) * per)]
        return parts[0] if len(parts) == 1 else jnp.concatenate(parts, axis=0)

    w_z = in_rows(0, o1)
    w_xbc = _group_major(in_rows(o1, o2), 0)
    w_dt = jnp.pad(in_rows(o2, o3), ((0, DT_PAD - N_HEADS), (0, 0)))
    w_qkv = in_rows(o3, o4)
    w_gate = in_rows(o4, n_in)
    conv_w = _group_major(wts["conv_w"], 1)
    conv_b = _group_major(wts["conv_b"], 1)

    def per_group_row(p):
        return p.reshape(N_GROUPS, 1, hg)

    def per_group_col(p):
        return p.reshape(N_GROUPS, hg, 1)

    a_neg = -jnp.exp(wts["a_log"])
    bias_r, bias_c = per_group_row(wts["dt_bias"]), per_group_col(wts["dt_bias"])
    a_r, a_c = per_group_row(a_neg), per_group_col(a_neg)
    dskip_r = per_group_row(wts["d_skip"])
    cos, sin = _rope_tables(seq)

    h = _rms_fwd(x, wts["norm_mix"], "rms_mix_fwd")
    z = _mm(h, w_z, "nt", BF16, "proj_z")
    xbc = _mm(h, w_xbc, "nt", F32, "proj_xbc")
    dt_raw = _mm(h, w_dt, "nt", F32, "proj_dt")
    qkv = _mm(h, w_qkv, "nt", BF16, "proj_qkv")
    gate_logits = _mm(h, w_gate, "nt", BF16, "proj_gate")

    xc = _conv_fwd(xbc, conv_w, conv_b, seq)
    dtr = dt_raw[:, :N_HEADS].reshape(t, N_GROUPS, hg).transpose(1, 0, 2)
    dtrt = dt_raw[:, :N_HEADS].reshape(nb, seq, N_GROUPS, hg).transpose(2, 0, 3, 1)
    y, states, *gathered = _ssd_fwd(xc, dtr, dtrt, bias_r, bias_c, a_r, a_c, dskip_r, nb, seq, overlap.gather_side)
    wts = {**wts, **overlap.late_weights(gathered)}
    yn, y_ssm = _gate_norm_out(y, z, wts["ssm_norm"], wts["w_ssm_out"])

    groups = range(len(ATT_DILATIONS))
    qg, kg, vg = _rope_fwd(qkv, cos, sin, nb, seq)
    o_g, lse_g = zip(*[_att_fwd(qg[i], kg[i], vg[i], i, seq) for i in groups])
    att = _merge_fwd(o_g, lse_g, nb, seq)
    gate_halves = [(gate_logits, d, 0), (gate_logits, d, 1)]
    b_gate_halves = [(wts["b_gate"], d, 0), (wts["b_gate"], d, 1)]
    y_att, mixed = _mm_fused(att, wts["w_att_out_t"], "nt", "att_out_mix", 512, _mix_fwd_epilogue,
                             gate_halves + [(y_ssm, d, 0)], b_gate_halves, [(d, BF16), (d, BF16)])

    def residual_and_norm(xv, rows, fulls):
        return [xv, xv * lax.rsqrt(jnp.mean(xv * xv, axis=-1, keepdims=True) + EPS) * fulls[0][...]]

    x1, h2 = _mm_fused(mixed, wts["w_mix_out"], "nn", "mix_out_norm", 512, residual_and_norm, [],
                       [(wts["norm_ffn"], d, 0)], [(d, F32), (d, BF16)], add=x)
    gt, up, act = _ffn_in(h2, wts["w_ffn_gate_t"], wts["w_ffn_up_t"])

    g = {}
    dx2, dx2_b, g["norm_final"], loss = _mm_fused(
        act, wts["w_ffn_down"], "nn", "ffn_down_loss", 512,
        lambda x2, rows, fulls: _final_values(x2, rows[0][...], fulls[0][...]),
        [(target, d, 0)], [(wts["norm_final"].reshape(1, d), d, 0)], [(d, F32), (d, BF16), (d, F32), (1, F32)],
        n_acc=2, add=x1)
    g["w_ffn_down"] = _mm(act, dx2_b, "tn", BF16, "g_ffn_down")
    dgt, dup = _ffn_bwd_in(dx2_b, wts["w_ffn_down"], gt, up)
    g["w_ffn_gate_t"] = _mm(dgt, h2, "tn", BF16, "g_ffn_gate")
    g["w_ffn_up_t"] = _mm(dup, h2, "tn", BF16, "g_ffn_up")
    dh2 = _mm(dgt, wts["w_ffn_gate_t"], "nn", F32, "d_h2_gate")
    dx1, dx1_b, g["norm_ffn"] = _mm_fused(
        dup, wts["w_ffn_up_t"], "nn", "d_h2_up_norm", 512,
        lambda dh, rows, fulls: _rms_bwd_values(rows[0][...], dh, fulls[0][...], rows[1][...]),
        [(x1, d, 0), (dx2, d, 0)], [(wts["norm_ffn"], d, 0)], [(d, F32), (d, BF16), (d, F32)], n_acc=1, add=dh2)

    g["w_mix_out"] = _mm(mixed, dx1_b, "tn", BF16, "g_mix_out")
    dy_ssm, dy_att, dgate, g["b_gate"] = _mm_fused(
        dx1_b, wts["w_mix_out"], "nt", "d_mixed_gates", 512, _mix_bwd_epilogue,
        gate_halves + [(y_ssm, d, 0), (y_att, d, 0)], b_gate_halves,
        [(d, BF16), (d, BF16), (2 * d, BF16), (2 * d, F32)], n_acc=1)

    datt = _mm(dy_att, wts["w_att_out_t"], "nn", F32, "d_att")
    g["w_att_out_t"] = _mm(dy_att, att, "tn", BF16, "g_att_out")
    do_g, dlt_g = _merge_bwd(o_g, lse_g, datt, nb, seq)
    dq_g, dk_g, dv_g = zip(*[_att_bwd(qg[i], kg[i], vg[i], do_g[i], lse_g[i], dlt_g[i], i, seq) for i in groups])
    dqkv = _rope_bwd(dq_g, dk_g, dv_g, cos, sin, nb, seq)

    g["w_ssm_out"] = _mm(yn, dy_ssm, "tn", BF16, "g_ssm_out")
    dy, dz, g["ssm_norm"] = _mm_fused(
        dy_ssm, wts["w_ssm_out"], "nt", "d_yn_norm", 256, _gate_norm_bwd_epilogue,
        [(y, D_INNER, 0), (z, D_INNER, 0)], [(wts["ssm_norm"], D_INNER, 0)],
        [(D_INNER, BF16), (D_INNER, BF16), (D_INNER, F32)], n_acc=1)
    side = overlap.scatter_side({n: g.pop(n) for n in LATE})
    dxc, ddtr, g_bias, g_alog, g_dskip, *scattered = _ssd_bwd(xc, dtr, dtrt, bias_r, bias_c, a_r, a_c, dskip_r,
                                                               states, dy, nb, seq, side)
    g["dt_bias"] = g_bias.reshape(1, N_HEADS)
    g["a_log"] = g_alog.reshape(1, N_HEADS)
    g["d_skip"] = g_dskip.reshape(1, N_HEADS)
    dpre, g_conv_w, g_conv_b = _conv_bwd_pre(xbc, conv_w, conv_b, dxc, seq)
    g["conv_w"] = _group_major_inv(g_conv_w, 1)
    g["conv_b"] = _group_major_inv(g_conv_b, 1)
    dxbc = _conv_bwd_in(dpre, conv_w, seq)
    ddt = jnp.pad(ddtr.transpose(1, 0, 2).reshape(t, N_HEADS), ((0, 0), (0, DT_PAD - N_HEADS))).astype(BF16)

    g_in_t = jnp.concatenate([
        _mm(dz, h, "tn", BF16, "g_in_z"),
        _group_major_inv(_mm(dxbc, h, "tn", BF16, "g_in_xbc"), 0),
        _mm(ddt, h, "tn", BF16, "g_in_dt")[:N_HEADS],
        _mm(dqkv, h, "tn", BF16, "g_in_qkv"),
        _mm(dgate, h, "tn", BF16, "g_in_gate")], axis=0)
    swap_side, packed = overlap.swap_in({"w_in_t": g_in_t})
    dh = _mm(dz, w_z, "nn", F32, "d_h_z")
    dh, *swapped = _mm(dxbc, w_xbc, "nn", F32, "d_h_xbc", add=dh, side=swap_side)
    dh = _mm(ddt, w_dt, "nn", F32, "d_h_dt", add=dh)
    dh, *scattered_in = _mm(dqkv, w_qkv, "nn", F32, "d_h_qkv", add=dh, side=overlap.scatter_in(packed, swapped))
    dx, _, g["norm_mix"] = _mm_fused(
        dgate, w_gate, "nn", "d_h_gate_norm", 512,
        lambda dhv, rows, fulls: _rms_bwd_values(rows[0][...], dhv, fulls[0][...], rows[1][...]),
        [(x, d, 0), (dx1, d, 0)], [(wts["norm_mix"], d, 0)], [(d, F32), (d, BF16), (d, F32)], n_acc=1, add=dh)
    return loss[0, 0], dx.reshape(nb, seq, d), g, scattered, scattered_in


def kernel(x, norm_mix, w_in, b_gate, conv_w, conv_b, dt_bias, a_log, d_skip, ssm_norm, w_ssm_out, w_att_out, w_mix_out, norm_ffn, w_ffn_gate, w_ffn_up, w_ffn_down, norm_final, loss_target, m_norm_mix, m_w_in, m_b_gate, m_conv_w, m_conv_b, m_dt_bias, m_a_log, m_d_skip, m_ssm_norm, m_w_ssm_out, m_w_att_out, m_w_mix_out, m_norm_ffn, m_w_ffn_gate, m_w_ffn_up, m_w_ffn_down, m_norm_final, v_norm_mix, v_w_in, v_b_gate, v_conv_w, v_conv_b, v_dt_bias, v_a_log, v_d_skip, v_ssm_norm, v_w_ssm_out, v_w_att_out, v_w_mix_out, v_norm_ffn, v_w_ffn_gate, v_w_ffn_up, v_w_ffn_down, v_norm_final):
    names = ("norm_mix", "w_in", "b_gate", "conv_w", "conv_b", "dt_bias", "a_log", "d_skip", "ssm_norm", "w_ssm_out",
             "w_att_out", "w_mix_out", "norm_ffn", "w_ffn_gate", "w_ffn_up", "w_ffn_down", "norm_final")
    w_loc = dict(zip(names, (norm_mix, w_in, b_gate, conv_w, conv_b, dt_bias, a_log, d_skip, ssm_norm, w_ssm_out,
                             w_att_out, w_mix_out, norm_ffn, w_ffn_gate, w_ffn_up, w_ffn_down, norm_final)))
    m_loc = dict(zip(names, (m_norm_mix, m_w_in, m_b_gate, m_conv_w, m_conv_b, m_dt_bias, m_a_log, m_d_skip,
                             m_ssm_norm, m_w_ssm_out, m_w_att_out, m_w_mix_out, m_norm_ffn, m_w_ffn_gate,
                             m_w_ffn_up, m_w_ffn_down, m_norm_final)))
    v_loc = dict(zip(names, (v_norm_mix, v_w_in, v_b_gate, v_conv_w, v_conv_b, v_dt_bias, v_a_log, v_d_skip,
                             v_ssm_norm, v_w_ssm_out, v_w_att_out, v_w_mix_out, v_norm_ffn, v_w_ffn_gate,
                             v_w_ffn_up, v_w_ffn_down, v_norm_final)))
    two_d = lambda a: a.reshape(a.shape[-2:]) if a.ndim >= 2 else a.reshape(1, -1)
    w2 = {n: two_d(a) for n, a in w_loc.items()}
    chip = 2 * lax.axis_index("x") + lax.axis_index("y")
    c = lax.axis_index("c")

    wire_shapes = {n: _wire_shard(w2[n], n).shape for n in SHARDED}
    true_rows = {n: wire_shapes[n][0] * wire_shapes[n][1] // PACK_W for n in SHARDED}
    seg_rows = {n: _rows(wire_shapes[n][0] * wire_shapes[n][1]) for n in SHARDED}
    buckets = {"first": ("w_in",), "late": tuple(n for n in SHARDED if n != "w_in")}
    rows_of = {b: _padded_rows(sum(seg_rows[n] for n in ns)) for b, ns in buckets.items()}

    def pack_shards(b):
        packed = _pack_rows([_wire_shard(w2[n], n).astype(BF16) for n in buckets[b]], rows_of[b])
        return packed.reshape(2, rows_of[b] // 2, PACK_W)

    def unpack_full(gathered, b):
        wg, out, off = gathered.reshape(N_CHIPS, rows_of[b], PACK_W), {}, 0
        for n in buckets[b]:
            rows, cols = wire_shapes[n]
            out[_wire_name(n)] = wg[:, off:off + true_rows[n]].reshape(N_CHIPS * rows, cols)
            off += seg_rows[n]
        return out

    def pack_grads(g, b):
        sections = [_pack_rows([g[_wire_name(n)].reshape(N_CHIPS, true_rows[n], PACK_W)[k] for n in buckets[b]],
                               rows_of[b]) for k in range(N_CHIPS)]
        return jnp.stack(sections).reshape(N_CHIPS, 2, rows_of[b] // 2, PACK_W)

    def chip_sums(g, b):
        g2 = pack_grads(g, b)
        return _add_own_half(g2, _swap_halves(g2, b), c, b)

    def finish(by_source, b):
        reduced = _join_halves(_sum_chips(by_source, b), b).reshape(rows_of[b], PACK_W)
        out, off = {}, 0
        for n in buckets[b]:
            out[n] = reduced[off:off + true_rows[n]].reshape(wire_shapes[n])
            off += seg_rows[n]
        return out

    full = {"w_in_t": _gather_weights(pack_shards("first")).reshape(N_CHIPS, rows_of["first"], PACK_W)}
    for n in SMALL:
        full[n] = w2[n]
    def swap_in(g):
        g2 = pack_grads(g, "first")
        return _swap_side(g2), g2

    overlap = _Overlap(_gather_side(pack_shards("late")), lambda outs: unpack_full(outs[0], "late"),
                       lambda g: _scatter_side(chip_sums(g, "late")), swap_in,
                       lambda g2, swapped: _scatter_side(_add_own_half(g2, swapped[0], c, "first")))

    n_conv = w2["conv_w"].shape[1]
    placed = lax.dynamic_update_slice_in_dim(jnp.zeros((CONV_K, N_CHIPS * n_conv), F32), w2["conv_w"], chip * n_conv, 1)
    placed = jnp.where(c == 0, placed, 0.0)
    full["conv_w"] = _allreduce_small(_pack_rows([placed], _rows(int(placed.size))), "gather_conv_w").reshape(
        -1)[:placed.size].reshape(placed.shape)

    loss_sum, grad_x, g_full, scattered, scattered_in = _local_step(x, loss_target, full, overlap)
    loss = lax.psum(loss_sum, ("x", "y", "c"))

    g_shard = {}
    small_names = SMALL + ("conv_w",)
    small_flat = jnp.concatenate([g_full[n].reshape(-1) for n in small_names])
    small = _allreduce_small(_pack_rows([small_flat], _rows(int(small_flat.size))), "allreduce_small").reshape(-1)
    off = 0
    for n in small_names:
        size = int(g_full[n].size)
        g_shard[n] = small[off:off + size].reshape(g_full[n].shape)
        off += size
    g_shard["conv_w"] = lax.dynamic_slice_in_dim(g_shard["conv_w"], chip * n_conv, n_conv, 1)

    g_shard.update(finish(scattered[0], "late"))
    g_shard.update(finish(scattered_in[0], "first"))

    grads, deltas, new_m, new_v = [], [], [], []
    for n in names:
        shape = w_loc[n].shape
        if n in COL_SHARDED:
            view = unview = lambda a: jnp.swapaxes(a, -1, -2)
        else:
            view, unview = ((lambda a: a) if len(shape) >= 2 else two_d), (lambda a: a.reshape(shape))
        gn = g_shard[n].reshape(view(w_loc[n]).shape)
        outs = _adamw(view(w_loc[n]), gn, view(m_loc[n]), view(v_loc[n]), "adamw_" + n)
        for acc, a in zip((grads, deltas, new_m, new_v), (gn, *outs)):
            acc.append(unview(a))
    return (loss, grad_x, *grads, *deltas, *new_m, *new_v)
```

```python
import functools
from typing import Callable, NamedTuple, Optional

import jax
import jax.numpy as jnp
from jax import lax
from jax.experimental import pallas as pl
from jax.experimental.pallas import tpu as pltpu

F32 = jnp.float32
BF16 = jnp.bfloat16
SDS = jax.ShapeDtypeStruct
MESH = pl.DeviceIdType.MESH

D_MODEL = 1024
D_INNER = 2048
N_HEADS = 32
HEAD_P = 64
N_GROUPS = 4
HEADS_PER_GROUP = N_HEADS // N_GROUPS
D_STATE = 128
CONV_K = 4
CHUNK = 128
CONV_DIM = D_INNER + 2 * N_GROUPS * D_STATE
GROUP_W = D_INNER // N_GROUPS + 2 * D_STATE
ATT_HEADS = 12
ATT_D = 128
ATT_SLOTS = 4
ATT_W = ATT_SLOTS * ATT_D
ATT_DILATIONS = (1, 4, 16)
ATT_BLOCK = 128
QKV_DIM = 3 * ATT_HEADS * ATT_D
DT_PAD = 128
ROPE_THETA = 10000.0
EPS = 1e-6
N_CHIPS = 4
LANES = 128

ADAM_LR = 0.001
ADAM_B1 = 0.9
ADAM_B2 = 0.999
ADAM_EPS = 1e-08
ADAM_WD = 0.01
ADAM_STEP = 10

VMEM_LIMIT = 48 * 1024 * 1024


def _cparams(semantics):
    return pltpu.CompilerParams(dimension_semantics=semantics, vmem_limit_bytes=VMEM_LIMIT)


def _pick(n, cap):
    best = None
    for t in range(LANES, min(n, cap) + 1, LANES):
        if n % t == 0:
            best = t
    return best or n


def _row_tile(rows, cap):
    best = None
    for t in range(8, min(rows, cap) + 1, 8):
        if rows % t == 0:
            best = t
    return best or rows


def _sigmoid(x):
    return pl.reciprocal(1.0 + jnp.exp(-x), approx=True)


def _softplus(x):
    return jnp.maximum(x, 0.0) + jnp.log(1.0 + jnp.exp(-jnp.abs(x)))


def _dot(a, b):
    return jnp.dot(a, b, preferred_element_type=F32)


def _dot_nt(a, b):
    return lax.dot_general(a, b, (((1,), (1,)), ((), ())), preferred_element_type=F32)


def _dot_tn(a, b):
    return lax.dot_general(a, b, (((0,), (0,)), ((), ())), preferred_element_type=F32)


def _mm(a, b, mode, out_dtype, name, add=None, side=None):
    if mode == "nn":
        (m, k), (_, n) = a.shape, b.shape
    elif mode == "nt":
        (m, k), (n, _) = a.shape, b.shape
    else:
        (k, m), (_, n) = a.shape, b.shape
    tm, tn = _pick(m, 1536), _pick(n, 2048)
    tk = k if k <= 2048 else _pick(k, 2048)
    nk = k // tk
    dims = {"nn": ((1,), (0,)), "nt": ((1,), (1,)), "tn": ((0,), (0,))}[mode]

    def partial_product(a_ref, b_ref):
        return lax.dot_general(a_ref[...].astype(BF16), b_ref[...].astype(BF16), (dims, ((), ())),
                               preferred_element_type=F32)

    def body(*refs):
        a_ref, b_ref = refs[:2]
        c_ref = refs[2] if add is not None else None
        o_ref = refs[3] if add is not None else refs[2]

        def finish(r):
            if add is not None:
                r = r + c_ref[...].astype(F32)
            o_ref[...] = r.astype(out_dtype)

        if nk == 1:
            finish(partial_product(a_ref, b_ref))
            return
        acc = refs[-1]
        kk = pl.program_id(2)

        @pl.when(kk == 0)
        def _():
            acc[...] = partial_product(a_ref, b_ref)

        @pl.when((kk > 0) & (kk < nk - 1))
        def _():
            acc[...] += partial_product(a_ref, b_ref)

        @pl.when(kk == nk - 1)
        def _():
            finish(acc[...] + partial_product(a_ref, b_ref))

    a_spec = {"nn": pl.BlockSpec((tm, tk), lambda j, i, q: (i, q)),
              "nt": pl.BlockSpec((tm, tk), lambda j, i, q: (i, q)),
              "tn": pl.BlockSpec((tk, tm), lambda j, i, q: (q, i))}[mode]
    b_spec = {"nn": pl.BlockSpec((tk, tn), lambda j, i, q: (q, j)),
              "nt": pl.BlockSpec((tn, tk), lambda j, i, q: (j, q)),
              "tn": pl.BlockSpec((tk, tn), lambda j, i, q: (q, j))}[mode]
    o_spec = pl.BlockSpec((tm, tn), lambda j, i, q: (i, j))
    ins, specs = [a, b], [a_spec, b_spec]
    if add is not None:
        ins.append(add)
        specs.append(o_spec)
    acc = [pltpu.VMEM((tm, tn), F32)] if nk > 1 else []
    grid = (n // tn, m // tm, nk)
    if side is None:
        return pl.pallas_call(
            body, name=name, grid=grid, in_specs=specs, out_specs=o_spec, out_shape=SDS((m, n), out_dtype),
            scratch_shapes=acc, compiler_params=_cparams(("parallel", "parallel", "arbitrary")))(*ins)
    return pl.pallas_call(
        _attach_side(body, len(ins), 1, side, grid), name=name, grid=grid,
        in_specs=specs + [ANY] * len(side.ins), out_specs=[o_spec] + [ANY] * len(side.out_shapes),
        out_shape=[SDS((m, n), out_dtype)] + list(side.out_shapes), scratch_shapes=acc + list(side.scratch),
        compiler_params=_cparams(("arbitrary", "arbitrary", "arbitrary")))(*ins, *side.ins)


def _mm_fused(a, b, mode, name, tm, epilogue, row_ins, full_ins, outs, n_acc=0, add=None):
    (m, k), n = a.shape, (b.shape[1] if mode == "nn" else b.shape[0])
    tk = k if k <= 2048 else _pick(k, 2048)
    nk = k // tk
    dims = {"nn": ((1,), (0,)), "nt": ((1,), (1,))}[mode]
    n_row, n_full, n_out = len(row_ins), len(full_ins), len(outs)

    def partial_product(a_ref, b_ref):
        return lax.dot_general(a_ref[...], b_ref[...], (dims, ((), ())), preferred_element_type=F32)

    def body(*refs):
        a_ref, b_ref = refs[:2]
        pos = 3 if add is not None else 2
        row_refs, full_refs = refs[pos:pos + n_row], refs[pos + n_row:pos + n_row + n_full]
        out_refs = refs[pos + n_row + n_full:pos + n_row + n_full + n_out]
        i, kk = pl.program_id(0), pl.program_id(1)

        def finish(r):
            if add is not None:
                r = r + refs[2][...].astype(F32)
            for q, (o_ref, v) in enumerate(zip(out_refs, epilogue(r, row_refs, full_refs))):
                if q < n_out - n_acc:
                    o_ref[...] = v.astype(o_ref.dtype)
                else:
                    @pl.when(i == 0)
                    def _(o_ref=o_ref, v=v):
                        o_ref[...] = v

                    @pl.when(i > 0)
                    def _(o_ref=o_ref, v=v):
                        o_ref[...] += v

        if nk == 1:
            finish(partial_product(a_ref, b_ref))
            return
        acc = refs[-1]

        @pl.when(kk == 0)
        def _():
            acc[...] = partial_product(a_ref, b_ref)

        @pl.when((kk > 0) & (kk < nk - 1))
        def _():
            acc[...] += partial_product(a_ref, b_ref)

        @pl.when(kk == nk - 1)
        def _():
            finish(acc[...] + partial_product(a_ref, b_ref))

    tile = lambda w, cb: pl.BlockSpec((tm, w), lambda i, q: (i, cb))
    b_spec = (pl.BlockSpec((tk, n), lambda i, q: (q, 0)) if mode == "nn" else pl.BlockSpec((n, tk), lambda i, q: (0, q)))
    specs = [pl.BlockSpec((tm, tk), lambda i, q: (i, q)), b_spec] + ([tile(n, 0)] if add is not None else [])
    specs += [tile(w, cb) for _, w, cb in row_ins]
    vec = lambda w, cb: pl.BlockSpec((1, w), lambda i, q: (0, cb))
    specs += [vec(w, cb) for _, w, cb in full_ins]
    out_specs = [tile(w, 0) for w, _ in outs[:n_out - n_acc]] + [vec(w, 0) for w, _ in outs[n_out - n_acc:]]
    out_shape = [SDS((m, w), dt) for w, dt in outs[:n_out - n_acc]] + [SDS((1, w), F32) for w, _ in outs[n_out - n_acc:]]
    ins = [a, b] + ([add] if add is not None else []) + [x for x, _, _ in row_ins] + [x for x, _, _ in full_ins]
    return pl.pallas_call(
        body, name=name, grid=(m // tm, nk), in_specs=specs, out_specs=out_specs, out_shape=out_shape,
        scratch_shapes=[pltpu.VMEM((tm, n), F32)] if nk > 1 else [],
        compiler_params=_cparams(("arbitrary", "arbitrary")))(*ins)


def _rw(name, fn, nsteps, ins, outs, n_acc=0):
    n_in, n_out = len(ins), len(outs)

    def body(*refs):
        i = pl.program_id(0)
        vals = fn(i, *refs[:n_in])
        for q, (r, v) in enumerate(zip(refs[n_in:], vals)):
            if q < n_out - n_acc:
                r[...] = v.astype(r.dtype)
            else:
                @pl.when(i == 0)
                def _(r=r):
                    r[...] = jnp.zeros_like(r)

                r[...] += v

    return pl.pallas_call(
        body, name=name, grid=(nsteps,), in_specs=[s for _, s in ins], out_specs=[s for _, s in outs],
        out_shape=[o for o, _ in outs], compiler_params=_cparams(("arbitrary",)))(*[a for a, _ in ins])


def _rs(tm, w, cb=0):
    return pl.BlockSpec((tm, w), lambda i: (i, cb))


def _fs(shape):
    nd = len(shape)
    return pl.BlockSpec(shape, lambda i: (0,) * nd)


def _colsum(v):
    return jnp.sum(v, axis=0, keepdims=True)


def _rms_fwd(x, g, name):
    t, d = x.shape
    tm = 512

    def fn(i, x_ref, g_ref):
        xv = x_ref[...]
        r = lax.rsqrt(jnp.mean(xv * xv, axis=-1, keepdims=True) + EPS)
        return [xv * r * g_ref[...]]

    return _rw(name, fn, t // tm, [(x, _rs(tm, d)), (g, _fs((1, d)))], [(SDS((t, d), BF16), _rs(tm, d))])[0]


def _rms_bwd_values(xv, dhv, gv, dres):
    r = lax.rsqrt(jnp.mean(xv * xv, axis=-1, keepdims=True) + EPS)
    xhat = xv * r
    dxhat = dhv * gv
    dx = dres + r * (dxhat - xhat * jnp.mean(dxhat * xhat, axis=-1, keepdims=True))
    return [dx, dx, _colsum(dhv * xhat)]


def _final_values(xv, target, gv):
    d = xv.shape[-1]
    r = lax.rsqrt(jnp.mean(xv * xv, axis=-1, keepdims=True) + EPS)
    xhat = xv * r
    diff = xhat * gv - target
    lsum = 0.5 * jnp.sum(jnp.sum(diff * diff, axis=-1, keepdims=True) * (1.0 / d), axis=0, keepdims=True)
    dy = diff * (1.0 / d)
    dxhat = dy * gv
    dx = r * (dxhat - xhat * jnp.mean(dxhat * xhat, axis=-1, keepdims=True))
    return [dx, dx, _colsum(dy * xhat), lsum]


CONV_TS = 512
CONV_HALO = 8


def _conv_specs(seq, c):
    ts, tc = CONV_TS, GROUP_W
    hb = ts // CONV_HALO
    u_spec = pl.BlockSpec((ts, tc), lambda j, i: (i, j))
    prev_spec = pl.BlockSpec((CONV_HALO, tc), lambda j, i: (jnp.maximum(i * hb - 1, 0), j))
    w_spec = pl.BlockSpec((CONV_K, tc), lambda j, i: (0, j))
    b_spec = pl.BlockSpec((1, tc), lambda j, i: (0, j))
    return u_spec, prev_spec, w_spec, b_spec


CONV_PIECE = 32


def _conv_fill(i, seq, u_ref, prev_ref, ext):
    first = (i % (seq // CONV_TS)) == 0
    ext[0:CONV_HALO, :] = jnp.where(first, 0.0, prev_ref[...])
    ext[CONV_HALO:, :] = u_ref[...]


def _conv_piece(ext, r0, wv, bv):
    lo = r0 + CONV_HALO - CONV_K + 1
    taps = [ext[lo + q:lo + q + CONV_PIECE, :] for q in range(CONV_K)]
    pre = bv
    for q, tap in enumerate(taps):
        pre = pre + wv[q:q + 1] * tap
    return taps, pre


def _conv_fwd(u, w, b, seq):
    t, c = u.shape
    ts, tc = CONV_TS, GROUP_W
    u_spec, prev_spec, w_spec, b_spec = _conv_specs(seq, c)

    def body(u_ref, prev_ref, w_ref, b_ref, o_ref, ext):
        _conv_fill(pl.program_id(1), seq, u_ref, prev_ref, ext)
        wv, bv = w_ref[...], b_ref[...]
        for r0 in range(0, ts, CONV_PIECE):
            _, pre = _conv_piece(ext, r0, wv, bv)
            o_ref[r0:r0 + CONV_PIECE, :] = pre * _sigmoid(pre)

    return pl.pallas_call(
        body, name="conv_fwd", grid=(c // tc, t // ts), in_specs=[u_spec, prev_spec, w_spec, b_spec],
        out_specs=u_spec, out_shape=SDS((t, c), F32), scratch_shapes=[pltpu.VMEM((ts + CONV_HALO, tc), F32)],
        compiler_params=_cparams(("parallel", "arbitrary")))(u, u, w, b)


def _conv_bwd_pre(u, w, b, dxc, seq):
    t, c = u.shape
    ts, tc = CONV_TS, GROUP_W
    u_spec, prev_spec, w_spec, b_spec = _conv_specs(seq, c)

    def body(u_ref, prev_ref, w_ref, b_ref, d_ref, dpre_ref, dw_ref, db_ref, ext):
        i = pl.program_id(1)
        _conv_fill(i, seq, u_ref, prev_ref, ext)
        wv, bv = w_ref[...], b_ref[...]
        fold = lambda v: sum(v[8 * s:8 * (s + 1)] for s in range(CONV_PIECE // 8))
        sums = [jnp.zeros((8, tc), F32)] * (CONV_K + 1)
        for r0 in range(0, ts, CONV_PIECE):
            taps, pre = _conv_piece(ext, r0, wv, bv)
            sg = _sigmoid(pre)
            dpre = d_ref[r0:r0 + CONV_PIECE, :] * sg * (1.0 + pre * (1.0 - sg))
            dpre_ref[r0:r0 + CONV_PIECE, :] = dpre
            sums = [s + fold(dpre * f) for s, f in zip(sums, taps + [1.0])]

        @pl.when(i == 0)
        def _():
            dw_ref[...] = jnp.zeros_like(dw_ref)
            db_ref[...] = jnp.zeros_like(db_ref)

        db_ref[...] += _colsum(sums[CONV_K])
        for q in range(CONV_K):
            dw_ref[q:q + 1, :] += _colsum(sums[q])

    return pl.pallas_call(
        body, name="conv_bwd_pre", grid=(c // tc, t // ts),
        in_specs=[u_spec, prev_spec, w_spec, b_spec, u_spec], out_specs=[u_spec, w_spec, b_spec],
        out_shape=[SDS((t, c), F32), SDS((CONV_K, c), F32), SDS((1, c), F32)],
        scratch_shapes=[pltpu.VMEM((ts + CONV_HALO, tc), F32)],
        compiler_params=_cparams(("parallel", "arbitrary")))(u, u, w, b, dxc)


def _conv_bwd_in(dpre, w, seq):
    t, c = dpre.shape
    ts, tc = CONV_TS, GROUP_W
    hb = ts // CONV_HALO
    last = t // CONV_HALO - 1
    d_spec = pl.BlockSpec((ts, tc), lambda j, i: (i, j))
    next_spec = pl.BlockSpec((CONV_HALO, tc), lambda j, i: (jnp.minimum((i + 1) * hb, last), j))
    w_spec = pl.BlockSpec((CONV_K, tc), lambda j, i: (0, j))

    def body(d_ref, next_ref, w_ref, o_ref, ext):
        i = pl.program_id(1)
        nts = seq // ts
        is_last = (i % nts) == nts - 1
        ext[0:ts, :] = d_ref[...]
        ext[ts:, :] = jnp.where(is_last, 0.0, next_ref[...])
        wv = w_ref[...]
        for r0 in range(0, ts, CONV_PIECE):
            acc = wv[CONV_K - 1:CONV_K] * ext[r0:r0 + CONV_PIECE, :]
            for q in range(CONV_K - 1):
                lo = r0 + CONV_K - 1 - q
                acc = acc + wv[q:q + 1] * ext[lo:lo + CONV_PIECE, :]
            o_ref[r0:r0 + CONV_PIECE, :] = acc.astype(o_ref.dtype)

    return pl.pallas_call(
        body, name="conv_bwd_in", grid=(c // tc, t // ts), in_specs=[d_spec, next_spec, w_spec],
        out_specs=d_spec, out_shape=SDS((t, c), BF16), scratch_shapes=[pltpu.VMEM((ts + CONV_HALO, tc), F32)],
        compiler_params=_cparams(("parallel", "arbitrary")))(dpre, dpre, w)


def _split3(v):
    hi = v.astype(BF16)
    r1 = v - hi.astype(F32)
    mid = r1.astype(BF16)
    lo = (r1 - mid.astype(F32)).astype(BF16)
    return hi, mid, lo


def _ssd_prelude(dtr_ref, dtrt_ref, bias_ref, biast_ref, a_ref, at_ref):
    dt = _softplus(dtr_ref[...] + bias_ref[...])
    dtt = _softplus(dtrt_ref[...] + biast_ref[...])
    ri = lax.broadcasted_iota(jnp.int32, (CHUNK, CHUNK), 0)
    ci = lax.broadcasted_iota(jnp.int32, (CHUNK, CHUNK), 1)
    lower = ri >= ci
    upper = ri <= ci
    lower_b = jnp.where(lower, 1.0, 0.0).astype(BF16)
    upper_b = jnp.where(upper, 1.0, 0.0).astype(BF16)
    acs = sum(_dot(lower_b, p) for p in _split3(dt * a_ref[...]))
    acst = sum(_dot(p, upper_b) for p in _split3(dtt * at_ref[...]))
    return dt, acs, acst, lower, upper, lower_b, upper_b


SSD_FWD_GPS = 2
SSD_BWD_GPS = 1


def _ssd_specs(seq, gps):
    nc = seq // CHUNK
    hg = HEADS_PER_GROUP
    fwd = lambda c: c
    rev = lambda c: nc - 1 - c

    def specs(cc):
        return dict(
            xc=pl.BlockSpec((CHUNK, gps * GROUP_W), lambda g, b, c: (b * nc + cc(c), g)),
            y=pl.BlockSpec((CHUNK, gps * hg * HEAD_P), lambda g, b, c: (b * nc + cc(c), g)),
            dtr=pl.BlockSpec((gps, CHUNK, hg), lambda g, b, c: (g, b * nc + cc(c), 0)),
            dtrt=pl.BlockSpec((gps, None, hg, CHUNK), lambda g, b, c: (g, b, 0, cc(c))),
            prow=pl.BlockSpec((gps, 1, hg), lambda g, b, c: (g, 0, 0)),
            pcol=pl.BlockSpec((gps, hg, 1), lambda g, b, c: (g, 0, 0)),
            st=pl.BlockSpec((gps, None, None, D_STATE, hg * HEAD_P), lambda g, b, c: (g, b, cc(c), 0, 0)),
        )

    return specs(fwd), specs(rev)


def _group_views(refs, lane_widths, gi):
    return [r.at[:, gi * w:(gi + 1) * w] if w else r.at[gi] for r, w in zip(refs, lane_widths)]


def _head_maps():
    hw = HEADS_PER_GROUP * HEAD_P
    shift = HEAD_P.bit_length() - 1
    hj = lax.broadcasted_iota(jnp.int32, (HEADS_PER_GROUP, hw), 0)
    lq = jnp.right_shift(lax.broadcasted_iota(jnp.int32, (HEADS_PER_GROUP, hw), 1), shift)
    spread = jnp.where(hj == lq, 1.0, 0.0).astype(BF16)
    rq = jnp.right_shift(lax.broadcasted_iota(jnp.int32, (hw, LANES), 0), shift)
    cj = lax.broadcasted_iota(jnp.int32, (hw, LANES), 1)
    gather = jnp.where(rq == cj, 1.0, 0.0).astype(BF16)
    return spread, gather


def _dot01(v, m01):
    hi, mid, _ = _split3(v)
    return _dot(hi, m01) + _dot(mid, m01)


class _Side(NamedTuple):
    ins: tuple
    out_shapes: tuple
    scratch: tuple
    first: Callable
    mid: Optional[Callable]
    last: Callable


def _attach_side(body, n_in, n_out, side, grid):
    si, so, ss = len(side.ins), len(side.out_shapes), len(side.scratch)

    def wrapped(*refs):
        ins, s_in = refs[:n_in], refs[n_in:n_in + si]
        outs = refs[n_in + si:n_in + si + n_out]
        s_out = refs[n_in + si + n_out:n_in + si + n_out + so]
        rest = refs[n_in + si + n_out + so:]
        scr, s_scr = rest[:len(rest) - ss], rest[len(rest) - ss:]
        ids = [pl.program_id(a) for a in range(len(grid))]
        inner_first = functools.reduce(lambda p, q: p & q, [i == 0 for i in ids[1:]], ids[0] >= 0)
        at_last = functools.reduce(lambda p, q: p & q, [i == n - 1 for i, n in zip(ids, grid)])

        @pl.when((ids[0] == 0) & inner_first)
        def _():
            side.first(s_in, s_out, s_scr)

        if side.mid is not None:
            outer_last = functools.reduce(lambda p, q: p & q, [i == n - 1 for i, n in zip(ids[:-1], grid[:-1])])

            @pl.when(outer_last & (ids[-1] == 0))
            def _():
                side.mid(s_in, s_out, s_scr)

        body(*ins, *outs, *scr)

        @pl.when(at_last)
        def _():
            side.last(s_in, s_out, s_scr)

    return wrapped


def _ssd_fwd(xc, dtr, dtrt, bias, biast, a, at, dskip, nb, seq, side):
    t = xc.shape[0]
    nc = seq // CHUNK
    hg = HEADS_PER_GROUP
    hw = hg * HEAD_P
    gps = SSD_FWD_GPS
    grid = (N_GROUPS // gps, nb, nc)
    sp, _ = _ssd_specs(seq, gps)

    def body(*refs):
        for gi in range(gps):
            one_group(*_group_views(refs, (GROUP_W, 0, 0, 0, 0, 0, 0, 0, hw, 0, 0), gi))

    def one_group(xc_ref, dtr_ref, dtrt_ref, bias_ref, biast_ref, a_ref, at_ref, d_ref, y_ref, sin_ref, st):
        @pl.when(pl.program_id(2) == 0)
        def _():
            st[...] = jnp.zeros_like(st)

        s_in = st[...]
        sin_ref[...] = s_in
        dt, acs, acst, lower, _, _, _ = _ssd_prelude(dtr_ref, dtrt_ref, bias_ref, biast_ref, a_ref, at_ref)
        spread, _ = _head_maps()
        x = xc_ref[...]
        xs = x[:, :hw]
        b16 = x[:, hw:hw + D_STATE].astype(BF16)
        c16 = x[:, hw + D_STATE:].astype(BF16)
        cb = _dot_nt(c16, b16)
        last = acs[CHUNK - 1:CHUNK, :]
        e_x = _dot01(jnp.exp(acs), spread)
        dec_x = _dot01(jnp.exp(last - acs), spread)
        tot_x = e_x[CHUNK - 1:CHUNK, :]
        d_x = _dot01(jnp.broadcast_to(d_ref[...], (8, hg)), spread)[0:1, :]
        xdtf = xs * _dot01(dt, spread)
        xdt16 = xdtf.astype(BF16)
        yoff = e_x * _dot(c16, s_in.astype(BF16))
        st[...] = tot_x * s_in + _dot_tn(b16, (dec_x * xdtf).astype(BF16))
        parts = []
        for j in range(hg):
            decay = jnp.exp(jnp.where(lower, acs[:, j:j + 1] - acst[j:j + 1, :], -jnp.inf))
            parts.append(_dot((cb * decay).astype(BF16), xdt16[:, HEAD_P * j:HEAD_P * (j + 1)]))
        y_ref[...] = (jnp.concatenate(parts, axis=-1) + yoff + d_x * xs).astype(y_ref.dtype)

    return pl.pallas_call(
        _attach_side(body, 8, 2, side, grid), name="ssd_fwd", grid=grid,
        in_specs=[sp["xc"], sp["dtr"], sp["dtrt"], sp["prow"], sp["pcol"], sp["prow"], sp["pcol"], sp["prow"]]
        + [ANY] * len(side.ins),
        out_specs=[sp["y"], sp["st"]] + [ANY] * len(side.out_shapes),
        out_shape=[SDS((t, D_INNER), BF16), SDS((N_GROUPS, nb, nc, D_STATE, hw), F32)] + list(side.out_shapes),
        scratch_shapes=[pltpu.VMEM((gps, D_STATE, hw), F32)] + list(side.scratch),
        compiler_params=_cparams(("arbitrary", "arbitrary", "arbitrary")))(
            xc, dtr, dtrt, bias, biast, a, at, dskip, *side.ins)


def _ssd_bwd(xc, dtr, dtrt, bias, biast, a, at, dskip, states, dy, nb, seq, side):
    t = xc.shape[0]
    nc = seq // CHUNK
    hg = HEADS_PER_GROUP
    hw = hg * HEAD_P
    gps = SSD_BWD_GPS
    grid = (N_GROUPS // gps, nb, nc)
    _, sp = _ssd_specs(seq, gps)

    def body(*refs):
        for gi in range(gps):
            one_group(*_group_views(refs, (GROUP_W, 0, 0, 0, 0, 0, 0, 0, 0, hw, GROUP_W, 0, 0, 0, 0, 0), gi))

    def one_group(xc_ref, dtr_ref, dtrt_ref, bias_ref, biast_ref, a_ref, at_ref, d_ref, sin_ref, dy_ref,
                  dxc_ref, ddtr_ref, gbias_ref, ga_ref, gd_ref, ds):
        first = (pl.program_id(1) == 0) & (pl.program_id(2) == 0)

        @pl.when(pl.program_id(2) == 0)
        def _():
            ds[...] = jnp.zeros_like(ds)

        @pl.when(first)
        def _():
            gbias_ref[...] = jnp.zeros_like(gbias_ref)
            ga_ref[...] = jnp.zeros_like(ga_ref)
            gd_ref[...] = jnp.zeros_like(gd_ref)

        dt, acs, acst, lower, upper, _, upper_b = _ssd_prelude(dtr_ref, dtrt_ref, bias_ref, biast_ref, a_ref, at_ref)
        spread, gather = _head_maps()
        x = xc_ref[...]
        dy = dy_ref[...].astype(F32)
        xs = x[:, :hw]
        b16 = x[:, hw:hw + D_STATE].astype(BF16)
        c16 = x[:, hw + D_STATE:].astype(BF16)
        dy16 = dy.astype(BF16)
        cb = _dot_nt(c16, b16)
        cbt = _dot_nt(b16, c16)
        last = acs[CHUNK - 1:CHUNK, :]
        e8 = jnp.exp(acs)
        dec8 = jnp.exp(last - acs)
        e_x = _dot01(e8, spread)
        dec_x = _dot01(dec8, spread)
        tot_x = e_x[CHUNK - 1:CHUNK, :]
        dt_x = _dot01(dt, spread)
        d_x = _dot01(jnp.broadcast_to(d_ref[...], (8, hg)), spread)[0:1, :]
        xdtf = xs * dt_x
        xdt16 = xdtf.astype(BF16)
        s_in = sin_ref[...]
        s16 = s_in.astype(BF16)
        ds_out = ds[...]
        ds16 = ds_out.astype(BF16)
        bds = _dot(b16, ds16)
        cs = _dot(c16, s16)
        edy16 = (e_x * dy).astype(BF16)
        ds[...] = tot_x * ds_out + _dot_tn(c16, edy16)
        lane8 = lax.broadcasted_iota(jnp.int32, (CHUNK, hg), 1)
        row8 = lax.broadcasted_iota(jnp.int32, (CHUNK, hg), 0)
        dacs8 = jnp.zeros((CHUNK, hg), F32)
        acc_m = jnp.zeros((CHUNK, CHUNK), F32)
        acc_mt = jnp.zeros((CHUNK, CHUNK), F32)
        dx_parts = []
        for j in range(hg):
            sl = slice(HEAD_P * j, HEAD_P * (j + 1))
            col = acs[:, j:j + 1]
            row = acst[j:j + 1, :]
            decay = jnp.exp(jnp.where(lower, col - row, -jnp.inf))
            decayt = jnp.exp(jnp.where(upper, row - col, -jnp.inf))
            wm = _dot_nt(dy16[:, sl], xdt16[:, sl]) * decay
            wmt = _dot_nt(xdt16[:, sl], dy16[:, sl]) * decayt
            acc_m = acc_m + wm
            acc_mt = acc_mt + wmt
            dacs8 = dacs8 + jnp.where(lane8 == j, jnp.sum(wm * cb, axis=-1, keepdims=True)
                                      - jnp.sum(wmt * cbt, axis=-1, keepdims=True), 0.0)
            dx_parts.append(_dot((cbt * decayt).astype(BF16), dy16[:, sl]))
        dx = jnp.concatenate(dx_parts, axis=-1) + dec_x * bds
        dxc_ref[:, :hw] = dx * dt_x + d_x * dy
        dxc_ref[:, hw:hw + D_STATE] = _dot(acc_mt.astype(BF16), c16) + _dot_nt((dec_x * xdtf).astype(BF16), ds16)
        dxc_ref[:, hw + D_STATE:] = _dot(acc_m.astype(BF16), b16) + _dot_nt(edy16, s16)
        dtot_rows = jnp.broadcast_to(_colsum(ds_out * s_in), (8, hw))
        sums = _dot01(jnp.concatenate([dy * cs, xdtf * bds, dx * xs, dy * xs, dtot_rows], axis=0), gather)
        de8 = sums[0:CHUNK, :hg]
        ddec8 = sums[CHUNK:2 * CHUNK, :hg]
        ddtx8 = sums[2 * CHUNK:3 * CHUNK, :hg]
        gd8 = _colsum(sums[3 * CHUNK:4 * CHUNK, :hg])
        dtot8 = sums[4 * CHUNK:4 * CHUNK + 1, :hg]
        extra = _colsum(ddec8 * dec8) + dtot8 * e8[CHUNK - 1:CHUNK, :]
        dacs8 = dacs8 + de8 * e8 - ddec8 * dec8 + jnp.where(row8 == CHUNK - 1, extra, 0.0)
        da = sum(_dot(upper_b, p) for p in _split3(dacs8))
        av = a_ref[...]
        ddt = da * av + ddtx8
        ddtr = ddt * _sigmoid(dtr_ref[...] + bias_ref[...])
        ddtr_ref[...] = ddtr
        gbias_ref[...] += _colsum(ddtr)
        ga_ref[...] += _colsum(da * dt) * av
        gd_ref[...] += gd8

    return pl.pallas_call(
        _attach_side(body, 10, 5, side, grid), name="ssd_bwd", grid=grid,
        in_specs=[sp["xc"], sp["dtr"], sp["dtrt"], sp["prow"], sp["pcol"], sp["prow"], sp["pcol"], sp["prow"],
                  sp["st"], sp["y"]] + [ANY] * len(side.ins),
        out_specs=[sp["xc"], sp["dtr"], sp["prow"], sp["prow"], sp["prow"]] + [ANY] * len(side.out_shapes),
        out_shape=[SDS((t, N_GROUPS * GROUP_W), F32), SDS((N_GROUPS, t, hg), F32)]
        + [SDS((N_GROUPS, 1, hg), F32)] * 3 + list(side.out_shapes),
        scratch_shapes=[pltpu.VMEM((gps, D_STATE, hw), F32)] + list(side.scratch),
        compiler_params=_cparams(("arbitrary", "arbitrary", "arbitrary")))(
            xc, dtr, dtrt, bias, biast, a, at, dskip, states, dy, *side.ins)


def _group_bcast(v, width, fn):
    parts = []
    for q in range(v.shape[-1] // width):
        s = fn(v[:, q * width:(q + 1) * width])
        parts.append(jnp.broadcast_to(s, (v.shape[0], width)))
    return jnp.concatenate(parts, axis=-1)


def _gate_norm_out(y, z, g, w):
    t, d = y.shape
    n = w.shape[1]
    tm = 256
    gw = d // N_GROUPS

    def body(y_ref, z_ref, g_ref, w_ref, yn_ref, o_ref):
        zv = z_ref[...].astype(F32)
        u = y_ref[...].astype(F32) * (zv * _sigmoid(zv))
        r = lax.rsqrt(_group_bcast(u * u, gw, lambda p: jnp.mean(p, axis=-1, keepdims=True)) + EPS)
        yn = (u * r * g_ref[...]).astype(yn_ref.dtype)
        yn_ref[...] = yn
        o_ref[...] = _dot(yn, w_ref[...]).astype(o_ref.dtype)

    return pl.pallas_call(
        body, name="gate_norm_ssm_out", grid=(t // tm,),
        in_specs=[_rs(tm, d), _rs(tm, d), _fs((1, d)), _fs((d, n))], out_specs=[_rs(tm, d), _rs(tm, n)],
        out_shape=[SDS((t, d), BF16), SDS((t, n), BF16)], compiler_params=_cparams(("arbitrary",)))(y, z, g, w)


def _gate_norm_bwd_epilogue(dv, rows, fulls):
    yv, zv = rows[0][...].astype(F32), rows[1][...].astype(F32)
    gw = yv.shape[-1] // N_GROUPS
    sg = _sigmoid(zv)
    sz = zv * sg
    u = yv * sz
    r = lax.rsqrt(_group_bcast(u * u, gw, lambda p: jnp.mean(p, axis=-1, keepdims=True)) + EPS)
    uhat = u * r
    duhat = dv * fulls[0][...]
    du = r * (duhat - uhat * _group_bcast(duhat * uhat, gw, lambda p: jnp.mean(p, axis=-1, keepdims=True)))
    dz = du * yv * sg * (1.0 + zv * (1.0 - sg))
    return [du * sz, dz, _colsum(dv * uhat)]


def _rope_tables(seq):
    half = ATT_D // 2
    inv = ROPE_THETA ** (-jnp.arange(half, dtype=F32) / half)
    ang = jnp.arange(seq, dtype=F32)[:, None] * inv[None, :]
    cos, sin = jnp.cos(ang), jnp.sin(ang)
    return jnp.concatenate([cos, cos], axis=-1), jnp.concatenate([-sin, sin], axis=-1)


ATT_TILE = 512
ATT_QB = 8


def _strided_spec(r, mtiles):
    return pl.BlockSpec((None, r, None, ATT_TILE // r, ATT_W), lambda i: (i // mtiles, 0, i % mtiles, 0, 0))


def _strided_shape(nb, r, mtiles, dtype):
    return SDS((nb, r, mtiles, ATT_TILE // r, ATT_W), dtype)


def _to_strided(val, out_ref, lanes, r, sc):
    if r == 1:
        out_ref[0, :, lanes] = val.astype(out_ref.dtype)
        return
    sc[...] = val
    for rr in range(r):
        out_ref[rr, :, lanes] = sc[pl.ds(rr, ATT_TILE // r, stride=r), :].astype(out_ref.dtype)


def _from_strided(in_ref, lanes, r, sc):
    if r == 1:
        return in_ref[0, :, lanes].astype(F32)
    for rr in range(r):
        sc[pl.ds(rr, ATT_TILE // r, stride=r), :] = in_ref[rr, :, lanes].astype(F32)
    return sc[...]


def _rope_fwd(qkv, cos, sin, nb, seq):
    t = qkv.shape[0]
    tm = ATT_TILE
    mtiles = seq // tm
    w = ATT_HEADS * ATT_D
    tab = pl.BlockSpec((tm, ATT_D), lambda i: (i % mtiles, 0))
    ng = len(ATT_DILATIONS)

    def body(q_ref, k_ref, v_ref, cos_ref, sin_ref, *rest):
        outs, sc = rest[:3 * ng], rest[3 * ng]
        c, s = cos_ref[...], sin_ref[...]
        for which, ref in enumerate((q_ref, k_ref, v_ref)):
            for h in range(ATT_HEADS):
                g, slot = divmod(h, ATT_SLOTS)
                p = ref[:, h * ATT_D:(h + 1) * ATT_D].astype(F32)
                if which < 2:
                    p = p * c + pltpu.roll(p, ATT_D // 2, 1) * s
                _to_strided(p, outs[which * ng + g], slice(slot * ATT_D, (slot + 1) * ATT_D), ATT_DILATIONS[g], sc)

    out_specs = [_strided_spec(r, mtiles) for _ in range(3) for r in ATT_DILATIONS]
    out_shape = [_strided_shape(nb, r, mtiles, BF16) for _ in range(3) for r in ATT_DILATIONS]
    outs = pl.pallas_call(
        body, name="rope_fwd", grid=(t // tm,),
        in_specs=[_rs(tm, w, 0), _rs(tm, w, 1), _rs(tm, w, 2), tab, tab], out_specs=out_specs, out_shape=out_shape,
        scratch_shapes=[pltpu.VMEM((tm, ATT_D), F32)], compiler_params=_cparams(("arbitrary",)))(
            qkv, qkv, qkv, cos, sin)
    flat = [o.reshape(t, ATT_W) for o in outs]
    return flat[0:ng], flat[ng:2 * ng], flat[2 * ng:]


def _rope_bwd(dq, dk, dv, cos, sin, nb, seq):
    t = dq[0].shape[0]
    tm = ATT_TILE
    mtiles = seq // tm
    w = ATT_HEADS * ATT_D
    tab = pl.BlockSpec((tm, ATT_D), lambda i: (i % mtiles, 0))
    ng = len(ATT_DILATIONS)

    def body(*refs):
        ins, (cos_ref, sin_ref, o_ref, sc) = refs[:3 * ng], refs[3 * ng:]
        c, s = cos_ref[...], sin_ref[...]
        for which in range(3):
            for h in range(ATT_HEADS):
                g, slot = divmod(h, ATT_SLOTS)
                p = _from_strided(ins[which * ng + g], slice(slot * ATT_D, (slot + 1) * ATT_D), ATT_DILATIONS[g], sc)
                if which < 2:
                    p = p * c - pltpu.roll(p, ATT_D // 2, 1) * s
                o_ref[:, which * w + h * ATT_D:which * w + (h + 1) * ATT_D] = p.astype(o_ref.dtype)

    views = [a.reshape(nb, r, mtiles, tm // r, ATT_W) for grp in (dq, dk, dv) for a, r in zip(grp, ATT_DILATIONS)]
    return pl.pallas_call(
        body, name="rope_bwd", grid=(t // tm,),
        in_specs=[_strided_spec(r, mtiles) for _ in range(3) for r in ATT_DILATIONS] + [tab, tab],
        out_specs=_rs(tm, 3 * w), out_shape=SDS((t, 3 * w), BF16),
        scratch_shapes=[pltpu.VMEM((tm, ATT_D), F32)], compiler_params=_cparams(("arbitrary",)))(*views, cos, sin)


def _att_masks():
    ri = lax.broadcasted_iota(jnp.int32, (ATT_BLOCK, ATT_BLOCK), 0)
    ci = lax.broadcasted_iota(jnp.int32, (ATT_BLOCK, ATT_BLOCK), 1)
    return ci <= ri, ci >= ri


def _att_fwd(q, k, v, g, seq):
    t, w = q.shape
    rows = ATT_QB * ATT_BLOCK
    nbs = seq // ATT_DILATIONS[g] // ATT_BLOCK
    scale = ATT_D ** -0.5
    cur = pl.BlockSpec((rows, w), lambda n: (n, 0))
    prev = pl.BlockSpec((ATT_BLOCK, w), lambda n: (jnp.maximum(n * ATT_QB - 1, 0), 0))

    def body(q_ref, kc_ref, kp_ref, vc_ref, vp_ref, o_ref, lse_ref):
        mcur, mprev = _att_masks()
        for i in range(ATT_QB):
            blk = pl.program_id(0) * ATT_QB + i
            mask = jnp.concatenate([mprev & ((blk % nbs) != 0), mcur], axis=-1)
            own = slice(i * ATT_BLOCK, (i + 1) * ATT_BLOCK)
            for h in range(ATT_SLOTS):
                sl = slice(h * ATT_D, (h + 1) * ATT_D)
                if i == 0:
                    keys = jnp.concatenate([kp_ref[:, sl], kc_ref[own, sl]], axis=0)
                    vals = jnp.concatenate([vp_ref[:, sl], vc_ref[own, sl]], axis=0)
                else:
                    both = slice((i - 1) * ATT_BLOCK, (i + 1) * ATT_BLOCK)
                    keys, vals = kc_ref[both, sl], vc_ref[both, sl]
                s = jnp.where(mask, _dot_nt(q_ref[own, sl], keys) * scale, -jnp.inf)
                m = jnp.max(s, axis=-1, keepdims=True)
                p = jnp.exp(s - m)
                den = jnp.sum(p, axis=-1, keepdims=True)
                o_ref[own, sl] = _dot(p.astype(BF16), vals) / den
                lse_ref[own, sl] = jnp.broadcast_to(m + jnp.log(den), (ATT_BLOCK, ATT_D))

    return pl.pallas_call(
        body, name=f"att_fwd_{g}", grid=(t // rows,), in_specs=[cur, cur, prev, cur, prev], out_specs=[cur, cur],
        out_shape=[SDS((t, w), F32), SDS((t, w), F32)],
        compiler_params=_cparams(("arbitrary",)))(q, k, k, v, v)


def _att_bwd(q, k, v, do, lse, dlt, g, seq):
    t, w = q.shape
    nblk = t // ATT_BLOCK
    rows = ATT_QB * ATT_BLOCK
    nbs = seq // ATT_DILATIONS[g] // ATT_BLOCK
    scale = ATT_D ** -0.5
    cur = pl.BlockSpec((rows, w), lambda n: (n, 0))
    nxt = pl.BlockSpec((ATT_BLOCK, w), lambda n: (jnp.minimum((n + 1) * ATT_QB, nblk - 1), 0))

    def body(qc_ref, qn_ref, k_ref, v_ref, doc_ref, don_ref, lsec_ref, lsen_ref, dltc_ref, dltn_ref,
             dq_ref, dk_ref, dv_ref, carry):
        n = pl.program_id(0)

        @pl.when(n == 0)
        def _():
            carry[...] = jnp.zeros_like(carry)

        mcur, mprev = _att_masks()

        def pair(cur_ref, nxt_ref, i, sl):
            if i + 1 < ATT_QB:
                return cur_ref[i * ATT_BLOCK:(i + 2) * ATT_BLOCK, sl]
            return jnp.concatenate([cur_ref[i * ATT_BLOCK:, sl], nxt_ref[:, sl]], axis=0)

        for h in range(ATT_SLOTS):
            sl = slice(h * ATT_D, (h + 1) * ATT_D)
            from_prev = carry[:, sl]
            for i in range(ATT_QB):
                blk = n * ATT_QB + i
                has_next = (((blk + 1) % nbs) != 0) & (blk + 1 < nblk)
                mask = jnp.concatenate([mcur, mprev & has_next], axis=0)
                own = slice(i * ATT_BLOCK, (i + 1) * ATT_BLOCK)
                kh, vh = k_ref[own, sl], v_ref[own, sl]
                qs, dos = pair(qc_ref, qn_ref, i, sl), pair(doc_ref, don_ref, i, sl)
                lse, dlt = pair(lsec_ref, lsen_ref, i, sl), pair(dltc_ref, dltn_ref, i, sl)
                p = jnp.where(mask, jnp.exp(_dot_nt(qs, kh) * scale - lse), 0.0)
                ds = (p * (_dot_nt(dos, vh) - dlt) * scale).astype(BF16)
                dqs = _dot(ds, kh)
                dq_ref[own, sl] = (from_prev + dqs[:ATT_BLOCK]).astype(dq_ref.dtype)
                from_prev = dqs[ATT_BLOCK:]
                dk_ref[own, sl] = _dot_tn(ds, qs).astype(dk_ref.dtype)
                dv_ref[own, sl] = _dot_tn(p.astype(BF16), dos).astype(dv_ref.dtype)
            carry[:, sl] = from_prev

    return pl.pallas_call(
        body, name=f"att_bwd_{g}", grid=(t // rows,), in_specs=[cur, nxt, cur, cur, cur, nxt, cur, nxt, cur, nxt],
        out_specs=[cur, cur, cur], out_shape=[SDS((t, w), BF16)] * 3,
        scratch_shapes=[pltpu.VMEM((ATT_BLOCK, w), F32)],
        compiler_params=_cparams(("arbitrary",)))(q, q, k, v, do, do, lse, lse, dlt, dlt)


def _merge_weights(ls):
    m = jnp.maximum(jnp.maximum(ls[0], ls[1]), ls[2])
    es = [jnp.exp(v - m) for v in ls]
    den = es[0] + es[1] + es[2]
    return [e / den for e in es]


def _merge_fwd(o, lse, nb, seq):
    t = o[0].shape[0]
    tm = ATT_TILE
    mtiles = seq // tm
    ng = len(ATT_DILATIONS)

    def body(*refs):
        o_refs, l_refs, out_ref, scs = refs[:ng], refs[ng:2 * ng], refs[2 * ng], refs[2 * ng + 1:]
        for slot in range(ATT_SLOTS):
            lanes = slice(slot * ATT_D, (slot + 1) * ATT_D)
            ov = [_from_strided(o_refs[g], lanes, r, scs[2 * g]) for g, r in enumerate(ATT_DILATIONS)]
            ws = _merge_weights([_from_strided(l_refs[g], lanes, r, scs[2 * g + 1])
                                 for g, r in enumerate(ATT_DILATIONS)])
            out_ref[:, lanes] = (ws[0] * ov[0] + ws[1] * ov[1] + ws[2] * ov[2]).astype(out_ref.dtype)

    views = [a.reshape(nb, r, mtiles, tm // r, ATT_W) for grp in (o, lse) for a, r in zip(grp, ATT_DILATIONS)]
    return pl.pallas_call(
        body, name="att_merge_fwd", grid=(t // tm,),
        in_specs=[_strided_spec(r, mtiles) for _ in range(2) for r in ATT_DILATIONS],
        out_specs=_rs(tm, ATT_W), out_shape=SDS((t, ATT_W), BF16),
        scratch_shapes=[pltpu.VMEM((tm, ATT_D), F32)] * (2 * ng), compiler_params=_cparams(("arbitrary",)))(*views)


def _merge_bwd(o, lse, datt, nb, seq):
    t = o[0].shape[0]
    tm = ATT_TILE
    mtiles = seq // tm
    ng = len(ATT_DILATIONS)

    def body(*refs):
        o_refs, l_refs, d_ref = refs[:ng], refs[ng:2 * ng], refs[2 * ng]
        do_refs, dlt_refs = refs[2 * ng + 1:3 * ng + 1], refs[3 * ng + 1:4 * ng + 1]
        scs = refs[4 * ng + 1:]
        for slot in range(ATT_SLOTS):
            lanes = slice(slot * ATT_D, (slot + 1) * ATT_D)
            ov = [_from_strided(o_refs[g], lanes, r, scs[2 * g]) for g, r in enumerate(ATT_DILATIONS)]
            ws = _merge_weights([_from_strided(l_refs[g], lanes, r, scs[2 * g + 1])
                                 for g, r in enumerate(ATT_DILATIONS)])
            dv = d_ref[:, lanes]
            att = ws[0] * ov[0] + ws[1] * ov[1] + ws[2] * ov[2]
            dot = jnp.broadcast_to(jnp.sum(dv * att, axis=-1, keepdims=True), (tm, ATT_D))
            for g, r in enumerate(ATT_DILATIONS):
                _to_strided(ws[g] * dv, do_refs[g], lanes, r, scs[2 * ng])
                _to_strided(ws[g] * dot, dlt_refs[g], lanes, r, scs[2 * ng + 1])

    views = [a.reshape(nb, r, mtiles, tm // r, ATT_W) for grp in (o, lse) for a, r in zip(grp, ATT_DILATIONS)]
    outs = pl.pallas_call(
        body, name="att_merge_bwd", grid=(t // tm,),
        in_specs=[_strided_spec(r, mtiles) for _ in range(2) for r in ATT_DILATIONS] + [_rs(tm, ATT_W)],
        out_specs=[_strided_spec(r, mtiles) for _ in range(2) for r in ATT_DILATIONS],
        out_shape=[_strided_shape(nb, r, mtiles, dt) for dt in (BF16, F32) for r in ATT_DILATIONS],
        scratch_shapes=[pltpu.VMEM((tm, ATT_D), F32)] * (2 * ng + 2), compiler_params=_cparams(("arbitrary",)))(
            *views, datt)
    flat = [a.reshape(t, ATT_W) for a in outs]
    return flat[:ng], flat[ng:]


def _branch_gates(rows, fulls):
    return (_sigmoid(rows[0][...].astype(F32) + fulls[0][...]), _sigmoid(rows[1][...].astype(F32) + fulls[1][...]))


def _mix_fwd_epilogue(y_att, rows, fulls):
    g0, g1 = _branch_gates(rows, fulls)
    return [y_att, g0 * rows[2][...].astype(F32) + g1 * y_att]


def _mix_bwd_epilogue(dm, rows, fulls):
    g0, g1 = _branch_gates(rows, fulls)
    dg = jnp.concatenate([dm * rows[2][...].astype(F32) * g0 * (1.0 - g0),
                          dm * rows[3][...].astype(F32) * g1 * (1.0 - g1)], axis=-1)
    return [dm * g0, dm * g1, dg, _colsum(dg)]


FFN_TM = 512


def _ffn_in(h2, wg_t, wu_t):
    t, d = h2.shape
    f = wg_t.shape[0]
    tm, tn = FFN_TM, _pick(f, 1536)

    def body(a_ref, g_ref, u_ref, gt_ref, up_ref, act_ref):
        a = a_ref[...]
        gt = _dot_nt(a, g_ref[...])
        up = _dot_nt(a, u_ref[...])
        gt_ref[...] = gt.astype(BF16)
        up_ref[...] = up.astype(BF16)
        act_ref[...] = (gt * _sigmoid(gt) * up).astype(BF16)

    a_spec = pl.BlockSpec((tm, d), lambda j, i: (i, 0))
    w_spec = pl.BlockSpec((tn, d), lambda j, i: (j, 0))
    o_spec = pl.BlockSpec((tm, tn), lambda j, i: (i, j))
    return pl.pallas_call(
        body, name="ffn_in", grid=(f // tn, t // tm), in_specs=[a_spec, w_spec, w_spec],
        out_specs=[o_spec] * 3, out_shape=[SDS((t, f), BF16)] * 3,
        compiler_params=_cparams(("parallel", "arbitrary")))(h2, wg_t, wu_t)


def _ffn_bwd_in(dx2, w_down, gt, up):
    t, d = dx2.shape
    f = w_down.shape[0]
    tm, tn = FFN_TM, _pick(f, 1536)

    def body(a_ref, w_ref, g_ref, u_ref, dgt_ref, dup_ref):
        dv = _dot_nt(a_ref[...], w_ref[...])
        gv = g_ref[...].astype(F32)
        sg = _sigmoid(gv)
        dgt_ref[...] = (dv * u_ref[...].astype(F32) * sg * (1.0 + gv * (1.0 - sg))).astype(BF16)
        dup_ref[...] = (dv * gv * sg).astype(BF16)

    a_spec = pl.BlockSpec((tm, d), lambda j, i: (i, 0))
    w_spec = pl.BlockSpec((tn, d), lambda j, i: (j, 0))
    o_spec = pl.BlockSpec((tm, tn), lambda j, i: (i, j))
    return pl.pallas_call(
        body, name="ffn_bwd_in", grid=(f // tn, t // tm), in_specs=[a_spec, w_spec, o_spec, o_spec],
        out_specs=[o_spec] * 2, out_shape=[SDS((t, f), BF16)] * 2,
        compiler_params=_cparams(("parallel", "arbitrary")))(dx2, w_down, gt, up)


def _adamw(w, g, m, v, name):
    r, c = w.shape[-2:]
    lead = w.ndim - 2
    tr = _row_tile(r, max(8, 400_000 // c))
    c1 = 1.0 / (1.0 - ADAM_B1 ** ADAM_STEP)
    c2 = 1.0 / (1.0 - ADAM_B2 ** ADAM_STEP)

    def fn(i, w_ref, g_ref, m_ref, v_ref):
        gv = g_ref[...]
        mn = ADAM_B1 * m_ref[...] + (1.0 - ADAM_B1) * gv
        vn = ADAM_B2 * v_ref[...] + (1.0 - ADAM_B2) * (gv * gv)
        delta = -ADAM_LR * ((mn * c1) / (jnp.sqrt(vn * c2) + ADAM_EPS) + ADAM_WD * w_ref[...])
        return [delta, mn, vn]

    spec = pl.BlockSpec((None,) * lead + (tr, c), lambda i: (0,) * lead + (i, 0))
    return _rw(name, fn, r // tr, [(w, spec), (g, spec), (m, spec), (v, spec)], [(SDS(w.shape, F32), spec)] * 3)


ANY = pl.BlockSpec(memory_space=pl.ANY)


def _place():
    x, y, c = lax.axis_index("x"), lax.axis_index("y"), lax.axis_index("c")
    chips = [(1 - x, y), (x, 1 - y), (1 - x, 1 - y)]
    return x, y, c, chips


def _remote(src, dst, ssem, rsem, to):
    return pltpu.make_async_remote_copy(src_ref=src, dst_ref=dst, send_sem=ssem, recv_sem=rsem, device_id=to,
                                        device_id_type=MESH)


def _copy_through_vmem(src, dst, buf, isem, osem):
    chunk = buf.shape[1]
    n = src.shape[0] // chunk
    load = lambda k: pltpu.make_async_copy(src.at[pl.ds(k * chunk, chunk)], buf.at[k % 2], isem.at[k % 2])
    store = lambda k: pltpu.make_async_copy(buf.at[k % 2], dst.at[pl.ds(k * chunk, chunk)], osem.at[k % 2])
    load(0).start()
    for k in range(n):
        load(k).wait()
        if k + 1 < n:
            if k >= 1:
                store(k - 1).wait()
            load(k + 1).start()
        store(k).start()
    if n >= 2:
        store(n - 2).wait()
    store(n - 1).wait()


def _copy_scratch(rows, width, dtype):
    chunk = _row_tile(rows, 512)
    return [pltpu.VMEM((2, chunk, width), dtype), pltpu.SemaphoreType.DMA((2,)), pltpu.SemaphoreType.DMA((2,))]


def _gather_weights(wp):
    def body(w_ref, out_ref, ssem, rsem, buf, isem, osem):
        x, y, c, chips = _place()
        me = 2 * x + y
        sib = (x, y, 1 - c)
        first = [_remote(w_ref.at[c], out_ref.at[me, c], ssem.at[j], rsem.at[j], (*chip, c))
                 for j, chip in enumerate(chips)]
        for cp in first:
            cp.start()
        for half in range(2):
            _copy_through_vmem(w_ref.at[half], out_ref.at[me, half], buf, isem, osem)
        passed = []
        for j, chip in enumerate(chips):
            ci = 2 * chip[0] + chip[1]
            _remote(w_ref.at[c], out_ref.at[ci, c], ssem.at[j], rsem.at[j], (*chip, c)).wait_recv()
            cp = _remote(out_ref.at[ci, c], out_ref.at[ci, c], ssem.at[3 + j], rsem.at[3 + j], sib)
            cp.start()
            passed.append(cp)
        for j, chip in enumerate(chips):
            ci = 2 * chip[0] + chip[1]
            _remote(out_ref.at[ci, 1 - c], out_ref.at[ci, 1 - c], ssem.at[3 + j], rsem.at[3 + j], sib).wait_recv()
        for cp in first + passed:
            cp.wait_send()

    return pl.pallas_call(
        body, name="gather_weights", in_specs=[ANY], out_specs=ANY,
        out_shape=SDS((N_CHIPS,) + wp.shape, wp.dtype),
        scratch_shapes=[pltpu.SemaphoreType.DMA((6,)), pltpu.SemaphoreType.DMA((6,))]
        + _copy_scratch(wp.shape[1], wp.shape[2], wp.dtype),
        compiler_params=pltpu.CompilerParams(has_side_effects=True))(wp)


def _swap_halves(g2, tag):
    nch = g2.shape[0]

    def body(g_ref, out_ref, ssem, rsem):
        x, y, c, _ = _place()
        cps = [_remote(g_ref.at[k, 1 - c], out_ref.at[k], ssem.at[k], rsem.at[k], (x, y, 1 - c)) for k in range(nch)]
        for cp in cps:
            cp.start()
        for cp in cps:
            cp.wait()

    return pl.pallas_call(
        body, name="swap_halves_" + tag, in_specs=[ANY], out_specs=ANY,
        out_shape=SDS((nch,) + g2.shape[2:], g2.dtype),
        scratch_shapes=[pltpu.SemaphoreType.DMA((nch,)), pltpu.SemaphoreType.DMA((nch,))],
        compiler_params=pltpu.CompilerParams(has_side_effects=True))(g2)


def _swap_side(g2):
    nch = g2.shape[0]

    def copies(ins, outs, scr):
        x, y, c, _ = _place()
        return [_remote(ins[0].at[k, 1 - c], outs[0].at[k], scr[0].at[k], scr[1].at[k], (x, y, 1 - c))
                for k in range(nch)]

    def first(ins, outs, scr):
        for cp in copies(ins, outs, scr):
            cp.start()

    def last(ins, outs, scr):
        for cp in copies(ins, outs, scr):
            cp.wait()

    return _Side((g2,), (SDS((nch,) + g2.shape[2:], g2.dtype),),
                 (pltpu.SemaphoreType.DMA((nch,)), pltpu.SemaphoreType.DMA((nch,))), first, None, last)


def _add_own_half(g2, other, c, tag):
    nch, _, rows, w = g2.shape
    tr = _row_tile(rows, 512)
    nr = rows // tr

    def body(c_ref, a_ref, b_ref, o_ref):
        o_ref[...] = (a_ref[...].astype(F32) + b_ref[...].astype(F32)).astype(o_ref.dtype)

    grid_spec = pltpu.PrefetchScalarGridSpec(
        num_scalar_prefetch=1, grid=(nch, nr),
        in_specs=[pl.BlockSpec((None, None, tr, w), lambda k, i, c_ref: (k, c_ref[0], i, 0)),
                  pl.BlockSpec((None, tr, w), lambda k, i, c_ref: (k, i, 0))],
        out_specs=pl.BlockSpec((None, tr, w), lambda k, i, c_ref: (k, i, 0)))
    return pl.pallas_call(
        body, name="add_own_half_" + tag, grid_spec=grid_spec, out_shape=SDS(other.shape, other.dtype),
        compiler_params=_cparams(("arbitrary", "arbitrary")))(jnp.reshape(c, (1,)).astype(jnp.int32), g2, other)


def _sum_chips(q, tag):
    nch, rows, w = q.shape
    tr = _row_tile(rows, 512)

    def fn(i, q_ref):
        return [((q_ref[0].astype(F32) + q_ref[1].astype(F32)) + q_ref[2].astype(F32)) + q_ref[3].astype(F32)]

    return _rw("sum_chips_" + tag, fn, rows // tr, [(q, pl.BlockSpec((nch, tr, w), lambda i: (0, i, 0)))],
               [(SDS((rows, w), F32), _rs(tr, w))])[0]


def _chip_copies(src_ref, dst_ref, ssem, rsem, outgoing):
    x, y, c, chips = _place()
    me = 2 * x + y
    cps = []
    for j, chip in enumerate(chips):
        ci = 2 * chip[0] + chip[1]
        cps.append(_remote(src_ref.at[ci], dst_ref.at[me if outgoing else ci], ssem.at[j], rsem.at[j], (*chip, c)))
    return cps, me


def _scatter_side(p):
    def first(ins, outs, scr):
        cps, me = _chip_copies(ins[0], outs[0], scr[0], scr[1], True)
        for cp in cps:
            cp.start()
        pltpu.make_async_copy(ins[0].at[me], outs[0].at[me], scr[2]).start()

    def last(ins, outs, scr):
        for cp in _chip_copies(ins[0], outs[0], scr[0], scr[1], False)[0]:
            cp.wait_recv()
        cps, me = _chip_copies(ins[0], outs[0], scr[0], scr[1], True)
        for cp in cps:
            cp.wait_send()
        pltpu.make_async_copy(ins[0].at[me], outs[0].at[me], scr[2]).wait()

    return _Side((p,), (SDS(p.shape, p.dtype),),
                 (pltpu.SemaphoreType.DMA((3,)), pltpu.SemaphoreType.DMA((3,)), pltpu.SemaphoreType.DMA(())),
                 first, None, last)


def _gather_copies(w_ref, out_ref, ssem, rsem):
    x, y, c, chips = _place()
    me = 2 * x + y
    sib = (x, y, 1 - c)
    sends, arrivals, forwards, from_sib = [], [], [], []
    for j, chip in enumerate(chips):
        ci = 2 * chip[0] + chip[1]
        sends.append(_remote(w_ref.at[c], out_ref.at[me, c], ssem.at[j], rsem.at[j], (*chip, c)))
        arrivals.append(_remote(w_ref.at[c], out_ref.at[ci, c], ssem.at[j], rsem.at[j], (*chip, c)))
        forwards.append(_remote(out_ref.at[ci, c], out_ref.at[ci, c], ssem.at[3 + j], rsem.at[3 + j], sib))
        from_sib.append(_remote(out_ref.at[ci, 1 - c], out_ref.at[ci, 1 - c], ssem.at[3 + j], rsem.at[3 + j], sib))
    return sends, arrivals, forwards, from_sib, me


def _gather_side(wp):
    def first(ins, outs, scr):
        sends, _, _, _, me = _gather_copies(ins[0], outs[0], scr[0], scr[1])
        for cp in sends:
            cp.start()
        pltpu.make_async_copy(ins[0], outs[0].at[me], scr[2]).start()

    def mid(ins, outs, scr):
        _, arrivals, forwards, _, _ = _gather_copies(ins[0], outs[0], scr[0], scr[1])
        for arrived, forward in zip(arrivals, forwards):
            arrived.wait_recv()
            forward.start()

    def last(ins, outs, scr):
        sends, _, forwards, from_sib, me = _gather_copies(ins[0], outs[0], scr[0], scr[1])
        for cp in from_sib:
            cp.wait_recv()
        for cp in sends + forwards:
            cp.wait_send()
        pltpu.make_async_copy(ins[0], outs[0].at[me], scr[2]).wait()

    return _Side((wp,), (SDS((N_CHIPS,) + wp.shape, wp.dtype),),
                 (pltpu.SemaphoreType.DMA((6,)), pltpu.SemaphoreType.DMA((6,)), pltpu.SemaphoreType.DMA(())),
                 first, mid, last)


def _allreduce_small(v, name):
    rows, w = v.shape
    offsets = [(dx, dy, dc) for dx in (0, 1) for dy in (0, 1) for dc in (0, 1)][1:]

    def body(v_ref, o_ref, buf, ssem, rsem):
        x, y, c, _ = _place()
        flip = lambda p, d: 1 - p if d else p
        peers = [(flip(x, dx), flip(y, dy), flip(c, dc)) for dx, dy, dc in offsets]
        index = lambda p: 4 * p[0] + 2 * p[1] + p[2]
        me = index((x, y, c))
        buf[me] = v_ref[...]
        sent = [_remote(v_ref, buf.at[me], ssem.at[q], rsem.at[q], p) for q, p in enumerate(peers)]
        for cp in sent:
            cp.start()
        for q, p in enumerate(peers):
            _remote(v_ref, buf.at[index(p)], ssem.at[q], rsem.at[q], p).wait_recv()
        for cp in sent:
            cp.wait_send()
        acc = buf[0]
        for q in range(1, 8):
            acc = acc + buf[q]
        o_ref[...] = acc

    vm = pl.BlockSpec(memory_space=pltpu.VMEM)
    return pl.pallas_call(
        body, name=name, in_specs=[vm], out_specs=vm, out_shape=SDS((rows, w), F32),
        scratch_shapes=[pltpu.VMEM((8, rows, w), F32), pltpu.SemaphoreType.DMA((7,)), pltpu.SemaphoreType.DMA((7,))],
        compiler_params=pltpu.CompilerParams(has_side_effects=True))(v)


def _join_halves(h, tag):
    def body(h_ref, out_ref, ssem, rsem, buf, isem, osem):
        x, y, c, _ = _place()
        cp = _remote(h_ref, out_ref.at[c], ssem, rsem, (x, y, 1 - c))
        cp.start()
        _copy_through_vmem(h_ref, out_ref.at[c], buf, isem, osem)
        _remote(h_ref, out_ref.at[1 - c], ssem, rsem, (x, y, 1 - c)).wait_recv()
        cp.wait_send()

    return pl.pallas_call(
        body, name="join_halves_" + tag, in_specs=[ANY], out_specs=ANY, out_shape=SDS((2,) + h.shape, h.dtype),
        scratch_shapes=[pltpu.SemaphoreType.DMA(()), pltpu.SemaphoreType.DMA(())]
        + _copy_scratch(h.shape[0], h.shape[1], h.dtype),
        compiler_params=pltpu.CompilerParams(has_side_effects=True))(h)


PACK_W = 1024
SHARDED = ("w_in", "w_ffn_gate", "w_ffn_up", "w_ssm_out", "w_att_out", "w_mix_out", "w_ffn_down")
COL_SHARDED = ("w_in", "w_ffn_gate", "w_ffn_up", "w_att_out")
SMALL = ("norm_mix", "b_gate", "conv_b", "dt_bias", "a_log", "d_skip", "ssm_norm", "norm_ffn", "norm_final")


PACK_ROW_ALIGN = 16


def _rows(n):
    return -(-n // (PACK_W * PACK_ROW_ALIGN)) * PACK_ROW_ALIGN


def _pack_rows(parts, total_rows):
    rows = []
    for p in parts:
        size = int(p.size)
        if size % PACK_W:
            p = jnp.pad(p.reshape(-1), (0, PACK_W - size % PACK_W))
        p = p.reshape(-1, PACK_W)
        rows.append(jnp.pad(p, ((0, _rows(size) - p.shape[0]), (0, 0))))
    used = sum(r.shape[0] for r in rows)
    if total_rows > used:
        rows.append(jnp.zeros((total_rows - used, PACK_W), rows[0].dtype))
    return jnp.concatenate(rows, axis=0)


def _padded_rows(n):
    return -(-n // 32) * 32


def _wire_name(name):
    return name + "_t" if name in COL_SHARDED else name


def _wire_shard(w, name):
    return w.T if name in COL_SHARDED else w


def _group_major(a, axis):
    gw = D_INNER // N_GROUPS
    take = lambda lo, n: lax.slice_in_dim(a, lo, lo + n, axis=axis)
    parts = []
    for g in range(N_GROUPS):
        parts += [take(g * gw, gw), take(D_INNER + g * D_STATE, D_STATE),
                  take(D_INNER + N_GROUPS * D_STATE + g * D_STATE, D_STATE)]
    return jnp.concatenate(parts, axis=axis)


def _group_major_inv(a, axis):
    gw = D_INNER // N_GROUPS
    take = lambda lo, n: lax.slice_in_dim(a, lo, lo + n, axis=axis)
    xs = [take(g * GROUP_W, gw) for g in range(N_GROUPS)]
    bs = [take(g * GROUP_W + gw, D_STATE) for g in range(N_GROUPS)]
    cs = [take(g * GROUP_W + gw + D_STATE, D_STATE) for g in range(N_GROUPS)]
    return jnp.concatenate(xs + bs + cs, axis=axis)


LATE = ("w_ffn_gate_t", "w_ffn_up_t", "w_ssm_out", "w_att_out_t", "w_mix_out", "w_ffn_down")


class _Overlap(NamedTuple):
    gather_side: _Side
    late_weights: Callable
    scatter_side: Callable
    swap_in: Callable
    scatter_in: Callable


def _local_step(x, target, wts, overlap):
    nb, seq, d = x.shape
    t = nb * seq
    x = x.reshape(t, d)
    target = target.reshape(t, d)
    hg = HEADS_PER_GROUP

    o1, o2, o3, o4 = D_INNER, D_INNER + CONV_DIM, D_INNER + CONV_DIM + N_HEADS, D_INNER + CONV_DIM + N_HEADS + QKV_DIM
    n_in = o4 + 2 * D_MODEL

    def in_rows(lo, hi):
        per = n_in // N_CHIPS
        parts = [wts["w_in_t"][k, max(lo, k * per) - k * per:min(hi, (k + 1) * per) - k * per]
                 for k in range(N_CHIPS) if max(lo, k * per) < min(hi, (k + 1) * per)]
        return parts[0] if len(parts) == 1 else jnp.concatenate(parts, axis=0)

    w_z = in_rows(0, o1)
    w_xbc = _group_major(in_rows(o1, o2), 0)
    w_dt = jnp.pad(in_rows(o2, o3), ((0, DT_PAD - N_HEADS), (0, 0)))
    w_qkv = in_rows(o3, o4)
    w_gate = in_rows(o4, n_in)
    conv_w = _group_major(wts["conv_w"], 1)
    conv_b = _group_major(wts["conv_b"], 1)

    def per_group_row(p):
        return p.reshape(N_GROUPS, 1, hg)

    def per_group_col(p):
        return p.reshape(N_GROUPS, hg, 1)

    a_neg = -jnp.exp(wts["a_log"])
    bias_r, bias_c = per_group_row(wts["dt_bias"]), per_group_col(wts["dt_bias"])
    a_r, a_c = per_group_row(a_neg), per_group_col(a_neg)
    dskip_r = per_group_row(wts["d_skip"])
    cos, sin = _rope_tables(seq)

    h = _rms_fwd(x, wts["norm_mix"], "rms_mix_fwd")
    z = _mm(h, w_z, "nt", BF16, "proj_z")
    xbc = _mm(h, w_xbc, "nt", F32, "proj_xbc")
    dt_raw = _mm(h, w_dt, "nt", F32, "proj_dt")
    qkv = _mm(h, w_qkv, "nt", BF16, "proj_qkv")
    gate_logits = _mm(h, w_gate, "nt", BF16, "proj_gate")

    xc = _conv_fwd(xbc, conv_w, conv_b, seq)
    dtr = dt_raw[:, :N_HEADS].reshape(t, N_GROUPS, hg).transpose(1, 0, 2)
    dtrt = dt_raw[:, :N_HEADS].reshape(nb, seq, N_GROUPS, hg).transpose(2, 0, 3, 1)
    y, states, *gathered = _ssd_fwd(xc, dtr, dtrt, bias_r, bias_c, a_r, a_c, dskip_r, nb, seq, overlap.gather_side)
    wts = {**wts, **overlap.late_weights(gathered)}
    yn, y_ssm = _gate_norm_out(y, z, wts["ssm_norm"], wts["w_ssm_out"])

    groups = range(len(ATT_DILATIONS))
    qg, kg, vg = _rope_fwd(qkv, cos, sin, nb, seq)
    o_g, lse_g = zip(*[_att_fwd(qg[i], kg[i], vg[i], i, seq) for i in groups])
    att = _merge_fwd(o_g, lse_g, nb, seq)
    gate_halves = [(gate_logits, d, 0), (gate_logits, d, 1)]
    b_gate_halves = [(wts["b_gate"], d, 0), (wts["b_gate"], d, 1)]
    y_att, mixed = _mm_fused(att, wts["w_att_out_t"], "nt", "att_out_mix", 512, _mix_fwd_epilogue,
                             gate_halves + [(y_ssm, d, 0)], b_gate_halves, [(d, BF16), (d, BF16)])

    def residual_and_norm(xv, rows, fulls):
        return [xv, xv * lax.rsqrt(jnp.mean(xv * xv, axis=-1, keepdims=True) + EPS) * fulls[0][...]]

    x1, h2 = _mm_fused(mixed, wts["w_mix_out"], "nn", "mix_out_norm", 512, residual_and_norm, [],
                       [(wts["norm_ffn"], d, 0)], [(d, F32), (d, BF16)], add=x)
    gt, up, act = _ffn_in(h2, wts["w_ffn_gate_t"], wts["w_ffn_up_t"])

    g = {}
    dx2, dx2_b, g["norm_final"], loss = _mm_fused(
        act, wts["w_ffn_down"], "nn", "ffn_down_loss", 512,
        lambda x2, rows, fulls: _final_values(x2, rows[0][...], fulls[0][...]),
        [(target, d, 0)], [(wts["norm_final"].reshape(1, d), d, 0)], [(d, F32), (d, BF16), (d, F32), (1, F32)],
        n_acc=2, add=x1)
    g["w_ffn_down"] = _mm(act, dx2_b, "tn", BF16, "g_ffn_down")
    dgt, dup = _ffn_bwd_in(dx2_b, wts["w_ffn_down"], gt, up)
    g["w_ffn_gate_t"] = _mm(dgt, h2, "tn", BF16, "g_ffn_gate")
    g["w_ffn_up_t"] = _mm(dup, h2, "tn", BF16, "g_ffn_up")
    dh2 = _mm(dgt, wts["w_ffn_gate_t"], "nn", F32, "d_h2_gate")
    dx1, dx1_b, g["norm_ffn"] = _mm_fused(
        dup, wts["w_ffn_up_t"], "nn", "d_h2_up_norm", 512,
        lambda dh, rows, fulls: _rms_bwd_values(rows[0][...], dh, fulls[0][...], rows[1][...]),
        [(x1, d, 0), (dx2, d, 0)], [(wts["norm_ffn"], d, 0)], [(d, F32), (d, BF16), (d, F32)], n_acc=1, add=dh2)

    g["w_mix_out"] = _mm(mixed, dx1_b, "tn", BF16, "g_mix_out")
    dy_ssm, dy_att, dgate, g["b_gate"] = _mm_fused(
        dx1_b, wts["w_mix_out"], "nt", "d_mixed_gates", 512, _mix_bwd_epilogue,
        gate_halves + [(y_ssm, d, 0), (y_att, d, 0)], b_gate_halves,
        [(d, BF16), (d, BF16), (2 * d, BF16), (2 * d, F32)], n_acc=1)

    datt = _mm(dy_att, wts["w_att_out_t"], "nn", F32, "d_att")
    g["w_att_out_t"] = _mm(dy_att, att, "tn", BF16, "g_att_out")
    do_g, dlt_g = _merge_bwd(o_g, lse_g, datt, nb, seq)
    dq_g, dk_g, dv_g = zip(*[_att_bwd(qg[i], kg[i], vg[i], do_g[i], lse_g[i], dlt_g[i], i, seq) for i in groups])
    dqkv = _rope_bwd(dq_g, dk_g, dv_g, cos, sin, nb, seq)

    g["w_ssm_out"] = _mm(yn, dy_ssm, "tn", BF16, "g_ssm_out")
    dy, dz, g["ssm_norm"] = _mm_fused(
        dy_ssm, wts["w_ssm_out"], "nt", "d_yn_norm", 256, _gate_norm_bwd_epilogue,
        [(y, D_INNER, 0), (z, D_INNER, 0)], [(wts["ssm_norm"], D_INNER, 0)],
        [(D_INNER, BF16), (D_INNER, BF16), (D_INNER, F32)], n_acc=1)
    side = overlap.scatter_side({n: g.pop(n) for n in LATE})
    dxc, ddtr, g_bias, g_alog, g_dskip, *scattered = _ssd_bwd(xc, dtr, dtrt, bias_r, bias_c, a_r, a_c, dskip_r,
                                                               states, dy, nb, seq, side)
    g["dt_bias"] = g_bias.reshape(1, N_HEADS)
    g["a_log"] = g_alog.reshape(1, N_HEADS)
    g["d_skip"] = g_dskip.reshape(1, N_HEADS)
    dpre, g_conv_w, g_conv_b = _conv_bwd_pre(xbc, conv_w, conv_b, dxc, seq)
    g["conv_w"] = _group_major_inv(g_conv_w, 1)
    g["conv_b"] = _group_major_inv(g_conv_b, 1)
    dxbc = _conv_bwd_in(dpre, conv_w, seq)
    ddt = jnp.pad(ddtr.transpose(1, 0, 2).reshape(t, N_HEADS), ((0, 0), (0, DT_PAD - N_HEADS))).astype(BF16)

    g_in_t = jnp.concatenate([
        _mm(dz, h, "tn", BF16, "g_in_z"),
        _group_major_inv(_mm(dxbc, h, "tn", BF16, "g_in_xbc"), 0),
        _mm(ddt, h, "tn", BF16, "g_in_dt")[:N_HEADS],
        _mm(dqkv, h, "tn", BF16, "g_in_qkv"),
        _mm(dgate, h, "tn", BF16, "g_in_gate")], axis=0)
    swap_side, packed = overlap.swap_in({"w_in_t": g_in_t})
    dh = _mm(dz, w_z, "nn", F32, "d_h_z")
    dh, *swapped = _mm(dxbc, w_xbc, "nn", F32, "d_h_xbc", add=dh, side=swap_side)
    dh = _mm(ddt, w_dt, "nn", F32, "d_h_dt", add=dh)
    dh, *scattered_in = _mm(dqkv, w_qkv, "nn", F32, "d_h_qkv", add=dh, side=overlap.scatter_in(packed, swapped))
    dx, g["norm_mix"] = _mm_fused(
        dgate, w_gate, "nn", "d_h_gate_norm", 512,
        lambda dhv, rows, fulls: _rms_bwd_values(rows[0][...], dhv, fulls[0][...], rows[1][...])[1:],
        [(x, d, 0), (dx1, d, 0)], [(wts["norm_mix"], d, 0)], [(d, F32), (d, F32)], n_acc=1, add=dh)
    return loss[0, 0], dx.reshape(nb, seq, d), g, scattered, scattered_in


def kernel(x, norm_mix, w_in, b_gate, conv_w, conv_b, dt_bias, a_log, d_skip, ssm_norm, w_ssm_out, w_att_out, w_mix_out, norm_ffn, w_ffn_gate, w_ffn_up, w_ffn_down, norm_final, loss_target, m_norm_mix, m_w_in, m_b_gate, m_conv_w, m_conv_b, m_dt_bias, m_a_log, m_d_skip, m_ssm_norm, m_w_ssm_out, m_w_att_out, m_w_mix_out, m_norm_ffn, m_w_ffn_gate, m_w_ffn_up, m_w_ffn_down, m_norm_final, v_norm_mix, v_w_in, v_b_gate, v_conv_w, v_conv_b, v_dt_bias, v_a_log, v_d_skip, v_ssm_norm, v_w_ssm_out, v_w_att_out, v_w_mix_out, v_norm_ffn, v_w_ffn_gate, v_w_ffn_up, v_w_ffn_down, v_norm_final):
    names = ("norm_mix", "w_in", "b_gate", "conv_w", "conv_b", "dt_bias", "a_log", "d_skip", "ssm_norm", "w_ssm_out",
             "w_att_out", "w_mix_out", "norm_ffn", "w_ffn_gate", "w_ffn_up", "w_ffn_down", "norm_final")
    w_loc = dict(zip(names, (norm_mix, w_in, b_gate, conv_w, conv_b, dt_bias, a_log, d_skip, ssm_norm, w_ssm_out,
                             w_att_out, w_mix_out, norm_ffn, w_ffn_gate, w_ffn_up, w_ffn_down, norm_final)))
    m_loc = dict(zip(names, (m_norm_mix, m_w_in, m_b_gate, m_conv_w, m_conv_b, m_dt_bias, m_a_log, m_d_skip,
                             m_ssm_norm, m_w_ssm_out, m_w_att_out, m_w_mix_out, m_norm_ffn, m_w_ffn_gate,
                             m_w_ffn_up, m_w_ffn_down, m_norm_final)))
    v_loc = dict(zip(names, (v_norm_mix, v_w_in, v_b_gate, v_conv_w, v_conv_b, v_dt_bias, v_a_log, v_d_skip,
                             v_ssm_norm, v_w_ssm_out, v_w_att_out, v_w_mix_out, v_norm_ffn, v_w_ffn_gate,
                             v_w_ffn_up, v_w_ffn_down, v_norm_final)))
    two_d = lambda a: a.reshape(a.shape[-2:]) if a.ndim >= 2 else a.reshape(1, -1)
    w2 = {n: two_d(a) for n, a in w_loc.items()}
    chip = 2 * lax.axis_index("x") + lax.axis_index("y")
    c = lax.axis_index("c")

    wire_shapes = {n: _wire_shard(w2[n], n).shape for n in SHARDED}
    true_rows = {n: wire_shapes[n][0] * wire_shapes[n][1] // PACK_W for n in SHARDED}
    seg_rows = {n: _rows(wire_shapes[n][0] * wire_shapes[n][1]) for n in SHARDED}
    buckets = {"first": ("w_in",), "late": tuple(n for n in SHARDED if n != "w_in")}
    rows_of = {b: _padded_rows(sum(seg_rows[n] for n in ns)) for b, ns in buckets.items()}

    def pack_shards(b):
        packed = _pack_rows([_wire_shard(w2[n], n).astype(BF16) for n in buckets[b]], rows_of[b])
        return packed.reshape(2, rows_of[b] // 2, PACK_W)

    def unpack_full(gathered, b):
        wg, out, off = gathered.reshape(N_CHIPS, rows_of[b], PACK_W), {}, 0
        for n in buckets[b]:
            rows, cols = wire_shapes[n]
            out[_wire_name(n)] = wg[:, off:off + true_rows[n]].reshape(N_CHIPS * rows, cols)
            off += seg_rows[n]
        return out

    def pack_grads(g, b):
        sections = [_pack_rows([g[_wire_name(n)].reshape(N_CHIPS, true_rows[n], PACK_W)[k] for n in buckets[b]],
                               rows_of[b]) for k in range(N_CHIPS)]
        return jnp.stack(sections).reshape(N_CHIPS, 2, rows_of[b] // 2, PACK_W)

    def chip_sums(g, b):
        g2 = pack_grads(g, b)
        return _add_own_half(g2, _swap_halves(g2, b), c, b)

    def finish(by_source, b):
        reduced = _join_halves(_sum_chips(by_source, b), b).reshape(rows_of[b], PACK_W)
        out, off = {}, 0
        for n in buckets[b]:
            out[n] = reduced[off:off + true_rows[n]].reshape(wire_shapes[n])
            off += seg_rows[n]
        return out

    full = {"w_in_t": _gather_weights(pack_shards("first")).reshape(N_CHIPS, rows_of["first"], PACK_W)}
    for n in SMALL:
        full[n] = w2[n]
    def swap_in(g):
        g2 = pack_grads(g, "first")
        return _swap_side(g2), g2

    overlap = _Overlap(_gather_side(pack_shards("late")), lambda outs: unpack_full(outs[0], "late"),
                       lambda g: _scatter_side(chip_sums(g, "late")), swap_in,
                       lambda g2, swapped: _scatter_side(_add_own_half(g2, swapped[0], c, "first")))

    n_conv = w2["conv_w"].shape[1]
    placed = lax.dynamic_update_slice_in_dim(jnp.zeros((CONV_K, N_CHIPS * n_conv), F32), w2["conv_w"], chip * n_conv, 1)
    placed = jnp.where(c == 0, placed, 0.0)
    full["conv_w"] = _allreduce_small(_pack_rows([placed], _rows(int(placed.size))), "gather_conv_w").reshape(
        -1)[:placed.size].reshape(placed.shape)

    loss_sum, grad_x, g_full, scattered, scattered_in = _local_step(x, loss_target, full, overlap)
    loss = lax.psum(loss_sum, ("x", "y", "c"))

    g_shard = {}
    small_names = SMALL + ("conv_w",)
    small_flat = jnp.concatenate([g_full[n].reshape(-1) for n in small_names])
    small = _allreduce_small(_pack_rows([small_flat], _rows(int(small_flat.size))), "allreduce_small").reshape(-1)
    off = 0
    for n in small_names:
        size = int(g_full[n].size)
        g_shard[n] = small[off:off + size].reshape(g_full[n].shape)
        off += size
    g_shard["conv_w"] = lax.dynamic_slice_in_dim(g_shard["conv_w"], chip * n_conv, n_conv, 1)

    g_shard.update(finish(scattered[0], "late"))
    g_shard.update(finish(scattered_in[0], "first"))

    grads, deltas, new_m, new_v = [], [], [], []
    for n in names:
        shape = w_loc[n].shape
        if n in COL_SHARDED:
            view = unview = lambda a: jnp.swapaxes(a, -1, -2)
        else:
            view, unview = ((lambda a: a) if len(shape) >= 2 else two_d), (lambda a: a.reshape(shape))
        gn = g_shard[n].reshape(view(w_loc[n]).shape)
        outs = _adamw(view(w_loc[n]), gn, view(m_loc[n]), view(v_loc[n]), "adamw_" + n)
        for acc, a in zip((grads, deltas, new_m, new_v), (gn, *outs)):
            acc.append(unview(a))
    return (loss, grad_x, *grads, *deltas, *new_m, *new_v)
```

```python
import functools
from typing import Callable, NamedTuple, Optional

import jax
import jax.numpy as jnp
from jax import lax
from jax.experimental import pallas as pl
from jax.experimental.pallas import tpu as pltpu

F32 = jnp.float32
BF16 = jnp.bfloat16
SDS = jax.ShapeDtypeStruct
MESH = pl.DeviceIdType.MESH

D_MODEL = 1024
D_INNER = 2048
N_HEADS = 32
HEAD_P = 64
N_GROUPS = 4
HEADS_PER_GROUP = N_HEADS // N_GROUPS
D_STATE = 128
CONV_K = 4
CHUNK = 128
CONV_DIM = D_INNER + 2 * N_GROUPS * D_STATE
GROUP_W = D_INNER // N_GROUPS + 2 * D_STATE
ATT_HEADS = 12
ATT_D = 128
ATT_SLOTS = 4
ATT_W = ATT_SLOTS * ATT_D
ATT_DILATIONS = (1, 4, 16)
ATT_BLOCK = 128
QKV_DIM = 3 * ATT_HEADS * ATT_D
DT_PAD = 128
ROPE_THETA = 10000.0
EPS = 1e-6
N_CHIPS = 4
LANES = 128

ADAM_LR = 0.001
ADAM_B1 = 0.9
ADAM_B2 = 0.999
ADAM_EPS = 1e-08
ADAM_WD = 0.01
ADAM_STEP = 10

VMEM_LIMIT = 48 * 1024 * 1024


def _cparams(semantics):
    return pltpu.CompilerParams(dimension_semantics=semantics, vmem_limit_bytes=VMEM_LIMIT)


def _pick(n, cap):
    best = None
    for t in range(LANES, min(n, cap) + 1, LANES):
        if n % t == 0:
            best = t
    return best or n


def _row_tile(rows, cap):
    best = None
    for t in range(8, min(rows, cap) + 1, 8):
        if rows % t == 0:
            best = t
    return best or rows


def _sigmoid(x):
    return pl.reciprocal(1.0 + jnp.exp(-x), approx=True)


def _softplus(x):
    return jnp.maximum(x, 0.0) + jnp.log(1.0 + jnp.exp(-jnp.abs(x)))


def _dot(a, b):
    return jnp.dot(a, b, preferred_element_type=F32)


def _dot_nt(a, b):
    return lax.dot_general(a, b, (((1,), (1,)), ((), ())), preferred_element_type=F32)


def _dot_tn(a, b):
    return lax.dot_general(a, b, (((0,), (0,)), ((), ())), preferred_element_type=F32)


def _mm(a, b, mode, out_dtype, name, add=None, side=None):
    if mode == "nn":
        (m, k), (_, n) = a.shape, b.shape
    elif mode == "nt":
        (m, k), (n, _) = a.shape, b.shape
    else:
        (k, m), (_, n) = a.shape, b.shape
    tm, tn = _pick(m, 1536), _pick(n, 2048)
    tk = k if k <= 2048 else _pick(k, 2048)
    nk = k // tk
    dims = {"nn": ((1,), (0,)), "nt": ((1,), (1,)), "tn": ((0,), (0,))}[mode]

    def partial_product(a_ref, b_ref):
        return lax.dot_general(a_ref[...].astype(BF16), b_ref[...].astype(BF16), (dims, ((), ())),
                               preferred_element_type=F32)

    def body(*refs):
        a_ref, b_ref = refs[:2]
        c_ref = refs[2] if add is not None else None
        o_ref = refs[3] if add is not None else refs[2]

        def finish(r):
            if add is not None:
                r = r + c_ref[...].astype(F32)
            o_ref[...] = r.astype(out_dtype)

        if nk == 1:
            finish(partial_product(a_ref, b_ref))
            return
        acc = refs[-1]
        kk = pl.program_id(2)

        @pl.when(kk == 0)
        def _():
            acc[...] = partial_product(a_ref, b_ref)

        @pl.when((kk > 0) & (kk < nk - 1))
        def _():
            acc[...] += partial_product(a_ref, b_ref)

        @pl.when(kk == nk - 1)
        def _():
            finish(acc[...] + partial_product(a_ref, b_ref))

    a_spec = {"nn": pl.BlockSpec((tm, tk), lambda j, i, q: (i, q)),
              "nt": pl.BlockSpec((tm, tk), lambda j, i, q: (i, q)),
              "tn": pl.BlockSpec((tk, tm), lambda j, i, q: (q, i))}[mode]
    b_spec = {"nn": pl.BlockSpec((tk, tn), lambda j, i, q: (q, j)),
              "nt": pl.BlockSpec((tn, tk), lambda j, i, q: (j, q)),
              "tn": pl.BlockSpec((tk, tn), lambda j, i, q: (q, j))}[mode]
    o_spec = pl.BlockSpec((tm, tn), lambda j, i, q: (i, j))
    ins, specs = [a, b], [a_spec, b_spec]
    if add is not None:
        ins.append(add)
        specs.append(o_spec)
    acc = [pltpu.VMEM((tm, tn), F32)] if nk > 1 else []
    grid = (n // tn, m // tm, nk)
    if side is None:
        return pl.pallas_call(
            body, name=name, grid=grid, in_specs=specs, out_specs=o_spec, out_shape=SDS((m, n), out_dtype),
            scratch_shapes=acc, compiler_params=_cparams(("parallel", "parallel", "arbitrary")))(*ins)
    return pl.pallas_call(
        _attach_side(body, len(ins), 1, side, grid), name=name, grid=grid,
        in_specs=specs + [ANY] * len(side.ins), out_specs=[o_spec] + [ANY] * len(side.out_shapes),
        out_shape=[SDS((m, n), out_dtype)] + list(side.out_shapes), scratch_shapes=acc + list(side.scratch),
        compiler_params=_cparams(("arbitrary", "arbitrary", "arbitrary")))(*ins, *side.ins)


def _mm_fused(a, b, mode, name, tm, epilogue, row_ins, full_ins, outs, n_acc=0, add=None):
    (m, k), n = a.shape, (b.shape[1] if mode == "nn" else b.shape[0])
    tk = k if k <= 2048 else _pick(k, 2048)
    nk = k // tk
    dims = {"nn": ((1,), (0,)), "nt": ((1,), (1,))}[mode]
    n_row, n_full, n_out = len(row_ins), len(full_ins), len(outs)

    def partial_product(a_ref, b_ref):
        return lax.dot_general(a_ref[...], b_ref[...], (dims, ((), ())), preferred_element_type=F32)

    def body(*refs):
        a_ref, b_ref = refs[:2]
        pos = 3 if add is not None else 2
        row_refs, full_refs = refs[pos:pos + n_row], refs[pos + n_row:pos + n_row + n_full]
        out_refs = refs[pos + n_row + n_full:pos + n_row + n_full + n_out]
        i, kk = pl.program_id(0), pl.program_id(1)

        def finish(r):
            if add is not None:
                r = r + refs[2][...].astype(F32)
            for q, (o_ref, v) in enumerate(zip(out_refs, epilogue(r, row_refs, full_refs))):
                if q < n_out - n_acc:
                    o_ref[...] = v.astype(o_ref.dtype)
                else:
                    @pl.when(i == 0)
                    def _(o_ref=o_ref, v=v):
                        o_ref[...] = v

                    @pl.when(i > 0)
                    def _(o_ref=o_ref, v=v):
                        o_ref[...] += v

        if nk == 1:
            finish(partial_product(a_ref, b_ref))
            return
        acc = refs[-1]

        @pl.when(kk == 0)
        def _():
            acc[...] = partial_product(a_ref, b_ref)

        @pl.when((kk > 0) & (kk < nk - 1))
        def _():
            acc[...] += partial_product(a_ref, b_ref)

        @pl.when(kk == nk - 1)
        def _():
            finish(acc[...] + partial_product(a_ref, b_ref))

    tile = lambda w, cb: pl.BlockSpec((tm, w), lambda i, q: (i, cb))
    b_spec = (pl.BlockSpec((tk, n), lambda i, q: (q, 0)) if mode == "nn" else pl.BlockSpec((n, tk), lambda i, q: (0, q)))
    specs = [pl.BlockSpec((tm, tk), lambda i, q: (i, q)), b_spec] + ([tile(n, 0)] if add is not None else [])
    specs += [tile(w, cb) for _, w, cb in row_ins]
    vec = lambda w, cb: pl.BlockSpec((1, w), lambda i, q: (0, cb))
    specs += [vec(w, cb) for _, w, cb in full_ins]
    out_specs = [tile(w, 0) for w, _ in outs[:n_out - n_acc]] + [vec(w, 0) for w, _ in outs[n_out - n_acc:]]
    out_shape = [SDS((m, w), dt) for w, dt in outs[:n_out - n_acc]] + [SDS((1, w), F32) for w, _ in outs[n_out - n_acc:]]
    ins = [a, b] + ([add] if add is not None else []) + [x for x, _, _ in row_ins] + [x for x, _, _ in full_ins]
    return pl.pallas_call(
        body, name=name, grid=(m // tm, nk), in_specs=specs, out_specs=out_specs, out_shape=out_shape,
        scratch_shapes=[pltpu.VMEM((tm, n), F32)] if nk > 1 else [],
        compiler_params=_cparams(("arbitrary", "arbitrary")))(*ins)


def _rw(name, fn, nsteps, ins, outs, n_acc=0):
    n_in, n_out = len(ins), len(outs)

    def body(*refs):
        i = pl.program_id(0)
        vals = fn(i, *refs[:n_in])
        for q, (r, v) in enumerate(zip(refs[n_in:], vals)):
            if q < n_out - n_acc:
                r[...] = v.astype(r.dtype)
            else:
                @pl.when(i == 0)
                def _(r=r):
                    r[...] = jnp.zeros_like(r)

                r[...] += v

    return pl.pallas_call(
        body, name=name, grid=(nsteps,), in_specs=[s for _, s in ins], out_specs=[s for _, s in outs],
        out_shape=[o for o, _ in outs], compiler_params=_cparams(("arbitrary",)))(*[a for a, _ in ins])


def _rs(tm, w, cb=0):
    return pl.BlockSpec((tm, w), lambda i: (i, cb))


def _fs(shape):
    nd = len(shape)
    return pl.BlockSpec(shape, lambda i: (0,) * nd)


def _colsum(v):
    return jnp.sum(v, axis=0, keepdims=True)


def _rms_fwd(x, g, name):
    t, d = x.shape
    tm = 512

    def fn(i, x_ref, g_ref):
        xv = x_ref[...]
        r = lax.rsqrt(jnp.mean(xv * xv, axis=-1, keepdims=True) + EPS)
        return [xv * r * g_ref[...]]

    return _rw(name, fn, t // tm, [(x, _rs(tm, d)), (g, _fs((1, d)))], [(SDS((t, d), BF16), _rs(tm, d))])[0]


def _rms_bwd_values(xv, dhv, gv, dres):
    r = lax.rsqrt(jnp.mean(xv * xv, axis=-1, keepdims=True) + EPS)
    xhat = xv * r
    dxhat = dhv * gv
    dx = dres + r * (dxhat - xhat * jnp.mean(dxhat * xhat, axis=-1, keepdims=True))
    return [dx, dx, _colsum(dhv * xhat)]


def _final_values(xv, target, gv):
    d = xv.shape[-1]
    r = lax.rsqrt(jnp.mean(xv * xv, axis=-1, keepdims=True) + EPS)
    xhat = xv * r
    diff = xhat * gv - target
    lsum = 0.5 * jnp.sum(jnp.sum(diff * diff, axis=-1, keepdims=True) * (1.0 / d), axis=0, keepdims=True)
    dy = diff * (1.0 / d)
    dxhat = dy * gv
    dx = r * (dxhat - xhat * jnp.mean(dxhat * xhat, axis=-1, keepdims=True))
    return [dx, dx, _colsum(dy * xhat), lsum]


CONV_TS = 512
CONV_HALO = 8


def _conv_specs(seq, c):
    ts, tc = CONV_TS, GROUP_W
    hb = ts // CONV_HALO
    u_spec = pl.BlockSpec((ts, tc), lambda j, i: (i, j))
    prev_spec = pl.BlockSpec((CONV_HALO, tc), lambda j, i: (jnp.maximum(i * hb - 1, 0), j))
    w_spec = pl.BlockSpec((CONV_K, tc), lambda j, i: (0, j))
    b_spec = pl.BlockSpec((1, tc), lambda j, i: (0, j))
    return u_spec, prev_spec, w_spec, b_spec


CONV_PIECE = 32


def _conv_fill(i, seq, u_ref, prev_ref, ext):
    first = (i % (seq // CONV_TS)) == 0
    ext[0:CONV_HALO, :] = jnp.where(first, 0.0, prev_ref[...])
    ext[CONV_HALO:, :] = u_ref[...]


def _conv_piece(ext, r0, wv, bv):
    lo = r0 + CONV_HALO - CONV_K + 1
    taps = [ext[lo + q:lo + q + CONV_PIECE, :] for q in range(CONV_K)]
    pre = bv
    for q, tap in enumerate(taps):
        pre = pre + wv[q:q + 1] * tap
    return taps, pre


def _conv_fwd(u, w, b, seq):
    t, c = u.shape
    ts, tc = CONV_TS, GROUP_W
    u_spec, prev_spec, w_spec, b_spec = _conv_specs(seq, c)

    def body(u_ref, prev_ref, w_ref, b_ref, o_ref, ext):
        _conv_fill(pl.program_id(1), seq, u_ref, prev_ref, ext)
        wv, bv = w_ref[...], b_ref[...]
        for r0 in range(0, ts, CONV_PIECE):
            _, pre = _conv_piece(ext, r0, wv, bv)
            o_ref[r0:r0 + CONV_PIECE, :] = pre * _sigmoid(pre)

    return pl.pallas_call(
        body, name="conv_fwd", grid=(c // tc, t // ts), in_specs=[u_spec, prev_spec, w_spec, b_spec],
        out_specs=u_spec, out_shape=SDS((t, c), F32), scratch_shapes=[pltpu.VMEM((ts + CONV_HALO, tc), F32)],
        compiler_params=_cparams(("parallel", "arbitrary")))(u, u, w, b)


def _conv_bwd_pre(u, w, b, dxc, seq):
    t, c = u.shape
    ts, tc = CONV_TS, GROUP_W
    u_spec, prev_spec, w_spec, b_spec = _conv_specs(seq, c)

    def body(u_ref, prev_ref, w_ref, b_ref, d_ref, dpre_ref, dw_ref, db_ref, ext):
        i = pl.program_id(1)
        _conv_fill(i, seq, u_ref, prev_ref, ext)
        wv, bv = w_ref[...], b_ref[...]
        fold = lambda v: sum(v[8 * s:8 * (s + 1)] for s in range(CONV_PIECE // 8))
        sums = [jnp.zeros((8, tc), F32)] * (CONV_K + 1)
        for r0 in range(0, ts, CONV_PIECE):
            taps, pre = _conv_piece(ext, r0, wv, bv)
            sg = _sigmoid(pre)
            dpre = d_ref[r0:r0 + CONV_PIECE, :] * sg * (1.0 + pre * (1.0 - sg))
            dpre_ref[r0:r0 + CONV_PIECE, :] = dpre
            sums = [s + fold(dpre * f) for s, f in zip(sums, taps + [1.0])]

        @pl.when(i == 0)
        def _():
            dw_ref[...] = jnp.zeros_like(dw_ref)
            db_ref[...] = jnp.zeros_like(db_ref)

        db_ref[...] += _colsum(sums[CONV_K])
        for q in range(CONV_K):
            dw_ref[q:q + 1, :] += _colsum(sums[q])

    return pl.pallas_call(
        body, name="conv_bwd_pre", grid=(c // tc, t // ts),
        in_specs=[u_spec, prev_spec, w_spec, b_spec, u_spec], out_specs=[u_spec, w_spec, b_spec],
        out_shape=[SDS((t, c), F32), SDS((CONV_K, c), F32), SDS((1, c), F32)],
        scratch_shapes=[pltpu.VMEM((ts + CONV_HALO, tc), F32)],
        compiler_params=_cparams(("parallel", "arbitrary")))(u, u, w, b, dxc)


def _conv_bwd_in(dpre, w, seq):
    t, c = dpre.shape
    ts, tc = CONV_TS, GROUP_W
    hb = ts // CONV_HALO
    last = t // CONV_HALO - 1
    d_spec = pl.BlockSpec((ts, tc), lambda j, i: (i, j))
    next_spec = pl.BlockSpec((CONV_HALO, tc), lambda j, i: (jnp.minimum((i + 1) * hb, last), j))
    w_spec = pl.BlockSpec((CONV_K, tc), lambda j, i: (0, j))

    def body(d_ref, next_ref, w_ref, o_ref, ext):
        i = pl.program_id(1)
        nts = seq // ts
        is_last = (i % nts) == nts - 1
        ext[0:ts, :] = d_ref[...]
        ext[ts:, :] = jnp.where(is_last, 0.0, next_ref[...])
        wv = w_ref[...]
        for r0 in range(0, ts, CONV_PIECE):
            acc = wv[CONV_K - 1:CONV_K] * ext[r0:r0 + CONV_PIECE, :]
            for q in range(CONV_K - 1):
                lo = r0 + CONV_K - 1 - q
                acc = acc + wv[q:q + 1] * ext[lo:lo + CONV_PIECE, :]
            o_ref[r0:r0 + CONV_PIECE, :] = acc.astype(o_ref.dtype)

    return pl.pallas_call(
        body, name="conv_bwd_in", grid=(c // tc, t // ts), in_specs=[d_spec, next_spec, w_spec],
        out_specs=d_spec, out_shape=SDS((t, c), BF16), scratch_shapes=[pltpu.VMEM((ts + CONV_HALO, tc), F32)],
        compiler_params=_cparams(("parallel", "arbitrary")))(dpre, dpre, w)


def _split3(v):
    hi = v.astype(BF16)
    r1 = v - hi.astype(F32)
    mid = r1.astype(BF16)
    lo = (r1 - mid.astype(F32)).astype(BF16)
    return hi, mid, lo


def _ssd_prelude(dtr_ref, dtrt_ref, bias_ref, biast_ref, a_ref, at_ref):
    dt = _softplus(dtr_ref[...] + bias_ref[...])
    dtt = _softplus(dtrt_ref[...] + biast_ref[...])
    ri = lax.broadcasted_iota(jnp.int32, (CHUNK, CHUNK), 0)
    ci = lax.broadcasted_iota(jnp.int32, (CHUNK, CHUNK), 1)
    lower = ri >= ci
    upper = ri <= ci
    lower_b = jnp.where(lower, 1.0, 0.0).astype(BF16)
    upper_b = jnp.where(upper, 1.0, 0.0).astype(BF16)
    acs = sum(_dot(lower_b, p) for p in _split3(dt * a_ref[...]))
    acst = sum(_dot(p, upper_b) for p in _split3(dtt * at_ref[...]))
    return dt, acs, acst, lower, upper, lower_b, upper_b


SSD_FWD_GPS = 4
SSD_BWD_GPS = 1


def _ssd_specs(seq, gps):
    nc = seq // CHUNK
    hg = HEADS_PER_GROUP
    fwd = lambda c: c
    rev = lambda c: nc - 1 - c

    def specs(cc):
        return dict(
            xc=pl.BlockSpec((CHUNK, gps * GROUP_W), lambda g, b, c: (b * nc + cc(c), g)),
            y=pl.BlockSpec((CHUNK, gps * hg * HEAD_P), lambda g, b, c: (b * nc + cc(c), g)),
            dtr=pl.BlockSpec((gps, CHUNK, hg), lambda g, b, c: (g, b * nc + cc(c), 0)),
            dtrt=pl.BlockSpec((gps, None, hg, CHUNK), lambda g, b, c: (g, b, 0, cc(c))),
            prow=pl.BlockSpec((gps, 1, hg), lambda g, b, c: (g, 0, 0)),
            pcol=pl.BlockSpec((gps, hg, 1), lambda g, b, c: (g, 0, 0)),
            st=pl.BlockSpec((gps, None, None, D_STATE, hg * HEAD_P), lambda g, b, c: (g, b, cc(c), 0, 0)),
        )

    return specs(fwd), specs(rev)


def _group_views(refs, lane_widths, gi):
    return [r.at[:, gi * w:(gi + 1) * w] if w else r.at[gi] for r, w in zip(refs, lane_widths)]


def _head_maps():
    hw = HEADS_PER_GROUP * HEAD_P
    shift = HEAD_P.bit_length() - 1
    hj = lax.broadcasted_iota(jnp.int32, (HEADS_PER_GROUP, hw), 0)
    lq = jnp.right_shift(lax.broadcasted_iota(jnp.int32, (HEADS_PER_GROUP, hw), 1), shift)
    spread = jnp.where(hj == lq, 1.0, 0.0).astype(BF16)
    rq = jnp.right_shift(lax.broadcasted_iota(jnp.int32, (hw, LANES), 0), shift)
    cj = lax.broadcasted_iota(jnp.int32, (hw, LANES), 1)
    gather = jnp.where(rq == cj, 1.0, 0.0).astype(BF16)
    return spread, gather


def _dot01(v, m01):
    hi, mid, _ = _split3(v)
    return _dot(hi, m01) + _dot(mid, m01)


class _Side(NamedTuple):
    ins: tuple
    out_shapes: tuple
    scratch: tuple
    first: Callable
    mid: Optional[Callable]
    last: Callable


def _attach_side(body, n_in, n_out, side, grid):
    si, so, ss = len(side.ins), len(side.out_shapes), len(side.scratch)

    def wrapped(*refs):
        ins, s_in = refs[:n_in], refs[n_in:n_in + si]
        outs = refs[n_in + si:n_in + si + n_out]
        s_out = refs[n_in + si + n_out:n_in + si + n_out + so]
        rest = refs[n_in + si + n_out + so:]
        scr, s_scr = rest[:len(rest) - ss], rest[len(rest) - ss:]
        ids = [pl.program_id(a) for a in range(len(grid))]
        inner_first = functools.reduce(lambda p, q: p & q, [i == 0 for i in ids[1:]], ids[0] >= 0)
        at_last = functools.reduce(lambda p, q: p & q, [i == n - 1 for i, n in zip(ids, grid)])

        @pl.when((ids[0] == 0) & inner_first)
        def _():
            side.first(s_in, s_out, s_scr)

        if side.mid is not None:
            outer_last = functools.reduce(lambda p, q: p & q, [i == n - 1 for i, n in zip(ids[:-1], grid[:-1])])

            @pl.when(outer_last & (ids[-1] == grid[-1] // 2))
            def _():
                side.mid(s_in, s_out, s_scr)

        body(*ins, *outs, *scr)

        @pl.when(at_last)
        def _():
            side.last(s_in, s_out, s_scr)

    return wrapped


def _ssd_fwd(xc, dtr, dtrt, bias, biast, a, at, dskip, nb, seq, side):
    t = xc.shape[0]
    nc = seq // CHUNK
    hg = HEADS_PER_GROUP
    hw = hg * HEAD_P
    gps = SSD_FWD_GPS
    grid = (N_GROUPS // gps, nb, nc)
    sp, _ = _ssd_specs(seq, gps)

    def body(*refs):
        for gi in range(gps):
            one_group(*_group_views(refs, (GROUP_W, 0, 0, 0, 0, 0, 0, 0, hw, 0, 0), gi))

    def one_group(xc_ref, dtr_ref, dtrt_ref, bias_ref, biast_ref, a_ref, at_ref, d_ref, y_ref, sin_ref, st):
        @pl.when(pl.program_id(2) == 0)
        def _():
            st[...] = jnp.zeros_like(st)

        s_in = st[...]
        sin_ref[...] = s_in
        dt, acs, acst, lower, _, _, _ = _ssd_prelude(dtr_ref, dtrt_ref, bias_ref, biast_ref, a_ref, at_ref)
        spread, _ = _head_maps()
        x = xc_ref[...]
        xs = x[:, :hw]
        b16 = x[:, hw:hw + D_STATE].astype(BF16)
        c16 = x[:, hw + D_STATE:].astype(BF16)
        cb = _dot_nt(c16, b16)
        last = acs[CHUNK - 1:CHUNK, :]
        e_x = _dot01(jnp.exp(acs), spread)
        dec_x = _dot01(jnp.exp(last - acs), spread)
        tot_x = e_x[CHUNK - 1:CHUNK, :]
        d_x = _dot01(jnp.broadcast_to(d_ref[...], (8, hg)), spread)[0:1, :]
        xdtf = xs * _dot01(dt, spread)
        xdt16 = xdtf.astype(BF16)
        yoff = e_x * _dot(c16, s_in.astype(BF16))
        st[...] = tot_x * s_in + _dot_tn(b16, (dec_x * xdtf).astype(BF16))
        parts = []
        for j in range(hg):
            decay = jnp.exp(jnp.where(lower, acs[:, j:j + 1] - acst[j:j + 1, :], -jnp.inf))
            parts.append(_dot((cb * decay).astype(BF16), xdt16[:, HEAD_P * j:HEAD_P * (j + 1)]))
        y_ref[...] = (jnp.concatenate(parts, axis=-1) + yoff + d_x * xs).astype(y_ref.dtype)

    return pl.pallas_call(
        _attach_side(body, 8, 2, side, grid), name="ssd_fwd", grid=grid,
        in_specs=[sp["xc"], sp["dtr"], sp["dtrt"], sp["prow"], sp["pcol"], sp["prow"], sp["pcol"], sp["prow"]]
        + [ANY] * len(side.ins),
        out_specs=[sp["y"], sp["st"]] + [ANY] * len(side.out_shapes),
        out_shape=[SDS((t, D_INNER), BF16), SDS((N_GROUPS, nb, nc, D_STATE, hw), F32)] + list(side.out_shapes),
        scratch_shapes=[pltpu.VMEM((gps, D_STATE, hw), F32)] + list(side.scratch),
        compiler_params=_cparams(("arbitrary", "arbitrary", "arbitrary")))(
            xc, dtr, dtrt, bias, biast, a, at, dskip, *side.ins)


def _ssd_bwd(xc, dtr, dtrt, bias, biast, a, at, dskip, states, dy, nb, seq, side):
    t = xc.shape[0]
    nc = seq // CHUNK
    hg = HEADS_PER_GROUP
    hw = hg * HEAD_P
    gps = SSD_BWD_GPS
    grid = (N_GROUPS // gps, nb, nc)
    _, sp = _ssd_specs(seq, gps)

    def body(*refs):
        for gi in range(gps):
            one_group(*_group_views(refs, (GROUP_W, 0, 0, 0, 0, 0, 0, 0, 0, hw, GROUP_W, 0, 0, 0, 0, 0), gi))

    def one_group(xc_ref, dtr_ref, dtrt_ref, bias_ref, biast_ref, a_ref, at_ref, d_ref, sin_ref, dy_ref,
                  dxc_ref, ddtr_ref, gbias_ref, ga_ref, gd_ref, ds):
        first = (pl.program_id(1) == 0) & (pl.program_id(2) == 0)

        @pl.when(pl.program_id(2) == 0)
        def _():
            ds[...] = jnp.zeros_like(ds)

        @pl.when(first)
        def _():
            gbias_ref[...] = jnp.zeros_like(gbias_ref)
            ga_ref[...] = jnp.zeros_like(ga_ref)
            gd_ref[...] = jnp.zeros_like(gd_ref)

        dt, acs, acst, lower, upper, _, upper_b = _ssd_prelude(dtr_ref, dtrt_ref, bias_ref, biast_ref, a_ref, at_ref)
        spread, gather = _head_maps()
        x = xc_ref[...]
        dy = dy_ref[...].astype(F32)
        xs = x[:, :hw]
        b16 = x[:, hw:hw + D_STATE].astype(BF16)
        c16 = x[:, hw + D_STATE:].astype(BF16)
        dy16 = dy.astype(BF16)
        cb = _dot_nt(c16, b16)
        cbt = _dot_nt(b16, c16)
        last = acs[CHUNK - 1:CHUNK, :]
        e8 = jnp.exp(acs)
        dec8 = jnp.exp(last - acs)
        e_x = _dot01(e8, spread)
        dec_x = _dot01(dec8, spread)
        tot_x = e_x[CHUNK - 1:CHUNK, :]
        dt_x = _dot01(dt, spread)
        d_x = _dot01(jnp.broadcast_to(d_ref[...], (8, hg)), spread)[0:1, :]
        xdtf = xs * dt_x
        xdt16 = xdtf.astype(BF16)
        s_in = sin_ref[...]
        s16 = s_in.astype(BF16)
        ds_out = ds[...]
        ds16 = ds_out.astype(BF16)
        bds = _dot(b16, ds16)
        cs = _dot(c16, s16)
        edy16 = (e_x * dy).astype(BF16)
        ds[...] = tot_x * ds_out + _dot_tn(c16, edy16)
        lane8 = lax.broadcasted_iota(jnp.int32, (CHUNK, hg), 1)
        row8 = lax.broadcasted_iota(jnp.int32, (CHUNK, hg), 0)
        dacs8 = jnp.zeros((CHUNK, hg), F32)
        acc_m = jnp.zeros((CHUNK, CHUNK), F32)
        acc_mt = jnp.zeros((CHUNK, CHUNK), F32)
        dx_parts = []
        for j in range(hg):
            sl = slice(HEAD_P * j, HEAD_P * (j + 1))
            col = acs[:, j:j + 1]
            row = acst[j:j + 1, :]
            decay = jnp.exp(jnp.where(lower, col - row, -jnp.inf))
            decayt = jnp.exp(jnp.where(upper, row - col, -jnp.inf))
            wm = _dot_nt(dy16[:, sl], xdt16[:, sl]) * decay
            wmt = _dot_nt(xdt16[:, sl], dy16[:, sl]) * decayt
            acc_m = acc_m + wm
            acc_mt = acc_mt + wmt
            dacs8 = dacs8 + jnp.where(lane8 == j, jnp.sum(wm * cb, axis=-1, keepdims=True)
                                      - jnp.sum(wmt * cbt, axis=-1, keepdims=True), 0.0)
            dx_parts.append(_dot((cbt * decayt).astype(BF16), dy16[:, sl]))
        dx = jnp.concatenate(dx_parts, axis=-1) + dec_x * bds
        dxc_ref[:, :hw] = dx * dt_x + d_x * dy
        dxc_ref[:, hw:hw + D_STATE] = _dot(acc_mt.astype(BF16), c16) + _dot_nt((dec_x * xdtf).astype(BF16), ds16)
        dxc_ref[:, hw + D_STATE:] = _dot(acc_m.astype(BF16), b16) + _dot_nt(edy16, s16)
        dtot_rows = jnp.broadcast_to(_colsum(ds_out * s_in), (8, hw))
        sums = _dot01(jnp.concatenate([dy * cs, xdtf * bds, dx * xs, dy * xs, dtot_rows], axis=0), gather)
        de8 = sums[0:CHUNK, :hg]
        ddec8 = sums[CHUNK:2 * CHUNK, :hg]
        ddtx8 = sums[2 * CHUNK:3 * CHUNK, :hg]
        gd8 = _colsum(sums[3 * CHUNK:4 * CHUNK, :hg])
        dtot8 = sums[4 * CHUNK:4 * CHUNK + 1, :hg]
        extra = _colsum(ddec8 * dec8) + dtot8 * e8[CHUNK - 1:CHUNK, :]
        dacs8 = dacs8 + de8 * e8 - ddec8 * dec8 + jnp.where(row8 == CHUNK - 1, extra, 0.0)
        da = sum(_dot(upper_b, p) for p in _split3(dacs8))
        av = a_ref[...]
        ddt = da * av + ddtx8
        ddtr = ddt * _sigmoid(dtr_ref[...] + bias_ref[...])
        ddtr_ref[...] = ddtr
        gbias_ref[...] += _colsum(ddtr)
        ga_ref[...] += _colsum(da * dt) * av
        gd_ref[...] += gd8

    return pl.pallas_call(
        _attach_side(body, 10, 5, side, grid), name="ssd_bwd", grid=grid,
        in_specs=[sp["xc"], sp["dtr"], sp["dtrt"], sp["prow"], sp["pcol"], sp["prow"], sp["pcol"], sp["prow"],
                  sp["st"], sp["y"]] + [ANY] * len(side.ins),
        out_specs=[sp["xc"], sp["dtr"], sp["prow"], sp["prow"], sp["prow"]] + [ANY] * len(side.out_shapes),
        out_shape=[SDS((t, N_GROUPS * GROUP_W), F32), SDS((N_GROUPS, t, hg), F32)]
        + [SDS((N_GROUPS, 1, hg), F32)] * 3 + list(side.out_shapes),
        scratch_shapes=[pltpu.VMEM((gps, D_STATE, hw), F32)] + list(side.scratch),
        compiler_params=_cparams(("arbitrary", "arbitrary", "arbitrary")))(
            xc, dtr, dtrt, bias, biast, a, at, dskip, states, dy, *side.ins)


def _group_bcast(v, width, fn):
    parts = []
    for q in range(v.shape[-1] // width):
        s = fn(v[:, q * width:(q + 1) * width])
        parts.append(jnp.broadcast_to(s, (v.shape[0], width)))
    return jnp.concatenate(parts, axis=-1)


def _gate_norm_out(y, z, g, w):
    t, d = y.shape
    n = w.shape[1]
    tm = 256
    gw = d // N_GROUPS

    def body(y_ref, z_ref, g_ref, w_ref, yn_ref, o_ref):
        zv = z_ref[...].astype(F32)
        u = y_ref[...].astype(F32) * (zv * _sigmoid(zv))
        r = lax.rsqrt(_group_bcast(u * u, gw, lambda p: jnp.mean(p, axis=-1, keepdims=True)) + EPS)
        yn = (u * r * g_ref[...]).astype(yn_ref.dtype)
        yn_ref[...] = yn
        o_ref[...] = _dot(yn, w_ref[...]).astype(o_ref.dtype)

    return pl.pallas_call(
        body, name="gate_norm_ssm_out", grid=(t // tm,),
        in_specs=[_rs(tm, d), _rs(tm, d), _fs((1, d)), _fs((d, n))], out_specs=[_rs(tm, d), _rs(tm, n)],
        out_shape=[SDS((t, d), BF16), SDS((t, n), BF16)], compiler_params=_cparams(("arbitrary",)))(y, z, g, w)


def _gate_norm_bwd_epilogue(dv, rows, fulls):
    yv, zv = rows[0][...].astype(F32), rows[1][...].astype(F32)
    gw = yv.shape[-1] // N_GROUPS
    sg = _sigmoid(zv)
    sz = zv * sg
    u = yv * sz
    r = lax.rsqrt(_group_bcast(u * u, gw, lambda p: jnp.mean(p, axis=-1, keepdims=True)) + EPS)
    uhat = u * r
    duhat = dv * fulls[0][...]
    du = r * (duhat - uhat * _group_bcast(duhat * uhat, gw, lambda p: jnp.mean(p, axis=-1, keepdims=True)))
    dz = du * yv * sg * (1.0 + zv * (1.0 - sg))
    return [du * sz, dz, _colsum(dv * uhat)]


def _rope_tables(seq):
    half = ATT_D // 2
    inv = ROPE_THETA ** (-jnp.arange(half, dtype=F32) / half)
    ang = jnp.arange(seq, dtype=F32)[:, None] * inv[None, :]
    cos, sin = jnp.cos(ang), jnp.sin(ang)
    return jnp.concatenate([cos, cos], axis=-1), jnp.concatenate([-sin, sin], axis=-1)


ATT_TILE = 512
ATT_QB = 8


def _strided_spec(r, mtiles):
    return pl.BlockSpec((None, r, None, ATT_TILE // r, ATT_W), lambda i: (i // mtiles, 0, i % mtiles, 0, 0))


def _strided_shape(nb, r, mtiles, dtype):
    return SDS((nb, r, mtiles, ATT_TILE // r, ATT_W), dtype)


def _to_strided(val, out_ref, lanes, r, sc):
    if r == 1:
        out_ref[0, :, lanes] = val.astype(out_ref.dtype)
        return
    sc[...] = val
    for rr in range(r):
        out_ref[rr, :, lanes] = sc[pl.ds(rr, ATT_TILE // r, stride=r), :].astype(out_ref.dtype)


def _from_strided(in_ref, lanes, r, sc):
    if r == 1:
        return in_ref[0, :, lanes].astype(F32)
    for rr in range(r):
        sc[pl.ds(rr, ATT_TILE // r, stride=r), :] = in_ref[rr, :, lanes].astype(F32)
    return sc[...]


def _rope_fwd(qkv, cos, sin, nb, seq):
    t = qkv.shape[0]
    tm = ATT_TILE
    mtiles = seq // tm
    w = ATT_HEADS * ATT_D
    tab = pl.BlockSpec((tm, ATT_D), lambda i: (i % mtiles, 0))
    ng = len(ATT_DILATIONS)

    def body(q_ref, k_ref, v_ref, cos_ref, sin_ref, *rest):
        outs, sc = rest[:3 * ng], rest[3 * ng]
        c, s = cos_ref[...], sin_ref[...]
        for which, ref in enumerate((q_ref, k_ref, v_ref)):
            for h in range(ATT_HEADS):
                g, slot = divmod(h, ATT_SLOTS)
                p = ref[:, h * ATT_D:(h + 1) * ATT_D].astype(F32)
                if which < 2:
                    p = p * c + pltpu.roll(p, ATT_D // 2, 1) * s
                _to_strided(p, outs[which * ng + g], slice(slot * ATT_D, (slot + 1) * ATT_D), ATT_DILATIONS[g], sc)

    out_specs = [_strided_spec(r, mtiles) for _ in range(3) for r in ATT_DILATIONS]
    out_shape = [_strided_shape(nb, r, mtiles, BF16) for _ in range(3) for r in ATT_DILATIONS]
    outs = pl.pallas_call(
        body, name="rope_fwd", grid=(t // tm,),
        in_specs=[_rs(tm, w, 0), _rs(tm, w, 1), _rs(tm, w, 2), tab, tab], out_specs=out_specs, out_shape=out_shape,
        scratch_shapes=[pltpu.VMEM((tm, ATT_D), F32)], compiler_params=_cparams(("arbitrary",)))(
            qkv, qkv, qkv, cos, sin)
    flat = [o.reshape(t, ATT_W) for o in outs]
    return flat[0:ng], flat[ng:2 * ng], flat[2 * ng:]


def _rope_bwd(dq, dk, dv, cos, sin, nb, seq):
    t = dq[0].shape[0]
    tm = ATT_TILE
    mtiles = seq // tm
    w = ATT_HEADS * ATT_D
    tab = pl.BlockSpec((tm, ATT_D), lambda i: (i % mtiles, 0))
    ng = len(ATT_DILATIONS)

    def body(*refs):
        ins, (cos_ref, sin_ref, o_ref, sc) = refs[:3 * ng], refs[3 * ng:]
        c, s = cos_ref[...], sin_ref[...]
        for which in range(3):
            for h in range(ATT_HEADS):
                g, slot = divmod(h, ATT_SLOTS)
                p = _from_strided(ins[which * ng + g], slice(slot * ATT_D, (slot + 1) * ATT_D), ATT_DILATIONS[g], sc)
                if which < 2:
                    p = p * c - pltpu.roll(p, ATT_D // 2, 1) * s
                o_ref[:, which * w + h * ATT_D:which * w + (h + 1) * ATT_D] = p.astype(o_ref.dtype)

    views = [a.reshape(nb, r, mtiles, tm // r, ATT_W) for grp in (dq, dk, dv) for a, r in zip(grp, ATT_DILATIONS)]
    return pl.pallas_call(
        body, name="rope_bwd", grid=(t // tm,),
        in_specs=[_strided_spec(r, mtiles) for _ in range(3) for r in ATT_DILATIONS] + [tab, tab],
        out_specs=_rs(tm, 3 * w), out_shape=SDS((t, 3 * w), BF16),
        scratch_shapes=[pltpu.VMEM((tm, ATT_D), F32)], compiler_params=_cparams(("arbitrary",)))(*views, cos, sin)


def _att_masks():
    ri = lax.broadcasted_iota(jnp.int32, (ATT_BLOCK, ATT_BLOCK), 0)
    ci = lax.broadcasted_iota(jnp.int32, (ATT_BLOCK, ATT_BLOCK), 1)
    return ci <= ri, ci >= ri


def _att_fwd(q, k, v, g, seq):
    t, w = q.shape
    rows = ATT_QB * ATT_BLOCK
    nbs = seq // ATT_DILATIONS[g] // ATT_BLOCK
    scale = ATT_D ** -0.5
    cur = pl.BlockSpec((rows, w), lambda n: (n, 0))
    prev = pl.BlockSpec((ATT_BLOCK, w), lambda n: (jnp.maximum(n * ATT_QB - 1, 0), 0))

    def body(q_ref, kc_ref, kp_ref, vc_ref, vp_ref, o_ref, lse_ref):
        mcur, mprev = _att_masks()
        for i in range(ATT_QB):
            blk = pl.program_id(0) * ATT_QB + i
            mask = jnp.concatenate([mprev & ((blk % nbs) != 0), mcur], axis=-1)
            own = slice(i * ATT_BLOCK, (i + 1) * ATT_BLOCK)
            for h in range(ATT_SLOTS):
                sl = slice(h * ATT_D, (h + 1) * ATT_D)
                if i == 0:
                    keys = jnp.concatenate([kp_ref[:, sl], kc_ref[own, sl]], axis=0)
                    vals = jnp.concatenate([vp_ref[:, sl], vc_ref[own, sl]], axis=0)
                else:
                    both = slice((i - 1) * ATT_BLOCK, (i + 1) * ATT_BLOCK)
                    keys, vals = kc_ref[both, sl], vc_ref[both, sl]
                s = jnp.where(mask, _dot_nt(q_ref[own, sl], keys) * scale, -jnp.inf)
                m = jnp.max(s, axis=-1, keepdims=True)
                p = jnp.exp(s - m)
                den = jnp.sum(p, axis=-1, keepdims=True)
                o_ref[own, sl] = _dot(p.astype(BF16), vals) / den
                lse_ref[own, sl] = jnp.broadcast_to(m + jnp.log(den), (ATT_BLOCK, ATT_D))

    return pl.pallas_call(
        body, name=f"att_fwd_{g}", grid=(t // rows,), in_specs=[cur, cur, prev, cur, prev], out_specs=[cur, cur],
        out_shape=[SDS((t, w), F32), SDS((t, w), F32)],
        compiler_params=_cparams(("arbitrary",)))(q, k, k, v, v)


def _att_bwd(q, k, v, do, lse, dlt, g, seq):
    t, w = q.shape
    nblk = t // ATT_BLOCK
    rows = ATT_QB * ATT_BLOCK
    nbs = seq // ATT_DILATIONS[g] // ATT_BLOCK
    scale = ATT_D ** -0.5
    cur = pl.BlockSpec((rows, w), lambda n: (n, 0))
    nxt = pl.BlockSpec((ATT_BLOCK, w), lambda n: (jnp.minimum((n + 1) * ATT_QB, nblk - 1), 0))

    def body(qc_ref, qn_ref, k_ref, v_ref, doc_ref, don_ref, lsec_ref, lsen_ref, dltc_ref, dltn_ref,
             dq_ref, dk_ref, dv_ref, carry):
        n = pl.program_id(0)

        @pl.when(n == 0)
        def _():
            carry[...] = jnp.zeros_like(carry)

        mcur, mprev = _att_masks()

        def pair(cur_ref, nxt_ref, i, sl):
            if i + 1 < ATT_QB:
                return cur_ref[i * ATT_BLOCK:(i + 2) * ATT_BLOCK, sl]
            return jnp.concatenate([cur_ref[i * ATT_BLOCK:, sl], nxt_ref[:, sl]], axis=0)

        for h in range(ATT_SLOTS):
            sl = slice(h * ATT_D, (h + 1) * ATT_D)
            from_prev = carry[:, sl]
            for i in range(ATT_QB):
                blk = n * ATT_QB + i
                has_next = (((blk + 1) % nbs) != 0) & (blk + 1 < nblk)
                mask = jnp.concatenate([mcur, mprev & has_next], axis=0)
                own = slice(i * ATT_BLOCK, (i + 1) * ATT_BLOCK)
                kh, vh = k_ref[own, sl], v_ref[own, sl]
                qs, dos = pair(qc_ref, qn_ref, i, sl), pair(doc_ref, don_ref, i, sl)
                lse, dlt = pair(lsec_ref, lsen_ref, i, sl), pair(dltc_ref, dltn_ref, i, sl)
                p = jnp.where(mask, jnp.exp(_dot_nt(qs, kh) * scale - lse), 0.0)
                ds = (p * (_dot_nt(dos, vh) - dlt) * scale).astype(BF16)
                dqs = _dot(ds, kh)
                dq_ref[own, sl] = (from_prev + dqs[:ATT_BLOCK]).astype(dq_ref.dtype)
                from_prev = dqs[ATT_BLOCK:]
                dk_ref[own, sl] = _dot_tn(ds, qs).astype(dk_ref.dtype)
                dv_ref[own, sl] = _dot_tn(p.astype(BF16), dos).astype(dv_ref.dtype)
            carry[:, sl] = from_prev

    return pl.pallas_call(
        body, name=f"att_bwd_{g}", grid=(t // rows,), in_specs=[cur, nxt, cur, cur, cur, nxt, cur, nxt, cur, nxt],
        out_specs=[cur, cur, cur], out_shape=[SDS((t, w), BF16)] * 3,
        scratch_shapes=[pltpu.VMEM((ATT_BLOCK, w), F32)],
        compiler_params=_cparams(("arbitrary",)))(q, q, k, v, do, do, lse, lse, dlt, dlt)


def _merge_weights(ls):
    m = jnp.maximum(jnp.maximum(ls[0], ls[1]), ls[2])
    es = [jnp.exp(v - m) for v in ls]
    den = es[0] + es[1] + es[2]
    return [e / den for e in es]


def _merge_fwd(o, lse, nb, seq):
    t = o[0].shape[0]
    tm = ATT_TILE
    mtiles = seq // tm
    ng = len(ATT_DILATIONS)

    def body(*refs):
        o_refs, l_refs, out_ref, scs = refs[:ng], refs[ng:2 * ng], refs[2 * ng], refs[2 * ng + 1:]
        for slot in range(ATT_SLOTS):
            lanes = slice(slot * ATT_D, (slot + 1) * ATT_D)
            ov = [_from_strided(o_refs[g], lanes, r, scs[2 * g]) for g, r in enumerate(ATT_DILATIONS)]
            ws = _merge_weights([_from_strided(l_refs[g], lanes, r, scs[2 * g + 1])
                                 for g, r in enumerate(ATT_DILATIONS)])
            out_ref[:, lanes] = (ws[0] * ov[0] + ws[1] * ov[1] + ws[2] * ov[2]).astype(out_ref.dtype)

    views = [a.reshape(nb, r, mtiles, tm // r, ATT_W) for grp in (o, lse) for a, r in zip(grp, ATT_DILATIONS)]
    return pl.pallas_call(
        body, name="att_merge_fwd", grid=(t // tm,),
        in_specs=[_strided_spec(r, mtiles) for _ in range(2) for r in ATT_DILATIONS],
        out_specs=_rs(tm, ATT_W), out_shape=SDS((t, ATT_W), BF16),
        scratch_shapes=[pltpu.VMEM((tm, ATT_D), F32)] * (2 * ng), compiler_params=_cparams(("arbitrary",)))(*views)


def _merge_bwd(o, lse, datt, nb, seq):
    t = o[0].shape[0]
    tm = ATT_TILE
    mtiles = seq // tm
    ng = len(ATT_DILATIONS)

    def body(*refs):
        o_refs, l_refs, d_ref = refs[:ng], refs[ng:2 * ng], refs[2 * ng]
        do_refs, dlt_refs = refs[2 * ng + 1:3 * ng + 1], refs[3 * ng + 1:4 * ng + 1]
        scs = refs[4 * ng + 1:]
        for slot in range(ATT_SLOTS):
            lanes = slice(slot * ATT_D, (slot + 1) * ATT_D)
            ov = [_from_strided(o_refs[g], lanes, r, scs[2 * g]) for g, r in enumerate(ATT_DILATIONS)]
            ws = _merge_weights([_from_strided(l_refs[g], lanes, r, scs[2 * g + 1])
                                 for g, r in enumerate(ATT_DILATIONS)])
            dv = d_ref[:, lanes]
            att = ws[0] * ov[0] + ws[1] * ov[1] + ws[2] * ov[2]
            dot = jnp.broadcast_to(jnp.sum(dv * att, axis=-1, keepdims=True), (tm, ATT_D))
            for g, r in enumerate(ATT_DILATIONS):
                _to_strided(ws[g] * dv, do_refs[g], lanes, r, scs[2 * ng])
                _to_strided(ws[g] * dot, dlt_refs[g], lanes, r, scs[2 * ng + 1])

    views = [a.reshape(nb, r, mtiles, tm // r, ATT_W) for grp in (o, lse) for a, r in zip(grp, ATT_DILATIONS)]
    outs = pl.pallas_call(
        body, name="att_merge_bwd", grid=(t // tm,),
        in_specs=[_strided_spec(r, mtiles) for _ in range(2) for r in ATT_DILATIONS] + [_rs(tm, ATT_W)],
        out_specs=[_strided_spec(r, mtiles) for _ in range(2) for r in ATT_DILATIONS],
        out_shape=[_strided_shape(nb, r, mtiles, dt) for dt in (BF16, F32) for r in ATT_DILATIONS],
        scratch_shapes=[pltpu.VMEM((tm, ATT_D), F32)] * (2 * ng + 2), compiler_params=_cparams(("arbitrary",)))(
            *views, datt)
    flat = [a.reshape(t, ATT_W) for a in outs]
    return flat[:ng], flat[ng:]


def _branch_gates(rows, fulls):
    return (_sigmoid(rows[0][...].astype(F32) + fulls[0][...]), _sigmoid(rows[1][...].astype(F32) + fulls[1][...]))


def _mix_fwd_epilogue(y_att, rows, fulls):
    g0, g1 = _branch_gates(rows, fulls)
    return [y_att, g0 * rows[2][...].astype(F32) + g1 * y_att]


def _mix_bwd_epilogue(dm, rows, fulls):
    g0, g1 = _branch_gates(rows, fulls)
    dg = jnp.concatenate([dm * rows[2][...].astype(F32) * g0 * (1.0 - g0),
                          dm * rows[3][...].astype(F32) * g1 * (1.0 - g1)], axis=-1)
    return [dm * g0, dm * g1, dg, _colsum(dg)]


FFN_TM = 512


def _ffn_in(h2, wg_t, wu_t):
    t, d = h2.shape
    f = wg_t.shape[0]
    tm, tn = FFN_TM, _pick(f, 1536)

    def body(a_ref, g_ref, u_ref, gt_ref, up_ref, act_ref):
        a = a_ref[...]
        gt = _dot_nt(a, g_ref[...])
        up = _dot_nt(a, u_ref[...])
        gt_ref[...] = gt.astype(BF16)
        up_ref[...] = up.astype(BF16)
        act_ref[...] = (gt * _sigmoid(gt) * up).astype(BF16)

    a_spec = pl.BlockSpec((tm, d), lambda j, i: (i, 0))
    w_spec = pl.BlockSpec((tn, d), lambda j, i: (j, 0))
    o_spec = pl.BlockSpec((tm, tn), lambda j, i: (i, j))
    return pl.pallas_call(
        body, name="ffn_in", grid=(f // tn, t // tm), in_specs=[a_spec, w_spec, w_spec],
        out_specs=[o_spec] * 3, out_shape=[SDS((t, f), BF16)] * 3,
        compiler_params=_cparams(("parallel", "arbitrary")))(h2, wg_t, wu_t)


def _ffn_bwd_in(dx2, w_down, gt, up):
    t, d = dx2.shape
    f = w_down.shape[0]
    tm, tn = FFN_TM, _pick(f, 1536)

    def body(a_ref, w_ref, g_ref, u_ref, dgt_ref, dup_ref):
        dv = _dot_nt(a_ref[...], w_ref[...])
        gv = g_ref[...].astype(F32)
        sg = _sigmoid(gv)
        dgt_ref[...] = (dv * u_ref[...].astype(F32) * sg * (1.0 + gv * (1.0 - sg))).astype(BF16)
        dup_ref[...] = (dv * gv * sg).astype(BF16)

    a_spec = pl.BlockSpec((tm, d), lambda j, i: (i, 0))
    w_spec = pl.BlockSpec((tn, d), lambda j, i: (j, 0))
    o_spec = pl.BlockSpec((tm, tn), lambda j, i: (i, j))
    return pl.pallas_call(
        body, name="ffn_bwd_in", grid=(f // tn, t // tm), in_specs=[a_spec, w_spec, o_spec, o_spec],
        out_specs=[o_spec] * 2, out_shape=[SDS((t, f), BF16)] * 2,
        compiler_params=_cparams(("parallel", "arbitrary")))(dx2, w_down, gt, up)


def _adamw(w, g, m, v, name):
    r, c = w.shape[-2:]
    lead = w.ndim - 2
    tr = _row_tile(r, max(8, 400_000 // c))
    c1 = 1.0 / (1.0 - ADAM_B1 ** ADAM_STEP)
    c2 = 1.0 / (1.0 - ADAM_B2 ** ADAM_STEP)

    def fn(i, w_ref, g_ref, m_ref, v_ref):
        gv = g_ref[...]
        mn = ADAM_B1 * m_ref[...] + (1.0 - ADAM_B1) * gv
        vn = ADAM_B2 * v_ref[...] + (1.0 - ADAM_B2) * (gv * gv)
        delta = -ADAM_LR * ((mn * c1) / (jnp.sqrt(vn * c2) + ADAM_EPS) + ADAM_WD * w_ref[...])
        return [delta, mn, vn]

    spec = pl.BlockSpec((None,) * lead + (tr, c), lambda i: (0,) * lead + (i, 0))
    return _rw(name, fn, r // tr, [(w, spec), (g, spec), (m, spec), (v, spec)], [(SDS(w.shape, F32), spec)] * 3)


ANY = pl.BlockSpec(memory_space=pl.ANY)


def _place():
    x, y, c = lax.axis_index("x"), lax.axis_index("y"), lax.axis_index("c")
    chips = [(1 - x, y), (x, 1 - y), (1 - x, 1 - y)]
    return x, y, c, chips


def _remote(src, dst, ssem, rsem, to):
    return pltpu.make_async_remote_copy(src_ref=src, dst_ref=dst, send_sem=ssem, recv_sem=rsem, device_id=to,
                                        device_id_type=MESH)


def _copy_through_vmem(src, dst, buf, isem, osem):
    chunk = buf.shape[1]
    n = src.shape[0] // chunk
    load = lambda k: pltpu.make_async_copy(src.at[pl.ds(k * chunk, chunk)], buf.at[k % 2], isem.at[k % 2])
    store = lambda k: pltpu.make_async_copy(buf.at[k % 2], dst.at[pl.ds(k * chunk, chunk)], osem.at[k % 2])
    load(0).start()
    for k in range(n):
        load(k).wait()
        if k + 1 < n:
            if k >= 1:
                store(k - 1).wait()
            load(k + 1).start()
        store(k).start()
    if n >= 2:
        store(n - 2).wait()
    store(n - 1).wait()


def _copy_scratch(rows, width, dtype):
    chunk = _row_tile(rows, 512)
    return [pltpu.VMEM((2, chunk, width), dtype), pltpu.SemaphoreType.DMA((2,)), pltpu.SemaphoreType.DMA((2,))]


def _gather_weights(wp):
    def body(w_ref, out_ref, ssem, rsem, buf, isem, osem):
        x, y, c, chips = _place()
        me = 2 * x + y
        sib = (x, y, 1 - c)
        first = [_remote(w_ref.at[c], out_ref.at[me, c], ssem.at[j], rsem.at[j], (*chip, c))
                 for j, chip in enumerate(chips)]
        for cp in first:
            cp.start()
        for half in range(2):
            _copy_through_vmem(w_ref.at[half], out_ref.at[me, half], buf, isem, osem)
        passed = []
        for j, chip in enumerate(chips):
            ci = 2 * chip[0] + chip[1]
            _remote(w_ref.at[c], out_ref.at[ci, c], ssem.at[j], rsem.at[j], (*chip, c)).wait_recv()
            cp = _remote(out_ref.at[ci, c], out_ref.at[ci, c], ssem.at[3 + j], rsem.at[3 + j], sib)
            cp.start()
            passed.append(cp)
        for j, chip in enumerate(chips):
            ci = 2 * chip[0] + chip[1]
            _remote(out_ref.at[ci, 1 - c], out_ref.at[ci, 1 - c], ssem.at[3 + j], rsem.at[3 + j], sib).wait_recv()
        for cp in first + passed:
            cp.wait_send()

    return pl.pallas_call(
        body, name="gather_weights", in_specs=[ANY], out_specs=ANY,
        out_shape=SDS((N_CHIPS,) + wp.shape, wp.dtype),
        scratch_shapes=[pltpu.SemaphoreType.DMA((6,)), pltpu.SemaphoreType.DMA((6,))]
        + _copy_scratch(wp.shape[1], wp.shape[2], wp.dtype),
        compiler_params=pltpu.CompilerParams(has_side_effects=True))(wp)


def _swap_halves(g2, tag):
    nch = g2.shape[0]

    def body(g_ref, out_ref, ssem, rsem):
        x, y, c, _ = _place()
        cps = [_remote(g_ref.at[k, 1 - c], out_ref.at[k], ssem.at[k], rsem.at[k], (x, y, 1 - c)) for k in range(nch)]
        for cp in cps:
            cp.start()
        for cp in cps:
            cp.wait()

    return pl.pallas_call(
        body, name="swap_halves_" + tag, in_specs=[ANY], out_specs=ANY,
        out_shape=SDS((nch,) + g2.shape[2:], g2.dtype),
        scratch_shapes=[pltpu.SemaphoreType.DMA((nch,)), pltpu.SemaphoreType.DMA((nch,))],
        compiler_params=pltpu.CompilerParams(has_side_effects=True))(g2)


def _swap_side(g2):
    nch = g2.shape[0]

    def copies(ins, outs, scr):
        x, y, c, _ = _place()
        return [_remote(ins[0].at[k, 1 - c], outs[0].at[k], scr[0].at[k], scr[1].at[k], (x, y, 1 - c))
                for k in range(nch)]

    def first(ins, outs, scr):
        for cp in copies(ins, outs, scr):
            cp.start()

    def last(ins, outs, scr):
        for cp in copies(ins, outs, scr):
            cp.wait()

    return _Side((g2,), (SDS((nch,) + g2.shape[2:], g2.dtype),),
                 (pltpu.SemaphoreType.DMA((nch,)), pltpu.SemaphoreType.DMA((nch,))), first, None, last)


def _add_own_half(g2, other, c, tag):
    nch, _, rows, w = g2.shape
    tr = _row_tile(rows, 512)
    nr = rows // tr

    def body(c_ref, a_ref, b_ref, o_ref):
        o_ref[...] = (a_ref[...].astype(F32) + b_ref[...].astype(F32)).astype(o_ref.dtype)

    grid_spec = pltpu.PrefetchScalarGridSpec(
        num_scalar_prefetch=1, grid=(nch, nr),
        in_specs=[pl.BlockSpec((None, None, tr, w), lambda k, i, c_ref: (k, c_ref[0], i, 0)),
                  pl.BlockSpec((None, tr, w), lambda k, i, c_ref: (k, i, 0))],
        out_specs=pl.BlockSpec((None, tr, w), lambda k, i, c_ref: (k, i, 0)))
    return pl.pallas_call(
        body, name="add_own_half_" + tag, grid_spec=grid_spec, out_shape=SDS(other.shape, other.dtype),
        compiler_params=_cparams(("arbitrary", "arbitrary")))(jnp.reshape(c, (1,)).astype(jnp.int32), g2, other)


def _sum_chips(q, tag):
    nch, rows, w = q.shape
    tr = _row_tile(rows, 512)

    def fn(i, q_ref):
        return [((q_ref[0].astype(F32) + q_ref[1].astype(F32)) + q_ref[2].astype(F32)) + q_ref[3].astype(F32)]

    return _rw("sum_chips_" + tag, fn, rows // tr, [(q, pl.BlockSpec((nch, tr, w), lambda i: (0, i, 0)))],
               [(SDS((rows, w), F32), _rs(tr, w))])[0]


def _chip_copies(src_ref, dst_ref, ssem, rsem, outgoing):
    x, y, c, chips = _place()
    me = 2 * x + y
    cps = []
    for j, chip in enumerate(chips):
        ci = 2 * chip[0] + chip[1]
        cps.append(_remote(src_ref.at[ci], dst_ref.at[me if outgoing else ci], ssem.at[j], rsem.at[j], (*chip, c)))
    return cps, me


def _scatter_side(p):
    def first(ins, outs, scr):
        cps, me = _chip_copies(ins[0], outs[0], scr[0], scr[1], True)
        for cp in cps:
            cp.start()
        pltpu.make_async_copy(ins[0].at[me], outs[0].at[me], scr[2]).start()

    def last(ins, outs, scr):
        for cp in _chip_copies(ins[0], outs[0], scr[0], scr[1], False)[0]:
            cp.wait_recv()
        cps, me = _chip_copies(ins[0], outs[0], scr[0], scr[1], True)
        for cp in cps:
            cp.wait_send()
        pltpu.make_async_copy(ins[0].at[me], outs[0].at[me], scr[2]).wait()

    return _Side((p,), (SDS(p.shape, p.dtype),),
                 (pltpu.SemaphoreType.DMA((3,)), pltpu.SemaphoreType.DMA((3,)), pltpu.SemaphoreType.DMA(())),
                 first, None, last)


def _gather_copies(w_ref, out_ref, ssem, rsem):
    x, y, c, chips = _place()
    me = 2 * x + y
    sib = (x, y, 1 - c)
    sends, arrivals, forwards, from_sib = [], [], [], []
    for j, chip in enumerate(chips):
        ci = 2 * chip[0] + chip[1]
        sends.append(_remote(w_ref.at[c], out_ref.at[me, c], ssem.at[j], rsem.at[j], (*chip, c)))
        arrivals.append(_remote(w_ref.at[c], out_ref.at[ci, c], ssem.at[j], rsem.at[j], (*chip, c)))
        forwards.append(_remote(out_ref.at[ci, c], out_ref.at[ci, c], ssem.at[3 + j], rsem.at[3 + j], sib))
        from_sib.append(_remote(out_ref.at[ci, 1 - c], out_ref.at[ci, 1 - c], ssem.at[3 + j], rsem.at[3 + j], sib))
    return sends, arrivals, forwards, from_sib, me


def _gather_side(wp):
    def first(ins, outs, scr):
        sends, _, _, _, me = _gather_copies(ins[0], outs[0], scr[0], scr[1])
        for cp in sends:
            cp.start()
        pltpu.make_async_copy(ins[0], outs[0].at[me], scr[2]).start()

    def mid(ins, outs, scr):
        _, arrivals, forwards, _, _ = _gather_copies(ins[0], outs[0], scr[0], scr[1])
        for arrived, forward in zip(arrivals, forwards):
            arrived.wait_recv()
            forward.start()

    def last(ins, outs, scr):
        sends, _, forwards, from_sib, me = _gather_copies(ins[0], outs[0], scr[0], scr[1])
        for cp in from_sib:
            cp.wait_recv()
        for cp in sends + forwards:
            cp.wait_send()
        pltpu.make_async_copy(ins[0], outs[0].at[me], scr[2]).wait()

    return _Side((wp,), (SDS((N_CHIPS,) + wp.shape, wp.dtype),),
                 (pltpu.SemaphoreType.DMA((6,)), pltpu.SemaphoreType.DMA((6,)), pltpu.SemaphoreType.DMA(())),
                 first, mid, last)


def _allreduce_small(v, name):
    rows, w = v.shape
    offsets = [(dx, dy, dc) for dx in (0, 1) for dy in (0, 1) for dc in (0, 1)][1:]

    def body(v_ref, o_ref, buf, ssem, rsem):
        x, y, c, _ = _place()
        flip = lambda p, d: 1 - p if d else p
        peers = [(flip(x, dx), flip(y, dy), flip(c, dc)) for dx, dy, dc in offsets]
        index = lambda p: 4 * p[0] + 2 * p[1] + p[2]
        me = index((x, y, c))
        buf[me] = v_ref[...]
        sent = [_remote(v_ref, buf.at[me], ssem.at[q], rsem.at[q], p) for q, p in enumerate(peers)]
        for cp in sent:
            cp.start()
        for q, p in enumerate(peers):
            _remote(v_ref, buf.at[index(p)], ssem.at[q], rsem.at[q], p).wait_recv()
        for cp in sent:
            cp.wait_send()
        acc = buf[0]
        for q in range(1, 8):
            acc = acc + buf[q]
        o_ref[...] = acc

    vm = pl.BlockSpec(memory_space=pltpu.VMEM)
    return pl.pallas_call(
        body, name=name, in_specs=[vm], out_specs=vm, out_shape=SDS((rows, w), F32),
        scratch_shapes=[pltpu.VMEM((8, rows, w), F32), pltpu.SemaphoreType.DMA((7,)), pltpu.SemaphoreType.DMA((7,))],
        compiler_params=pltpu.CompilerParams(has_side_effects=True))(v)


def _join_halves(h, tag):
    def body(h_ref, out_ref, ssem, rsem, buf, isem, osem):
        x, y, c, _ = _place()
        cp = _remote(h_ref, out_ref.at[c], ssem, rsem, (x, y, 1 - c))
        cp.start()
        _copy_through_vmem(h_ref, out_ref.at[c], buf, isem, osem)
        _remote(h_ref, out_ref.at[1 - c], ssem, rsem, (x, y, 1 - c)).wait_recv()
        cp.wait_send()

    return pl.pallas_call(
        body, name="join_halves_" + tag, in_specs=[ANY], out_specs=ANY, out_shape=SDS((2,) + h.shape, h.dtype),
        scratch_shapes=[pltpu.SemaphoreType.DMA(()), pltpu.SemaphoreType.DMA(())]
        + _copy_scratch(h.shape[0], h.shape[1], h.dtype),
        compiler_params=pltpu.CompilerParams(has_side_effects=True))(h)


PACK_W = 1024
SHARDED = ("w_in", "w_ffn_gate", "w_ffn_up", "w_ssm_out", "w_att_out", "w_mix_out", "w_ffn_down")
COL_SHARDED = ("w_in", "w_ffn_gate", "w_ffn_up", "w_att_out")
SMALL = ("norm_mix", "b_gate", "conv_b", "dt_bias", "a_log", "d_skip", "ssm_norm", "norm_ffn", "norm_final")


PACK_ROW_ALIGN = 16


def _rows(n):
    return -(-n // (PACK_W * PACK_ROW_ALIGN)) * PACK_ROW_ALIGN


def _pack_rows(parts, total_rows):
    rows = []
    for p in parts:
        size = int(p.size)
        if size % PACK_W:
            p = jnp.pad(p.reshape(-1), (0, PACK_W - size % PACK_W))
        p = p.reshape(-1, PACK_W)
        rows.append(jnp.pad(p, ((0, _rows(size) - p.shape[0]), (0, 0))))
    used = sum(r.shape[0] for r in rows)
    if total_rows > used:
        rows.append(jnp.zeros((total_rows - used, PACK_W), rows[0].dtype))
    return jnp.concatenate(rows, axis=0)


def _padded_rows(n):
    return -(-n // 32) * 32


def _wire_name(name):
    return name + "_t" if name in COL_SHARDED else name


def _wire_shard(w, name):
    return w.T if name in COL_SHARDED else w


def _group_major(a, axis):
    gw = D_INNER // N_GROUPS
    take = lambda lo, n: lax.slice_in_dim(a, lo, lo + n, axis=axis)
    parts = []
    for g in range(N_GROUPS):
        parts += [take(g * gw, gw), take(D_INNER + g * D_STATE, D_STATE),
                  take(D_INNER + N_GROUPS * D_STATE + g * D_STATE, D_STATE)]
    return jnp.concatenate(parts, axis=axis)


def _group_major_inv(a, axis):
    gw = D_INNER // N_GROUPS
    take = lambda lo, n: lax.slice_in_dim(a, lo, lo + n, axis=axis)
    xs = [take(g * GROUP_W, gw) for g in range(N_GROUPS)]
    bs = [take(g * GROUP_W + gw, D_STATE) for g in range(N_GROUPS)]
    cs = [take(g * GROUP_W + gw + D_STATE, D_STATE) for g in range(N_GROUPS)]
    return jnp.concatenate(xs + bs + cs, axis=axis)


LATE = ("w_ffn_gate_t", "w_ffn_up_t", "w_ssm_out", "w_att_out_t", "w_mix_out", "w_ffn_down")


class _Overlap(NamedTuple):
    gather_side: _Side
    late_weights: Callable
    scatter_side: Callable
    swap_in: Callable
    scatter_in: Callable


def _local_step(x, target, wts, overlap):
    nb, seq, d = x.shape
    t = nb * seq
    x = x.reshape(t, d)
    target = target.reshape(t, d)
    hg = HEADS_PER_GROUP

    o1, o2, o3, o4 = D_INNER, D_INNER + CONV_DIM, D_INNER + CONV_DIM + N_HEADS, D_INNER + CONV_DIM + N_HEADS + QKV_DIM
    n_in = o4 + 2 * D_MODEL

    def in_rows(lo, hi):
        per = n_in // N_CHIPS
        parts = [wts["w_in_t"][k, max(lo, k * per) - k * per:min(hi, (k + 1) * per) - k * per]
                 for k in range(N_CHIPS) if max(lo, k * per) < min(hi, (k + 1) * per)]
        return parts[0] if len(parts) == 1 else jnp.concatenate(parts, axis=0)

    w_z = in_rows(0, o1)
    w_xbc = _group_major(in_rows(o1, o2), 0)
    w_dt = jnp.pad(in_rows(o2, o3), ((0, DT_PAD - N_HEADS), (0, 0)))
    w_qkv = in_rows(o3, o4)
    w_gate = in_rows(o4, n_in)
    conv_w = _group_major(wts["conv_w"], 1)
    conv_b = _group_major(wts["conv_b"], 1)

    def per_group_row(p):
        return p.reshape(N_GROUPS, 1, hg)

    def per_group_col(p):
        return p.reshape(N_GROUPS, hg, 1)

    a_neg = -jnp.exp(wts["a_log"])
    bias_r, bias_c = per_group_row(wts["dt_bias"]), per_group_col(wts["dt_bias"])
    a_r, a_c = per_group_row(a_neg), per_group_col(a_neg)
    dskip_r = per_group_row(wts["d_skip"])
    cos, sin = _rope_tables(seq)

    h = _rms_fwd(x, wts["norm_mix"], "rms_mix_fwd")
    z = _mm(h, w_z, "nt", BF16, "proj_z")
    xbc = _mm(h, w_xbc, "nt", F32, "proj_xbc")
    dt_raw = _mm(h, w_dt, "nt", F32, "proj_dt")
    qkv = _mm(h, w_qkv, "nt", BF16, "proj_qkv")
    gate_logits = _mm(h, w_gate, "nt", BF16, "proj_gate")

    xc = _conv_fwd(xbc, conv_w, conv_b, seq)
    dtr = dt_raw[:, :N_HEADS].reshape(t, N_GROUPS, hg).transpose(1, 0, 2)
    dtrt = dt_raw[:, :N_HEADS].reshape(nb, seq, N_GROUPS, hg).transpose(2, 0, 3, 1)
    y, states, *gathered = _ssd_fwd(xc, dtr, dtrt, bias_r, bias_c, a_r, a_c, dskip_r, nb, seq, overlap.gather_side)
    wts = {**wts, **overlap.late_weights(gathered)}
    yn, y_ssm = _gate_norm_out(y, z, wts["ssm_norm"], wts["w_ssm_out"])

    groups = range(len(ATT_DILATIONS))
    qg, kg, vg = _rope_fwd(qkv, cos, sin, nb, seq)
    o_g, lse_g = zip(*[_att_fwd(qg[i], kg[i], vg[i], i, seq) for i in groups])
    att = _merge_fwd(o_g, lse_g, nb, seq)
    gate_halves = [(gate_logits, d, 0), (gate_logits, d, 1)]
    b_gate_halves = [(wts["b_gate"], d, 0), (wts["b_gate"], d, 1)]
    y_att, mixed = _mm_fused(att, wts["w_att_out_t"], "nt", "att_out_mix", 512, _mix_fwd_epilogue,
                             gate_halves + [(y_ssm, d, 0)], b_gate_halves, [(d, BF16), (d, BF16)])

    def residual_and_norm(xv, rows, fulls):
        return [xv, xv * lax.rsqrt(jnp.mean(xv * xv, axis=-1, keepdims=True) + EPS) * fulls[0][...]]

    x1, h2 = _mm_fused(mixed, wts["w_mix_out"], "nn", "mix_out_norm", 512, residual_and_norm, [],
                       [(wts["norm_ffn"], d, 0)], [(d, F32), (d, BF16)], add=x)
    gt, up, act = _ffn_in(h2, wts["w_ffn_gate_t"], wts["w_ffn_up_t"])

    g = {}
    dx2, dx2_b, g["norm_final"], loss = _mm_fused(
        act, wts["w_ffn_down"], "nn", "ffn_down_loss", 512,
        lambda x2, rows, fulls: _final_values(x2, rows[0][...], fulls[0][...]),
        [(target, d, 0)], [(wts["norm_final"].reshape(1, d), d, 0)], [(d, F32), (d, BF16), (d, F32), (1, F32)],
        n_acc=2, add=x1)
    g["w_ffn_down"] = _mm(act, dx2_b, "tn", BF16, "g_ffn_down")
    dgt, dup = _ffn_bwd_in(dx2_b, wts["w_ffn_down"], gt, up)
    g["w_ffn_gate_t"] = _mm(dgt, h2, "tn", BF16, "g_ffn_gate")
    g["w_ffn_up_t"] = _mm(dup, h2, "tn", BF16, "g_ffn_up")
    dh2 = _mm(dgt, wts["w_ffn_gate_t"], "nn", F32, "d_h2_gate")
    dx1, dx1_b, g["norm_ffn"] = _mm_fused(
        dup, wts["w_ffn_up_t"], "nn", "d_h2_up_norm", 512,
        lambda dh, rows, fulls: _rms_bwd_values(rows[0][...], dh, fulls[0][...], rows[1][...]),
        [(x1, d, 0), (dx2, d, 0)], [(wts["norm_ffn"], d, 0)], [(d, F32), (d, BF16), (d, F32)], n_acc=1, add=dh2)

    g["w_mix_out"] = _mm(mixed, dx1_b, "tn", BF16, "g_mix_out")
    dy_ssm, dy_att, dgate, g["b_gate"] = _mm_fused(
        dx1_b, wts["w_mix_out"], "nt", "d_mixed_gates", 512, _mix_bwd_epilogue,
        gate_halves + [(y_ssm, d, 0), (y_att, d, 0)], b_gate_halves,
        [(d, BF16), (d, BF16), (2 * d, BF16), (2 * d, F32)], n_acc=1)

    datt = _mm(dy_att, wts["w_att_out_t"], "nn", F32, "d_att")
    g["w_att_out_t"] = _mm(dy_att, att, "tn", BF16, "g_att_out")
    do_g, dlt_g = _merge_bwd(o_g, lse_g, datt, nb, seq)
    dq_g, dk_g, dv_g = zip(*[_att_bwd(qg[i], kg[i], vg[i], do_g[i], lse_g[i], dlt_g[i], i, seq) for i in groups])
    dqkv = _rope_bwd(dq_g, dk_g, dv_g, cos, sin, nb, seq)

    g["w_ssm_out"] = _mm(yn, dy_ssm, "tn", BF16, "g_ssm_out")
    dy, dz, g["ssm_norm"] = _mm_fused(
        dy_ssm, wts["w_ssm_out"], "nt", "d_yn_norm", 256, _gate_norm_bwd_epilogue,
        [(y, D_INNER, 0), (z, D_INNER, 0)], [(wts["ssm_norm"], D_INNER, 0)],
        [(D_INNER, BF16), (D_INNER, BF16), (D_INNER, F32)], n_acc=1)
    side = overlap.scatter_side({n: g.pop(n) for n in LATE})
    dxc, ddtr, g_bias, g_alog, g_dskip, *scattered = _ssd_bwd(xc, dtr, dtrt, bias_r, bias_c, a_r, a_c, dskip_r,
                                                               states, dy, nb, seq, side)
    g["dt_bias"] = g_bias.reshape(1, N_HEADS)
    g["a_log"] = g_alog.reshape(1, N_HEADS)
    g["d_skip"] = g_dskip.reshape(1, N_HEADS)
    dpre, g_conv_w, g_conv_b = _conv_bwd_pre(xbc, conv_w, conv_b, dxc, seq)
    g["conv_w"] = _group_major_inv(g_conv_w, 1)
    g["conv_b"] = _group_major_inv(g_conv_b, 1)
    dxbc = _conv_bwd_in(dpre, conv_w, seq)
    ddt = jnp.pad(ddtr.transpose(1, 0, 2).reshape(t, N_HEADS), ((0, 0), (0, DT_PAD - N_HEADS))).astype(BF16)

    g_in_t = jnp.concatenate([
        _mm(dz, h, "tn", BF16, "g_in_z"),
        _group_major_inv(_mm(dxbc, h, "tn", BF16, "g_in_xbc"), 0),
        _mm(ddt, h, "tn", BF16, "g_in_dt")[:N_HEADS],
        _mm(dqkv, h, "tn", BF16, "g_in_qkv"),
        _mm(dgate, h, "tn", BF16, "g_in_gate")], axis=0)
    swap_side, packed = overlap.swap_in({"w_in_t": g_in_t})
    dh = _mm(dz, w_z, "nn", F32, "d_h_z")
    dh, *swapped = _mm(dxbc, w_xbc, "nn", F32, "d_h_xbc", add=dh, side=swap_side)
    dh = _mm(ddt, w_dt, "nn", F32, "d_h_dt", add=dh)
    dh, *scattered_in = _mm(dqkv, w_qkv, "nn", F32, "d_h_qkv", add=dh, side=overlap.scatter_in(packed, swapped))
    dx, g["norm_mix"] = _mm_fused(
        dgate, w_gate, "nn", "d_h_gate_norm", 512,
        lambda dhv, rows, fulls: _rms_bwd_values(rows[0][...], dhv, fulls[0][...], rows[1][...])[1:],
        [(x, d, 0), (dx1, d, 0)], [(wts["norm_mix"], d, 0)], [(d, F32), (d, F32)], n_acc=1, add=dh)
    return loss[0, 0], dx.reshape(nb, seq, d), g, scattered, scattered_in


def kernel(x, norm_mix, w_in, b_gate, conv_w, conv_b, dt_bias, a_log, d_skip, ssm_norm, w_ssm_out, w_att_out, w_mix_out, norm_ffn, w_ffn_gate, w_ffn_up, w_ffn_down, norm_final, loss_target, m_norm_mix, m_w_in, m_b_gate, m_conv_w, m_conv_b, m_dt_bias, m_a_log, m_d_skip, m_ssm_norm, m_w_ssm_out, m_w_att_out, m_w_mix_out, m_norm_ffn, m_w_ffn_gate, m_w_ffn_up, m_w_ffn_down, m_norm_final, v_norm_mix, v_w_in, v_b_gate, v_conv_w, v_conv_b, v_dt_bias, v_a_log, v_d_skip, v_ssm_norm, v_w_ssm_out, v_w_att_out, v_w_mix_out, v_norm_ffn, v_w_ffn_gate, v_w_ffn_up, v_w_ffn_down, v_norm_final):
    names = ("norm_mix", "w_in", "b_gate", "conv_w", "conv_b", "dt_bias", "a_log", "d_skip", "ssm_norm", "w_ssm_out",
             "w_att_out", "w_mix_out", "norm_ffn", "w_ffn_gate", "w_ffn_up", "w_ffn_down", "norm_final")
    w_loc = dict(zip(names, (norm_mix, w_in, b_gate, conv_w, conv_b, dt_bias, a_log, d_skip, ssm_norm, w_ssm_out,
                             w_att_out, w_mix_out, norm_ffn, w_ffn_gate, w_ffn_up, w_ffn_down, norm_final)))
    m_loc = dict(zip(names, (m_norm_mix, m_w_in, m_b_gate, m_conv_w, m_conv_b, m_dt_bias, m_a_log, m_d_skip,
                             m_ssm_norm, m_w_ssm_out, m_w_att_out, m_w_mix_out, m_norm_ffn, m_w_ffn_gate,
                             m_w_ffn_up, m_w_ffn_down, m_norm_final)))
    v_loc = dict(zip(names, (v_norm_mix, v_w_in, v_b_gate, v_conv_w, v_conv_b, v_dt_bias, v_a_log, v_d_skip,
                             v_ssm_norm, v_w_ssm_out, v_w_att_out, v_w_mix_out, v_norm_ffn, v_w_ffn_gate,
                             v_w_ffn_up, v_w_ffn_down, v_norm_final)))
    two_d = lambda a: a.reshape(a.shape[-2:]) if a.ndim >= 2 else a.reshape(1, -1)
    w2 = {n: two_d(a) for n, a in w_loc.items()}
    chip = 2 * lax.axis_index("x") + lax.axis_index("y")
    c = lax.axis_index("c")

    wire_shapes = {n: _wire_shard(w2[n], n).shape for n in SHARDED}
    true_rows = {n: wire_shapes[n][0] * wire_shapes[n][1] // PACK_W for n in SHARDED}
    seg_rows = {n: _rows(wire_shapes[n][0] * wire_shapes[n][1]) for n in SHARDED}
    buckets = {"first": ("w_in",), "late": tuple(n for n in SHARDED if n != "w_in")}
    rows_of = {b: _padded_rows(sum(seg_rows[n] for n in ns)) for b, ns in buckets.items()}

    def pack_shards(b):
        packed = _pack_rows([_wire_shard(w2[n], n).astype(BF16) for n in buckets[b]], rows_of[b])
        return packed.reshape(2, rows_of[b] // 2, PACK_W)

    def unpack_full(gathered, b):
        wg, out, off = gathered.reshape(N_CHIPS, rows_of[b], PACK_W), {}, 0
        for n in buckets[b]:
            rows, cols = wire_shapes[n]
            out[_wire_name(n)] = wg[:, off:off + true_rows[n]].reshape(N_CHIPS * rows, cols)
            off += seg_rows[n]
        return out

    def pack_grads(g, b):
        sections = [_pack_rows([g[_wire_name(n)].reshape(N_CHIPS, true_rows[n], PACK_W)[k] for n in buckets[b]],
                               rows_of[b]) for k in range(N_CHIPS)]
        return jnp.stack(sections).reshape(N_CHIPS, 2, rows_of[b] // 2, PACK_W)

    def chip_sums(g, b):
        g2 = pack_grads(g, b)
        return _add_own_half(g2, _swap_halves(g2, b), c, b)

    def finish(by_source, b):
        reduced = _join_halves(_sum_chips(by_source, b), b).reshape(rows_of[b], PACK_W)
        out, off = {}, 0
        for n in buckets[b]:
            out[n] = reduced[off:off + true_rows[n]].reshape(wire_shapes[n])
            off += seg_rows[n]
        return out

    full = {"w_in_t": _gather_weights(pack_shards("first")).reshape(N_CHIPS, rows_of["first"], PACK_W)}
    for n in SMALL:
        full[n] = w2[n]
    def swap_in(g):
        g2 = pack_grads(g, "first")
        return _swap_side(g2), g2

    overlap = _Overlap(_gather_side(pack_shards("late")), lambda outs: unpack_full(outs[0], "late"),
                       lambda g: _scatter_side(chip_sums(g, "late")), swap_in,
                       lambda g2, swapped: _scatter_side(_add_own_half(g2, swapped[0], c, "first")))

    n_conv = w2["conv_w"].shape[1]
    placed = lax.dynamic_update_slice_in_dim(jnp.zeros((CONV_K, N_CHIPS * n_conv), F32), w2["conv_w"], chip * n_conv, 1)
    placed = jnp.where(c == 0, placed, 0.0)
    full["conv_w"] = _allreduce_small(_pack_rows([placed], _rows(int(placed.size))), "gather_conv_w").reshape(
        -1)[:placed.size].reshape(placed.shape)

    loss_sum, grad_x, g_full, scattered, scattered_in = _local_step(x, loss_target, full, overlap)
    loss = lax.psum(loss_sum, ("x", "y", "c"))

    g_shard = {}
    small_names = SMALL + ("conv_w",)
    small_flat = jnp.concatenate([g_full[n].reshape(-1) for n in small_names])
    small = _allreduce_small(_pack_rows([small_flat], _rows(int(small_flat.size))), "allreduce_small").reshape(-1)
    off = 0
    for n in small_names:
        size = int(g_full[n].size)
        g_shard[n] = small[off:off + size].reshape(g_full[n].shape)
        off += size
    g_shard["conv_w"] = lax.dynamic_slice_in_dim(g_shard["conv_w"], chip * n_conv, n_conv, 1)

    g_shard.update(finish(scattered[0], "late"))
    g_shard.update(finish(scattered_in[0], "first"))

    grads, deltas, new_m, new_v = [], [], [], []
    for n in names:
        shape = w_loc[n].shape
        if n in COL_SHARDED:
            view = unview = lambda a: jnp.swapaxes(a, -1, -2)
        else:
            view, unview = ((lambda a: a) if len(shape) >= 2 else two_d), (lambda a: a.reshape(shape))
        gn = g_shard[n].reshape(view(w_loc[n]).shape)
        outs = _adamw(view(w_loc[n]), gn, view(m_loc[n]), view(v_loc[n]), "adamw_" + n)
        for acc, a in zip((grads, deltas, new_m, new_v), (gn, *outs)):
            acc.append(unview(a))
    return (loss, grad_x, *grads, *deltas, *new_m, *new_v)
```
